```python
import jax, jax.numpy as jnp
from jax import lax
import numpy as np

D_MODEL = 1024
BATCH = 16
SEQ = 2048
DEPTH = 1

CHUNK = 64
FOX_HEADS = 16
HEAD_DIM = 64
FOX_WIDTH = FOX_HEADS * HEAD_DIM
CONV_GROUPS = 16
CONV_WIDTH = D_MODEL
D_MIX = FOX_WIDTH + CONV_WIDTH
CONV_KERNEL = 31
Q_BLOCK = 128
EPS = 1e-6
NEG_INF = -1e30
IN_COLS = 3 * FOX_WIDTH + FOX_HEADS + FOX_WIDTH + 2 * CONV_WIDTH + CONV_WIDTH

kernel_name = "hybrid_fox_conformer_block"


def rmsnorm(x, g):
    xf = x.astype(jnp.float32)
    y = xf * lax.rsqrt(jnp.mean(xf * xf, axis=-1, keepdims=True) + EPS)
    return (y * g.astype(jnp.float32)).astype(x.dtype)


def layernorm(x, g, b):
    xf = x.astype(jnp.float32)
    mu = jnp.mean(xf, axis=-1, keepdims=True)
    var = jnp.mean(jnp.square(xf - mu), axis=-1, keepdims=True)
    y = (xf - mu) * lax.rsqrt(var + EPS)
    return (y * g.astype(jnp.float32) + b.astype(jnp.float32)).astype(x.dtype)


def fox_attention(q, k, v, log_f):
    S = q.shape[1]
    c = jnp.cumsum(log_f, axis=1)
    c = jnp.transpose(c, (0, 2, 1))
    scale = HEAD_DIM ** -0.5
    outs = []
    for i in range(S // Q_BLOCK):
        q0, q1 = i * Q_BLOCK, (i + 1) * Q_BLOCK
        qb = q[:, q0:q1].astype(jnp.float32)
        kb = k[:, :q1].astype(jnp.float32)
        vb = v[:, :q1]
        logits = jnp.einsum('bqhd,bkhd->bhqk', qb, kb) * scale
        decay = c[:, :, q0:q1, None] - c[:, :, None, :q1]
        qpos = jnp.arange(q0, q1)[:, None]
        kpos = jnp.arange(q1)[None, :]
        logits = jnp.where(kpos <= qpos, logits + decay, NEG_INF)
        probs = jax.nn.softmax(logits, axis=-1).astype(v.dtype)
        outs.append(jnp.einsum('bhqk,bkhd->bqhd', probs, vb))
    return jnp.concatenate(outs, axis=1)


def causal_depthwise_conv(u, w, b):
    C = u.shape[-1]
    y = lax.conv_general_dilated(
        u, w.reshape(CONV_KERNEL, 1, C).astype(u.dtype),
        window_strides=(1,), padding=[(CONV_KERNEL - 1, 0)],
        dimension_numbers=('NWC', 'WIO', 'NWC'), feature_group_count=C)
    return y + b.astype(u.dtype)


def _fwd_setup_inputs(seed: int = 0) -> dict:
    key = jax.random.key(seed)
    ks = jax.random.split(key, 12)
    f32 = jnp.float32
    x = jax.random.normal(ks[0], (BATCH, SEQ, D_MODEL), f32)
    norm_g = 1.0 + 0.02 * jax.random.normal(ks[1], (DEPTH, D_MODEL), f32)
    w_in = jax.random.normal(ks[2], (DEPTH, D_MODEL, IN_COLS), f32) * D_MODEL ** -0.5
    b_forget = (jnp.linspace(1.0, 5.0, FOX_HEADS, dtype=f32)[None, :]
                + 0.1 * jax.random.normal(ks[3], (DEPTH, FOX_HEADS), f32))
    q_norm_g = 1.0 + 0.02 * jax.random.normal(ks[4], (DEPTH, FOX_HEADS, HEAD_DIM), f32)
    k_norm_g = 1.0 + 0.02 * jax.random.normal(ks[5], (DEPTH, FOX_HEADS, HEAD_DIM), f32)
    conv_w = jax.random.normal(ks[6], (DEPTH, CONV_KERNEL, CONV_WIDTH), f32) * CONV_KERNEL ** -0.5
    conv_b = 0.02 * jax.random.normal(ks[7], (DEPTH, CONV_WIDTH), f32)
    conv_ln_g = 1.0 + 0.02 * jax.random.normal(ks[8], (DEPTH, CONV_WIDTH), f32)
    conv_ln_b = 0.02 * jax.random.normal(ks[9], (DEPTH, CONV_WIDTH), f32)
    w_out = jax.random.normal(ks[10], (DEPTH, D_MIX, D_MODEL), f32) * D_MIX ** -0.5
    return {"x": x, "norm_g": norm_g, "w_in": w_in, "b_forget": b_forget,
            "q_norm_g": q_norm_g, "k_norm_g": k_norm_g, "conv_w": conv_w,
            "conv_b": conv_b, "conv_ln_g": conv_ln_g, "conv_ln_b": conv_ln_b,
            "w_out": w_out}


def _fwd_reference(x, norm_g, w_in, b_forget, q_norm_g, k_norm_g, conv_w, conv_b,
              conv_ln_g, conv_ln_b, w_out):
    B, S, _ = x.shape
    o_q = 0
    o_k = o_q + FOX_WIDTH
    o_v = o_k + FOX_WIDTH
    o_f = o_v + FOX_WIDTH
    o_gf = o_f + FOX_HEADS
    o_glu = o_gf + FOX_WIDTH
    o_gc = o_glu + 2 * CONV_WIDTH
    for l in range(DEPTH):
        h = rmsnorm(x, norm_g[l])
        z = jnp.einsum('bsd,de->bse', h, w_in[l])

        q = z[..., o_q:o_k].reshape(B, S, FOX_HEADS, HEAD_DIM)
        k = z[..., o_k:o_v].reshape(B, S, FOX_HEADS, HEAD_DIM)
        v = z[..., o_v:o_f].reshape(B, S, FOX_HEADS, HEAD_DIM)
        q = rmsnorm(q, q_norm_g[l])
        k = rmsnorm(k, k_norm_g[l])
        log_f = jax.nn.log_sigmoid(z[..., o_f:o_gf].astype(jnp.float32)
                                   + b_forget[l].astype(jnp.float32))
        a = fox_attention(q, k, v, log_f).reshape(B, S, FOX_WIDTH)
        a = a * jax.nn.silu(z[..., o_gf:o_glu])

        u = z[..., o_glu:o_gc]
        u = u[..., :CONV_WIDTH] * jax.nn.sigmoid(u[..., CONV_WIDTH:])
        u = causal_depthwise_conv(u, conv_w[l], conv_b[l])
        u = jax.nn.silu(layernorm(u, conv_ln_g[l], conv_ln_b[l]))
        u = u * jax.nn.silu(z[..., o_gc:])

        y = jnp.concatenate([a, u], axis=-1)
        x = x + jnp.einsum('bse,ed->bsd', y, w_out[l])
    return x


import jax as _jax
import jax.numpy as _jnp

TWIN_FORMAT = 'train_step'
FWD_PARAMS = ['x', 'norm_g', 'w_in', 'b_forget', 'q_norm_g', 'k_norm_g', 'conv_w', 'conv_b', 'conv_ln_g', 'conv_ln_b', 'w_out']
TWIN_WEIGHTS = ['norm_g', 'w_in', 'b_forget', 'q_norm_g', 'k_norm_g', 'conv_w', 'conv_b', 'conv_ln_g', 'conv_ln_b', 'w_out']
TWIN_DIFF_INPUT = 'x'
TWIN_INPUTS = ['x', 'norm_g', 'w_in', 'b_forget', 'q_norm_g', 'k_norm_g', 'conv_w', 'conv_b', 'conv_ln_g', 'conv_ln_b', 'w_out', 'loss_target', 'm_norm_g', 'm_w_in', 'm_b_forget', 'm_q_norm_g', 'm_k_norm_g', 'm_conv_w', 'm_conv_b', 'm_conv_ln_g', 'm_conv_ln_b', 'm_w_out', 'v_norm_g', 'v_w_in', 'v_b_forget', 'v_q_norm_g', 'v_k_norm_g', 'v_conv_w', 'v_conv_b', 'v_conv_ln_g', 'v_conv_ln_b', 'v_w_out']
TWIN_OUTPUTS = ['loss', 'grad_x', 'grad_norm_g', 'grad_w_in', 'grad_b_forget', 'grad_q_norm_g', 'grad_k_norm_g', 'grad_conv_w', 'grad_conv_b', 'grad_conv_ln_g', 'grad_conv_ln_b', 'grad_w_out', 'delta_norm_g', 'delta_w_in', 'delta_b_forget', 'delta_q_norm_g', 'delta_k_norm_g', 'delta_conv_w', 'delta_conv_b', 'delta_conv_ln_g', 'delta_conv_ln_b', 'delta_w_out', 'new_m_norm_g', 'new_m_w_in', 'new_m_b_forget', 'new_m_q_norm_g', 'new_m_k_norm_g', 'new_m_conv_w', 'new_m_conv_b', 'new_m_conv_ln_g', 'new_m_conv_ln_b', 'new_m_w_out', 'new_v_norm_g', 'new_v_w_in', 'new_v_b_forget', 'new_v_q_norm_g', 'new_v_k_norm_g', 'new_v_conv_w', 'new_v_conv_b', 'new_v_conv_ln_g', 'new_v_conv_ln_b', 'new_v_w_out']
TWIN_LEAF_KINDS = {'loss': 'loss', 'grad_x': 'grad_x', 'grad_norm_g': 'grad_w', 'grad_w_in': 'grad_w', 'grad_b_forget': 'grad_w', 'grad_q_norm_g': 'grad_w', 'grad_k_norm_g': 'grad_w', 'grad_conv_w': 'grad_w', 'grad_conv_b': 'grad_w', 'grad_conv_ln_g': 'grad_w', 'grad_conv_ln_b': 'grad_w', 'grad_w_out': 'grad_w', 'delta_norm_g': 'delta_w', 'delta_w_in': 'delta_w', 'delta_b_forget': 'delta_w', 'delta_q_norm_g': 'delta_w', 'delta_k_norm_g': 'delta_w', 'delta_conv_w': 'delta_w', 'delta_conv_b': 'delta_w', 'delta_conv_ln_g': 'delta_w', 'delta_conv_ln_b': 'delta_w', 'delta_w_out': 'delta_w', 'new_m_norm_g': 'new_m', 'new_m_w_in': 'new_m', 'new_m_b_forget': 'new_m', 'new_m_q_norm_g': 'new_m', 'new_m_k_norm_g': 'new_m', 'new_m_conv_w': 'new_m', 'new_m_conv_b': 'new_m', 'new_m_conv_ln_g': 'new_m', 'new_m_conv_ln_b': 'new_m', 'new_m_w_out': 'new_m', 'new_v_norm_g': 'new_v', 'new_v_w_in': 'new_v', 'new_v_b_forget': 'new_v', 'new_v_q_norm_g': 'new_v', 'new_v_k_norm_g': 'new_v', 'new_v_conv_w': 'new_v', 'new_v_conv_b': 'new_v', 'new_v_conv_ln_g': 'new_v', 'new_v_conv_ln_b': 'new_v', 'new_v_w_out': 'new_v'}


def _forward(args):
    return _fwd_reference(*[args[k] for k in FWD_PARAMS])


def _output_shape():
    out = _jax.eval_shape(lambda: _forward(_fwd_setup_inputs(0)))
    return out.shape, out.dtype

N_MICROBATCH = 1
ADAM_LR = 0.001
ADAM_B1 = 0.9
ADAM_B2 = 0.999
ADAM_EPS = 1e-08
ADAM_WD = 0.01
ADAM_STEP = 10
PER_EXAMPLE_BATCH_AXIS = {'x': 0, 'loss_target': 0}
SHARED_INPUTS = []
_WEIGHT_DTYPES = {'norm_g': _jnp.float32, 'w_in': _jnp.float32, 'b_forget': _jnp.float32, 'q_norm_g': _jnp.float32, 'k_norm_g': _jnp.float32, 'conv_w': _jnp.float32, 'conv_b': _jnp.float32, 'conv_ln_g': _jnp.float32, 'conv_ln_b': _jnp.float32, 'w_out': _jnp.float32}
MOMENT_SCALE = {'norm_g': 3.656014e+00, 'w_in': 5.896644e-02, 'b_forget': 1.335110e+01, 'q_norm_g': 2.288449e-01, 'k_norm_g': 2.291147e-01, 'conv_w': 8.756001e-02, 'conv_b': 5.079216e-01, 'conv_ln_g': 2.395992e+00, 'conv_ln_b': 1.529270e+00, 'w_out': 9.421647e-02}


def _to_microbatches(a, axis):
    t = _jnp.moveaxis(a, axis, 0)
    t = t.reshape((N_MICROBATCH, t.shape[0] // N_MICROBATCH) + t.shape[1:])
    return _jnp.moveaxis(t, 1, axis + 1)


def setup_inputs(seed: int = 0) -> dict:
    inp = _fwd_setup_inputs(seed)
    key = _jax.random.fold_in(_jax.random.key(seed), 7919)
    shape, _ = _output_shape()
    out = dict(inp)
    out["loss_target"] = _jax.random.normal(_jax.random.fold_in(key, 0), shape, _jnp.float32)
    for i, name in enumerate(TWIN_WEIGHTS):
        w = inp[name].astype(_jnp.float32)
        if MOMENT_SCALE is None:
            s = _jnp.sqrt(_jnp.mean(_jnp.square(w)) + 1e-30)
        else:
            s = MOMENT_SCALE[name]
        km, kv = _jax.random.split(_jax.random.fold_in(key, i + 1))
        out[name] = w
        out["m_" + name] = s * _jax.random.normal(km, w.shape, _jnp.float32)
        out["v_" + name] = (s * s) * _jax.random.uniform(kv, w.shape, _jnp.float32, 0.5, 1.5)
    if N_MICROBATCH > 1:
        for name, axis in PER_EXAMPLE_BATCH_AXIS.items():
            out[name] = _to_microbatches(out[name], axis)
    return {'x': out['x'], 'norm_g': out['norm_g'], 'w_in': out['w_in'], 'b_forget': out['b_forget'], 'q_norm_g': out['q_norm_g'], 'k_norm_g': out['k_norm_g'], 'conv_w': out['conv_w'], 'conv_b': out['conv_b'], 'conv_ln_g': out['conv_ln_g'], 'conv_ln_b': out['conv_ln_b'], 'w_out': out['w_out'], 'loss_target': out['loss_target'], 'm_norm_g': out['m_norm_g'], 'm_w_in': out['m_w_in'], 'm_b_forget': out['m_b_forget'], 'm_q_norm_g': out['m_q_norm_g'], 'm_k_norm_g': out['m_k_norm_g'], 'm_conv_w': out['m_conv_w'], 'm_conv_b': out['m_conv_b'], 'm_conv_ln_g': out['m_conv_ln_g'], 'm_conv_ln_b': out['m_conv_ln_b'], 'm_w_out': out['m_w_out'], 'v_norm_g': out['v_norm_g'], 'v_w_in': out['v_w_in'], 'v_b_forget': out['v_b_forget'], 'v_q_norm_g': out['v_q_norm_g'], 'v_k_norm_g': out['v_k_norm_g'], 'v_conv_w': out['v_conv_w'], 'v_conv_b': out['v_conv_b'], 'v_conv_ln_g': out['v_conv_ln_g'], 'v_conv_ln_b': out['v_conv_ln_b'], 'v_w_out': out['v_w_out']}


def _loss(weights, diff, rest, loss_target):
    with _jax.named_scope("forward"):
        args = {**rest, TWIN_DIFF_INPUT: diff, **{k: w.astype(_WEIGHT_DTYPES[k]) for k, w in weights.items()}}
        y = _forward(args)
    with _jax.named_scope("loss_head"):
        err = _jnp.square(y.astype(_jnp.float32) - loss_target)
        return 0.5 * _jnp.sum(_jnp.mean(err, axis=-1)) if err.ndim else 0.5 * err


def _adamw(w, g, m, v):
    m = ADAM_B1 * m + (1.0 - ADAM_B1) * g
    v = ADAM_B2 * v + (1.0 - ADAM_B2) * _jnp.square(g)
    m_hat = m / (1.0 - ADAM_B1 ** ADAM_STEP)
    v_hat = v / (1.0 - ADAM_B2 ** ADAM_STEP)
    delta = -ADAM_LR * (m_hat / (_jnp.sqrt(v_hat) + ADAM_EPS) + ADAM_WD * w)
    return delta, m, v


def reference(x, norm_g, w_in, b_forget, q_norm_g, k_norm_g, conv_w, conv_b, conv_ln_g, conv_ln_b, w_out, loss_target, m_norm_g, m_w_in, m_b_forget, m_q_norm_g, m_k_norm_g, m_conv_w, m_conv_b, m_conv_ln_g, m_conv_ln_b, m_w_out, v_norm_g, v_w_in, v_b_forget, v_q_norm_g, v_k_norm_g, v_conv_w, v_conv_b, v_conv_ln_g, v_conv_ln_b, v_w_out):
    given = dict(x=x, norm_g=norm_g, w_in=w_in, b_forget=b_forget, q_norm_g=q_norm_g, k_norm_g=k_norm_g, conv_w=conv_w, conv_b=conv_b, conv_ln_g=conv_ln_g, conv_ln_b=conv_ln_b, w_out=w_out, loss_target=loss_target, m_norm_g=m_norm_g, m_w_in=m_w_in, m_b_forget=m_b_forget, m_q_norm_g=m_q_norm_g, m_k_norm_g=m_k_norm_g, m_conv_w=m_conv_w, m_conv_b=m_conv_b, m_conv_ln_g=m_conv_ln_g, m_conv_ln_b=m_conv_ln_b, m_w_out=m_w_out, v_norm_g=v_norm_g, v_w_in=v_w_in, v_b_forget=v_b_forget, v_q_norm_g=v_q_norm_g, v_k_norm_g=v_k_norm_g, v_conv_w=v_conv_w, v_conv_b=v_conv_b, v_conv_ln_g=v_conv_ln_g, v_conv_ln_b=v_conv_ln_b, v_w_out=v_w_out)
    weights = {n: given[n] for n in TWIN_WEIGHTS}
    shared = {n: given[n] for n in SHARED_INPUTS}
    per_example = {n: given[n] for n in ['x']}
    grad_fn = _jax.value_and_grad(_loss, argnums=(0, 1))

    def one_microbatch(ex, loss_target):
        ex = dict(ex)
        diff = ex.pop(TWIN_DIFF_INPUT)
        return grad_fn(weights, diff, {**shared, **ex}, loss_target)

    if N_MICROBATCH == 1:
        loss, (grad_w, grad_x) = one_microbatch(per_example, given["loss_target"])
    else:
        def body(carry, xs):
            loss_sum, grad_sum = carry
            l_k, (gw_k, gx_k) = one_microbatch(xs[0], xs[1])
            with _jax.named_scope("update"):
                return (loss_sum + l_k, _jax.tree.map(_jnp.add, grad_sum, gw_k)), gx_k

        init = (_jnp.zeros((), _jnp.float32), _jax.tree.map(_jnp.zeros_like, weights))
        (loss, grad_w), grad_x = _jax.lax.scan(body, init, (per_example, given["loss_target"]))
    with _jax.named_scope("update"):
        delta_w, new_m, new_v = {}, {}, {}
        for n in TWIN_WEIGHTS:
            delta_w[n], new_m[n], new_v[n] = _adamw(weights[n], grad_w[n], given["m_" + n], given["v_" + n])
    return (loss, grad_x, *[grad_w[n] for n in TWIN_WEIGHTS], *[delta_w[n] for n in TWIN_WEIGHTS],
            *[new_m[n] for n in TWIN_WEIGHTS], *[new_v[n] for n in TWIN_WEIGHTS])
```

```python
import functools

import jax
import jax.numpy as jnp
from jax import lax
from jax.experimental import pallas as pl
from jax.experimental.pallas import tpu as pltpu

F32 = jnp.float32
BF16 = jnp.bfloat16
SDS = jax.ShapeDtypeStruct
MESH = pl.DeviceIdType.MESH

EPS = 1e-6
NEG_INF = -1e30
LANES = 128
HEAD_DIM = 64
HALO = 32
VMEM_LIMIT = 56 * 1024 * 1024

L_ROWSUM = 64
L_KDECAY = 67
L_LSE = 70
L_D = 65

ADAM_LR = 0.001
ADAM_B1 = 0.9
ADAM_B2 = 0.999
ADAM_EPS = 1e-08
ADAM_WD = 0.01
ADAM_STEP = 10


def _params(sem, vmem=VMEM_LIMIT):
    return pltpu.CompilerParams(dimension_semantics=sem, vmem_limit_bytes=vmem)


def _sigmoid(x):
    return 1.0 / (1.0 + jnp.exp(-x))


def _split3(x):
    hi = x.astype(BF16).astype(F32)
    r = x - hi
    mid = r.astype(BF16).astype(F32)
    lo = (r - mid).astype(BF16).astype(F32)
    return hi, mid, lo


def _dot(a, b):
    return jnp.dot(a, b, preferred_element_type=F32)


def _dot_nt(a, b):
    return lax.dot_general(a, b, (((1,), (1,)), ((), ())), preferred_element_type=F32)


def _dot_tn(a, b):
    return lax.dot_general(a, b, (((0,), (0,)), ((), ())), preferred_element_type=F32)


def _lane(shape):
    return lax.broadcasted_iota(jnp.int32, shape, 1)


def _lane_col(x, lane, idx):
    return jnp.sum(jnp.where(lane == idx, x, 0.0), axis=-1, keepdims=True)


def _put3(base, lane, start, pieces):
    out = base
    for k, p in enumerate(pieces):
        out = jnp.where(lane == start + k, p, out)
    return out


def _half_stats(t, lo):
    s0 = jnp.sum(jnp.where(lo, t, 0.0), axis=-1, keepdims=True)
    s1 = jnp.sum(jnp.where(lo, 0.0, t), axis=-1, keepdims=True)
    return jnp.where(lo, s0, s1)


def _fwd_in(x2, g, w_all, tm, tn):
    T, D = x2.shape
    N = w_all.shape[1]

    def body(x_ref, g_ref, w_ref, z_ref, h_ref):
        @pl.when(pl.program_id(1) == 0)
        def _():
            x = x_ref[...]
            r = lax.rsqrt(jnp.mean(x * x, axis=-1, keepdims=True) + EPS)
            h_ref[...] = (x * r * g_ref[...]).astype(BF16)

        z_ref[...] = _dot(h_ref[...], w_ref[...])

    return pl.pallas_call(
        body, name="fwd_in", grid=(T // tm, N // tn),
        in_specs=[pl.BlockSpec((tm, D), lambda i, j: (i, 0)),
                  pl.BlockSpec((1, D), lambda i, j: (0, 0)),
                  pl.BlockSpec((D, tn), lambda i, j: (0, j))],
        out_specs=[pl.BlockSpec((tm, tn), lambda i, j: (i, j)),
                   pl.BlockSpec((tm, D), lambda i, j: (i, 0))],
        out_shape=[SDS((T, N), F32), SDS((T, D), BF16)],
        compiler_params=_params(("parallel", "arbitrary")),
    )(x2, g, w_all)


def _tri_cumsum(x, reverse):
    t = x.shape[0]
    row = lax.broadcasted_iota(jnp.int32, (t, t), 0)
    col = lax.broadcasted_iota(jnp.int32, (t, t), 1)
    tri = (row <= col) if reverse else (row >= col)
    tri = jnp.where(tri, 1.0, 0.0).astype(BF16)
    hi, mid, lo = _split3(x)
    return _dot(tri, hi.astype(BF16)) + _dot(tri, mid.astype(BF16)) + _dot(tri, lo.astype(BF16))


def _gate_fwd(z, b_pad, B, S, H, col_blk, tc):
    T = B * S
    nsb = S // tc

    def body(zf_ref, b_ref, c_ref, carry):
        @pl.when(pl.program_id(1) == 0)
        def _():
            carry[...] = jnp.zeros_like(carry)

        x = zf_ref[...] + b_ref[...]
        lf = jnp.minimum(x, 0.0) - jnp.log(1.0 + jnp.exp(-jnp.abs(x)))
        lf = jnp.where(_lane(lf.shape) < H, lf, 0.0)
        c_ref[...] = _tri_cumsum(lf, False) + carry[...]
        carry[...] = carry[...] + jnp.sum(lf, axis=0, keepdims=True)

    return pl.pallas_call(
        body, name="gate_fwd", grid=(B, nsb),
        in_specs=[pl.BlockSpec((tc, LANES), lambda b, s: (b * nsb + s, col_blk)),
                  pl.BlockSpec((1, LANES), lambda b, s: (0, 0))],
        out_specs=pl.BlockSpec((tc, LANES), lambda b, s: (b * nsb + s, 0)),
        out_shape=SDS((T, LANES), F32),
        scratch_shapes=[pltpu.VMEM((1, LANES), F32)],
        compiler_params=_params(("parallel", "arbitrary")),
    )(z, b_pad)


def _qk_normalize(x, g, lo):
    r = lax.rsqrt(_half_stats(x * x, lo) * (1.0 / HEAD_DIM) + EPS)
    return x * r * g


def _attn_prep(z, c, gq, gk, B, S, H, tm):
    T = B * S
    FW = H * HEAD_DIM
    nsb = S // tm
    nfb = FW // LANES
    scale = HEAD_DIM ** -0.5

    def body(zq_ref, zk_ref, zv_ref, c_ref, gq_ref, gk_ref, qa_ref, ka_ref, va_ref):
        p = pl.program_id(1)
        lane = _lane((tm, LANES))
        lo = lane < HEAD_DIM
        qn = _qk_normalize(zq_ref[...], gq_ref[...], lo) * scale
        kn = _qk_normalize(zk_ref[...], gk_ref[...], lo)
        v = zv_ref[...]
        cc = c_ref[...]
        ones_q = ((lane >= L_KDECAY) & (lane < L_KDECAY + 3)).astype(F32)
        ones_k = (((lane >= L_ROWSUM) & (lane < L_ROWSUM + 3)) | ((lane >= L_LSE) & (lane < L_LSE + 3))).astype(F32)
        ones_v = ((lane >= L_ROWSUM) & (lane < L_D + 3)).astype(F32)
        for e in range(2):
            if e == 0:
                qe, ke, ve = qn, kn, v
            else:
                qe, ke, ve = (pltpu.roll(t, HEAD_DIM, 1) for t in (qn, kn, v))
            ch = _lane_col(cc, lane, 2 * p + e)
            pieces = _split3(ch)
            qa = jnp.where(lo, qe, _put3(ones_q, lane, L_ROWSUM, pieces))
            ka = jnp.where(lo, ke, _put3(ones_k, lane, L_KDECAY, [-t for t in pieces]))
            va = jnp.where(lo, ve, ones_v)
            qa_ref[0, e] = qa.astype(BF16)
            ka_ref[0, e] = ka.astype(BF16)
            va_ref[0, e] = va.astype(BF16)

    zspec = lambda off: pl.BlockSpec((tm, LANES), lambda i, p: (i, off + p))
    gspec = pl.BlockSpec((1, LANES), lambda i, p: (0, p))
    ospec = pl.BlockSpec((1, 2, tm, LANES), lambda i, p: (i // nsb, p, i % nsb, 0))
    oshape = SDS((B, H, S, LANES), BF16)
    return pl.pallas_call(
        body, name="attn_prep", grid=(T // tm, H // 2),
        in_specs=[zspec(0), zspec(nfb), zspec(2 * nfb),
                  pl.BlockSpec((tm, LANES), lambda i, p: (i, 0)), gspec, gspec],
        out_specs=[ospec, ospec, ospec],
        out_shape=[oshape, oshape, oshape],
        compiler_params=_params(("parallel", "arbitrary")),
    )(z, z, z, c, gq, gk)


def _attn_fwd(qa, ka, va, t):
    B, H, S, _ = qa.shape
    nq = S // t

    def body(q_ref, k_ref, v_ref, o_ref, m_ref, acc_ref):
        i = pl.program_id(2)
        q = q_ref[0, 0]
        m_ref[...] = jnp.full_like(m_ref, NEG_INF)
        acc_ref[...] = jnp.zeros_like(acc_ref)

        def step(j, masked):
            rows = pl.ds(pl.multiple_of(j * t, t), t)
            s = _dot_nt(q, k_ref[0, 0, rows, :])
            if masked:
                r = lax.broadcasted_iota(jnp.int32, (t, t), 0)
                cidx = lax.broadcasted_iota(jnp.int32, (t, t), 1)
                s = jnp.where(r >= cidx, s, NEG_INF)
            m_prev = m_ref[...]
            m_new = jnp.maximum(m_prev, jnp.max(s, axis=-1, keepdims=True))
            alpha = jnp.exp(m_prev - m_new)
            p = jnp.exp(s - m_new).astype(BF16)
            acc_ref[...] = alpha * acc_ref[...] + _dot(p, v_ref[0, 0, rows, :])
            m_ref[...] = m_new

        def loop_body(j, carry):
            step(j, False)
            return carry

        lax.fori_loop(0, i, loop_body, 0)
        step(i, True)

        acc = acc_ref[...]
        lane = _lane(acc.shape)
        l = _lane_col(acc, lane, L_ROWSUM)
        o_ref[0, 0] = jnp.where(lane < HEAD_DIM, acc / l, m_ref[...] + jnp.log(l))

    return pl.pallas_call(
        body, name="attn_fwd", grid=(B, H, nq),
        in_specs=[pl.BlockSpec((1, 1, t, LANES), lambda b, h, i: (b, h, i, 0)),
                  pl.BlockSpec((1, 1, S, LANES), lambda b, h, i: (b, h, 0, 0)),
                  pl.BlockSpec((1, 1, S, LANES), lambda b, h, i: (b, h, 0, 0))],
        out_specs=pl.BlockSpec((1, 1, t, LANES), lambda b, h, i: (b, h, i, 0)),
        out_shape=SDS((B, H, S, LANES), F32),
        scratch_shapes=[pltpu.VMEM((t, 1), F32), pltpu.VMEM((t, LANES), F32)],
        compiler_params=_params(("parallel", "parallel", "arbitrary")),
    )(qa, ka, va)


def _conv_taps(w_ref, ext_ref, n_taps, tm, first):
    acc = None
    for j in range(n_taps):
        term = w_ref[j:j + 1, :] * ext_ref[pl.ds(first(j), tm), :]
        acc = term if acc is None else acc + term
    return acc


def _layernorm_stats(u2):
    mu = jnp.mean(u2, axis=-1, keepdims=True)
    xc = u2 - mu
    rstd = lax.rsqrt(jnp.mean(xc * xc, axis=-1, keepdims=True) + EPS)
    return xc * rstd, rstd


def _fwd_out(oa, z, x2, tgt, conv_w, conv_b, ln_g, ln_b, w_out, B, S, H, n_taps, tm):
    T, D = x2.shape
    FW = H * HEAD_DIM
    CW = conv_w.shape[1]
    nsb = S // tm
    hb = tm // HALO

    def body(oa_ref, gf_ref, ga_ref, gb_ref, gc_ref, ha_ref, hb_ref, x_ref, t_ref, w_ref, cb_ref, lg_ref,
             lb_ref, wo_ref, y_ref, u2_ref, a_ref, do_ref, dob_ref, dy_ref, loss_ref, ext_ref):
        first_step = (pl.program_id(0) == 0) & (pl.program_id(1) == 0)

        @pl.when(first_step)
        def _():
            loss_ref[...] = jnp.zeros_like(loss_ref)

        u1 = ga_ref[...] * _sigmoid(gb_ref[...])
        halo = ha_ref[...] * _sigmoid(hb_ref[...])
        ext_ref[0:HALO, :] = jnp.where(pl.program_id(1) > 0, halo, 0.0)
        ext_ref[HALO:, :] = u1
        u2 = _conv_taps(w_ref, ext_ref, n_taps, tm, lambda j: HALO - (n_taps - 1) + j) + cb_ref[...]
        u2_ref[...] = u2
        uh, _ = _layernorm_stats(u2)
        u3 = uh * lg_ref[...] + lb_ref[...]
        gc = gc_ref[...]
        yu = u3 * _sigmoid(u3) * (gc * _sigmoid(gc))
        y_ref[:, FW:] = yu.astype(BF16)

        lane = _lane((tm, LANES))
        lo = lane < HEAD_DIM
        for p in range(H // 2):
            a_ref[:, p * LANES:(p + 1) * LANES] = jnp.where(
                lo, oa_ref[0, 2 * p], pltpu.roll(oa_ref[0, 2 * p + 1], HEAD_DIM, 1))
        gf = gf_ref[...]
        y_ref[:, :FW] = (a_ref[...] * (gf * _sigmoid(gf))).astype(BF16)

        out = x_ref[...] + _dot(y_ref[...], wo_ref[...])
        diff = out - t_ref[...]
        loss_ref[...] = loss_ref[...] + jnp.sum(diff * diff)
        dout = diff * (1.0 / D)
        do_ref[...] = dout
        dob = dout.astype(BF16)
        dob_ref[...] = dob
        dy_ref[...] = _dot_nt(dob, wo_ref[...])

    row = lambda b, s: b * nsb + s
    zspec = lambda cb: pl.BlockSpec((tm, FW), lambda b, s: (row(b, s), cb))
    hspec = lambda cb: pl.BlockSpec((HALO, CW), lambda b, s: (jnp.maximum(row(b, s) * hb - 1, 0), cb))
    vspec = pl.BlockSpec((1, CW), lambda b, s: (0, 0))
    tspec = lambda w: pl.BlockSpec((tm, w), lambda b, s: (row(b, s), 0))
    return pl.pallas_call(
        body, name="fwd_out", grid=(B, nsb),
        in_specs=[pl.BlockSpec((1, H, tm, LANES), lambda b, s: (b, 0, s, 0)),
                  zspec(3), zspec(4), zspec(5), zspec(6), hspec(4), hspec(5),
                  tspec(D), tspec(D),
                  pl.BlockSpec((HALO, CW), lambda b, s: (0, 0)), vspec, vspec, vspec,
                  pl.BlockSpec((FW + CW, D), lambda b, s: (0, 0))],
        out_specs=[tspec(FW + CW), tspec(CW), tspec(FW), tspec(D), tspec(D), tspec(FW + CW),
                   pl.BlockSpec((8, LANES), lambda b, s: (0, 0))],
        out_shape=[SDS((T, FW + CW), BF16), SDS((T, CW), F32), SDS((T, FW), F32), SDS((T, D), F32),
                   SDS((T, D), BF16), SDS((T, FW + CW), F32), SDS((8, LANES), F32)],
        scratch_shapes=[pltpu.VMEM((tm + HALO, CW), F32)],
        compiler_params=_params(("arbitrary", "arbitrary")),
    )(oa, z, z, z, z, z, z, x2, tgt, conv_w, conv_b, ln_g, ln_b, w_out)


def _bwd_prep(dy, z, a_nat, oa, qa, u2, ln_g, ln_b, B, S, H, tm):
    T = B * S
    FW = H * HEAD_DIM
    CW = u2.shape[1]
    nsb = S // tm

    def body(dya_ref, dyu_ref, gf_ref, gc_ref, a_ref, oa_ref, qa_ref, u2_ref, lg_ref, lb_ref,
             dzgf_ref, dzgc_ref, du2_ref, doa_ref, qb_ref, sg_ref):
        first_step = (pl.program_id(0) == 0) & (pl.program_id(1) == 0)

        @pl.when(first_step)
        def _():
            sg_ref[...] = jnp.zeros_like(sg_ref)

        gf = gf_ref[...]
        sg = _sigmoid(gf)
        a = a_ref[...]
        dya = dya_ref[...]
        da = dya * (gf * sg)
        dzgf_ref[...] = (dya * a * (sg * (1.0 + gf * (1.0 - sg)))).astype(BF16)
        dd = da * a
        lane = _lane((tm, LANES))
        lo = lane < HEAD_DIM
        for p in range(H // 2):
            cols = slice(p * LANES, (p + 1) * LANES)
            da_p = da[:, cols]
            dd_p = dd[:, cols]
            d_heads = (jnp.sum(jnp.where(lo, dd_p, 0.0), axis=-1, keepdims=True),
                       jnp.sum(jnp.where(lo, 0.0, dd_p), axis=-1, keepdims=True))
            for e in range(2):
                da_e = da_p if e == 0 else pltpu.roll(da_p, HEAD_DIM, 1)
                d_e = d_heads[e]
                aug = _put3(jnp.zeros((tm, LANES), F32), lane, L_D, _split3(-d_e))
                doa_ref[0, 2 * p + e] = jnp.where(lo, da_e, aug).astype(BF16)
                lse = _lane_col(oa_ref[0, 2 * p + e], lane, L_ROWSUM)
                qb = _put3(qa_ref[0, 2 * p + e].astype(F32), lane, L_LSE, _split3(-lse))
                qb_ref[0, 2 * p + e] = qb.astype(BF16)

        gc = gc_ref[...]
        sc = _sigmoid(gc)
        dyu = dyu_ref[...]
        uh, rstd = _layernorm_stats(u2_ref[...])
        u3 = uh * lg_ref[...] + lb_ref[...]
        s3 = _sigmoid(u3)
        dzgc_ref[...] = (dyu * (u3 * s3) * (sc * (1.0 + gc * (1.0 - sc)))).astype(BF16)
        du3 = dyu * (gc * sc) * (s3 * (1.0 + u3 * (1.0 - s3)))
        sg_ref[0:1, :] = sg_ref[0:1, :] + jnp.sum(du3 * uh, axis=0, keepdims=True)
        sg_ref[1:2, :] = sg_ref[1:2, :] + jnp.sum(du3, axis=0, keepdims=True)
        duh = du3 * lg_ref[...]
        du2 = rstd * (duh - jnp.mean(duh, axis=-1, keepdims=True)
                      - uh * jnp.mean(duh * uh, axis=-1, keepdims=True))
        sg_ref[2:3, :] = sg_ref[2:3, :] + jnp.sum(du2, axis=0, keepdims=True)
        du2_ref[...] = du2

    row = lambda b, s: b * nsb + s
    tspec = lambda w, cb=0: pl.BlockSpec((tm, w), lambda b, s: (row(b, s), cb))
    hspec = pl.BlockSpec((1, H, tm, LANES), lambda b, s: (b, 0, s, 0))
    vspec = pl.BlockSpec((1, CW), lambda b, s: (0, 0))
    return pl.pallas_call(
        body, name="bwd_prep", grid=(B, nsb),
        in_specs=[tspec(FW, 0), tspec(CW, 1), tspec(FW, 3), tspec(CW, 6), tspec(FW), hspec, hspec,
                  tspec(CW), vspec, vspec],
        out_specs=[tspec(FW), tspec(CW), tspec(CW), hspec, hspec,
                   pl.BlockSpec((8, CW), lambda b, s: (0, 0))],
        out_shape=[SDS((T, FW), BF16), SDS((T, CW), BF16), SDS((T, CW), F32),
                   SDS((B, H, S, LANES), BF16), SDS((B, H, S, LANES), BF16), SDS((8, CW), F32)],
        compiler_params=_params(("arbitrary", "arbitrary")),
    )(dy, dy, z, z, a_nat, oa, qa, u2, ln_g, ln_b)


def _conv_bwd(du2, z, conv_w, B, S, n_taps, tm):
    T, CW = du2.shape
    nsb = S // tm
    hb = tm // HALO

    def body(d_ref, dh_ref, ga_ref, gb_ref, ha_ref, hb_ref, w_ref, dz_ref, dw_ref, extu_ref, extd_ref):
        s = pl.program_id(1)
        first_step = (pl.program_id(0) == 0) & (s == 0)

        @pl.when(first_step)
        def _():
            dw_ref[...] = jnp.zeros_like(dw_ref)

        ga = ga_ref[...]
        sb = _sigmoid(gb_ref[...])
        halo = ha_ref[...] * _sigmoid(hb_ref[...])
        extu_ref[0:HALO, :] = jnp.where(s > 0, halo, 0.0)
        extu_ref[HALO:, :] = ga * sb
        du2v = d_ref[...]
        extd_ref[0:tm, :] = du2v
        extd_ref[tm:, :] = jnp.where(s < nsb - 1, dh_ref[...], 0.0)
        du1 = _conv_taps(w_ref, extd_ref, n_taps, tm, lambda j: n_taps - 1 - j)
        for j in range(n_taps):
            dw_ref[j:j + 1, :] = dw_ref[j:j + 1, :] + jnp.sum(
                du2v * extu_ref[pl.ds(HALO - (n_taps - 1) + j, tm), :], axis=0, keepdims=True)
        dz_ref[:, :CW] = (du1 * sb).astype(BF16)
        dz_ref[:, CW:] = (du1 * ga * (sb * (1.0 - sb))).astype(BF16)

    row = lambda b, s: b * nsb + s
    last_halo = T // HALO - 1
    return pl.pallas_call(
        body, name="conv_bwd", grid=(B, nsb),
        in_specs=[pl.BlockSpec((tm, CW), lambda b, s: (row(b, s), 0)),
                  pl.BlockSpec((HALO, CW), lambda b, s: (jnp.minimum((row(b, s) + 1) * hb, last_halo), 0)),
                  pl.BlockSpec((tm, CW), lambda b, s: (row(b, s), 4)),
                  pl.BlockSpec((tm, CW), lambda b, s: (row(b, s), 5)),
                  pl.BlockSpec((HALO, CW), lambda b, s: (jnp.maximum(row(b, s) * hb - 1, 0), 4)),
                  pl.BlockSpec((HALO, CW), lambda b, s: (jnp.maximum(row(b, s) * hb - 1, 0), 5)),
                  pl.BlockSpec((HALO, CW), lambda b, s: (0, 0))],
        out_specs=[pl.BlockSpec((tm, 2 * CW), lambda b, s: (row(b, s), 0)),
                   pl.BlockSpec((HALO, CW), lambda b, s: (0, 0))],
        out_shape=[SDS((T, 2 * CW), BF16), SDS((HALO, CW), F32)],
        scratch_shapes=[pltpu.VMEM((tm + HALO, CW), F32), pltpu.VMEM((tm + HALO, CW), F32)],
        compiler_params=_params(("arbitrary", "arbitrary")),
    )(du2, du2, z, z, z, z, conv_w)


def _attn_bwd(qb, ka, va, doa, t):
    B, H, S, _ = qb.shape
    nk = S // t

    def body(q_ref, k_ref, v_ref, do_ref, dq_ref, dk_ref, dv_ref):
        j = pl.program_id(2)

        @pl.when(j == 0)
        def _():
            dq_ref[...] = jnp.zeros_like(dq_ref)

        k = k_ref[0, 0]
        v = v_ref[0, 0]
        dk_ref[...] = jnp.zeros_like(dk_ref)
        dv_ref[...] = jnp.zeros_like(dv_ref)

        def step(i, masked):
            rows = pl.ds(pl.multiple_of(i * t, t), t)
            q = q_ref[0, 0, rows, :]
            do = do_ref[0, 0, rows, :]
            p = jnp.exp(_dot_nt(q, k))
            if masked:
                r = lax.broadcasted_iota(jnp.int32, (t, t), 0)
                cidx = lax.broadcasted_iota(jnp.int32, (t, t), 1)
                p = jnp.where(r >= cidx, p, 0.0)
            ds = (p * _dot_nt(do, v)).astype(BF16)
            dv_ref[0, 0] = dv_ref[0, 0] + _dot_tn(p.astype(BF16), do)
            dk_ref[0, 0] = dk_ref[0, 0] + _dot_tn(ds, q)
            dq_ref[0, 0, rows, :] = dq_ref[0, 0, rows, :] + _dot(ds, k)

        step(j, True)

        def loop_body(i, carry):
            step(i, False)
            return carry

        lax.fori_loop(j + 1, nk, loop_body, 0)

    full = pl.BlockSpec((1, 1, S, LANES), lambda b, h, j: (b, h, 0, 0))
    blk = pl.BlockSpec((1, 1, t, LANES), lambda b, h, j: (b, h, j, 0))
    oshape = SDS((B, H, S, LANES), F32)
    return pl.pallas_call(
        body, name="attn_bwd", grid=(B, H, nk),
        in_specs=[full, blk, blk, full],
        out_specs=[full, blk, blk],
        out_shape=[oshape, oshape, oshape],
        compiler_params=_params(("parallel", "parallel", "arbitrary")),
    )(qb, ka, va, doa)


def _qk_bwd(dqa, dka, dva, z, gq, gk, B, S, H, tm):
    T = B * S
    FW = H * HEAD_DIM
    nsb = S // tm
    nfb = FW // LANES
    scale = HEAD_DIM ** -0.5

    def body(dq_ref, dk_ref, dv_ref, zq_ref, zk_ref, gq_ref, gk_ref, dzq_ref, dzk_ref, dzv_ref, dc_ref, dg_ref):
        p = pl.program_id(0)

        @pl.when(pl.program_id(1) == 0)
        def _():
            dg_ref[...] = jnp.zeros_like(dg_ref)

        lane = _lane((tm, LANES))
        lo = lane < HEAD_DIM

        def natural(ref):
            return jnp.where(lo, ref[0, 0], pltpu.roll(ref[0, 1], HEAD_DIM, 1))

        def norm_bwd(dn, x, g, row, out_ref):
            r = lax.rsqrt(_half_stats(x * x, lo) * (1.0 / HEAD_DIM) + EPS)
            xh = x * r
            dg_ref[row:row + 1, :] = dg_ref[row:row + 1, :] + jnp.sum(dn * xh, axis=0, keepdims=True)
            dxh = dn * g
            mm = _half_stats(dxh * xh, lo) * (1.0 / HEAD_DIM)
            out_ref[...] = (r * (dxh - xh * mm)).astype(BF16)

        norm_bwd(natural(dq_ref) * scale, zq_ref[...], gq_ref[...], 0, dzq_ref)
        norm_bwd(natural(dk_ref), zk_ref[...], gk_ref[...], 1, dzk_ref)
        dzv_ref[...] = natural(dv_ref).astype(BF16)

        dc = jnp.zeros((tm, LANES), F32)
        for e in range(2):
            val = _lane_col(dq_ref[0, e], lane, L_ROWSUM) - _lane_col(dk_ref[0, e], lane, L_KDECAY)
            dc = jnp.where(lane == 2 * p + e, val, dc)
        dc_ref[0] = dc

    hspec = pl.BlockSpec((1, 2, tm, LANES), lambda p, i: (i // nsb, p, i % nsb, 0))
    zspec = lambda off: pl.BlockSpec((tm, LANES), lambda p, i: (i, off + p))
    gspec = pl.BlockSpec((1, LANES), lambda p, i: (0, p))
    ospec = pl.BlockSpec((tm, LANES), lambda p, i: (i, p))
    return pl.pallas_call(
        body, name="qk_bwd", grid=(H // 2, T // tm),
        in_specs=[hspec, hspec, hspec, zspec(0), zspec(nfb), gspec, gspec],
        out_specs=[ospec, ospec, ospec,
                   pl.BlockSpec((1, tm, LANES), lambda p, i: (p, i, 0)),
                   pl.BlockSpec((8, LANES), lambda p, i: (0, p))],
        out_shape=[SDS((T, FW), BF16), SDS((T, FW), BF16), SDS((T, FW), BF16),
                   SDS((H // 2, T, LANES), F32), SDS((8, FW), F32)],
        compiler_params=_params(("parallel", "arbitrary")),
    )(dqa, dka, dva, z, z, gq, gk)


def _gate_bwd(dc8, z, b_pad, B, S, H, col_blk, fp, tc):
    T = B * S
    nsb = S // tc
    npair = dc8.shape[0]

    def body(dc_ref, zf_ref, b_ref, dz_ref, db_ref, carry):
        first_step = (pl.program_id(0) == 0) & (pl.program_id(1) == 0)

        @pl.when(first_step)
        def _():
            db_ref[...] = jnp.zeros_like(db_ref)

        @pl.when(pl.program_id(1) == 0)
        def _():
            carry[...] = jnp.zeros_like(carry)

        dc = dc_ref[0]
        for k in range(1, npair):
            dc = dc + dc_ref[k]
        dlf = _tri_cumsum(dc, True) + carry[...]
        carry[...] = carry[...] + jnp.sum(dc, axis=0, keepdims=True)
        x = zf_ref[...] + b_ref[...]
        dlogit = dlf * _sigmoid(-x)
        db_ref[0:1, :] = db_ref[0:1, :] + jnp.sum(dlogit, axis=0, keepdims=True)
        dz_ref[...] = jnp.zeros_like(dz_ref)
        dz_ref[:, :LANES] = dlogit.astype(BF16)

    rrow = lambda b, s: b * nsb + (nsb - 1 - s)
    return pl.pallas_call(
        body, name="gate_bwd", grid=(B, nsb),
        in_specs=[pl.BlockSpec((npair, tc, LANES), lambda b, s: (0, rrow(b, s), 0)),
                  pl.BlockSpec((tc, LANES), lambda b, s: (rrow(b, s), col_blk)),
                  pl.BlockSpec((1, LANES), lambda b, s: (0, 0))],
        out_specs=[pl.BlockSpec((tc, fp), lambda b, s: (rrow(b, s), 0)),
                   pl.BlockSpec((8, LANES), lambda b, s: (0, 0))],
        out_shape=[SDS((T, fp), BF16), SDS((8, LANES), F32)],
        scratch_shapes=[pltpu.VMEM((1, LANES), F32)],
        compiler_params=_params(("arbitrary", "arbitrary")),
    )(dc8, z, b_pad)


def _matmul_tn(a, b, name, tmm, tn, tk):
    T, M = a.shape
    N = b.shape[1]
    tmm, tn, tk = min(tmm, M), min(tn, N), min(tk, T)

    def body(a_ref, b_ref, o_ref):
        @pl.when(pl.program_id(2) == 0)
        def _():
            o_ref[...] = jnp.zeros_like(o_ref)

        o_ref[...] = o_ref[...] + _dot_tn(a_ref[...], b_ref[...])

    return pl.pallas_call(
        body, name=name, grid=(M // tmm, N // tn, T // tk),
        in_specs=[pl.BlockSpec((tk, tmm), lambda i, j, k: (k, i)),
                  pl.BlockSpec((tk, tn), lambda i, j, k: (k, j))],
        out_specs=pl.BlockSpec((tmm, tn), lambda i, j, k: (i, j)),
        out_shape=SDS((M, N), F32),
        compiler_params=_params(("parallel", "parallel", "arbitrary")),
    )(a, b)


def _dh_rms_bwd(pieces, w_all, x2, g, dout, tm, tk):
    T, D = x2.shape
    nks = [p.shape[1] // tk for p in pieces]
    starts = [sum(nks[:k]) for k in range(len(pieces))]
    nk = sum(nks)

    def body(*refs):
        dz_refs = refs[:len(pieces)]
        w_ref, x_ref, g_ref, do_ref, gx_ref, dg_ref, acc_ref = refs[len(pieces):]
        k = pl.program_id(1)
        first_step = (pl.program_id(0) == 0) & (k == 0)

        @pl.when(first_step)
        def _():
            dg_ref[...] = jnp.zeros_like(dg_ref)

        @pl.when(k == 0)
        def _():
            acc_ref[...] = jnp.zeros_like(acc_ref)

        for dz_ref, st, n in zip(dz_refs, starts, nks):
            @pl.when((k >= st) & (k < st + n))
            def _(dz_ref=dz_ref):
                acc_ref[...] = acc_ref[...] + _dot_nt(dz_ref[...], w_ref[...])

        @pl.when(k == nk - 1)
        def _():
            x = x_ref[...]
            r = lax.rsqrt(jnp.mean(x * x, axis=-1, keepdims=True) + EPS)
            xh = x * r
            dh = acc_ref[...]
            dg_ref[0:1, :] = dg_ref[0:1, :] + jnp.sum(dh * xh, axis=0, keepdims=True)
            dxn = dh * g_ref[...]
            gx_ref[...] = do_ref[...] + r * (dxn - xh * jnp.mean(dxn * xh, axis=-1, keepdims=True))

    def piece_spec(st, n):
        return pl.BlockSpec((tm, tk), lambda i, k: (i, jnp.clip(k - st, 0, n - 1)))

    tspec = pl.BlockSpec((tm, D), lambda i, k: (i, 0))
    return pl.pallas_call(
        body, name="dh_rms_bwd", grid=(T // tm, nk),
        in_specs=[piece_spec(st, n) for st, n in zip(starts, nks)]
        + [pl.BlockSpec((D, tk), lambda i, k: (0, k)), tspec, pl.BlockSpec((1, D), lambda i, k: (0, 0)), tspec],
        out_specs=[tspec, pl.BlockSpec((8, D), lambda i, k: (0, 0))],
        out_shape=[SDS((T, D), F32), SDS((8, D), F32)],
        scratch_shapes=[pltpu.VMEM((tm, D), F32)],
        compiler_params=_params(("arbitrary", "arbitrary")),
    )(*pieces, w_all, x2, g, dout)


def _rows_call(body, name, ins, n_out, out_dtypes, tr):
    R, C = ins[0].shape
    tr = min(tr, R)
    spec = pl.BlockSpec((tr, C), lambda i: (i, 0))
    return pl.pallas_call(
        body, name=name, grid=(pl.cdiv(R, tr),),
        in_specs=[spec] * len(ins), out_specs=[spec] * n_out,
        out_shape=[SDS((R, C), dt) for dt in out_dtypes],
        compiler_params=_params(("parallel",)),
    )(*ins)


def _pair_sum_bf16(a, b, name):
    def body(a_ref, b_ref, o_ref):
        o_ref[...] = (a_ref[...] + b_ref[...]).astype(BF16)

    return _rows_call(body, name, [a, b], 1, [BF16], 256)[0]


def _sum_slots(slots, name, tr=256):
    n, R, C = slots.shape
    tr = min(tr, R)

    def body(s_ref, o_ref):
        acc = s_ref[0].astype(F32)
        for k in range(1, n):
            acc = acc + s_ref[k].astype(F32)
        o_ref[...] = acc

    return pl.pallas_call(
        body, name=name, grid=(pl.cdiv(R, tr),),
        in_specs=[pl.BlockSpec((n, tr, C), lambda i: (0, i, 0))],
        out_specs=pl.BlockSpec((tr, C), lambda i: (i, 0)),
        out_shape=SDS((R, C), F32),
        compiler_params=_params(("parallel",)),
    )(slots)


def _adamw(w, g, m, v, name):
    def body(w_ref, g_ref, m_ref, v_ref, d_ref, nm_ref, nv_ref):
        gg = g_ref[...]
        nm = ADAM_B1 * m_ref[...] + (1.0 - ADAM_B1) * gg
        nv = ADAM_B2 * v_ref[...] + (1.0 - ADAM_B2) * (gg * gg)
        m_hat = nm / (1.0 - ADAM_B1 ** ADAM_STEP)
        v_hat = nv / (1.0 - ADAM_B2 ** ADAM_STEP)
        d_ref[...] = -ADAM_LR * (m_hat / (jnp.sqrt(v_hat) + ADAM_EPS) + ADAM_WD * w_ref[...])
        nm_ref[...] = nm
        nv_ref[...] = nv

    return _rows_call(body, name, [w, g, m, v], 3, [F32, F32, F32], 128)


ANY = pl.BlockSpec(memory_space=pl.ANY)


def _place():
    return lax.axis_index("x"), lax.axis_index("y"), lax.axis_index("c")


def _gather_chips(shards):
    n = len(shards)

    def body(*refs):
        ins, outs = refs[:n], refs[n:2 * n]
        send_sems, recv_sems, local_sems = refs[2 * n:]
        x, y, c = _place()
        mine = 2 * x + y
        sibling = (x, y, 1 - c)
        chips = [(1 - x, y), (x, 1 - y), (1 - x, 1 - y)]

        def copy(a, k, chip_idx, half, to, src=None):
            dst = outs[a].at[chip_idx, half]
            return pltpu.make_async_remote_copy(
                src_ref=dst if src is None else src, dst_ref=dst,
                send_sem=send_sems.at[a * 6 + k], recv_sem=recv_sems.at[a * 6 + k],
                device_id=to, device_id_type=MESH)

        local = [pltpu.make_async_copy(ins[a], outs[a].at[mine], local_sems.at[a]) for a in range(n)]
        for cp in local:
            cp.start()
        first = [copy(a, k, mine, c, (*chip, c), src=ins[a].at[c]) for a in range(n) for k, chip in enumerate(chips)]
        for cp in first:
            cp.start()
        passed = []
        for k, (cx, cy) in enumerate(chips):
            for a in range(n):
                copy(a, k, 2 * cx + cy, c, (x, y, c)).wait_recv()
                fwd = copy(a, 3 + k, 2 * cx + cy, c, sibling)
                fwd.start()
                passed.append(fwd)
        for k, (cx, cy) in enumerate(chips):
            for a in range(n):
                copy(a, 3 + k, 2 * cx + cy, 1 - c, (x, y, c)).wait_recv()
        for cp in first + passed:
            cp.wait_send()
        for cp in local:
            cp.wait()

    return pl.pallas_call(
        body, name="gather_chips",
        in_specs=[ANY] * n, out_specs=[ANY] * n,
        out_shape=[SDS((4,) + s.shape, s.dtype) for s in shards],
        scratch_shapes=[pltpu.SemaphoreType.DMA((6 * n,)), pltpu.SemaphoreType.DMA((6 * n,)),
                        pltpu.SemaphoreType.DMA((n,))],
    )(*shards)


def _pair_swap(halves):
    n = len(halves)

    def body(*refs):
        ins, outs = refs[:n], refs[n:2 * n]
        send_sems, recv_sems = refs[2 * n:]
        x, y, c = _place()
        copies = [pltpu.make_async_remote_copy(
            src_ref=ins[a].at[1 - c], dst_ref=outs[a], send_sem=send_sems.at[a], recv_sem=recv_sems.at[a],
            device_id=(x, y, 1 - c), device_id_type=MESH) for a in range(n)]
        for cp in copies:
            cp.start()
        for cp in copies:
            cp.wait()

    return pl.pallas_call(
        body, name="pair_swap",
        in_specs=[ANY] * n, out_specs=[ANY] * n,
        out_shape=[SDS(h.shape[1:], h.dtype) for h in halves],
        scratch_shapes=[pltpu.SemaphoreType.DMA((n,)), pltpu.SemaphoreType.DMA((n,))],
    )(*halves)


def _chip_exchange(parts):
    n = len(parts)

    def body(*refs):
        ins, outs = refs[:n], refs[n:2 * n]
        send_sems, recv_sems, local_sems = refs[2 * n:]
        x, y, c = _place()
        mine = 2 * x + y
        chips = [(1 - x, y), (x, 1 - y), (1 - x, 1 - y)]
        local = [pltpu.make_async_copy(ins[a].at[mine], outs[a].at[mine], local_sems.at[a]) for a in range(n)]
        for cp in local:
            cp.start()
        sends = []
        for a in range(n):
            for k, (cx, cy) in enumerate(chips):
                cp = pltpu.make_async_remote_copy(
                    src_ref=ins[a].at[2 * cx + cy], dst_ref=outs[a].at[mine],
                    send_sem=send_sems.at[a * 3 + k], recv_sem=recv_sems.at[a * 3 + k],
                    device_id=(cx, cy, c), device_id_type=MESH)
                cp.start()
                sends.append(cp)
        for a in range(n):
            for k, (cx, cy) in enumerate(chips):
                pltpu.make_async_remote_copy(
                    src_ref=ins[a].at[mine], dst_ref=outs[a].at[2 * cx + cy],
                    send_sem=send_sems.at[a * 3 + k], recv_sem=recv_sems.at[a * 3 + k],
                    device_id=(x, y, c), device_id_type=MESH).wait_recv()
        for cp in sends:
            cp.wait_send()
        for cp in local:
            cp.wait()

    return pl.pallas_call(
        body, name="chip_exchange",
        in_specs=[ANY] * n, out_specs=[ANY] * n,
        out_shape=[SDS(p.shape, p.dtype) for p in parts],
        scratch_shapes=[pltpu.SemaphoreType.DMA((3 * n,)), pltpu.SemaphoreType.DMA((3 * n,)),
                        pltpu.SemaphoreType.DMA((n,))],
    )(*parts)


def _pair_share(mine_halves):
    n = len(mine_halves)

    def body(*refs):
        ins, outs = refs[:n], refs[n:2 * n]
        send_sems, recv_sems, local_sems = refs[2 * n:]
        x, y, c = _place()
        local = [pltpu.make_async_copy(ins[a], outs[a].at[c], local_sems.at[a]) for a in range(n)]
        copies = [pltpu.make_async_remote_copy(
            src_ref=ins[a], dst_ref=outs[a].at[c], send_sem=send_sems.at[a], recv_sem=recv_sems.at[a],
            device_id=(x, y, 1 - c), device_id_type=MESH) for a in range(n)]
        for cp in local + copies:
            cp.start()
        for a in range(n):
            pltpu.make_async_remote_copy(
                src_ref=ins[a], dst_ref=outs[a].at[1 - c], send_sem=send_sems.at[a], recv_sem=recv_sems.at[a],
                device_id=(x, y, c), device_id_type=MESH).wait_recv()
        for cp in copies:
            cp.wait_send()
        for cp in local:
            cp.wait()

    return pl.pallas_call(
        body, name="pair_share",
        in_specs=[ANY] * n, out_specs=[ANY] * n,
        out_shape=[SDS((2,) + h.shape, h.dtype) for h in mine_halves],
        scratch_shapes=[pltpu.SemaphoreType.DMA((n,)), pltpu.SemaphoreType.DMA((n,)),
                        pltpu.SemaphoreType.DMA((n,))],
    )(*mine_halves)


def _gather_all(buf):
    flips = [(fx, fy, fc) for fx in (0, 1) for fy in (0, 1) for fc in (0, 1)][1:]

    def body(in_ref, out_ref, send_sems, recv_sems, local_sem):
        x, y, c = _place()
        me = 4 * x + 2 * y + c
        local = pltpu.make_async_copy(in_ref, out_ref.at[me], local_sem)
        local.start()
        sends = []
        for k, (fx, fy, fc) in enumerate(flips):
            cp = pltpu.make_async_remote_copy(
                src_ref=in_ref, dst_ref=out_ref.at[me], send_sem=send_sems.at[k], recv_sem=recv_sems.at[k],
                device_id=(x ^ fx, y ^ fy, c ^ fc), device_id_type=MESH)
            cp.start()
            sends.append(cp)
        for k, (fx, fy, fc) in enumerate(flips):
            src = 4 * (x ^ fx) + 2 * (y ^ fy) + (c ^ fc)
            pltpu.make_async_remote_copy(
                src_ref=in_ref, dst_ref=out_ref.at[src], send_sem=send_sems.at[k], recv_sem=recv_sems.at[k],
                device_id=(x, y, c), device_id_type=MESH).wait_recv()
        for cp in sends:
            cp.wait_send()
        local.wait()

    return pl.pallas_call(
        body, name="gather_all",
        in_specs=[ANY], out_specs=ANY,
        out_shape=SDS((8,) + buf.shape, buf.dtype),
        scratch_shapes=[pltpu.SemaphoreType.DMA((7,)), pltpu.SemaphoreType.DMA((7,)), pltpu.SemaphoreType.DMA],
    )(buf)


def _tiles(S, FW):
    big = FW % 512 == 0
    return dict(
        fp=512 if big else LANES,
        tn=512 if big else LANES,
        tm_in=min(512, S),
        t_attn=min(512, S),
        tm_prep=min(512, S),
        tm_mix=min(128, S),
        tc=min(256, S),
        tk=512 if big else LANES,
    )


def kernel(x, norm_g, w_in, b_forget, q_norm_g, k_norm_g, conv_w, conv_b, conv_ln_g, conv_ln_b, w_out, loss_target, m_norm_g, m_w_in, m_b_forget, m_q_norm_g, m_k_norm_g, m_conv_w, m_conv_b, m_conv_ln_g, m_conv_ln_b, m_w_out, v_norm_g, v_w_in, v_b_forget, v_q_norm_g, v_k_norm_g, v_conv_w, v_conv_b, v_conv_ln_g, v_conv_ln_b, v_w_out):
    B, S, D = x.shape
    H, dh = q_norm_g.shape[1:]
    FW = H * dh
    CW = conv_b.shape[-1]
    n_taps, cw_shard = conv_w.shape[1:]
    in_shard = w_in.shape[2]
    out_shard = w_out.shape[1]
    assert dh == HEAD_DIM and H % 2 == 0 and H <= LANES and FW == CW == D
    assert n_taps - 1 <= HALO and 4 * cw_shard == CW and 4 * out_shard == FW + CW
    assert 4 * in_shard == 4 * FW + 3 * CW + H
    T = B * S
    tl = _tiles(S, FW)
    fp = tl["fp"]
    xi, yi, ci = _place()

    conv_pad = jnp.pad(conv_w[0], ((0, HALO - n_taps), (0, 0)))
    g_in, g_out, g_cw = _gather_chips([
        w_in[0].astype(BF16).reshape(2, D // 2, in_shard),
        w_out[0].astype(BF16).reshape(2, out_shard // 2, D),
        conv_pad.reshape(2, HALO // 2, cw_shard)])
    w_in_full = g_in.reshape(4, D, in_shard).transpose(1, 0, 2).reshape(D, 4 * in_shard)
    w_out_full = g_out.reshape(FW + CW, D)
    conv_full = g_cw.reshape(4, HALO, cw_shard).transpose(1, 0, 2).reshape(HALO, CW)
    o_f = 3 * FW
    w_all = jnp.concatenate([w_in_full[:, :o_f], w_in_full[:, o_f + H:],
                             jnp.pad(w_in_full[:, o_f:o_f + H], ((0, 0), (0, fp - H)))], axis=1)
    f_col = 4 * FW + 3 * CW

    x2 = x.reshape(T, D)
    tgt = loss_target.reshape(T, D)
    b_pad = jnp.pad(b_forget, ((0, 0), (0, LANES - H)))
    gq = q_norm_g.reshape(1, FW)
    gk = k_norm_g.reshape(1, FW)

    z, h = _fwd_in(x2, norm_g, w_all, tl["tm_in"], tl["tn"])
    c = _gate_fwd(z, b_pad, B, S, H, f_col // LANES, tl["tc"])
    qa, ka, va = _attn_prep(z, c, gq, gk, B, S, H, tl["tm_prep"])
    oa = _attn_fwd(qa, ka, va, tl["t_attn"])
    y, u2, a_nat, dout, dout_b, dy, loss_acc = _fwd_out(
        oa, z, x2, tgt, conv_full, conv_b, conv_ln_g, conv_ln_b, w_out_full, B, S, H, n_taps, tl["tm_mix"])
    loss = lax.psum(0.5 * loss_acc[0, 0] / D, ("x", "y", "c"))

    dzgf, dzgc, du2, doa, qb, sg_conv = _bwd_prep(dy, z, a_nat, oa, qa, u2, conv_ln_g, conv_ln_b, B, S, H, tl["tm_mix"])
    dzglu, dconv_w = _conv_bwd(du2, z, conv_full, B, S, n_taps, tl["tm_mix"])
    dqa, dka, dva = _attn_bwd(qb, ka, va, doa, tl["t_attn"])
    dzq, dzk, dzv, dc8, dg_qk = _qk_bwd(dqa, dka, dva, z, gq, gk, B, S, H, tl["tm_prep"])
    dzf, db_f = _gate_bwd(dc8, z, b_pad, B, S, H, f_col // LANES, fp, tl["tc"])
    pieces = [dzq, dzk, dzv, dzgf, dzglu, dzgc, dzf]
    grad_x2, dg_norm = _dh_rms_bwd(pieces, w_all, x2, norm_g, dout, tl["tm_in"], tl["tk"])
    dw_pieces = [_matmul_tn(h, p, f"dw_in_{k}", 1024, 512, 512) for k, p in enumerate(pieces)]
    dw_out = _matmul_tn(y, dout_b, "dw_out", 1024, 512, 512)
    dw_in = jnp.concatenate(dw_pieces[:3] + [dw_pieces[6][:, :H]] + dw_pieces[3:6], axis=1)

    halves_in = dw_in.reshape(2, D // 2, 4, in_shard).transpose(0, 2, 1, 3).reshape(2, 4 * (D // 2), in_shard)
    halves_out = dw_out.reshape(4, 2, out_shard // 2, D).transpose(1, 0, 2, 3).reshape(2, 2 * out_shard, D)
    got_in, got_out = _pair_swap([halves_in, halves_out])
    own_in = lax.dynamic_index_in_dim(halves_in, ci, 0, keepdims=False)
    own_out = lax.dynamic_index_in_dim(halves_out, ci, 0, keepdims=False)
    part_in = _pair_sum_bf16(own_in, got_in, "pair_sum_in").reshape(4, D // 2, in_shard)
    part_out = _pair_sum_bf16(own_out, got_out, "pair_sum_out").reshape(4, out_shard // 2, D)
    slots_in, slots_out = _chip_exchange([part_in, part_out])
    half_in = _sum_slots(slots_in, "chip_sum_in")
    half_out = _sum_slots(slots_out, "chip_sum_out")
    full_in, full_out = _pair_share([half_in, half_out])
    grad_w_in = full_in.reshape(D, in_shard)
    grad_w_out = full_out.reshape(out_shard, D)

    small = jnp.concatenate([
        dg_norm[0:1], jnp.pad(db_f[0:1, :], ((0, 0), (0, D - LANES))), dg_qk[0:1], dg_qk[1:2],
        sg_conv[2:3], sg_conv[0:1], sg_conv[1:2], dconv_w], axis=0)
    n_small = small.shape[0]
    small_sum = _sum_slots(_gather_all(small), "small_sum", tr=n_small)
    grad_norm_g, grad_b_f = small_sum[0:1], small_sum[1:2, :H]
    grad_gq, grad_gk = small_sum[2:3].reshape(1, H, dh), small_sum[3:4].reshape(1, H, dh)
    grad_conv_b, grad_ln_g, grad_ln_b = small_sum[4:5], small_sum[5:6], small_sum[6:7]
    chip = 2 * xi + yi
    grad_conv_w = lax.dynamic_slice_in_dim(small_sum[7:7 + n_taps], chip * cw_shard, cw_shard, axis=1)

    d_in, nm_in, nv_in = _adamw(w_in[0], grad_w_in, m_w_in[0], v_w_in[0], "adamw_in")
    d_out, nm_out, nv_out = _adamw(w_out[0], grad_w_out, m_w_out[0], v_w_out[0], "adamw_out")
    d_cw, nm_cw, nv_cw = _adamw(conv_w[0], grad_conv_w, m_conv_w[0], v_conv_w[0], "adamw_conv_w")

    def rows(ws):
        return jnp.concatenate([jnp.pad(t.reshape(1, -1), ((0, 0), (0, D - t.size))) for t in ws], axis=0)

    small_w = [norm_g, b_forget, q_norm_g, k_norm_g, conv_b, conv_ln_g, conv_ln_b]
    small_m = [m_norm_g, m_b_forget, m_q_norm_g, m_k_norm_g, m_conv_b, m_conv_ln_g, m_conv_ln_b]
    small_v = [v_norm_g, v_b_forget, v_q_norm_g, v_k_norm_g, v_conv_b, v_conv_ln_g, v_conv_ln_b]
    d_s, nm_s, nv_s = _adamw(rows(small_w), small_sum[0:7], rows(small_m), rows(small_v), "adamw_small")

    def unpack(t):
        return [t[k:k + 1, :w.size].reshape(w.shape) for k, w in enumerate(small_w)]

    def order(s, in_, cw, out_):
        ng, bf, qg, kg, cb, lg, lb = s
        return [ng, in_[None], bf, qg, kg, cw[None], cb, lg, lb, out_[None]]

    grads = [grad_norm_g, grad_w_in[None], grad_b_f, grad_gq, grad_gk, grad_conv_w[None],
             grad_conv_b, grad_ln_g, grad_ln_b, grad_w_out[None]]
    return (loss, grad_x2.reshape(B, S, D), *grads,
            *order(unpack(d_s), d_in, d_cw, d_out),
            *order(unpack(nm_s), nm_in, nm_cw, nm_out),
            *order(unpack(nv_s), nv_in, nv_cw, nv_out))
```

```python
import functools

import jax
import jax.numpy as jnp
from jax import lax
from jax.experimental import pallas as pl
from jax.experimental.pallas import tpu as pltpu

F32 = jnp.float32
BF16 = jnp.bfloat16
SDS = jax.ShapeDtypeStruct
MESH = pl.DeviceIdType.MESH

EPS = 1e-6
NEG_INF = -1e30
LANES = 128
SUBLANES = 8
HEAD_DIM = 64
HALO = 32
VMEM_LIMIT = 56 * 1024 * 1024

L_ROWSUM = 64
L_KDECAY = 67
L_LSE = 70
L_D = 65

ADAM_LR = 0.001
ADAM_B1 = 0.9
ADAM_B2 = 0.999
ADAM_EPS = 1e-08
ADAM_WD = 0.01
ADAM_STEP = 10


def _params(sem, vmem=VMEM_LIMIT):
    return pltpu.CompilerParams(dimension_semantics=sem, vmem_limit_bytes=vmem)


def _sigmoid(x):
    return 1.0 / (1.0 + jnp.exp(-x))


def _split3(x):
    hi = x.astype(BF16).astype(F32)
    r = x - hi
    mid = r.astype(BF16).astype(F32)
    lo = (r - mid).astype(BF16).astype(F32)
    return hi, mid, lo


def _dot(a, b):
    return jnp.dot(a, b, preferred_element_type=F32)


def _dot_nt(a, b):
    return lax.dot_general(a, b, (((1,), (1,)), ((), ())), preferred_element_type=F32)


def _dot_tn(a, b):
    return lax.dot_general(a, b, (((0,), (0,)), ((), ())), preferred_element_type=F32)


def _lane(shape):
    return lax.broadcasted_iota(jnp.int32, shape, 1)


def _lane_col(x, lane, idx):
    return jnp.sum(jnp.where(lane == idx, x, 0.0), axis=-1, keepdims=True)


def _put3(base, lane, start, pieces):
    out = base
    for k, p in enumerate(pieces):
        out = jnp.where(lane == start + k, p, out)
    return out


def _half_stats(t, lo):
    s0 = jnp.sum(jnp.where(lo, t, 0.0), axis=-1, keepdims=True)
    s1 = jnp.sum(jnp.where(lo, 0.0, t), axis=-1, keepdims=True)
    return jnp.where(lo, s0, s1)


def _fwd_in(x2, g, w_all, tm, tn):
    T, D = x2.shape
    N = w_all.shape[1]

    def body(x_ref, g_ref, w_ref, z_ref, h_ref):
        @pl.when(pl.program_id(1) == 0)
        def _():
            x = x_ref[...]
            r = lax.rsqrt(jnp.mean(x * x, axis=-1, keepdims=True) + EPS)
            h_ref[...] = (x * r * g_ref[...]).astype(BF16)

        z_ref[...] = _dot(h_ref[...], w_ref[...])

    return pl.pallas_call(
        body, name="fwd_in", grid=(T // tm, N // tn),
        in_specs=[pl.BlockSpec((tm, D), lambda i, j: (i, 0)),
                  pl.BlockSpec((1, D), lambda i, j: (0, 0)),
                  pl.BlockSpec((D, tn), lambda i, j: (0, j))],
        out_specs=[pl.BlockSpec((tm, tn), lambda i, j: (i, j)),
                   pl.BlockSpec((tm, D), lambda i, j: (i, 0))],
        out_shape=[SDS((T, N), F32), SDS((T, D), BF16)],
        compiler_params=_params(("parallel", "arbitrary")),
    )(x2, g, w_all)


def _tri_cumsum(x, reverse):
    t = x.shape[0]
    row = lax.broadcasted_iota(jnp.int32, (t, t), 0)
    col = lax.broadcasted_iota(jnp.int32, (t, t), 1)
    tri = (row <= col) if reverse else (row >= col)
    tri = jnp.where(tri, 1.0, 0.0).astype(BF16)
    hi, mid, lo = _split3(x)
    return _dot(tri, hi.astype(BF16)) + _dot(tri, mid.astype(BF16)) + _dot(tri, lo.astype(BF16))


def _gate_fwd(z, b_pad, B, S, H, col_blk, tc):
    T = B * S
    nsb = S // tc

    def body(zf_ref, b_ref, c_ref, carry):
        @pl.when(pl.program_id(1) == 0)
        def _():
            carry[...] = jnp.zeros_like(carry)

        x = zf_ref[...] + b_ref[...]
        lf = jnp.minimum(x, 0.0) - jnp.log(1.0 + jnp.exp(-jnp.abs(x)))
        lf = jnp.where(_lane(lf.shape) < H, lf, 0.0)
        c_ref[...] = _tri_cumsum(lf, False) + carry[...]
        carry[...] = carry[...] + jnp.sum(lf, axis=0, keepdims=True)

    return pl.pallas_call(
        body, name="gate_fwd", grid=(B, nsb),
        in_specs=[pl.BlockSpec((tc, LANES), lambda b, s: (b * nsb + s, col_blk)),
                  pl.BlockSpec((1, LANES), lambda b, s: (0, 0))],
        out_specs=pl.BlockSpec((tc, LANES), lambda b, s: (b * nsb + s, 0)),
        out_shape=SDS((T, LANES), F32),
        scratch_shapes=[pltpu.VMEM((1, LANES), F32)],
        compiler_params=_params(("parallel", "arbitrary")),
    )(z, b_pad)


def _qk_normalize(x, g, lo):
    r = lax.rsqrt(_half_stats(x * x, lo) * (1.0 / HEAD_DIM) + EPS)
    return x * r * g


def _attn_prep(z, c, gq, gk, B, S, H, tm):
    T = B * S
    FW = H * HEAD_DIM
    nsb = S // tm
    nfb = FW // LANES
    scale = HEAD_DIM ** -0.5

    def body(zq_ref, zk_ref, zv_ref, c_ref, gq_ref, gk_ref, qa_ref, ka_ref, va_ref):
        p = pl.program_id(1)
        lane = _lane((tm, LANES))
        lo = lane < HEAD_DIM
        qn = _qk_normalize(zq_ref[...], gq_ref[...], lo) * scale
        kn = _qk_normalize(zk_ref[...], gk_ref[...], lo)
        v = zv_ref[...]
        cc = c_ref[...]
        ones_q = ((lane >= L_KDECAY) & (lane < L_KDECAY + 3)).astype(F32)
        ones_k = (((lane >= L_ROWSUM) & (lane < L_ROWSUM + 3)) | ((lane >= L_LSE) & (lane < L_LSE + 3))).astype(F32)
        ones_v = ((lane >= L_ROWSUM) & (lane < L_D + 3)).astype(F32)
        for e in range(2):
            if e == 0:
                qe, ke, ve = qn, kn, v
            else:
                qe, ke, ve = (pltpu.roll(t, HEAD_DIM, 1) for t in (qn, kn, v))
            ch = _lane_col(cc, lane, 2 * p + e)
            pieces = _split3(ch)
            qa = jnp.where(lo, qe, _put3(ones_q, lane, L_ROWSUM, pieces))
            ka = jnp.where(lo, ke, _put3(ones_k, lane, L_KDECAY, [-t for t in pieces]))
            va = jnp.where(lo, ve, ones_v)
            qa_ref[0, e] = qa.astype(BF16)
            ka_ref[0, e] = ka.astype(BF16)
            va_ref[0, e] = va.astype(BF16)

    zspec = lambda off: pl.BlockSpec((tm, LANES), lambda i, p: (i, off + p))
    gspec = pl.BlockSpec((1, LANES), lambda i, p: (0, p))
    ospec = pl.BlockSpec((1, 2, tm, LANES), lambda i, p: (i // nsb, p, i % nsb, 0))
    oshape = SDS((B, H, S, LANES), BF16)
    return pl.pallas_call(
        body, name="attn_prep", grid=(T // tm, H // 2),
        in_specs=[zspec(0), zspec(nfb), zspec(2 * nfb),
                  pl.BlockSpec((tm, LANES), lambda i, p: (i, 0)), gspec, gspec],
        out_specs=[ospec, ospec, ospec],
        out_shape=[oshape, oshape, oshape],
        compiler_params=_params(("parallel", "arbitrary")),
    )(z, z, z, c, gq, gk)


def _attn_fwd(qa, ka, va, t, hb):
    B, H, S, _ = qa.shape
    nq = S // t

    def body(q_ref, k_ref, v_ref, o_ref, m_ref, acc_ref):
        i = pl.program_id(2)
        m_ref[...] = jnp.full_like(m_ref, NEG_INF)
        acc_ref[...] = jnp.zeros_like(acc_ref)

        def step(j, masked):
            rows = pl.ds(pl.multiple_of(j * t, t), t)
            if masked:
                keep = lax.broadcasted_iota(jnp.int32, (t, t), 0) >= lax.broadcasted_iota(jnp.int32, (t, t), 1)
            for e in range(hb):
                s = _dot_nt(q_ref[0, e], k_ref[0, e, rows, :])
                if masked:
                    s = jnp.where(keep, s, NEG_INF)
                m_prev = m_ref[e]
                m_new = jnp.maximum(m_prev, jnp.max(s, axis=-1, keepdims=True))
                alpha = jnp.exp(m_prev - m_new)
                p = jnp.exp(s - m_new).astype(BF16)
                acc_ref[e] = alpha * acc_ref[e] + _dot(p, v_ref[0, e, rows, :])
                m_ref[e] = m_new

        def loop_body(j, carry):
            step(j, False)
            return carry

        lax.fori_loop(0, i, loop_body, 0)
        step(i, True)

        lane = _lane((t, LANES))
        for e in range(hb):
            acc = acc_ref[e]
            l = _lane_col(acc, lane, L_ROWSUM)
            o_ref[0, e] = jnp.where(lane < HEAD_DIM, acc / l, m_ref[e] + jnp.log(l))

    return pl.pallas_call(
        body, name="attn_fwd", grid=(B, H // hb, nq),
        in_specs=[pl.BlockSpec((1, hb, t, LANES), lambda b, h, i: (b, h, i, 0)),
                  pl.BlockSpec((1, hb, S, LANES), lambda b, h, i: (b, h, 0, 0)),
                  pl.BlockSpec((1, hb, S, LANES), lambda b, h, i: (b, h, 0, 0))],
        out_specs=pl.BlockSpec((1, hb, t, LANES), lambda b, h, i: (b, h, i, 0)),
        out_shape=SDS((B, H, S, LANES), F32),
        scratch_shapes=[pltpu.VMEM((hb, t, 1), F32), pltpu.VMEM((hb, t, LANES), F32)],
        compiler_params=_params(("parallel", "parallel", "arbitrary")),
    )(qa, ka, va)


def _fill_shifts(ext_ref, sh_ref):
    rows = sh_ref.shape[1]
    for b in range(1, SUBLANES):
        sh_ref[b - 1] = ext_ref[pl.ds(b, rows), :]


def _tap_window(ext_ref, sh_ref, off, tm, cols):
    b = off % SUBLANES
    if b == 0:
        return ext_ref[pl.ds(off, tm), cols]
    return sh_ref[b - 1, pl.ds(off - b, tm), cols]


def _conv_taps(w_ref, ext_ref, sh_ref, out_ref, n_taps, tm, offset_of, bias_ref=None):
    for cc in range(out_ref.shape[1] // LANES):
        cols = slice(cc * LANES, (cc + 1) * LANES)
        acc = None
        for j in range(n_taps):
            term = w_ref[j:j + 1, cols] * _tap_window(ext_ref, sh_ref, offset_of(j), tm, cols)
            acc = term if acc is None else acc + term
        out_ref[:, cols] = acc if bias_ref is None else acc + bias_ref[:, cols]


def _layernorm_stats(u2):
    mu = jnp.mean(u2, axis=-1, keepdims=True)
    xc = u2 - mu
    rstd = lax.rsqrt(jnp.mean(xc * xc, axis=-1, keepdims=True) + EPS)
    return xc * rstd, rstd


def _fwd_out(oa, z, x2, tgt, conv_w, conv_b, ln_g, ln_b, w_out, B, S, H, n_taps, tm):
    T, D = x2.shape
    FW = H * HEAD_DIM
    CW = conv_w.shape[1]
    nsb = S // tm
    hb = tm // HALO

    def body(oa_ref, gf_ref, ga_ref, gb_ref, gc_ref, ha_ref, hb_ref, x_ref, t_ref, w_ref, cb_ref, lg_ref,
             lb_ref, wo_ref, y_ref, u2_ref, a_ref, do_ref, dob_ref, dy_ref, loss_ref, ext_ref, sh_ref):
        first_step = (pl.program_id(0) == 0) & (pl.program_id(1) == 0)

        @pl.when(first_step)
        def _():
            loss_ref[...] = jnp.zeros_like(loss_ref)

        u1 = ga_ref[...] * _sigmoid(gb_ref[...])
        halo = ha_ref[...] * _sigmoid(hb_ref[...])
        ext_ref[0:HALO, :] = jnp.where(pl.program_id(1) > 0, halo, 0.0)
        ext_ref[HALO:, :] = u1
        _fill_shifts(ext_ref, sh_ref)
        _conv_taps(w_ref, ext_ref, sh_ref, u2_ref, n_taps, tm, lambda j: HALO - (n_taps - 1) + j, cb_ref)
        uh, _ = _layernorm_stats(u2_ref[...])
        u3 = uh * lg_ref[...] + lb_ref[...]
        gc = gc_ref[...]
        yu = u3 * _sigmoid(u3) * (gc * _sigmoid(gc))
        y_ref[:, FW:] = yu.astype(BF16)

        lane = _lane((tm, LANES))
        lo = lane < HEAD_DIM
        for p in range(H // 2):
            a_ref[:, p * LANES:(p + 1) * LANES] = jnp.where(
                lo, oa_ref[0, 2 * p], pltpu.roll(oa_ref[0, 2 * p + 1], HEAD_DIM, 1))
        gf = gf_ref[...]
        y_ref[:, :FW] = (a_ref[...] * (gf * _sigmoid(gf))).astype(BF16)

        out = x_ref[...] + _dot(y_ref[...], wo_ref[...])
        diff = out - t_ref[...]
        loss_ref[...] = loss_ref[...] + jnp.sum(diff * diff)
        dout = diff * (1.0 / D)
        do_ref[...] = dout
        dob = dout.astype(BF16)
        dob_ref[...] = dob
        dy_ref[...] = _dot_nt(dob, wo_ref[...])

    row = lambda b, s: b * nsb + s
    zspec = lambda cb: pl.BlockSpec((tm, FW), lambda b, s: (row(b, s), cb))
    hspec = lambda cb: pl.BlockSpec((HALO, CW), lambda b, s: (jnp.maximum(row(b, s) * hb - 1, 0), cb))
    vspec = pl.BlockSpec((1, CW), lambda b, s: (0, 0))
    tspec = lambda w: pl.BlockSpec((tm, w), lambda b, s: (row(b, s), 0))
    return pl.pallas_call(
        body, name="fwd_out", grid=(B, nsb),
        in_specs=[pl.BlockSpec((1, H, tm, LANES), lambda b, s: (b, 0, s, 0)),
                  zspec(3), zspec(4), zspec(5), zspec(6), hspec(4), hspec(5),
                  tspec(D), tspec(D),
                  pl.BlockSpec((HALO, CW), lambda b, s: (0, 0)), vspec, vspec, vspec,
                  pl.BlockSpec((FW + CW, D), lambda b, s: (0, 0))],
        out_specs=[tspec(FW + CW), tspec(CW), tspec(FW), tspec(D), tspec(D), tspec(FW + CW),
                   pl.BlockSpec((8, LANES), lambda b, s: (0, 0))],
        out_shape=[SDS((T, FW + CW), BF16), SDS((T, CW), F32), SDS((T, FW), F32), SDS((T, D), F32),
                   SDS((T, D), BF16), SDS((T, FW + CW), F32), SDS((8, LANES), F32)],
        scratch_shapes=[pltpu.VMEM((tm + HALO, CW), F32),
                        pltpu.VMEM((SUBLANES - 1, tm + HALO - SUBLANES, CW), F32)],
        compiler_params=_params(("arbitrary", "arbitrary")),
    )(oa, z, z, z, z, z, z, x2, tgt, conv_w, conv_b, ln_g, ln_b, w_out)


def _bwd_prep(dy, z, a_nat, oa, qa, u2, ln_g, ln_b, B, S, H, tm):
    T = B * S
    FW = H * HEAD_DIM
    CW = u2.shape[1]
    nsb = S // tm

    def body(dya_ref, dyu_ref, gf_ref, gc_ref, a_ref, oa_ref, qa_ref, u2_ref, lg_ref, lb_ref,
             dzgf_ref, dzgc_ref, du2_ref, doa_ref, qb_ref, sg_ref):
        first_step = (pl.program_id(0) == 0) & (pl.program_id(1) == 0)

        @pl.when(first_step)
        def _():
            sg_ref[...] = jnp.zeros_like(sg_ref)

        gf = gf_ref[...]
        sg = _sigmoid(gf)
        a = a_ref[...]
        dya = dya_ref[...]
        da = dya * (gf * sg)
        dzgf_ref[...] = (dya * a * (sg * (1.0 + gf * (1.0 - sg)))).astype(BF16)
        dd = da * a
        lane = _lane((tm, LANES))
        lo = lane < HEAD_DIM
        for p in range(H // 2):
            cols = slice(p * LANES, (p + 1) * LANES)
            da_p = da[:, cols]
            dd_p = dd[:, cols]
            d_heads = (jnp.sum(jnp.where(lo, dd_p, 0.0), axis=-1, keepdims=True),
                       jnp.sum(jnp.where(lo, 0.0, dd_p), axis=-1, keepdims=True))
            for e in range(2):
                da_e = da_p if e == 0 else pltpu.roll(da_p, HEAD_DIM, 1)
                d_e = d_heads[e]
                aug = _put3(jnp.zeros((tm, LANES), F32), lane, L_D, _split3(-d_e))
                doa_ref[0, 2 * p + e] = jnp.where(lo, da_e, aug).astype(BF16)
                lse = _lane_col(oa_ref[0, 2 * p + e], lane, L_ROWSUM)
                qb = _put3(qa_ref[0, 2 * p + e].astype(F32), lane, L_LSE, _split3(-lse))
                qb_ref[0, 2 * p + e] = qb.astype(BF16)

        gc = gc_ref[...]
        sc = _sigmoid(gc)
        dyu = dyu_ref[...]
        uh, rstd = _layernorm_stats(u2_ref[...])
        u3 = uh * lg_ref[...] + lb_ref[...]
        s3 = _sigmoid(u3)
        dzgc_ref[...] = (dyu * (u3 * s3) * (sc * (1.0 + gc * (1.0 - sc)))).astype(BF16)
        du3 = dyu * (gc * sc) * (s3 * (1.0 + u3 * (1.0 - s3)))
        sg_ref[0:1, :] = sg_ref[0:1, :] + jnp.sum(du3 * uh, axis=0, keepdims=True)
        sg_ref[1:2, :] = sg_ref[1:2, :] + jnp.sum(du3, axis=0, keepdims=True)
        duh = du3 * lg_ref[...]
        du2 = rstd * (duh - jnp.mean(duh, axis=-1, keepdims=True)
                      - uh * jnp.mean(duh * uh, axis=-1, keepdims=True))
        sg_ref[2:3, :] = sg_ref[2:3, :] + jnp.sum(du2, axis=0, keepdims=True)
        du2_ref[...] = du2

    row = lambda b, s: b * nsb + s
    tspec = lambda w, cb=0: pl.BlockSpec((tm, w), lambda b, s: (row(b, s), cb))
    hspec = pl.BlockSpec((1, H, tm, LANES), lambda b, s: (b, 0, s, 0))
    vspec = pl.BlockSpec((1, CW), lambda b, s: (0, 0))
    return pl.pallas_call(
        body, name="bwd_prep", grid=(B, nsb),
        in_specs=[tspec(FW, 0), tspec(CW, 1), tspec(FW, 3), tspec(CW, 6), tspec(FW), hspec, hspec,
                  tspec(CW), vspec, vspec],
        out_specs=[tspec(FW), tspec(CW), tspec(CW), hspec, hspec,
                   pl.BlockSpec((8, CW), lambda b, s: (0, 0))],
        out_shape=[SDS((T, FW), BF16), SDS((T, CW), BF16), SDS((T, CW), F32),
                   SDS((B, H, S, LANES), BF16), SDS((B, H, S, LANES), BF16), SDS((8, CW), F32)],
        compiler_params=_params(("arbitrary", "arbitrary")),
    )(dy, dy, z, z, a_nat, oa, qa, u2, ln_g, ln_b)


def _conv_bwd(du2, z, conv_w, B, S, n_taps, tm):
    T, CW = du2.shape
    nsb = S // tm
    hb = tm // HALO

    def body(d_ref, dh_ref, ga_ref, gb_ref, ha_ref, hb_ref, w_ref, dz_ref, dw_ref,
             extu_ref, extd_ref, shu_ref, shd_ref, du1_ref, dwacc_ref):
        s = pl.program_id(1)
        first_step = (pl.program_id(0) == 0) & (s == 0)
        last_step = (pl.program_id(0) == B - 1) & (s == nsb - 1)

        @pl.when(first_step)
        def _():
            dwacc_ref[...] = jnp.zeros_like(dwacc_ref)

        ga = ga_ref[...]
        sb = _sigmoid(gb_ref[...])
        halo = ha_ref[...] * _sigmoid(hb_ref[...])
        extu_ref[0:HALO, :] = jnp.where(s > 0, halo, 0.0)
        extu_ref[HALO:, :] = ga * sb
        extd_ref[0:tm, :] = d_ref[...]
        extd_ref[tm:, :] = jnp.where(s < nsb - 1, dh_ref[...], 0.0)
        _fill_shifts(extu_ref, shu_ref)
        _fill_shifts(extd_ref, shd_ref)
        _conv_taps(w_ref, extd_ref, shd_ref, du1_ref, n_taps, tm, lambda j: n_taps - 1 - j)
        for cc in range(CW // LANES):
            cols = slice(cc * LANES, (cc + 1) * LANES)
            dv = d_ref[:, cols]
            for j in range(n_taps):
                prod = dv * _tap_window(extu_ref, shu_ref, HALO - (n_taps - 1) + j, tm, cols)
                part = prod[0:SUBLANES]
                for r in range(1, tm // SUBLANES):
                    part = part + prod[r * SUBLANES:(r + 1) * SUBLANES]
                rows = slice(j * SUBLANES, (j + 1) * SUBLANES)
                dwacc_ref[rows, cols] = dwacc_ref[rows, cols] + part
        du1 = du1_ref[...]
        dz_ref[:, :CW] = (du1 * sb).astype(BF16)
        dz_ref[:, CW:] = (du1 * ga * (sb * (1.0 - sb))).astype(BF16)

        @pl.when(last_step)
        def _():
            dw_ref[...] = jnp.zeros_like(dw_ref)
            for j in range(n_taps):
                dw_ref[j:j + 1, :] = jnp.sum(dwacc_ref[j * SUBLANES:(j + 1) * SUBLANES, :], axis=0, keepdims=True)

    row = lambda b, s: b * nsb + s
    last_halo = T // HALO - 1
    return pl.pallas_call(
        body, name="conv_bwd", grid=(B, nsb),
        in_specs=[pl.BlockSpec((tm, CW), lambda b, s: (row(b, s), 0)),
                  pl.BlockSpec((HALO, CW), lambda b, s: (jnp.minimum((row(b, s) + 1) * hb, last_halo), 0)),
                  pl.BlockSpec((tm, CW), lambda b, s: (row(b, s), 4)),
                  pl.BlockSpec((tm, CW), lambda b, s: (row(b, s), 5)),
                  pl.BlockSpec((HALO, CW), lambda b, s: (jnp.maximum(row(b, s) * hb - 1, 0), 4)),
                  pl.BlockSpec((HALO, CW), lambda b, s: (jnp.maximum(row(b, s) * hb - 1, 0), 5)),
                  pl.BlockSpec((HALO, CW), lambda b, s: (0, 0))],
        out_specs=[pl.BlockSpec((tm, 2 * CW), lambda b, s: (row(b, s), 0)),
                   pl.BlockSpec((HALO, CW), lambda b, s: (0, 0))],
        out_shape=[SDS((T, 2 * CW), BF16), SDS((HALO, CW), F32)],
        scratch_shapes=[pltpu.VMEM((tm + HALO, CW), F32), pltpu.VMEM((tm + HALO, CW), F32),
                        pltpu.VMEM((SUBLANES - 1, tm + HALO - SUBLANES, CW), F32),
                        pltpu.VMEM((SUBLANES - 1, tm + HALO - SUBLANES, CW), F32),
                        pltpu.VMEM((tm, CW), F32), pltpu.VMEM((HALO * SUBLANES, CW), F32)],
        compiler_params=_params(("arbitrary", "arbitrary")),
    )(du2, du2, z, z, z, z, conv_w)


def _attn_bwd(qb, ka, va, doa, t, hb):
    B, H, S, _ = qb.shape
    nk = S // t

    def body(q_ref, k_ref, v_ref, do_ref, dq_ref, dk_ref, dv_ref):
        j = pl.program_id(2)

        @pl.when(j == 0)
        def _():
            dq_ref[...] = jnp.zeros_like(dq_ref)

        dk_ref[...] = jnp.zeros_like(dk_ref)
        dv_ref[...] = jnp.zeros_like(dv_ref)

        def step(i, masked):
            rows = pl.ds(pl.multiple_of(i * t, t), t)
            if masked:
                keep = lax.broadcasted_iota(jnp.int32, (t, t), 0) >= lax.broadcasted_iota(jnp.int32, (t, t), 1)
            for e in range(hb):
                k = k_ref[0, e]
                q = q_ref[0, e, rows, :]
                do = do_ref[0, e, rows, :]
                p = jnp.exp(_dot_nt(q, k))
                if masked:
                    p = jnp.where(keep, p, 0.0)
                ds = (p * _dot_nt(do, v_ref[0, e])).astype(BF16)
                dv_ref[0, e] = dv_ref[0, e] + _dot_tn(p.astype(BF16), do)
                dk_ref[0, e] = dk_ref[0, e] + _dot_tn(ds, q)
                dq_ref[0, e, rows, :] = dq_ref[0, e, rows, :] + _dot(ds, k)

        step(j, True)

        def loop_body(i, carry):
            step(i, False)
            return carry

        lax.fori_loop(j + 1, nk, loop_body, 0)

    full = pl.BlockSpec((1, hb, S, LANES), lambda b, h, j: (b, h, 0, 0))
    blk = pl.BlockSpec((1, hb, t, LANES), lambda b, h, j: (b, h, j, 0))
    oshape = SDS((B, H, S, LANES), F32)
    return pl.pallas_call(
        body, name="attn_bwd", grid=(B, H // hb, nk),
        in_specs=[full, blk, blk, full],
        out_specs=[full, blk, blk],
        out_shape=[oshape, oshape, oshape],
        compiler_params=_params(("parallel", "parallel", "arbitrary")),
    )(qb, ka, va, doa)


def _qk_bwd(dqa, dka, dva, z, gq, gk, B, S, H, tm):
    T = B * S
    FW = H * HEAD_DIM
    nsb = S // tm
    nfb = FW // LANES
    scale = HEAD_DIM ** -0.5

    def body(dq_ref, dk_ref, dv_ref, zq_ref, zk_ref, gq_ref, gk_ref, dzq_ref, dzk_ref, dzv_ref, dc_ref, dg_ref):
        p = pl.program_id(0)

        @pl.when(pl.program_id(1) == 0)
        def _():
            dg_ref[...] = jnp.zeros_like(dg_ref)

        lane = _lane((tm, LANES))
        lo = lane < HEAD_DIM

        def natural(ref):
            return jnp.where(lo, ref[0, 0], pltpu.roll(ref[0, 1], HEAD_DIM, 1))

        def norm_bwd(dn, x, g, row, out_ref):
            r = lax.rsqrt(_half_stats(x * x, lo) * (1.0 / HEAD_DIM) + EPS)
            xh = x * r
            dg_ref[row:row + 1, :] = dg_ref[row:row + 1, :] + jnp.sum(dn * xh, axis=0, keepdims=True)
            dxh = dn * g
            mm = _half_stats(dxh * xh, lo) * (1.0 / HEAD_DIM)
            out_ref[...] = (r * (dxh - xh * mm)).astype(BF16)

        norm_bwd(natural(dq_ref) * scale, zq_ref[...], gq_ref[...], 0, dzq_ref)
        norm_bwd(natural(dk_ref), zk_ref[...], gk_ref[...], 1, dzk_ref)
        dzv_ref[...] = natural(dv_ref).astype(BF16)

        dc = jnp.zeros((tm, LANES), F32)
        for e in range(2):
            val = _lane_col(dq_ref[0, e], lane, L_ROWSUM) - _lane_col(dk_ref[0, e], lane, L_KDECAY)
            dc = jnp.where(lane == 2 * p + e, val, dc)
        dc_ref[0] = dc

    hspec = pl.BlockSpec((1, 2, tm, LANES), lambda p, i: (i // nsb, p, i % nsb, 0))
    zspec = lambda off: pl.BlockSpec((tm, LANES), lambda p, i: (i, off + p))
    gspec = pl.BlockSpec((1, LANES), lambda p, i: (0, p))
    ospec = pl.BlockSpec((tm, LANES), lambda p, i: (i, p))
    return pl.pallas_call(
        body, name="qk_bwd", grid=(H // 2, T // tm),
        in_specs=[hspec, hspec, hspec, zspec(0), zspec(nfb), gspec, gspec],
        out_specs=[ospec, ospec, ospec,
                   pl.BlockSpec((1, tm, LANES), lambda p, i: (p, i, 0)),
                   pl.BlockSpec((8, LANES), lambda p, i: (0, p))],
        out_shape=[SDS((T, FW), BF16), SDS((T, FW), BF16), SDS((T, FW), BF16),
                   SDS((H // 2, T, LANES), F32), SDS((8, FW), F32)],
        compiler_params=_params(("parallel", "arbitrary")),
    )(dqa, dka, dva, z, z, gq, gk)


def _gate_bwd(dc8, z, b_pad, B, S, H, col_blk, fp, tc):
    T = B * S
    nsb = S // tc
    npair = dc8.shape[0]

    def body(dc_ref, zf_ref, b_ref, dz_ref, db_ref, carry):
        first_step = (pl.program_id(0) == 0) & (pl.program_id(1) == 0)

        @pl.when(first_step)
        def _():
            db_ref[...] = jnp.zeros_like(db_ref)

        @pl.when(pl.program_id(1) == 0)
        def _():
            carry[...] = jnp.zeros_like(carry)

        dc = dc_ref[0]
        for k in range(1, npair):
            dc = dc + dc_ref[k]
        dlf = _tri_cumsum(dc, True) + carry[...]
        carry[...] = carry[...] + jnp.sum(dc, axis=0, keepdims=True)
        x = zf_ref[...] + b_ref[...]
        dlogit = dlf * _sigmoid(-x)
        db_ref[0:1, :] = db_ref[0:1, :] + jnp.sum(dlogit, axis=0, keepdims=True)
        dz_ref[...] = jnp.zeros_like(dz_ref)
        dz_ref[:, :LANES] = dlogit.astype(BF16)

    rrow = lambda b, s: b * nsb + (nsb - 1 - s)
    return pl.pallas_call(
        body, name="gate_bwd", grid=(B, nsb),
        in_specs=[pl.BlockSpec((npair, tc, LANES), lambda b, s: (0, rrow(b, s), 0)),
                  pl.BlockSpec((tc, LANES), lambda b, s: (rrow(b, s), col_blk)),
                  pl.BlockSpec((1, LANES), lambda b, s: (0, 0))],
        out_specs=[pl.BlockSpec((tc, fp), lambda b, s: (rrow(b, s), 0)),
                   pl.BlockSpec((8, LANES), lambda b, s: (0, 0))],
        out_shape=[SDS((T, fp), BF16), SDS((8, LANES), F32)],
        scratch_shapes=[pltpu.VMEM((1, LANES), F32)],
        compiler_params=_params(("arbitrary", "arbitrary")),
    )(dc8, z, b_pad)


def _matmul_tn(a, b, name, tmm, tn, tk):
    T, M = a.shape
    N = b.shape[1]
    tmm, tn, tk = min(tmm, M), min(tn, N), min(tk, T)

    def body(a_ref, b_ref, o_ref):
        @pl.when(pl.program_id(2) == 0)
        def _():
            o_ref[...] = jnp.zeros_like(o_ref)

        o_ref[...] = o_ref[...] + _dot_tn(a_ref[...], b_ref[...])

    return pl.pallas_call(
        body, name=name, grid=(M // tmm, N // tn, T // tk),
        in_specs=[pl.BlockSpec((tk, tmm), lambda i, j, k: (k, i)),
                  pl.BlockSpec((tk, tn), lambda i, j, k: (k, j))],
        out_specs=pl.BlockSpec((tmm, tn), lambda i, j, k: (i, j)),
        out_shape=SDS((M, N), F32),
        compiler_params=_params(("parallel", "parallel", "arbitrary")),
    )(a, b)


def _dh_rms_bwd(pieces, w_all, x2, g, dout, tm, tk):
    T, D = x2.shape
    nks = [p.shape[1] // tk for p in pieces]
    starts = [sum(nks[:k]) for k in range(len(pieces))]
    nk = sum(nks)

    def body(*refs):
        dz_refs = refs[:len(pieces)]
        w_ref, x_ref, g_ref, do_ref, gx_ref, dg_ref, acc_ref = refs[len(pieces):]
        k = pl.program_id(1)
        first_step = (pl.program_id(0) == 0) & (k == 0)

        @pl.when(first_step)
        def _():
            dg_ref[...] = jnp.zeros_like(dg_ref)

        @pl.when(k == 0)
        def _():
            acc_ref[...] = jnp.zeros_like(acc_ref)

        for dz_ref, st, n in zip(dz_refs, starts, nks):
            @pl.when((k >= st) & (k < st + n))
            def _(dz_ref=dz_ref):
                acc_ref[...] = acc_ref[...] + _dot_nt(dz_ref[...], w_ref[...])

        @pl.when(k == nk - 1)
        def _():
            x = x_ref[...]
            r = lax.rsqrt(jnp.mean(x * x, axis=-1, keepdims=True) + EPS)
            xh = x * r
            dh = acc_ref[...]
            dg_ref[0:1, :] = dg_ref[0:1, :] + jnp.sum(dh * xh, axis=0, keepdims=True)
            dxn = dh * g_ref[...]
            gx_ref[...] = do_ref[...] + r * (dxn - xh * jnp.mean(dxn * xh, axis=-1, keepdims=True))

    def piece_spec(st, n):
        return pl.BlockSpec((tm, tk), lambda i, k: (i, jnp.clip(k - st, 0, n - 1)))

    tspec = pl.BlockSpec((tm, D), lambda i, k: (i, 0))
    return pl.pallas_call(
        body, name="dh_rms_bwd", grid=(T // tm, nk),
        in_specs=[piece_spec(st, n) for st, n in zip(starts, nks)]
        + [pl.BlockSpec((D, tk), lambda i, k: (0, k)), tspec, pl.BlockSpec((1, D), lambda i, k: (0, 0)), tspec],
        out_specs=[tspec, pl.BlockSpec((8, D), lambda i, k: (0, 0))],
        out_shape=[SDS((T, D), F32), SDS((8, D), F32)],
        scratch_shapes=[pltpu.VMEM((tm, D), F32)],
        compiler_params=_params(("arbitrary", "arbitrary")),
    )(*pieces, w_all, x2, g, dout)


def _rows_call(body, name, ins, n_out, out_dtypes, tr):
    lead = ins[0].shape[:-2]
    R, C = ins[0].shape[-2:]
    assert all(d == 1 for d in lead)
    tr = min(tr, R)
    spec = pl.BlockSpec(lead + (tr, C), lambda i: (0,) * len(lead) + (i, 0))
    return pl.pallas_call(
        body, name=name, grid=(pl.cdiv(R, tr),),
        in_specs=[spec] * len(ins), out_specs=[spec] * n_out,
        out_shape=[SDS(lead + (R, C), dt) for dt in out_dtypes],
        compiler_params=_params(("parallel",)),
    )(*ins)


def _pair_sum_bf16(a, b, name):
    def body(a_ref, b_ref, o_ref):
        o_ref[...] = (a_ref[...] + b_ref[...]).astype(BF16)

    return _rows_call(body, name, [a, b], 1, [BF16], 256)[0]


def _sum_slots(slots, name, first=None, tr=256):
    n, R, C = slots.shape
    tr = min(tr, R)
    lead = [] if first is None else [first]

    def body(*refs):
        s_ref, o_ref = refs[-2:]
        acc = refs[0][...].astype(F32) if lead else s_ref[0].astype(F32)
        for k in range(0 if lead else 1, n):
            acc = acc + s_ref[k].astype(F32)
        o_ref[...] = acc

    return pl.pallas_call(
        body, name=name, grid=(pl.cdiv(R, tr),),
        in_specs=[pl.BlockSpec((tr, C), lambda i: (i, 0))] * len(lead) + [pl.BlockSpec((n, tr, C), lambda i: (0, i, 0))],
        out_specs=pl.BlockSpec((tr, C), lambda i: (i, 0)),
        out_shape=SDS((R, C), F32),
        compiler_params=_params(("parallel",)),
    )(*lead, slots)


def _adamw(w, g, m, v, name):
    def body(w_ref, g_ref, m_ref, v_ref, d_ref, nm_ref, nv_ref):
        gg = g_ref[...]
        nm = ADAM_B1 * m_ref[...] + (1.0 - ADAM_B1) * gg
        nv = ADAM_B2 * v_ref[...] + (1.0 - ADAM_B2) * (gg * gg)
        m_hat = nm / (1.0 - ADAM_B1 ** ADAM_STEP)
        v_hat = nv / (1.0 - ADAM_B2 ** ADAM_STEP)
        d_ref[...] = -ADAM_LR * (m_hat / (jnp.sqrt(v_hat) + ADAM_EPS) + ADAM_WD * w_ref[...])
        nm_ref[...] = nm
        nv_ref[...] = nv

    return _rows_call(body, name, [w, g, m, v], 3, [F32, F32, F32], 128)


ANY = pl.BlockSpec(memory_space=pl.ANY)


def _place():
    return lax.axis_index("x"), lax.axis_index("y"), lax.axis_index("c")


def _gather_chips(shards):
    n = len(shards)
    per = 7

    def body(*refs):
        ins, outs = refs[:n], refs[n:2 * n]
        send_sems, recv_sems = refs[2 * n:]
        x, y, c = _place()
        mine = 2 * x + y
        me, sibling = (x, y, c), (x, y, 1 - c)
        chips = [(1 - x, y), (x, 1 - y), (1 - x, 1 - y)]

        def copy(a, k, chip_idx, half, to, src=None):
            dst = outs[a].at[chip_idx, half]
            return pltpu.make_async_remote_copy(
                src_ref=dst if src is None else src, dst_ref=dst,
                send_sem=send_sems.at[a * per + k], recv_sem=recv_sems.at[a * per + k],
                device_id=to, device_id_type=MESH)

        def own(a, to):
            return pltpu.make_async_remote_copy(
                src_ref=ins[a], dst_ref=outs[a].at[mine],
                send_sem=send_sems.at[a * per + 6], recv_sem=recv_sems.at[a * per + 6],
                device_id=to, device_id_type=MESH)

        first = [copy(a, k, mine, c, (*chip, c), src=ins[a].at[c]) for a in range(n) for k, chip in enumerate(chips)]
        first += [own(a, sibling) for a in range(n)]
        for cp in first:
            cp.start()
        passed = []
        for k, (cx, cy) in enumerate(chips):
            for a in range(n):
                copy(a, k, 2 * cx + cy, c, me).wait_recv()
                fwd = copy(a, 3 + k, 2 * cx + cy, c, sibling)
                fwd.start()
                passed.append(fwd)
        for k, (cx, cy) in enumerate(chips):
            for a in range(n):
                copy(a, 3 + k, 2 * cx + cy, 1 - c, me).wait_recv()
        for a in range(n):
            own(a, me).wait_recv()
        for cp in first + passed:
            cp.wait_send()

    return pl.pallas_call(
        body, name="gather_chips",
        in_specs=[ANY] * n, out_specs=[ANY] * n,
        out_shape=[SDS((4,) + s.shape, s.dtype) for s in shards],
        scratch_shapes=[pltpu.SemaphoreType.DMA((per * n,)), pltpu.SemaphoreType.DMA((per * n,))],
    )(*shards)


def _pair_swap(halves):
    n = len(halves)

    def body(*refs):
        ins, outs = refs[:n], refs[n:2 * n]
        send_sems, recv_sems = refs[2 * n:]
        x, y, c = _place()
        copies = [pltpu.make_async_remote_copy(
            src_ref=ins[a].at[1 - c], dst_ref=outs[a], send_sem=send_sems.at[a], recv_sem=recv_sems.at[a],
            device_id=(x, y, 1 - c), device_id_type=MESH) for a in range(n)]
        for cp in copies:
            cp.start()
        for cp in copies:
            cp.wait()

    return pl.pallas_call(
        body, name="pair_swap",
        in_specs=[ANY] * n, out_specs=[ANY] * n,
        out_shape=[SDS(h.shape[1:], h.dtype) for h in halves],
        scratch_shapes=[pltpu.SemaphoreType.DMA((n,)), pltpu.SemaphoreType.DMA((n,))],
    )(*halves)


def _chip_exchange(parts):
    n = len(parts)

    def body(*refs):
        ins, outs = refs[:n], refs[n:2 * n]
        send_sems, recv_sems = refs[2 * n:]
        x, y, c = _place()
        chips = [(1 - x, y), (x, 1 - y), (1 - x, 1 - y)]

        def copy(a, k, to):
            cx, cy = chips[k]
            return pltpu.make_async_remote_copy(
                src_ref=ins[a].at[2 * cx + cy], dst_ref=outs[a].at[k],
                send_sem=send_sems.at[a * 3 + k], recv_sem=recv_sems.at[a * 3 + k],
                device_id=to, device_id_type=MESH)

        sends = [copy(a, k, (*chips[k], c)) for a in range(n) for k in range(3)]
        for cp in sends:
            cp.start()
        for a in range(n):
            for k in range(3):
                copy(a, k, (x, y, c)).wait_recv()
        for cp in sends:
            cp.wait_send()

    return pl.pallas_call(
        body, name="chip_exchange",
        in_specs=[ANY] * n, out_specs=[ANY] * n,
        out_shape=[SDS((3,) + p.shape[1:], p.dtype) for p in parts],
        scratch_shapes=[pltpu.SemaphoreType.DMA((3 * n,)), pltpu.SemaphoreType.DMA((3 * n,))],
    )(*parts)


def _pair_send(arrs):
    n = len(arrs)

    def body(*refs):
        ins, outs = refs[:n], refs[n:2 * n]
        send_sems, recv_sems = refs[2 * n:]
        x, y, c = _place()
        copies = [pltpu.make_async_remote_copy(
            src_ref=ins[a], dst_ref=outs[a], send_sem=send_sems.at[a], recv_sem=recv_sems.at[a],
            device_id=(x, y, 1 - c), device_id_type=MESH) for a in range(n)]
        for cp in copies:
            cp.start()
        for cp in copies:
            cp.wait()

    return pl.pallas_call(
        body, name="pair_send",
        in_specs=[ANY] * n, out_specs=[ANY] * n,
        out_shape=[SDS(h.shape, h.dtype) for h in arrs],
        scratch_shapes=[pltpu.SemaphoreType.DMA((n,)), pltpu.SemaphoreType.DMA((n,))],
    )(*arrs)


def _gather_all(buf):
    flips = [(fx, fy, fc) for fx in (0, 1) for fy in (0, 1) for fc in (0, 1)][1:]

    def body(in_ref, out_ref, send_sems, recv_sems, local_sem):
        x, y, c = _place()
        me = 4 * x + 2 * y + c
        local = pltpu.make_async_copy(in_ref, out_ref.at[me], local_sem)
        local.start()
        sends = []
        for k, (fx, fy, fc) in enumerate(flips):
            cp = pltpu.make_async_remote_copy(
                src_ref=in_ref, dst_ref=out_ref.at[me], send_sem=send_sems.at[k], recv_sem=recv_sems.at[k],
                device_id=(x ^ fx, y ^ fy, c ^ fc), device_id_type=MESH)
            cp.start()
            sends.append(cp)
        for k, (fx, fy, fc) in enumerate(flips):
            src = 4 * (x ^ fx) + 2 * (y ^ fy) + (c ^ fc)
            pltpu.make_async_remote_copy(
                src_ref=in_ref, dst_ref=out_ref.at[src], send_sem=send_sems.at[k], recv_sem=recv_sems.at[k],
                device_id=(x, y, c), device_id_type=MESH).wait_recv()
        for cp in sends:
            cp.wait_send()
        local.wait()

    return pl.pallas_call(
        body, name="gather_all",
        in_specs=[ANY], out_specs=ANY,
        out_shape=SDS((8,) + buf.shape, buf.dtype),
        scratch_shapes=[pltpu.SemaphoreType.DMA((7,)), pltpu.SemaphoreType.DMA((7,)), pltpu.SemaphoreType.DMA],
    )(buf)


def _tiles(S, FW):
    big = FW % 512 == 0
    return dict(
        fp=512 if big else LANES,
        tn=512 if big else LANES,
        tm_in=min(1024, S),
        t_attn=min(512, S),
        hb_fwd=4,
        hb_bwd=2,
        tm_prep=min(512, S),
        tm_mix=min(128, S),
        tc=min(256, S),
        tk=512 if big else LANES,
    )


def kernel(x, norm_g, w_in, b_forget, q_norm_g, k_norm_g, conv_w, conv_b, conv_ln_g, conv_ln_b, w_out, loss_target, m_norm_g, m_w_in, m_b_forget, m_q_norm_g, m_k_norm_g, m_conv_w, m_conv_b, m_conv_ln_g, m_conv_ln_b, m_w_out, v_norm_g, v_w_in, v_b_forget, v_q_norm_g, v_k_norm_g, v_conv_w, v_conv_b, v_conv_ln_g, v_conv_ln_b, v_w_out):
    B, S, D = x.shape
    H, dh = q_norm_g.shape[1:]
    FW = H * dh
    CW = conv_b.shape[-1]
    n_taps, cw_shard = conv_w.shape[1:]
    in_shard = w_in.shape[2]
    out_shard = w_out.shape[1]
    assert dh == HEAD_DIM and H % 2 == 0 and H <= LANES and FW == CW == D
    assert n_taps - 1 <= HALO and 4 * cw_shard == CW and 4 * out_shard == FW + CW
    assert 4 * in_shard == 4 * FW + 3 * CW + H
    T = B * S
    tl = _tiles(S, FW)
    fp = tl["fp"]
    xi, yi, ci = _place()

    conv_pad = jnp.pad(conv_w[0], ((0, HALO - n_taps), (0, 0)))
    g_in, g_out, g_cw = _gather_chips([
        w_in[0].astype(BF16).reshape(2, D // 2, in_shard),
        w_out[0].astype(BF16).reshape(2, out_shard // 2, D),
        conv_pad.reshape(2, HALO // 2, cw_shard)])
    w_in_full = g_in.reshape(4, D, in_shard).transpose(1, 0, 2).reshape(D, 4 * in_shard)
    w_out_full = g_out.reshape(FW + CW, D)
    conv_full = g_cw.reshape(4, HALO, cw_shard).transpose(1, 0, 2).reshape(HALO, CW)
    o_f = 3 * FW
    w_all = jnp.concatenate([w_in_full[:, :o_f], w_in_full[:, o_f + H:],
                             jnp.pad(w_in_full[:, o_f:o_f + H], ((0, 0), (0, fp - H)))], axis=1)
    f_col = 4 * FW + 3 * CW

    x2 = x.reshape(T, D)
    tgt = loss_target.reshape(T, D)
    b_pad = jnp.pad(b_forget, ((0, 0), (0, LANES - H)))
    gq = q_norm_g.reshape(1, FW)
    gk = k_norm_g.reshape(1, FW)

    z, h = _fwd_in(x2, norm_g, w_all, tl["tm_in"], tl["tn"])
    c = _gate_fwd(z, b_pad, B, S, H, f_col // LANES, tl["tc"])
    qa, ka, va = _attn_prep(z, c, gq, gk, B, S, H, tl["tm_prep"])
    oa = _attn_fwd(qa, ka, va, tl["t_attn"], tl["hb_fwd"])
    y, u2, a_nat, dout, dout_b, dy, loss_acc = _fwd_out(
        oa, z, x2, tgt, conv_full, conv_b, conv_ln_g, conv_ln_b, w_out_full, B, S, H, n_taps, tl["tm_mix"])
    loss = lax.psum(0.5 * loss_acc[0, 0] / D, ("x", "y", "c"))

    dzgf, dzgc, du2, doa, qb, sg_conv = _bwd_prep(dy, z, a_nat, oa, qa, u2, conv_ln_g, conv_ln_b, B, S, H, tl["tm_mix"])
    dzglu, dconv_w = _conv_bwd(du2, z, conv_full, B, S, n_taps, tl["tm_mix"])
    dqa, dka, dva = _attn_bwd(qb, ka, va, doa, tl["t_attn"], tl["hb_bwd"])
    dzq, dzk, dzv, dc8, dg_qk = _qk_bwd(dqa, dka, dva, z, gq, gk, B, S, H, tl["tm_prep"])
    dzf, db_f = _gate_bwd(dc8, z, b_pad, B, S, H, f_col // LANES, fp, tl["tc"])
    pieces = [dzq, dzk, dzv, dzgf, dzglu, dzgc, dzf]
    grad_x2, dg_norm = _dh_rms_bwd(pieces, w_all, x2, norm_g, dout, tl["tm_in"], tl["tk"])
    dw_pieces = [_matmul_tn(h, p, f"dw_in_{k}", 1024, 1024, 512) for k, p in enumerate(pieces)]
    dw_out = _matmul_tn(y, dout_b, "dw_out", 1024, 1024, 512)
    dw_in = jnp.concatenate(dw_pieces[:3] + [dw_pieces[6][:, :H]] + dw_pieces[3:6], axis=1)

    halves_in = dw_in.reshape(2, D // 2, 4, in_shard).transpose(0, 2, 1, 3).reshape(2, 4 * (D // 2), in_shard)
    halves_out = dw_out.reshape(4, 2, out_shard // 2, D).transpose(1, 0, 2, 3).reshape(2, 2 * out_shard, D)
    got_in, got_out = _pair_swap([halves_in, halves_out])
    own_in = lax.dynamic_index_in_dim(halves_in, ci, 0, keepdims=False)
    own_out = lax.dynamic_index_in_dim(halves_out, ci, 0, keepdims=False)
    part_in = _pair_sum_bf16(own_in, got_in, "pair_sum_in").reshape(4, D // 2, in_shard)
    part_out = _pair_sum_bf16(own_out, got_out, "pair_sum_out").reshape(4, out_shard // 2, D)
    slots_in, slots_out = _chip_exchange([part_in, part_out])
    chip = 2 * xi + yi
    half_in = _sum_slots(slots_in, "chip_sum_in", lax.dynamic_index_in_dim(part_in, chip, 0, keepdims=False))
    half_out = _sum_slots(slots_out, "chip_sum_out", lax.dynamic_index_in_dim(part_out, chip, 0, keepdims=False))
    other_in, other_out = _pair_send([half_in, half_out])

    def both_halves(mine, other):
        return jnp.where(ci == 0, jnp.concatenate([mine, other], axis=0), jnp.concatenate([other, mine], axis=0))

    grad_w_in = both_halves(half_in, other_in)
    grad_w_out = both_halves(half_out, other_out)

    small = jnp.concatenate([
        dg_norm[0:1], jnp.pad(db_f[0:1, :], ((0, 0), (0, D - LANES))), dg_qk[0:1], dg_qk[1:2],
        sg_conv[2:3], sg_conv[0:1], sg_conv[1:2], dconv_w], axis=0)
    n_small = small.shape[0]
    small_sum = _sum_slots(_gather_all(small), "small_sum", tr=n_small)
    grad_norm_g, grad_b_f = small_sum[0:1], small_sum[1:2, :H]
    grad_gq, grad_gk = small_sum[2:3].reshape(1, H, dh), small_sum[3:4].reshape(1, H, dh)
    grad_conv_b, grad_ln_g, grad_ln_b = small_sum[4:5], small_sum[5:6], small_sum[6:7]
    grad_conv_w = lax.dynamic_slice_in_dim(small_sum[7:7 + n_taps], chip * cw_shard, cw_shard, axis=1)

    d_in, nm_in, nv_in = _adamw(w_in, grad_w_in[None], m_w_in, v_w_in, "adamw_in")
    d_out, nm_out, nv_out = _adamw(w_out, grad_w_out[None], m_w_out, v_w_out, "adamw_out")
    d_cw, nm_cw, nv_cw = _adamw(conv_w, grad_conv_w[None], m_conv_w, v_conv_w, "adamw_conv_w")

    def rows(ws):
        return jnp.concatenate([jnp.pad(t.reshape(1, -1), ((0, 0), (0, D - t.size))) for t in ws], axis=0)

    small_w = [norm_g, b_forget, q_norm_g, k_norm_g, conv_b, conv_ln_g, conv_ln_b]
    small_m = [m_norm_g, m_b_forget, m_q_norm_g, m_k_norm_g, m_conv_b, m_conv_ln_g, m_conv_ln_b]
    small_v = [v_norm_g, v_b_forget, v_q_norm_g, v_k_norm_g, v_conv_b, v_conv_ln_g, v_conv_ln_b]
    d_s, nm_s, nv_s = _adamw(rows(small_w), small_sum[0:7], rows(small_m), rows(small_v), "adamw_small")

    def unpack(t):
        return [t[k:k + 1, :w.size].reshape(w.shape) for k, w in enumerate(small_w)]

    def order(s, in_, cw, out_):
        ng, bf, qg, kg, cb, lg, lb = s
        return [ng, in_, bf, qg, kg, cw, cb, lg, lb, out_]

    grads = [grad_norm_g, grad_w_in[None], grad_b_f, grad_gq, grad_gk, grad_conv_w[None],
             grad_conv_b, grad_ln_g, grad_ln_b, grad_w_out[None]]
    return (loss, grad_x2.reshape(B, S, D), *grads,
            *order(unpack(d_s), d_in, d_cw, d_out),
            *order(unpack(nm_s), nm_in, nm_cw, nm_out),
            *order(unpack(nv_s), nv_in, nv_cw, nv_out))
```

```python
import functools

import jax
import jax.numpy as jnp
from jax import lax
from jax.experimental import pallas as pl
from jax.experimental.pallas import tpu as pltpu

F32 = jnp.float32
BF16 = jnp.bfloat16
SDS = jax.ShapeDtypeStruct
MESH = pl.DeviceIdType.MESH

EPS = 1e-6
NEG_INF = -1e30
LANES = 128
SUBLANES = 8
HEAD_DIM = 64
HALO = 32
VMEM_LIMIT = 56 * 1024 * 1024

L_ROWSUM = 64
L_KDECAY = 67
L_LSE = 70
L_D = 65

ADAM_LR = 0.001
ADAM_B1 = 0.9
ADAM_B2 = 0.999
ADAM_EPS = 1e-08
ADAM_WD = 0.01
ADAM_STEP = 10


def _params(sem, vmem=VMEM_LIMIT):
    return pltpu.CompilerParams(dimension_semantics=sem, vmem_limit_bytes=vmem)


def _sigmoid(x):
    return 1.0 / (1.0 + jnp.exp(-x))


def _split3(x):
    hi = x.astype(BF16).astype(F32)
    r = x - hi
    mid = r.astype(BF16).astype(F32)
    lo = (r - mid).astype(BF16).astype(F32)
    return hi, mid, lo


def _dot(a, b):
    return jnp.dot(a, b, preferred_element_type=F32)


def _dot_nt(a, b):
    return lax.dot_general(a, b, (((1,), (1,)), ((), ())), preferred_element_type=F32)


def _dot_tn(a, b):
    return lax.dot_general(a, b, (((0,), (0,)), ((), ())), preferred_element_type=F32)


def _lane(shape):
    return lax.broadcasted_iota(jnp.int32, shape, 1)


def _lane_col(x, lane, idx):
    return jnp.sum(jnp.where(lane == idx, x, 0.0), axis=-1, keepdims=True)


def _put3(base, lane, start, pieces):
    out = base
    for k, p in enumerate(pieces):
        out = jnp.where(lane == start + k, p, out)
    return out


def _half_stats(t, lo):
    s0 = jnp.sum(jnp.where(lo, t, 0.0), axis=-1, keepdims=True)
    s1 = jnp.sum(jnp.where(lo, 0.0, t), axis=-1, keepdims=True)
    return jnp.where(lo, s0, s1)


def _fwd_in(x2, g, w_t, tm, tn):
    T, D = x2.shape
    N = w_t.shape[0]

    def body(x_ref, g_ref, w_ref, z_ref, h_ref):
        @pl.when(pl.program_id(1) == 0)
        def _():
            x = x_ref[...]
            r = lax.rsqrt(jnp.mean(x * x, axis=-1, keepdims=True) + EPS)
            h_ref[...] = (x * r * g_ref[...]).astype(BF16)

        z_ref[...] = _dot_nt(h_ref[...], w_ref[...])

    return pl.pallas_call(
        body, name="fwd_in", grid=(T // tm, N // tn),
        in_specs=[pl.BlockSpec((tm, D), lambda i, j: (i, 0)),
                  pl.BlockSpec((1, D), lambda i, j: (0, 0)),
                  pl.BlockSpec((tn, D), lambda i, j: (j, 0))],
        out_specs=[pl.BlockSpec((tm, tn), lambda i, j: (i, j)),
                   pl.BlockSpec((tm, D), lambda i, j: (i, 0))],
        out_shape=[SDS((T, N), F32), SDS((T, D), BF16)],
        compiler_params=_params(("parallel", "arbitrary")),
    )(x2, g, w_t)


def _tri_cumsum(x, reverse):
    t = x.shape[0]
    row = lax.broadcasted_iota(jnp.int32, (t, t), 0)
    col = lax.broadcasted_iota(jnp.int32, (t, t), 1)
    tri = (row <= col) if reverse else (row >= col)
    tri = jnp.where(tri, 1.0, 0.0).astype(BF16)
    hi, mid, lo = _split3(x)
    return _dot(tri, hi.astype(BF16)) + _dot(tri, mid.astype(BF16)) + _dot(tri, lo.astype(BF16))


def _gate_fwd(z, b_pad, B, S, H, col_blk, tc):
    T = B * S
    nsb = S // tc

    def body(zf_ref, b_ref, c_ref, carry):
        @pl.when(pl.program_id(1) == 0)
        def _():
            carry[...] = jnp.zeros_like(carry)

        x = zf_ref[...] + b_ref[...]
        lf = jnp.minimum(x, 0.0) - jnp.log(1.0 + jnp.exp(-jnp.abs(x)))
        lf = jnp.where(_lane(lf.shape) < H, lf, 0.0)
        c_ref[...] = _tri_cumsum(lf, False) + carry[...]
        carry[...] = carry[...] + jnp.sum(lf, axis=0, keepdims=True)

    return pl.pallas_call(
        body, name="gate_fwd", grid=(B, nsb),
        in_specs=[pl.BlockSpec((tc, LANES), lambda b, s: (b * nsb + s, col_blk)),
                  pl.BlockSpec((1, LANES), lambda b, s: (0, 0))],
        out_specs=pl.BlockSpec((tc, LANES), lambda b, s: (b * nsb + s, 0)),
        out_shape=SDS((T, LANES), F32),
        scratch_shapes=[pltpu.VMEM((1, LANES), F32)],
        compiler_params=_params(("parallel", "arbitrary")),
    )(z, b_pad)


def _qk_normalize(x, g, lo):
    r = lax.rsqrt(_half_stats(x * x, lo) * (1.0 / HEAD_DIM) + EPS)
    return x * r * g


def _attn_prep(z, c, gq, gk, B, S, H, tm):
    T = B * S
    FW = H * HEAD_DIM
    nsb = S // tm
    nfb = FW // LANES
    scale = HEAD_DIM ** -0.5

    def body(zq_ref, zk_ref, zv_ref, c_ref, gq_ref, gk_ref, qa_ref, ka_ref, va_ref):
        p = pl.program_id(1)
        lane = _lane((tm, LANES))
        lo = lane < HEAD_DIM
        qn = _qk_normalize(zq_ref[...], gq_ref[...], lo) * scale
        kn = _qk_normalize(zk_ref[...], gk_ref[...], lo)
        v = zv_ref[...]
        cc = c_ref[...]
        ones_q = ((lane >= L_KDECAY) & (lane < L_KDECAY + 3)).astype(F32)
        ones_k = (((lane >= L_ROWSUM) & (lane < L_ROWSUM + 3)) | ((lane >= L_LSE) & (lane < L_LSE + 3))).astype(F32)
        ones_v = ((lane >= L_ROWSUM) & (lane < L_D + 3)).astype(F32)
        for e in range(2):
            if e == 0:
                qe, ke, ve = qn, kn, v
            else:
                qe, ke, ve = (pltpu.roll(t, HEAD_DIM, 1) for t in (qn, kn, v))
            ch = _lane_col(cc, lane, 2 * p + e)
            pieces = _split3(ch)
            qa = jnp.where(lo, qe, _put3(ones_q, lane, L_ROWSUM, pieces))
            ka = jnp.where(lo, ke, _put3(ones_k, lane, L_KDECAY, [-t for t in pieces]))
            va = jnp.where(lo, ve, ones_v)
            qa_ref[0, e] = qa.astype(BF16)
            ka_ref[0, e] = ka.astype(BF16)
            va_ref[0, e] = va.astype(BF16)

    zspec = lambda off: pl.BlockSpec((tm, LANES), lambda i, p: (i, off + p))
    gspec = pl.BlockSpec((1, LANES), lambda i, p: (0, p))
    ospec = pl.BlockSpec((1, 2, tm, LANES), lambda i, p: (i // nsb, p, i % nsb, 0))
    oshape = SDS((B, H, S, LANES), BF16)
    return pl.pallas_call(
        body, name="attn_prep", grid=(T // tm, H // 2),
        in_specs=[zspec(0), zspec(nfb), zspec(2 * nfb),
                  pl.BlockSpec((tm, LANES), lambda i, p: (i, 0)), gspec, gspec],
        out_specs=[ospec, ospec, ospec],
        out_shape=[oshape, oshape, oshape],
        compiler_params=_params(("parallel", "arbitrary")),
    )(z, z, z, c, gq, gk)


def _attn_fwd(qa, ka, va, t, hb):
    B, H, S, _ = qa.shape
    nq = S // t

    def body(q_ref, k_ref, v_ref, o_ref, m_ref, acc_ref):
        i = pl.program_id(2)
        m_ref[...] = jnp.full_like(m_ref, NEG_INF)
        acc_ref[...] = jnp.zeros_like(acc_ref)

        def step(j, masked):
            rows = pl.ds(pl.multiple_of(j * t, t), t)
            if masked:
                keep = lax.broadcasted_iota(jnp.int32, (t, t), 0) >= lax.broadcasted_iota(jnp.int32, (t, t), 1)
            for e in range(hb):
                s = _dot_nt(q_ref[0, e], k_ref[0, e, rows, :])
                if masked:
                    s = jnp.where(keep, s, NEG_INF)
                m_prev = m_ref[e]
                m_new = jnp.maximum(m_prev, jnp.max(s, axis=-1, keepdims=True))
                alpha = jnp.exp(m_prev - m_new)
                p = jnp.exp(s - m_new).astype(BF16)
                acc_ref[e] = alpha * acc_ref[e] + _dot(p, v_ref[0, e, rows, :])
                m_ref[e] = m_new

        def loop_body(j, carry):
            step(j, False)
            return carry

        lax.fori_loop(0, i, loop_body, 0)
        step(i, True)

        lane = _lane((t, LANES))
        for e in range(hb):
            acc = acc_ref[e]
            l = _lane_col(acc, lane, L_ROWSUM)
            o_ref[0, e] = jnp.where(lane < HEAD_DIM, acc / l, m_ref[e] + jnp.log(l))

    return pl.pallas_call(
        body, name="attn_fwd", grid=(B, H // hb, nq),
        in_specs=[pl.BlockSpec((1, hb, t, LANES), lambda b, h, i: (b, h, i, 0)),
                  pl.BlockSpec((1, hb, S, LANES), lambda b, h, i: (b, h, 0, 0)),
                  pl.BlockSpec((1, hb, S, LANES), lambda b, h, i: (b, h, 0, 0))],
        out_specs=pl.BlockSpec((1, hb, t, LANES), lambda b, h, i: (b, h, i, 0)),
        out_shape=SDS((B, H, S, LANES), F32),
        scratch_shapes=[pltpu.VMEM((hb, t, 1), F32), pltpu.VMEM((hb, t, LANES), F32)],
        compiler_params=_params(("parallel", "parallel", "arbitrary")),
    )(qa, ka, va)


def _fill_shifts(ext_ref, sh_ref):
    rows = sh_ref.shape[1]
    for b in range(1, SUBLANES):
        sh_ref[b - 1] = ext_ref[pl.ds(b, rows), :]


def _tap_window(ext_ref, sh_ref, off, tm, cols):
    b = off % SUBLANES
    if b == 0:
        return ext_ref[pl.ds(off, tm), cols]
    return sh_ref[b - 1, pl.ds(off - b, tm), cols]


def _conv_taps(w_ref, ext_ref, sh_ref, out_ref, n_taps, tm, offset_of, bias_ref=None):
    for cc in range(out_ref.shape[1] // LANES):
        cols = slice(cc * LANES, (cc + 1) * LANES)
        acc = None
        for j in range(n_taps):
            term = w_ref[j:j + 1, cols] * _tap_window(ext_ref, sh_ref, offset_of(j), tm, cols)
            acc = term if acc is None else acc + term
        out_ref[:, cols] = acc if bias_ref is None else acc + bias_ref[:, cols]


def _layernorm_stats(u2):
    mu = jnp.mean(u2, axis=-1, keepdims=True)
    xc = u2 - mu
    rstd = lax.rsqrt(jnp.mean(xc * xc, axis=-1, keepdims=True) + EPS)
    return xc * rstd, rstd


def _fwd_out(oa, z, x2, tgt, conv_w, conv_b, ln_g, ln_b, w_out, B, S, H, n_taps, tm):
    T, D = x2.shape
    FW = H * HEAD_DIM
    CW = conv_w.shape[1]
    nsb = S // tm
    hb = tm // HALO

    def body(oa_ref, gf_ref, ga_ref, gb_ref, gc_ref, ha_ref, hb_ref, x_ref, t_ref, w_ref, cb_ref, lg_ref,
             lb_ref, wo_ref, y_ref, u2_ref, a_ref, do_ref, dob_ref, dy_ref, loss_ref, ext_ref, sh_ref):
        first_step = (pl.program_id(0) == 0) & (pl.program_id(1) == 0)

        @pl.when(first_step)
        def _():
            loss_ref[...] = jnp.zeros_like(loss_ref)

        u1 = ga_ref[...] * _sigmoid(gb_ref[...])
        halo = ha_ref[...] * _sigmoid(hb_ref[...])
        ext_ref[0:HALO, :] = jnp.where(pl.program_id(1) > 0, halo, 0.0)
        ext_ref[HALO:, :] = u1
        _fill_shifts(ext_ref, sh_ref)
        _conv_taps(w_ref, ext_ref, sh_ref, u2_ref, n_taps, tm, lambda j: HALO - (n_taps - 1) + j, cb_ref)
        uh, _ = _layernorm_stats(u2_ref[...])
        u3 = uh * lg_ref[...] + lb_ref[...]
        gc = gc_ref[...]
        yu = u3 * _sigmoid(u3) * (gc * _sigmoid(gc))
        y_ref[:, FW:] = yu.astype(BF16)

        lane = _lane((tm, LANES))
        lo = lane < HEAD_DIM
        for p in range(H // 2):
            a_ref[:, p * LANES:(p + 1) * LANES] = jnp.where(
                lo, oa_ref[0, 2 * p], pltpu.roll(oa_ref[0, 2 * p + 1], HEAD_DIM, 1))
        gf = gf_ref[...]
        y_ref[:, :FW] = (a_ref[...] * (gf * _sigmoid(gf))).astype(BF16)

        out = x_ref[...] + _dot(y_ref[...], wo_ref[...])
        diff = out - t_ref[...]
        loss_ref[...] = loss_ref[...] + jnp.sum(diff * diff)
        dout = diff * (1.0 / D)
        do_ref[...] = dout
        dob = dout.astype(BF16)
        dob_ref[...] = dob
        dy_ref[...] = _dot_nt(dob, wo_ref[...])

    row = lambda b, s: b * nsb + s
    zspec = lambda cb: pl.BlockSpec((tm, FW), lambda b, s: (row(b, s), cb))
    hspec = lambda cb: pl.BlockSpec((HALO, CW), lambda b, s: (jnp.maximum(row(b, s) * hb - 1, 0), cb))
    vspec = pl.BlockSpec((1, CW), lambda b, s: (0, 0))
    tspec = lambda w: pl.BlockSpec((tm, w), lambda b, s: (row(b, s), 0))
    return pl.pallas_call(
        body, name="fwd_out", grid=(B, nsb),
        in_specs=[pl.BlockSpec((1, H, tm, LANES), lambda b, s: (b, 0, s, 0)),
                  zspec(3), zspec(4), zspec(5), zspec(6), hspec(4), hspec(5),
                  tspec(D), tspec(D),
                  pl.BlockSpec((HALO, CW), lambda b, s: (0, 0)), vspec, vspec, vspec,
                  pl.BlockSpec((FW + CW, D), lambda b, s: (0, 0))],
        out_specs=[tspec(FW + CW), tspec(CW), tspec(FW), tspec(D), tspec(D), tspec(FW + CW),
                   pl.BlockSpec((8, LANES), lambda b, s: (0, 0))],
        out_shape=[SDS((T, FW + CW), BF16), SDS((T, CW), F32), SDS((T, FW), F32), SDS((T, D), F32),
                   SDS((T, D), BF16), SDS((T, FW + CW), F32), SDS((8, LANES), F32)],
        scratch_shapes=[pltpu.VMEM((tm + HALO, CW), F32),
                        pltpu.VMEM((SUBLANES - 1, tm + HALO - SUBLANES, CW), F32)],
        compiler_params=_params(("arbitrary", "arbitrary")),
    )(oa, z, z, z, z, z, z, x2, tgt, conv_w, conv_b, ln_g, ln_b, w_out)


def _bwd_prep(dy, z, a_nat, oa, qa, u2, ln_g, ln_b, B, S, H, tm):
    T = B * S
    FW = H * HEAD_DIM
    CW = u2.shape[1]
    nsb = S // tm

    def body(dya_ref, dyu_ref, gf_ref, gc_ref, a_ref, oa_ref, qa_ref, u2_ref, lg_ref, lb_ref,
             dzgf_ref, dzgc_ref, du2_ref, doa_ref, qb_ref, sg_ref):
        first_step = (pl.program_id(0) == 0) & (pl.program_id(1) == 0)

        @pl.when(first_step)
        def _():
            sg_ref[...] = jnp.zeros_like(sg_ref)

        gf = gf_ref[...]
        sg = _sigmoid(gf)
        a = a_ref[...]
        dya = dya_ref[...]
        da = dya * (gf * sg)
        dzgf_ref[...] = (dya * a * (sg * (1.0 + gf * (1.0 - sg)))).astype(BF16)
        dd = da * a
        lane = _lane((tm, LANES))
        lo = lane < HEAD_DIM
        for p in range(H // 2):
            cols = slice(p * LANES, (p + 1) * LANES)
            da_p = da[:, cols]
            dd_p = dd[:, cols]
            d_heads = (jnp.sum(jnp.where(lo, dd_p, 0.0), axis=-1, keepdims=True),
                       jnp.sum(jnp.where(lo, 0.0, dd_p), axis=-1, keepdims=True))
            for e in range(2):
                da_e = da_p if e == 0 else pltpu.roll(da_p, HEAD_DIM, 1)
                d_e = d_heads[e]
                aug = _put3(jnp.zeros((tm, LANES), F32), lane, L_D, _split3(-d_e))
                doa_ref[0, 2 * p + e] = jnp.where(lo, da_e, aug).astype(BF16)
                lse = _lane_col(oa_ref[0, 2 * p + e], lane, L_ROWSUM)
                qb = _put3(qa_ref[0, 2 * p + e].astype(F32), lane, L_LSE, _split3(-lse))
                qb_ref[0, 2 * p + e] = qb.astype(BF16)

        gc = gc_ref[...]
        sc = _sigmoid(gc)
        dyu = dyu_ref[...]
        uh, rstd = _layernorm_stats(u2_ref[...])
        u3 = uh * lg_ref[...] + lb_ref[...]
        s3 = _sigmoid(u3)
        dzgc_ref[...] = (dyu * (u3 * s3) * (sc * (1.0 + gc * (1.0 - sc)))).astype(BF16)
        du3 = dyu * (gc * sc) * (s3 * (1.0 + u3 * (1.0 - s3)))
        sg_ref[0:1, :] = sg_ref[0:1, :] + jnp.sum(du3 * uh, axis=0, keepdims=True)
        sg_ref[1:2, :] = sg_ref[1:2, :] + jnp.sum(du3, axis=0, keepdims=True)
        duh = du3 * lg_ref[...]
        du2 = rstd * (duh - jnp.mean(duh, axis=-1, keepdims=True)
                      - uh * jnp.mean(duh * uh, axis=-1, keepdims=True))
        sg_ref[2:3, :] = sg_ref[2:3, :] + jnp.sum(du2, axis=0, keepdims=True)
        du2_ref[...] = du2

    row = lambda b, s: b * nsb + s
    tspec = lambda w, cb=0: pl.BlockSpec((tm, w), lambda b, s: (row(b, s), cb))
    hspec = pl.BlockSpec((1, H, tm, LANES), lambda b, s: (b, 0, s, 0))
    vspec = pl.BlockSpec((1, CW), lambda b, s: (0, 0))
    return pl.pallas_call(
        body, name="bwd_prep", grid=(B, nsb),
        in_specs=[tspec(FW, 0), tspec(CW, 1), tspec(FW, 3), tspec(CW, 6), tspec(FW), hspec, hspec,
                  tspec(CW), vspec, vspec],
        out_specs=[tspec(FW), tspec(CW), tspec(CW), hspec, hspec,
                   pl.BlockSpec((8, CW), lambda b, s: (0, 0))],
        out_shape=[SDS((T, FW), BF16), SDS((T, CW), BF16), SDS((T, CW), F32),
                   SDS((B, H, S, LANES), BF16), SDS((B, H, S, LANES), BF16), SDS((8, CW), F32)],
        compiler_params=_params(("arbitrary", "arbitrary")),
    )(dy, dy, z, z, a_nat, oa, qa, u2, ln_g, ln_b)


def _conv_bwd(du2, z, conv_w, B, S, n_taps, tm):
    T, CW = du2.shape
    nsb = S // tm
    hb = tm // HALO

    def body(d_ref, dh_ref, ga_ref, gb_ref, ha_ref, hb_ref, w_ref, dz_ref, dw_ref,
             extu_ref, extd_ref, shu_ref, shd_ref, du1_ref, dwacc_ref):
        s = pl.program_id(1)
        first_step = (pl.program_id(0) == 0) & (s == 0)
        last_step = (pl.program_id(0) == B - 1) & (s == nsb - 1)

        @pl.when(first_step)
        def _():
            dwacc_ref[...] = jnp.zeros_like(dwacc_ref)

        ga = ga_ref[...]
        sb = _sigmoid(gb_ref[...])
        halo = ha_ref[...] * _sigmoid(hb_ref[...])
        extu_ref[0:HALO, :] = jnp.where(s > 0, halo, 0.0)
        extu_ref[HALO:, :] = ga * sb
        extd_ref[0:tm, :] = d_ref[...]
        extd_ref[tm:, :] = jnp.where(s < nsb - 1, dh_ref[...], 0.0)
        _fill_shifts(extu_ref, shu_ref)
        _fill_shifts(extd_ref, shd_ref)
        _conv_taps(w_ref, extd_ref, shd_ref, du1_ref, n_taps, tm, lambda j: n_taps - 1 - j)
        for cc in range(CW // LANES):
            cols = slice(cc * LANES, (cc + 1) * LANES)
            parts = [None] * n_taps
            for r in range(tm // SUBLANES):
                dv = d_ref[r * SUBLANES:(r + 1) * SUBLANES, cols]
                for j in range(n_taps):
                    off = HALO - (n_taps - 1) + j + r * SUBLANES
                    term = dv * _tap_window(extu_ref, shu_ref, off, SUBLANES, cols)
                    parts[j] = term if parts[j] is None else parts[j] + term
            for j in range(n_taps):
                rows = slice(j * SUBLANES, (j + 1) * SUBLANES)
                dwacc_ref[rows, cols] = dwacc_ref[rows, cols] + parts[j]
        du1 = du1_ref[...]
        dz_ref[:, :CW] = (du1 * sb).astype(BF16)
        dz_ref[:, CW:] = (du1 * ga * (sb * (1.0 - sb))).astype(BF16)

        @pl.when(last_step)
        def _():
            dw_ref[...] = jnp.zeros_like(dw_ref)
            for j in range(n_taps):
                dw_ref[j:j + 1, :] = jnp.sum(dwacc_ref[j * SUBLANES:(j + 1) * SUBLANES, :], axis=0, keepdims=True)

    row = lambda b, s: b * nsb + s
    last_halo = T // HALO - 1
    return pl.pallas_call(
        body, name="conv_bwd", grid=(B, nsb),
        in_specs=[pl.BlockSpec((tm, CW), lambda b, s: (row(b, s), 0)),
                  pl.BlockSpec((HALO, CW), lambda b, s: (jnp.minimum((row(b, s) + 1) * hb, last_halo), 0)),
                  pl.BlockSpec((tm, CW), lambda b, s: (row(b, s), 4)),
                  pl.BlockSpec((tm, CW), lambda b, s: (row(b, s), 5)),
                  pl.BlockSpec((HALO, CW), lambda b, s: (jnp.maximum(row(b, s) * hb - 1, 0), 4)),
                  pl.BlockSpec((HALO, CW), lambda b, s: (jnp.maximum(row(b, s) * hb - 1, 0), 5)),
                  pl.BlockSpec((HALO, CW), lambda b, s: (0, 0))],
        out_specs=[pl.BlockSpec((tm, 2 * CW), lambda b, s: (row(b, s), 0)),
                   pl.BlockSpec((HALO, CW), lambda b, s: (0, 0))],
        out_shape=[SDS((T, 2 * CW), BF16), SDS((HALO, CW), F32)],
        scratch_shapes=[pltpu.VMEM((tm + HALO, CW), F32), pltpu.VMEM((tm + HALO, CW), F32),
                        pltpu.VMEM((SUBLANES - 1, tm + HALO - SUBLANES, CW), F32),
                        pltpu.VMEM((SUBLANES - 1, tm + HALO - SUBLANES, CW), F32),
                        pltpu.VMEM((tm, CW), F32), pltpu.VMEM((HALO * SUBLANES, CW), F32)],
        compiler_params=_params(("arbitrary", "arbitrary")),
    )(du2, du2, z, z, z, z, conv_w)


def _attn_bwd(qb, ka, va, doa, t, hb):
    B, H, S, _ = qb.shape
    nk = S // t

    def body(q_ref, k_ref, v_ref, do_ref, dq_ref, dk_ref, dv_ref):
        j = pl.program_id(2)

        @pl.when(j == 0)
        def _():
            dq_ref[...] = jnp.zeros_like(dq_ref)

        dk_ref[...] = jnp.zeros_like(dk_ref)
        dv_ref[...] = jnp.zeros_like(dv_ref)

        def step(i, masked):
            rows = pl.ds(pl.multiple_of(i * t, t), t)
            if masked:
                keep = lax.broadcasted_iota(jnp.int32, (t, t), 0) >= lax.broadcasted_iota(jnp.int32, (t, t), 1)
            for e in range(hb):
                k = k_ref[0, e]
                q = q_ref[0, e, rows, :]
                do = do_ref[0, e, rows, :]
                p = jnp.exp(_dot_nt(q, k))
                if masked:
                    p = jnp.where(keep, p, 0.0)
                ds = (p * _dot_nt(do, v_ref[0, e])).astype(BF16)
                dv_ref[0, e] = dv_ref[0, e] + _dot_tn(p.astype(BF16), do)
                dk_ref[0, e] = dk_ref[0, e] + _dot_tn(ds, q)
                dq_ref[0, e, rows, :] = dq_ref[0, e, rows, :] + _dot(ds, k)

        step(j, True)

        def loop_body(i, carry):
            step(i, False)
            return carry

        lax.fori_loop(j + 1, nk, loop_body, 0)

    full = pl.BlockSpec((1, hb, S, LANES), lambda b, h, j: (b, h, 0, 0))
    blk = pl.BlockSpec((1, hb, t, LANES), lambda b, h, j: (b, h, j, 0))
    oshape = SDS((B, H, S, LANES), F32)
    return pl.pallas_call(
        body, name="attn_bwd", grid=(B, H // hb, nk),
        in_specs=[full, blk, blk, full],
        out_specs=[full, blk, blk],
        out_shape=[oshape, oshape, oshape],
        compiler_params=_params(("parallel", "parallel", "arbitrary")),
    )(qb, ka, va, doa)


def _qk_bwd(dqa, dka, dva, z, gq, gk, B, S, H, tm):
    T = B * S
    FW = H * HEAD_DIM
    nsb = S // tm
    nfb = FW // LANES
    scale = HEAD_DIM ** -0.5

    def body(dq_ref, dk_ref, dv_ref, zq_ref, zk_ref, gq_ref, gk_ref, dzq_ref, dzk_ref, dzv_ref, dc_ref, dg_ref):
        p = pl.program_id(0)

        @pl.when(pl.program_id(1) == 0)
        def _():
            dg_ref[...] = jnp.zeros_like(dg_ref)

        lane = _lane((tm, LANES))
        lo = lane < HEAD_DIM

        def natural(ref):
            return jnp.where(lo, ref[0, 0], pltpu.roll(ref[0, 1], HEAD_DIM, 1))

        def norm_bwd(dn, x, g, row, out_ref):
            r = lax.rsqrt(_half_stats(x * x, lo) * (1.0 / HEAD_DIM) + EPS)
            xh = x * r
            dg_ref[row:row + 1, :] = dg_ref[row:row + 1, :] + jnp.sum(dn * xh, axis=0, keepdims=True)
            dxh = dn * g
            mm = _half_stats(dxh * xh, lo) * (1.0 / HEAD_DIM)
            out_ref[...] = (r * (dxh - xh * mm)).astype(BF16)

        norm_bwd(natural(dq_ref) * scale, zq_ref[...], gq_ref[...], 0, dzq_ref)
        norm_bwd(natural(dk_ref), zk_ref[...], gk_ref[...], 1, dzk_ref)
        dzv_ref[...] = natural(dv_ref).astype(BF16)

        dc = jnp.zeros((tm, LANES), F32)
        for e in range(2):
            val = _lane_col(dq_ref[0, e], lane, L_ROWSUM) - _lane_col(dk_ref[0, e], lane, L_KDECAY)
            dc = jnp.where(lane == 2 * p + e, val, dc)
        dc_ref[0] = dc

    hspec = pl.BlockSpec((1, 2, tm, LANES), lambda p, i: (i // nsb, p, i % nsb, 0))
    zspec = lambda off: pl.BlockSpec((tm, LANES), lambda p, i: (i, off + p))
    gspec = pl.BlockSpec((1, LANES), lambda p, i: (0, p))
    ospec = pl.BlockSpec((tm, LANES), lambda p, i: (i, p))
    return pl.pallas_call(
        body, name="qk_bwd", grid=(H // 2, T // tm),
        in_specs=[hspec, hspec, hspec, zspec(0), zspec(nfb), gspec, gspec],
        out_specs=[ospec, ospec, ospec,
                   pl.BlockSpec((1, tm, LANES), lambda p, i: (p, i, 0)),
                   pl.BlockSpec((8, LANES), lambda p, i: (0, p))],
        out_shape=[SDS((T, FW), BF16), SDS((T, FW), BF16), SDS((T, FW), BF16),
                   SDS((H // 2, T, LANES), F32), SDS((8, FW), F32)],
        compiler_params=_params(("parallel", "arbitrary")),
    )(dqa, dka, dva, z, z, gq, gk)


def _gate_bwd(dc8, z, b_pad, B, S, H, col_blk, fp, tc):
    T = B * S
    nsb = S // tc
    npair = dc8.shape[0]

    def body(dc_ref, zf_ref, b_ref, dz_ref, db_ref, carry):
        first_step = (pl.program_id(0) == 0) & (pl.program_id(1) == 0)

        @pl.when(first_step)
        def _():
            db_ref[...] = jnp.zeros_like(db_ref)

        @pl.when(pl.program_id(1) == 0)
        def _():
            carry[...] = jnp.zeros_like(carry)

        dc = dc_ref[0]
        for k in range(1, npair):
            dc = dc + dc_ref[k]
        dlf = _tri_cumsum(dc, True) + carry[...]
        carry[...] = carry[...] + jnp.sum(dc, axis=0, keepdims=True)
        x = zf_ref[...] + b_ref[...]
        dlogit = dlf * _sigmoid(-x)
        db_ref[0:1, :] = db_ref[0:1, :] + jnp.sum(dlogit, axis=0, keepdims=True)
        dz_ref[...] = jnp.zeros_like(dz_ref)
        dz_ref[:, :LANES] = dlogit.astype(BF16)

    rrow = lambda b, s: b * nsb + (nsb - 1 - s)
    return pl.pallas_call(
        body, name="gate_bwd", grid=(B, nsb),
        in_specs=[pl.BlockSpec((npair, tc, LANES), lambda b, s: (0, rrow(b, s), 0)),
                  pl.BlockSpec((tc, LANES), lambda b, s: (rrow(b, s), col_blk)),
                  pl.BlockSpec((1, LANES), lambda b, s: (0, 0))],
        out_specs=[pl.BlockSpec((tc, fp), lambda b, s: (rrow(b, s), 0)),
                   pl.BlockSpec((8, LANES), lambda b, s: (0, 0))],
        out_shape=[SDS((T, fp), BF16), SDS((8, LANES), F32)],
        scratch_shapes=[pltpu.VMEM((1, LANES), F32)],
        compiler_params=_params(("arbitrary", "arbitrary")),
    )(dc8, z, b_pad)


def _matmul_tn(a, b, name, tmm, tn, tk):
    T, M = a.shape
    N = b.shape[1]
    tmm, tn, tk = min(tmm, M), min(tn, N), min(tk, T)

    def body(a_ref, b_ref, o_ref):
        @pl.when(pl.program_id(2) == 0)
        def _():
            o_ref[...] = jnp.zeros_like(o_ref)

        o_ref[...] = o_ref[...] + _dot_tn(a_ref[...], b_ref[...])

    return pl.pallas_call(
        body, name=name, grid=(M // tmm, N // tn, T // tk),
        in_specs=[pl.BlockSpec((tk, tmm), lambda i, j, k: (k, i)),
                  pl.BlockSpec((tk, tn), lambda i, j, k: (k, j))],
        out_specs=pl.BlockSpec((tmm, tn), lambda i, j, k: (i, j)),
        out_shape=SDS((M, N), F32),
        compiler_params=_params(("parallel", "parallel", "arbitrary")),
    )(a, b)


def _dh_rms_bwd(pieces, w_t, x2, g, dout, tm, tk):
    T, D = x2.shape
    nks = [p.shape[1] // tk for p in pieces]
    starts = [sum(nks[:k]) for k in range(len(pieces))]
    nk = sum(nks)

    def body(*refs):
        dz_refs = refs[:len(pieces)]
        w_ref, x_ref, g_ref, do_ref, gx_ref, dg_ref, acc_ref = refs[len(pieces):]
        k = pl.program_id(1)
        first_step = (pl.program_id(0) == 0) & (k == 0)

        @pl.when(first_step)
        def _():
            dg_ref[...] = jnp.zeros_like(dg_ref)

        @pl.when(k == 0)
        def _():
            acc_ref[...] = jnp.zeros_like(acc_ref)

        for dz_ref, st, n in zip(dz_refs, starts, nks):
            @pl.when((k >= st) & (k < st + n))
            def _(dz_ref=dz_ref):
                acc_ref[...] = acc_ref[...] + _dot(dz_ref[...], w_ref[...])

        @pl.when(k == nk - 1)
        def _():
            x = x_ref[...]
            r = lax.rsqrt(jnp.mean(x * x, axis=-1, keepdims=True) + EPS)
            xh = x * r
            dh = acc_ref[...]
            dg_ref[0:1, :] = dg_ref[0:1, :] + jnp.sum(dh * xh, axis=0, keepdims=True)
            dxn = dh * g_ref[...]
            gx_ref[...] = do_ref[...] + r * (dxn - xh * jnp.mean(dxn * xh, axis=-1, keepdims=True))

    def piece_spec(st, n):
        return pl.BlockSpec((tm, tk), lambda i, k: (i, jnp.clip(k - st, 0, n - 1)))

    tspec = pl.BlockSpec((tm, D), lambda i, k: (i, 0))
    return pl.pallas_call(
        body, name="dh_rms_bwd", grid=(T // tm, nk),
        in_specs=[piece_spec(st, n) for st, n in zip(starts, nks)]
        + [pl.BlockSpec((tk, D), lambda i, k: (k, 0)), tspec, pl.BlockSpec((1, D), lambda i, k: (0, 0)), tspec],
        out_specs=[tspec, pl.BlockSpec((8, D), lambda i, k: (0, 0))],
        out_shape=[SDS((T, D), F32), SDS((8, D), F32)],
        scratch_shapes=[pltpu.VMEM((tm, D), F32)],
        compiler_params=_params(("arbitrary", "arbitrary")),
    )(*pieces, w_t, x2, g, dout)


def _block_plan(R, C, tr, tc):
    br = min(tr, R)
    if R % br == 0:
        return (br, C), R // br, lambda i: (i, 0)
    bc = min(tc, C)
    assert C % bc == 0
    return (R, bc), C // bc, lambda i: (0, i)


def _ew_call(body, name, ins, n_out, out_dtypes, tr, tc):
    R, C = ins[0].shape
    blk, steps, imap = _block_plan(R, C, tr, tc)
    spec = pl.BlockSpec(blk, imap)
    return pl.pallas_call(
        body, name=name, grid=(steps,),
        in_specs=[spec] * len(ins), out_specs=[spec] * n_out,
        out_shape=[SDS((R, C), dt) for dt in out_dtypes],
        compiler_params=_params(("parallel",)),
    )(*ins)


def _pair_sum_bf16(a, b, name):
    def body(a_ref, b_ref, o_ref):
        o_ref[...] = (a_ref[...] + b_ref[...]).astype(BF16)

    return _ew_call(body, name, [a, b], 1, [BF16], 256, LANES)[0]


def _sum_slots(slots, name, first=None, tr=256):
    n, R, C = slots.shape
    blk, steps, imap = _block_plan(R, C, tr, 2 * LANES)
    lead = [] if first is None else [first]

    def body(*refs):
        s_ref, o_ref = refs[-2:]
        acc = refs[0][...].astype(F32) if lead else s_ref[0].astype(F32)
        for k in range(0 if lead else 1, n):
            acc = acc + s_ref[k].astype(F32)
        o_ref[...] = acc

    return pl.pallas_call(
        body, name=name, grid=(steps,),
        in_specs=[pl.BlockSpec(blk, imap)] * len(lead) + [pl.BlockSpec((n,) + blk, lambda i: (0,) + imap(i))],
        out_specs=pl.BlockSpec(blk, imap),
        out_shape=SDS((R, C), F32),
        compiler_params=_params(("parallel",)),
    )(*lead, slots)


def _adamw(w, g, m, v, name):
    def body(w_ref, g_ref, m_ref, v_ref, d_ref, nm_ref, nv_ref):
        gg = g_ref[...]
        nm = ADAM_B1 * m_ref[...] + (1.0 - ADAM_B1) * gg
        nv = ADAM_B2 * v_ref[...] + (1.0 - ADAM_B2) * (gg * gg)
        m_hat = nm / (1.0 - ADAM_B1 ** ADAM_STEP)
        v_hat = nv / (1.0 - ADAM_B2 ** ADAM_STEP)
        d_ref[...] = -ADAM_LR * (m_hat / (jnp.sqrt(v_hat) + ADAM_EPS) + ADAM_WD * w_ref[...])
        nm_ref[...] = nm
        nv_ref[...] = nv

    return _ew_call(body, name, [w, g, m, v], 3, [F32, F32, F32], 128, 2 * LANES)


ANY = pl.BlockSpec(memory_space=pl.ANY)


def _place():
    return lax.axis_index("x"), lax.axis_index("y"), lax.axis_index("c")


def _gather_chips(shards):
    n = len(shards)
    per = 7

    def body(*refs):
        ins, outs = refs[:n], refs[n:2 * n]
        send_sems, recv_sems = refs[2 * n:]
        x, y, c = _place()
        mine = 2 * x + y
        me, sibling = (x, y, c), (x, y, 1 - c)
        chips = [(1 - x, y), (x, 1 - y), (1 - x, 1 - y)]

        def copy(a, k, chip_idx, half, to, src=None):
            dst = outs[a].at[chip_idx, half]
            return pltpu.make_async_remote_copy(
                src_ref=dst if src is None else src, dst_ref=dst,
                send_sem=send_sems.at[a * per + k], recv_sem=recv_sems.at[a * per + k],
                device_id=to, device_id_type=MESH)

        def own(a, to):
            return pltpu.make_async_remote_copy(
                src_ref=ins[a], dst_ref=outs[a].at[mine],
                send_sem=send_sems.at[a * per + 6], recv_sem=recv_sems.at[a * per + 6],
                device_id=to, device_id_type=MESH)

        first = [copy(a, k, mine, c, (*chip, c), src=ins[a].at[c]) for a in range(n) for k, chip in enumerate(chips)]
        first += [own(a, sibling) for a in range(n)]
        for cp in first:
            cp.start()
        passed = []
        for k, (cx, cy) in enumerate(chips):
            for a in range(n):
                copy(a, k, 2 * cx + cy, c, me).wait_recv()
                fwd = copy(a, 3 + k, 2 * cx + cy, c, sibling)
                fwd.start()
                passed.append(fwd)
        for k, (cx, cy) in enumerate(chips):
            for a in range(n):
                copy(a, 3 + k, 2 * cx + cy, 1 - c, me).wait_recv()
        for a in range(n):
            own(a, me).wait_recv()
        for cp in first + passed:
            cp.wait_send()

    return pl.pallas_call(
        body, name="gather_chips",
        in_specs=[ANY] * n, out_specs=[ANY] * n,
        out_shape=[SDS((4,) + s.shape, s.dtype) for s in shards],
        scratch_shapes=[pltpu.SemaphoreType.DMA((per * n,)), pltpu.SemaphoreType.DMA((per * n,))],
    )(*shards)


def _pair_swap(halves):
    n = len(halves)

    def body(*refs):
        ins, outs = refs[:n], refs[n:2 * n]
        send_sems, recv_sems = refs[2 * n:]
        x, y, c = _place()
        copies = [pltpu.make_async_remote_copy(
            src_ref=ins[a].at[1 - c], dst_ref=outs[a], send_sem=send_sems.at[a], recv_sem=recv_sems.at[a],
            device_id=(x, y, 1 - c), device_id_type=MESH) for a in range(n)]
        for cp in copies:
            cp.start()
        for cp in copies:
            cp.wait()

    return pl.pallas_call(
        body, name="pair_swap",
        in_specs=[ANY] * n, out_specs=[ANY] * n,
        out_shape=[SDS(h.shape[1:], h.dtype) for h in halves],
        scratch_shapes=[pltpu.SemaphoreType.DMA((n,)), pltpu.SemaphoreType.DMA((n,))],
    )(*halves)


def _chip_exchange(parts):
    n = len(parts)

    def body(*refs):
        ins, outs = refs[:n], refs[n:2 * n]
        send_sems, recv_sems = refs[2 * n:]
        x, y, c = _place()
        chips = [(1 - x, y), (x, 1 - y), (1 - x, 1 - y)]

        def copy(a, k, to):
            cx, cy = chips[k]
            return pltpu.make_async_remote_copy(
                src_ref=ins[a].at[2 * cx + cy], dst_ref=outs[a].at[k],
                send_sem=send_sems.at[a * 3 + k], recv_sem=recv_sems.at[a * 3 + k],
                device_id=to, device_id_type=MESH)

        sends = [copy(a, k, (*chips[k], c)) for a in range(n) for k in range(3)]
        for cp in sends:
            cp.start()
        for a in range(n):
            for k in range(3):
                copy(a, k, (x, y, c)).wait_recv()
        for cp in sends:
            cp.wait_send()

    return pl.pallas_call(
        body, name="chip_exchange",
        in_specs=[ANY] * n, out_specs=[ANY] * n,
        out_shape=[SDS((3,) + p.shape[1:], p.dtype) for p in parts],
        scratch_shapes=[pltpu.SemaphoreType.DMA((3 * n,)), pltpu.SemaphoreType.DMA((3 * n,))],
    )(*parts)


def _pair_send(arrs):
    n = len(arrs)

    def body(*refs):
        ins, outs = refs[:n], refs[n:2 * n]
        send_sems, recv_sems = refs[2 * n:]
        x, y, c = _place()
        copies = [pltpu.make_async_remote_copy(
            src_ref=ins[a], dst_ref=outs[a], send_sem=send_sems.at[a], recv_sem=recv_sems.at[a],
            device_id=(x, y, 1 - c), device_id_type=MESH) for a in range(n)]
        for cp in copies:
            cp.start()
        for cp in copies:
            cp.wait()

    return pl.pallas_call(
        body, name="pair_send",
        in_specs=[ANY] * n, out_specs=[ANY] * n,
        out_shape=[SDS(h.shape, h.dtype) for h in arrs],
        scratch_shapes=[pltpu.SemaphoreType.DMA((n,)), pltpu.SemaphoreType.DMA((n,))],
    )(*arrs)


def _gather_all(buf):
    flips = [(fx, fy, fc) for fx in (0, 1) for fy in (0, 1) for fc in (0, 1)][1:]

    def body(in_ref, out_ref, send_sems, recv_sems, local_sem):
        x, y, c = _place()
        me = 4 * x + 2 * y + c
        local = pltpu.make_async_copy(in_ref, out_ref.at[me], local_sem)
        local.start()
        sends = []
        for k, (fx, fy, fc) in enumerate(flips):
            cp = pltpu.make_async_remote_copy(
                src_ref=in_ref, dst_ref=out_ref.at[me], send_sem=send_sems.at[k], recv_sem=recv_sems.at[k],
                device_id=(x ^ fx, y ^ fy, c ^ fc), device_id_type=MESH)
            cp.start()
            sends.append(cp)
        for k, (fx, fy, fc) in enumerate(flips):
            src = 4 * (x ^ fx) + 2 * (y ^ fy) + (c ^ fc)
            pltpu.make_async_remote_copy(
                src_ref=in_ref, dst_ref=out_ref.at[src], send_sem=send_sems.at[k], recv_sem=recv_sems.at[k],
                device_id=(x, y, c), device_id_type=MESH).wait_recv()
        for cp in sends:
            cp.wait_send()
        local.wait()

    return pl.pallas_call(
        body, name="gather_all",
        in_specs=[ANY], out_specs=ANY,
        out_shape=SDS((8,) + buf.shape, buf.dtype),
        scratch_shapes=[pltpu.SemaphoreType.DMA((7,)), pltpu.SemaphoreType.DMA((7,)), pltpu.SemaphoreType.DMA],
    )(buf)


def _tiles(S, FW):
    big = FW % 512 == 0
    return dict(
        fp=512 if big else LANES,
        tn=512 if big else LANES,
        tm_in=min(1024, S),
        t_attn=min(512, S),
        hb_fwd=4,
        hb_bwd=2,
        tm_prep=min(512, S),
        tm_mix=min(128, S),
        tc=min(256, S),
        tk=512 if big else LANES,
    )


def kernel(x, norm_g, w_in, b_forget, q_norm_g, k_norm_g, conv_w, conv_b, conv_ln_g, conv_ln_b, w_out, loss_target, m_norm_g, m_w_in, m_b_forget, m_q_norm_g, m_k_norm_g, m_conv_w, m_conv_b, m_conv_ln_g, m_conv_ln_b, m_w_out, v_norm_g, v_w_in, v_b_forget, v_q_norm_g, v_k_norm_g, v_conv_w, v_conv_b, v_conv_ln_g, v_conv_ln_b, v_w_out):
    B, S, D = x.shape
    H, dh = q_norm_g.shape[1:]
    FW = H * dh
    CW = conv_b.shape[-1]
    n_taps, cw_shard = conv_w.shape[1:]
    in_shard = w_in.shape[2]
    out_shard = w_out.shape[1]
    assert dh == HEAD_DIM and H % 2 == 0 and H <= LANES and FW == CW == D
    assert n_taps - 1 <= HALO and 4 * cw_shard == CW and 4 * out_shard == FW + CW
    assert 4 * in_shard == 4 * FW + 3 * CW + H
    T = B * S
    tl = _tiles(S, FW)
    fp = tl["fp"]
    xi, yi, ci = _place()

    w_t = jnp.transpose(w_in[0])
    conv_pad = jnp.pad(conv_w[0], ((0, HALO - n_taps), (0, 0)))
    g_in, g_out, g_cw = _gather_chips([
        w_t.astype(BF16).reshape(in_shard, 2, D // 2).transpose(1, 0, 2),
        w_out[0].astype(BF16).reshape(2, out_shard // 2, D),
        conv_pad.reshape(2, HALO // 2, cw_shard)])
    w_t_full = g_in.transpose(0, 2, 1, 3).reshape(4 * in_shard, D)
    w_out_full = g_out.reshape(FW + CW, D)
    conv_full = g_cw.reshape(4, HALO, cw_shard).transpose(1, 0, 2).reshape(HALO, CW)
    o_f = 3 * FW
    w_pack = jnp.concatenate([w_t_full[:o_f], w_t_full[o_f + H:],
                              jnp.pad(w_t_full[o_f:o_f + H], ((0, fp - H), (0, 0)))], axis=0)
    f_col = 4 * FW + 3 * CW

    x2 = x.reshape(T, D)
    tgt = loss_target.reshape(T, D)
    b_pad = jnp.pad(b_forget, ((0, 0), (0, LANES - H)))
    gq = q_norm_g.reshape(1, FW)
    gk = k_norm_g.reshape(1, FW)

    z, h = _fwd_in(x2, norm_g, w_pack, tl["tm_in"], tl["tn"])
    c = _gate_fwd(z, b_pad, B, S, H, f_col // LANES, tl["tc"])
    qa, ka, va = _attn_prep(z, c, gq, gk, B, S, H, tl["tm_prep"])
    oa = _attn_fwd(qa, ka, va, tl["t_attn"], tl["hb_fwd"])
    y, u2, a_nat, dout, dout_b, dy, loss_acc = _fwd_out(
        oa, z, x2, tgt, conv_full, conv_b, conv_ln_g, conv_ln_b, w_out_full, B, S, H, n_taps, tl["tm_mix"])
    loss = lax.psum(0.5 * loss_acc[0, 0] / D, ("x", "y", "c"))

    dzgf, dzgc, du2, doa, qb, sg_conv = _bwd_prep(dy, z, a_nat, oa, qa, u2, conv_ln_g, conv_ln_b, B, S, H, tl["tm_mix"])
    dzglu, dconv_w = _conv_bwd(du2, z, conv_full, B, S, n_taps, tl["tm_mix"])
    dqa, dka, dva = _attn_bwd(qb, ka, va, doa, tl["t_attn"], tl["hb_bwd"])
    dzq, dzk, dzv, dc8, dg_qk = _qk_bwd(dqa, dka, dva, z, gq, gk, B, S, H, tl["tm_prep"])
    dzf, db_f = _gate_bwd(dc8, z, b_pad, B, S, H, f_col // LANES, fp, tl["tc"])
    pieces = [dzq, dzk, dzv, dzgf, dzglu, dzgc, dzf]
    grad_x2, dg_norm = _dh_rms_bwd(pieces, w_pack, x2, norm_g, dout, tl["tm_in"], tl["tk"])
    dw_pieces = [_matmul_tn(p, h, f"dw_in_{k}", 1024, 1024, 512) for k, p in enumerate(pieces)]
    dw_out = _matmul_tn(y, dout_b, "dw_out", 1024, 1024, 512)
    dw_t = jnp.concatenate(dw_pieces[:3] + [dw_pieces[6][:H]] + dw_pieces[3:6], axis=0)

    halves_in = dw_t.reshape(4 * in_shard, 2, D // 2).transpose(1, 0, 2)
    halves_out = dw_out.reshape(4, 2, out_shard // 2, D).transpose(1, 0, 2, 3).reshape(2, 2 * out_shard, D)
    got_in, got_out = _pair_swap([halves_in, halves_out])
    own_in = lax.dynamic_index_in_dim(halves_in, ci, 0, keepdims=False)
    own_out = lax.dynamic_index_in_dim(halves_out, ci, 0, keepdims=False)
    part_in = _pair_sum_bf16(own_in, got_in, "pair_sum_in").reshape(4, in_shard, D // 2)
    part_out = _pair_sum_bf16(own_out, got_out, "pair_sum_out").reshape(4, out_shard // 2, D)
    slots_in, slots_out = _chip_exchange([part_in, part_out])
    chip = 2 * xi + yi
    half_in = _sum_slots(slots_in, "chip_sum_in", lax.dynamic_index_in_dim(part_in, chip, 0, keepdims=False))
    half_out = _sum_slots(slots_out, "chip_sum_out", lax.dynamic_index_in_dim(part_out, chip, 0, keepdims=False))
    other_in, other_out = _pair_send([half_in, half_out])

    def both_halves(mine, other, axis):
        return jnp.where(ci == 0, jnp.concatenate([mine, other], axis=axis), jnp.concatenate([other, mine], axis=axis))

    grad_w_t = both_halves(half_in, other_in, 1)
    grad_w_out = both_halves(half_out, other_out, 0)

    small = jnp.concatenate([
        dg_norm[0:1], jnp.pad(db_f[0:1, :], ((0, 0), (0, D - LANES))), dg_qk[0:1], dg_qk[1:2],
        sg_conv[2:3], sg_conv[0:1], sg_conv[1:2], dconv_w], axis=0)
    n_small = small.shape[0]
    small_sum = _sum_slots(_gather_all(small), "small_sum", tr=n_small)
    grad_norm_g, grad_b_f = small_sum[0:1], small_sum[1:2, :H]
    grad_gq, grad_gk = small_sum[2:3].reshape(1, H, dh), small_sum[3:4].reshape(1, H, dh)
    grad_conv_b, grad_ln_g, grad_ln_b = small_sum[4:5], small_sum[5:6], small_sum[6:7]
    grad_conv_w = lax.dynamic_slice_in_dim(small_sum[7:7 + n_taps], chip * cw_shard, cw_shard, axis=1)

    d_t, nm_t, nv_t = _adamw(w_t, grad_w_t, jnp.transpose(m_w_in[0]), jnp.transpose(v_w_in[0]), "adamw_in")
    grad_w_in, d_in, nm_in, nv_in = (jnp.transpose(t)[None] for t in (grad_w_t, d_t, nm_t, nv_t))
    d_out, nm_out, nv_out = (t[None] for t in _adamw(w_out[0], grad_w_out, m_w_out[0], v_w_out[0], "adamw_out"))
    d_cw, nm_cw, nv_cw = (t[None] for t in _adamw(conv_w[0], grad_conv_w, m_conv_w[0], v_conv_w[0], "adamw_conv_w"))

    def rows(ws):
        return jnp.concatenate([jnp.pad(t.reshape(1, -1), ((0, 0), (0, D - t.size))) for t in ws], axis=0)

    small_w = [norm_g, b_forget, q_norm_g, k_norm_g, conv_b, conv_ln_g, conv_ln_b]
    small_m = [m_norm_g, m_b_forget, m_q_norm_g, m_k_norm_g, m_conv_b, m_conv_ln_g, m_conv_ln_b]
    small_v = [v_norm_g, v_b_forget, v_q_norm_g, v_k_norm_g, v_conv_b, v_conv_ln_g, v_conv_ln_b]
    d_s, nm_s, nv_s = _adamw(rows(small_w), small_sum[0:7], rows(small_m), rows(small_v), "adamw_small")

    def unpack(t):
        return [t[k:k + 1, :w.size].reshape(w.shape) for k, w in enumerate(small_w)]

    def order(s, in_, cw, out_):
        ng, bf, qg, kg, cb, lg, lb = s
        return [ng, in_, bf, qg, kg, cw, cb, lg, lb, out_]

    grads = [grad_norm_g, grad_w_in, grad_b_f, grad_gq, grad_gk, grad_conv_w[None],
             grad_conv_b, grad_ln_g, grad_ln_b, grad_w_out[None]]
    return (loss, grad_x2.reshape(B, S, D), *grads,
            *order(unpack(d_s), d_in, d_cw, d_out),
            *order(unpack(nm_s), nm_in, nm_cw, nm_out),
            *order(unpack(nv_s), nv_in, nv_cw, nv_out))
```

```python
import functools

import jax
import jax.numpy as jnp
from jax import lax
from jax.experimental import pallas as pl
from jax.experimental.pallas import tpu as pltpu

F32 = jnp.float32
BF16 = jnp.bfloat16
SDS = jax.ShapeDtypeStruct
MESH = pl.DeviceIdType.MESH

EPS = 1e-6
NEG_INF = -1e30
LANES = 128
SUBLANES = 8
HEAD_DIM = 64
HALO = 32
VMEM_LIMIT = 56 * 1024 * 1024

L_ROWSUM = 64
L_KDECAY = 67
L_LSE = 70
L_D = 65
L_QNORM = 73
L_KNORM = 74
NORM_SLACK = 1.02
SHIFT_MAX = 40.0

ADAM_LR = 0.001
ADAM_B1 = 0.9
ADAM_B2 = 0.999
ADAM_EPS = 1e-08
ADAM_WD = 0.01
ADAM_STEP = 10


def _params(sem, vmem=VMEM_LIMIT):
    return pltpu.CompilerParams(dimension_semantics=sem, vmem_limit_bytes=vmem)


def _sigmoid(x):
    return 1.0 / (1.0 + jnp.exp(-x))


def _split3(x):
    hi = x.astype(BF16).astype(F32)
    r = x - hi
    mid = r.astype(BF16).astype(F32)
    lo = (r - mid).astype(BF16).astype(F32)
    return hi, mid, lo


def _dot(a, b):
    return jnp.dot(a, b, preferred_element_type=F32)


def _dot_nt(a, b):
    return lax.dot_general(a, b, (((1,), (1,)), ((), ())), preferred_element_type=F32)


def _dot_tn(a, b):
    return lax.dot_general(a, b, (((0,), (0,)), ((), ())), preferred_element_type=F32)


def _lane(shape):
    return lax.broadcasted_iota(jnp.int32, shape, 1)


def _lane_col(x, lane, idx):
    return jnp.sum(jnp.where(lane == idx, x, 0.0), axis=-1, keepdims=True)


def _put3(base, lane, start, pieces):
    out = base
    for k, p in enumerate(pieces):
        out = jnp.where(lane == start + k, p, out)
    return out


def _half_stats(t, lo):
    s0 = jnp.sum(jnp.where(lo, t, 0.0), axis=-1, keepdims=True)
    s1 = jnp.sum(jnp.where(lo, 0.0, t), axis=-1, keepdims=True)
    return jnp.where(lo, s0, s1)


def _fwd_in(x2, g, w_t, tm, tn):
    T, D = x2.shape
    N = w_t.shape[0]

    def body(x_ref, g_ref, w_ref, z_ref, h_ref):
        @pl.when(pl.program_id(1) == 0)
        def _():
            x = x_ref[...]
            r = lax.rsqrt(jnp.mean(x * x, axis=-1, keepdims=True) + EPS)
            h_ref[...] = (x * r * g_ref[...]).astype(BF16)

        z_ref[...] = _dot_nt(h_ref[...], w_ref[...])

    return pl.pallas_call(
        body, name="fwd_in", grid=(T // tm, N // tn),
        in_specs=[pl.BlockSpec((tm, D), lambda i, j: (i, 0)),
                  pl.BlockSpec((1, D), lambda i, j: (0, 0)),
                  pl.BlockSpec((tn, D), lambda i, j: (j, 0))],
        out_specs=[pl.BlockSpec((tm, tn), lambda i, j: (i, j)),
                   pl.BlockSpec((tm, D), lambda i, j: (i, 0))],
        out_shape=[SDS((T, N), F32), SDS((T, D), BF16)],
        compiler_params=_params(("parallel", "arbitrary")),
    )(x2, g, w_t)


def _tri_cumsum(x, reverse):
    t = x.shape[0]
    row = lax.broadcasted_iota(jnp.int32, (t, t), 0)
    col = lax.broadcasted_iota(jnp.int32, (t, t), 1)
    tri = (row <= col) if reverse else (row >= col)
    tri = jnp.where(tri, 1.0, 0.0).astype(BF16)
    hi, mid, lo = _split3(x)
    return _dot(tri, hi.astype(BF16)) + _dot(tri, mid.astype(BF16)) + _dot(tri, lo.astype(BF16))


def _gate_fwd(z, b_pad, B, S, H, col_blk, tc):
    T = B * S
    nsb = S // tc

    def body(zf_ref, b_ref, c_ref, carry):
        @pl.when(pl.program_id(1) == 0)
        def _():
            carry[...] = jnp.zeros_like(carry)

        x = zf_ref[...] + b_ref[...]
        lf = jnp.minimum(x, 0.0) - jnp.log(1.0 + jnp.exp(-jnp.abs(x)))
        lf = jnp.where(_lane(lf.shape) < H, lf, 0.0)
        c_ref[...] = _tri_cumsum(lf, False) + carry[...]
        carry[...] = carry[...] + jnp.sum(lf, axis=0, keepdims=True)

    return pl.pallas_call(
        body, name="gate_fwd", grid=(B, nsb),
        in_specs=[pl.BlockSpec((tc, LANES), lambda b, s: (b * nsb + s, col_blk)),
                  pl.BlockSpec((1, LANES), lambda b, s: (0, 0))],
        out_specs=pl.BlockSpec((tc, LANES), lambda b, s: (b * nsb + s, 0)),
        out_shape=SDS((T, LANES), F32),
        scratch_shapes=[pltpu.VMEM((1, LANES), F32)],
        compiler_params=_params(("parallel", "arbitrary")),
    )(z, b_pad)


def _qk_normalize(x, g, lo):
    r = lax.rsqrt(_half_stats(x * x, lo) * (1.0 / HEAD_DIM) + EPS)
    return x * r * g


def _head_norms(x, lo):
    xx = x * x
    sums = (jnp.sum(jnp.where(lo, xx, 0.0), axis=-1, keepdims=True),
            jnp.sum(jnp.where(lo, 0.0, xx), axis=-1, keepdims=True))
    return [jnp.sqrt(t) * NORM_SLACK for t in sums]


def _attn_prep(z, c, gq, gk, B, S, H, tm):
    T = B * S
    FW = H * HEAD_DIM
    nsb = S // tm
    nfb = FW // LANES
    scale = HEAD_DIM ** -0.5

    def body(zq_ref, zk_ref, zv_ref, c_ref, gq_ref, gk_ref, qa_ref, ka_ref, va_ref):
        p = pl.program_id(1)
        lane = _lane((tm, LANES))
        lo = lane < HEAD_DIM
        qn = _qk_normalize(zq_ref[...], gq_ref[...], lo) * scale
        kn = _qk_normalize(zk_ref[...], gk_ref[...], lo)
        v = zv_ref[...]
        cc = c_ref[...]
        ones_q = ((lane >= L_KDECAY) & (lane < L_KDECAY + 3)).astype(F32)
        ones_k = (((lane >= L_ROWSUM) & (lane < L_ROWSUM + 3)) | ((lane >= L_LSE) & (lane < L_LSE + 3))).astype(F32)
        ones_v = ((lane >= L_ROWSUM) & (lane < L_D + 3)).astype(F32)
        q_norms, k_norms = _head_norms(qn, lo), _head_norms(kn, lo)
        for e in range(2):
            if e == 0:
                qe, ke, ve = qn, kn, v
            else:
                qe, ke, ve = (pltpu.roll(t, HEAD_DIM, 1) for t in (qn, kn, v))
            ch = _lane_col(cc, lane, 2 * p + e)
            pieces = _split3(ch)
            qa = jnp.where(lo, qe, _put3(ones_q, lane, L_ROWSUM, pieces))
            qa = jnp.where(lane == L_QNORM, q_norms[e], qa)
            ka = jnp.where(lo, ke, _put3(ones_k, lane, L_KDECAY, [-t for t in pieces]))
            ka = jnp.where(lane == L_KNORM, k_norms[e], ka)
            va = jnp.where(lo, ve, ones_v)
            qa_ref[0, e] = qa.astype(BF16)
            ka_ref[0, e] = ka.astype(BF16)
            va_ref[0, e] = va.astype(BF16)

    zspec = lambda off: pl.BlockSpec((tm, LANES), lambda i, p: (i, off + p))
    gspec = pl.BlockSpec((1, LANES), lambda i, p: (0, p))
    ospec = pl.BlockSpec((1, 2, tm, LANES), lambda i, p: (i // nsb, p, i % nsb, 0))
    oshape = SDS((B, H, S, LANES), BF16)
    return pl.pallas_call(
        body, name="attn_prep", grid=(T // tm, H // 2),
        in_specs=[zspec(0), zspec(nfb), zspec(2 * nfb),
                  pl.BlockSpec((tm, LANES), lambda i, p: (i, 0)), gspec, gspec],
        out_specs=[ospec, ospec, ospec],
        out_shape=[oshape, oshape, oshape],
        compiler_params=_params(("parallel", "arbitrary")),
    )(z, z, z, c, gq, gk)


def _attn_fwd(qa, ka, va, t, hb):
    B, H, S, _ = qa.shape
    nq = S // t

    def body(q_ref, k_ref, v_ref, o_ref, m_ref, acc_ref, kmax_ref, qs_ref):
        i = pl.program_id(2)
        lane = _lane((t, LANES))

        @pl.when(i == 0)
        def _():
            for e in range(hb):
                norms = jnp.where(_lane((S, LANES)) == L_KNORM, k_ref[0, e].astype(F32), 0.0)
                kmax_ref[e] = jnp.full((1, LANES), jnp.max(norms), F32)

        shifts = [_lane_col(q_ref[0, e].astype(F32), lane, L_QNORM) * kmax_ref[e][:, 0:1] for e in range(hb)]
        worst = shifts[0]
        for e in range(1, hb):
            worst = jnp.maximum(worst, shifts[e])
        bounded = jnp.max(worst) <= SHIFT_MAX
        acc_ref[...] = jnp.zeros_like(acc_ref)

        def tiles(step):
            def loop_body(j, carry):
                step(j, False)
                return carry

            lax.fori_loop(0, i, loop_body, 0)
            step(i, True)

        def keep_mask():
            return lax.broadcasted_iota(jnp.int32, (t, t), 0) >= lax.broadcasted_iota(jnp.int32, (t, t), 1)

        def finish(e, shift):
            acc = acc_ref[e]
            l = _lane_col(acc, lane, L_ROWSUM)
            o_ref[0, e] = jnp.where(lane < HEAD_DIM, acc / l, shift + jnp.log(l))

        @pl.when(bounded)
        def _():
            for e in range(hb):
                qs_ref[e] = _put3(q_ref[0, e].astype(F32), lane, L_LSE, _split3(-shifts[e])).astype(BF16)

            def step(j, masked):
                rows = pl.ds(pl.multiple_of(j * t, t), t)
                for e in range(hb):
                    p = jnp.exp(_dot_nt(qs_ref[e], k_ref[0, e, rows, :]))
                    if masked:
                        p = jnp.where(keep_mask(), p, 0.0)
                    acc_ref[e] = acc_ref[e] + _dot(p.astype(BF16), v_ref[0, e, rows, :])

            tiles(step)
            for e in range(hb):
                finish(e, shifts[e])

        @pl.when(jnp.logical_not(bounded))
        def _():
            m_ref[...] = jnp.full_like(m_ref, NEG_INF)

            def step(j, masked):
                rows = pl.ds(pl.multiple_of(j * t, t), t)
                for e in range(hb):
                    s = _dot_nt(q_ref[0, e], k_ref[0, e, rows, :])
                    if masked:
                        s = jnp.where(keep_mask(), s, NEG_INF)
                    m_prev = m_ref[e]
                    m_new = jnp.maximum(m_prev, jnp.max(s, axis=-1, keepdims=True))
                    alpha = jnp.exp(m_prev - m_new)
                    p = jnp.exp(s - m_new).astype(BF16)
                    acc_ref[e] = alpha * acc_ref[e] + _dot(p, v_ref[0, e, rows, :])
                    m_ref[e] = m_new

            tiles(step)
            for e in range(hb):
                finish(e, m_ref[e])

    return pl.pallas_call(
        body, name="attn_fwd", grid=(B, H // hb, nq),
        in_specs=[pl.BlockSpec((1, hb, t, LANES), lambda b, h, i: (b, h, i, 0)),
                  pl.BlockSpec((1, hb, S, LANES), lambda b, h, i: (b, h, 0, 0)),
                  pl.BlockSpec((1, hb, S, LANES), lambda b, h, i: (b, h, 0, 0))],
        out_specs=pl.BlockSpec((1, hb, t, LANES), lambda b, h, i: (b, h, i, 0)),
        out_shape=SDS((B, H, S, LANES), F32),
        scratch_shapes=[pltpu.VMEM((hb, t, 1), F32), pltpu.VMEM((hb, t, LANES), F32),
                        pltpu.VMEM((hb, 1, LANES), F32), pltpu.VMEM((hb, t, LANES), BF16)],
        compiler_params=_params(("parallel", "parallel", "arbitrary")),
    )(qa, ka, va)


def _fill_shifts(ext_ref, sh_ref):
    rows = sh_ref.shape[1]
    for b in range(1, SUBLANES):
        sh_ref[b - 1] = ext_ref[pl.ds(b, rows), :]


def _tap_window(ext_ref, sh_ref, off, tm, cols):
    b = off % SUBLANES
    if b == 0:
        return ext_ref[pl.ds(off, tm), cols]
    return sh_ref[b - 1, pl.ds(off - b, tm), cols]


def _conv_taps(w_ref, ext_ref, sh_ref, out_ref, n_taps, tm, offset_of, bias_ref=None):
    for cc in range(out_ref.shape[1] // LANES):
        cols = slice(cc * LANES, (cc + 1) * LANES)
        acc = None
        for j in range(n_taps):
            term = w_ref[j:j + 1, cols] * _tap_window(ext_ref, sh_ref, offset_of(j), tm, cols)
            acc = term if acc is None else acc + term
        out_ref[:, cols] = acc if bias_ref is None else acc + bias_ref[:, cols]


def _layernorm_stats(u2):
    mu = jnp.mean(u2, axis=-1, keepdims=True)
    xc = u2 - mu
    rstd = lax.rsqrt(jnp.mean(xc * xc, axis=-1, keepdims=True) + EPS)
    return xc * rstd, rstd


def _fwd_out(oa, z, x2, tgt, conv_w, conv_b, ln_g, ln_b, w_out, B, S, H, n_taps, tm):
    T, D = x2.shape
    FW = H * HEAD_DIM
    CW = conv_w.shape[1]
    nsb = S // tm
    hb = tm // HALO

    def body(oa_ref, gf_ref, ga_ref, gb_ref, gc_ref, ha_ref, hb_ref, x_ref, t_ref, w_ref, cb_ref, lg_ref,
             lb_ref, wo_ref, y_ref, u2_ref, a_ref, do_ref, dob_ref, dy_ref, loss_ref, ext_ref, sh_ref):
        first_step = (pl.program_id(0) == 0) & (pl.program_id(1) == 0)

        @pl.when(first_step)
        def _():
            loss_ref[...] = jnp.zeros_like(loss_ref)

        u1 = ga_ref[...] * _sigmoid(gb_ref[...])
        halo = ha_ref[...] * _sigmoid(hb_ref[...])
        ext_ref[0:HALO, :] = jnp.where(pl.program_id(1) > 0, halo, 0.0)
        ext_ref[HALO:, :] = u1
        _fill_shifts(ext_ref, sh_ref)
        _conv_taps(w_ref, ext_ref, sh_ref, u2_ref, n_taps, tm, lambda j: HALO - (n_taps - 1) + j, cb_ref)
        uh, _ = _layernorm_stats(u2_ref[...])
        u3 = uh * lg_ref[...] + lb_ref[...]
        gc = gc_ref[...]
        yu = u3 * _sigmoid(u3) * (gc * _sigmoid(gc))
        y_ref[:, FW:] = yu.astype(BF16)

        lane = _lane((tm, LANES))
        lo = lane < HEAD_DIM
        for p in range(H // 2):
            a_ref[:, p * LANES:(p + 1) * LANES] = jnp.where(
                lo, oa_ref[0, 2 * p], pltpu.roll(oa_ref[0, 2 * p + 1], HEAD_DIM, 1))
        gf = gf_ref[...]
        y_ref[:, :FW] = (a_ref[...] * (gf * _sigmoid(gf))).astype(BF16)

        out = x_ref[...] + _dot(y_ref[...], wo_ref[...])
        diff = out - t_ref[...]
        loss_ref[...] = loss_ref[...] + jnp.sum(diff * diff)
        dout = diff * (1.0 / D)
        do_ref[...] = dout
        dob = dout.astype(BF16)
        dob_ref[...] = dob
        dy_ref[...] = _dot_nt(dob, wo_ref[...])

    row = lambda b, s: b * nsb + s
    zspec = lambda cb: pl.BlockSpec((tm, FW), lambda b, s: (row(b, s), cb))
    hspec = lambda cb: pl.BlockSpec((HALO, CW), lambda b, s: (jnp.maximum(row(b, s) * hb - 1, 0), cb))
    vspec = pl.BlockSpec((1, CW), lambda b, s: (0, 0))
    tspec = lambda w: pl.BlockSpec((tm, w), lambda b, s: (row(b, s), 0))
    return pl.pallas_call(
        body, name="fwd_out", grid=(B, nsb),
        in_specs=[pl.BlockSpec((1, H, tm, LANES), lambda b, s: (b, 0, s, 0)),
                  zspec(3), zspec(4), zspec(5), zspec(6), hspec(4), hspec(5),
                  tspec(D), tspec(D),
                  pl.BlockSpec((HALO, CW), lambda b, s: (0, 0)), vspec, vspec, vspec,
                  pl.BlockSpec((FW + CW, D), lambda b, s: (0, 0))],
        out_specs=[tspec(FW + CW), tspec(CW), tspec(FW), tspec(D), tspec(D), tspec(FW + CW),
                   pl.BlockSpec((8, LANES), lambda b, s: (0, 0))],
        out_shape=[SDS((T, FW + CW), BF16), SDS((T, CW), F32), SDS((T, FW), F32), SDS((T, D), F32),
                   SDS((T, D), BF16), SDS((T, FW + CW), F32), SDS((8, LANES), F32)],
        scratch_shapes=[pltpu.VMEM((tm + HALO, CW), F32),
                        pltpu.VMEM((SUBLANES - 1, tm + HALO - SUBLANES, CW), F32)],
        compiler_params=_params(("arbitrary", "arbitrary")),
    )(oa, z, z, z, z, z, z, x2, tgt, conv_w, conv_b, ln_g, ln_b, w_out)


def _bwd_prep(dy, z, a_nat, oa, qa, u2, ln_g, ln_b, B, S, H, tm):
    T = B * S
    FW = H * HEAD_DIM
    CW = u2.shape[1]
    nsb = S // tm

    def body(dya_ref, dyu_ref, gf_ref, gc_ref, a_ref, oa_ref, qa_ref, u2_ref, lg_ref, lb_ref,
             dzgf_ref, dzgc_ref, du2_ref, doa_ref, qb_ref, sg_ref):
        first_step = (pl.program_id(0) == 0) & (pl.program_id(1) == 0)

        @pl.when(first_step)
        def _():
            sg_ref[...] = jnp.zeros_like(sg_ref)

        gf = gf_ref[...]
        sg = _sigmoid(gf)
        a = a_ref[...]
        dya = dya_ref[...]
        da = dya * (gf * sg)
        dzgf_ref[...] = (dya * a * (sg * (1.0 + gf * (1.0 - sg)))).astype(BF16)
        dd = da * a
        lane = _lane((tm, LANES))
        lo = lane < HEAD_DIM
        for p in range(H // 2):
            cols = slice(p * LANES, (p + 1) * LANES)
            da_p = da[:, cols]
            dd_p = dd[:, cols]
            d_heads = (jnp.sum(jnp.where(lo, dd_p, 0.0), axis=-1, keepdims=True),
                       jnp.sum(jnp.where(lo, 0.0, dd_p), axis=-1, keepdims=True))
            for e in range(2):
                da_e = da_p if e == 0 else pltpu.roll(da_p, HEAD_DIM, 1)
                d_e = d_heads[e]
                aug = _put3(jnp.zeros((tm, LANES), F32), lane, L_D, _split3(-d_e))
                doa_ref[0, 2 * p + e] = jnp.where(lo, da_e, aug).astype(BF16)
                lse = _lane_col(oa_ref[0, 2 * p + e], lane, L_ROWSUM)
                qb = _put3(qa_ref[0, 2 * p + e].astype(F32), lane, L_LSE, _split3(-lse))
                qb_ref[0, 2 * p + e] = qb.astype(BF16)

        gc = gc_ref[...]
        sc = _sigmoid(gc)
        dyu = dyu_ref[...]
        uh, rstd = _layernorm_stats(u2_ref[...])
        u3 = uh * lg_ref[...] + lb_ref[...]
        s3 = _sigmoid(u3)
        dzgc_ref[...] = (dyu * (u3 * s3) * (sc * (1.0 + gc * (1.0 - sc)))).astype(BF16)
        du3 = dyu * (gc * sc) * (s3 * (1.0 + u3 * (1.0 - s3)))
        sg_ref[0:1, :] = sg_ref[0:1, :] + jnp.sum(du3 * uh, axis=0, keepdims=True)
        sg_ref[1:2, :] = sg_ref[1:2, :] + jnp.sum(du3, axis=0, keepdims=True)
        duh = du3 * lg_ref[...]
        du2 = rstd * (duh - jnp.mean(duh, axis=-1, keepdims=True)
                      - uh * jnp.mean(duh * uh, axis=-1, keepdims=True))
        sg_ref[2:3, :] = sg_ref[2:3, :] + jnp.sum(du2, axis=0, keepdims=True)
        du2_ref[...] = du2

    row = lambda b, s: b * nsb + s
    tspec = lambda w, cb=0: pl.BlockSpec((tm, w), lambda b, s: (row(b, s), cb))
    hspec = pl.BlockSpec((1, H, tm, LANES), lambda b, s: (b, 0, s, 0))
    vspec = pl.BlockSpec((1, CW), lambda b, s: (0, 0))
    return pl.pallas_call(
        body, name="bwd_prep", grid=(B, nsb),
        in_specs=[tspec(FW, 0), tspec(CW, 1), tspec(FW, 3), tspec(CW, 6), tspec(FW), hspec, hspec,
                  tspec(CW), vspec, vspec],
        out_specs=[tspec(FW), tspec(CW), tspec(CW), hspec, hspec,
                   pl.BlockSpec((8, CW), lambda b, s: (0, 0))],
        out_shape=[SDS((T, FW), BF16), SDS((T, CW), BF16), SDS((T, CW), F32),
                   SDS((B, H, S, LANES), BF16), SDS((B, H, S, LANES), BF16), SDS((8, CW), F32)],
        compiler_params=_params(("arbitrary", "arbitrary")),
    )(dy, dy, z, z, a_nat, oa, qa, u2, ln_g, ln_b)


def _conv_bwd(du2, z, conv_w, B, S, n_taps, tm):
    T, CW = du2.shape
    nsb = S // tm
    hb = tm // HALO

    def body(d_ref, dh_ref, ga_ref, gb_ref, ha_ref, hb_ref, w_ref, dz_ref, dw_ref,
             extu_ref, extd_ref, shu_ref, shd_ref, du1_ref, dwacc_ref):
        s = pl.program_id(1)
        first_step = (pl.program_id(0) == 0) & (s == 0)
        last_step = (pl.program_id(0) == B - 1) & (s == nsb - 1)

        @pl.when(first_step)
        def _():
            dwacc_ref[...] = jnp.zeros_like(dwacc_ref)

        ga = ga_ref[...]
        sb = _sigmoid(gb_ref[...])
        halo = ha_ref[...] * _sigmoid(hb_ref[...])
        extu_ref[0:HALO, :] = jnp.where(s > 0, halo, 0.0)
        extu_ref[HALO:, :] = ga * sb
        extd_ref[0:tm, :] = d_ref[...]
        extd_ref[tm:, :] = jnp.where(s < nsb - 1, dh_ref[...], 0.0)
        _fill_shifts(extu_ref, shu_ref)
        _fill_shifts(extd_ref, shd_ref)
        _conv_taps(w_ref, extd_ref, shd_ref, du1_ref, n_taps, tm, lambda j: n_taps - 1 - j)
        for cc in range(CW // LANES):
            cols = slice(cc * LANES, (cc + 1) * LANES)
            parts = [None] * n_taps
            for r in range(tm // SUBLANES):
                dv = d_ref[r * SUBLANES:(r + 1) * SUBLANES, cols]
                for j in range(n_taps):
                    off = HALO - (n_taps - 1) + j + r * SUBLANES
                    term = dv * _tap_window(extu_ref, shu_ref, off, SUBLANES, cols)
                    parts[j] = term if parts[j] is None else parts[j] + term
            for j in range(n_taps):
                rows = slice(j * SUBLANES, (j + 1) * SUBLANES)
                dwacc_ref[rows, cols] = dwacc_ref[rows, cols] + parts[j]
        du1 = du1_ref[...]
        dz_ref[:, :CW] = (du1 * sb).astype(BF16)
        dz_ref[:, CW:] = (du1 * ga * (sb * (1.0 - sb))).astype(BF16)

        @pl.when(last_step)
        def _():
            dw_ref[...] = jnp.zeros_like(dw_ref)
            for j in range(n_taps):
                dw_ref[j:j + 1, :] = jnp.sum(dwacc_ref[j * SUBLANES:(j + 1) * SUBLANES, :], axis=0, keepdims=True)

    row = lambda b, s: b * nsb + s
    last_halo = T // HALO - 1
    return pl.pallas_call(
        body, name="conv_bwd", grid=(B, nsb),
        in_specs=[pl.BlockSpec((tm, CW), lambda b, s: (row(b, s), 0)),
                  pl.BlockSpec((HALO, CW), lambda b, s: (jnp.minimum((row(b, s) + 1) * hb, last_halo), 0)),
                  pl.BlockSpec((tm, CW), lambda b, s: (row(b, s), 4)),
                  pl.BlockSpec((tm, CW), lambda b, s: (row(b, s), 5)),
                  pl.BlockSpec((HALO, CW), lambda b, s: (jnp.maximum(row(b, s) * hb - 1, 0), 4)),
                  pl.BlockSpec((HALO, CW), lambda b, s: (jnp.maximum(row(b, s) * hb - 1, 0), 5)),
                  pl.BlockSpec((HALO, CW), lambda b, s: (0, 0))],
        out_specs=[pl.BlockSpec((tm, 2 * CW), lambda b, s: (row(b, s), 0)),
                   pl.BlockSpec((HALO, CW), lambda b, s: (0, 0))],
        out_shape=[SDS((T, 2 * CW), BF16), SDS((HALO, CW), F32)],
        scratch_shapes=[pltpu.VMEM((tm + HALO, CW), F32), pltpu.VMEM((tm + HALO, CW), F32),
                        pltpu.VMEM((SUBLANES - 1, tm + HALO - SUBLANES, CW), F32),
                        pltpu.VMEM((SUBLANES - 1, tm + HALO - SUBLANES, CW), F32),
                        pltpu.VMEM((tm, CW), F32), pltpu.VMEM((HALO * SUBLANES, CW), F32)],
        compiler_params=_params(("arbitrary", "arbitrary")),
    )(du2, du2, z, z, z, z, conv_w)


def _attn_bwd(qb, ka, va, doa, t, hb):
    B, H, S, _ = qb.shape
    nk = S // t

    def body(q_ref, k_ref, v_ref, do_ref, dq_ref, dk_ref, dv_ref):
        j = pl.program_id(2)

        @pl.when(j == 0)
        def _():
            dq_ref[...] = jnp.zeros_like(dq_ref)

        dk_ref[...] = jnp.zeros_like(dk_ref)
        dv_ref[...] = jnp.zeros_like(dv_ref)

        def step(i, masked):
            rows = pl.ds(pl.multiple_of(i * t, t), t)
            if masked:
                keep = lax.broadcasted_iota(jnp.int32, (t, t), 0) >= lax.broadcasted_iota(jnp.int32, (t, t), 1)
            for e in range(hb):
                k = k_ref[0, e]
                q = q_ref[0, e, rows, :]
                do = do_ref[0, e, rows, :]
                p = jnp.exp(_dot_nt(q, k))
                if masked:
                    p = jnp.where(keep, p, 0.0)
                ds = (p * _dot_nt(do, v_ref[0, e])).astype(BF16)
                dv_ref[0, e] = dv_ref[0, e] + _dot_tn(p.astype(BF16), do)
                dk_ref[0, e] = dk_ref[0, e] + _dot_tn(ds, q)
                dq_ref[0, e, rows, :] = dq_ref[0, e, rows, :] + _dot(ds, k)

        step(j, True)

        def loop_body(i, carry):
            step(i, False)
            return carry

        lax.fori_loop(j + 1, nk, loop_body, 0)

    full = pl.BlockSpec((1, hb, S, LANES), lambda b, h, j: (b, h, 0, 0))
    blk = pl.BlockSpec((1, hb, t, LANES), lambda b, h, j: (b, h, j, 0))
    oshape = SDS((B, H, S, LANES), F32)
    return pl.pallas_call(
        body, name="attn_bwd", grid=(B, H // hb, nk),
        in_specs=[full, blk, blk, full],
        out_specs=[full, blk, blk],
        out_shape=[oshape, oshape, oshape],
        compiler_params=_params(("parallel", "parallel", "arbitrary")),
    )(qb, ka, va, doa)


def _qk_bwd(dqa, dka, dva, z, gq, gk, B, S, H, tm):
    T = B * S
    FW = H * HEAD_DIM
    nsb = S // tm
    nfb = FW // LANES
    scale = HEAD_DIM ** -0.5

    def body(dq_ref, dk_ref, dv_ref, zq_ref, zk_ref, gq_ref, gk_ref, dzq_ref, dzk_ref, dzv_ref, dc_ref, dg_ref):
        p = pl.program_id(0)

        @pl.when(pl.program_id(1) == 0)
        def _():
            dg_ref[...] = jnp.zeros_like(dg_ref)

        lane = _lane((tm, LANES))
        lo = lane < HEAD_DIM

        def natural(ref):
            return jnp.where(lo, ref[0, 0], pltpu.roll(ref[0, 1], HEAD_DIM, 1))

        def norm_bwd(dn, x, g, row, out_ref):
            r = lax.rsqrt(_half_stats(x * x, lo) * (1.0 / HEAD_DIM) + EPS)
            xh = x * r
            dg_ref[row:row + 1, :] = dg_ref[row:row + 1, :] + jnp.sum(dn * xh, axis=0, keepdims=True)
            dxh = dn * g
            mm = _half_stats(dxh * xh, lo) * (1.0 / HEAD_DIM)
            out_ref[...] = (r * (dxh - xh * mm)).astype(BF16)

        norm_bwd(natural(dq_ref) * scale, zq_ref[...], gq_ref[...], 0, dzq_ref)
        norm_bwd(natural(dk_ref), zk_ref[...], gk_ref[...], 1, dzk_ref)
        dzv_ref[...] = natural(dv_ref).astype(BF16)

        dc = jnp.zeros((tm, LANES), F32)
        for e in range(2):
            val = _lane_col(dq_ref[0, e], lane, L_ROWSUM) - _lane_col(dk_ref[0, e], lane, L_KDECAY)
            dc = jnp.where(lane == 2 * p + e, val, dc)
        dc_ref[0] = dc

    hspec = pl.BlockSpec((1, 2, tm, LANES), lambda p, i: (i // nsb, p, i % nsb, 0))
    zspec = lambda off: pl.BlockSpec((tm, LANES), lambda p, i: (i, off + p))
    gspec = pl.BlockSpec((1, LANES), lambda p, i: (0, p))
    ospec = pl.BlockSpec((tm, LANES), lambda p, i: (i, p))
    return pl.pallas_call(
        body, name="qk_bwd", grid=(H // 2, T // tm),
        in_specs=[hspec, hspec, hspec, zspec(0), zspec(nfb), gspec, gspec],
        out_specs=[ospec, ospec, ospec,
                   pl.BlockSpec((1, tm, LANES), lambda p, i: (p, i, 0)),
                   pl.BlockSpec((8, LANES), lambda p, i: (0, p))],
        out_shape=[SDS((T, FW), BF16), SDS((T, FW), BF16), SDS((T, FW), BF16),
                   SDS((H // 2, T, LANES), F32), SDS((8, FW), F32)],
        compiler_params=_params(("parallel", "arbitrary")),
    )(dqa, dka, dva, z, z, gq, gk)


def _gate_bwd(dc8, z, b_pad, B, S, H, col_blk, fp, tc):
    T = B * S
    nsb = S // tc
    npair = dc8.shape[0]

    def body(dc_ref, zf_ref, b_ref, dz_ref, db_ref, carry):
        first_step = (pl.program_id(0) == 0) & (pl.program_id(1) == 0)

        @pl.when(first_step)
        def _():
            db_ref[...] = jnp.zeros_like(db_ref)

        @pl.when(pl.program_id(1) == 0)
        def _():
            carry[...] = jnp.zeros_like(carry)

        dc = dc_ref[0]
        for k in range(1, npair):
            dc = dc + dc_ref[k]
        dlf = _tri_cumsum(dc, True) + carry[...]
        carry[...] = carry[...] + jnp.sum(dc, axis=0, keepdims=True)
        x = zf_ref[...] + b_ref[...]
        dlogit = dlf * _sigmoid(-x)
        db_ref[0:1, :] = db_ref[0:1, :] + jnp.sum(dlogit, axis=0, keepdims=True)
        dz_ref[...] = jnp.zeros_like(dz_ref)
        dz_ref[:, :LANES] = dlogit.astype(BF16)

    rrow = lambda b, s: b * nsb + (nsb - 1 - s)
    return pl.pallas_call(
        body, name="gate_bwd", grid=(B, nsb),
        in_specs=[pl.BlockSpec((npair, tc, LANES), lambda b, s: (0, rrow(b, s), 0)),
                  pl.BlockSpec((tc, LANES), lambda b, s: (rrow(b, s), col_blk)),
                  pl.BlockSpec((1, LANES), lambda b, s: (0, 0))],
        out_specs=[pl.BlockSpec((tc, fp), lambda b, s: (rrow(b, s), 0)),
                   pl.BlockSpec((8, LANES), lambda b, s: (0, 0))],
        out_shape=[SDS((T, fp), BF16), SDS((8, LANES), F32)],
        scratch_shapes=[pltpu.VMEM((1, LANES), F32)],
        compiler_params=_params(("arbitrary", "arbitrary")),
    )(dc8, z, b_pad)


def _matmul_tn(a, b, name, tmm, tn, tk):
    T, M = a.shape
    N = b.shape[1]
    tmm, tn, tk = min(tmm, M), min(tn, N), min(tk, T)

    def body(a_ref, b_ref, o_ref):
        @pl.when(pl.program_id(2) == 0)
        def _():
            o_ref[...] = jnp.zeros_like(o_ref)

        o_ref[...] = o_ref[...] + _dot_tn(a_ref[...], b_ref[...])

    return pl.pallas_call(
        body, name=name, grid=(M // tmm, N // tn, T // tk),
        in_specs=[pl.BlockSpec((tk, tmm), lambda i, j, k: (k, i)),
                  pl.BlockSpec((tk, tn), lambda i, j, k: (k, j))],
        out_specs=pl.BlockSpec((tmm, tn), lambda i, j, k: (i, j)),
        out_shape=SDS((M, N), F32),
        compiler_params=_params(("parallel", "parallel", "arbitrary")),
    )(a, b)


def _dh_rms_bwd(pieces, w_t, x2, g, dout, tm, tk):
    T, D = x2.shape
    nks = [p.shape[1] // tk for p in pieces]
    starts = [sum(nks[:k]) for k in range(len(pieces))]
    nk = sum(nks)

    def body(*refs):
        dz_refs = refs[:len(pieces)]
        w_ref, x_ref, g_ref, do_ref, gx_ref, dg_ref, acc_ref = refs[len(pieces):]
        k = pl.program_id(1)
        first_step = (pl.program_id(0) == 0) & (k == 0)

        @pl.when(first_step)
        def _():
            dg_ref[...] = jnp.zeros_like(dg_ref)

        @pl.when(k == 0)
        def _():
            acc_ref[...] = jnp.zeros_like(acc_ref)

        for dz_ref, st, n in zip(dz_refs, starts, nks):
            @pl.when((k >= st) & (k < st + n))
            def _(dz_ref=dz_ref):
                acc_ref[...] = acc_ref[...] + _dot(dz_ref[...], w_ref[...])

        @pl.when(k == nk - 1)
        def _():
            x = x_ref[...]
            r = lax.rsqrt(jnp.mean(x * x, axis=-1, keepdims=True) + EPS)
            xh = x * r
            dh = acc_ref[...]
            dg_ref[0:1, :] = dg_ref[0:1, :] + jnp.sum(dh * xh, axis=0, keepdims=True)
            dxn = dh * g_ref[...]
            gx_ref[...] = do_ref[...] + r * (dxn - xh * jnp.mean(dxn * xh, axis=-1, keepdims=True))

    def piece_spec(st, n):
        return pl.BlockSpec((tm, tk), lambda i, k: (i, jnp.clip(k - st, 0, n - 1)))

    tspec = pl.BlockSpec((tm, D), lambda i, k: (i, 0))
    return pl.pallas_call(
        body, name="dh_rms_bwd", grid=(T // tm, nk),
        in_specs=[piece_spec(st, n) for st, n in zip(starts, nks)]
        + [pl.BlockSpec((tk, D), lambda i, k: (k, 0)), tspec, pl.BlockSpec((1, D), lambda i, k: (0, 0)), tspec],
        out_specs=[tspec, pl.BlockSpec((8, D), lambda i, k: (0, 0))],
        out_shape=[SDS((T, D), F32), SDS((8, D), F32)],
        scratch_shapes=[pltpu.VMEM((tm, D), F32)],
        compiler_params=_params(("arbitrary", "arbitrary")),
    )(*pieces, w_t, x2, g, dout)


def _block_plan(R, C, tr, tc):
    br = min(tr, R)
    if R % br == 0:
        return (br, C), R // br, lambda i: (i, 0)
    bc = min(tc, C)
    assert C % bc == 0
    return (R, bc), C // bc, lambda i: (0, i)


def _ew_call(body, name, ins, n_out, out_dtypes, tr, tc):
    R, C = ins[0].shape
    blk, steps, imap = _block_plan(R, C, tr, tc)
    spec = pl.BlockSpec(blk, imap)
    return pl.pallas_call(
        body, name=name, grid=(steps,),
        in_specs=[spec] * len(ins), out_specs=[spec] * n_out,
        out_shape=[SDS((R, C), dt) for dt in out_dtypes],
        compiler_params=_params(("parallel",)),
    )(*ins)


def _pair_sum_bf16(a, b, name):
    def body(a_ref, b_ref, o_ref):
        o_ref[...] = (a_ref[...] + b_ref[...]).astype(BF16)

    return _ew_call(body, name, [a, b], 1, [BF16], 256, LANES)[0]


def _sum_slots(slots, name, first=None, tr=256):
    n, R, C = slots.shape
    blk, steps, imap = _block_plan(R, C, tr, 2 * LANES)
    lead = [] if first is None else [first]

    def body(*refs):
        s_ref, o_ref = refs[-2:]
        acc = refs[0][...].astype(F32) if lead else s_ref[0].astype(F32)
        for k in range(0 if lead else 1, n):
            acc = acc + s_ref[k].astype(F32)
        o_ref[...] = acc

    return pl.pallas_call(
        body, name=name, grid=(steps,),
        in_specs=[pl.BlockSpec(blk, imap)] * len(lead) + [pl.BlockSpec((n,) + blk, lambda i: (0,) + imap(i))],
        out_specs=pl.BlockSpec(blk, imap),
        out_shape=SDS((R, C), F32),
        compiler_params=_params(("parallel",)),
    )(*lead, slots)


def _adamw(w, g, m, v, name):
    def body(w_ref, g_ref, m_ref, v_ref, d_ref, nm_ref, nv_ref):
        gg = g_ref[...]
        nm = ADAM_B1 * m_ref[...] + (1.0 - ADAM_B1) * gg
        nv = ADAM_B2 * v_ref[...] + (1.0 - ADAM_B2) * (gg * gg)
        m_hat = nm / (1.0 - ADAM_B1 ** ADAM_STEP)
        v_hat = nv / (1.0 - ADAM_B2 ** ADAM_STEP)
        d_ref[...] = -ADAM_LR * (m_hat / (jnp.sqrt(v_hat) + ADAM_EPS) + ADAM_WD * w_ref[...])
        nm_ref[...] = nm
        nv_ref[...] = nv

    return _ew_call(body, name, [w, g, m, v], 3, [F32, F32, F32], 128, 2 * LANES)


ANY = pl.BlockSpec(memory_space=pl.ANY)


def _place():
    return lax.axis_index("x"), lax.axis_index("y"), lax.axis_index("c")


def _gather_chips(shards):
    n = len(shards)
    per = 7

    def body(*refs):
        ins, outs = refs[:n], refs[n:2 * n]
        send_sems, recv_sems = refs[2 * n:]
        x, y, c = _place()
        mine = 2 * x + y
        me, sibling = (x, y, c), (x, y, 1 - c)
        chips = [(1 - x, y), (x, 1 - y), (1 - x, 1 - y)]

        def copy(a, k, chip_idx, half, to, src=None):
            dst = outs[a].at[chip_idx, half]
            return pltpu.make_async_remote_copy(
                src_ref=dst if src is None else src, dst_ref=dst,
                send_sem=send_sems.at[a * per + k], recv_sem=recv_sems.at[a * per + k],
                device_id=to, device_id_type=MESH)

        def own(a, to):
            return pltpu.make_async_remote_copy(
                src_ref=ins[a], dst_ref=outs[a].at[mine],
                send_sem=send_sems.at[a * per + 6], recv_sem=recv_sems.at[a * per + 6],
                device_id=to, device_id_type=MESH)

        first = [copy(a, k, mine, c, (*chip, c), src=ins[a].at[c]) for a in range(n) for k, chip in enumerate(chips)]
        first += [own(a, sibling) for a in range(n)]
        for cp in first:
            cp.start()
        passed = []
        for k, (cx, cy) in enumerate(chips):
            for a in range(n):
                copy(a, k, 2 * cx + cy, c, me).wait_recv()
                fwd = copy(a, 3 + k, 2 * cx + cy, c, sibling)
                fwd.start()
                passed.append(fwd)
        for k, (cx, cy) in enumerate(chips):
            for a in range(n):
                copy(a, 3 + k, 2 * cx + cy, 1 - c, me).wait_recv()
        for a in range(n):
            own(a, me).wait_recv()
        for cp in first + passed:
            cp.wait_send()

    return pl.pallas_call(
        body, name="gather_chips",
        in_specs=[ANY] * n, out_specs=[ANY] * n,
        out_shape=[SDS((4,) + s.shape, s.dtype) for s in shards],
        scratch_shapes=[pltpu.SemaphoreType.DMA((per * n,)), pltpu.SemaphoreType.DMA((per * n,))],
    )(*shards)


def _pair_swap(halves):
    n = len(halves)

    def body(*refs):
        ins, outs = refs[:n], refs[n:2 * n]
        send_sems, recv_sems = refs[2 * n:]
        x, y, c = _place()
        copies = [pltpu.make_async_remote_copy(
            src_ref=ins[a].at[1 - c], dst_ref=outs[a], send_sem=send_sems.at[a], recv_sem=recv_sems.at[a],
            device_id=(x, y, 1 - c), device_id_type=MESH) for a in range(n)]
        for cp in copies:
            cp.start()
        for cp in copies:
            cp.wait()

    return pl.pallas_call(
        body, name="pair_swap",
        in_specs=[ANY] * n, out_specs=[ANY] * n,
        out_shape=[SDS(h.shape[1:], h.dtype) for h in halves],
        scratch_shapes=[pltpu.SemaphoreType.DMA((n,)), pltpu.SemaphoreType.DMA((n,))],
    )(*halves)


def _chip_exchange(parts):
    n = len(parts)

    def body(*refs):
        ins, outs = refs[:n], refs[n:2 * n]
        send_sems, recv_sems = refs[2 * n:]
        x, y, c = _place()
        chips = [(1 - x, y), (x, 1 - y), (1 - x, 1 - y)]

        def copy(a, k, to):
            cx, cy = chips[k]
            return pltpu.make_async_remote_copy(
                src_ref=ins[a].at[2 * cx + cy], dst_ref=outs[a].at[k],
                send_sem=send_sems.at[a * 3 + k], recv_sem=recv_sems.at[a * 3 + k],
                device_id=to, device_id_type=MESH)

        sends = [copy(a, k, (*chips[k], c)) for a in range(n) for k in range(3)]
        for cp in sends:
            cp.start()
        for a in range(n):
            for k in range(3):
                copy(a, k, (x, y, c)).wait_recv()
        for cp in sends:
            cp.wait_send()

    return pl.pallas_call(
        body, name="chip_exchange",
        in_specs=[ANY] * n, out_specs=[ANY] * n,
        out_shape=[SDS((3,) + p.shape[1:], p.dtype) for p in parts],
        scratch_shapes=[pltpu.SemaphoreType.DMA((3 * n,)), pltpu.SemaphoreType.DMA((3 * n,))],
    )(*parts)


def _pair_send(arrs):
    n = len(arrs)

    def body(*refs):
        ins, outs = refs[:n], refs[n:2 * n]
        send_sems, recv_sems = refs[2 * n:]
        x, y, c = _place()
        copies = [pltpu.make_async_remote_copy(
            src_ref=ins[a], dst_ref=outs[a], send_sem=send_sems.at[a], recv_sem=recv_sems.at[a],
            device_id=(x, y, 1 - c), device_id_type=MESH) for a in range(n)]
        for cp in copies:
            cp.start()
        for cp in copies:
            cp.wait()

    return pl.pallas_call(
        body, name="pair_send",
        in_specs=[ANY] * n, out_specs=[ANY] * n,
        out_shape=[SDS(h.shape, h.dtype) for h in arrs],
        scratch_shapes=[pltpu.SemaphoreType.DMA((n,)), pltpu.SemaphoreType.DMA((n,))],
    )(*arrs)


def _gather_all(buf):
    flips = [(fx, fy, fc) for fx in (0, 1) for fy in (0, 1) for fc in (0, 1)][1:]

    def body(in_ref, out_ref, send_sems, recv_sems, local_sem):
        x, y, c = _place()
        me = 4 * x + 2 * y + c
        local = pltpu.make_async_copy(in_ref, out_ref.at[me], local_sem)
        local.start()
        sends = []
        for k, (fx, fy, fc) in enumerate(flips):
            cp = pltpu.make_async_remote_copy(
                src_ref=in_ref, dst_ref=out_ref.at[me], send_sem=send_sems.at[k], recv_sem=recv_sems.at[k],
                device_id=(x ^ fx, y ^ fy, c ^ fc), device_id_type=MESH)
            cp.start()
            sends.append(cp)
        for k, (fx, fy, fc) in enumerate(flips):
            src = 4 * (x ^ fx) + 2 * (y ^ fy) + (c ^ fc)
            pltpu.make_async_remote_copy(
                src_ref=in_ref, dst_ref=out_ref.at[src], send_sem=send_sems.at[k], recv_sem=recv_sems.at[k],
                device_id=(x, y, c), device_id_type=MESH).wait_recv()
        for cp in sends:
            cp.wait_send()
        local.wait()

    return pl.pallas_call(
        body, name="gather_all",
        in_specs=[ANY], out_specs=ANY,
        out_shape=SDS((8,) + buf.shape, buf.dtype),
        scratch_shapes=[pltpu.SemaphoreType.DMA((7,)), pltpu.SemaphoreType.DMA((7,)), pltpu.SemaphoreType.DMA],
    )(buf)


def _tiles(S, FW):
    big = FW % 512 == 0
    return dict(
        fp=512 if big else LANES,
        tn=512 if big else LANES,
        tm_in=min(1024, S),
        t_attn=min(512, S),
        hb_fwd=4,
        hb_bwd=2,
        tm_prep=min(512, S),
        tm_mix=min(128, S),
        tc=min(256, S),
        tk=512 if big else LANES,
    )


def kernel(x, norm_g, w_in, b_forget, q_norm_g, k_norm_g, conv_w, conv_b, conv_ln_g, conv_ln_b, w_out, loss_target, m_norm_g, m_w_in, m_b_forget, m_q_norm_g, m_k_norm_g, m_conv_w, m_conv_b, m_conv_ln_g, m_conv_ln_b, m_w_out, v_norm_g, v_w_in, v_b_forget, v_q_norm_g, v_k_norm_g, v_conv_w, v_conv_b, v_conv_ln_g, v_conv_ln_b, v_w_out):
    B, S, D = x.shape
    H, dh = q_norm_g.shape[1:]
    FW = H * dh
    CW = conv_b.shape[-1]
    n_taps, cw_shard = conv_w.shape[1:]
    in_shard = w_in.shape[2]
    out_shard = w_out.shape[1]
    assert dh == HEAD_DIM and H % 2 == 0 and H <= LANES and FW == CW == D
    assert n_taps - 1 <= HALO and 4 * cw_shard == CW and 4 * out_shard == FW + CW
    assert 4 * in_shard == 4 * FW + 3 * CW + H
    T = B * S
    tl = _tiles(S, FW)
    fp = tl["fp"]
    xi, yi, ci = _place()

    w_t = jnp.transpose(w_in[0])
    conv_pad = jnp.pad(conv_w[0], ((0, HALO - n_taps), (0, 0)))
    g_in, g_out, g_cw = _gather_chips([
        w_t.astype(BF16).reshape(in_shard, 2, D // 2).transpose(1, 0, 2),
        w_out[0].astype(BF16).reshape(2, out_shard // 2, D),
        conv_pad.reshape(2, HALO // 2, cw_shard)])
    w_t_full = g_in.transpose(0, 2, 1, 3).reshape(4 * in_shard, D)
    w_out_full = g_out.reshape(FW + CW, D)
    conv_full = g_cw.reshape(4, HALO, cw_shard).transpose(1, 0, 2).reshape(HALO, CW)
    o_f = 3 * FW
    w_pack = jnp.concatenate([w_t_full[:o_f], w_t_full[o_f + H:],
                              jnp.pad(w_t_full[o_f:o_f + H], ((0, fp - H), (0, 0)))], axis=0)
    f_col = 4 * FW + 3 * CW

    x2 = x.reshape(T, D)
    tgt = loss_target.reshape(T, D)
    b_pad = jnp.pad(b_forget, ((0, 0), (0, LANES - H)))
    gq = q_norm_g.reshape(1, FW)
    gk = k_norm_g.reshape(1, FW)

    z, h = _fwd_in(x2, norm_g, w_pack, tl["tm_in"], tl["tn"])
    c = _gate_fwd(z, b_pad, B, S, H, f_col // LANES, tl["tc"])
    qa, ka, va = _attn_prep(z, c, gq, gk, B, S, H, tl["tm_prep"])
    oa = _attn_fwd(qa, ka, va, tl["t_attn"], tl["hb_fwd"])
    y, u2, a_nat, dout, dout_b, dy, loss_acc = _fwd_out(
        oa, z, x2, tgt, conv_full, conv_b, conv_ln_g, conv_ln_b, w_out_full, B, S, H, n_taps, tl["tm_mix"])
    loss = lax.psum(0.5 * loss_acc[0, 0] / D, ("x", "y", "c"))

    dzgf, dzgc, du2, doa, qb, sg_conv = _bwd_prep(dy, z, a_nat, oa, qa, u2, conv_ln_g, conv_ln_b, B, S, H, tl["tm_mix"])
    dzglu, dconv_w = _conv_bwd(du2, z, conv_full, B, S, n_taps, tl["tm_mix"])
    dqa, dka, dva = _attn_bwd(qb, ka, va, doa, tl["t_attn"], tl["hb_bwd"])
    dzq, dzk, dzv, dc8, dg_qk = _qk_bwd(dqa, dka, dva, z, gq, gk, B, S, H, tl["tm_prep"])
    dzf, db_f = _gate_bwd(dc8, z, b_pad, B, S, H, f_col // LANES, fp, tl["tc"])
    pieces = [dzq, dzk, dzv, dzgf, dzglu, dzgc, dzf]
    grad_x2, dg_norm = _dh_rms_bwd(pieces, w_pack, x2, norm_g, dout, tl["tm_in"], tl["tk"])
    dw_pieces = [_matmul_tn(p, h, f"dw_in_{k}", 1024, 1024, 512) for k, p in enumerate(pieces)]
    dw_out = _matmul_tn(y, dout_b, "dw_out", 1024, 1024, 512)
    dw_t = jnp.concatenate(dw_pieces[:3] + [dw_pieces[6][:H]] + dw_pieces[3:6], axis=0)

    halves_in = dw_t.reshape(4 * in_shard, 2, D // 2).transpose(1, 0, 2)
    halves_out = dw_out.reshape(4, 2, out_shard // 2, D).transpose(1, 0, 2, 3).reshape(2, 2 * out_shard, D)
    got_in, got_out = _pair_swap([halves_in, halves_out])
    own_in = lax.dynamic_index_in_dim(halves_in, ci, 0, keepdims=False)
    own_out = lax.dynamic_index_in_dim(halves_out, ci, 0, keepdims=False)
    part_in = _pair_sum_bf16(own_in, got_in, "pair_sum_in").reshape(4, in_shard, D // 2)
    part_out = _pair_sum_bf16(own_out, got_out, "pair_sum_out").reshape(4, out_shard // 2, D)
    slots_in, slots_out = _chip_exchange([part_in, part_out])
    chip = 2 * xi + yi
    half_in = _sum_slots(slots_in, "chip_sum_in", lax.dynamic_index_in_dim(part_in, chip, 0, keepdims=False))
    half_out = _sum_slots(slots_out, "chip_sum_out", lax.dynamic_index_in_dim(part_out, chip, 0, keepdims=False))
    other_in, other_out = _pair_send([half_in, half_out])

    def both_halves(mine, other, axis):
        return jnp.where(ci == 0, jnp.concatenate([mine, other], axis=axis), jnp.concatenate([other, mine], axis=axis))

    grad_w_t = both_halves(half_in, other_in, 1)
    grad_w_out = both_halves(half_out, other_out, 0)

    small = jnp.concatenate([
        dg_norm[0:1], jnp.pad(db_f[0:1, :], ((0, 0), (0, D - LANES))), dg_qk[0:1], dg_qk[1:2],
        sg_conv[2:3], sg_conv[0:1], sg_conv[1:2], dconv_w], axis=0)
    n_small = small.shape[0]
    small_sum = _sum_slots(_gather_all(small), "small_sum", tr=n_small)
    grad_norm_g, grad_b_f = small_sum[0:1], small_sum[1:2, :H]
    grad_gq, grad_gk = small_sum[2:3].reshape(1, H, dh), small_sum[3:4].reshape(1, H, dh)
    grad_conv_b, grad_ln_g, grad_ln_b = small_sum[4:5], small_sum[5:6], small_sum[6:7]
    grad_conv_w = lax.dynamic_slice_in_dim(small_sum[7:7 + n_taps], chip * cw_shard, cw_shard, axis=1)

    d_t, nm_t, nv_t = _adamw(w_t, grad_w_t, jnp.transpose(m_w_in[0]), jnp.transpose(v_w_in[0]), "adamw_in")
    grad_w_in, d_in, nm_in, nv_in = (jnp.transpose(t)[None] for t in (grad_w_t, d_t, nm_t, nv_t))
    d_out, nm_out, nv_out = (t[None] for t in _adamw(w_out[0], grad_w_out, m_w_out[0], v_w_out[0], "adamw_out"))
    d_cw, nm_cw, nv_cw = (t[None] for t in _adamw(conv_w[0], grad_conv_w, m_conv_w[0], v_conv_w[0], "adamw_conv_w"))

    def rows(ws):
        return jnp.concatenate([jnp.pad(t.reshape(1, -1), ((0, 0), (0, D - t.size))) for t in ws], axis=0)

    small_w = [norm_g, b_forget, q_norm_g, k_norm_g, conv_b, conv_ln_g, conv_ln_b]
    small_m = [m_norm_g, m_b_forget, m_q_norm_g, m_k_norm_g, m_conv_b, m_conv_ln_g, m_conv_ln_b]
    small_v = [v_norm_g, v_b_forget, v_q_norm_g, v_k_norm_g, v_conv_b, v_conv_ln_g, v_conv_ln_b]
    d_s, nm_s, nv_s = _adamw(rows(small_w), small_sum[0:7], rows(small_m), rows(small_v), "adamw_small")

    def unpack(t):
        return [t[k:k + 1, :w.size].reshape(w.shape) for k, w in enumerate(small_w)]

    def order(s, in_, cw, out_):
        ng, bf, qg, kg, cb, lg, lb = s
        return [ng, in_, bf, qg, kg, cw, cb, lg, lb, out_]

    grads = [grad_norm_g, grad_w_in, grad_b_f, grad_gq, grad_gk, grad_conv_w[None],
             grad_conv_b, grad_ln_g, grad_ln_b, grad_w_out[None]]
    return (loss, grad_x2.reshape(B, S, D), *grads,
            *order(unpack(d_s), d_in, d_cw, d_out),
            *order(unpack(nm_s), nm_in, nm_cw, nm_out),
            *order(unpack(nv_s), nv_in, nv_cw, nv_out))
```

```python
import functools

import jax
import jax.numpy as jnp
from jax import lax
from jax.experimental import pallas as pl
from jax.experimental.pallas import tpu as pltpu

F32 = jnp.float32
BF16 = jnp.bfloat16
SDS = jax.ShapeDtypeStruct
MESH = pl.DeviceIdType.MESH

EPS = 1e-6
NEG_INF = -1e30
LANES = 128
SUBLANES = 8
HEAD_DIM = 64
HALO = 32
VMEM_LIMIT = 56 * 1024 * 1024

L_ROWSUM = 64
L_KDECAY = 67
L_LSE = 70
L_D = 65
L_QNORM = 73
L_KNORM = 74
NORM_SLACK = 1.02
SHIFT_MAX = 40.0

ADAM_LR = 0.001
ADAM_B1 = 0.9
ADAM_B2 = 0.999
ADAM_EPS = 1e-08
ADAM_WD = 0.01
ADAM_STEP = 10


def _params(sem, vmem=VMEM_LIMIT):
    return pltpu.CompilerParams(dimension_semantics=sem, vmem_limit_bytes=vmem)


def _sigmoid(x):
    return 1.0 / (1.0 + jnp.exp(-x))


def _split3(x):
    hi = x.astype(BF16).astype(F32)
    r = x - hi
    mid = r.astype(BF16).astype(F32)
    lo = (r - mid).astype(BF16).astype(F32)
    return hi, mid, lo


def _dot(a, b):
    return jnp.dot(a, b, preferred_element_type=F32)


def _dot_nt(a, b):
    return lax.dot_general(a, b, (((1,), (1,)), ((), ())), preferred_element_type=F32)


def _dot_tn(a, b):
    return lax.dot_general(a, b, (((0,), (0,)), ((), ())), preferred_element_type=F32)


def _lane(shape):
    return lax.broadcasted_iota(jnp.int32, shape, 1)


def _lane_col(x, lane, idx):
    return jnp.sum(jnp.where(lane == idx, x, 0.0), axis=-1, keepdims=True)


def _put3(base, lane, start, pieces):
    out = base
    for k, p in enumerate(pieces):
        out = jnp.where(lane == start + k, p, out)
    return out


def _half_stats(t, lo):
    s0 = jnp.sum(jnp.where(lo, t, 0.0), axis=-1, keepdims=True)
    s1 = jnp.sum(jnp.where(lo, 0.0, t), axis=-1, keepdims=True)
    return jnp.where(lo, s0, s1)


def _fwd_in(x2, g, w_t, tm, tn):
    T, D = x2.shape
    N = w_t.shape[0]

    def body(x_ref, g_ref, w_ref, z_ref, h_ref):
        @pl.when(pl.program_id(1) == 0)
        def _():
            x = x_ref[...]
            r = lax.rsqrt(jnp.mean(x * x, axis=-1, keepdims=True) + EPS)
            h_ref[...] = (x * r * g_ref[...]).astype(BF16)

        z_ref[...] = _dot_nt(h_ref[...], w_ref[...])

    return pl.pallas_call(
        body, name="fwd_in", grid=(T // tm, N // tn),
        in_specs=[pl.BlockSpec((tm, D), lambda i, j: (i, 0)),
                  pl.BlockSpec((1, D), lambda i, j: (0, 0)),
                  pl.BlockSpec((tn, D), lambda i, j: (j, 0))],
        out_specs=[pl.BlockSpec((tm, tn), lambda i, j: (i, j)),
                   pl.BlockSpec((tm, D), lambda i, j: (i, 0))],
        out_shape=[SDS((T, N), F32), SDS((T, D), BF16)],
        compiler_params=_params(("parallel", "arbitrary")),
    )(x2, g, w_t)


def _tri_cumsum(x, reverse):
    t = x.shape[0]
    row = lax.broadcasted_iota(jnp.int32, (t, t), 0)
    col = lax.broadcasted_iota(jnp.int32, (t, t), 1)
    tri = (row <= col) if reverse else (row >= col)
    tri = jnp.where(tri, 1.0, 0.0).astype(BF16)
    hi, mid, lo = _split3(x)
    return _dot(tri, hi.astype(BF16)) + _dot(tri, mid.astype(BF16)) + _dot(tri, lo.astype(BF16))


def _gate_fwd(z, b_pad, B, S, H, col_blk, tc):
    T = B * S
    nsb = S // tc

    def body(zf_ref, b_ref, c_ref, carry):
        @pl.when(pl.program_id(1) == 0)
        def _():
            carry[...] = jnp.zeros_like(carry)

        x = zf_ref[...] + b_ref[...]
        lf = jnp.minimum(x, 0.0) - jnp.log(1.0 + jnp.exp(-jnp.abs(x)))
        lf = jnp.where(_lane(lf.shape) < H, lf, 0.0)
        c_ref[...] = _tri_cumsum(lf, False) + carry[...]
        carry[...] = carry[...] + jnp.sum(lf, axis=0, keepdims=True)

    return pl.pallas_call(
        body, name="gate_fwd", grid=(B, nsb),
        in_specs=[pl.BlockSpec((tc, LANES), lambda b, s: (b * nsb + s, col_blk)),
                  pl.BlockSpec((1, LANES), lambda b, s: (0, 0))],
        out_specs=pl.BlockSpec((tc, LANES), lambda b, s: (b * nsb + s, 0)),
        out_shape=SDS((T, LANES), F32),
        scratch_shapes=[pltpu.VMEM((1, LANES), F32)],
        compiler_params=_params(("parallel", "arbitrary")),
    )(z, b_pad)


def _qk_normalize(x, g, lo):
    r = lax.rsqrt(_half_stats(x * x, lo) * (1.0 / HEAD_DIM) + EPS)
    return x * r * g


def _head_norms(x, lo):
    xx = x * x
    sums = (jnp.sum(jnp.where(lo, xx, 0.0), axis=-1, keepdims=True),
            jnp.sum(jnp.where(lo, 0.0, xx), axis=-1, keepdims=True))
    return [jnp.sqrt(t) * NORM_SLACK for t in sums]


def _attn_prep(z, c, gq, gk, B, S, H, tm):
    T = B * S
    FW = H * HEAD_DIM
    nsb = S // tm
    nfb = FW // LANES
    scale = HEAD_DIM ** -0.5

    def body(zq_ref, zk_ref, zv_ref, c_ref, gq_ref, gk_ref, qa_ref, ka_ref, va_ref):
        p = pl.program_id(1)
        lane = _lane((tm, LANES))
        lo = lane < HEAD_DIM
        qn = _qk_normalize(zq_ref[...], gq_ref[...], lo) * scale
        kn = _qk_normalize(zk_ref[...], gk_ref[...], lo)
        v = zv_ref[...]
        cc = c_ref[...]
        ones_q = ((lane >= L_KDECAY) & (lane < L_KDECAY + 3)).astype(F32)
        ones_k = (((lane >= L_ROWSUM) & (lane < L_ROWSUM + 3)) | ((lane >= L_LSE) & (lane < L_LSE + 3))).astype(F32)
        ones_v = ((lane >= L_ROWSUM) & (lane < L_D + 3)).astype(F32)
        q_norms, k_norms = _head_norms(qn, lo), _head_norms(kn, lo)
        for e in range(2):
            if e == 0:
                qe, ke, ve = qn, kn, v
            else:
                qe, ke, ve = (pltpu.roll(t, HEAD_DIM, 1) for t in (qn, kn, v))
            ch = _lane_col(cc, lane, 2 * p + e)
            pieces = _split3(ch)
            qa = jnp.where(lo, qe, _put3(ones_q, lane, L_ROWSUM, pieces))
            qa = jnp.where(lane == L_QNORM, q_norms[e], qa)
            ka = jnp.where(lo, ke, _put3(ones_k, lane, L_KDECAY, [-t for t in pieces]))
            ka = jnp.where(lane == L_KNORM, k_norms[e], ka)
            va = jnp.where(lo, ve, ones_v)
            qa_ref[0, e] = qa.astype(BF16)
            ka_ref[0, e] = ka.astype(BF16)
            va_ref[0, e] = va.astype(BF16)

    zspec = lambda off: pl.BlockSpec((tm, LANES), lambda i, p: (i, off + p))
    gspec = pl.BlockSpec((1, LANES), lambda i, p: (0, p))
    ospec = pl.BlockSpec((1, 2, tm, LANES), lambda i, p: (i // nsb, p, i % nsb, 0))
    oshape = SDS((B, H, S, LANES), BF16)
    return pl.pallas_call(
        body, name="attn_prep", grid=(T // tm, H // 2),
        in_specs=[zspec(0), zspec(nfb), zspec(2 * nfb),
                  pl.BlockSpec((tm, LANES), lambda i, p: (i, 0)), gspec, gspec],
        out_specs=[ospec, ospec, ospec],
        out_shape=[oshape, oshape, oshape],
        compiler_params=_params(("parallel", "arbitrary")),
    )(z, z, z, c, gq, gk)


def _attn_fwd(qa, ka, va, t, hb):
    B, H, S, _ = qa.shape
    nq = S // t

    def body(q_ref, k_ref, v_ref, o_ref, m_ref, acc_ref, kmax_ref, qs_ref):
        i = pl.program_id(2)
        lane = _lane((t, LANES))

        @pl.when(i == 0)
        def _():
            for e in range(hb):
                norms = jnp.where(_lane((S, LANES)) == L_KNORM, k_ref[0, e].astype(F32), 0.0)
                kmax_ref[e] = jnp.full((1, LANES), jnp.max(norms), F32)

        shifts = [_lane_col(q_ref[0, e].astype(F32), lane, L_QNORM) * kmax_ref[e][:, 0:1] for e in range(hb)]
        worst = shifts[0]
        for e in range(1, hb):
            worst = jnp.maximum(worst, shifts[e])
        bounded = jnp.max(worst) <= SHIFT_MAX
        acc_ref[...] = jnp.zeros_like(acc_ref)

        def tiles(step):
            def loop_body(j, carry):
                step(j, False)
                return carry

            lax.fori_loop(0, i, loop_body, 0)
            step(i, True)

        def keep_mask():
            return lax.broadcasted_iota(jnp.int32, (t, t), 0) >= lax.broadcasted_iota(jnp.int32, (t, t), 1)

        def finish(e, shift):
            acc = acc_ref[e]
            l = _lane_col(acc, lane, L_ROWSUM)
            o_ref[0, e] = jnp.where(lane < HEAD_DIM, acc / l, shift + jnp.log(l))

        @pl.when(bounded)
        def _():
            for e in range(hb):
                qs_ref[e] = _put3(q_ref[0, e].astype(F32), lane, L_LSE, _split3(-shifts[e])).astype(BF16)

            def step(j, masked):
                rows = pl.ds(pl.multiple_of(j * t, t), t)
                for e in range(hb):
                    p = jnp.exp(_dot_nt(qs_ref[e], k_ref[0, e, rows, :]))
                    if masked:
                        p = jnp.where(keep_mask(), p, 0.0)
                    acc_ref[e] = acc_ref[e] + _dot(p.astype(BF16), v_ref[0, e, rows, :])

            tiles(step)
            for e in range(hb):
                finish(e, shifts[e])

        @pl.when(jnp.logical_not(bounded))
        def _():
            m_ref[...] = jnp.full_like(m_ref, NEG_INF)

            def step(j, masked):
                rows = pl.ds(pl.multiple_of(j * t, t), t)
                for e in range(hb):
                    s = _dot_nt(q_ref[0, e], k_ref[0, e, rows, :])
                    if masked:
                        s = jnp.where(keep_mask(), s, NEG_INF)
                    m_prev = m_ref[e]
                    m_new = jnp.maximum(m_prev, jnp.max(s, axis=-1, keepdims=True))
                    alpha = jnp.exp(m_prev - m_new)
                    p = jnp.exp(s - m_new).astype(BF16)
                    acc_ref[e] = alpha * acc_ref[e] + _dot(p, v_ref[0, e, rows, :])
                    m_ref[e] = m_new

            tiles(step)
            for e in range(hb):
                finish(e, m_ref[e])

    return pl.pallas_call(
        body, name="attn_fwd", grid=(B, H // hb, nq),
        in_specs=[pl.BlockSpec((1, hb, t, LANES), lambda b, h, i: (b, h, i, 0)),
                  pl.BlockSpec((1, hb, S, LANES), lambda b, h, i: (b, h, 0, 0)),
                  pl.BlockSpec((1, hb, S, LANES), lambda b, h, i: (b, h, 0, 0))],
        out_specs=pl.BlockSpec((1, hb, t, LANES), lambda b, h, i: (b, h, i, 0)),
        out_shape=SDS((B, H, S, LANES), F32),
        scratch_shapes=[pltpu.VMEM((hb, t, 1), F32), pltpu.VMEM((hb, t, LANES), F32),
                        pltpu.VMEM((hb, 1, LANES), F32), pltpu.VMEM((hb, t, LANES), BF16)],
        compiler_params=_params(("parallel", "parallel", "arbitrary")),
    )(qa, ka, va)


def _fill_shifts(ext_ref, sh_ref):
    rows = sh_ref.shape[1]
    for b in range(1, SUBLANES):
        sh_ref[b - 1] = ext_ref[pl.ds(b, rows), :]


def _tap_window(ext_ref, sh_ref, off, tm, cols):
    b = off % SUBLANES
    if b == 0:
        return ext_ref[pl.ds(off, tm), cols]
    return sh_ref[b - 1, pl.ds(off - b, tm), cols]


def _conv_taps(w_ref, ext_ref, sh_ref, out_ref, n_taps, tm, offset_of, bias_ref=None):
    for cc in range(out_ref.shape[1] // LANES):
        cols = slice(cc * LANES, (cc + 1) * LANES)
        acc = None
        for j in range(n_taps):
            term = w_ref[j:j + 1, cols] * _tap_window(ext_ref, sh_ref, offset_of(j), tm, cols)
            acc = term if acc is None else acc + term
        out_ref[:, cols] = acc if bias_ref is None else acc + bias_ref[:, cols]


def _layernorm_stats(u2):
    mu = jnp.mean(u2, axis=-1, keepdims=True)
    xc = u2 - mu
    rstd = lax.rsqrt(jnp.mean(xc * xc, axis=-1, keepdims=True) + EPS)
    return xc * rstd, rstd


def _fwd_out(oa, z, x2, tgt, conv_w, conv_b, ln_g, ln_b, w_out, B, S, H, n_taps, tm):
    T, D = x2.shape
    FW = H * HEAD_DIM
    CW = conv_w.shape[1]
    nsb = S // tm
    hb = tm // HALO

    def body(oa_ref, gf_ref, ga_ref, gb_ref, gc_ref, ha_ref, hb_ref, x_ref, t_ref, w_ref, cb_ref, lg_ref,
             lb_ref, wo_ref, y_ref, u2_ref, a_ref, do_ref, dob_ref, dy_ref, loss_ref, ext_ref, sh_ref):
        first_step = (pl.program_id(0) == 0) & (pl.program_id(1) == 0)

        @pl.when(first_step)
        def _():
            loss_ref[...] = jnp.zeros_like(loss_ref)

        u1 = ga_ref[...] * _sigmoid(gb_ref[...])
        halo = ha_ref[...] * _sigmoid(hb_ref[...])
        ext_ref[0:HALO, :] = jnp.where(pl.program_id(1) > 0, halo, 0.0)
        ext_ref[HALO:, :] = u1
        _fill_shifts(ext_ref, sh_ref)
        _conv_taps(w_ref, ext_ref, sh_ref, u2_ref, n_taps, tm, lambda j: HALO - (n_taps - 1) + j, cb_ref)
        uh, _ = _layernorm_stats(u2_ref[...])
        u3 = uh * lg_ref[...] + lb_ref[...]
        gc = gc_ref[...]
        yu = u3 * _sigmoid(u3) * (gc * _sigmoid(gc))
        y_ref[:, FW:] = yu.astype(BF16)

        lane = _lane((tm, LANES))
        lo = lane < HEAD_DIM
        for p in range(H // 2):
            a_ref[:, p * LANES:(p + 1) * LANES] = jnp.where(
                lo, oa_ref[0, 2 * p], pltpu.roll(oa_ref[0, 2 * p + 1], HEAD_DIM, 1))
        gf = gf_ref[...]
        y_ref[:, :FW] = (a_ref[...] * (gf * _sigmoid(gf))).astype(BF16)

        out = x_ref[...] + _dot(y_ref[...], wo_ref[...])
        diff = out - t_ref[...]
        loss_ref[...] = loss_ref[...] + jnp.sum(diff * diff)
        dout = diff * (1.0 / D)
        do_ref[...] = dout
        dob = dout.astype(BF16)
        dob_ref[...] = dob
        dy_ref[...] = _dot_nt(dob, wo_ref[...])

    row = lambda b, s: b * nsb + s
    zspec = lambda cb: pl.BlockSpec((tm, FW), lambda b, s: (row(b, s), cb))
    hspec = lambda cb: pl.BlockSpec((HALO, CW), lambda b, s: (jnp.maximum(row(b, s) * hb - 1, 0), cb))
    vspec = pl.BlockSpec((1, CW), lambda b, s: (0, 0))
    tspec = lambda w: pl.BlockSpec((tm, w), lambda b, s: (row(b, s), 0))
    return pl.pallas_call(
        body, name="fwd_out", grid=(B, nsb),
        in_specs=[pl.BlockSpec((1, H, tm, LANES), lambda b, s: (b, 0, s, 0)),
                  zspec(3), zspec(4), zspec(5), zspec(6), hspec(4), hspec(5),
                  tspec(D), tspec(D),
                  pl.BlockSpec((HALO, CW), lambda b, s: (0, 0)), vspec, vspec, vspec,
                  pl.BlockSpec((FW + CW, D), lambda b, s: (0, 0))],
        out_specs=[tspec(FW + CW), tspec(CW), tspec(FW), tspec(D), tspec(D), tspec(FW + CW),
                   pl.BlockSpec((8, LANES), lambda b, s: (0, 0))],
        out_shape=[SDS((T, FW + CW), BF16), SDS((T, CW), F32), SDS((T, FW), F32), SDS((T, D), F32),
                   SDS((T, D), BF16), SDS((T, FW + CW), F32), SDS((8, LANES), F32)],
        scratch_shapes=[pltpu.VMEM((tm + HALO, CW), F32),
                        pltpu.VMEM((SUBLANES - 1, tm + HALO - SUBLANES, CW), F32)],
        compiler_params=_params(("arbitrary", "arbitrary")),
    )(oa, z, z, z, z, z, z, x2, tgt, conv_w, conv_b, ln_g, ln_b, w_out)


def _bwd_prep(dy, z, a_nat, oa, qa, u2, ln_g, ln_b, B, S, H, tm):
    T = B * S
    FW = H * HEAD_DIM
    CW = u2.shape[1]
    nsb = S // tm

    def body(dya_ref, dyu_ref, gf_ref, gc_ref, a_ref, oa_ref, qa_ref, u2_ref, lg_ref, lb_ref,
             dzgf_ref, dzgc_ref, du2_ref, doa_ref, qb_ref, sg_ref):
        first_step = (pl.program_id(0) == 0) & (pl.program_id(1) == 0)

        @pl.when(first_step)
        def _():
            sg_ref[...] = jnp.zeros_like(sg_ref)

        gf = gf_ref[...]
        sg = _sigmoid(gf)
        a = a_ref[...]
        dya = dya_ref[...]
        da = dya * (gf * sg)
        dzgf_ref[...] = (dya * a * (sg * (1.0 + gf * (1.0 - sg)))).astype(BF16)
        dd = da * a
        lane = _lane((tm, LANES))
        lo = lane < HEAD_DIM
        for p in range(H // 2):
            cols = slice(p * LANES, (p + 1) * LANES)
            da_p = da[:, cols]
            dd_p = dd[:, cols]
            d_heads = (jnp.sum(jnp.where(lo, dd_p, 0.0), axis=-1, keepdims=True),
                       jnp.sum(jnp.where(lo, 0.0, dd_p), axis=-1, keepdims=True))
            for e in range(2):
                da_e = da_p if e == 0 else pltpu.roll(da_p, HEAD_DIM, 1)
                d_e = d_heads[e]
                aug = _put3(jnp.zeros((tm, LANES), F32), lane, L_D, _split3(-d_e))
                doa_ref[0, 2 * p + e] = jnp.where(lo, da_e, aug).astype(BF16)
                lse = _lane_col(oa_ref[0, 2 * p + e], lane, L_ROWSUM)
                qb = _put3(qa_ref[0, 2 * p + e].astype(F32), lane, L_LSE, _split3(-lse))
                qb_ref[0, 2 * p + e] = qb.astype(BF16)

        gc = gc_ref[...]
        sc = _sigmoid(gc)
        dyu = dyu_ref[...]
        uh, rstd = _layernorm_stats(u2_ref[...])
        u3 = uh * lg_ref[...] + lb_ref[...]
        s3 = _sigmoid(u3)
        dzgc_ref[...] = (dyu * (u3 * s3) * (sc * (1.0 + gc * (1.0 - sc)))).astype(BF16)
        du3 = dyu * (gc * sc) * (s3 * (1.0 + u3 * (1.0 - s3)))
        sg_ref[0:1, :] = sg_ref[0:1, :] + jnp.sum(du3 * uh, axis=0, keepdims=True)
        sg_ref[1:2, :] = sg_ref[1:2, :] + jnp.sum(du3, axis=0, keepdims=True)
        duh = du3 * lg_ref[...]
        du2 = rstd * (duh - jnp.mean(duh, axis=-1, keepdims=True)
                      - uh * jnp.mean(duh * uh, axis=-1, keepdims=True))
        sg_ref[2:3, :] = sg_ref[2:3, :] + jnp.sum(du2, axis=0, keepdims=True)
        du2_ref[...] = du2

    row = lambda b, s: b * nsb + s
    tspec = lambda w, cb=0: pl.BlockSpec((tm, w), lambda b, s: (row(b, s), cb))
    hspec = pl.BlockSpec((1, H, tm, LANES), lambda b, s: (b, 0, s, 0))
    vspec = pl.BlockSpec((1, CW), lambda b, s: (0, 0))
    return pl.pallas_call(
        body, name="bwd_prep", grid=(B, nsb),
        in_specs=[tspec(FW, 0), tspec(CW, 1), tspec(FW, 3), tspec(CW, 6), tspec(FW), hspec, hspec,
                  tspec(CW), vspec, vspec],
        out_specs=[tspec(FW), tspec(CW), tspec(CW), hspec, hspec,
                   pl.BlockSpec((8, CW), lambda b, s: (0, 0))],
        out_shape=[SDS((T, FW), BF16), SDS((T, CW), BF16), SDS((T, CW), F32),
                   SDS((B, H, S, LANES), BF16), SDS((B, H, S, LANES), BF16), SDS((8, CW), F32)],
        compiler_params=_params(("arbitrary", "arbitrary")),
    )(dy, dy, z, z, a_nat, oa, qa, u2, ln_g, ln_b)


def _conv_bwd(du2, z, conv_w, B, S, n_taps, tm):
    T, CW = du2.shape
    nsb = S // tm
    hb = tm // HALO

    def body(d_ref, dh_ref, ga_ref, gb_ref, ha_ref, hb_ref, w_ref, dz_ref, dw_ref,
             extu_ref, extd_ref, shu_ref, shd_ref, du1_ref, dwacc_ref):
        s = pl.program_id(1)
        first_step = (pl.program_id(0) == 0) & (s == 0)
        last_step = (pl.program_id(0) == B - 1) & (s == nsb - 1)

        @pl.when(first_step)
        def _():
            dwacc_ref[...] = jnp.zeros_like(dwacc_ref)

        ga = ga_ref[...]
        sb = _sigmoid(gb_ref[...])
        halo = ha_ref[...] * _sigmoid(hb_ref[...])
        extu_ref[0:HALO, :] = jnp.where(s > 0, halo, 0.0)
        extu_ref[HALO:, :] = ga * sb
        extd_ref[0:tm, :] = d_ref[...]
        extd_ref[tm:, :] = jnp.where(s < nsb - 1, dh_ref[...], 0.0)
        _fill_shifts(extu_ref, shu_ref)
        _fill_shifts(extd_ref, shd_ref)
        _conv_taps(w_ref, extd_ref, shd_ref, du1_ref, n_taps, tm, lambda j: n_taps - 1 - j)
        for cc in range(CW // LANES):
            cols = slice(cc * LANES, (cc + 1) * LANES)
            parts = [None] * n_taps
            for r in range(tm // SUBLANES):
                dv = d_ref[r * SUBLANES:(r + 1) * SUBLANES, cols]
                for j in range(n_taps):
                    off = HALO - (n_taps - 1) + j + r * SUBLANES
                    term = dv * _tap_window(extu_ref, shu_ref, off, SUBLANES, cols)
                    parts[j] = term if parts[j] is None else parts[j] + term
            for j in range(n_taps):
                rows = slice(j * SUBLANES, (j + 1) * SUBLANES)
                dwacc_ref[rows, cols] = dwacc_ref[rows, cols] + parts[j]
        du1 = du1_ref[...]
        dz_ref[:, :CW] = (du1 * sb).astype(BF16)
        dz_ref[:, CW:] = (du1 * ga * (sb * (1.0 - sb))).astype(BF16)

        @pl.when(last_step)
        def _():
            dw_ref[...] = jnp.zeros_like(dw_ref)
            for j in range(n_taps):
                dw_ref[j:j + 1, :] = jnp.sum(dwacc_ref[j * SUBLANES:(j + 1) * SUBLANES, :], axis=0, keepdims=True)

    row = lambda b, s: b * nsb + s
    last_halo = T // HALO - 1
    return pl.pallas_call(
        body, name="conv_bwd", grid=(B, nsb),
        in_specs=[pl.BlockSpec((tm, CW), lambda b, s: (row(b, s), 0)),
                  pl.BlockSpec((HALO, CW), lambda b, s: (jnp.minimum((row(b, s) + 1) * hb, last_halo), 0)),
                  pl.BlockSpec((tm, CW), lambda b, s: (row(b, s), 4)),
                  pl.BlockSpec((tm, CW), lambda b, s: (row(b, s), 5)),
                  pl.BlockSpec((HALO, CW), lambda b, s: (jnp.maximum(row(b, s) * hb - 1, 0), 4)),
                  pl.BlockSpec((HALO, CW), lambda b, s: (jnp.maximum(row(b, s) * hb - 1, 0), 5)),
                  pl.BlockSpec((HALO, CW), lambda b, s: (0, 0))],
        out_specs=[pl.BlockSpec((tm, 2 * CW), lambda b, s: (row(b, s), 0)),
                   pl.BlockSpec((HALO, CW), lambda b, s: (0, 0))],
        out_shape=[SDS((T, 2 * CW), BF16), SDS((HALO, CW), F32)],
        scratch_shapes=[pltpu.VMEM((tm + HALO, CW), F32), pltpu.VMEM((tm + HALO, CW), F32),
                        pltpu.VMEM((SUBLANES - 1, tm + HALO - SUBLANES, CW), F32),
                        pltpu.VMEM((SUBLANES - 1, tm + HALO - SUBLANES, CW), F32),
                        pltpu.VMEM((tm, CW), F32), pltpu.VMEM((HALO * SUBLANES, CW), F32)],
        compiler_params=_params(("arbitrary", "arbitrary")),
    )(du2, du2, z, z, z, z, conv_w)


def _attn_bwd(qb, ka, va, doa, t, hb):
    B, H, S, _ = qb.shape
    nk = S // t

    def body(q_ref, k_ref, v_ref, do_ref, dq_ref, dk_ref, dv_ref):
        j = pl.program_id(2)

        @pl.when(j == 0)
        def _():
            dq_ref[...] = jnp.zeros_like(dq_ref)

        dk_ref[...] = jnp.zeros_like(dk_ref)
        dv_ref[...] = jnp.zeros_like(dv_ref)

        def step(i, masked):
            rows = pl.ds(pl.multiple_of(i * t, t), t)
            if masked:
                keep = lax.broadcasted_iota(jnp.int32, (t, t), 0) >= lax.broadcasted_iota(jnp.int32, (t, t), 1)
            for e in range(hb):
                k = k_ref[0, e]
                q = q_ref[0, e, rows, :]
                do = do_ref[0, e, rows, :]
                p = jnp.exp(_dot_nt(q, k))
                if masked:
                    p = jnp.where(keep, p, 0.0)
                ds = (p * _dot_nt(do, v_ref[0, e])).astype(BF16)
                dv_ref[0, e] = dv_ref[0, e] + _dot_tn(p.astype(BF16), do)
                dk_ref[0, e] = dk_ref[0, e] + _dot_tn(ds, q)
                dq_ref[0, e, rows, :] = dq_ref[0, e, rows, :] + _dot(ds, k)

        step(j, True)

        def loop_body(i, carry):
            step(i, False)
            return carry

        lax.fori_loop(j + 1, nk, loop_body, 0)

    full = pl.BlockSpec((1, hb, S, LANES), lambda b, h, j: (b, h, 0, 0))
    blk = pl.BlockSpec((1, hb, t, LANES), lambda b, h, j: (b, h, j, 0))
    oshape = SDS((B, H, S, LANES), F32)
    return pl.pallas_call(
        body, name="attn_bwd", grid=(B, H // hb, nk),
        in_specs=[full, blk, blk, full],
        out_specs=[full, blk, blk],
        out_shape=[oshape, oshape, oshape],
        compiler_params=_params(("parallel", "parallel", "arbitrary")),
    )(qb, ka, va, doa)


def _qk_bwd(dqa, dka, dva, z, gq, gk, B, S, H, tm):
    T = B * S
    FW = H * HEAD_DIM
    nsb = S // tm
    nfb = FW // LANES
    scale = HEAD_DIM ** -0.5

    def body(dq_ref, dk_ref, dv_ref, zq_ref, zk_ref, gq_ref, gk_ref, dzq_ref, dzk_ref, dzv_ref, dc_ref, dg_ref):
        p = pl.program_id(0)

        @pl.when(pl.program_id(1) == 0)
        def _():
            dg_ref[...] = jnp.zeros_like(dg_ref)

        lane = _lane((tm, LANES))
        lo = lane < HEAD_DIM

        def natural(ref):
            return jnp.where(lo, ref[0, 0], pltpu.roll(ref[0, 1], HEAD_DIM, 1))

        def norm_bwd(dn, x, g, row, out_ref):
            r = lax.rsqrt(_half_stats(x * x, lo) * (1.0 / HEAD_DIM) + EPS)
            xh = x * r
            dg_ref[row:row + 1, :] = dg_ref[row:row + 1, :] + jnp.sum(dn * xh, axis=0, keepdims=True)
            dxh = dn * g
            mm = _half_stats(dxh * xh, lo) * (1.0 / HEAD_DIM)
            out_ref[...] = (r * (dxh - xh * mm)).astype(BF16)

        norm_bwd(natural(dq_ref) * scale, zq_ref[...], gq_ref[...], 0, dzq_ref)
        norm_bwd(natural(dk_ref), zk_ref[...], gk_ref[...], 1, dzk_ref)
        dzv_ref[...] = natural(dv_ref).astype(BF16)

        dc = jnp.zeros((tm, LANES), F32)
        for e in range(2):
            val = _lane_col(dq_ref[0, e], lane, L_ROWSUM) - _lane_col(dk_ref[0, e], lane, L_KDECAY)
            dc = jnp.where(lane == 2 * p + e, val, dc)
        dc_ref[0] = dc

    hspec = pl.BlockSpec((1, 2, tm, LANES), lambda p, i: (i // nsb, p, i % nsb, 0))
    zspec = lambda off: pl.BlockSpec((tm, LANES), lambda p, i: (i, off + p))
    gspec = pl.BlockSpec((1, LANES), lambda p, i: (0, p))
    ospec = pl.BlockSpec((tm, LANES), lambda p, i: (i, p))
    return pl.pallas_call(
        body, name="qk_bwd", grid=(H // 2, T // tm),
        in_specs=[hspec, hspec, hspec, zspec(0), zspec(nfb), gspec, gspec],
        out_specs=[ospec, ospec, ospec,
                   pl.BlockSpec((1, tm, LANES), lambda p, i: (p, i, 0)),
                   pl.BlockSpec((8, LANES), lambda p, i: (0, p))],
        out_shape=[SDS((T, FW), BF16), SDS((T, FW), BF16), SDS((T, FW), BF16),
                   SDS((H // 2, T, LANES), F32), SDS((8, FW), F32)],
        compiler_params=_params(("parallel", "arbitrary")),
    )(dqa, dka, dva, z, z, gq, gk)


def _gate_bwd(dc8, z, b_pad, B, S, H, col_blk, fp, tc):
    T = B * S
    nsb = S // tc
    npair = dc8.shape[0]

    def body(dc_ref, zf_ref, b_ref, dz_ref, db_ref, carry):
        first_step = (pl.program_id(0) == 0) & (pl.program_id(1) == 0)

        @pl.when(first_step)
        def _():
            db_ref[...] = jnp.zeros_like(db_ref)

        @pl.when(pl.program_id(1) == 0)
        def _():
            carry[...] = jnp.zeros_like(carry)

        dc = dc_ref[0]
        for k in range(1, npair):
            dc = dc + dc_ref[k]
        dlf = _tri_cumsum(dc, True) + carry[...]
        carry[...] = carry[...] + jnp.sum(dc, axis=0, keepdims=True)
        x = zf_ref[...] + b_ref[...]
        dlogit = dlf * _sigmoid(-x)
        db_ref[0:1, :] = db_ref[0:1, :] + jnp.sum(dlogit, axis=0, keepdims=True)
        dz_ref[...] = jnp.zeros_like(dz_ref)
        dz_ref[:, :LANES] = dlogit.astype(BF16)

    rrow = lambda b, s: b * nsb + (nsb - 1 - s)
    return pl.pallas_call(
        body, name="gate_bwd", grid=(B, nsb),
        in_specs=[pl.BlockSpec((npair, tc, LANES), lambda b, s: (0, rrow(b, s), 0)),
                  pl.BlockSpec((tc, LANES), lambda b, s: (rrow(b, s), col_blk)),
                  pl.BlockSpec((1, LANES), lambda b, s: (0, 0))],
        out_specs=[pl.BlockSpec((tc, fp), lambda b, s: (rrow(b, s), 0)),
                   pl.BlockSpec((8, LANES), lambda b, s: (0, 0))],
        out_shape=[SDS((T, fp), BF16), SDS((8, LANES), F32)],
        scratch_shapes=[pltpu.VMEM((1, LANES), F32)],
        compiler_params=_params(("arbitrary", "arbitrary")),
    )(dc8, z, b_pad)


def _matmul_tn(a, b, name, tmm, tn, tk):
    T, M = a.shape
    N = b.shape[1]
    tmm, tn, tk = min(tmm, M), min(tn, N), min(tk, T)

    def body(a_ref, b_ref, o_ref):
        @pl.when(pl.program_id(2) == 0)
        def _():
            o_ref[...] = jnp.zeros_like(o_ref)

        o_ref[...] = o_ref[...] + _dot_tn(a_ref[...], b_ref[...])

    return pl.pallas_call(
        body, name=name, grid=(M // tmm, N // tn, T // tk),
        in_specs=[pl.BlockSpec((tk, tmm), lambda i, j, k: (k, i)),
                  pl.BlockSpec((tk, tn), lambda i, j, k: (k, j))],
        out_specs=pl.BlockSpec((tmm, tn), lambda i, j, k: (i, j)),
        out_shape=SDS((M, N), F32),
        compiler_params=_params(("parallel", "parallel", "arbitrary")),
    )(a, b)


def _dh_rms_bwd(pieces, w_t, x2, g, dout, tm, tk, parts):
    T, D = x2.shape
    nks = [p.shape[1] // tk for p in pieces]
    starts = [sum(nks[:k]) for k in range(len(pieces))]
    nk = sum(nks)
    ni = T // tm
    n = len(parts)

    def body(*refs):
        dz_refs = refs[:len(pieces)]
        w_ref, x_ref, g_ref, do_ref = refs[len(pieces):len(pieces) + 4]
        part_refs = refs[len(pieces) + 4:len(pieces) + 4 + n]
        gx_ref, dg_ref = refs[len(pieces) + 4 + n:len(pieces) + 6 + n]
        slot_refs = refs[len(pieces) + 6 + n:len(pieces) + 6 + 2 * n]
        acc_ref, send_sems, recv_sems = refs[len(pieces) + 6 + 2 * n:]
        k = pl.program_id(1)
        first_step = (pl.program_id(0) == 0) & (k == 0)
        last_step = (pl.program_id(0) == ni - 1) & (k == nk - 1)
        x, y, c = _place()
        chips = [(1 - x, y), (x, 1 - y), (1 - x, 1 - y)]

        def copy(a, f, to):
            cx, cy = chips[f]
            return pltpu.make_async_remote_copy(
                src_ref=part_refs[a].at[2 * cx + cy], dst_ref=slot_refs[a].at[f],
                send_sem=send_sems.at[a * 3 + f], recv_sem=recv_sems.at[a * 3 + f],
                device_id=to, device_id_type=MESH)

        @pl.when(first_step)
        def _():
            dg_ref[...] = jnp.zeros_like(dg_ref)
            for a in range(n):
                for f in range(3):
                    copy(a, f, (*chips[f], c)).start()

        @pl.when(last_step)
        def _():
            for a in range(n):
                for f in range(3):
                    copy(a, f, (x, y, c)).wait_recv()
            for a in range(n):
                for f in range(3):
                    copy(a, f, (*chips[f], c)).wait_send()

        @pl.when(k == 0)
        def _():
            acc_ref[...] = jnp.zeros_like(acc_ref)

        for dz_ref, st, cnt in zip(dz_refs, starts, nks):
            @pl.when((k >= st) & (k < st + cnt))
            def _(dz_ref=dz_ref):
                acc_ref[...] = acc_ref[...] + _dot(dz_ref[...], w_ref[...])

        @pl.when(k == nk - 1)
        def _():
            x = x_ref[...]
            r = lax.rsqrt(jnp.mean(x * x, axis=-1, keepdims=True) + EPS)
            xh = x * r
            dh = acc_ref[...]
            dg_ref[0:1, :] = dg_ref[0:1, :] + jnp.sum(dh * xh, axis=0, keepdims=True)
            dxn = dh * g_ref[...]
            gx_ref[...] = do_ref[...] + r * (dxn - xh * jnp.mean(dxn * xh, axis=-1, keepdims=True))

    def piece_spec(st, cnt):
        return pl.BlockSpec((tm, tk), lambda i, k: (i, jnp.clip(k - st, 0, cnt - 1)))

    tspec = pl.BlockSpec((tm, D), lambda i, k: (i, 0))
    return pl.pallas_call(
        body, name="dh_rms_bwd", grid=(T // tm, nk),
        in_specs=[piece_spec(st, cnt) for st, cnt in zip(starts, nks)]
        + [pl.BlockSpec((tk, D), lambda i, k: (k, 0)), tspec, pl.BlockSpec((1, D), lambda i, k: (0, 0)), tspec]
        + [ANY] * n,
        out_specs=[tspec, pl.BlockSpec((8, D), lambda i, k: (0, 0))] + [ANY] * n,
        out_shape=[SDS((T, D), F32), SDS((8, D), F32)] + [SDS((3,) + p.shape[1:], p.dtype) for p in parts],
        scratch_shapes=[pltpu.VMEM((tm, D), F32),
                        pltpu.SemaphoreType.DMA((3 * n,)), pltpu.SemaphoreType.DMA((3 * n,))],
        compiler_params=_params(("arbitrary", "arbitrary")),
    )(*pieces, w_t, x2, g, dout, *parts)


def _block_plan(R, C, tr, tc):
    br = min(tr, R)
    if R % br == 0:
        return (br, C), R // br, lambda i: (i, 0)
    bc = min(tc, C)
    assert C % bc == 0
    return (R, bc), C // bc, lambda i: (0, i)


def _ew_call(body, name, ins, n_out, out_dtypes, tr, tc):
    R, C = ins[0].shape
    blk, steps, imap = _block_plan(R, C, tr, tc)
    spec = pl.BlockSpec(blk, imap)
    return pl.pallas_call(
        body, name=name, grid=(steps,),
        in_specs=[spec] * len(ins), out_specs=[spec] * n_out,
        out_shape=[SDS((R, C), dt) for dt in out_dtypes],
        compiler_params=_params(("parallel",)),
    )(*ins)


def _pair_sum_bf16(a, b, name):
    def body(a_ref, b_ref, o_ref):
        o_ref[...] = (a_ref[...] + b_ref[...]).astype(BF16)

    return _ew_call(body, name, [a, b], 1, [BF16], 256, LANES)[0]


def _sum_slots(slots, name, first=None, tr=256):
    n, R, C = slots.shape
    blk, steps, imap = _block_plan(R, C, tr, 2 * LANES)
    lead = [] if first is None else [first]

    def body(*refs):
        s_ref, o_ref = refs[-2:]
        acc = refs[0][...].astype(F32) if lead else s_ref[0].astype(F32)
        for k in range(0 if lead else 1, n):
            acc = acc + s_ref[k].astype(F32)
        o_ref[...] = acc

    return pl.pallas_call(
        body, name=name, grid=(steps,),
        in_specs=[pl.BlockSpec(blk, imap)] * len(lead) + [pl.BlockSpec((n,) + blk, lambda i: (0,) + imap(i))],
        out_specs=pl.BlockSpec(blk, imap),
        out_shape=SDS((R, C), F32),
        compiler_params=_params(("parallel",)),
    )(*lead, slots)


def _adamw(w, g, m, v, name):
    def body(w_ref, g_ref, m_ref, v_ref, d_ref, nm_ref, nv_ref):
        gg = g_ref[...]
        nm = ADAM_B1 * m_ref[...] + (1.0 - ADAM_B1) * gg
        nv = ADAM_B2 * v_ref[...] + (1.0 - ADAM_B2) * (gg * gg)
        m_hat = nm / (1.0 - ADAM_B1 ** ADAM_STEP)
        v_hat = nv / (1.0 - ADAM_B2 ** ADAM_STEP)
        d_ref[...] = -ADAM_LR * (m_hat / (jnp.sqrt(v_hat) + ADAM_EPS) + ADAM_WD * w_ref[...])
        nm_ref[...] = nm
        nv_ref[...] = nv

    return _ew_call(body, name, [w, g, m, v], 3, [F32, F32, F32], 128, 2 * LANES)


ANY = pl.BlockSpec(memory_space=pl.ANY)


def _place():
    return lax.axis_index("x"), lax.axis_index("y"), lax.axis_index("c")


def _gather_chips(shards):
    n = len(shards)
    per = 7

    def body(*refs):
        ins, outs = refs[:n], refs[n:2 * n]
        send_sems, recv_sems = refs[2 * n:]
        x, y, c = _place()
        mine = 2 * x + y
        me, sibling = (x, y, c), (x, y, 1 - c)
        chips = [(1 - x, y), (x, 1 - y), (1 - x, 1 - y)]

        def copy(a, k, chip_idx, half, to, src=None):
            dst = outs[a].at[chip_idx, half]
            return pltpu.make_async_remote_copy(
                src_ref=dst if src is None else src, dst_ref=dst,
                send_sem=send_sems.at[a * per + k], recv_sem=recv_sems.at[a * per + k],
                device_id=to, device_id_type=MESH)

        def own(a, to):
            return pltpu.make_async_remote_copy(
                src_ref=ins[a], dst_ref=outs[a].at[mine],
                send_sem=send_sems.at[a * per + 6], recv_sem=recv_sems.at[a * per + 6],
                device_id=to, device_id_type=MESH)

        first = [copy(a, k, mine, c, (*chip, c), src=ins[a].at[c]) for a in range(n) for k, chip in enumerate(chips)]
        first += [own(a, sibling) for a in range(n)]
        for cp in first:
            cp.start()
        passed = []
        for k, (cx, cy) in enumerate(chips):
            for a in range(n):
                copy(a, k, 2 * cx + cy, c, me).wait_recv()
                fwd = copy(a, 3 + k, 2 * cx + cy, c, sibling)
                fwd.start()
                passed.append(fwd)
        for k, (cx, cy) in enumerate(chips):
            for a in range(n):
                copy(a, 3 + k, 2 * cx + cy, 1 - c, me).wait_recv()
        for a in range(n):
            own(a, me).wait_recv()
        for cp in first + passed:
            cp.wait_send()

    return pl.pallas_call(
        body, name="gather_chips",
        in_specs=[ANY] * n, out_specs=[ANY] * n,
        out_shape=[SDS((4,) + s.shape, s.dtype) for s in shards],
        scratch_shapes=[pltpu.SemaphoreType.DMA((per * n,)), pltpu.SemaphoreType.DMA((per * n,))],
    )(*shards)


def _pair_swap(halves):
    n = len(halves)

    def body(*refs):
        ins, outs = refs[:n], refs[n:2 * n]
        send_sems, recv_sems = refs[2 * n:]
        x, y, c = _place()
        copies = [pltpu.make_async_remote_copy(
            src_ref=ins[a].at[1 - c], dst_ref=outs[a], send_sem=send_sems.at[a], recv_sem=recv_sems.at[a],
            device_id=(x, y, 1 - c), device_id_type=MESH) for a in range(n)]
        for cp in copies:
            cp.start()
        for cp in copies:
            cp.wait()

    return pl.pallas_call(
        body, name="pair_swap",
        in_specs=[ANY] * n, out_specs=[ANY] * n,
        out_shape=[SDS(h.shape[1:], h.dtype) for h in halves],
        scratch_shapes=[pltpu.SemaphoreType.DMA((n,)), pltpu.SemaphoreType.DMA((n,))],
    )(*halves)


def _pair_send(arrs):
    n = len(arrs)

    def body(*refs):
        ins, outs = refs[:n], refs[n:2 * n]
        send_sems, recv_sems = refs[2 * n:]
        x, y, c = _place()
        copies = [pltpu.make_async_remote_copy(
            src_ref=ins[a], dst_ref=outs[a], send_sem=send_sems.at[a], recv_sem=recv_sems.at[a],
            device_id=(x, y, 1 - c), device_id_type=MESH) for a in range(n)]
        for cp in copies:
            cp.start()
        for cp in copies:
            cp.wait()

    return pl.pallas_call(
        body, name="pair_send",
        in_specs=[ANY] * n, out_specs=[ANY] * n,
        out_shape=[SDS(h.shape, h.dtype) for h in arrs],
        scratch_shapes=[pltpu.SemaphoreType.DMA((n,)), pltpu.SemaphoreType.DMA((n,))],
    )(*arrs)


def _gather_all(buf):
    flips = [(fx, fy, fc) for fx in (0, 1) for fy in (0, 1) for fc in (0, 1)][1:]

    def body(in_ref, out_ref, send_sems, recv_sems, local_sem):
        x, y, c = _place()
        me = 4 * x + 2 * y + c
        local = pltpu.make_async_copy(in_ref, out_ref.at[me], local_sem)
        local.start()
        sends = []
        for k, (fx, fy, fc) in enumerate(flips):
            cp = pltpu.make_async_remote_copy(
                src_ref=in_ref, dst_ref=out_ref.at[me], send_sem=send_sems.at[k], recv_sem=recv_sems.at[k],
                device_id=(x ^ fx, y ^ fy, c ^ fc), device_id_type=MESH)
            cp.start()
            sends.append(cp)
        for k, (fx, fy, fc) in enumerate(flips):
            src = 4 * (x ^ fx) + 2 * (y ^ fy) + (c ^ fc)
            pltpu.make_async_remote_copy(
                src_ref=in_ref, dst_ref=out_ref.at[src], send_sem=send_sems.at[k], recv_sem=recv_sems.at[k],
                device_id=(x, y, c), device_id_type=MESH).wait_recv()
        for cp in sends:
            cp.wait_send()
        local.wait()

    return pl.pallas_call(
        body, name="gather_all",
        in_specs=[ANY], out_specs=ANY,
        out_shape=SDS((8,) + buf.shape, buf.dtype),
        scratch_shapes=[pltpu.SemaphoreType.DMA((7,)), pltpu.SemaphoreType.DMA((7,)), pltpu.SemaphoreType.DMA],
    )(buf)


def _tiles(S, FW):
    big = FW % 512 == 0
    return dict(
        fp=512 if big else LANES,
        tn=512 if big else LANES,
        tm_in=min(1024, S),
        t_attn=min(512, S),
        hb_fwd=4,
        hb_bwd=2,
        tm_prep=min(512, S),
        tm_mix=min(128, S),
        tc=min(256, S),
        tk=512 if big else LANES,
    )


def kernel(x, norm_g, w_in, b_forget, q_norm_g, k_norm_g, conv_w, conv_b, conv_ln_g, conv_ln_b, w_out, loss_target, m_norm_g, m_w_in, m_b_forget, m_q_norm_g, m_k_norm_g, m_conv_w, m_conv_b, m_conv_ln_g, m_conv_ln_b, m_w_out, v_norm_g, v_w_in, v_b_forget, v_q_norm_g, v_k_norm_g, v_conv_w, v_conv_b, v_conv_ln_g, v_conv_ln_b, v_w_out):
    B, S, D = x.shape
    H, dh = q_norm_g.shape[1:]
    FW = H * dh
    CW = conv_b.shape[-1]
    n_taps, cw_shard = conv_w.shape[1:]
    in_shard = w_in.shape[2]
    out_shard = w_out.shape[1]
    assert dh == HEAD_DIM and H % 2 == 0 and H <= LANES and FW == CW == D
    assert n_taps - 1 <= HALO and 4 * cw_shard == CW and 4 * out_shard == FW + CW
    assert 4 * in_shard == 4 * FW + 3 * CW + H
    T = B * S
    tl = _tiles(S, FW)
    fp = tl["fp"]
    xi, yi, ci = _place()

    w_t = jnp.transpose(w_in[0])
    conv_pad = jnp.pad(conv_w[0], ((0, HALO - n_taps), (0, 0)))
    g_in, g_out, g_cw = _gather_chips([
        w_t.astype(BF16).reshape(in_shard, 2, D // 2).transpose(1, 0, 2),
        w_out[0].astype(BF16).reshape(2, out_shard // 2, D),
        conv_pad.reshape(2, HALO // 2, cw_shard)])
    w_t_full = g_in.transpose(0, 2, 1, 3).reshape(4 * in_shard, D)
    w_out_full = g_out.reshape(FW + CW, D)
    conv_full = g_cw.reshape(4, HALO, cw_shard).transpose(1, 0, 2).reshape(HALO, CW)
    o_f = 3 * FW
    w_pack = jnp.concatenate([w_t_full[:o_f], w_t_full[o_f + H:],
                              jnp.pad(w_t_full[o_f:o_f + H], ((0, fp - H), (0, 0)))], axis=0)
    f_col = 4 * FW + 3 * CW

    x2 = x.reshape(T, D)
    tgt = loss_target.reshape(T, D)
    b_pad = jnp.pad(b_forget, ((0, 0), (0, LANES - H)))
    gq = q_norm_g.reshape(1, FW)
    gk = k_norm_g.reshape(1, FW)

    z, h = _fwd_in(x2, norm_g, w_pack, tl["tm_in"], tl["tn"])
    c = _gate_fwd(z, b_pad, B, S, H, f_col // LANES, tl["tc"])
    qa, ka, va = _attn_prep(z, c, gq, gk, B, S, H, tl["tm_prep"])
    oa = _attn_fwd(qa, ka, va, tl["t_attn"], tl["hb_fwd"])
    y, u2, a_nat, dout, dout_b, dy, loss_acc = _fwd_out(
        oa, z, x2, tgt, conv_full, conv_b, conv_ln_g, conv_ln_b, w_out_full, B, S, H, n_taps, tl["tm_mix"])
    loss = lax.psum(0.5 * loss_acc[0, 0] / D, ("x", "y", "c"))

    dzgf, dzgc, du2, doa, qb, sg_conv = _bwd_prep(dy, z, a_nat, oa, qa, u2, conv_ln_g, conv_ln_b, B, S, H, tl["tm_mix"])
    dzglu, dconv_w = _conv_bwd(du2, z, conv_full, B, S, n_taps, tl["tm_mix"])
    dqa, dka, dva = _attn_bwd(qb, ka, va, doa, tl["t_attn"], tl["hb_bwd"])
    dzq, dzk, dzv, dc8, dg_qk = _qk_bwd(dqa, dka, dva, z, gq, gk, B, S, H, tl["tm_prep"])
    dzf, db_f = _gate_bwd(dc8, z, b_pad, B, S, H, f_col // LANES, fp, tl["tc"])
    pieces = [dzq, dzk, dzv, dzgf, dzglu, dzgc, dzf]
    dw_pieces = [_matmul_tn(p, h, f"dw_in_{k}", 1024, 1024, 512) for k, p in enumerate(pieces)]
    dw_out = _matmul_tn(y, dout_b, "dw_out", 1024, 1024, 512)
    dw_t = jnp.concatenate(dw_pieces[:3] + [dw_pieces[6][:H]] + dw_pieces[3:6], axis=0)

    halves_in = dw_t.reshape(4 * in_shard, 2, D // 2).transpose(1, 0, 2)
    halves_out = dw_out.reshape(4, 2, out_shard // 2, D).transpose(1, 0, 2, 3).reshape(2, 2 * out_shard, D)
    got_in, got_out = _pair_swap([halves_in, halves_out])
    own_in = lax.dynamic_index_in_dim(halves_in, ci, 0, keepdims=False)
    own_out = lax.dynamic_index_in_dim(halves_out, ci, 0, keepdims=False)
    part_in = _pair_sum_bf16(own_in, got_in, "pair_sum_in").reshape(4, in_shard, D // 2)
    part_out = _pair_sum_bf16(own_out, got_out, "pair_sum_out").reshape(4, out_shard // 2, D)
    grad_x2, dg_norm, slots_in, slots_out = _dh_rms_bwd(
        pieces, w_pack, x2, norm_g, dout, tl["tm_in"], tl["tk"], [part_in, part_out])
    chip = 2 * xi + yi
    half_in = _sum_slots(slots_in, "chip_sum_in", lax.dynamic_index_in_dim(part_in, chip, 0, keepdims=False))
    half_out = _sum_slots(slots_out, "chip_sum_out", lax.dynamic_index_in_dim(part_out, chip, 0, keepdims=False))
    other_in, other_out = _pair_send([half_in, half_out])

    def both_halves(mine, other, axis):
        return jnp.where(ci == 0, jnp.concatenate([mine, other], axis=axis), jnp.concatenate([other, mine], axis=axis))

    grad_w_t = both_halves(half_in, other_in, 1)
    grad_w_out = both_halves(half_out, other_out, 0)

    small = jnp.concatenate([
        dg_norm[0:1], jnp.pad(db_f[0:1, :], ((0, 0), (0, D - LANES))), dg_qk[0:1], dg_qk[1:2],
        sg_conv[2:3], sg_conv[0:1], sg_conv[1:2], dconv_w], axis=0)
    n_small = small.shape[0]
    small_sum = _sum_slots(_gather_all(small), "small_sum", tr=n_small)
    grad_norm_g, grad_b_f = small_sum[0:1], small_sum[1:2, :H]
    grad_gq, grad_gk = small_sum[2:3].reshape(1, H, dh), small_sum[3:4].reshape(1, H, dh)
    grad_conv_b, grad_ln_g, grad_ln_b = small_sum[4:5], small_sum[5:6], small_sum[6:7]
    grad_conv_w = lax.dynamic_slice_in_dim(small_sum[7:7 + n_taps], chip * cw_shard, cw_shard, axis=1)

    d_t, nm_t, nv_t = _adamw(w_t, grad_w_t, jnp.transpose(m_w_in[0]), jnp.transpose(v_w_in[0]), "adamw_in")
    grad_w_in, d_in, nm_in, nv_in = (jnp.transpose(t)[None] for t in (grad_w_t, d_t, nm_t, nv_t))
    d_out, nm_out, nv_out = (t[None] for t in _adamw(w_out[0], grad_w_out, m_w_out[0], v_w_out[0], "adamw_out"))
    d_cw, nm_cw, nv_cw = (t[None] for t in _adamw(conv_w[0], grad_conv_w, m_conv_w[0], v_conv_w[0], "adamw_conv_w"))

    def rows(ws):
        return jnp.concatenate([jnp.pad(t.reshape(1, -1), ((0, 0), (0, D - t.size))) for t in ws], axis=0)

    small_w = [norm_g, b_forget, q_norm_g, k_norm_g, conv_b, conv_ln_g, conv_ln_b]
    small_m = [m_norm_g, m_b_forget, m_q_norm_g, m_k_norm_g, m_conv_b, m_conv_ln_g, m_conv_ln_b]
    small_v = [v_norm_g, v_b_forget, v_q_norm_g, v_k_norm_g, v_conv_b, v_conv_ln_g, v_conv_ln_b]
    d_s, nm_s, nv_s = _adamw(rows(small_w), small_sum[0:7], rows(small_m), rows(small_v), "adamw_small")

    def unpack(t):
        return [t[k:k + 1, :w.size].reshape(w.shape) for k, w in enumerate(small_w)]

    def order(s, in_, cw, out_):
        ng, bf, qg, kg, cb, lg, lb = s
        return [ng, in_, bf, qg, kg, cw, cb, lg, lb, out_]

    grads = [grad_norm_g, grad_w_in, grad_b_f, grad_gq, grad_gk, grad_conv_w[None],
             grad_conv_b, grad_ln_g, grad_ln_b, grad_w_out[None]]
    return (loss, grad_x2.reshape(B, S, D), *grads,
            *order(unpack(d_s), d_in, d_cw, d_out),
            *order(unpack(nm_s), nm_in, nm_cw, nm_out),
            *order(unpack(nv_s), nv_in, nv_cw, nv_out))
```

```python
import functools

import jax
import jax.numpy as jnp
from jax import lax
from jax.experimental import pallas as pl
from jax.experimental.pallas import tpu as pltpu

F32 = jnp.float32
BF16 = jnp.bfloat16
SDS = jax.ShapeDtypeStruct
MESH = pl.DeviceIdType.MESH

EPS = 1e-6
NEG_INF = -1e30
LANES = 128
SUBLANES = 8
HEAD_DIM = 64
HALO = 32
VMEM_LIMIT = 56 * 1024 * 1024

L_ROWSUM = 64
L_KDECAY = 67
L_LSE = 70
L_D = 65
L_QNORM = 73
L_KNORM = 74
NORM_SLACK = 1.02
SHIFT_MAX = 40.0

ADAM_LR = 0.001
ADAM_B1 = 0.9
ADAM_B2 = 0.999
ADAM_EPS = 1e-08
ADAM_WD = 0.01
ADAM_STEP = 10


def _params(sem, vmem=VMEM_LIMIT):
    return pltpu.CompilerParams(dimension_semantics=sem, vmem_limit_bytes=vmem)


def _sigmoid(x):
    return 1.0 / (1.0 + jnp.exp(-x))


def _split3(x):
    hi = x.astype(BF16).astype(F32)
    r = x - hi
    mid = r.astype(BF16).astype(F32)
    lo = (r - mid).astype(BF16).astype(F32)
    return hi, mid, lo


def _dot(a, b):
    return jnp.dot(a, b, preferred_element_type=F32)


def _dot_nt(a, b):
    return lax.dot_general(a, b, (((1,), (1,)), ((), ())), preferred_element_type=F32)


def _dot_tn(a, b):
    return lax.dot_general(a, b, (((0,), (0,)), ((), ())), preferred_element_type=F32)


def _lane(shape):
    return lax.broadcasted_iota(jnp.int32, shape, 1)


def _lane_col(x, lane, idx):
    return jnp.sum(jnp.where(lane == idx, x, 0.0), axis=-1, keepdims=True)


def _put3(base, lane, start, pieces):
    out = base
    for k, p in enumerate(pieces):
        out = jnp.where(lane == start + k, p, out)
    return out


def _half_stats(t, lo):
    del lo
    hi = t.astype(BF16)
    mid = (t - hi.astype(F32)).astype(BF16)
    row = lax.broadcasted_iota(jnp.int32, (2 * LANES, LANES), 0)
    col = lax.broadcasted_iota(jnp.int32, (2 * LANES, LANES), 1)
    same_half = (jnp.bitwise_and(row, LANES - 1) < HEAD_DIM) == (col < HEAD_DIM)
    return _dot(jnp.concatenate([hi, mid], axis=1), jnp.where(same_half, 1.0, 0.0).astype(BF16))


def _fwd_in(x2, g, w_t, tm, tn):
    T, D = x2.shape
    N = w_t.shape[0]

    def body(x_ref, g_ref, w_ref, z_ref, h_ref):
        @pl.when(pl.program_id(1) == 0)
        def _():
            x = x_ref[...]
            r = lax.rsqrt(jnp.mean(x * x, axis=-1, keepdims=True) + EPS)
            h_ref[...] = (x * r * g_ref[...]).astype(BF16)

        z_ref[...] = _dot_nt(h_ref[...], w_ref[...])

    return pl.pallas_call(
        body, name="fwd_in", grid=(T // tm, N // tn),
        in_specs=[pl.BlockSpec((tm, D), lambda i, j: (i, 0)),
                  pl.BlockSpec((1, D), lambda i, j: (0, 0)),
                  pl.BlockSpec((tn, D), lambda i, j: (j, 0))],
        out_specs=[pl.BlockSpec((tm, tn), lambda i, j: (i, j)),
                   pl.BlockSpec((tm, D), lambda i, j: (i, 0))],
        out_shape=[SDS((T, N), F32), SDS((T, D), BF16)],
        compiler_params=_params(("parallel", "arbitrary")),
    )(x2, g, w_t)


def _tri_cumsum(x, reverse):
    t = x.shape[0]
    row = lax.broadcasted_iota(jnp.int32, (t, t), 0)
    col = lax.broadcasted_iota(jnp.int32, (t, t), 1)
    tri = (row <= col) if reverse else (row >= col)
    tri = jnp.where(tri, 1.0, 0.0).astype(BF16)
    hi, mid, lo = _split3(x)
    return _dot(tri, hi.astype(BF16)) + _dot(tri, mid.astype(BF16)) + _dot(tri, lo.astype(BF16))


def _gate_fwd(z, b_pad, B, S, H, col_blk, tc):
    T = B * S
    nsb = S // tc

    def body(zf_ref, b_ref, c_ref, carry):
        @pl.when(pl.program_id(1) == 0)
        def _():
            carry[...] = jnp.zeros_like(carry)

        x = zf_ref[...] + b_ref[...]
        lf = jnp.minimum(x, 0.0) - jnp.log(1.0 + jnp.exp(-jnp.abs(x)))
        lf = jnp.where(_lane(lf.shape) < H, lf, 0.0)
        c_ref[...] = _tri_cumsum(lf, False) + carry[...]
        carry[...] = carry[...] + jnp.sum(lf, axis=0, keepdims=True)

    return pl.pallas_call(
        body, name="gate_fwd", grid=(B, nsb),
        in_specs=[pl.BlockSpec((tc, LANES), lambda b, s: (b * nsb + s, col_blk)),
                  pl.BlockSpec((1, LANES), lambda b, s: (0, 0))],
        out_specs=pl.BlockSpec((tc, LANES), lambda b, s: (b * nsb + s, 0)),
        out_shape=SDS((T, LANES), F32),
        scratch_shapes=[pltpu.VMEM((1, LANES), F32)],
        compiler_params=_params(("parallel", "arbitrary")),
    )(z, b_pad)


def _qk_normalize(x, g, lo):
    r = lax.rsqrt(_half_stats(x * x, lo) * (1.0 / HEAD_DIM) + EPS)
    return x * r * g


def _head_norms(x, lo):
    own = jnp.sqrt(_half_stats(x * x, lo)) * NORM_SLACK
    return [pltpu.roll(own, HEAD_DIM, 1), own]


def _attn_prep(z, c, gq, gk, B, S, H, tm):
    T = B * S
    FW = H * HEAD_DIM
    nsb = S // tm
    nfb = FW // LANES
    scale = HEAD_DIM ** -0.5

    def body(zq_ref, zk_ref, zv_ref, c_ref, gq_ref, gk_ref, qa_ref, ka_ref, va_ref):
        p = pl.program_id(1)
        lane = _lane((tm, LANES))
        lo = lane < HEAD_DIM
        qn = _qk_normalize(zq_ref[...], gq_ref[...], lo) * scale
        kn = _qk_normalize(zk_ref[...], gk_ref[...], lo)
        v = zv_ref[...]
        cc = c_ref[...]
        ones_q = ((lane >= L_KDECAY) & (lane < L_KDECAY + 3)).astype(F32)
        ones_k = (((lane >= L_ROWSUM) & (lane < L_ROWSUM + 3)) | ((lane >= L_LSE) & (lane < L_LSE + 3))).astype(F32)
        ones_v = ((lane >= L_ROWSUM) & (lane < L_D + 3)).astype(F32)
        q_norms, k_norms = _head_norms(qn, lo), _head_norms(kn, lo)
        for e in range(2):
            if e == 0:
                qe, ke, ve = qn, kn, v
            else:
                qe, ke, ve = (pltpu.roll(t, HEAD_DIM, 1) for t in (qn, kn, v))
            ch = _lane_col(cc, lane, 2 * p + e)
            pieces = _split3(ch)
            qa = jnp.where(lo, qe, _put3(ones_q, lane, L_ROWSUM, pieces))
            qa = jnp.where(lane == L_QNORM, q_norms[e], qa)
            ka = jnp.where(lo, ke, _put3(ones_k, lane, L_KDECAY, [-t for t in pieces]))
            ka = jnp.where(lane == L_KNORM, k_norms[e], ka)
            va = jnp.where(lo, ve, ones_v)
            qa_ref[0, e] = qa.astype(BF16)
            ka_ref[0, e] = ka.astype(BF16)
            va_ref[0, e] = va.astype(BF16)

    zspec = lambda off: pl.BlockSpec((tm, LANES), lambda i, p: (i, off + p))
    gspec = pl.BlockSpec((1, LANES), lambda i, p: (0, p))
    ospec = pl.BlockSpec((1, 2, tm, LANES), lambda i, p: (i // nsb, p, i % nsb, 0))
    oshape = SDS((B, H, S, LANES), BF16)
    return pl.pallas_call(
        body, name="attn_prep", grid=(T // tm, H // 2),
        in_specs=[zspec(0), zspec(nfb), zspec(2 * nfb),
                  pl.BlockSpec((tm, LANES), lambda i, p: (i, 0)), gspec, gspec],
        out_specs=[ospec, ospec, ospec],
        out_shape=[oshape, oshape, oshape],
        compiler_params=_params(("parallel", "arbitrary")),
    )(z, z, z, c, gq, gk)


def _attn_fwd(qa, ka, va, t, hb):
    B, H, S, _ = qa.shape
    nq = S // t

    def body(q_ref, k_ref, v_ref, o_ref, m_ref, acc_ref, kmax_ref, qs_ref):
        i = pl.program_id(2)
        lane = _lane((t, LANES))

        @pl.when(i == 0)
        def _():
            for e in range(hb):
                norms = jnp.where(_lane((S, LANES)) == L_KNORM, k_ref[0, e].astype(F32), 0.0)
                kmax_ref[e] = jnp.full((1, LANES), jnp.max(norms), F32)

        shifts = [_lane_col(q_ref[0, e].astype(F32), lane, L_QNORM) * kmax_ref[e][:, 0:1] for e in range(hb)]
        worst = shifts[0]
        for e in range(1, hb):
            worst = jnp.maximum(worst, shifts[e])
        bounded = jnp.max(worst) <= SHIFT_MAX
        acc_ref[...] = jnp.zeros_like(acc_ref)

        def tiles(step):
            def loop_body(j, carry):
                step(j, False)
                return carry

            lax.fori_loop(0, i, loop_body, 0)
            step(i, True)

        def keep_mask(n=t):
            return lax.broadcasted_iota(jnp.int32, (n, n), 0) >= lax.broadcasted_iota(jnp.int32, (n, n), 1)

        def finish(e, shift):
            acc = acc_ref[e]
            l = _lane_col(acc, lane, L_ROWSUM)
            o_ref[0, e] = jnp.where(lane < HEAD_DIM, acc / l, shift + jnp.log(l))

        @pl.when(bounded)
        def _():
            for e in range(hb):
                qs_ref[e] = _put3(q_ref[0, e].astype(F32), lane, L_LSE, _split3(-shifts[e])).astype(BF16)

            def pair(e, q_rows, k_start, n, masked):
                k_rows = pl.ds(pl.multiple_of(k_start, n), n)
                p = jnp.exp(_dot_nt(qs_ref[e, q_rows, :], k_ref[0, e, k_rows, :]))
                if masked:
                    p = jnp.where(keep_mask(n), p, 0.0)
                acc_ref[e, q_rows, :] = acc_ref[e, q_rows, :] + _dot(p.astype(BF16), v_ref[0, e, k_rows, :])

            def step(j, masked):
                for e in range(hb):
                    if masked:
                        h = t // 2
                        pair(e, slice(0, h), j * t, h, True)
                        pair(e, slice(h, t), j * t, h, False)
                        pair(e, slice(h, t), j * t + h, h, True)
                    else:
                        pair(e, slice(0, t), j * t, t, False)

            tiles(step)
            for e in range(hb):
                finish(e, shifts[e])

        @pl.when(jnp.logical_not(bounded))
        def _():
            m_ref[...] = jnp.full_like(m_ref, NEG_INF)

            def step(j, masked):
                rows = pl.ds(pl.multiple_of(j * t, t), t)
                for e in range(hb):
                    s = _dot_nt(q_ref[0, e], k_ref[0, e, rows, :])
                    if masked:
                        s = jnp.where(keep_mask(), s, NEG_INF)
                    m_prev = m_ref[e]
                    m_new = jnp.maximum(m_prev, jnp.max(s, axis=-1, keepdims=True))
                    alpha = jnp.exp(m_prev - m_new)
                    p = jnp.exp(s - m_new).astype(BF16)
                    acc_ref[e] = alpha * acc_ref[e] + _dot(p, v_ref[0, e, rows, :])
                    m_ref[e] = m_new

            tiles(step)
            for e in range(hb):
                finish(e, m_ref[e])

    return pl.pallas_call(
        body, name="attn_fwd", grid=(B, H // hb, nq),
        in_specs=[pl.BlockSpec((1, hb, t, LANES), lambda b, h, i: (b, h, i, 0)),
                  pl.BlockSpec((1, hb, S, LANES), lambda b, h, i: (b, h, 0, 0)),
                  pl.BlockSpec((1, hb, S, LANES), lambda b, h, i: (b, h, 0, 0))],
        out_specs=pl.BlockSpec((1, hb, t, LANES), lambda b, h, i: (b, h, i, 0)),
        out_shape=SDS((B, H, S, LANES), F32),
        scratch_shapes=[pltpu.VMEM((hb, t, 1), F32), pltpu.VMEM((hb, t, LANES), F32),
                        pltpu.VMEM((hb, 1, LANES), F32), pltpu.VMEM((hb, t, LANES), BF16)],
        compiler_params=_params(("parallel", "parallel", "arbitrary")),
    )(qa, ka, va)


def _fill_shifts(ext_ref, sh_ref):
    rows = sh_ref.shape[1]
    for b in range(1, SUBLANES):
        sh_ref[b - 1] = ext_ref[pl.ds(b, rows), :]


def _tap_window(ext_ref, sh_ref, off, tm, cols):
    b = off % SUBLANES
    if b == 0:
        return ext_ref[pl.ds(off, tm), cols]
    return sh_ref[b - 1, pl.ds(off - b, tm), cols]


def _conv_taps(w_ref, ext_ref, sh_ref, out_ref, n_taps, tm, offset_of, bias_ref=None):
    for cc in range(out_ref.shape[1] // LANES):
        cols = slice(cc * LANES, (cc + 1) * LANES)
        acc = None
        for j in range(n_taps):
            term = w_ref[j:j + 1, cols] * _tap_window(ext_ref, sh_ref, offset_of(j), tm, cols)
            acc = term if acc is None else acc + term
        out_ref[:, cols] = acc if bias_ref is None else acc + bias_ref[:, cols]


def _layernorm_stats(u2):
    mu = jnp.mean(u2, axis=-1, keepdims=True)
    xc = u2 - mu
    rstd = lax.rsqrt(jnp.mean(xc * xc, axis=-1, keepdims=True) + EPS)
    return xc * rstd, rstd


def _fwd_out(oa, z, x2, tgt, conv_w, conv_b, ln_g, ln_b, w_out, B, S, H, n_taps, tm):
    T, D = x2.shape
    FW = H * HEAD_DIM
    CW = conv_w.shape[1]
    nsb = S // tm
    hb = tm // HALO

    def body(oa_ref, gf_ref, ga_ref, gb_ref, gc_ref, ha_ref, hb_ref, x_ref, t_ref, w_ref, cb_ref, lg_ref,
             lb_ref, wo_ref, y_ref, u2_ref, a_ref, do_ref, dob_ref, dy_ref, loss_ref, ext_ref, sh_ref):
        first_step = (pl.program_id(0) == 0) & (pl.program_id(1) == 0)

        @pl.when(first_step)
        def _():
            loss_ref[...] = jnp.zeros_like(loss_ref)

        u1 = ga_ref[...] * _sigmoid(gb_ref[...])
        halo = ha_ref[...] * _sigmoid(hb_ref[...])
        ext_ref[0:HALO, :] = jnp.where(pl.program_id(1) > 0, halo, 0.0)
        ext_ref[HALO:, :] = u1
        _fill_shifts(ext_ref, sh_ref)
        _conv_taps(w_ref, ext_ref, sh_ref, u2_ref, n_taps, tm, lambda j: HALO - (n_taps - 1) + j, cb_ref)
        uh, _ = _layernorm_stats(u2_ref[...])
        u3 = uh * lg_ref[...] + lb_ref[...]
        gc = gc_ref[...]
        yu = u3 * _sigmoid(u3) * (gc * _sigmoid(gc))
        y_ref[:, FW:] = yu.astype(BF16)

        lane = _lane((tm, LANES))
        lo = lane < HEAD_DIM
        for p in range(H // 2):
            a_ref[:, p * LANES:(p + 1) * LANES] = jnp.where(
                lo, oa_ref[0, 2 * p], pltpu.roll(oa_ref[0, 2 * p + 1], HEAD_DIM, 1))
        gf = gf_ref[...]
        y_ref[:, :FW] = (a_ref[...] * (gf * _sigmoid(gf))).astype(BF16)

        out = x_ref[...] + _dot(y_ref[...], wo_ref[...])
        diff = out - t_ref[...]
        loss_ref[...] = loss_ref[...] + jnp.sum(diff * diff)
        dout = diff * (1.0 / D)
        do_ref[...] = dout
        dob = dout.astype(BF16)
        dob_ref[...] = dob
        dy_ref[...] = _dot_nt(dob, wo_ref[...])

    row = lambda b, s: b * nsb + s
    zspec = lambda cb: pl.BlockSpec((tm, FW), lambda b, s: (row(b, s), cb))
    hspec = lambda cb: pl.BlockSpec((HALO, CW), lambda b, s: (jnp.maximum(row(b, s) * hb - 1, 0), cb))
    vspec = pl.BlockSpec((1, CW), lambda b, s: (0, 0))
    tspec = lambda w: pl.BlockSpec((tm, w), lambda b, s: (row(b, s), 0))
    return pl.pallas_call(
        body, name="fwd_out", grid=(B, nsb),
        in_specs=[pl.BlockSpec((1, H, tm, LANES), lambda b, s: (b, 0, s, 0)),
                  zspec(3), zspec(4), zspec(5), zspec(6), hspec(4), hspec(5),
                  tspec(D), tspec(D),
                  pl.BlockSpec((HALO, CW), lambda b, s: (0, 0)), vspec, vspec, vspec,
                  pl.BlockSpec((FW + CW, D), lambda b, s: (0, 0))],
        out_specs=[tspec(FW + CW), tspec(CW), tspec(FW), tspec(D), tspec(D), tspec(FW + CW),
                   pl.BlockSpec((8, LANES), lambda b, s: (0, 0))],
        out_shape=[SDS((T, FW + CW), BF16), SDS((T, CW), F32), SDS((T, FW), F32), SDS((T, D), F32),
                   SDS((T, D), BF16), SDS((T, FW + CW), F32), SDS((8, LANES), F32)],
        scratch_shapes=[pltpu.VMEM((tm + HALO, CW), F32),
                        pltpu.VMEM((SUBLANES - 1, tm + HALO - SUBLANES, CW), F32)],
        compiler_params=_params(("arbitrary", "arbitrary")),
    )(oa, z, z, z, z, z, z, x2, tgt, conv_w, conv_b, ln_g, ln_b, w_out)


def _bwd_prep(dy, z, a_nat, oa, qa, u2, ln_g, ln_b, B, S, H, tm):
    T = B * S
    FW = H * HEAD_DIM
    CW = u2.shape[1]
    nsb = S // tm

    def body(dya_ref, dyu_ref, gf_ref, gc_ref, a_ref, oa_ref, qa_ref, u2_ref, lg_ref, lb_ref,
             dzgf_ref, dzgc_ref, du2_ref, doa_ref, qb_ref, sg_ref):
        first_step = (pl.program_id(0) == 0) & (pl.program_id(1) == 0)

        @pl.when(first_step)
        def _():
            sg_ref[...] = jnp.zeros_like(sg_ref)

        gf = gf_ref[...]
        sg = _sigmoid(gf)
        a = a_ref[...]
        dya = dya_ref[...]
        da = dya * (gf * sg)
        dzgf_ref[...] = (dya * a * (sg * (1.0 + gf * (1.0 - sg)))).astype(BF16)
        dd = da * a
        lane = _lane((tm, LANES))
        lo = lane < HEAD_DIM
        for p in range(H // 2):
            cols = slice(p * LANES, (p + 1) * LANES)
            da_p = da[:, cols]
            dd_p = dd[:, cols]
            d_heads = (jnp.sum(jnp.where(lo, dd_p, 0.0), axis=-1, keepdims=True),
                       jnp.sum(jnp.where(lo, 0.0, dd_p), axis=-1, keepdims=True))
            for e in range(2):
                da_e = da_p if e == 0 else pltpu.roll(da_p, HEAD_DIM, 1)
                d_e = d_heads[e]
                aug = _put3(jnp.zeros((tm, LANES), F32), lane, L_D, _split3(-d_e))
                doa_ref[0, 2 * p + e] = jnp.where(lo, da_e, aug).astype(BF16)
                lse = _lane_col(oa_ref[0, 2 * p + e], lane, L_ROWSUM)
                qb = _put3(qa_ref[0, 2 * p + e].astype(F32), lane, L_LSE, _split3(-lse))
                qb_ref[0, 2 * p + e] = qb.astype(BF16)

        gc = gc_ref[...]
        sc = _sigmoid(gc)
        dyu = dyu_ref[...]
        uh, rstd = _layernorm_stats(u2_ref[...])
        u3 = uh * lg_ref[...] + lb_ref[...]
        s3 = _sigmoid(u3)
        dzgc_ref[...] = (dyu * (u3 * s3) * (sc * (1.0 + gc * (1.0 - sc)))).astype(BF16)
        du3 = dyu * (gc * sc) * (s3 * (1.0 + u3 * (1.0 - s3)))
        sg_ref[0:1, :] = sg_ref[0:1, :] + jnp.sum(du3 * uh, axis=0, keepdims=True)
        sg_ref[1:2, :] = sg_ref[1:2, :] + jnp.sum(du3, axis=0, keepdims=True)
        duh = du3 * lg_ref[...]
        du2 = rstd * (duh - jnp.mean(duh, axis=-1, keepdims=True)
                      - uh * jnp.mean(duh * uh, axis=-1, keepdims=True))
        sg_ref[2:3, :] = sg_ref[2:3, :] + jnp.sum(du2, axis=0, keepdims=True)
        du2_ref[...] = du2

    row = lambda b, s: b * nsb + s
    tspec = lambda w, cb=0: pl.BlockSpec((tm, w), lambda b, s: (row(b, s), cb))
    hspec = pl.BlockSpec((1, H, tm, LANES), lambda b, s: (b, 0, s, 0))
    vspec = pl.BlockSpec((1, CW), lambda b, s: (0, 0))
    return pl.pallas_call(
        body, name="bwd_prep", grid=(B, nsb),
        in_specs=[tspec(FW, 0), tspec(CW, 1), tspec(FW, 3), tspec(CW, 6), tspec(FW), hspec, hspec,
                  tspec(CW), vspec, vspec],
        out_specs=[tspec(FW), tspec(CW), tspec(CW), hspec, hspec,
                   pl.BlockSpec((8, CW), lambda b, s: (0, 0))],
        out_shape=[SDS((T, FW), BF16), SDS((T, CW), BF16), SDS((T, CW), F32),
                   SDS((B, H, S, LANES), BF16), SDS((B, H, S, LANES), BF16), SDS((8, CW), F32)],
        compiler_params=_params(("arbitrary", "arbitrary")),
    )(dy, dy, z, z, a_nat, oa, qa, u2, ln_g, ln_b)


def _conv_bwd(du2, z, conv_w, B, S, n_taps, tm):
    T, CW = du2.shape
    nsb = S // tm
    hb = tm // HALO

    def body(d_ref, dh_ref, ga_ref, gb_ref, ha_ref, hb_ref, w_ref, dz_ref, dw_ref,
             extu_ref, extd_ref, shu_ref, shd_ref, du1_ref, dwacc_ref):
        s = pl.program_id(1)
        first_step = (pl.program_id(0) == 0) & (s == 0)
        last_step = (pl.program_id(0) == B - 1) & (s == nsb - 1)

        @pl.when(first_step)
        def _():
            dwacc_ref[...] = jnp.zeros_like(dwacc_ref)

        ga = ga_ref[...]
        sb = _sigmoid(gb_ref[...])
        halo = ha_ref[...] * _sigmoid(hb_ref[...])
        extu_ref[0:HALO, :] = jnp.where(s > 0, halo, 0.0)
        extu_ref[HALO:, :] = ga * sb
        extd_ref[0:tm, :] = d_ref[...]
        extd_ref[tm:, :] = jnp.where(s < nsb - 1, dh_ref[...], 0.0)
        _fill_shifts(extu_ref, shu_ref)
        _fill_shifts(extd_ref, shd_ref)
        _conv_taps(w_ref, extd_ref, shd_ref, du1_ref, n_taps, tm, lambda j: n_taps - 1 - j)
        for cc in range(CW // LANES):
            cols = slice(cc * LANES, (cc + 1) * LANES)
            parts = [None] * n_taps
            for r in range(tm // SUBLANES):
                dv = d_ref[r * SUBLANES:(r + 1) * SUBLANES, cols]
                for j in range(n_taps):
                    off = HALO - (n_taps - 1) + j + r * SUBLANES
                    term = dv * _tap_window(extu_ref, shu_ref, off, SUBLANES, cols)
                    parts[j] = term if parts[j] is None else parts[j] + term
            for j in range(n_taps):
                rows = slice(j * SUBLANES, (j + 1) * SUBLANES)
                dwacc_ref[rows, cols] = dwacc_ref[rows, cols] + parts[j]
        du1 = du1_ref[...]
        dz_ref[:, :CW] = (du1 * sb).astype(BF16)
        dz_ref[:, CW:] = (du1 * ga * (sb * (1.0 - sb))).astype(BF16)

        @pl.when(last_step)
        def _():
            dw_ref[...] = jnp.zeros_like(dw_ref)
            for j in range(n_taps):
                dw_ref[j:j + 1, :] = jnp.sum(dwacc_ref[j * SUBLANES:(j + 1) * SUBLANES, :], axis=0, keepdims=True)

    row = lambda b, s: b * nsb + s
    last_halo = T // HALO - 1
    return pl.pallas_call(
        body, name="conv_bwd", grid=(B, nsb),
        in_specs=[pl.BlockSpec((tm, CW), lambda b, s: (row(b, s), 0)),
                  pl.BlockSpec((HALO, CW), lambda b, s: (jnp.minimum((row(b, s) + 1) * hb, last_halo), 0)),
                  pl.BlockSpec((tm, CW), lambda b, s: (row(b, s), 4)),
                  pl.BlockSpec((tm, CW), lambda b, s: (row(b, s), 5)),
                  pl.BlockSpec((HALO, CW), lambda b, s: (jnp.maximum(row(b, s) * hb - 1, 0), 4)),
                  pl.BlockSpec((HALO, CW), lambda b, s: (jnp.maximum(row(b, s) * hb - 1, 0), 5)),
                  pl.BlockSpec((HALO, CW), lambda b, s: (0, 0))],
        out_specs=[pl.BlockSpec((tm, 2 * CW), lambda b, s: (row(b, s), 0)),
                   pl.BlockSpec((HALO, CW), lambda b, s: (0, 0))],
        out_shape=[SDS((T, 2 * CW), BF16), SDS((HALO, CW), F32)],
        scratch_shapes=[pltpu.VMEM((tm + HALO, CW), F32), pltpu.VMEM((tm + HALO, CW), F32),
                        pltpu.VMEM((SUBLANES - 1, tm + HALO - SUBLANES, CW), F32),
                        pltpu.VMEM((SUBLANES - 1, tm + HALO - SUBLANES, CW), F32),
                        pltpu.VMEM((tm, CW), F32), pltpu.VMEM((HALO * SUBLANES, CW), F32)],
        compiler_params=_params(("arbitrary", "arbitrary")),
    )(du2, du2, z, z, z, z, conv_w)


def _attn_bwd(qb, ka, va, doa, t, hb):
    B, H, S, _ = qb.shape
    nk = S // t

    def body(q_ref, k_ref, v_ref, do_ref, dq_ref, dk_ref, dv_ref):
        j = pl.program_id(2)

        @pl.when(j == 0)
        def _():
            dq_ref[...] = jnp.zeros_like(dq_ref)

        dk_ref[...] = jnp.zeros_like(dk_ref)
        dv_ref[...] = jnp.zeros_like(dv_ref)

        def pair(e, q_start, k_rows, n, masked):
            q_rows = pl.ds(pl.multiple_of(q_start, n), n)
            k = k_ref[0, e, k_rows, :]
            q = q_ref[0, e, q_rows, :]
            do = do_ref[0, e, q_rows, :]
            p = jnp.exp(_dot_nt(q, k))
            if masked:
                keep = lax.broadcasted_iota(jnp.int32, (n, n), 0) >= lax.broadcasted_iota(jnp.int32, (n, n), 1)
                p = jnp.where(keep, p, 0.0)
            ds = (p * _dot_nt(do, v_ref[0, e, k_rows, :])).astype(BF16)
            dv_ref[0, e, k_rows, :] = dv_ref[0, e, k_rows, :] + _dot_tn(p.astype(BF16), do)
            dk_ref[0, e, k_rows, :] = dk_ref[0, e, k_rows, :] + _dot_tn(ds, q)
            dq_ref[0, e, q_rows, :] = dq_ref[0, e, q_rows, :] + _dot(ds, k)

        def step(i, masked):
            for e in range(hb):
                if masked:
                    h = t // 2
                    pair(e, i * t, slice(0, h), h, True)
                    pair(e, i * t + h, slice(0, h), h, False)
                    pair(e, i * t + h, slice(h, t), h, True)
                else:
                    pair(e, i * t, slice(0, t), t, False)

        step(j, True)

        def loop_body(i, carry):
            step(i, False)
            return carry

        lax.fori_loop(j + 1, nk, loop_body, 0)

    full = pl.BlockSpec((1, hb, S, LANES), lambda b, h, j: (b, h, 0, 0))
    blk = pl.BlockSpec((1, hb, t, LANES), lambda b, h, j: (b, h, j, 0))
    oshape = SDS((B, H, S, LANES), F32)
    return pl.pallas_call(
        body, name="attn_bwd", grid=(B, H // hb, nk),
        in_specs=[full, blk, blk, full],
        out_specs=[full, blk, blk],
        out_shape=[oshape, oshape, oshape],
        compiler_params=_params(("parallel", "parallel", "arbitrary")),
    )(qb, ka, va, doa)


def _qk_bwd(dqa, dka, dva, z, gq, gk, B, S, H, tm):
    T = B * S
    FW = H * HEAD_DIM
    nsb = S // tm
    nfb = FW // LANES
    scale = HEAD_DIM ** -0.5

    def body(dq_ref, dk_ref, dv_ref, zq_ref, zk_ref, gq_ref, gk_ref, dzq_ref, dzk_ref, dzv_ref, dc_ref, dg_ref):
        p = pl.program_id(0)

        @pl.when(pl.program_id(1) == 0)
        def _():
            dg_ref[...] = jnp.zeros_like(dg_ref)

        lane = _lane((tm, LANES))
        lo = lane < HEAD_DIM

        def natural(ref):
            return jnp.where(lo, ref[0, 0], pltpu.roll(ref[0, 1], HEAD_DIM, 1))

        def norm_bwd(dn, x, g, row, out_ref):
            r = lax.rsqrt(_half_stats(x * x, lo) * (1.0 / HEAD_DIM) + EPS)
            xh = x * r
            dg_ref[row:row + 1, :] = dg_ref[row:row + 1, :] + jnp.sum(dn * xh, axis=0, keepdims=True)
            dxh = dn * g
            mm = _half_stats(dxh * xh, lo) * (1.0 / HEAD_DIM)
            out_ref[...] = (r * (dxh - xh * mm)).astype(BF16)

        norm_bwd(natural(dq_ref) * scale, zq_ref[...], gq_ref[...], 0, dzq_ref)
        norm_bwd(natural(dk_ref), zk_ref[...], gk_ref[...], 1, dzk_ref)
        dzv_ref[...] = natural(dv_ref).astype(BF16)

        dc = jnp.zeros((tm, LANES), F32)
        for e in range(2):
            val = _lane_col(dq_ref[0, e], lane, L_ROWSUM) - _lane_col(dk_ref[0, e], lane, L_KDECAY)
            dc = jnp.where(lane == 2 * p + e, val, dc)
        dc_ref[0] = dc

    hspec = pl.BlockSpec((1, 2, tm, LANES), lambda p, i: (i // nsb, p, i % nsb, 0))
    zspec = lambda off: pl.BlockSpec((tm, LANES), lambda p, i: (i, off + p))
    gspec = pl.BlockSpec((1, LANES), lambda p, i: (0, p))
    ospec = pl.BlockSpec((tm, LANES), lambda p, i: (i, p))
    return pl.pallas_call(
        body, name="qk_bwd", grid=(H // 2, T // tm),
        in_specs=[hspec, hspec, hspec, zspec(0), zspec(nfb), gspec, gspec],
        out_specs=[ospec, ospec, ospec,
                   pl.BlockSpec((1, tm, LANES), lambda p, i: (p, i, 0)),
                   pl.BlockSpec((8, LANES), lambda p, i: (0, p))],
        out_shape=[SDS((T, FW), BF16), SDS((T, FW), BF16), SDS((T, FW), BF16),
                   SDS((H // 2, T, LANES), F32), SDS((8, FW), F32)],
        compiler_params=_params(("parallel", "arbitrary")),
    )(dqa, dka, dva, z, z, gq, gk)


def _gate_bwd(dc8, z, b_pad, B, S, H, col_blk, fp, tc):
    T = B * S
    nsb = S // tc
    npair = dc8.shape[0]

    def body(dc_ref, zf_ref, b_ref, dz_ref, db_ref, carry):
        first_step = (pl.program_id(0) == 0) & (pl.program_id(1) == 0)

        @pl.when(first_step)
        def _():
            db_ref[...] = jnp.zeros_like(db_ref)

        @pl.when(pl.program_id(1) == 0)
        def _():
            carry[...] = jnp.zeros_like(carry)

        dc = dc_ref[0]
        for k in range(1, npair):
            dc = dc + dc_ref[k]
        dlf = _tri_cumsum(dc, True) + carry[...]
        carry[...] = carry[...] + jnp.sum(dc, axis=0, keepdims=True)
        x = zf_ref[...] + b_ref[...]
        dlogit = dlf * _sigmoid(-x)
        db_ref[0:1, :] = db_ref[0:1, :] + jnp.sum(dlogit, axis=0, keepdims=True)
        dz_ref[...] = jnp.zeros_like(dz_ref)
        dz_ref[:, :LANES] = dlogit.astype(BF16)

    rrow = lambda b, s: b * nsb + (nsb - 1 - s)
    return pl.pallas_call(
        body, name="gate_bwd", grid=(B, nsb),
        in_specs=[pl.BlockSpec((npair, tc, LANES), lambda b, s: (0, rrow(b, s), 0)),
                  pl.BlockSpec((tc, LANES), lambda b, s: (rrow(b, s), col_blk)),
                  pl.BlockSpec((1, LANES), lambda b, s: (0, 0))],
        out_specs=[pl.BlockSpec((tc, fp), lambda b, s: (rrow(b, s), 0)),
                   pl.BlockSpec((8, LANES), lambda b, s: (0, 0))],
        out_shape=[SDS((T, fp), BF16), SDS((8, LANES), F32)],
        scratch_shapes=[pltpu.VMEM((1, LANES), F32)],
        compiler_params=_params(("arbitrary", "arbitrary")),
    )(dc8, z, b_pad)


def _matmul_tn(a, b, name, tmm, tn, tk):
    T, M = a.shape
    N = b.shape[1]
    tmm, tn, tk = min(tmm, M), min(tn, N), min(tk, T)

    def body(a_ref, b_ref, o_ref):
        @pl.when(pl.program_id(2) == 0)
        def _():
            o_ref[...] = jnp.zeros_like(o_ref)

        o_ref[...] = o_ref[...] + _dot_tn(a_ref[...], b_ref[...])

    return pl.pallas_call(
        body, name=name, grid=(M // tmm, N // tn, T // tk),
        in_specs=[pl.BlockSpec((tk, tmm), lambda i, j, k: (k, i)),
                  pl.BlockSpec((tk, tn), lambda i, j, k: (k, j))],
        out_specs=pl.BlockSpec((tmm, tn), lambda i, j, k: (i, j)),
        out_shape=SDS((M, N), F32),
        compiler_params=_params(("parallel", "parallel", "arbitrary")),
    )(a, b)


def _dh_rms_bwd(pieces, w_t, x2, g, dout, tm, tk, parts):
    T, D = x2.shape
    nks = [p.shape[1] // tk for p in pieces]
    starts = [sum(nks[:k]) for k in range(len(pieces))]
    nk = sum(nks)
    ni = T // tm
    n = len(parts)

    def body(*refs):
        dz_refs = refs[:len(pieces)]
        w_ref, x_ref, g_ref, do_ref = refs[len(pieces):len(pieces) + 4]
        part_refs = refs[len(pieces) + 4:len(pieces) + 4 + n]
        gx_ref, dg_ref = refs[len(pieces) + 4 + n:len(pieces) + 6 + n]
        slot_refs = refs[len(pieces) + 6 + n:len(pieces) + 6 + 2 * n]
        acc_ref, send_sems, recv_sems = refs[len(pieces) + 6 + 2 * n:]
        k = pl.program_id(1)
        first_step = (pl.program_id(0) == 0) & (k == 0)
        last_step = (pl.program_id(0) == ni - 1) & (k == nk - 1)
        x, y, c = _place()
        chips = [(1 - x, y), (x, 1 - y), (1 - x, 1 - y)]

        def copy(a, f, to):
            cx, cy = chips[f]
            return pltpu.make_async_remote_copy(
                src_ref=part_refs[a].at[2 * cx + cy], dst_ref=slot_refs[a].at[f],
                send_sem=send_sems.at[a * 3 + f], recv_sem=recv_sems.at[a * 3 + f],
                device_id=to, device_id_type=MESH)

        @pl.when(first_step)
        def _():
            dg_ref[...] = jnp.zeros_like(dg_ref)
            for a in range(n):
                for f in range(3):
                    copy(a, f, (*chips[f], c)).start()

        @pl.when(last_step)
        def _():
            for a in range(n):
                for f in range(3):
                    copy(a, f, (x, y, c)).wait_recv()
            for a in range(n):
                for f in range(3):
                    copy(a, f, (*chips[f], c)).wait_send()

        @pl.when(k == 0)
        def _():
            acc_ref[...] = jnp.zeros_like(acc_ref)

        for dz_ref, st, cnt in zip(dz_refs, starts, nks):
            @pl.when((k >= st) & (k < st + cnt))
            def _(dz_ref=dz_ref):
                acc_ref[...] = acc_ref[...] + _dot(dz_ref[...], w_ref[...])

        @pl.when(k == nk - 1)
        def _():
            x = x_ref[...]
            r = lax.rsqrt(jnp.mean(x * x, axis=-1, keepdims=True) + EPS)
            xh = x * r
            dh = acc_ref[...]
            dg_ref[0:1, :] = dg_ref[0:1, :] + jnp.sum(dh * xh, axis=0, keepdims=True)
            dxn = dh * g_ref[...]
            gx_ref[...] = do_ref[...] + r * (dxn - xh * jnp.mean(dxn * xh, axis=-1, keepdims=True))

    def piece_spec(st, cnt):
        return pl.BlockSpec((tm, tk), lambda i, k: (i, jnp.clip(k - st, 0, cnt - 1)))

    tspec = pl.BlockSpec((tm, D), lambda i, k: (i, 0))
    return pl.pallas_call(
        body, name="dh_rms_bwd", grid=(T // tm, nk),
        in_specs=[piece_spec(st, cnt) for st, cnt in zip(starts, nks)]
        + [pl.BlockSpec((tk, D), lambda i, k: (k, 0)), tspec, pl.BlockSpec((1, D), lambda i, k: (0, 0)), tspec]
        + [ANY] * n,
        out_specs=[tspec, pl.BlockSpec((8, D), lambda i, k: (0, 0))] + [ANY] * n,
        out_shape=[SDS((T, D), F32), SDS((8, D), F32)] + [SDS((3,) + p.shape[1:], p.dtype) for p in parts],
        scratch_shapes=[pltpu.VMEM((tm, D), F32),
                        pltpu.SemaphoreType.DMA((3 * n,)), pltpu.SemaphoreType.DMA((3 * n,))],
        compiler_params=_params(("arbitrary", "arbitrary")),
    )(*pieces, w_t, x2, g, dout, *parts)


def _block_plan(R, C, tr, tc):
    br = min(tr, R)
    if R % br == 0:
        return (br, C), R // br, lambda i: (i, 0)
    bc = min(tc, C)
    assert C % bc == 0
    return (R, bc), C // bc, lambda i: (0, i)


def _ew_call(body, name, ins, n_out, out_dtypes, tr, tc):
    R, C = ins[0].shape
    blk, steps, imap = _block_plan(R, C, tr, tc)
    spec = pl.BlockSpec(blk, imap)
    return pl.pallas_call(
        body, name=name, grid=(steps,),
        in_specs=[spec] * len(ins), out_specs=[spec] * n_out,
        out_shape=[SDS((R, C), dt) for dt in out_dtypes],
        compiler_params=_params(("parallel",)),
    )(*ins)


def _pair_sum_bf16(a, b, name):
    def body(a_ref, b_ref, o_ref):
        o_ref[...] = (a_ref[...] + b_ref[...]).astype(BF16)

    return _ew_call(body, name, [a, b], 1, [BF16], 256, LANES)[0]


def _sum_slots(slots, name, first=None, tr=256):
    n, R, C = slots.shape
    blk, steps, imap = _block_plan(R, C, tr, 2 * LANES)
    lead = [] if first is None else [first]

    def body(*refs):
        s_ref, o_ref = refs[-2:]
        acc = refs[0][...].astype(F32) if lead else s_ref[0].astype(F32)
        for k in range(0 if lead else 1, n):
            acc = acc + s_ref[k].astype(F32)
        o_ref[...] = acc

    return pl.pallas_call(
        body, name=name, grid=(steps,),
        in_specs=[pl.BlockSpec(blk, imap)] * len(lead) + [pl.BlockSpec((n,) + blk, lambda i: (0,) + imap(i))],
        out_specs=pl.BlockSpec(blk, imap),
        out_shape=SDS((R, C), F32),
        compiler_params=_params(("parallel",)),
    )(*lead, slots)


def _adamw(w, g, m, v, name):
    def body(w_ref, g_ref, m_ref, v_ref, d_ref, nm_ref, nv_ref):
        gg = g_ref[...]
        nm = ADAM_B1 * m_ref[...] + (1.0 - ADAM_B1) * gg
        nv = ADAM_B2 * v_ref[...] + (1.0 - ADAM_B2) * (gg * gg)
        m_hat = nm / (1.0 - ADAM_B1 ** ADAM_STEP)
        v_hat = nv / (1.0 - ADAM_B2 ** ADAM_STEP)
        d_ref[...] = -ADAM_LR * (m_hat / (jnp.sqrt(v_hat) + ADAM_EPS) + ADAM_WD * w_ref[...])
        nm_ref[...] = nm
        nv_ref[...] = nv

    return _ew_call(body, name, [w, g, m, v], 3, [F32, F32, F32], 128, 2 * LANES)


ANY = pl.BlockSpec(memory_space=pl.ANY)


def _place():
    return lax.axis_index("x"), lax.axis_index("y"), lax.axis_index("c")


def _gather_chips(shards):
    n = len(shards)
    per = 7

    def body(*refs):
        ins, outs = refs[:n], refs[n:2 * n]
        send_sems, recv_sems = refs[2 * n:]
        x, y, c = _place()
        mine = 2 * x + y
        me, sibling = (x, y, c), (x, y, 1 - c)
        chips = [(1 - x, y), (x, 1 - y), (1 - x, 1 - y)]

        def copy(a, k, chip_idx, half, to, src=None):
            dst = outs[a].at[chip_idx, half]
            return pltpu.make_async_remote_copy(
                src_ref=dst if src is None else src, dst_ref=dst,
                send_sem=send_sems.at[a * per + k], recv_sem=recv_sems.at[a * per + k],
                device_id=to, device_id_type=MESH)

        def own(a, to):
            return pltpu.make_async_remote_copy(
                src_ref=ins[a], dst_ref=outs[a].at[mine],
                send_sem=send_sems.at[a * per + 6], recv_sem=recv_sems.at[a * per + 6],
                device_id=to, device_id_type=MESH)

        first = [copy(a, k, mine, c, (*chip, c), src=ins[a].at[c]) for a in range(n) for k, chip in enumerate(chips)]
        first += [own(a, sibling) for a in range(n)]
        for cp in first:
            cp.start()
        passed = []
        for k, (cx, cy) in enumerate(chips):
            for a in range(n):
                copy(a, k, 2 * cx + cy, c, me).wait_recv()
                fwd = copy(a, 3 + k, 2 * cx + cy, c, sibling)
                fwd.start()
                passed.append(fwd)
        for k, (cx, cy) in enumerate(chips):
            for a in range(n):
                copy(a, 3 + k, 2 * cx + cy, 1 - c, me).wait_recv()
        for a in range(n):
            own(a, me).wait_recv()
        for cp in first + passed:
            cp.wait_send()

    return pl.pallas_call(
        body, name="gather_chips",
        in_specs=[ANY] * n, out_specs=[ANY] * n,
        out_shape=[SDS((4,) + s.shape, s.dtype) for s in shards],
        scratch_shapes=[pltpu.SemaphoreType.DMA((per * n,)), pltpu.SemaphoreType.DMA((per * n,))],
    )(*shards)


def _pair_swap(halves):
    n = len(halves)

    def body(*refs):
        ins, outs = refs[:n], refs[n:2 * n]
        send_sems, recv_sems = refs[2 * n:]
        x, y, c = _place()
        copies = [pltpu.make_async_remote_copy(
            src_ref=ins[a].at[1 - c], dst_ref=outs[a], send_sem=send_sems.at[a], recv_sem=recv_sems.at[a],
            device_id=(x, y, 1 - c), device_id_type=MESH) for a in range(n)]
        for cp in copies:
            cp.start()
        for cp in copies:
            cp.wait()

    return pl.pallas_call(
        body, name="pair_swap",
        in_specs=[ANY] * n, out_specs=[ANY] * n,
        out_shape=[SDS(h.shape[1:], h.dtype) for h in halves],
        scratch_shapes=[pltpu.SemaphoreType.DMA((n,)), pltpu.SemaphoreType.DMA((n,))],
    )(*halves)


def _pair_send(arrs):
    n = len(arrs)

    def body(*refs):
        ins, outs = refs[:n], refs[n:2 * n]
        send_sems, recv_sems = refs[2 * n:]
        x, y, c = _place()
        copies = [pltpu.make_async_remote_copy(
            src_ref=ins[a], dst_ref=outs[a], send_sem=send_sems.at[a], recv_sem=recv_sems.at[a],
            device_id=(x, y, 1 - c), device_id_type=MESH) for a in range(n)]
        for cp in copies:
            cp.start()
        for cp in copies:
            cp.wait()

    return pl.pallas_call(
        body, name="pair_send",
        in_specs=[ANY] * n, out_specs=[ANY] * n,
        out_shape=[SDS(h.shape, h.dtype) for h in arrs],
        scratch_shapes=[pltpu.SemaphoreType.DMA((n,)), pltpu.SemaphoreType.DMA((n,))],
    )(*arrs)


def _gather_all(buf):
    flips = [(fx, fy, fc) for fx in (0, 1) for fy in (0, 1) for fc in (0, 1)][1:]

    def body(in_ref, out_ref, send_sems, recv_sems, local_sem):
        x, y, c = _place()
        me = 4 * x + 2 * y + c
        local = pltpu.make_async_copy(in_ref, out_ref.at[me], local_sem)
        local.start()
        sends = []
        for k, (fx, fy, fc) in enumerate(flips):
            cp = pltpu.make_async_remote_copy(
                src_ref=in_ref, dst_ref=out_ref.at[me], send_sem=send_sems.at[k], recv_sem=recv_sems.at[k],
                device_id=(x ^ fx, y ^ fy, c ^ fc), device_id_type=MESH)
            cp.start()
            sends.append(cp)
        for k, (fx, fy, fc) in enumerate(flips):
            src = 4 * (x ^ fx) + 2 * (y ^ fy) + (c ^ fc)
            pltpu.make_async_remote_copy(
                src_ref=in_ref, dst_ref=out_ref.at[src], send_sem=send_sems.at[k], recv_sem=recv_sems.at[k],
                device_id=(x, y, c), device_id_type=MESH).wait_recv()
        for cp in sends:
            cp.wait_send()
        local.wait()

    return pl.pallas_call(
        body, name="gather_all",
        in_specs=[ANY], out_specs=ANY,
        out_shape=SDS((8,) + buf.shape, buf.dtype),
        scratch_shapes=[pltpu.SemaphoreType.DMA((7,)), pltpu.SemaphoreType.DMA((7,)), pltpu.SemaphoreType.DMA],
    )(buf)


def _tiles(S, FW):
    big = FW % 512 == 0
    return dict(
        fp=512 if big else LANES,
        tn=512 if big else LANES,
        tm_in=min(1024, S),
        t_attn=min(512, S),
        hb_fwd=4,
        hb_bwd=2,
        tm_prep=min(512, S),
        tm_mix=min(128, S),
        tc=min(256, S),
        tk=512 if big else LANES,
    )


def kernel(x, norm_g, w_in, b_forget, q_norm_g, k_norm_g, conv_w, conv_b, conv_ln_g, conv_ln_b, w_out, loss_target, m_norm_g, m_w_in, m_b_forget, m_q_norm_g, m_k_norm_g, m_conv_w, m_conv_b, m_conv_ln_g, m_conv_ln_b, m_w_out, v_norm_g, v_w_in, v_b_forget, v_q_norm_g, v_k_norm_g, v_conv_w, v_conv_b, v_conv_ln_g, v_conv_ln_b, v_w_out):
    B, S, D = x.shape
    H, dh = q_norm_g.shape[1:]
    FW = H * dh
    CW = conv_b.shape[-1]
    n_taps, cw_shard = conv_w.shape[1:]
    in_shard = w_in.shape[2]
    out_shard = w_out.shape[1]
    assert dh == HEAD_DIM and H % 2 == 0 and H <= LANES and FW == CW == D
    assert n_taps - 1 <= HALO and 4 * cw_shard == CW and 4 * out_shard == FW + CW
    assert 4 * in_shard == 4 * FW + 3 * CW + H
    T = B * S
    tl = _tiles(S, FW)
    fp = tl["fp"]
    xi, yi, ci = _place()

    w_t = jnp.transpose(w_in[0])
    conv_pad = jnp.pad(conv_w[0], ((0, HALO - n_taps), (0, 0)))
    g_in, g_out, g_cw = _gather_chips([
        w_t.astype(BF16).reshape(in_shard, 2, D // 2).transpose(1, 0, 2),
        w_out[0].astype(BF16).reshape(2, out_shard // 2, D),
        conv_pad.reshape(2, HALO // 2, cw_shard)])
    w_t_full = g_in.transpose(0, 2, 1, 3).reshape(4 * in_shard, D)
    w_out_full = g_out.reshape(FW + CW, D)
    conv_full = g_cw.reshape(4, HALO, cw_shard).transpose(1, 0, 2).reshape(HALO, CW)
    o_f = 3 * FW
    w_pack = jnp.concatenate([w_t_full[:o_f], w_t_full[o_f + H:],
                              jnp.pad(w_t_full[o_f:o_f + H], ((0, fp - H), (0, 0)))], axis=0)
    f_col = 4 * FW + 3 * CW

    x2 = x.reshape(T, D)
    tgt = loss_target.reshape(T, D)
    b_pad = jnp.pad(b_forget, ((0, 0), (0, LANES - H)))
    gq = q_norm_g.reshape(1, FW)
    gk = k_norm_g.reshape(1, FW)

    z, h = _fwd_in(x2, norm_g, w_pack, tl["tm_in"], tl["tn"])
    c = _gate_fwd(z, b_pad, B, S, H, f_col // LANES, tl["tc"])
    qa, ka, va = _attn_prep(z, c, gq, gk, B, S, H, tl["tm_prep"])
    oa = _attn_fwd(qa, ka, va, tl["t_attn"], tl["hb_fwd"])
    y, u2, a_nat, dout, dout_b, dy, loss_acc = _fwd_out(
        oa, z, x2, tgt, conv_full, conv_b, conv_ln_g, conv_ln_b, w_out_full, B, S, H, n_taps, tl["tm_mix"])
    loss = lax.psum(0.5 * loss_acc[0, 0] / D, ("x", "y", "c"))

    dzgf, dzgc, du2, doa, qb, sg_conv = _bwd_prep(dy, z, a_nat, oa, qa, u2, conv_ln_g, conv_ln_b, B, S, H, tl["tm_mix"])
    dzglu, dconv_w = _conv_bwd(du2, z, conv_full, B, S, n_taps, tl["tm_mix"])
    dqa, dka, dva = _attn_bwd(qb, ka, va, doa, tl["t_attn"], tl["hb_bwd"])
    dzq, dzk, dzv, dc8, dg_qk = _qk_bwd(dqa, dka, dva, z, gq, gk, B, S, H, tl["tm_prep"])
    dzf, db_f = _gate_bwd(dc8, z, b_pad, B, S, H, f_col // LANES, fp, tl["tc"])
    pieces = [dzq, dzk, dzv, dzgf, dzglu, dzgc, dzf]
    dw_pieces = [_matmul_tn(p, h, f"dw_in_{k}", 1024, 1024, 512) for k, p in enumerate(pieces)]
    dw_out = _matmul_tn(y, dout_b, "dw_out", 1024, 1024, 512)
    dw_t = jnp.concatenate(dw_pieces[:3] + [dw_pieces[6][:H]] + dw_pieces[3:6], axis=0)

    halves_in = dw_t.reshape(4 * in_shard, 2, D // 2).transpose(1, 0, 2)
    halves_out = dw_out.reshape(4, 2, out_shard // 2, D).transpose(1, 0, 2, 3).reshape(2, 2 * out_shard, D)
    got_in, got_out = _pair_swap([halves_in, halves_out])
    own_in = lax.dynamic_index_in_dim(halves_in, ci, 0, keepdims=False)
    own_out = lax.dynamic_index_in_dim(halves_out, ci, 0, keepdims=False)
    part_in = _pair_sum_bf16(own_in, got_in, "pair_sum_in").reshape(4, in_shard, D // 2)
    part_out = _pair_sum_bf16(own_out, got_out, "pair_sum_out").reshape(4, out_shard // 2, D)
    grad_x2, dg_norm, slots_in, slots_out = _dh_rms_bwd(
        pieces, w_pack, x2, norm_g, dout, tl["tm_in"], tl["tk"], [part_in, part_out])
    chip = 2 * xi + yi
    half_in = _sum_slots(slots_in, "chip_sum_in", lax.dynamic_index_in_dim(part_in, chip, 0, keepdims=False))
    half_out = _sum_slots(slots_out, "chip_sum_out", lax.dynamic_index_in_dim(part_out, chip, 0, keepdims=False))
    other_in, other_out = _pair_send([half_in, half_out])

    def both_halves(mine, other, axis):
        return jnp.where(ci == 0, jnp.concatenate([mine, other], axis=axis), jnp.concatenate([other, mine], axis=axis))

    grad_w_t = both_halves(half_in, other_in, 1)
    grad_w_out = both_halves(half_out, other_out, 0)

    small = jnp.concatenate([
        dg_norm[0:1], jnp.pad(db_f[0:1, :], ((0, 0), (0, D - LANES))), dg_qk[0:1], dg_qk[1:2],
        sg_conv[2:3], sg_conv[0:1], sg_conv[1:2], dconv_w], axis=0)
    n_small = small.shape[0]
    small_sum = _sum_slots(_gather_all(small), "small_sum", tr=n_small)
    grad_norm_g, grad_b_f = small_sum[0:1], small_sum[1:2, :H]
    grad_gq, grad_gk = small_sum[2:3].reshape(1, H, dh), small_sum[3:4].reshape(1, H, dh)
    grad_conv_b, grad_ln_g, grad_ln_b = small_sum[4:5], small_sum[5:6], small_sum[6:7]
    grad_conv_w = lax.dynamic_slice_in_dim(small_sum[7:7 + n_taps], chip * cw_shard, cw_shard, axis=1)

    d_t, nm_t, nv_t = _adamw(w_t, grad_w_t, jnp.transpose(m_w_in[0]), jnp.transpose(v_w_in[0]), "adamw_in")
    grad_w_in, d_in, nm_in, nv_in = (jnp.transpose(t)[None] for t in (grad_w_t, d_t, nm_t, nv_t))
    d_out, nm_out, nv_out = (t[None] for t in _adamw(w_out[0], grad_w_out, m_w_out[0], v_w_out[0], "adamw_out"))
    d_cw, nm_cw, nv_cw = (t[None] for t in _adamw(conv_w[0], grad_conv_w, m_conv_w[0], v_conv_w[0], "adamw_conv_w"))

    def rows(ws):
        return jnp.concatenate([jnp.pad(t.reshape(1, -1), ((0, 0), (0, D - t.size))) for t in ws], axis=0)

    small_w = [norm_g, b_forget, q_norm_g, k_norm_g, conv_b, conv_ln_g, conv_ln_b]
    small_m = [m_norm_g, m_b_forget, m_q_norm_g, m_k_norm_g, m_conv_b, m_conv_ln_g, m_conv_ln_b]
    small_v = [v_norm_g, v_b_forget, v_q_norm_g, v_k_norm_g, v_conv_b, v_conv_ln_g, v_conv_ln_b]
    d_s, nm_s, nv_s = _adamw(rows(small_w), small_sum[0:7], rows(small_m), rows(small_v), "adamw_small")

    def unpack(t):
        return [t[k:k + 1, :w.size].reshape(w.shape) for k, w in enumerate(small_w)]

    def order(s, in_, cw, out_):
        ng, bf, qg, kg, cb, lg, lb = s
        return [ng, in_, bf, qg, kg, cw, cb, lg, lb, out_]

    grads = [grad_norm_g, grad_w_in, grad_b_f, grad_gq, grad_gk, grad_conv_w[None],
             grad_conv_b, grad_ln_g, grad_ln_b, grad_w_out[None]]
    return (loss, grad_x2.reshape(B, S, D), *grads,
            *order(unpack(d_s), d_in, d_cw, d_out),
            *order(unpack(nm_s), nm_in, nm_cw, nm_out),
            *order(unpack(nv_s), nv_in, nv_cw, nv_out))
```

```python
import functools

import jax
import jax.numpy as jnp
from jax import lax
from jax.experimental import pallas as pl
from jax.experimental.pallas import tpu as pltpu

F32 = jnp.float32
BF16 = jnp.bfloat16
SDS = jax.ShapeDtypeStruct
MESH = pl.DeviceIdType.MESH

EPS = 1e-6
NEG_INF = -1e30
LANES = 128
SUBLANES = 8
HEAD_DIM = 64
HALO = 32
VMEM_LIMIT = 56 * 1024 * 1024

L_ROWSUM = 64
L_KDECAY = 67
L_LSE = 70
L_D = 65
L_QNORM = 73
L_KNORM = 74
NORM_SLACK = 1.02
SHIFT_MAX = 40.0

ADAM_LR = 0.001
ADAM_B1 = 0.9
ADAM_B2 = 0.999
ADAM_EPS = 1e-08
ADAM_WD = 0.01
ADAM_STEP = 10


def _params(sem, vmem=VMEM_LIMIT):
    return pltpu.CompilerParams(dimension_semantics=sem, vmem_limit_bytes=vmem)


def _sigmoid(x):
    return 1.0 / (1.0 + jnp.exp(-x))


def _split3(x):
    hi = x.astype(BF16).astype(F32)
    r = x - hi
    mid = r.astype(BF16).astype(F32)
    lo = (r - mid).astype(BF16).astype(F32)
    return hi, mid, lo


def _dot(a, b):
    return jnp.dot(a, b, preferred_element_type=F32)


def _dot_nt(a, b):
    return lax.dot_general(a, b, (((1,), (1,)), ((), ())), preferred_element_type=F32)


def _dot_tn(a, b):
    return lax.dot_general(a, b, (((0,), (0,)), ((), ())), preferred_element_type=F32)


def _lane(shape):
    return lax.broadcasted_iota(jnp.int32, shape, 1)


def _lane_col(x, lane, idx):
    return jnp.sum(jnp.where(lane == idx, x, 0.0), axis=-1, keepdims=True)


def _put3(base, lane, start, pieces):
    out = base
    for k, p in enumerate(pieces):
        out = jnp.where(lane == start + k, p, out)
    return out


def _half_stats(t, lo):
    del lo
    hi = t.astype(BF16)
    mid = (t - hi.astype(F32)).astype(BF16)
    row = lax.broadcasted_iota(jnp.int32, (2 * LANES, LANES), 0)
    col = lax.broadcasted_iota(jnp.int32, (2 * LANES, LANES), 1)
    same_half = (jnp.bitwise_and(row, LANES - 1) < HEAD_DIM) == (col < HEAD_DIM)
    return _dot(jnp.concatenate([hi, mid], axis=1), jnp.where(same_half, 1.0, 0.0).astype(BF16))


def _fwd_in(x2, g, w_t, tm, tn):
    T, D = x2.shape
    N = w_t.shape[0]

    def body(x_ref, g_ref, w_ref, z_ref, h_ref):
        @pl.when(pl.program_id(1) == 0)
        def _():
            x = x_ref[...]
            r = lax.rsqrt(jnp.mean(x * x, axis=-1, keepdims=True) + EPS)
            h_ref[...] = (x * r * g_ref[...]).astype(BF16)

        z_ref[...] = _dot_nt(h_ref[...], w_ref[...])

    return pl.pallas_call(
        body, name="fwd_in", grid=(T // tm, N // tn),
        in_specs=[pl.BlockSpec((tm, D), lambda i, j: (i, 0)),
                  pl.BlockSpec((1, D), lambda i, j: (0, 0)),
                  pl.BlockSpec((tn, D), lambda i, j: (j, 0))],
        out_specs=[pl.BlockSpec((tm, tn), lambda i, j: (i, j)),
                   pl.BlockSpec((tm, D), lambda i, j: (i, 0))],
        out_shape=[SDS((T, N), F32), SDS((T, D), BF16)],
        compiler_params=_params(("parallel", "arbitrary")),
    )(x2, g, w_t)


def _tri_cumsum(x, reverse):
    t = x.shape[0]
    row = lax.broadcasted_iota(jnp.int32, (t, t), 0)
    col = lax.broadcasted_iota(jnp.int32, (t, t), 1)
    tri = (row <= col) if reverse else (row >= col)
    tri = jnp.where(tri, 1.0, 0.0).astype(BF16)
    hi, mid, lo = _split3(x)
    return _dot(tri, hi.astype(BF16)) + _dot(tri, mid.astype(BF16)) + _dot(tri, lo.astype(BF16))


def _gate_fwd(z, b_pad, B, S, H, col_blk, tc):
    T = B * S
    nsb = S // tc

    def body(zf_ref, b_ref, c_ref, carry):
        @pl.when(pl.program_id(1) == 0)
        def _():
            carry[...] = jnp.zeros_like(carry)

        x = zf_ref[...] + b_ref[...]
        lf = jnp.minimum(x, 0.0) - jnp.log(1.0 + jnp.exp(-jnp.abs(x)))
        lf = jnp.where(_lane(lf.shape) < H, lf, 0.0)
        c_ref[...] = _tri_cumsum(lf, False) + carry[...]
        carry[...] = carry[...] + jnp.sum(lf, axis=0, keepdims=True)

    return pl.pallas_call(
        body, name="gate_fwd", grid=(B, nsb),
        in_specs=[pl.BlockSpec((tc, LANES), lambda b, s: (b * nsb + s, col_blk)),
                  pl.BlockSpec((1, LANES), lambda b, s: (0, 0))],
        out_specs=pl.BlockSpec((tc, LANES), lambda b, s: (b * nsb + s, 0)),
        out_shape=SDS((T, LANES), F32),
        scratch_shapes=[pltpu.VMEM((1, LANES), F32)],
        compiler_params=_params(("parallel", "arbitrary")),
    )(z, b_pad)


def _qk_normalize(x, g, lo):
    r = lax.rsqrt(_half_stats(x * x, lo) * (1.0 / HEAD_DIM) + EPS)
    return x * r * g


def _head_norms(x, lo):
    own = jnp.sqrt(_half_stats(x * x, lo)) * NORM_SLACK
    return [pltpu.roll(own, HEAD_DIM, 1), own]


def _attn_prep(z, c, gq, gk, B, S, H, tm):
    T = B * S
    FW = H * HEAD_DIM
    nsb = S // tm
    nfb = FW // LANES
    scale = HEAD_DIM ** -0.5

    def body(zq_ref, zk_ref, zv_ref, c_ref, gq_ref, gk_ref, qa_ref, ka_ref, va_ref):
        p = pl.program_id(1)
        lane = _lane((tm, LANES))
        lo = lane < HEAD_DIM
        qn = _qk_normalize(zq_ref[...], gq_ref[...], lo) * scale
        kn = _qk_normalize(zk_ref[...], gk_ref[...], lo)
        v = zv_ref[...]
        cc = c_ref[...]
        ones_q = ((lane >= L_KDECAY) & (lane < L_KDECAY + 3)).astype(F32)
        ones_k = (((lane >= L_ROWSUM) & (lane < L_ROWSUM + 3)) | ((lane >= L_LSE) & (lane < L_LSE + 3))).astype(F32)
        ones_v = ((lane >= L_ROWSUM) & (lane < L_D + 3)).astype(F32)
        q_norms, k_norms = _head_norms(qn, lo), _head_norms(kn, lo)
        for e in range(2):
            if e == 0:
                qe, ke, ve = qn, kn, v
            else:
                qe, ke, ve = (pltpu.roll(t, HEAD_DIM, 1) for t in (qn, kn, v))
            ch = _lane_col(cc, lane, 2 * p + e)
            pieces = _split3(ch)
            qa = jnp.where(lo, qe, _put3(ones_q, lane, L_ROWSUM, pieces))
            qa = jnp.where(lane == L_QNORM, q_norms[e], qa)
            ka = jnp.where(lo, ke, _put3(ones_k, lane, L_KDECAY, [-t for t in pieces]))
            ka = jnp.where(lane == L_KNORM, k_norms[e], ka)
            va = jnp.where(lo, ve, ones_v)
            qa_ref[0, e] = qa.astype(BF16)
            ka_ref[0, e] = ka.astype(BF16)
            va_ref[0, e] = va.astype(BF16)

    zspec = lambda off: pl.BlockSpec((tm, LANES), lambda i, p: (i, off + p))
    gspec = pl.BlockSpec((1, LANES), lambda i, p: (0, p))
    ospec = pl.BlockSpec((1, 2, tm, LANES), lambda i, p: (i // nsb, p, i % nsb, 0))
    oshape = SDS((B, H, S, LANES), BF16)
    return pl.pallas_call(
        body, name="attn_prep", grid=(T // tm, H // 2),
        in_specs=[zspec(0), zspec(nfb), zspec(2 * nfb),
                  pl.BlockSpec((tm, LANES), lambda i, p: (i, 0)), gspec, gspec],
        out_specs=[ospec, ospec, ospec],
        out_shape=[oshape, oshape, oshape],
        compiler_params=_params(("parallel", "arbitrary")),
    )(z, z, z, c, gq, gk)


def _attn_fwd(qa, ka, va, t, hb):
    B, H, S, _ = qa.shape
    nq = S // t

    def body(q_ref, k_ref, v_ref, o_ref, m_ref, acc_ref, kmax_ref, qs_ref):
        i = pl.program_id(2)
        lane = _lane((t, LANES))

        @pl.when(i == 0)
        def _():
            for e in range(hb):
                norms = jnp.where(_lane((S, LANES)) == L_KNORM, k_ref[0, e].astype(F32), 0.0)
                kmax_ref[e] = jnp.full((1, LANES), jnp.max(norms), F32)

        shifts = [_lane_col(q_ref[0, e].astype(F32), lane, L_QNORM) * kmax_ref[e][:, 0:1] for e in range(hb)]
        worst = shifts[0]
        for e in range(1, hb):
            worst = jnp.maximum(worst, shifts[e])
        bounded = jnp.max(worst) <= SHIFT_MAX
        acc_ref[...] = jnp.zeros_like(acc_ref)

        def tiles(step):
            def loop_body(j, carry):
                step(j, False)
                return carry

            lax.fori_loop(0, i, loop_body, 0)
            step(i, True)

        def keep_mask(n=t):
            return lax.broadcasted_iota(jnp.int32, (n, n), 0) >= lax.broadcasted_iota(jnp.int32, (n, n), 1)

        def finish(e, shift):
            acc = acc_ref[e]
            l = _lane_col(acc, lane, L_ROWSUM)
            o_ref[0, e] = jnp.where(lane < HEAD_DIM, acc / l, shift + jnp.log(l))

        @pl.when(bounded)
        def _():
            for e in range(hb):
                qs_ref[e] = _put3(q_ref[0, e].astype(F32), lane, L_LSE, _split3(-shifts[e])).astype(BF16)

            def pair(e, q_rows, k_start, n, masked):
                k_rows = pl.ds(pl.multiple_of(k_start, n), n)
                p = jnp.exp(_dot_nt(qs_ref[e, q_rows, :], k_ref[0, e, k_rows, :]))
                if masked:
                    p = jnp.where(keep_mask(n), p, 0.0)
                acc_ref[e, q_rows, :] = acc_ref[e, q_rows, :] + _dot(p.astype(BF16), v_ref[0, e, k_rows, :])

            def step(j, masked):
                for e in range(hb):
                    if masked:
                        h = t // 2
                        pair(e, slice(0, h), j * t, h, True)
                        pair(e, slice(h, t), j * t, h, False)
                        pair(e, slice(h, t), j * t + h, h, True)
                    else:
                        pair(e, slice(0, t), j * t, t, False)

            tiles(step)
            for e in range(hb):
                finish(e, shifts[e])

        @pl.when(jnp.logical_not(bounded))
        def _():
            m_ref[...] = jnp.full_like(m_ref, NEG_INF)

            def step(j, masked):
                rows = pl.ds(pl.multiple_of(j * t, t), t)
                for e in range(hb):
                    s = _dot_nt(q_ref[0, e], k_ref[0, e, rows, :])
                    if masked:
                        s = jnp.where(keep_mask(), s, NEG_INF)
                    m_prev = m_ref[e]
                    m_new = jnp.maximum(m_prev, jnp.max(s, axis=-1, keepdims=True))
                    alpha = jnp.exp(m_prev - m_new)
                    p = jnp.exp(s - m_new).astype(BF16)
                    acc_ref[e] = alpha * acc_ref[e] + _dot(p, v_ref[0, e, rows, :])
                    m_ref[e] = m_new

            tiles(step)
            for e in range(hb):
                finish(e, m_ref[e])

    return pl.pallas_call(
        body, name="attn_fwd", grid=(B, H // hb, nq),
        in_specs=[pl.BlockSpec((1, hb, t, LANES), lambda b, h, i: (b, h, i, 0)),
                  pl.BlockSpec((1, hb, S, LANES), lambda b, h, i: (b, h, 0, 0)),
                  pl.BlockSpec((1, hb, S, LANES), lambda b, h, i: (b, h, 0, 0))],
        out_specs=pl.BlockSpec((1, hb, t, LANES), lambda b, h, i: (b, h, i, 0)),
        out_shape=SDS((B, H, S, LANES), F32),
        scratch_shapes=[pltpu.VMEM((hb, t, 1), F32), pltpu.VMEM((hb, t, LANES), F32),
                        pltpu.VMEM((hb, 1, LANES), F32), pltpu.VMEM((hb, t, LANES), BF16)],
        compiler_params=_params(("parallel", "parallel", "arbitrary")),
    )(qa, ka, va)


def _fill_shifts(ext_ref, sh_ref):
    rows = sh_ref.shape[1]
    for b in range(1, SUBLANES):
        sh_ref[b - 1] = ext_ref[pl.ds(b, rows), :]


def _tap_window(ext_ref, sh_ref, off, tm, cols):
    b = off % SUBLANES
    if b == 0:
        return ext_ref[pl.ds(off, tm), cols]
    return sh_ref[b - 1, pl.ds(off - b, tm), cols]


def _conv_taps(w_ref, ext_ref, sh_ref, out_ref, n_taps, tm, offset_of, bias_ref=None):
    for cc in range(out_ref.shape[1] // LANES):
        cols = slice(cc * LANES, (cc + 1) * LANES)
        acc = None
        for j in range(n_taps):
            term = w_ref[j:j + 1, cols] * _tap_window(ext_ref, sh_ref, offset_of(j), tm, cols)
            acc = term if acc is None else acc + term
        out_ref[:, cols] = acc if bias_ref is None else acc + bias_ref[:, cols]


def _layernorm_stats(u2):
    mu = jnp.mean(u2, axis=-1, keepdims=True)
    xc = u2 - mu
    rstd = lax.rsqrt(jnp.mean(xc * xc, axis=-1, keepdims=True) + EPS)
    return xc * rstd, rstd


def _fwd_out(oa, z, x2, tgt, conv_w, conv_b, ln_g, ln_b, w_out, B, S, H, n_taps, tm):
    T, D = x2.shape
    FW = H * HEAD_DIM
    CW = conv_w.shape[1]
    nsb = S // tm
    hb = tm // HALO

    def body(oa_ref, gf_ref, ga_ref, gb_ref, gc_ref, ha_ref, hb_ref, x_ref, t_ref, w_ref, cb_ref, lg_ref,
             lb_ref, wo_ref, y_ref, u2_ref, a_ref, do_ref, dob_ref, dy_ref, loss_ref, ext_ref, sh_ref):
        first_step = (pl.program_id(0) == 0) & (pl.program_id(1) == 0)

        @pl.when(first_step)
        def _():
            loss_ref[...] = jnp.zeros_like(loss_ref)

        u1 = ga_ref[...] * _sigmoid(gb_ref[...])
        halo = ha_ref[...] * _sigmoid(hb_ref[...])
        ext_ref[0:HALO, :] = jnp.where(pl.program_id(1) > 0, halo, 0.0)
        ext_ref[HALO:, :] = u1
        _fill_shifts(ext_ref, sh_ref)
        _conv_taps(w_ref, ext_ref, sh_ref, u2_ref, n_taps, tm, lambda j: HALO - (n_taps - 1) + j, cb_ref)
        uh, _ = _layernorm_stats(u2_ref[...])
        u3 = uh * lg_ref[...] + lb_ref[...]
        gc = gc_ref[...]
        yu = u3 * _sigmoid(u3) * (gc * _sigmoid(gc))
        y_ref[:, FW:] = yu.astype(BF16)

        lane = _lane((tm, LANES))
        lo = lane < HEAD_DIM
        for p in range(H // 2):
            a_ref[:, p * LANES:(p + 1) * LANES] = jnp.where(
                lo, oa_ref[0, 2 * p], pltpu.roll(oa_ref[0, 2 * p + 1], HEAD_DIM, 1))
        gf = gf_ref[...]
        y_ref[:, :FW] = (a_ref[...] * (gf * _sigmoid(gf))).astype(BF16)

        out = x_ref[...] + _dot(y_ref[...], wo_ref[...])
        diff = out - t_ref[...]
        loss_ref[...] = loss_ref[...] + jnp.sum(diff * diff)
        dout = diff * (1.0 / D)
        do_ref[...] = dout
        dob = dout.astype(BF16)
        dob_ref[...] = dob
        dy_ref[...] = _dot_nt(dob, wo_ref[...])

    row = lambda b, s: b * nsb + s
    zspec = lambda cb: pl.BlockSpec((tm, FW), lambda b, s: (row(b, s), cb))
    hspec = lambda cb: pl.BlockSpec((HALO, CW), lambda b, s: (jnp.maximum(row(b, s) * hb - 1, 0), cb))
    vspec = pl.BlockSpec((1, CW), lambda b, s: (0, 0))
    tspec = lambda w: pl.BlockSpec((tm, w), lambda b, s: (row(b, s), 0))
    return pl.pallas_call(
        body, name="fwd_out", grid=(B, nsb),
        in_specs=[pl.BlockSpec((1, H, tm, LANES), lambda b, s: (b, 0, s, 0)),
                  zspec(3), zspec(4), zspec(5), zspec(6), hspec(4), hspec(5),
                  tspec(D), tspec(D),
                  pl.BlockSpec((HALO, CW), lambda b, s: (0, 0)), vspec, vspec, vspec,
                  pl.BlockSpec((FW + CW, D), lambda b, s: (0, 0))],
        out_specs=[tspec(FW + CW), tspec(CW), tspec(FW), tspec(D), tspec(D), tspec(FW + CW),
                   pl.BlockSpec((8, LANES), lambda b, s: (0, 0))],
        out_shape=[SDS((T, FW + CW), BF16), SDS((T, CW), F32), SDS((T, FW), F32), SDS((T, D), F32),
                   SDS((T, D), BF16), SDS((T, FW + CW), F32), SDS((8, LANES), F32)],
        scratch_shapes=[pltpu.VMEM((tm + HALO, CW), F32),
                        pltpu.VMEM((SUBLANES - 1, tm + HALO - SUBLANES, CW), F32)],
        compiler_params=_params(("arbitrary", "arbitrary")),
    )(oa, z, z, z, z, z, z, x2, tgt, conv_w, conv_b, ln_g, ln_b, w_out)


def _bwd_prep(dy, z, a_nat, oa, qa, u2, ln_g, ln_b, B, S, H, tm):
    T = B * S
    FW = H * HEAD_DIM
    CW = u2.shape[1]
    nsb = S // tm

    def body(dya_ref, dyu_ref, gf_ref, gc_ref, a_ref, oa_ref, qa_ref, u2_ref, lg_ref, lb_ref,
             dzgf_ref, dzgc_ref, du2_ref, doa_ref, qb_ref, sg_ref):
        first_step = (pl.program_id(0) == 0) & (pl.program_id(1) == 0)

        @pl.when(first_step)
        def _():
            sg_ref[...] = jnp.zeros_like(sg_ref)

        gf = gf_ref[...]
        sg = _sigmoid(gf)
        a = a_ref[...]
        dya = dya_ref[...]
        da = dya * (gf * sg)
        dzgf_ref[...] = (dya * a * (sg * (1.0 + gf * (1.0 - sg)))).astype(BF16)
        dd = da * a
        lane = _lane((tm, LANES))
        lo = lane < HEAD_DIM
        for p in range(H // 2):
            cols = slice(p * LANES, (p + 1) * LANES)
            da_p = da[:, cols]
            dd_p = dd[:, cols]
            d_heads = (jnp.sum(jnp.where(lo, dd_p, 0.0), axis=-1, keepdims=True),
                       jnp.sum(jnp.where(lo, 0.0, dd_p), axis=-1, keepdims=True))
            for e in range(2):
                da_e = da_p if e == 0 else pltpu.roll(da_p, HEAD_DIM, 1)
                d_e = d_heads[e]
                aug = _put3(jnp.zeros((tm, LANES), F32), lane, L_D, _split3(-d_e))
                doa_ref[0, 2 * p + e] = jnp.where(lo, da_e, aug).astype(BF16)
                lse = _lane_col(oa_ref[0, 2 * p + e], lane, L_ROWSUM)
                qb = _put3(qa_ref[0, 2 * p + e].astype(F32), lane, L_LSE, _split3(-lse))
                qb_ref[0, 2 * p + e] = qb.astype(BF16)

        gc = gc_ref[...]
        sc = _sigmoid(gc)
        dyu = dyu_ref[...]
        uh, rstd = _layernorm_stats(u2_ref[...])
        u3 = uh * lg_ref[...] + lb_ref[...]
        s3 = _sigmoid(u3)
        dzgc_ref[...] = (dyu * (u3 * s3) * (sc * (1.0 + gc * (1.0 - sc)))).astype(BF16)
        du3 = dyu * (gc * sc) * (s3 * (1.0 + u3 * (1.0 - s3)))
        sg_ref[0:1, :] = sg_ref[0:1, :] + jnp.sum(du3 * uh, axis=0, keepdims=True)
        sg_ref[1:2, :] = sg_ref[1:2, :] + jnp.sum(du3, axis=0, keepdims=True)
        duh = du3 * lg_ref[...]
        du2 = rstd * (duh - jnp.mean(duh, axis=-1, keepdims=True)
                      - uh * jnp.mean(duh * uh, axis=-1, keepdims=True))
        sg_ref[2:3, :] = sg_ref[2:3, :] + jnp.sum(du2, axis=0, keepdims=True)
        du2_ref[...] = du2

    row = lambda b, s: b * nsb + s
    tspec = lambda w, cb=0: pl.BlockSpec((tm, w), lambda b, s: (row(b, s), cb))
    hspec = pl.BlockSpec((1, H, tm, LANES), lambda b, s: (b, 0, s, 0))
    vspec = pl.BlockSpec((1, CW), lambda b, s: (0, 0))
    return pl.pallas_call(
        body, name="bwd_prep", grid=(B, nsb),
        in_specs=[tspec(FW, 0), tspec(CW, 1), tspec(FW, 3), tspec(CW, 6), tspec(FW), hspec, hspec,
                  tspec(CW), vspec, vspec],
        out_specs=[tspec(FW), tspec(CW), tspec(CW), hspec, hspec,
                   pl.BlockSpec((8, CW), lambda b, s: (0, 0))],
        out_shape=[SDS((T, FW), BF16), SDS((T, CW), BF16), SDS((T, CW), F32),
                   SDS((B, H, S, LANES), BF16), SDS((B, H, S, LANES), BF16), SDS((8, CW), F32)],
        compiler_params=_params(("arbitrary", "arbitrary")),
    )(dy, dy, z, z, a_nat, oa, qa, u2, ln_g, ln_b)


def _conv_bwd(du2, z, conv_w, B, S, n_taps, tm):
    T, CW = du2.shape
    nsb = S // tm
    hb = tm // HALO

    def body(d_ref, dh_ref, ga_ref, gb_ref, ha_ref, hb_ref, w_ref, dz_ref, dw_ref,
             extu_ref, extd_ref, shu_ref, shd_ref, du1_ref, dwacc_ref):
        s = pl.program_id(1)
        first_step = (pl.program_id(0) == 0) & (s == 0)
        last_step = (pl.program_id(0) == B - 1) & (s == nsb - 1)

        @pl.when(first_step)
        def _():
            dwacc_ref[...] = jnp.zeros_like(dwacc_ref)

        ga = ga_ref[...]
        sb = _sigmoid(gb_ref[...])
        halo = ha_ref[...] * _sigmoid(hb_ref[...])
        extu_ref[0:HALO, :] = jnp.where(s > 0, halo, 0.0)
        extu_ref[HALO:, :] = ga * sb
        extd_ref[0:tm, :] = d_ref[...]
        extd_ref[tm:, :] = jnp.where(s < nsb - 1, dh_ref[...], 0.0)
        _fill_shifts(extu_ref, shu_ref)
        _fill_shifts(extd_ref, shd_ref)
        _conv_taps(w_ref, extd_ref, shd_ref, du1_ref, n_taps, tm, lambda j: n_taps - 1 - j)
        for cc in range(CW // LANES):
            cols = slice(cc * LANES, (cc + 1) * LANES)
            parts = [None] * n_taps
            for r in range(tm // SUBLANES):
                dv = d_ref[r * SUBLANES:(r + 1) * SUBLANES, cols]
                for j in range(n_taps):
                    off = HALO - (n_taps - 1) + j + r * SUBLANES
                    term = dv * _tap_window(extu_ref, shu_ref, off, SUBLANES, cols)
                    parts[j] = term if parts[j] is None else parts[j] + term
            for j in range(n_taps):
                rows = slice(j * SUBLANES, (j + 1) * SUBLANES)
                dwacc_ref[rows, cols] = dwacc_ref[rows, cols] + parts[j]
        du1 = du1_ref[...]
        dz_ref[:, :CW] = (du1 * sb).astype(BF16)
        dz_ref[:, CW:] = (du1 * ga * (sb * (1.0 - sb))).astype(BF16)

        @pl.when(last_step)
        def _():
            dw_ref[...] = jnp.zeros_like(dw_ref)
            for j in range(n_taps):
                dw_ref[j:j + 1, :] = jnp.sum(dwacc_ref[j * SUBLANES:(j + 1) * SUBLANES, :], axis=0, keepdims=True)

    row = lambda b, s: b * nsb + s
    last_halo = T // HALO - 1
    return pl.pallas_call(
        body, name="conv_bwd", grid=(B, nsb),
        in_specs=[pl.BlockSpec((tm, CW), lambda b, s: (row(b, s), 0)),
                  pl.BlockSpec((HALO, CW), lambda b, s: (jnp.minimum((row(b, s) + 1) * hb, last_halo), 0)),
                  pl.BlockSpec((tm, CW), lambda b, s: (row(b, s), 4)),
                  pl.BlockSpec((tm, CW), lambda b, s: (row(b, s), 5)),
                  pl.BlockSpec((HALO, CW), lambda b, s: (jnp.maximum(row(b, s) * hb - 1, 0), 4)),
                  pl.BlockSpec((HALO, CW), lambda b, s: (jnp.maximum(row(b, s) * hb - 1, 0), 5)),
                  pl.BlockSpec((HALO, CW), lambda b, s: (0, 0))],
        out_specs=[pl.BlockSpec((tm, 2 * CW), lambda b, s: (row(b, s), 0)),
                   pl.BlockSpec((HALO, CW), lambda b, s: (0, 0))],
        out_shape=[SDS((T, 2 * CW), BF16), SDS((HALO, CW), F32)],
        scratch_shapes=[pltpu.VMEM((tm + HALO, CW), F32), pltpu.VMEM((tm + HALO, CW), F32),
                        pltpu.VMEM((SUBLANES - 1, tm + HALO - SUBLANES, CW), F32),
                        pltpu.VMEM((SUBLANES - 1, tm + HALO - SUBLANES, CW), F32),
                        pltpu.VMEM((tm, CW), F32), pltpu.VMEM((HALO * SUBLANES, CW), F32)],
        compiler_params=_params(("arbitrary", "arbitrary")),
    )(du2, du2, z, z, z, z, conv_w)


def _attn_bwd(qb, ka, va, doa, t, hb):
    B, H, S, _ = qb.shape
    nk = S // t

    def body(q_ref, k_ref, v_ref, do_ref, dq_ref, dk_ref, dv_ref):
        j = pl.program_id(2)

        @pl.when(j == 0)
        def _():
            dq_ref[...] = jnp.zeros_like(dq_ref)

        dk_ref[...] = jnp.zeros_like(dk_ref)
        dv_ref[...] = jnp.zeros_like(dv_ref)

        def pair(e, q_start, k_rows, n, masked):
            q_rows = pl.ds(pl.multiple_of(q_start, n), n)
            k = k_ref[0, e, k_rows, :]
            q = q_ref[0, e, q_rows, :]
            do = do_ref[0, e, q_rows, :]
            p = jnp.exp(_dot_nt(q, k))
            if masked:
                keep = lax.broadcasted_iota(jnp.int32, (n, n), 0) >= lax.broadcasted_iota(jnp.int32, (n, n), 1)
                p = jnp.where(keep, p, 0.0)
            ds = (p * _dot_nt(do, v_ref[0, e, k_rows, :])).astype(BF16)
            dv_ref[0, e, k_rows, :] = dv_ref[0, e, k_rows, :] + _dot_tn(p.astype(BF16), do)
            dk_ref[0, e, k_rows, :] = dk_ref[0, e, k_rows, :] + _dot_tn(ds, q)
            dq_ref[0, e, q_rows, :] = dq_ref[0, e, q_rows, :] + _dot(ds, k)

        def step(i, masked):
            for e in range(hb):
                pair(e, i * t, slice(0, t), t, masked)

        step(j, True)

        def loop_body(i, carry):
            step(i, False)
            return carry

        lax.fori_loop(j + 1, nk, loop_body, 0)

    full = pl.BlockSpec((1, hb, S, LANES), lambda b, h, j: (b, h, 0, 0))
    blk = pl.BlockSpec((1, hb, t, LANES), lambda b, h, j: (b, h, j, 0))
    oshape = SDS((B, H, S, LANES), F32)
    return pl.pallas_call(
        body, name="attn_bwd", grid=(B, H // hb, nk),
        in_specs=[full, blk, blk, full],
        out_specs=[full, blk, blk],
        out_shape=[oshape, oshape, oshape],
        compiler_params=_params(("parallel", "parallel", "arbitrary")),
    )(qb, ka, va, doa)


def _qk_bwd(dqa, dka, dva, z, gq, gk, B, S, H, tm):
    T = B * S
    FW = H * HEAD_DIM
    nsb = S // tm
    nfb = FW // LANES
    scale = HEAD_DIM ** -0.5

    def body(dq_ref, dk_ref, dv_ref, zq_ref, zk_ref, gq_ref, gk_ref, dzq_ref, dzk_ref, dzv_ref, dc_ref, dg_ref):
        p = pl.program_id(0)

        @pl.when(pl.program_id(1) == 0)
        def _():
            dg_ref[...] = jnp.zeros_like(dg_ref)

        lane = _lane((tm, LANES))
        lo = lane < HEAD_DIM

        def natural(ref):
            return jnp.where(lo, ref[0, 0], pltpu.roll(ref[0, 1], HEAD_DIM, 1))

        def norm_bwd(dn, x, g, row, out_ref):
            r = lax.rsqrt(_half_stats(x * x, lo) * (1.0 / HEAD_DIM) + EPS)
            xh = x * r
            dg_ref[row:row + 1, :] = dg_ref[row:row + 1, :] + jnp.sum(dn * xh, axis=0, keepdims=True)
            dxh = dn * g
            mm = _half_stats(dxh * xh, lo) * (1.0 / HEAD_DIM)
            out_ref[...] = (r * (dxh - xh * mm)).astype(BF16)

        norm_bwd(natural(dq_ref) * scale, zq_ref[...], gq_ref[...], 0, dzq_ref)
        norm_bwd(natural(dk_ref), zk_ref[...], gk_ref[...], 1, dzk_ref)
        dzv_ref[...] = natural(dv_ref).astype(BF16)

        dc = jnp.zeros((tm, LANES), F32)
        for e in range(2):
            val = _lane_col(dq_ref[0, e], lane, L_ROWSUM) - _lane_col(dk_ref[0, e], lane, L_KDECAY)
            dc = jnp.where(lane == 2 * p + e, val, dc)
        dc_ref[0] = dc

    hspec = pl.BlockSpec((1, 2, tm, LANES), lambda p, i: (i // nsb, p, i % nsb, 0))
    zspec = lambda off: pl.BlockSpec((tm, LANES), lambda p, i: (i, off + p))
    gspec = pl.BlockSpec((1, LANES), lambda p, i: (0, p))
    ospec = pl.BlockSpec((tm, LANES), lambda p, i: (i, p))
    return pl.pallas_call(
        body, name="qk_bwd", grid=(H // 2, T // tm),
        in_specs=[hspec, hspec, hspec, zspec(0), zspec(nfb), gspec, gspec],
        out_specs=[ospec, ospec, ospec,
                   pl.BlockSpec((1, tm, LANES), lambda p, i: (p, i, 0)),
                   pl.BlockSpec((8, LANES), lambda p, i: (0, p))],
        out_shape=[SDS((T, FW), BF16), SDS((T, FW), BF16), SDS((T, FW), BF16),
                   SDS((H // 2, T, LANES), F32), SDS((8, FW), F32)],
        compiler_params=_params(("parallel", "arbitrary")),
    )(dqa, dka, dva, z, z, gq, gk)


def _gate_bwd(dc8, z, b_pad, B, S, H, col_blk, fp, tc):
    T = B * S
    nsb = S // tc
    npair = dc8.shape[0]

    def body(dc_ref, zf_ref, b_ref, dz_ref, db_ref, carry):
        first_step = (pl.program_id(0) == 0) & (pl.program_id(1) == 0)

        @pl.when(first_step)
        def _():
            db_ref[...] = jnp.zeros_like(db_ref)

        @pl.when(pl.program_id(1) == 0)
        def _():
            carry[...] = jnp.zeros_like(carry)

        dc = dc_ref[0]
        for k in range(1, npair):
            dc = dc + dc_ref[k]
        dlf = _tri_cumsum(dc, True) + carry[...]
        carry[...] = carry[...] + jnp.sum(dc, axis=0, keepdims=True)
        x = zf_ref[...] + b_ref[...]
        dlogit = dlf * _sigmoid(-x)
        db_ref[0:1, :] = db_ref[0:1, :] + jnp.sum(dlogit, axis=0, keepdims=True)
        dz_ref[...] = jnp.zeros_like(dz_ref)
        dz_ref[:, :LANES] = dlogit.astype(BF16)

    rrow = lambda b, s: b * nsb + (nsb - 1 - s)
    return pl.pallas_call(
        body, name="gate_bwd", grid=(B, nsb),
        in_specs=[pl.BlockSpec((npair, tc, LANES), lambda b, s: (0, rrow(b, s), 0)),
                  pl.BlockSpec((tc, LANES), lambda b, s: (rrow(b, s), col_blk)),
                  pl.BlockSpec((1, LANES), lambda b, s: (0, 0))],
        out_specs=[pl.BlockSpec((tc, fp), lambda b, s: (rrow(b, s), 0)),
                   pl.BlockSpec((8, LANES), lambda b, s: (0, 0))],
        out_shape=[SDS((T, fp), BF16), SDS((8, LANES), F32)],
        scratch_shapes=[pltpu.VMEM((1, LANES), F32)],
        compiler_params=_params(("arbitrary", "arbitrary")),
    )(dc8, z, b_pad)


def _matmul_tn(a, b, name, tmm, tn, tk):
    T, M = a.shape
    N = b.shape[1]
    tmm, tn, tk = min(tmm, M), min(tn, N), min(tk, T)

    def body(a_ref, b_ref, o_ref):
        @pl.when(pl.program_id(2) == 0)
        def _():
            o_ref[...] = jnp.zeros_like(o_ref)

        o_ref[...] = o_ref[...] + _dot_tn(a_ref[...], b_ref[...])

    return pl.pallas_call(
        body, name=name, grid=(M // tmm, N // tn, T // tk),
        in_specs=[pl.BlockSpec((tk, tmm), lambda i, j, k: (k, i)),
                  pl.BlockSpec((tk, tn), lambda i, j, k: (k, j))],
        out_specs=pl.BlockSpec((tmm, tn), lambda i, j, k: (i, j)),
        out_shape=SDS((M, N), F32),
        compiler_params=_params(("parallel", "parallel", "arbitrary")),
    )(a, b)


def _dh_rms_bwd(pieces, w_t, x2, g, dout, tm, tk, parts):
    T, D = x2.shape
    nks = [p.shape[1] // tk for p in pieces]
    starts = [sum(nks[:k]) for k in range(len(pieces))]
    nk = sum(nks)
    ni = T // tm
    n = len(parts)

    def body(*refs):
        dz_refs = refs[:len(pieces)]
        w_ref, x_ref, g_ref, do_ref = refs[len(pieces):len(pieces) + 4]
        part_refs = refs[len(pieces) + 4:len(pieces) + 4 + n]
        gx_ref, dg_ref = refs[len(pieces) + 4 + n:len(pieces) + 6 + n]
        slot_refs = refs[len(pieces) + 6 + n:len(pieces) + 6 + 2 * n]
        acc_ref, send_sems, recv_sems = refs[len(pieces) + 6 + 2 * n:]
        k = pl.program_id(1)
        first_step = (pl.program_id(0) == 0) & (k == 0)
        last_step = (pl.program_id(0) == ni - 1) & (k == nk - 1)
        x, y, c = _place()
        chips = [(1 - x, y), (x, 1 - y), (1 - x, 1 - y)]

        def copy(a, f, to):
            cx, cy = chips[f]
            return pltpu.make_async_remote_copy(
                src_ref=part_refs[a].at[2 * cx + cy], dst_ref=slot_refs[a].at[f],
                send_sem=send_sems.at[a * 3 + f], recv_sem=recv_sems.at[a * 3 + f],
                device_id=to, device_id_type=MESH)

        @pl.when(first_step)
        def _():
            dg_ref[...] = jnp.zeros_like(dg_ref)
            for a in range(n):
                for f in range(3):
                    copy(a, f, (*chips[f], c)).start()

        @pl.when(last_step)
        def _():
            for a in range(n):
                for f in range(3):
                    copy(a, f, (x, y, c)).wait_recv()
            for a in range(n):
                for f in range(3):
                    copy(a, f, (*chips[f], c)).wait_send()

        @pl.when(k == 0)
        def _():
            acc_ref[...] = jnp.zeros_like(acc_ref)

        for dz_ref, st, cnt in zip(dz_refs, starts, nks):
            @pl.when((k >= st) & (k < st + cnt))
            def _(dz_ref=dz_ref):
                acc_ref[...] = acc_ref[...] + _dot(dz_ref[...], w_ref[...])

        @pl.when(k == nk - 1)
        def _():
            x = x_ref[...]
            r = lax.rsqrt(jnp.mean(x * x, axis=-1, keepdims=True) + EPS)
            xh = x * r
            dh = acc_ref[...]
            dg_ref[0:1, :] = dg_ref[0:1, :] + jnp.sum(dh * xh, axis=0, keepdims=True)
            dxn = dh * g_ref[...]
            gx_ref[...] = do_ref[...] + r * (dxn - xh * jnp.mean(dxn * xh, axis=-1, keepdims=True))

    def piece_spec(st, cnt):
        return pl.BlockSpec((tm, tk), lambda i, k: (i, jnp.clip(k - st, 0, cnt - 1)))

    tspec = pl.BlockSpec((tm, D), lambda i, k: (i, 0))
    return pl.pallas_call(
        body, name="dh_rms_bwd", grid=(T // tm, nk),
        in_specs=[piece_spec(st, cnt) for st, cnt in zip(starts, nks)]
        + [pl.BlockSpec((tk, D), lambda i, k: (k, 0)), tspec, pl.BlockSpec((1, D), lambda i, k: (0, 0)), tspec]
        + [ANY] * n,
        out_specs=[tspec, pl.BlockSpec((8, D), lambda i, k: (0, 0))] + [ANY] * n,
        out_shape=[SDS((T, D), F32), SDS((8, D), F32)] + [SDS((3,) + p.shape[1:], p.dtype) for p in parts],
        scratch_shapes=[pltpu.VMEM((tm, D), F32),
                        pltpu.SemaphoreType.DMA((3 * n,)), pltpu.SemaphoreType.DMA((3 * n,))],
        compiler_params=_params(("arbitrary", "arbitrary")),
    )(*pieces, w_t, x2, g, dout, *parts)


def _block_plan(R, C, tr, tc):
    br = min(tr, R)
    if R % br == 0:
        return (br, C), R // br, lambda i: (i, 0)
    bc = min(tc, C)
    assert C % bc == 0
    return (R, bc), C // bc, lambda i: (0, i)


def _ew_call(body, name, ins, n_out, out_dtypes, tr, tc):
    R, C = ins[0].shape
    blk, steps, imap = _block_plan(R, C, tr, tc)
    spec = pl.BlockSpec(blk, imap)
    return pl.pallas_call(
        body, name=name, grid=(steps,),
        in_specs=[spec] * len(ins), out_specs=[spec] * n_out,
        out_shape=[SDS((R, C), dt) for dt in out_dtypes],
        compiler_params=_params(("parallel",)),
    )(*ins)


def _pair_sum_bf16(a, b, name):
    def body(a_ref, b_ref, o_ref):
        o_ref[...] = (a_ref[...] + b_ref[...]).astype(BF16)

    return _ew_call(body, name, [a, b], 1, [BF16], 256, LANES)[0]


def _sum_slots(slots, name, first=None, tr=256):
    n, R, C = slots.shape
    blk, steps, imap = _block_plan(R, C, tr, 2 * LANES)
    lead = [] if first is None else [first]

    def body(*refs):
        s_ref, o_ref = refs[-2:]
        acc = refs[0][...].astype(F32) if lead else s_ref[0].astype(F32)
        for k in range(0 if lead else 1, n):
            acc = acc + s_ref[k].astype(F32)
        o_ref[...] = acc

    return pl.pallas_call(
        body, name=name, grid=(steps,),
        in_specs=[pl.BlockSpec(blk, imap)] * len(lead) + [pl.BlockSpec((n,) + blk, lambda i: (0,) + imap(i))],
        out_specs=pl.BlockSpec(blk, imap),
        out_shape=SDS((R, C), F32),
        compiler_params=_params(("parallel",)),
    )(*lead, slots)


def _adamw(w, g, m, v, name):
    def body(w_ref, g_ref, m_ref, v_ref, d_ref, nm_ref, nv_ref):
        gg = g_ref[...]
        nm = ADAM_B1 * m_ref[...] + (1.0 - ADAM_B1) * gg
        nv = ADAM_B2 * v_ref[...] + (1.0 - ADAM_B2) * (gg * gg)
        m_hat = nm / (1.0 - ADAM_B1 ** ADAM_STEP)
        v_hat = nv / (1.0 - ADAM_B2 ** ADAM_STEP)
        d_ref[...] = -ADAM_LR * (m_hat / (jnp.sqrt(v_hat) + ADAM_EPS) + ADAM_WD * w_ref[...])
        nm_ref[...] = nm
        nv_ref[...] = nv

    return _ew_call(body, name, [w, g, m, v], 3, [F32, F32, F32], 128, 2 * LANES)


ANY = pl.BlockSpec(memory_space=pl.ANY)


def _place():
    return lax.axis_index("x"), lax.axis_index("y"), lax.axis_index("c")


def _gather_chips(shards):
    n = len(shards)
    per = 7

    def body(*refs):
        ins, outs = refs[:n], refs[n:2 * n]
        send_sems, recv_sems = refs[2 * n:]
        x, y, c = _place()
        mine = 2 * x + y
        me, sibling = (x, y, c), (x, y, 1 - c)
        chips = [(1 - x, y), (x, 1 - y), (1 - x, 1 - y)]

        def copy(a, k, chip_idx, half, to, src=None):
            dst = outs[a].at[chip_idx, half]
            return pltpu.make_async_remote_copy(
                src_ref=dst if src is None else src, dst_ref=dst,
                send_sem=send_sems.at[a * per + k], recv_sem=recv_sems.at[a * per + k],
                device_id=to, device_id_type=MESH)

        def own(a, to):
            return pltpu.make_async_remote_copy(
                src_ref=ins[a], dst_ref=outs[a].at[mine],
                send_sem=send_sems.at[a * per + 6], recv_sem=recv_sems.at[a * per + 6],
                device_id=to, device_id_type=MESH)

        first = [copy(a, k, mine, c, (*chip, c), src=ins[a].at[c]) for a in range(n) for k, chip in enumerate(chips)]
        first += [own(a, sibling) for a in range(n)]
        for cp in first:
            cp.start()
        passed = []
        for k, (cx, cy) in enumerate(chips):
            for a in range(n):
                copy(a, k, 2 * cx + cy, c, me).wait_recv()
                fwd = copy(a, 3 + k, 2 * cx + cy, c, sibling)
                fwd.start()
                passed.append(fwd)
        for k, (cx, cy) in enumerate(chips):
            for a in range(n):
                copy(a, 3 + k, 2 * cx + cy, 1 - c, me).wait_recv()
        for a in range(n):
            own(a, me).wait_recv()
        for cp in first + passed:
            cp.wait_send()

    return pl.pallas_call(
        body, name="gather_chips",
        in_specs=[ANY] * n, out_specs=[ANY] * n,
        out_shape=[SDS((4,) + s.shape, s.dtype) for s in shards],
        scratch_shapes=[pltpu.SemaphoreType.DMA((per * n,)), pltpu.SemaphoreType.DMA((per * n,))],
    )(*shards)


def _pair_swap(halves):
    n = len(halves)

    def body(*refs):
        ins, outs = refs[:n], refs[n:2 * n]
        send_sems, recv_sems = refs[2 * n:]
        x, y, c = _place()
        copies = [pltpu.make_async_remote_copy(
            src_ref=ins[a].at[1 - c], dst_ref=outs[a], send_sem=send_sems.at[a], recv_sem=recv_sems.at[a],
            device_id=(x, y, 1 - c), device_id_type=MESH) for a in range(n)]
        for cp in copies:
            cp.start()
        for cp in copies:
            cp.wait()

    return pl.pallas_call(
        body, name="pair_swap",
        in_specs=[ANY] * n, out_specs=[ANY] * n,
        out_shape=[SDS(h.shape[1:], h.dtype) for h in halves],
        scratch_shapes=[pltpu.SemaphoreType.DMA((n,)), pltpu.SemaphoreType.DMA((n,))],
    )(*halves)


def _pair_send(arrs):
    n = len(arrs)

    def body(*refs):
        ins, outs = refs[:n], refs[n:2 * n]
        send_sems, recv_sems = refs[2 * n:]
        x, y, c = _place()
        copies = [pltpu.make_async_remote_copy(
            src_ref=ins[a], dst_ref=outs[a], send_sem=send_sems.at[a], recv_sem=recv_sems.at[a],
            device_id=(x, y, 1 - c), device_id_type=MESH) for a in range(n)]
        for cp in copies:
            cp.start()
        for cp in copies:
            cp.wait()

    return pl.pallas_call(
        body, name="pair_send",
        in_specs=[ANY] * n, out_specs=[ANY] * n,
        out_shape=[SDS(h.shape, h.dtype) for h in arrs],
        scratch_shapes=[pltpu.SemaphoreType.DMA((n,)), pltpu.SemaphoreType.DMA((n,))],
    )(*arrs)


def _gather_all(buf):
    flips = [(fx, fy, fc) for fx in (0, 1) for fy in (0, 1) for fc in (0, 1)][1:]

    def body(in_ref, out_ref, send_sems, recv_sems, local_sem):
        x, y, c = _place()
        me = 4 * x + 2 * y + c
        local = pltpu.make_async_copy(in_ref, out_ref.at[me], local_sem)
        local.start()
        sends = []
        for k, (fx, fy, fc) in enumerate(flips):
            cp = pltpu.make_async_remote_copy(
                src_ref=in_ref, dst_ref=out_ref.at[me], send_sem=send_sems.at[k], recv_sem=recv_sems.at[k],
                device_id=(x ^ fx, y ^ fy, c ^ fc), device_id_type=MESH)
            cp.start()
            sends.append(cp)
        for k, (fx, fy, fc) in enumerate(flips):
            src = 4 * (x ^ fx) + 2 * (y ^ fy) + (c ^ fc)
            pltpu.make_async_remote_copy(
                src_ref=in_ref, dst_ref=out_ref.at[src], send_sem=send_sems.at[k], recv_sem=recv_sems.at[k],
                device_id=(x, y, c), device_id_type=MESH).wait_recv()
        for cp in sends:
            cp.wait_send()
        local.wait()

    return pl.pallas_call(
        body, name="gather_all",
        in_specs=[ANY], out_specs=ANY,
        out_shape=SDS((8,) + buf.shape, buf.dtype),
        scratch_shapes=[pltpu.SemaphoreType.DMA((7,)), pltpu.SemaphoreType.DMA((7,)), pltpu.SemaphoreType.DMA],
    )(buf)


def _tiles(S, FW):
    big = FW % 512 == 0
    return dict(
        fp=512 if big else LANES,
        tn=1536 if big else LANES,
        tm_in=min(1024, S),
        t_attn=min(512, S),
        hb_fwd=4,
        hb_bwd=2,
        tm_prep=min(512, S),
        tm_mix=min(128, S),
        tc=min(256, S),
        tk=512 if big else LANES,
    )


def kernel(x, norm_g, w_in, b_forget, q_norm_g, k_norm_g, conv_w, conv_b, conv_ln_g, conv_ln_b, w_out, loss_target, m_norm_g, m_w_in, m_b_forget, m_q_norm_g, m_k_norm_g, m_conv_w, m_conv_b, m_conv_ln_g, m_conv_ln_b, m_w_out, v_norm_g, v_w_in, v_b_forget, v_q_norm_g, v_k_norm_g, v_conv_w, v_conv_b, v_conv_ln_g, v_conv_ln_b, v_w_out):
    B, S, D = x.shape
    H, dh = q_norm_g.shape[1:]
    FW = H * dh
    CW = conv_b.shape[-1]
    n_taps, cw_shard = conv_w.shape[1:]
    in_shard = w_in.shape[2]
    out_shard = w_out.shape[1]
    assert dh == HEAD_DIM and H % 2 == 0 and H <= LANES and FW == CW == D
    assert n_taps - 1 <= HALO and 4 * cw_shard == CW and 4 * out_shard == FW + CW
    assert 4 * in_shard == 4 * FW + 3 * CW + H
    T = B * S
    tl = _tiles(S, FW)
    fp = tl["fp"]
    xi, yi, ci = _place()

    w_t = jnp.transpose(w_in[0])
    conv_pad = jnp.pad(conv_w[0], ((0, HALO - n_taps), (0, 0)))
    g_in, g_out, g_cw = _gather_chips([
        w_t.astype(BF16).reshape(in_shard, 2, D // 2).transpose(1, 0, 2),
        w_out[0].astype(BF16).reshape(2, out_shard // 2, D),
        conv_pad.reshape(2, HALO // 2, cw_shard)])
    w_t_full = g_in.transpose(0, 2, 1, 3).reshape(4 * in_shard, D)
    w_out_full = g_out.reshape(FW + CW, D)
    conv_full = g_cw.reshape(4, HALO, cw_shard).transpose(1, 0, 2).reshape(HALO, CW)
    o_f = 3 * FW
    w_pack = jnp.concatenate([w_t_full[:o_f], w_t_full[o_f + H:],
                              jnp.pad(w_t_full[o_f:o_f + H], ((0, fp - H), (0, 0)))], axis=0)
    f_col = 4 * FW + 3 * CW

    x2 = x.reshape(T, D)
    tgt = loss_target.reshape(T, D)
    b_pad = jnp.pad(b_forget, ((0, 0), (0, LANES - H)))
    gq = q_norm_g.reshape(1, FW)
    gk = k_norm_g.reshape(1, FW)

    z, h = _fwd_in(x2, norm_g, w_pack, tl["tm_in"], tl["tn"])
    c = _gate_fwd(z, b_pad, B, S, H, f_col // LANES, tl["tc"])
    qa, ka, va = _attn_prep(z, c, gq, gk, B, S, H, tl["tm_prep"])
    oa = _attn_fwd(qa, ka, va, tl["t_attn"], tl["hb_fwd"])
    y, u2, a_nat, dout, dout_b, dy, loss_acc = _fwd_out(
        oa, z, x2, tgt, conv_full, conv_b, conv_ln_g, conv_ln_b, w_out_full, B, S, H, n_taps, tl["tm_mix"])
    loss = lax.psum(0.5 * loss_acc[0, 0] / D, ("x", "y", "c"))

    dzgf, dzgc, du2, doa, qb, sg_conv = _bwd_prep(dy, z, a_nat, oa, qa, u2, conv_ln_g, conv_ln_b, B, S, H, tl["tm_mix"])
    dzglu, dconv_w = _conv_bwd(du2, z, conv_full, B, S, n_taps, tl["tm_mix"])
    dqa, dka, dva = _attn_bwd(qb, ka, va, doa, tl["t_attn"], tl["hb_bwd"])
    dzq, dzk, dzv, dc8, dg_qk = _qk_bwd(dqa, dka, dva, z, gq, gk, B, S, H, tl["tm_prep"])
    dzf, db_f = _gate_bwd(dc8, z, b_pad, B, S, H, f_col // LANES, fp, tl["tc"])
    pieces = [dzq, dzk, dzv, dzgf, dzglu, dzgc, dzf]
    dw_pieces = [_matmul_tn(p, h, f"dw_in_{k}", 1024, 1024, 512) for k, p in enumerate(pieces)]
    dw_out = _matmul_tn(y, dout_b, "dw_out", 1024, 1024, 512)
    dw_t = jnp.concatenate(dw_pieces[:3] + [dw_pieces[6][:H]] + dw_pieces[3:6], axis=0)

    halves_in = dw_t.reshape(4 * in_shard, 2, D // 2).transpose(1, 0, 2)
    halves_out = dw_out.reshape(4, 2, out_shard // 2, D).transpose(1, 0, 2, 3).reshape(2, 2 * out_shard, D)
    got_in, got_out = _pair_swap([halves_in, halves_out])
    own_in = lax.dynamic_index_in_dim(halves_in, ci, 0, keepdims=False)
    own_out = lax.dynamic_index_in_dim(halves_out, ci, 0, keepdims=False)
    part_in = _pair_sum_bf16(own_in, got_in, "pair_sum_in").reshape(4, in_shard, D // 2)
    part_out = _pair_sum_bf16(own_out, got_out, "pair_sum_out").reshape(4, out_shard // 2, D)
    grad_x2, dg_norm, slots_in, slots_out = _dh_rms_bwd(
        pieces, w_pack, x2, norm_g, dout, tl["tm_in"], tl["tk"], [part_in, part_out])
    chip = 2 * xi + yi
    half_in = _sum_slots(slots_in, "chip_sum_in", lax.dynamic_index_in_dim(part_in, chip, 0, keepdims=False))
    half_out = _sum_slots(slots_out, "chip_sum_out", lax.dynamic_index_in_dim(part_out, chip, 0, keepdims=False))
    other_in, other_out = _pair_send([half_in, half_out])

    def both_halves(mine, other, axis):
        return jnp.where(ci == 0, jnp.concatenate([mine, other], axis=axis), jnp.concatenate([other, mine], axis=axis))

    grad_w_t = both_halves(half_in, other_in, 1)
    grad_w_out = both_halves(half_out, other_out, 0)

    small = jnp.concatenate([
        dg_norm[0:1], jnp.pad(db_f[0:1, :], ((0, 0), (0, D - LANES))), dg_qk[0:1], dg_qk[1:2],
        sg_conv[2:3], sg_conv[0:1], sg_conv[1:2], dconv_w], axis=0)
    n_small = small.shape[0]
    small_sum = _sum_slots(_gather_all(small), "small_sum", tr=n_small)
    grad_norm_g, grad_b_f = small_sum[0:1], small_sum[1:2, :H]
    grad_gq, grad_gk = small_sum[2:3].reshape(1, H, dh), small_sum[3:4].reshape(1, H, dh)
    grad_conv_b, grad_ln_g, grad_ln_b = small_sum[4:5], small_sum[5:6], small_sum[6:7]
    grad_conv_w = lax.dynamic_slice_in_dim(small_sum[7:7 + n_taps], chip * cw_shard, cw_shard, axis=1)

    d_t, nm_t, nv_t = _adamw(w_t, grad_w_t, jnp.transpose(m_w_in[0]), jnp.transpose(v_w_in[0]), "adamw_in")
    grad_w_in, d_in, nm_in, nv_in = (jnp.transpose(t)[None] for t in (grad_w_t, d_t, nm_t, nv_t))
    d_out, nm_out, nv_out = (t[None] for t in _adamw(w_out[0], grad_w_out, m_w_out[0], v_w_out[0], "adamw_out"))
    d_cw, nm_cw, nv_cw = (t[None] for t in _adamw(conv_w[0], grad_conv_w, m_conv_w[0], v_conv_w[0], "adamw_conv_w"))

    def rows(ws):
        return jnp.concatenate([jnp.pad(t.reshape(1, -1), ((0, 0), (0, D - t.size))) for t in ws], axis=0)

    small_w = [norm_g, b_forget, q_norm_g, k_norm_g, conv_b, conv_ln_g, conv_ln_b]
    small_m = [m_norm_g, m_b_forget, m_q_norm_g, m_k_norm_g, m_conv_b, m_conv_ln_g, m_conv_ln_b]
    small_v = [v_norm_g, v_b_forget, v_q_norm_g, v_k_norm_g, v_conv_b, v_conv_ln_g, v_conv_ln_b]
    d_s, nm_s, nv_s = _adamw(rows(small_w), small_sum[0:7], rows(small_m), rows(small_v), "adamw_small")

    def unpack(t):
        return [t[k:k + 1, :w.size].reshape(w.shape) for k, w in enumerate(small_w)]

    def order(s, in_, cw, out_):
        ng, bf, qg, kg, cb, lg, lb = s
        return [ng, in_, bf, qg, kg, cw, cb, lg, lb, out_]

    grads = [grad_norm_g, grad_w_in, grad_b_f, grad_gq, grad_gk, grad_conv_w[None],
             grad_conv_b, grad_ln_g, grad_ln_b, grad_w_out[None]]
    return (loss, grad_x2.reshape(B, S, D), *grads,
            *order(unpack(d_s), d_in, d_cw, d_out),
            *order(unpack(nm_s), nm_in, nm_cw, nm_out),
            *order(unpack(nv_s), nv_in, nv_cw, nv_out))
```

```python
import functools

import jax
import jax.numpy as jnp
from jax import lax
from jax.experimental import pallas as pl
from jax.experimental.pallas import tpu as pltpu

F32 = jnp.float32
BF16 = jnp.bfloat16
SDS = jax.ShapeDtypeStruct
MESH = pl.DeviceIdType.MESH

EPS = 1e-6
NEG_INF = -1e30
LANES = 128
SUBLANES = 8
HEAD_DIM = 64
HALO = 32
VMEM_LIMIT = 56 * 1024 * 1024

L_ROWSUM = 64
L_KDECAY = 67
L_LSE = 70
L_D = 65
L_QNORM = 73
L_KNORM = 74
NORM_SLACK = 1.02
SHIFT_MAX = 40.0

ADAM_LR = 0.001
ADAM_B1 = 0.9
ADAM_B2 = 0.999
ADAM_EPS = 1e-08
ADAM_WD = 0.01
ADAM_STEP = 10


def _params(sem, vmem=VMEM_LIMIT):
    return pltpu.CompilerParams(dimension_semantics=sem, vmem_limit_bytes=vmem)


def _sigmoid(x):
    return 1.0 / (1.0 + jnp.exp(-x))


def _split3(x):
    hi = x.astype(BF16).astype(F32)
    r = x - hi
    mid = r.astype(BF16).astype(F32)
    lo = (r - mid).astype(BF16).astype(F32)
    return hi, mid, lo


def _dot(a, b):
    return jnp.dot(a, b, preferred_element_type=F32)


def _dot_nt(a, b):
    return lax.dot_general(a, b, (((1,), (1,)), ((), ())), preferred_element_type=F32)


def _dot_tn(a, b):
    return lax.dot_general(a, b, (((0,), (0,)), ((), ())), preferred_element_type=F32)


def _lane(shape):
    return lax.broadcasted_iota(jnp.int32, shape, 1)


def _lane_col(x, lane, idx):
    return jnp.sum(jnp.where(lane == idx, x, 0.0), axis=-1, keepdims=True)


def _put3(base, lane, start, pieces):
    out = base
    for k, p in enumerate(pieces):
        out = jnp.where(lane == start + k, p, out)
    return out


def _half_stats(t, lo):
    del lo
    hi = t.astype(BF16)
    mid = (t - hi.astype(F32)).astype(BF16)
    row = lax.broadcasted_iota(jnp.int32, (2 * LANES, LANES), 0)
    col = lax.broadcasted_iota(jnp.int32, (2 * LANES, LANES), 1)
    same_half = (jnp.bitwise_and(row, LANES - 1) < HEAD_DIM) == (col < HEAD_DIM)
    return _dot(jnp.concatenate([hi, mid], axis=1), jnp.where(same_half, 1.0, 0.0).astype(BF16))


def _fwd_in(x2, g, w_t, tm, tn):
    T, D = x2.shape
    N = w_t.shape[0]

    def body(x_ref, g_ref, w_ref, z_ref, h_ref):
        @pl.when(pl.program_id(1) == 0)
        def _():
            x = x_ref[...]
            r = lax.rsqrt(jnp.mean(x * x, axis=-1, keepdims=True) + EPS)
            h_ref[...] = (x * r * g_ref[...]).astype(BF16)

        z_ref[...] = _dot_nt(h_ref[...], w_ref[...])

    return pl.pallas_call(
        body, name="fwd_in", grid=(T // tm, N // tn),
        in_specs=[pl.BlockSpec((tm, D), lambda i, j: (i, 0)),
                  pl.BlockSpec((1, D), lambda i, j: (0, 0)),
                  pl.BlockSpec((tn, D), lambda i, j: (j, 0))],
        out_specs=[pl.BlockSpec((tm, tn), lambda i, j: (i, j)),
                   pl.BlockSpec((tm, D), lambda i, j: (i, 0))],
        out_shape=[SDS((T, N), F32), SDS((T, D), BF16)],
        compiler_params=_params(("parallel", "arbitrary")),
    )(x2, g, w_t)


def _tri_cumsum(x, reverse):
    t = x.shape[0]
    row = lax.broadcasted_iota(jnp.int32, (t, t), 0)
    col = lax.broadcasted_iota(jnp.int32, (t, t), 1)
    tri = (row <= col) if reverse else (row >= col)
    tri = jnp.where(tri, 1.0, 0.0).astype(BF16)
    hi, mid, lo = _split3(x)
    return _dot(tri, hi.astype(BF16)) + _dot(tri, mid.astype(BF16)) + _dot(tri, lo.astype(BF16))


def _gate_fwd(z, b_pad, B, S, H, col_blk, tc):
    T = B * S
    nsb = S // tc

    def body(zf_ref, b_ref, c_ref, carry):
        @pl.when(pl.program_id(1) == 0)
        def _():
            carry[...] = jnp.zeros_like(carry)

        x = zf_ref[...] + b_ref[...]
        lf = jnp.minimum(x, 0.0) - jnp.log(1.0 + jnp.exp(-jnp.abs(x)))
        lf = jnp.where(_lane(lf.shape) < H, lf, 0.0)
        c_ref[...] = _tri_cumsum(lf, False) + carry[...]
        carry[...] = carry[...] + jnp.sum(lf, axis=0, keepdims=True)

    return pl.pallas_call(
        body, name="gate_fwd", grid=(B, nsb),
        in_specs=[pl.BlockSpec((tc, LANES), lambda b, s: (b * nsb + s, col_blk)),
                  pl.BlockSpec((1, LANES), lambda b, s: (0, 0))],
        out_specs=pl.BlockSpec((tc, LANES), lambda b, s: (b * nsb + s, 0)),
        out_shape=SDS((T, LANES), F32),
        scratch_shapes=[pltpu.VMEM((1, LANES), F32)],
        compiler_params=_params(("parallel", "arbitrary")),
    )(z, b_pad)


def _qk_normalize(x, g, lo):
    r = lax.rsqrt(_half_stats(x * x, lo) * (1.0 / HEAD_DIM) + EPS)
    return x * r * g


def _head_norms(x, lo):
    own = jnp.sqrt(_half_stats(x * x, lo)) * NORM_SLACK
    return [pltpu.roll(own, HEAD_DIM, 1), own]


def _attn_prep(z, c, gq, gk, B, S, H, tm):
    T = B * S
    FW = H * HEAD_DIM
    nsb = S // tm
    nfb = FW // LANES
    scale = HEAD_DIM ** -0.5

    def body(zq_ref, zk_ref, zv_ref, c_ref, gq_ref, gk_ref, qa_ref, ka_ref, va_ref):
        p = pl.program_id(1)
        lane = _lane((tm, LANES))
        lo = lane < HEAD_DIM
        qn = _qk_normalize(zq_ref[...], gq_ref[...], lo) * scale
        kn = _qk_normalize(zk_ref[...], gk_ref[...], lo)
        v = zv_ref[...]
        cc = c_ref[...]
        ones_q = ((lane >= L_KDECAY) & (lane < L_KDECAY + 3)).astype(F32)
        ones_k = (((lane >= L_ROWSUM) & (lane < L_ROWSUM + 3)) | ((lane >= L_LSE) & (lane < L_LSE + 3))).astype(F32)
        ones_v = ((lane >= L_ROWSUM) & (lane < L_D + 3)).astype(F32)
        q_norms, k_norms = _head_norms(qn, lo), _head_norms(kn, lo)
        for e in range(2):
            if e == 0:
                qe, ke, ve = qn, kn, v
            else:
                qe, ke, ve = (pltpu.roll(t, HEAD_DIM, 1) for t in (qn, kn, v))
            ch = _lane_col(cc, lane, 2 * p + e)
            pieces = _split3(ch)
            qa = jnp.where(lo, qe, _put3(ones_q, lane, L_ROWSUM, pieces))
            qa = jnp.where(lane == L_QNORM, q_norms[e], qa)
            ka = jnp.where(lo, ke, _put3(ones_k, lane, L_KDECAY, [-t for t in pieces]))
            ka = jnp.where(lane == L_KNORM, k_norms[e], ka)
            va = jnp.where(lo, ve, ones_v)
            qa_ref[0, e] = qa.astype(BF16)
            ka_ref[0, e] = ka.astype(BF16)
            va_ref[0, e] = va.astype(BF16)

    zspec = lambda off: pl.BlockSpec((tm, LANES), lambda i, p: (i, off + p))
    gspec = pl.BlockSpec((1, LANES), lambda i, p: (0, p))
    ospec = pl.BlockSpec((1, 2, tm, LANES), lambda i, p: (i // nsb, p, i % nsb, 0))
    oshape = SDS((B, H, S, LANES), BF16)
    return pl.pallas_call(
        body, name="attn_prep", grid=(T // tm, H // 2),
        in_specs=[zspec(0), zspec(nfb), zspec(2 * nfb),
                  pl.BlockSpec((tm, LANES), lambda i, p: (i, 0)), gspec, gspec],
        out_specs=[ospec, ospec, ospec],
        out_shape=[oshape, oshape, oshape],
        compiler_params=_params(("parallel", "arbitrary")),
    )(z, z, z, c, gq, gk)


def _attn_fwd(qa, ka, va, t, hb):
    B, H, S, _ = qa.shape
    nq = S // t

    def body(q_ref, k_ref, v_ref, o_ref, m_ref, acc_ref, kmax_ref, qs_ref):
        i = pl.program_id(2)
        lane = _lane((t, LANES))

        @pl.when(i == 0)
        def _():
            for e in range(hb):
                norms = jnp.where(_lane((S, LANES)) == L_KNORM, k_ref[0, e].astype(F32), 0.0)
                kmax_ref[e] = jnp.full((1, LANES), jnp.max(norms), F32)

        shifts = [_lane_col(q_ref[0, e].astype(F32), lane, L_QNORM) * kmax_ref[e][:, 0:1] for e in range(hb)]
        worst = shifts[0]
        for e in range(1, hb):
            worst = jnp.maximum(worst, shifts[e])
        bounded = jnp.max(worst) <= SHIFT_MAX
        acc_ref[...] = jnp.zeros_like(acc_ref)

        def tiles(step):
            def loop_body(j, carry):
                step(j, False)
                return carry

            lax.fori_loop(0, i, loop_body, 0)
            step(i, True)

        def keep_mask(n=t):
            return lax.broadcasted_iota(jnp.int32, (n, n), 0) >= lax.broadcasted_iota(jnp.int32, (n, n), 1)

        def finish(e, shift):
            acc = acc_ref[e]
            l = _lane_col(acc, lane, L_ROWSUM)
            o_ref[0, e] = jnp.where(lane < HEAD_DIM, acc / l, shift + jnp.log(l))

        @pl.when(bounded)
        def _():
            for e in range(hb):
                qs_ref[e] = _put3(q_ref[0, e].astype(F32), lane, L_LSE, _split3(-shifts[e])).astype(BF16)

            def pair(e, q_rows, k_start, n, masked):
                k_rows = pl.ds(pl.multiple_of(k_start, n), n)
                p = jnp.exp(_dot_nt(qs_ref[e, q_rows, :], k_ref[0, e, k_rows, :]))
                if masked:
                    p = jnp.where(keep_mask(n), p, 0.0)
                acc_ref[e, q_rows, :] = acc_ref[e, q_rows, :] + _dot(p.astype(BF16), v_ref[0, e, k_rows, :])

            def step(j, masked):
                for e in range(hb):
                    if masked:
                        h = t // 2
                        pair(e, slice(0, h), j * t, h, True)
                        pair(e, slice(h, t), j * t, h, False)
                        pair(e, slice(h, t), j * t + h, h, True)
                    else:
                        pair(e, slice(0, t), j * t, t, False)

            tiles(step)
            for e in range(hb):
                finish(e, shifts[e])

        @pl.when(jnp.logical_not(bounded))
        def _():
            m_ref[...] = jnp.full_like(m_ref, NEG_INF)

            def step(j, masked):
                rows = pl.ds(pl.multiple_of(j * t, t), t)
                for e in range(hb):
                    s = _dot_nt(q_ref[0, e], k_ref[0, e, rows, :])
                    if masked:
                        s = jnp.where(keep_mask(), s, NEG_INF)
                    m_prev = m_ref[e]
                    m_new = jnp.maximum(m_prev, jnp.max(s, axis=-1, keepdims=True))
                    alpha = jnp.exp(m_prev - m_new)
                    p = jnp.exp(s - m_new).astype(BF16)
                    acc_ref[e] = alpha * acc_ref[e] + _dot(p, v_ref[0, e, rows, :])
                    m_ref[e] = m_new

            tiles(step)
            for e in range(hb):
                finish(e, m_ref[e])

    return pl.pallas_call(
        body, name="attn_fwd", grid=(B, H // hb, nq),
        in_specs=[pl.BlockSpec((1, hb, t, LANES), lambda b, h, i: (b, h, i, 0)),
                  pl.BlockSpec((1, hb, S, LANES), lambda b, h, i: (b, h, 0, 0)),
                  pl.BlockSpec((1, hb, S, LANES), lambda b, h, i: (b, h, 0, 0))],
        out_specs=pl.BlockSpec((1, hb, t, LANES), lambda b, h, i: (b, h, i, 0)),
        out_shape=SDS((B, H, S, LANES), F32),
        scratch_shapes=[pltpu.VMEM((hb, t, 1), F32), pltpu.VMEM((hb, t, LANES), F32),
                        pltpu.VMEM((hb, 1, LANES), F32), pltpu.VMEM((hb, t, LANES), BF16)],
        compiler_params=_params(("parallel", "parallel", "arbitrary")),
    )(qa, ka, va)


def _fill_shifts(ext_ref, sh_ref):
    rows = sh_ref.shape[1]
    for b in range(1, SUBLANES):
        sh_ref[b - 1] = ext_ref[pl.ds(b, rows), :]


def _tap_window(ext_ref, sh_ref, off, tm, cols):
    b = off % SUBLANES
    if b == 0:
        return ext_ref[pl.ds(off, tm), cols]
    return sh_ref[b - 1, pl.ds(off - b, tm), cols]


def _conv_taps(w_ref, ext_ref, sh_ref, out_ref, n_taps, tm, offset_of, bias_ref=None):
    for cc in range(out_ref.shape[1] // LANES):
        cols = slice(cc * LANES, (cc + 1) * LANES)
        acc = None
        for j in range(n_taps):
            term = w_ref[j:j + 1, cols] * _tap_window(ext_ref, sh_ref, offset_of(j), tm, cols)
            acc = term if acc is None else acc + term
        out_ref[:, cols] = acc if bias_ref is None else acc + bias_ref[:, cols]


def _layernorm_stats(u2):
    mu = jnp.mean(u2, axis=-1, keepdims=True)
    xc = u2 - mu
    rstd = lax.rsqrt(jnp.mean(xc * xc, axis=-1, keepdims=True) + EPS)
    return xc * rstd, rstd


def _fwd_out(oa, z, x2, tgt, conv_w, conv_b, ln_g, ln_b, w_out, B, S, H, n_taps, tm):
    T, D = x2.shape
    FW = H * HEAD_DIM
    CW = conv_w.shape[1]
    nsb = S // tm
    hb = tm // HALO

    def body(oa_ref, gf_ref, ga_ref, gb_ref, gc_ref, ha_ref, hb_ref, x_ref, t_ref, w_ref, cb_ref, lg_ref,
             lb_ref, wo_ref, y_ref, u2_ref, a_ref, do_ref, dob_ref, dy_ref, loss_ref, ext_ref, sh_ref):
        first_step = (pl.program_id(0) == 0) & (pl.program_id(1) == 0)

        @pl.when(first_step)
        def _():
            loss_ref[...] = jnp.zeros_like(loss_ref)

        u1 = ga_ref[...] * _sigmoid(gb_ref[...])
        halo = ha_ref[...] * _sigmoid(hb_ref[...])
        ext_ref[0:HALO, :] = jnp.where(pl.program_id(1) > 0, halo, 0.0)
        ext_ref[HALO:, :] = u1
        _fill_shifts(ext_ref, sh_ref)
        _conv_taps(w_ref, ext_ref, sh_ref, u2_ref, n_taps, tm, lambda j: HALO - (n_taps - 1) + j, cb_ref)
        uh, _ = _layernorm_stats(u2_ref[...])
        u3 = uh * lg_ref[...] + lb_ref[...]
        gc = gc_ref[...]
        yu = u3 * _sigmoid(u3) * (gc * _sigmoid(gc))
        y_ref[:, FW:] = yu.astype(BF16)

        lane = _lane((tm, LANES))
        lo = lane < HEAD_DIM
        for p in range(H // 2):
            a_ref[:, p * LANES:(p + 1) * LANES] = jnp.where(
                lo, oa_ref[0, 2 * p], pltpu.roll(oa_ref[0, 2 * p + 1], HEAD_DIM, 1))
        gf = gf_ref[...]
        y_ref[:, :FW] = (a_ref[...] * (gf * _sigmoid(gf))).astype(BF16)

        out = x_ref[...] + _dot(y_ref[...], wo_ref[...])
        diff = out - t_ref[...]
        loss_ref[...] = loss_ref[...] + jnp.sum(diff * diff)
        dout = diff * (1.0 / D)
        do_ref[...] = dout
        dob = dout.astype(BF16)
        dob_ref[...] = dob
        dy_ref[...] = _dot_nt(dob, wo_ref[...])

    row = lambda b, s: b * nsb + s
    zspec = lambda cb: pl.BlockSpec((tm, FW), lambda b, s: (row(b, s), cb))
    hspec = lambda cb: pl.BlockSpec((HALO, CW), lambda b, s: (jnp.maximum(row(b, s) * hb - 1, 0), cb))
    vspec = pl.BlockSpec((1, CW), lambda b, s: (0, 0))
    tspec = lambda w: pl.BlockSpec((tm, w), lambda b, s: (row(b, s), 0))
    return pl.pallas_call(
        body, name="fwd_out", grid=(B, nsb),
        in_specs=[pl.BlockSpec((1, H, tm, LANES), lambda b, s: (b, 0, s, 0)),
                  zspec(3), zspec(4), zspec(5), zspec(6), hspec(4), hspec(5),
                  tspec(D), tspec(D),
                  pl.BlockSpec((HALO, CW), lambda b, s: (0, 0)), vspec, vspec, vspec,
                  pl.BlockSpec((FW + CW, D), lambda b, s: (0, 0))],
        out_specs=[tspec(FW + CW), tspec(CW), tspec(FW), tspec(D), tspec(D), tspec(FW + CW),
                   pl.BlockSpec((8, LANES), lambda b, s: (0, 0))],
        out_shape=[SDS((T, FW + CW), BF16), SDS((T, CW), F32), SDS((T, FW), F32), SDS((T, D), F32),
                   SDS((T, D), BF16), SDS((T, FW + CW), F32), SDS((8, LANES), F32)],
        scratch_shapes=[pltpu.VMEM((tm + HALO, CW), F32),
                        pltpu.VMEM((SUBLANES - 1, tm + HALO - SUBLANES, CW), F32)],
        compiler_params=_params(("arbitrary", "arbitrary")),
    )(oa, z, z, z, z, z, z, x2, tgt, conv_w, conv_b, ln_g, ln_b, w_out)


def _bwd_prep(dy, z, a_nat, oa, qa, u2, ln_g, ln_b, B, S, H, tm):
    T = B * S
    FW = H * HEAD_DIM
    CW = u2.shape[1]
    nsb = S // tm

    def body(dya_ref, dyu_ref, gf_ref, gc_ref, a_ref, oa_ref, qa_ref, u2_ref, lg_ref, lb_ref,
             dzgf_ref, dzgc_ref, du2_ref, doa_ref, qb_ref, sg_ref):
        first_step = (pl.program_id(0) == 0) & (pl.program_id(1) == 0)

        @pl.when(first_step)
        def _():
            sg_ref[...] = jnp.zeros_like(sg_ref)

        gf = gf_ref[...]
        sg = _sigmoid(gf)
        a = a_ref[...]
        dya = dya_ref[...]
        da = dya * (gf * sg)
        dzgf_ref[...] = (dya * a * (sg * (1.0 + gf * (1.0 - sg)))).astype(BF16)
        dd = da * a
        lane = _lane((tm, LANES))
        lo = lane < HEAD_DIM
        for p in range(H // 2):
            cols = slice(p * LANES, (p + 1) * LANES)
            da_p = da[:, cols]
            dd_p = dd[:, cols]
            d_heads = (jnp.sum(jnp.where(lo, dd_p, 0.0), axis=-1, keepdims=True),
                       jnp.sum(jnp.where(lo, 0.0, dd_p), axis=-1, keepdims=True))
            for e in range(2):
                da_e = da_p if e == 0 else pltpu.roll(da_p, HEAD_DIM, 1)
                d_e = d_heads[e]
                aug = _put3(jnp.zeros((tm, LANES), F32), lane, L_D, _split3(-d_e))
                doa_ref[0, 2 * p + e] = jnp.where(lo, da_e, aug).astype(BF16)
                lse = _lane_col(oa_ref[0, 2 * p + e], lane, L_ROWSUM)
                qb = _put3(qa_ref[0, 2 * p + e].astype(F32), lane, L_LSE, _split3(-lse))
                qb_ref[0, 2 * p + e] = qb.astype(BF16)

        gc = gc_ref[...]
        sc = _sigmoid(gc)
        dyu = dyu_ref[...]
        uh, rstd = _layernorm_stats(u2_ref[...])
        u3 = uh * lg_ref[...] + lb_ref[...]
        s3 = _sigmoid(u3)
        dzgc_ref[...] = (dyu * (u3 * s3) * (sc * (1.0 + gc * (1.0 - sc)))).astype(BF16)
        du3 = dyu * (gc * sc) * (s3 * (1.0 + u3 * (1.0 - s3)))
        sg_ref[0:1, :] = sg_ref[0:1, :] + jnp.sum(du3 * uh, axis=0, keepdims=True)
        sg_ref[1:2, :] = sg_ref[1:2, :] + jnp.sum(du3, axis=0, keepdims=True)
        duh = du3 * lg_ref[...]
        du2 = rstd * (duh - jnp.mean(duh, axis=-1, keepdims=True)
                      - uh * jnp.mean(duh * uh, axis=-1, keepdims=True))
        sg_ref[2:3, :] = sg_ref[2:3, :] + jnp.sum(du2, axis=0, keepdims=True)
        du2_ref[...] = du2

    row = lambda b, s: b * nsb + s
    tspec = lambda w, cb=0: pl.BlockSpec((tm, w), lambda b, s: (row(b, s), cb))
    hspec = pl.BlockSpec((1, H, tm, LANES), lambda b, s: (b, 0, s, 0))
    vspec = pl.BlockSpec((1, CW), lambda b, s: (0, 0))
    return pl.pallas_call(
        body, name="bwd_prep", grid=(B, nsb),
        in_specs=[tspec(FW, 0), tspec(CW, 1), tspec(FW, 3), tspec(CW, 6), tspec(FW), hspec, hspec,
                  tspec(CW), vspec, vspec],
        out_specs=[tspec(FW), tspec(CW), tspec(CW), hspec, hspec,
                   pl.BlockSpec((8, CW), lambda b, s: (0, 0))],
        out_shape=[SDS((T, FW), BF16), SDS((T, CW), BF16), SDS((T, CW), F32),
                   SDS((B, H, S, LANES), BF16), SDS((B, H, S, LANES), BF16), SDS((8, CW), F32)],
        compiler_params=_params(("arbitrary", "arbitrary")),
    )(dy, dy, z, z, a_nat, oa, qa, u2, ln_g, ln_b)


def _conv_bwd(du2, z, conv_w, B, S, n_taps, tm):
    T, CW = du2.shape
    nsb = S // tm
    hb = tm // HALO

    def body(d_ref, dh_ref, ga_ref, gb_ref, ha_ref, hb_ref, w_ref, dz_ref, dw_ref,
             extu_ref, extd_ref, shu_ref, shd_ref, du1_ref, dwacc_ref):
        s = pl.program_id(1)
        first_step = (pl.program_id(0) == 0) & (s == 0)
        last_step = (pl.program_id(0) == B - 1) & (s == nsb - 1)

        @pl.when(first_step)
        def _():
            dwacc_ref[...] = jnp.zeros_like(dwacc_ref)

        ga = ga_ref[...]
        sb = _sigmoid(gb_ref[...])
        halo = ha_ref[...] * _sigmoid(hb_ref[...])
        extu_ref[0:HALO, :] = jnp.where(s > 0, halo, 0.0)
        extu_ref[HALO:, :] = ga * sb
        extd_ref[0:tm, :] = d_ref[...]
        extd_ref[tm:, :] = jnp.where(s < nsb - 1, dh_ref[...], 0.0)
        _fill_shifts(extu_ref, shu_ref)
        _fill_shifts(extd_ref, shd_ref)
        _conv_taps(w_ref, extd_ref, shd_ref, du1_ref, n_taps, tm, lambda j: n_taps - 1 - j)
        for cc in range(CW // LANES):
            cols = slice(cc * LANES, (cc + 1) * LANES)
            parts = [None] * n_taps
            for r in range(tm // SUBLANES):
                dv = d_ref[r * SUBLANES:(r + 1) * SUBLANES, cols]
                for j in range(n_taps):
                    off = HALO - (n_taps - 1) + j + r * SUBLANES
                    term = dv * _tap_window(extu_ref, shu_ref, off, SUBLANES, cols)
                    parts[j] = term if parts[j] is None else parts[j] + term
            for j in range(n_taps):
                rows = slice(j * SUBLANES, (j + 1) * SUBLANES)
                dwacc_ref[rows, cols] = dwacc_ref[rows, cols] + parts[j]
        du1 = du1_ref[...]
        dz_ref[:, :CW] = (du1 * sb).astype(BF16)
        dz_ref[:, CW:] = (du1 * ga * (sb * (1.0 - sb))).astype(BF16)

        @pl.when(last_step)
        def _():
            dw_ref[...] = jnp.zeros_like(dw_ref)
            for j in range(n_taps):
                dw_ref[j:j + 1, :] = jnp.sum(dwacc_ref[j * SUBLANES:(j + 1) * SUBLANES, :], axis=0, keepdims=True)

    row = lambda b, s: b * nsb + s
    last_halo = T // HALO - 1
    return pl.pallas_call(
        body, name="conv_bwd", grid=(B, nsb),
        in_specs=[pl.BlockSpec((tm, CW), lambda b, s: (row(b, s), 0)),
                  pl.BlockSpec((HALO, CW), lambda b, s: (jnp.minimum((row(b, s) + 1) * hb, last_halo), 0)),
                  pl.BlockSpec((tm, CW), lambda b, s: (row(b, s), 4)),
                  pl.BlockSpec((tm, CW), lambda b, s: (row(b, s), 5)),
                  pl.BlockSpec((HALO, CW), lambda b, s: (jnp.maximum(row(b, s) * hb - 1, 0), 4)),
                  pl.BlockSpec((HALO, CW), lambda b, s: (jnp.maximum(row(b, s) * hb - 1, 0), 5)),
                  pl.BlockSpec((HALO, CW), lambda b, s: (0, 0))],
        out_specs=[pl.BlockSpec((tm, 2 * CW), lambda b, s: (row(b, s), 0)),
                   pl.BlockSpec((HALO, CW), lambda b, s: (0, 0))],
        out_shape=[SDS((T, 2 * CW), BF16), SDS((HALO, CW), F32)],
        scratch_shapes=[pltpu.VMEM((tm + HALO, CW), F32), pltpu.VMEM((tm + HALO, CW), F32),
                        pltpu.VMEM((SUBLANES - 1, tm + HALO - SUBLANES, CW), F32),
                        pltpu.VMEM((SUBLANES - 1, tm + HALO - SUBLANES, CW), F32),
                        pltpu.VMEM((tm, CW), F32), pltpu.VMEM((HALO * SUBLANES, CW), F32)],
        compiler_params=_params(("arbitrary", "arbitrary")),
    )(du2, du2, z, z, z, z, conv_w)


def _attn_bwd(qb, ka, va, doa, t, hb):
    B, H, S, _ = qb.shape
    nk = S // t

    def body(q_ref, k_ref, v_ref, do_ref, dq_ref, dk_ref, dv_ref):
        j = pl.program_id(2)

        @pl.when(j == 0)
        def _():
            dq_ref[...] = jnp.zeros_like(dq_ref)

        dk_ref[...] = jnp.zeros_like(dk_ref)
        dv_ref[...] = jnp.zeros_like(dv_ref)

        def pair(e, q_start, k_rows, n, masked):
            q_rows = pl.ds(pl.multiple_of(q_start, n), n)
            k = k_ref[0, e, k_rows, :]
            q = q_ref[0, e, q_rows, :]
            do = do_ref[0, e, q_rows, :]
            p = jnp.exp(_dot_nt(q, k))
            if masked:
                keep = lax.broadcasted_iota(jnp.int32, (n, n), 0) >= lax.broadcasted_iota(jnp.int32, (n, n), 1)
                p = jnp.where(keep, p, 0.0)
            ds = (p * _dot_nt(do, v_ref[0, e, k_rows, :])).astype(BF16)
            dv_ref[0, e, k_rows, :] = dv_ref[0, e, k_rows, :] + _dot_tn(p.astype(BF16), do)
            dk_ref[0, e, k_rows, :] = dk_ref[0, e, k_rows, :] + _dot_tn(ds, q)
            dq_ref[0, e, q_rows, :] = dq_ref[0, e, q_rows, :] + _dot(ds, k)

        def step(i, masked):
            for e in range(hb):
                pair(e, i * t, slice(0, t), t, masked)

        step(j, True)

        def loop_body(i, carry):
            step(i, False)
            return carry

        lax.fori_loop(j + 1, nk, loop_body, 0)

    full = pl.BlockSpec((1, hb, S, LANES), lambda b, h, j: (b, h, 0, 0))
    blk = pl.BlockSpec((1, hb, t, LANES), lambda b, h, j: (b, h, j, 0))
    oshape = SDS((B, H, S, LANES), F32)
    return pl.pallas_call(
        body, name="attn_bwd", grid=(B, H // hb, nk),
        in_specs=[full, blk, blk, full],
        out_specs=[full, blk, blk],
        out_shape=[oshape, oshape, oshape],
        compiler_params=_params(("parallel", "parallel", "arbitrary")),
    )(qb, ka, va, doa)


def _qk_bwd(dqa, dka, dva, z, gq, gk, B, S, H, tm):
    T = B * S
    FW = H * HEAD_DIM
    nsb = S // tm
    nfb = FW // LANES
    scale = HEAD_DIM ** -0.5

    def body(dq_ref, dk_ref, dv_ref, zq_ref, zk_ref, gq_ref, gk_ref, dzq_ref, dzk_ref, dzv_ref, dc_ref, dg_ref):
        p = pl.program_id(0)

        @pl.when(pl.program_id(1) == 0)
        def _():
            dg_ref[...] = jnp.zeros_like(dg_ref)

        lane = _lane((tm, LANES))
        lo = lane < HEAD_DIM

        def natural(ref):
            return jnp.where(lo, ref[0, 0], pltpu.roll(ref[0, 1], HEAD_DIM, 1))

        def norm_bwd(dn, x, g, row, out_ref):
            r = lax.rsqrt(_half_stats(x * x, lo) * (1.0 / HEAD_DIM) + EPS)
            xh = x * r
            dg_ref[row:row + 1, :] = dg_ref[row:row + 1, :] + jnp.sum(dn * xh, axis=0, keepdims=True)
            dxh = dn * g
            mm = _half_stats(dxh * xh, lo) * (1.0 / HEAD_DIM)
            out_ref[...] = (r * (dxh - xh * mm)).astype(BF16)

        norm_bwd(natural(dq_ref) * scale, zq_ref[...], gq_ref[...], 0, dzq_ref)
        norm_bwd(natural(dk_ref), zk_ref[...], gk_ref[...], 1, dzk_ref)
        dzv_ref[...] = natural(dv_ref).astype(BF16)

        dc = jnp.zeros((tm, LANES), F32)
        for e in range(2):
            val = _lane_col(dq_ref[0, e], lane, L_ROWSUM) - _lane_col(dk_ref[0, e], lane, L_KDECAY)
            dc = jnp.where(lane == 2 * p + e, val, dc)
        dc_ref[0] = dc

    hspec = pl.BlockSpec((1, 2, tm, LANES), lambda p, i: (i // nsb, p, i % nsb, 0))
    zspec = lambda off: pl.BlockSpec((tm, LANES), lambda p, i: (i, off + p))
    gspec = pl.BlockSpec((1, LANES), lambda p, i: (0, p))
    ospec = pl.BlockSpec((tm, LANES), lambda p, i: (i, p))
    return pl.pallas_call(
        body, name="qk_bwd", grid=(H // 2, T // tm),
        in_specs=[hspec, hspec, hspec, zspec(0), zspec(nfb), gspec, gspec],
        out_specs=[ospec, ospec, ospec,
                   pl.BlockSpec((1, tm, LANES), lambda p, i: (p, i, 0)),
                   pl.BlockSpec((8, LANES), lambda p, i: (0, p))],
        out_shape=[SDS((T, FW), BF16), SDS((T, FW), BF16), SDS((T, FW), BF16),
                   SDS((H // 2, T, LANES), F32), SDS((8, FW), F32)],
        compiler_params=_params(("parallel", "arbitrary")),
    )(dqa, dka, dva, z, z, gq, gk)


def _gate_bwd(dc8, z, b_pad, B, S, H, col_blk, fp, tc):
    T = B * S
    nsb = S // tc
    npair = dc8.shape[0]

    def body(dc_ref, zf_ref, b_ref, dz_ref, db_ref, carry):
        first_step = (pl.program_id(0) == 0) & (pl.program_id(1) == 0)

        @pl.when(first_step)
        def _():
            db_ref[...] = jnp.zeros_like(db_ref)

        @pl.when(pl.program_id(1) == 0)
        def _():
            carry[...] = jnp.zeros_like(carry)

        dc = dc_ref[0]
        for k in range(1, npair):
            dc = dc + dc_ref[k]
        dlf = _tri_cumsum(dc, True) + carry[...]
        carry[...] = carry[...] + jnp.sum(dc, axis=0, keepdims=True)
        x = zf_ref[...] + b_ref[...]
        dlogit = dlf * _sigmoid(-x)
        db_ref[0:1, :] = db_ref[0:1, :] + jnp.sum(dlogit, axis=0, keepdims=True)
        dz_ref[...] = jnp.zeros_like(dz_ref)
        dz_ref[:, :LANES] = dlogit.astype(BF16)

    rrow = lambda b, s: b * nsb + (nsb - 1 - s)
    return pl.pallas_call(
        body, name="gate_bwd", grid=(B, nsb),
        in_specs=[pl.BlockSpec((npair, tc, LANES), lambda b, s: (0, rrow(b, s), 0)),
                  pl.BlockSpec((tc, LANES), lambda b, s: (rrow(b, s), col_blk)),
                  pl.BlockSpec((1, LANES), lambda b, s: (0, 0))],
        out_specs=[pl.BlockSpec((tc, fp), lambda b, s: (rrow(b, s), 0)),
                   pl.BlockSpec((8, LANES), lambda b, s: (0, 0))],
        out_shape=[SDS((T, fp), BF16), SDS((8, LANES), F32)],
        scratch_shapes=[pltpu.VMEM((1, LANES), F32)],
        compiler_params=_params(("arbitrary", "arbitrary")),
    )(dc8, z, b_pad)


def _matmul_tn(a, b, name, tmm, tn, tk):
    T, M = a.shape
    N = b.shape[1]
    tmm, tn, tk = min(tmm, M), min(tn, N), min(tk, T)

    def body(a_ref, b_ref, o_ref):
        @pl.when(pl.program_id(2) == 0)
        def _():
            o_ref[...] = jnp.zeros_like(o_ref)

        o_ref[...] = o_ref[...] + _dot_tn(a_ref[...], b_ref[...])

    return pl.pallas_call(
        body, name=name, grid=(M // tmm, N // tn, T // tk),
        in_specs=[pl.BlockSpec((tk, tmm), lambda i, j, k: (k, i)),
                  pl.BlockSpec((tk, tn), lambda i, j, k: (k, j))],
        out_specs=pl.BlockSpec((tmm, tn), lambda i, j, k: (i, j)),
        out_shape=SDS((M, N), F32),
        compiler_params=_params(("parallel", "parallel", "arbitrary")),
    )(a, b)


def _dh_rms_bwd(pieces, w_t, x2, g, dout, tm, tk, parts):
    T, D = x2.shape
    nks = [p.shape[1] // tk for p in pieces]
    starts = [sum(nks[:k]) for k in range(len(pieces))]
    nk = sum(nks)
    ni = T // tm
    n = len(parts)

    def body(*refs):
        dz_refs = refs[:len(pieces)]
        w_ref, x_ref, g_ref, do_ref = refs[len(pieces):len(pieces) + 4]
        part_refs = refs[len(pieces) + 4:len(pieces) + 4 + n]
        gx_ref, dg_ref = refs[len(pieces) + 4 + n:len(pieces) + 6 + n]
        slot_refs = refs[len(pieces) + 6 + n:len(pieces) + 6 + 2 * n]
        acc_ref, send_sems, recv_sems = refs[len(pieces) + 6 + 2 * n:]
        k = pl.program_id(1)
        first_step = (pl.program_id(0) == 0) & (k == 0)
        last_step = (pl.program_id(0) == ni - 1) & (k == nk - 1)
        x, y, c = _place()
        chips = [(1 - x, y), (x, 1 - y), (1 - x, 1 - y)]

        def copy(a, f, to):
            cx, cy = chips[f]
            return pltpu.make_async_remote_copy(
                src_ref=part_refs[a].at[2 * cx + cy], dst_ref=slot_refs[a].at[f],
                send_sem=send_sems.at[a * 3 + f], recv_sem=recv_sems.at[a * 3 + f],
                device_id=to, device_id_type=MESH)

        @pl.when(first_step)
        def _():
            dg_ref[...] = jnp.zeros_like(dg_ref)
            for a in range(n):
                for f in range(3):
                    copy(a, f, (*chips[f], c)).start()

        @pl.when(last_step)
        def _():
            for a in range(n):
                for f in range(3):
                    copy(a, f, (x, y, c)).wait_recv()
            for a in range(n):
                for f in range(3):
                    copy(a, f, (*chips[f], c)).wait_send()

        @pl.when(k == 0)
        def _():
            acc_ref[...] = jnp.zeros_like(acc_ref)

        for dz_ref, st, cnt in zip(dz_refs, starts, nks):
            @pl.when((k >= st) & (k < st + cnt))
            def _(dz_ref=dz_ref):
                acc_ref[...] = acc_ref[...] + _dot(dz_ref[...], w_ref[...])

        @pl.when(k == nk - 1)
        def _():
            x = x_ref[...]
            r = lax.rsqrt(jnp.mean(x * x, axis=-1, keepdims=True) + EPS)
            xh = x * r
            dh = acc_ref[...]
            dg_ref[0:1, :] = dg_ref[0:1, :] + jnp.sum(dh * xh, axis=0, keepdims=True)
            dxn = dh * g_ref[...]
            gx_ref[...] = do_ref[...] + r * (dxn - xh * jnp.mean(dxn * xh, axis=-1, keepdims=True))

    def piece_spec(st, cnt):
        return pl.BlockSpec((tm, tk), lambda i, k: (i, jnp.clip(k - st, 0, cnt - 1)))

    tspec = pl.BlockSpec((tm, D), lambda i, k: (i, 0))
    return pl.pallas_call(
        body, name="dh_rms_bwd", grid=(T // tm, nk),
        in_specs=[piece_spec(st, cnt) for st, cnt in zip(starts, nks)]
        + [pl.BlockSpec((tk, D), lambda i, k: (k, 0)), tspec, pl.BlockSpec((1, D), lambda i, k: (0, 0)), tspec]
        + [ANY] * n,
        out_specs=[tspec, pl.BlockSpec((8, D), lambda i, k: (0, 0))] + [ANY] * n,
        out_shape=[SDS((T, D), F32), SDS((8, D), F32)] + [SDS((3,) + p.shape[1:], p.dtype) for p in parts],
        scratch_shapes=[pltpu.VMEM((tm, D), F32),
                        pltpu.SemaphoreType.DMA((3 * n,)), pltpu.SemaphoreType.DMA((3 * n,))],
        compiler_params=_params(("arbitrary", "arbitrary")),
    )(*pieces, w_t, x2, g, dout, *parts)


def _block_plan(R, C, tr, tc):
    br = min(tr, R)
    if R % br == 0:
        return (br, C), R // br, lambda i: (i, 0)
    bc = min(tc, C)
    assert C % bc == 0
    return (R, bc), C // bc, lambda i: (0, i)


def _ew_call(body, name, ins, n_out, out_dtypes, tr, tc):
    R, C = ins[0].shape
    blk, steps, imap = _block_plan(R, C, tr, tc)
    spec = pl.BlockSpec(blk, imap)
    return pl.pallas_call(
        body, name=name, grid=(steps,),
        in_specs=[spec] * len(ins), out_specs=[spec] * n_out,
        out_shape=[SDS((R, C), dt) for dt in out_dtypes],
        compiler_params=_params(("parallel",)),
    )(*ins)


def _sum_slots(slots, name, first=None, tr=256):
    n, R, C = slots.shape
    blk, steps, imap = _block_plan(R, C, tr, 2 * LANES)
    lead = [] if first is None else [first]

    def body(*refs):
        s_ref, o_ref = refs[-2:]
        acc = refs[0][...].astype(F32) if lead else s_ref[0].astype(F32)
        for k in range(0 if lead else 1, n):
            acc = acc + s_ref[k].astype(F32)
        o_ref[...] = acc

    return pl.pallas_call(
        body, name=name, grid=(steps,),
        in_specs=[pl.BlockSpec(blk, imap)] * len(lead) + [pl.BlockSpec((n,) + blk, lambda i: (0,) + imap(i))],
        out_specs=pl.BlockSpec(blk, imap),
        out_shape=SDS((R, C), F32),
        compiler_params=_params(("parallel",)),
    )(*lead, slots)


def _adamw(w, g, m, v, name):
    def body(w_ref, g_ref, m_ref, v_ref, d_ref, nm_ref, nv_ref):
        gg = g_ref[...]
        nm = ADAM_B1 * m_ref[...] + (1.0 - ADAM_B1) * gg
        nv = ADAM_B2 * v_ref[...] + (1.0 - ADAM_B2) * (gg * gg)
        m_hat = nm / (1.0 - ADAM_B1 ** ADAM_STEP)
        v_hat = nv / (1.0 - ADAM_B2 ** ADAM_STEP)
        d_ref[...] = -ADAM_LR * (m_hat / (jnp.sqrt(v_hat) + ADAM_EPS) + ADAM_WD * w_ref[...])
        nm_ref[...] = nm
        nv_ref[...] = nv

    return _ew_call(body, name, [w, g, m, v], 3, [F32, F32, F32], 128, 2 * LANES)


ANY = pl.BlockSpec(memory_space=pl.ANY)


def _place():
    return lax.axis_index("x"), lax.axis_index("y"), lax.axis_index("c")


def _gather_chips(shards, splits):
    n = len(shards)
    per = 7

    def body(*refs):
        ins, outs = refs[:n], refs[n:2 * n]
        send_sems, recv_sems = refs[2 * n:]
        x, y, c = _place()
        mine = 2 * x + y
        chips = [(1 - x, y), (x, 1 - y), (1 - x, 1 - y)]

        def rows(a, half):
            return pl.ds(0, splits[a]) if half == 0 else pl.ds(splits[a], ins[a].shape[0] - splits[a])

        def copy(a, k, chip_idx, half, to, src=None):
            dst = outs[a].at[chip_idx, rows(a, half)]
            return pltpu.make_async_remote_copy(
                src_ref=dst if src is None else src, dst_ref=dst,
                send_sem=send_sems.at[a * per + k], recv_sem=recv_sems.at[a * per + k],
                device_id=to, device_id_type=MESH)

        def own(a, to):
            return pltpu.make_async_remote_copy(
                src_ref=ins[a], dst_ref=outs[a].at[mine],
                send_sem=send_sems.at[a * per + 6], recv_sem=recv_sems.at[a * per + 6],
                device_id=to, device_id_type=MESH)

        for cc in (0, 1):
            @pl.when(c == cc)
            def _(cc=cc):
                me, sibling = (x, y, cc), (x, y, 1 - cc)
                first = [copy(a, k, mine, cc, (*chip, cc), src=ins[a].at[rows(a, cc)])
                         for a in range(n) for k, chip in enumerate(chips)]
                first += [own(a, sibling) for a in range(n)]
                for cp in first:
                    cp.start()
                passed = []
                for k, (cx, cy) in enumerate(chips):
                    for a in range(n):
                        copy(a, k, 2 * cx + cy, cc, me).wait_recv()
                        fwd = copy(a, 3 + k, 2 * cx + cy, cc, sibling)
                        fwd.start()
                        passed.append(fwd)
                for k, (cx, cy) in enumerate(chips):
                    for a in range(n):
                        copy(a, 3 + k, 2 * cx + cy, 1 - cc, me).wait_recv()
                for a in range(n):
                    own(a, me).wait_recv()
                for cp in first + passed:
                    cp.wait_send()

    return pl.pallas_call(
        body, name="gather_chips",
        in_specs=[ANY] * n, out_specs=[ANY] * n,
        out_shape=[SDS((4,) + s.shape, s.dtype) for s in shards],
        scratch_shapes=[pltpu.SemaphoreType.DMA((per * n,)), pltpu.SemaphoreType.DMA((per * n,))],
    )(*shards)


def _pair_swap(arrs):
    n = len(arrs)

    def body(*refs):
        ins, outs = refs[:n], refs[n:2 * n]
        send_sems, recv_sems = refs[2 * n:]
        x, y, c = _place()
        for cc in (0, 1):
            @pl.when(c == cc)
            def _(cc=cc):
                copies = []
                for a in range(n):
                    half = ins[a].shape[1] // 2
                    copies.append(pltpu.make_async_remote_copy(
                        src_ref=ins[a].at[:, pl.ds((1 - cc) * half, half)], dst_ref=outs[a],
                        send_sem=send_sems.at[a], recv_sem=recv_sems.at[a],
                        device_id=(x, y, 1 - cc), device_id_type=MESH))
                for cp in copies:
                    cp.start()
                for cp in copies:
                    cp.wait()

    return pl.pallas_call(
        body, name="pair_swap",
        in_specs=[ANY] * n, out_specs=[ANY] * n,
        out_shape=[SDS((h.shape[0], h.shape[1] // 2), h.dtype) for h in arrs],
        scratch_shapes=[pltpu.SemaphoreType.DMA((n,)), pltpu.SemaphoreType.DMA((n,))],
    )(*arrs)


def _pair_sum(arrs, got, tr):
    D = arrs[0].shape[1]
    half = D // 2
    cnts = [p.shape[0] // tr for p in arrs]
    starts = [sum(cnts[:k]) for k in range(len(arrs))]

    def body(*refs):
        own_refs, got_refs, o_ref = refs[:len(arrs)], refs[len(arrs):2 * len(arrs)], refs[-1]
        s = pl.program_id(0)
        c = lax.axis_index("c")
        for own_ref, got_ref, st, cnt in zip(own_refs, got_refs, starts, cnts):
            @pl.when((s >= st) & (s < st + cnt))
            def _(own_ref=own_ref, got_ref=got_ref):
                mine = jnp.where(c == 0, own_ref[:, :half], own_ref[:, half:])
                o_ref[...] = (mine + got_ref[...]).astype(BF16)

    def spec(width, st, cnt):
        return pl.BlockSpec((tr, width), lambda s: (jnp.clip(s - st, 0, cnt - 1), 0))

    return pl.pallas_call(
        body, name="pair_sum", grid=(sum(cnts),),
        in_specs=[spec(D, st, cnt) for st, cnt in zip(starts, cnts)]
        + [spec(half, st, cnt) for st, cnt in zip(starts, cnts)],
        out_specs=pl.BlockSpec((tr, half), lambda s: (s, 0)),
        out_shape=SDS((sum(cnts) * tr, half), BF16),
        compiler_params=_params(("arbitrary",)),
    )(*arrs, *got)


def _pair_send(arrs):
    n = len(arrs)

    def body(*refs):
        ins, outs = refs[:n], refs[n:2 * n]
        send_sems, recv_sems = refs[2 * n:]
        x, y, c = _place()
        copies = [pltpu.make_async_remote_copy(
            src_ref=ins[a], dst_ref=outs[a], send_sem=send_sems.at[a], recv_sem=recv_sems.at[a],
            device_id=(x, y, 1 - c), device_id_type=MESH) for a in range(n)]
        for cp in copies:
            cp.start()
        for cp in copies:
            cp.wait()

    return pl.pallas_call(
        body, name="pair_send",
        in_specs=[ANY] * n, out_specs=[ANY] * n,
        out_shape=[SDS(h.shape, h.dtype) for h in arrs],
        scratch_shapes=[pltpu.SemaphoreType.DMA((n,)), pltpu.SemaphoreType.DMA((n,))],
    )(*arrs)


def _gather_all(buf):
    flips = [(fx, fy, fc) for fx in (0, 1) for fy in (0, 1) for fc in (0, 1)][1:]

    def body(in_ref, out_ref, send_sems, recv_sems, local_sem):
        x, y, c = _place()
        me = 4 * x + 2 * y + c
        local = pltpu.make_async_copy(in_ref, out_ref.at[me], local_sem)
        local.start()
        sends = []
        for k, (fx, fy, fc) in enumerate(flips):
            cp = pltpu.make_async_remote_copy(
                src_ref=in_ref, dst_ref=out_ref.at[me], send_sem=send_sems.at[k], recv_sem=recv_sems.at[k],
                device_id=(x ^ fx, y ^ fy, c ^ fc), device_id_type=MESH)
            cp.start()
            sends.append(cp)
        for k, (fx, fy, fc) in enumerate(flips):
            src = 4 * (x ^ fx) + 2 * (y ^ fy) + (c ^ fc)
            pltpu.make_async_remote_copy(
                src_ref=in_ref, dst_ref=out_ref.at[src], send_sem=send_sems.at[k], recv_sem=recv_sems.at[k],
                device_id=(x, y, c), device_id_type=MESH).wait_recv()
        for cp in sends:
            cp.wait_send()
        local.wait()

    return pl.pallas_call(
        body, name="gather_all",
        in_specs=[ANY], out_specs=ANY,
        out_shape=SDS((8,) + buf.shape, buf.dtype),
        scratch_shapes=[pltpu.SemaphoreType.DMA((7,)), pltpu.SemaphoreType.DMA((7,)), pltpu.SemaphoreType.DMA],
    )(buf)


def _tiles(S, FW):
    big = FW % 512 == 0
    return dict(
        fp=512 if big else LANES,
        tn=1536 if big else LANES,
        tm_in=min(1024, S),
        t_attn=min(512, S),
        hb_fwd=4,
        hb_bwd=2,
        tm_prep=min(512, S),
        tm_mix=min(128, S),
        tc=min(256, S),
        tk=512 if big else LANES,
    )


def kernel(x, norm_g, w_in, b_forget, q_norm_g, k_norm_g, conv_w, conv_b, conv_ln_g, conv_ln_b, w_out, loss_target, m_norm_g, m_w_in, m_b_forget, m_q_norm_g, m_k_norm_g, m_conv_w, m_conv_b, m_conv_ln_g, m_conv_ln_b, m_w_out, v_norm_g, v_w_in, v_b_forget, v_q_norm_g, v_k_norm_g, v_conv_w, v_conv_b, v_conv_ln_g, v_conv_ln_b, v_w_out):
    B, S, D = x.shape
    H, dh = q_norm_g.shape[1:]
    FW = H * dh
    CW = conv_b.shape[-1]
    n_taps, cw_shard = conv_w.shape[1:]
    in_shard = w_in.shape[2]
    out_shard = w_out.shape[1]
    assert dh == HEAD_DIM and H % 2 == 0 and H <= LANES and FW == CW == D
    assert n_taps - 1 <= HALO and 4 * cw_shard == CW and 4 * out_shard == FW + CW
    assert 4 * in_shard == 4 * FW + 3 * CW + H
    T = B * S
    tl = _tiles(S, FW)
    fp = tl["fp"]
    xi, yi, ci = _place()

    w_t = jnp.transpose(w_in[0])
    conv_pad = jnp.pad(conv_w[0], ((0, HALO - n_taps), (0, 0)))
    bf16_rows = 2 * SUBLANES
    g_in, g_out, g_cw = _gather_chips(
        [w_t.astype(BF16), w_out[0].astype(BF16), conv_pad],
        [in_shard // 2 // bf16_rows * bf16_rows, out_shard // 2, HALO // 2])
    w_t_full = g_in.reshape(4 * in_shard, D)
    w_out_full = g_out.reshape(FW + CW, D)
    conv_full = g_cw.transpose(1, 0, 2).reshape(HALO, CW)
    o_f = 3 * FW
    w_pack = jnp.concatenate([w_t_full[:o_f], w_t_full[o_f + H:],
                              jnp.pad(w_t_full[o_f:o_f + H], ((0, fp - H), (0, 0)))], axis=0)
    f_col = 4 * FW + 3 * CW

    x2 = x.reshape(T, D)
    tgt = loss_target.reshape(T, D)
    b_pad = jnp.pad(b_forget, ((0, 0), (0, LANES - H)))
    gq = q_norm_g.reshape(1, FW)
    gk = k_norm_g.reshape(1, FW)

    z, h = _fwd_in(x2, norm_g, w_pack, tl["tm_in"], tl["tn"])
    c = _gate_fwd(z, b_pad, B, S, H, f_col // LANES, tl["tc"])
    qa, ka, va = _attn_prep(z, c, gq, gk, B, S, H, tl["tm_prep"])
    oa = _attn_fwd(qa, ka, va, tl["t_attn"], tl["hb_fwd"])
    y, u2, a_nat, dout, dout_b, dy, loss_acc = _fwd_out(
        oa, z, x2, tgt, conv_full, conv_b, conv_ln_g, conv_ln_b, w_out_full, B, S, H, n_taps, tl["tm_mix"])

    dzgf, dzgc, du2, doa, qb, sg_conv = _bwd_prep(dy, z, a_nat, oa, qa, u2, conv_ln_g, conv_ln_b, B, S, H, tl["tm_mix"])
    dzglu, dconv_w = _conv_bwd(du2, z, conv_full, B, S, n_taps, tl["tm_mix"])
    dqa, dka, dva = _attn_bwd(qb, ka, va, doa, tl["t_attn"], tl["hb_bwd"])
    dzq, dzk, dzv, dc8, dg_qk = _qk_bwd(dqa, dka, dva, z, gq, gk, B, S, H, tl["tm_prep"])
    dzf, db_f = _gate_bwd(dc8, z, b_pad, B, S, H, f_col // LANES, fp, tl["tc"])
    pieces = [dzq, dzk, dzv, dzgf, dzglu, dzgc, dzf]
    dw_all = [_matmul_tn(p, h, f"dw_in_{k}", 1024, 1024, 1024) for k, p in enumerate(pieces)]
    dw_all.append(_matmul_tn(y, dout_b, "dw_out", 1024, 1024, 1024))

    summed = _pair_sum(dw_all, _pair_swap(dw_all), fp // 2)
    ends = [0]
    for t in dw_all:
        ends.append(ends[-1] + t.shape[0])
    cut = lambda k, rows=None: summed[ends[k]:(ends[k + 1] if rows is None else ends[k] + rows)]
    part_in = jnp.concatenate([cut(0), cut(1), cut(2), cut(6, H), cut(3), cut(4), cut(5)],
                              axis=0).reshape(4, in_shard, D // 2)
    part_out = cut(7).reshape(4, out_shard, D // 2)
    grad_x2, dg_norm, slots_in, slots_out = _dh_rms_bwd(
        pieces, w_pack, x2, norm_g, dout, tl["tm_in"], tl["tk"], [part_in, part_out])
    chip = 2 * xi + yi
    half_in = _sum_slots(slots_in, "chip_sum_in", lax.dynamic_index_in_dim(part_in, chip, 0, keepdims=False))
    half_out = _sum_slots(slots_out, "chip_sum_out", lax.dynamic_index_in_dim(part_out, chip, 0, keepdims=False))
    other_in, other_out = _pair_send([half_in, half_out])

    def both_halves(mine, other, axis):
        return jnp.where(ci == 0, jnp.concatenate([mine, other], axis=axis), jnp.concatenate([other, mine], axis=axis))

    grad_w_t = both_halves(half_in, other_in, 1)
    grad_w_out = both_halves(half_out, other_out, 1)

    lanes_to_d = lambda t: jnp.pad(t, ((0, 0), (0, D - LANES)))
    small = jnp.concatenate([
        dg_norm[0:1], lanes_to_d(db_f[0:1, :]), dg_qk[0:1], dg_qk[1:2],
        sg_conv[2:3], sg_conv[0:1], sg_conv[1:2], dconv_w, lanes_to_d(loss_acc[0:1, :])], axis=0)
    n_small = small.shape[0]
    small_sum = _sum_slots(_gather_all(small), "small_sum", tr=n_small)
    loss = 0.5 * small_sum[n_small - 1, 0] / D
    grad_norm_g, grad_b_f = small_sum[0:1], small_sum[1:2, :H]
    grad_gq, grad_gk = small_sum[2:3].reshape(1, H, dh), small_sum[3:4].reshape(1, H, dh)
    grad_conv_b, grad_ln_g, grad_ln_b = small_sum[4:5], small_sum[5:6], small_sum[6:7]
    grad_conv_w = lax.dynamic_slice_in_dim(small_sum[7:7 + n_taps], chip * cw_shard, cw_shard, axis=1)

    d_t, nm_t, nv_t = _adamw(w_t, grad_w_t, jnp.transpose(m_w_in[0]), jnp.transpose(v_w_in[0]), "adamw_in")
    grad_w_in, d_in, nm_in, nv_in = (jnp.transpose(t)[None] for t in (grad_w_t, d_t, nm_t, nv_t))
    d_out, nm_out, nv_out = (t[None] for t in _adamw(w_out[0], grad_w_out, m_w_out[0], v_w_out[0], "adamw_out"))
    d_cw, nm_cw, nv_cw = (t[None] for t in _adamw(conv_w[0], grad_conv_w, m_conv_w[0], v_conv_w[0], "adamw_conv_w"))

    def rows(ws):
        return jnp.concatenate([jnp.pad(t.reshape(1, -1), ((0, 0), (0, D - t.size))) for t in ws], axis=0)

    small_w = [norm_g, b_forget, q_norm_g, k_norm_g, conv_b, conv_ln_g, conv_ln_b]
    small_m = [m_norm_g, m_b_forget, m_q_norm_g, m_k_norm_g, m_conv_b, m_conv_ln_g, m_conv_ln_b]
    small_v = [v_norm_g, v_b_forget, v_q_norm_g, v_k_norm_g, v_conv_b, v_conv_ln_g, v_conv_ln_b]
    d_s, nm_s, nv_s = _adamw(rows(small_w), small_sum[0:7], rows(small_m), rows(small_v), "adamw_small")

    def unpack(t):
        return [t[k:k + 1, :w.size].reshape(w.shape) for k, w in enumerate(small_w)]

    def order(s, in_, cw, out_):
        ng, bf, qg, kg, cb, lg, lb = s
        return [ng, in_, bf, qg, kg, cw, cb, lg, lb, out_]

    grads = [grad_norm_g, grad_w_in, grad_b_f, grad_gq, grad_gk, grad_conv_w[None],
             grad_conv_b, grad_ln_g, grad_ln_b, grad_w_out[None]]
    return (loss, grad_x2.reshape(B, S, D), *grads,
            *order(unpack(d_s), d_in, d_cw, d_out),
            *order(unpack(nm_s), nm_in, nm_cw, nm_out),
            *order(unpack(nv_s), nv_in, nv_cw, nv_out))
```

```python
import jax
import jax.numpy as jnp
from jax import lax
from jax.experimental import pallas as pl
from jax.experimental.pallas import tpu as pltpu

F32 = jnp.float32
BF16 = jnp.bfloat16
SDS = jax.ShapeDtypeStruct
MESH = pl.DeviceIdType.MESH

EPS = 1e-6
NEG_INF = -1e30
LANES = 128
SUBLANES = 8
HEAD_DIM = 64
HALO = 32
VMEM_LIMIT = 56 * 1024 * 1024

L_ROWSUM = 64
L_KDECAY = 67
L_LSE = 70
L_D = 65
L_QNORM = 73
L_KNORM = 74
NORM_SLACK = 1.02
SHIFT_MAX = 40.0

ADAM_LR = 0.001
ADAM_B1 = 0.9
ADAM_B2 = 0.999
ADAM_EPS = 1e-08
ADAM_WD = 0.01
ADAM_STEP = 10


def _params(sem, vmem=VMEM_LIMIT):
    return pltpu.CompilerParams(dimension_semantics=sem, vmem_limit_bytes=vmem)


def _sigmoid(x):
    return 1.0 / (1.0 + jnp.exp(-x))


def _split3(x):
    hi = x.astype(BF16).astype(F32)
    r = x - hi
    mid = r.astype(BF16).astype(F32)
    lo = (r - mid).astype(BF16).astype(F32)
    return hi, mid, lo


def _dot(a, b):
    return jnp.dot(a, b, preferred_element_type=F32)


def _dot_nt(a, b):
    return lax.dot_general(a, b, (((1,), (1,)), ((), ())), preferred_element_type=F32)


def _dot_tn(a, b):
    return lax.dot_general(a, b, (((0,), (0,)), ((), ())), preferred_element_type=F32)


def _lane(shape):
    return lax.broadcasted_iota(jnp.int32, shape, 1)


def _lane_col(x, lane, idx):
    return jnp.sum(jnp.where(lane == idx, x, 0.0), axis=-1, keepdims=True)


def _put3(base, lane, start, pieces):
    out = base
    for k, p in enumerate(pieces):
        out = jnp.where(lane == start + k, p, out)
    return out


def _half_stats(t):
    hi = t.astype(BF16)
    mid = (t - hi.astype(F32)).astype(BF16)
    row = lax.broadcasted_iota(jnp.int32, (2 * LANES, LANES), 0)
    col = lax.broadcasted_iota(jnp.int32, (2 * LANES, LANES), 1)
    same_half = (jnp.bitwise_and(row, LANES - 1) < HEAD_DIM) == (col < HEAD_DIM)
    return _dot(jnp.concatenate([hi, mid], axis=1), jnp.where(same_half, 1.0, 0.0).astype(BF16))


def _fwd_in(x2, g, w_t, tm, tn):
    T, D = x2.shape
    N = w_t.shape[0]

    def body(x_ref, g_ref, w_ref, z_ref, h_ref):
        @pl.when(pl.program_id(1) == 0)
        def _():
            x = x_ref[...]
            r = lax.rsqrt(jnp.mean(x * x, axis=-1, keepdims=True) + EPS)
            h_ref[...] = (x * r * g_ref[...]).astype(BF16)

        z_ref[...] = _dot_nt(h_ref[...], w_ref[...])

    return pl.pallas_call(
        body, name="fwd_in", grid=(T // tm, N // tn),
        in_specs=[pl.BlockSpec((tm, D), lambda i, j: (i, 0)),
                  pl.BlockSpec((1, D), lambda i, j: (0, 0)),
                  pl.BlockSpec((tn, D), lambda i, j: (j, 0))],
        out_specs=[pl.BlockSpec((tm, tn), lambda i, j: (i, j)),
                   pl.BlockSpec((tm, D), lambda i, j: (i, 0))],
        out_shape=[SDS((T, N), F32), SDS((T, D), BF16)],
        compiler_params=_params(("parallel", "arbitrary")),
    )(x2, g, w_t)


def _tri_cumsum(x, reverse):
    t = x.shape[0]
    row = lax.broadcasted_iota(jnp.int32, (t, t), 0)
    col = lax.broadcasted_iota(jnp.int32, (t, t), 1)
    tri = (row <= col) if reverse else (row >= col)
    tri = jnp.where(tri, 1.0, 0.0).astype(BF16)
    hi, mid, lo = _split3(x)
    return _dot(tri, hi.astype(BF16)) + _dot(tri, mid.astype(BF16)) + _dot(tri, lo.astype(BF16))


def _gate_fwd(z, b_pad, B, S, H, col_blk, tc):
    T = B * S
    nsb = S // tc

    def body(zf_ref, b_ref, c_ref, carry):
        @pl.when(pl.program_id(1) == 0)
        def _():
            carry[...] = jnp.zeros_like(carry)

        x = zf_ref[...] + b_ref[...]
        lf = jnp.minimum(x, 0.0) - jnp.log(1.0 + jnp.exp(-jnp.abs(x)))
        lf = jnp.where(_lane(lf.shape) < H, lf, 0.0)
        c_ref[...] = _tri_cumsum(lf, False) + carry[...]
        carry[...] = carry[...] + jnp.sum(lf, axis=0, keepdims=True)

    return pl.pallas_call(
        body, name="gate_fwd", grid=(B, nsb),
        in_specs=[pl.BlockSpec((tc, LANES), lambda b, s: (b * nsb + s, col_blk)),
                  pl.BlockSpec((1, LANES), lambda b, s: (0, 0))],
        out_specs=pl.BlockSpec((tc, LANES), lambda b, s: (b * nsb + s, 0)),
        out_shape=SDS((T, LANES), F32),
        scratch_shapes=[pltpu.VMEM((1, LANES), F32)],
        compiler_params=_params(("parallel", "arbitrary")),
    )(z, b_pad)


def _qk_normalize(x, g):
    r = lax.rsqrt(_half_stats(x * x) * (1.0 / HEAD_DIM) + EPS)
    return x * r * g


def _head_norms(x):
    own = jnp.sqrt(_half_stats(x * x)) * NORM_SLACK
    return [pltpu.roll(own, HEAD_DIM, 1), own]


def _attn_prep(z, c, gq, gk, B, S, H, tm):
    T = B * S
    FW = H * HEAD_DIM
    nsb = S // tm
    nfb = FW // LANES
    scale = HEAD_DIM ** -0.5

    def body(zq_ref, zk_ref, zv_ref, c_ref, gq_ref, gk_ref, qa_ref, ka_ref, va_ref):
        p = pl.program_id(1)
        lane = _lane((tm, LANES))
        lo = lane < HEAD_DIM
        qn = _qk_normalize(zq_ref[...], gq_ref[...]) * scale
        kn = _qk_normalize(zk_ref[...], gk_ref[...])
        v = zv_ref[...]
        cc = c_ref[...]
        ones_q = ((lane >= L_KDECAY) & (lane < L_KDECAY + 3)).astype(F32)
        ones_k = (((lane >= L_ROWSUM) & (lane < L_ROWSUM + 3)) | ((lane >= L_LSE) & (lane < L_LSE + 3))).astype(F32)
        ones_v = ((lane >= L_ROWSUM) & (lane < L_D + 3)).astype(F32)
        q_norms, k_norms = _head_norms(qn), _head_norms(kn)
        for e in range(2):
            if e == 0:
                qe, ke, ve = qn, kn, v
            else:
                qe, ke, ve = (pltpu.roll(t, HEAD_DIM, 1) for t in (qn, kn, v))
            ch = _lane_col(cc, lane, 2 * p + e)
            pieces = _split3(ch)
            qa = jnp.where(lo, qe, _put3(ones_q, lane, L_ROWSUM, pieces))
            qa = jnp.where(lane == L_QNORM, q_norms[e], qa)
            ka = jnp.where(lo, ke, _put3(ones_k, lane, L_KDECAY, [-t for t in pieces]))
            ka = jnp.where(lane == L_KNORM, k_norms[e], ka)
            va = jnp.where(lo, ve, ones_v)
            qa_ref[0, e] = qa.astype(BF16)
            ka_ref[0, e] = ka.astype(BF16)
            va_ref[0, e] = va.astype(BF16)

    zspec = lambda off: pl.BlockSpec((tm, LANES), lambda i, p: (i, off + p))
    gspec = pl.BlockSpec((1, LANES), lambda i, p: (0, p))
    ospec = pl.BlockSpec((1, 2, tm, LANES), lambda i, p: (i // nsb, p, i % nsb, 0))
    oshape = SDS((B, H, S, LANES), BF16)
    return pl.pallas_call(
        body, name="attn_prep", grid=(T // tm, H // 2),
        in_specs=[zspec(0), zspec(nfb), zspec(2 * nfb),
                  pl.BlockSpec((tm, LANES), lambda i, p: (i, 0)), gspec, gspec],
        out_specs=[ospec, ospec, ospec],
        out_shape=[oshape, oshape, oshape],
        compiler_params=_params(("parallel", "arbitrary")),
    )(z, z, z, c, gq, gk)


def _attn_fwd(qa, ka, va, t, hb):
    B, H, S, _ = qa.shape
    nq = S // t

    def body(q_ref, k_ref, v_ref, o_ref, m_ref, acc_ref, kmax_ref, qs_ref):
        i = pl.program_id(2)
        lane = _lane((t, LANES))

        @pl.when(i == 0)
        def _():
            for e in range(hb):
                norms = jnp.where(_lane((S, LANES)) == L_KNORM, k_ref[0, e].astype(F32), 0.0)
                kmax_ref[e] = jnp.full((1, LANES), jnp.max(norms), F32)

        shifts = [_lane_col(q_ref[0, e].astype(F32), lane, L_QNORM) * kmax_ref[e][:, 0:1] for e in range(hb)]
        worst = shifts[0]
        for e in range(1, hb):
            worst = jnp.maximum(worst, shifts[e])
        bounded = jnp.max(worst) <= SHIFT_MAX
        acc_ref[...] = jnp.zeros_like(acc_ref)

        def tiles(step):
            def loop_body(j, carry):
                step(j, False)
                return carry

            lax.fori_loop(0, i, loop_body, 0)
            step(i, True)

        def keep_mask(n=t):
            return lax.broadcasted_iota(jnp.int32, (n, n), 0) >= lax.broadcasted_iota(jnp.int32, (n, n), 1)

        def finish(e, shift):
            acc = acc_ref[e]
            l = _lane_col(acc, lane, L_ROWSUM)
            o_ref[0, e] = jnp.where(lane < HEAD_DIM, acc / l, shift + jnp.log(l))

        @pl.when(bounded)
        def _():
            for e in range(hb):
                qs_ref[e] = _put3(q_ref[0, e].astype(F32), lane, L_LSE, _split3(-shifts[e])).astype(BF16)

            def pair(e, q_rows, k_start, n, masked):
                k_rows = pl.ds(pl.multiple_of(k_start, n), n)
                p = jnp.exp(_dot_nt(qs_ref[e, q_rows, :], k_ref[0, e, k_rows, :]))
                if masked:
                    p = jnp.where(keep_mask(n), p, 0.0)
                acc_ref[e, q_rows, :] = acc_ref[e, q_rows, :] + _dot(p.astype(BF16), v_ref[0, e, k_rows, :])

            def step(j, masked):
                for e in range(hb):
                    if masked:
                        h = t // 2
                        pair(e, slice(0, h), j * t, h, True)
                        pair(e, slice(h, t), j * t, h, False)
                        pair(e, slice(h, t), j * t + h, h, True)
                    else:
                        pair(e, slice(0, t), j * t, t, False)

            tiles(step)
            for e in range(hb):
                finish(e, shifts[e])

        @pl.when(jnp.logical_not(bounded))
        def _():
            m_ref[...] = jnp.full_like(m_ref, NEG_INF)

            def step(j, masked):
                rows = pl.ds(pl.multiple_of(j * t, t), t)
                for e in range(hb):
                    s = _dot_nt(q_ref[0, e], k_ref[0, e, rows, :])
                    if masked:
                        s = jnp.where(keep_mask(), s, NEG_INF)
                    m_prev = m_ref[e]
                    m_new = jnp.maximum(m_prev, jnp.max(s, axis=-1, keepdims=True))
                    alpha = jnp.exp(m_prev - m_new)
                    p = jnp.exp(s - m_new).astype(BF16)
                    acc_ref[e] = alpha * acc_ref[e] + _dot(p, v_ref[0, e, rows, :])
                    m_ref[e] = m_new

            tiles(step)
            for e in range(hb):
                finish(e, m_ref[e])

    return pl.pallas_call(
        body, name="attn_fwd", grid=(B, H // hb, nq),
        in_specs=[pl.BlockSpec((1, hb, t, LANES), lambda b, h, i: (b, h, i, 0)),
                  pl.BlockSpec((1, hb, S, LANES), lambda b, h, i: (b, h, 0, 0)),
                  pl.BlockSpec((1, hb, S, LANES), lambda b, h, i: (b, h, 0, 0))],
        out_specs=pl.BlockSpec((1, hb, t, LANES), lambda b, h, i: (b, h, i, 0)),
        out_shape=SDS((B, H, S, LANES), F32),
        scratch_shapes=[pltpu.VMEM((hb, t, 1), F32), pltpu.VMEM((hb, t, LANES), F32),
                        pltpu.VMEM((hb, 1, LANES), F32), pltpu.VMEM((hb, t, LANES), BF16)],
        compiler_params=_params(("parallel", "parallel", "arbitrary")),
    )(qa, ka, va)


def _fill_shifts(ext_ref, sh_ref):
    rows = sh_ref.shape[1]
    for b in range(1, SUBLANES):
        sh_ref[b - 1] = ext_ref[pl.ds(b, rows), :]


def _tap_window(ext_ref, sh_ref, off, tm, cols):
    b = off % SUBLANES
    if b == 0:
        return ext_ref[pl.ds(off, tm), cols]
    return sh_ref[b - 1, pl.ds(off - b, tm), cols]


def _conv_taps(w_ref, ext_ref, sh_ref, out_ref, n_taps, tm, offset_of, bias_ref=None):
    for cc in range(out_ref.shape[1] // LANES):
        cols = slice(cc * LANES, (cc + 1) * LANES)
        acc = None
        for j in range(n_taps):
            term = w_ref[j:j + 1, cols] * _tap_window(ext_ref, sh_ref, offset_of(j), tm, cols)
            acc = term if acc is None else acc + term
        out_ref[:, cols] = acc if bias_ref is None else acc + bias_ref[:, cols]


def _layernorm_stats(u2):
    mu = jnp.mean(u2, axis=-1, keepdims=True)
    xc = u2 - mu
    rstd = lax.rsqrt(jnp.mean(xc * xc, axis=-1, keepdims=True) + EPS)
    return xc * rstd, rstd


def _fwd_out(oa, z, x2, tgt, conv_w, conv_b, ln_g, ln_b, w_out, B, S, H, n_taps, tm):
    T, D = x2.shape
    FW = H * HEAD_DIM
    CW = conv_w.shape[1]
    nsb = S // tm
    hb = tm // HALO

    def body(oa_ref, gf_ref, ga_ref, gb_ref, gc_ref, ha_ref, hb_ref, x_ref, t_ref, w_ref, cb_ref, lg_ref,
             lb_ref, wo_ref, y_ref, u2_ref, a_ref, do_ref, dob_ref, dy_ref, loss_ref, ext_ref, sh_ref):
        first_step = (pl.program_id(0) == 0) & (pl.program_id(1) == 0)

        @pl.when(first_step)
        def _():
            loss_ref[...] = jnp.zeros_like(loss_ref)

        u1 = ga_ref[...] * _sigmoid(gb_ref[...])
        halo = ha_ref[...] * _sigmoid(hb_ref[...])
        ext_ref[0:HALO, :] = jnp.where(pl.program_id(1) > 0, halo, 0.0)
        ext_ref[HALO:, :] = u1
        _fill_shifts(ext_ref, sh_ref)
        _conv_taps(w_ref, ext_ref, sh_ref, u2_ref, n_taps, tm, lambda j: HALO - (n_taps - 1) + j, cb_ref)
        uh, _ = _layernorm_stats(u2_ref[...])
        u3 = uh * lg_ref[...] + lb_ref[...]
        gc = gc_ref[...]
        yu = u3 * _sigmoid(u3) * (gc * _sigmoid(gc))
        y_ref[:, FW:] = yu.astype(BF16)

        lane = _lane((tm, LANES))
        lo = lane < HEAD_DIM
        for p in range(H // 2):
            a_ref[:, p * LANES:(p + 1) * LANES] = jnp.where(
                lo, oa_ref[0, 2 * p], pltpu.roll(oa_ref[0, 2 * p + 1], HEAD_DIM, 1))
        gf = gf_ref[...]
        y_ref[:, :FW] = (a_ref[...] * (gf * _sigmoid(gf))).astype(BF16)

        out = x_ref[...] + _dot(y_ref[...], wo_ref[...])
        diff = out - t_ref[...]
        loss_ref[...] = loss_ref[...] + jnp.sum(diff * diff)
        dout = diff * (1.0 / D)
        do_ref[...] = dout
        dob = dout.astype(BF16)
        dob_ref[...] = dob
        dy_ref[...] = _dot_nt(dob, wo_ref[...])

    row = lambda b, s: b * nsb + s
    zspec = lambda cb: pl.BlockSpec((tm, FW), lambda b, s: (row(b, s), cb))
    hspec = lambda cb: pl.BlockSpec((HALO, CW), lambda b, s: (jnp.maximum(row(b, s) * hb - 1, 0), cb))
    vspec = pl.BlockSpec((1, CW), lambda b, s: (0, 0))
    tspec = lambda w: pl.BlockSpec((tm, w), lambda b, s: (row(b, s), 0))
    return pl.pallas_call(
        body, name="fwd_out", grid=(B, nsb),
        in_specs=[pl.BlockSpec((1, H, tm, LANES), lambda b, s: (b, 0, s, 0)),
                  zspec(3), zspec(4), zspec(5), zspec(6), hspec(4), hspec(5),
                  tspec(D), tspec(D),
                  pl.BlockSpec((HALO, CW), lambda b, s: (0, 0)), vspec, vspec, vspec,
                  pl.BlockSpec((FW + CW, D), lambda b, s: (0, 0))],
        out_specs=[tspec(FW + CW), tspec(CW), tspec(FW), tspec(D), tspec(D), tspec(FW + CW),
                   pl.BlockSpec((8, LANES), lambda b, s: (0, 0))],
        out_shape=[SDS((T, FW + CW), BF16), SDS((T, CW), F32), SDS((T, FW), F32), SDS((T, D), F32),
                   SDS((T, D), BF16), SDS((T, FW + CW), F32), SDS((8, LANES), F32)],
        scratch_shapes=[pltpu.VMEM((tm + HALO, CW), F32),
                        pltpu.VMEM((SUBLANES - 1, tm + HALO - SUBLANES, CW), F32)],
        compiler_params=_params(("arbitrary", "arbitrary")),
    )(oa, z, z, z, z, z, z, x2, tgt, conv_w, conv_b, ln_g, ln_b, w_out)


def _bwd_prep(dy, z, a_nat, oa, qa, u2, ln_g, ln_b, B, S, H, tm):
    T = B * S
    FW = H * HEAD_DIM
    CW = u2.shape[1]
    nsb = S // tm

    def body(dya_ref, dyu_ref, gf_ref, gc_ref, a_ref, oa_ref, qa_ref, u2_ref, lg_ref, lb_ref,
             dzgf_ref, dzgc_ref, du2_ref, doa_ref, qb_ref, sg_ref):
        first_step = (pl.program_id(0) == 0) & (pl.program_id(1) == 0)

        @pl.when(first_step)
        def _():
            sg_ref[...] = jnp.zeros_like(sg_ref)

        gf = gf_ref[...]
        sg = _sigmoid(gf)
        a = a_ref[...]
        dya = dya_ref[...]
        da = dya * (gf * sg)
        dzgf_ref[...] = (dya * a * (sg * (1.0 + gf * (1.0 - sg)))).astype(BF16)
        dd = da * a
        lane = _lane((tm, LANES))
        lo = lane < HEAD_DIM
        for p in range(H // 2):
            cols = slice(p * LANES, (p + 1) * LANES)
            da_p = da[:, cols]
            dd_p = dd[:, cols]
            d_heads = (jnp.sum(jnp.where(lo, dd_p, 0.0), axis=-1, keepdims=True),
                       jnp.sum(jnp.where(lo, 0.0, dd_p), axis=-1, keepdims=True))
            for e in range(2):
                da_e = da_p if e == 0 else pltpu.roll(da_p, HEAD_DIM, 1)
                d_e = d_heads[e]
                aug = _put3(jnp.zeros((tm, LANES), F32), lane, L_D, _split3(-d_e))
                doa_ref[0, 2 * p + e] = jnp.where(lo, da_e, aug).astype(BF16)
                lse = _lane_col(oa_ref[0, 2 * p + e], lane, L_ROWSUM)
                qb = _put3(qa_ref[0, 2 * p + e].astype(F32), lane, L_LSE, _split3(-lse))
                qb_ref[0, 2 * p + e] = qb.astype(BF16)

        gc = gc_ref[...]
        sc = _sigmoid(gc)
        dyu = dyu_ref[...]
        uh, rstd = _layernorm_stats(u2_ref[...])
        u3 = uh * lg_ref[...] + lb_ref[...]
        s3 = _sigmoid(u3)
        dzgc_ref[...] = (dyu * (u3 * s3) * (sc * (1.0 + gc * (1.0 - sc)))).astype(BF16)
        du3 = dyu * (gc * sc) * (s3 * (1.0 + u3 * (1.0 - s3)))
        sg_ref[0:1, :] = sg_ref[0:1, :] + jnp.sum(du3 * uh, axis=0, keepdims=True)
        sg_ref[1:2, :] = sg_ref[1:2, :] + jnp.sum(du3, axis=0, keepdims=True)
        duh = du3 * lg_ref[...]
        du2 = rstd * (duh - jnp.mean(duh, axis=-1, keepdims=True)
                      - uh * jnp.mean(duh * uh, axis=-1, keepdims=True))
        sg_ref[2:3, :] = sg_ref[2:3, :] + jnp.sum(du2, axis=0, keepdims=True)
        du2_ref[...] = du2

    row = lambda b, s: b * nsb + s
    tspec = lambda w, cb=0: pl.BlockSpec((tm, w), lambda b, s: (row(b, s), cb))
    hspec = pl.BlockSpec((1, H, tm, LANES), lambda b, s: (b, 0, s, 0))
    vspec = pl.BlockSpec((1, CW), lambda b, s: (0, 0))
    return pl.pallas_call(
        body, name="bwd_prep", grid=(B, nsb),
        in_specs=[tspec(FW, 0), tspec(CW, 1), tspec(FW, 3), tspec(CW, 6), tspec(FW), hspec, hspec,
                  tspec(CW), vspec, vspec],
        out_specs=[tspec(FW), tspec(CW), tspec(CW), hspec, hspec,
                   pl.BlockSpec((8, CW), lambda b, s: (0, 0))],
        out_shape=[SDS((T, FW), BF16), SDS((T, CW), BF16), SDS((T, CW), F32),
                   SDS((B, H, S, LANES), BF16), SDS((B, H, S, LANES), BF16), SDS((8, CW), F32)],
        compiler_params=_params(("arbitrary", "arbitrary")),
    )(dy, dy, z, z, a_nat, oa, qa, u2, ln_g, ln_b)


def _conv_bwd(du2, z, conv_w, B, S, n_taps, tm):
    T, CW = du2.shape
    nsb = S // tm
    hb = tm // HALO

    def body(d_ref, dh_ref, ga_ref, gb_ref, ha_ref, hb_ref, w_ref, dz_ref, dw_ref,
             extu_ref, extd_ref, shu_ref, shd_ref, du1_ref, dwacc_ref):
        s = pl.program_id(1)
        first_step = (pl.program_id(0) == 0) & (s == 0)
        last_step = (pl.program_id(0) == B - 1) & (s == nsb - 1)

        @pl.when(first_step)
        def _():
            dwacc_ref[...] = jnp.zeros_like(dwacc_ref)

        ga = ga_ref[...]
        sb = _sigmoid(gb_ref[...])
        halo = ha_ref[...] * _sigmoid(hb_ref[...])
        extu_ref[0:HALO, :] = jnp.where(s > 0, halo, 0.0)
        extu_ref[HALO:, :] = ga * sb
        extd_ref[0:tm, :] = d_ref[...]
        extd_ref[tm:, :] = jnp.where(s < nsb - 1, dh_ref[...], 0.0)
        _fill_shifts(extu_ref, shu_ref)
        _fill_shifts(extd_ref, shd_ref)
        _conv_taps(w_ref, extd_ref, shd_ref, du1_ref, n_taps, tm, lambda j: n_taps - 1 - j)
        for cc in range(CW // LANES):
            cols = slice(cc * LANES, (cc + 1) * LANES)
            parts = [None] * n_taps
            for r in range(tm // SUBLANES):
                dv = d_ref[r * SUBLANES:(r + 1) * SUBLANES, cols]
                for j in range(n_taps):
                    off = HALO - (n_taps - 1) + j + r * SUBLANES
                    term = dv * _tap_window(extu_ref, shu_ref, off, SUBLANES, cols)
                    parts[j] = term if parts[j] is None else parts[j] + term
            for j in range(n_taps):
                rows = slice(j * SUBLANES, (j + 1) * SUBLANES)
                dwacc_ref[rows, cols] = dwacc_ref[rows, cols] + parts[j]
        du1 = du1_ref[...]
        dz_ref[:, :CW] = (du1 * sb).astype(BF16)
        dz_ref[:, CW:] = (du1 * ga * (sb * (1.0 - sb))).astype(BF16)

        @pl.when(last_step)
        def _():
            dw_ref[...] = jnp.zeros_like(dw_ref)
            for j in range(n_taps):
                dw_ref[j:j + 1, :] = jnp.sum(dwacc_ref[j * SUBLANES:(j + 1) * SUBLANES, :], axis=0, keepdims=True)

    row = lambda b, s: b * nsb + s
    last_halo = T // HALO - 1
    return pl.pallas_call(
        body, name="conv_bwd", grid=(B, nsb),
        in_specs=[pl.BlockSpec((tm, CW), lambda b, s: (row(b, s), 0)),
                  pl.BlockSpec((HALO, CW), lambda b, s: (jnp.minimum((row(b, s) + 1) * hb, last_halo), 0)),
                  pl.BlockSpec((tm, CW), lambda b, s: (row(b, s), 4)),
                  pl.BlockSpec((tm, CW), lambda b, s: (row(b, s), 5)),
                  pl.BlockSpec((HALO, CW), lambda b, s: (jnp.maximum(row(b, s) * hb - 1, 0), 4)),
                  pl.BlockSpec((HALO, CW), lambda b, s: (jnp.maximum(row(b, s) * hb - 1, 0), 5)),
                  pl.BlockSpec((HALO, CW), lambda b, s: (0, 0))],
        out_specs=[pl.BlockSpec((tm, 2 * CW), lambda b, s: (row(b, s), 0)),
                   pl.BlockSpec((HALO, CW), lambda b, s: (0, 0))],
        out_shape=[SDS((T, 2 * CW), BF16), SDS((HALO, CW), F32)],
        scratch_shapes=[pltpu.VMEM((tm + HALO, CW), F32), pltpu.VMEM((tm + HALO, CW), F32),
                        pltpu.VMEM((SUBLANES - 1, tm + HALO - SUBLANES, CW), F32),
                        pltpu.VMEM((SUBLANES - 1, tm + HALO - SUBLANES, CW), F32),
                        pltpu.VMEM((tm, CW), F32), pltpu.VMEM((HALO * SUBLANES, CW), F32)],
        compiler_params=_params(("arbitrary", "arbitrary")),
    )(du2, du2, z, z, z, z, conv_w)


def _attn_bwd(qb, ka, va, doa, t, hb):
    B, H, S, _ = qb.shape
    nk = S // t

    def body(q_ref, k_ref, v_ref, do_ref, dq_ref, dk_ref, dv_ref, dv_acc):
        j = pl.program_id(2)

        @pl.when(j == 0)
        def _():
            dq_ref[...] = jnp.zeros_like(dq_ref)

        dk_ref[...] = jnp.zeros_like(dk_ref)
        dv_acc[...] = jnp.zeros_like(dv_acc)

        def step(i, masked):
            q_rows = pl.ds(pl.multiple_of(i * t, t), t)
            for e in range(hb):
                k = k_ref[0, e]
                q = q_ref[0, e, q_rows, :]
                do = do_ref[0, e, q_rows, :]
                p = jnp.exp(_dot_nt(q, k))
                if masked:
                    keep = lax.broadcasted_iota(jnp.int32, (t, t), 0) >= lax.broadcasted_iota(jnp.int32, (t, t), 1)
                    p = jnp.where(keep, p, 0.0)
                ds = (p * _dot_nt(do, v_ref[0, e])).astype(BF16)
                dv_acc[e] = dv_acc[e] + _dot_tn(p.astype(BF16), do)
                dk_ref[0, e] = dk_ref[0, e] + _dot_tn(ds, q)
                dq_ref[0, e, q_rows, :] = dq_ref[0, e, q_rows, :] + _dot(ds, k)

        step(j, True)

        def loop_body(i, carry):
            step(i, False)
            return carry

        lax.fori_loop(j + 1, nk, loop_body, 0)
        dv_ref[0] = dv_acc[...].astype(BF16)

    full = pl.BlockSpec((1, hb, S, LANES), lambda b, h, j: (b, h, 0, 0))
    blk = pl.BlockSpec((1, hb, t, LANES), lambda b, h, j: (b, h, j, 0))
    oshape = SDS((B, H, S, LANES), F32)
    return pl.pallas_call(
        body, name="attn_bwd", grid=(B, H // hb, nk),
        in_specs=[full, blk, blk, full],
        out_specs=[full, blk, blk],
        out_shape=[oshape, oshape, SDS((B, H, S, LANES), BF16)],
        scratch_shapes=[pltpu.VMEM((hb, t, LANES), F32)],
        compiler_params=_params(("parallel", "parallel", "arbitrary")),
    )(qb, ka, va, doa)


def _qk_bwd(dqa, dka, dva, z, gq, gk, B, S, H, tm):
    T = B * S
    FW = H * HEAD_DIM
    nsb = S // tm
    nfb = FW // LANES
    scale = HEAD_DIM ** -0.5

    def body(dq_ref, dk_ref, dv_ref, zq_ref, zk_ref, gq_ref, gk_ref, dzq_ref, dzk_ref, dzv_ref, dc_ref, dg_ref):
        p = pl.program_id(0)

        @pl.when(pl.program_id(1) == 0)
        def _():
            dg_ref[...] = jnp.zeros_like(dg_ref)

        lane = _lane((tm, LANES))
        lo = lane < HEAD_DIM

        def natural(ref):
            return jnp.where(lo, ref[0, 0].astype(F32), pltpu.roll(ref[0, 1].astype(F32), HEAD_DIM, 1))

        def norm_bwd(dn, x, g, row, out_ref):
            r = lax.rsqrt(_half_stats(x * x) * (1.0 / HEAD_DIM) + EPS)
            xh = x * r
            dg_ref[row:row + 1, :] = dg_ref[row:row + 1, :] + jnp.sum(dn * xh, axis=0, keepdims=True)
            dxh = dn * g
            mm = _half_stats(dxh * xh) * (1.0 / HEAD_DIM)
            out_ref[...] = (r * (dxh - xh * mm)).astype(BF16)

        norm_bwd(natural(dq_ref) * scale, zq_ref[...], gq_ref[...], 0, dzq_ref)
        norm_bwd(natural(dk_ref), zk_ref[...], gk_ref[...], 1, dzk_ref)
        dzv_ref[...] = natural(dv_ref).astype(BF16)

        dc = jnp.zeros((tm, LANES), F32)
        for e in range(2):
            val = _lane_col(dq_ref[0, e], lane, L_ROWSUM) - _lane_col(dk_ref[0, e], lane, L_KDECAY)
            dc = jnp.where(lane == 2 * p + e, val, dc)
        dc_ref[0] = dc

    hspec = pl.BlockSpec((1, 2, tm, LANES), lambda p, i: (i // nsb, p, i % nsb, 0))
    zspec = lambda off: pl.BlockSpec((tm, LANES), lambda p, i: (i, off + p))
    gspec = pl.BlockSpec((1, LANES), lambda p, i: (0, p))
    ospec = pl.BlockSpec((tm, LANES), lambda p, i: (i, p))
    return pl.pallas_call(
        body, name="qk_bwd", grid=(H // 2, T // tm),
        in_specs=[hspec, hspec, hspec, zspec(0), zspec(nfb), gspec, gspec],
        out_specs=[ospec, ospec, ospec,
                   pl.BlockSpec((1, tm, LANES), lambda p, i: (p, i, 0)),
                   pl.BlockSpec((8, LANES), lambda p, i: (0, p))],
        out_shape=[SDS((T, FW), BF16), SDS((T, FW), BF16), SDS((T, FW), BF16),
                   SDS((H // 2, T, LANES), F32), SDS((8, FW), F32)],
        compiler_params=_params(("parallel", "arbitrary")),
    )(dqa, dka, dva, z, z, gq, gk)


def _gate_bwd(dc8, z, b_pad, B, S, H, col_blk, fp, tc):
    T = B * S
    nsb = S // tc
    npair = dc8.shape[0]

    def body(dc_ref, zf_ref, b_ref, dz_ref, db_ref, carry):
        first_step = (pl.program_id(0) == 0) & (pl.program_id(1) == 0)

        @pl.when(first_step)
        def _():
            db_ref[...] = jnp.zeros_like(db_ref)

        @pl.when(pl.program_id(1) == 0)
        def _():
            carry[...] = jnp.zeros_like(carry)

        dc = dc_ref[0]
        for k in range(1, npair):
            dc = dc + dc_ref[k]
        dlf = _tri_cumsum(dc, True) + carry[...]
        carry[...] = carry[...] + jnp.sum(dc, axis=0, keepdims=True)
        x = zf_ref[...] + b_ref[...]
        dlogit = dlf * _sigmoid(-x)
        db_ref[0:1, :] = db_ref[0:1, :] + jnp.sum(dlogit, axis=0, keepdims=True)
        dz_ref[...] = jnp.zeros_like(dz_ref)
        dz_ref[:, :LANES] = dlogit.astype(BF16)

    rrow = lambda b, s: b * nsb + (nsb - 1 - s)
    return pl.pallas_call(
        body, name="gate_bwd", grid=(B, nsb),
        in_specs=[pl.BlockSpec((npair, tc, LANES), lambda b, s: (0, rrow(b, s), 0)),
                  pl.BlockSpec((tc, LANES), lambda b, s: (rrow(b, s), col_blk)),
                  pl.BlockSpec((1, LANES), lambda b, s: (0, 0))],
        out_specs=[pl.BlockSpec((tc, fp), lambda b, s: (rrow(b, s), 0)),
                   pl.BlockSpec((8, LANES), lambda b, s: (0, 0))],
        out_shape=[SDS((T, fp), BF16), SDS((8, LANES), F32)],
        scratch_shapes=[pltpu.VMEM((1, LANES), F32)],
        compiler_params=_params(("arbitrary", "arbitrary")),
    )(dc8, z, b_pad)


def _matmul_tn(a, b, name, tmm, tn, tk):
    T, M = a.shape
    N = b.shape[1]
    tmm, tn, tk = min(tmm, M), min(tn, N), min(tk, T)

    def body(a_ref, b_ref, o_ref):
        @pl.when(pl.program_id(2) == 0)
        def _():
            o_ref[...] = jnp.zeros_like(o_ref)

        o_ref[...] = o_ref[...] + _dot_tn(a_ref[...], b_ref[...])

    return pl.pallas_call(
        body, name=name, grid=(M // tmm, N // tn, T // tk),
        in_specs=[pl.BlockSpec((tk, tmm), lambda i, j, k: (k, i)),
                  pl.BlockSpec((tk, tn), lambda i, j, k: (k, j))],
        out_specs=pl.BlockSpec((tmm, tn), lambda i, j, k: (i, j)),
        out_shape=SDS((M, N), F32),
        compiler_params=_params(("parallel", "parallel", "arbitrary")),
    )(a, b)


def _dh_rms_bwd(pieces, w_t, x2, g, dout, tm, tk, parts):
    T, D = x2.shape
    nks = [p.shape[1] // tk for p in pieces]
    starts = [sum(nks[:k]) for k in range(len(pieces))]
    nk = sum(nks)
    ni = T // tm
    n = len(parts)

    def body(*refs):
        dz_refs = refs[:len(pieces)]
        w_ref, x_ref, g_ref, do_ref = refs[len(pieces):len(pieces) + 4]
        part_refs = refs[len(pieces) + 4:len(pieces) + 4 + n]
        gx_ref, dg_ref = refs[len(pieces) + 4 + n:len(pieces) + 6 + n]
        slot_refs = refs[len(pieces) + 6 + n:len(pieces) + 6 + 2 * n]
        acc_ref, send_sems, recv_sems = refs[len(pieces) + 6 + 2 * n:]
        k = pl.program_id(1)
        first_step = (pl.program_id(0) == 0) & (k == 0)
        last_step = (pl.program_id(0) == ni - 1) & (k == nk - 1)
        x, y, c = _place()
        chips = [(1 - x, y), (x, 1 - y), (1 - x, 1 - y)]

        def copy(a, f, to):
            cx, cy = chips[f]
            return pltpu.make_async_remote_copy(
                src_ref=part_refs[a].at[2 * cx + cy], dst_ref=slot_refs[a].at[f],
                send_sem=send_sems.at[a * 3 + f], recv_sem=recv_sems.at[a * 3 + f],
                device_id=to, device_id_type=MESH)

        @pl.when(first_step)
        def _():
            dg_ref[...] = jnp.zeros_like(dg_ref)
            for a in range(n):
                for f in range(3):
                    copy(a, f, (*chips[f], c)).start()

        @pl.when(last_step)
        def _():
            for a in range(n):
                for f in range(3):
                    copy(a, f, (x, y, c)).wait_recv()
            for a in range(n):
                for f in range(3):
                    copy(a, f, (*chips[f], c)).wait_send()

        @pl.when(k == 0)
        def _():
            acc_ref[...] = jnp.zeros_like(acc_ref)

        for dz_ref, st, cnt in zip(dz_refs, starts, nks):
            @pl.when((k >= st) & (k < st + cnt))
            def _(dz_ref=dz_ref):
                acc_ref[...] = acc_ref[...] + _dot(dz_ref[...], w_ref[...])

        @pl.when(k == nk - 1)
        def _():
            x = x_ref[...]
            r = lax.rsqrt(jnp.mean(x * x, axis=-1, keepdims=True) + EPS)
            xh = x * r
            dh = acc_ref[...]
            dg_ref[0:1, :] = dg_ref[0:1, :] + jnp.sum(dh * xh, axis=0, keepdims=True)
            dxn = dh * g_ref[...]
            gx_ref[...] = do_ref[...] + r * (dxn - xh * jnp.mean(dxn * xh, axis=-1, keepdims=True))

    def piece_spec(st, cnt):
        return pl.BlockSpec((tm, tk), lambda i, k: (i, jnp.clip(k - st, 0, cnt - 1)))

    tspec = pl.BlockSpec((tm, D), lambda i, k: (i, 0))
    return pl.pallas_call(
        body, name="dh_rms_bwd", grid=(T // tm, nk),
        in_specs=[piece_spec(st, cnt) for st, cnt in zip(starts, nks)]
        + [pl.BlockSpec((tk, D), lambda i, k: (k, 0)), tspec, pl.BlockSpec((1, D), lambda i, k: (0, 0)), tspec]
        + [ANY] * n,
        out_specs=[tspec, pl.BlockSpec((8, D), lambda i, k: (0, 0))] + [ANY] * n,
        out_shape=[SDS((T, D), F32), SDS((8, D), F32)] + [SDS((3,) + p.shape[1:], p.dtype) for p in parts],
        scratch_shapes=[pltpu.VMEM((tm, D), F32),
                        pltpu.SemaphoreType.DMA((3 * n,)), pltpu.SemaphoreType.DMA((3 * n,))],
        compiler_params=_params(("arbitrary", "arbitrary")),
    )(*pieces, w_t, x2, g, dout, *parts)


def _block_plan(R, C, tr, tc):
    br = min(tr, R)
    if R % br == 0:
        return (br, C), R // br, lambda i: (i, 0)
    bc = min(tc, C)
    assert C % bc == 0
    return (R, bc), C // bc, lambda i: (0, i)


def _ew_call(body, name, ins, n_out, out_dtypes, tr, tc):
    R, C = ins[0].shape
    blk, steps, imap = _block_plan(R, C, tr, tc)
    spec = pl.BlockSpec(blk, imap)
    return pl.pallas_call(
        body, name=name, grid=(steps,),
        in_specs=[spec] * len(ins), out_specs=[spec] * n_out,
        out_shape=[SDS((R, C), dt) for dt in out_dtypes],
        compiler_params=_params(("parallel",)),
    )(*ins)


def _sum_slots(slots, name, first=None, tr=256):
    n, R, C = slots.shape
    blk, steps, imap = _block_plan(R, C, tr, 2 * LANES)
    lead = [] if first is None else [first]

    def body(*refs):
        s_ref, o_ref = refs[-2:]
        acc = refs[0][...].astype(F32) if lead else s_ref[0].astype(F32)
        for k in range(0 if lead else 1, n):
            acc = acc + s_ref[k].astype(F32)
        o_ref[...] = acc

    return pl.pallas_call(
        body, name=name, grid=(steps,),
        in_specs=[pl.BlockSpec(blk, imap)] * len(lead) + [pl.BlockSpec((n,) + blk, lambda i: (0,) + imap(i))],
        out_specs=pl.BlockSpec(blk, imap),
        out_shape=SDS((R, C), F32),
        compiler_params=_params(("parallel",)),
    )(*lead, slots)


def _adamw(w, g, m, v, name):
    def body(w_ref, g_ref, m_ref, v_ref, d_ref, nm_ref, nv_ref):
        gg = g_ref[...]
        nm = ADAM_B1 * m_ref[...] + (1.0 - ADAM_B1) * gg
        nv = ADAM_B2 * v_ref[...] + (1.0 - ADAM_B2) * (gg * gg)
        m_hat = nm / (1.0 - ADAM_B1 ** ADAM_STEP)
        v_hat = nv / (1.0 - ADAM_B2 ** ADAM_STEP)
        d_ref[...] = -ADAM_LR * (m_hat / (jnp.sqrt(v_hat) + ADAM_EPS) + ADAM_WD * w_ref[...])
        nm_ref[...] = nm
        nv_ref[...] = nv

    return _ew_call(body, name, [w, g, m, v], 3, [F32, F32, F32], 128, 2 * LANES)


ANY = pl.BlockSpec(memory_space=pl.ANY)


def _place():
    return lax.axis_index("x"), lax.axis_index("y"), lax.axis_index("c")


def _gather_chips(shards, splits):
    n = len(shards)
    per = 7

    def body(*refs):
        ins, outs = refs[:n], refs[n:2 * n]
        send_sems, recv_sems = refs[2 * n:]
        x, y, c = _place()
        mine = 2 * x + y
        chips = [(1 - x, y), (x, 1 - y), (1 - x, 1 - y)]

        def rows(a, half):
            return pl.ds(0, splits[a]) if half == 0 else pl.ds(splits[a], ins[a].shape[0] - splits[a])

        def copy(a, k, chip_idx, half, to, src=None):
            dst = outs[a].at[chip_idx, rows(a, half)]
            return pltpu.make_async_remote_copy(
                src_ref=dst if src is None else src, dst_ref=dst,
                send_sem=send_sems.at[a * per + k], recv_sem=recv_sems.at[a * per + k],
                device_id=to, device_id_type=MESH)

        def own(a, to):
            return pltpu.make_async_remote_copy(
                src_ref=ins[a], dst_ref=outs[a].at[mine],
                send_sem=send_sems.at[a * per + 6], recv_sem=recv_sems.at[a * per + 6],
                device_id=to, device_id_type=MESH)

        for cc in (0, 1):
            @pl.when(c == cc)
            def _(cc=cc):
                me, sibling = (x, y, cc), (x, y, 1 - cc)
                first = [copy(a, k, mine, cc, (*chip, cc), src=ins[a].at[rows(a, cc)])
                         for a in range(n) for k, chip in enumerate(chips)]
                first += [own(a, sibling) for a in range(n)]
                for cp in first:
                    cp.start()
                passed = []
                for k, (cx, cy) in enumerate(chips):
                    for a in range(n):
                        copy(a, k, 2 * cx + cy, cc, me).wait_recv()
                        fwd = copy(a, 3 + k, 2 * cx + cy, cc, sibling)
                        fwd.start()
                        passed.append(fwd)
                for k, (cx, cy) in enumerate(chips):
                    for a in range(n):
                        copy(a, 3 + k, 2 * cx + cy, 1 - cc, me).wait_recv()
                for a in range(n):
                    own(a, me).wait_recv()
                for cp in first + passed:
                    cp.wait_send()

    return pl.pallas_call(
        body, name="gather_chips",
        in_specs=[ANY] * n, out_specs=[ANY] * n,
        out_shape=[SDS((4,) + s.shape, s.dtype) for s in shards],
        scratch_shapes=[pltpu.SemaphoreType.DMA((per * n,)), pltpu.SemaphoreType.DMA((per * n,))],
    )(*shards)


def _pair_swap(arrs):
    n = len(arrs)

    def body(*refs):
        ins, outs = refs[:n], refs[n:2 * n]
        send_sems, recv_sems = refs[2 * n:]
        x, y, c = _place()
        for cc in (0, 1):
            @pl.when(c == cc)
            def _(cc=cc):
                copies = []
                for a in range(n):
                    half = ins[a].shape[1] // 2
                    copies.append(pltpu.make_async_remote_copy(
                        src_ref=ins[a].at[:, pl.ds((1 - cc) * half, half)], dst_ref=outs[a],
                        send_sem=send_sems.at[a], recv_sem=recv_sems.at[a],
                        device_id=(x, y, 1 - cc), device_id_type=MESH))
                for cp in copies:
                    cp.start()
                for cp in copies:
                    cp.wait()

    return pl.pallas_call(
        body, name="pair_swap",
        in_specs=[ANY] * n, out_specs=[ANY] * n,
        out_shape=[SDS((h.shape[0], h.shape[1] // 2), h.dtype) for h in arrs],
        scratch_shapes=[pltpu.SemaphoreType.DMA((n,)), pltpu.SemaphoreType.DMA((n,))],
    )(*arrs)


def _pair_sum(arrs, got, core, tr):
    half = arrs[0].shape[1] // 2
    cnts = [p.shape[0] // tr for p in arrs]
    starts = [sum(cnts[:k]) for k in range(len(arrs))]

    def body(core_ref, *refs):
        del core_ref
        own_refs, got_refs, o_ref = refs[:len(arrs)], refs[len(arrs):2 * len(arrs)], refs[-1]
        s = pl.program_id(0)
        for own_ref, got_ref, st, cnt in zip(own_refs, got_refs, starts, cnts):
            @pl.when((s >= st) & (s < st + cnt))
            def _(own_ref=own_ref, got_ref=got_ref):
                o_ref[...] = (own_ref[...] + got_ref[...]).astype(BF16)

    def own_spec(st, cnt):
        return pl.BlockSpec((tr, half), lambda s, core_ref: (jnp.clip(s - st, 0, cnt - 1), core_ref[0]))

    def got_spec(st, cnt):
        return pl.BlockSpec((tr, half), lambda s, core_ref: (jnp.clip(s - st, 0, cnt - 1), 0))

    return pl.pallas_call(
        body, name="pair_sum",
        grid_spec=pltpu.PrefetchScalarGridSpec(
            num_scalar_prefetch=1, grid=(sum(cnts),),
            in_specs=[own_spec(st, cnt) for st, cnt in zip(starts, cnts)]
            + [got_spec(st, cnt) for st, cnt in zip(starts, cnts)],
            out_specs=pl.BlockSpec((tr, half), lambda s, core_ref: (s, 0))),
        out_shape=SDS((sum(cnts) * tr, half), BF16),
        compiler_params=_params(("arbitrary",)),
    )(core, *arrs, *got)


def _share_results(arrs, rows):
    n = len(arrs)
    flips = [(fx, fy, fc) for fx in (0, 1) for fy in (0, 1) for fc in (0, 1)][1:]

    def body(*refs):
        ins, rows_ref, outs, all_ref = refs[:n], refs[n], refs[n + 1:2 * n + 1], refs[2 * n + 1]
        send_sems, recv_sems, local_sem = refs[2 * n + 2:]
        x, y, c = _place()
        me = 4 * x + 2 * y + c
        local = pltpu.make_async_copy(rows_ref, all_ref.at[me], local_sem)
        local.start()
        copies = [pltpu.make_async_remote_copy(
            src_ref=ins[a], dst_ref=outs[a], send_sem=send_sems.at[a], recv_sem=recv_sems.at[a],
            device_id=(x, y, 1 - c), device_id_type=MESH) for a in range(n)]
        for k, (fx, fy, fc) in enumerate(flips):
            copies.append(pltpu.make_async_remote_copy(
                src_ref=rows_ref, dst_ref=all_ref.at[me], send_sem=send_sems.at[n + k], recv_sem=recv_sems.at[n + k],
                device_id=(x ^ fx, y ^ fy, c ^ fc), device_id_type=MESH))
        for cp in copies:
            cp.start()
        for cp in copies[:n]:
            cp.wait_recv()
        for k, (fx, fy, fc) in enumerate(flips):
            src = 4 * (x ^ fx) + 2 * (y ^ fy) + (c ^ fc)
            pltpu.make_async_remote_copy(
                src_ref=rows_ref, dst_ref=all_ref.at[src], send_sem=send_sems.at[n + k], recv_sem=recv_sems.at[n + k],
                device_id=(x, y, c), device_id_type=MESH).wait_recv()
        for cp in copies:
            cp.wait_send()
        local.wait()

    outs = pl.pallas_call(
        body, name="share_results",
        in_specs=[ANY] * (n + 1), out_specs=[ANY] * (n + 1),
        out_shape=[SDS(h.shape, h.dtype) for h in arrs] + [SDS((8,) + rows.shape, rows.dtype)],
        scratch_shapes=[pltpu.SemaphoreType.DMA((n + 7,)), pltpu.SemaphoreType.DMA((n + 7,)),
                        pltpu.SemaphoreType.DMA],
    )(*arrs, rows)
    return outs[:n], outs[n]


def _tiles(S, FW):
    big = FW % 512 == 0
    return dict(
        fp=512 if big else LANES,
        tn=1536 if big else LANES,
        tm_in=min(1024, S),
        t_attn=min(512, S),
        hb_fwd=4,
        hb_bwd=2,
        tm_prep=min(512, S),
        tm_mix=min(128, S),
        tc=min(256, S),
        tk=512 if big else LANES,
    )


def kernel(x, norm_g, w_in, b_forget, q_norm_g, k_norm_g, conv_w, conv_b, conv_ln_g, conv_ln_b, w_out, loss_target, m_norm_g, m_w_in, m_b_forget, m_q_norm_g, m_k_norm_g, m_conv_w, m_conv_b, m_conv_ln_g, m_conv_ln_b, m_w_out, v_norm_g, v_w_in, v_b_forget, v_q_norm_g, v_k_norm_g, v_conv_w, v_conv_b, v_conv_ln_g, v_conv_ln_b, v_w_out):
    B, S, D = x.shape
    H, dh = q_norm_g.shape[1:]
    FW = H * dh
    CW = conv_b.shape[-1]
    n_taps, cw_shard = conv_w.shape[1:]
    in_shard = w_in.shape[2]
    out_shard = w_out.shape[1]
    assert dh == HEAD_DIM and H % 2 == 0 and H <= LANES and FW == CW == D
    assert n_taps - 1 <= HALO and 4 * cw_shard == CW and 4 * out_shard == FW + CW
    assert 4 * in_shard == 4 * FW + 3 * CW + H
    T = B * S
    tl = _tiles(S, FW)
    fp = tl["fp"]
    xi, yi, ci = _place()

    w_t = jnp.transpose(w_in[0])
    conv_pad = jnp.pad(conv_w[0], ((0, HALO - n_taps), (0, 0)))
    bf16_rows = 2 * SUBLANES
    g_in, g_out, g_cw = _gather_chips(
        [w_t.astype(BF16), w_out[0].astype(BF16), conv_pad],
        [in_shard // 2 // bf16_rows * bf16_rows, out_shard // 2, HALO // 2])
    w_t_full = g_in.reshape(4 * in_shard, D)
    w_out_full = g_out.reshape(FW + CW, D)
    conv_full = g_cw.transpose(1, 0, 2).reshape(HALO, CW)
    o_f = 3 * FW
    w_pack = jnp.concatenate([w_t_full[:o_f], w_t_full[o_f + H:],
                              jnp.pad(w_t_full[o_f:o_f + H], ((0, fp - H), (0, 0)))], axis=0)
    f_col = 4 * FW + 3 * CW

    x2 = x.reshape(T, D)
    tgt = loss_target.reshape(T, D)
    b_pad = jnp.pad(b_forget, ((0, 0), (0, LANES - H)))
    gq = q_norm_g.reshape(1, FW)
    gk = k_norm_g.reshape(1, FW)

    z, h = _fwd_in(x2, norm_g, w_pack, tl["tm_in"], tl["tn"])
    c = _gate_fwd(z, b_pad, B, S, H, f_col // LANES, tl["tc"])
    qa, ka, va = _attn_prep(z, c, gq, gk, B, S, H, tl["tm_prep"])
    oa = _attn_fwd(qa, ka, va, tl["t_attn"], tl["hb_fwd"])
    y, u2, a_nat, dout, dout_b, dy, loss_acc = _fwd_out(
        oa, z, x2, tgt, conv_full, conv_b, conv_ln_g, conv_ln_b, w_out_full, B, S, H, n_taps, tl["tm_mix"])

    dzgf, dzgc, du2, doa, qb, sg_conv = _bwd_prep(dy, z, a_nat, oa, qa, u2, conv_ln_g, conv_ln_b, B, S, H, tl["tm_mix"])
    dzglu, dconv_w = _conv_bwd(du2, z, conv_full, B, S, n_taps, tl["tm_mix"])
    dqa, dka, dva = _attn_bwd(qb, ka, va, doa, tl["t_attn"], tl["hb_bwd"])
    dzq, dzk, dzv, dc8, dg_qk = _qk_bwd(dqa, dka, dva, z, gq, gk, B, S, H, tl["tm_prep"])
    dzf, db_f = _gate_bwd(dc8, z, b_pad, B, S, H, f_col // LANES, fp, tl["tc"])
    pieces = [dzq, dzk, dzv, dzgf, dzglu, dzgc, dzf]
    dw_all = [_matmul_tn(p, h, f"dw_in_{k}", 1024, 1024, 1024) for k, p in enumerate(pieces)]
    dw_all.append(_matmul_tn(y, dout_b, "dw_out", 1024, 1024, 1024))

    summed = _pair_sum(dw_all, _pair_swap(dw_all), ci.astype(jnp.int32).reshape(1), fp)
    ends = [0]
    for t in dw_all:
        ends.append(ends[-1] + t.shape[0])
    cut = lambda k, rows=None: summed[ends[k]:(ends[k + 1] if rows is None else ends[k] + rows)]
    part_in = jnp.concatenate([cut(0), cut(1), cut(2), cut(6, H), cut(3), cut(4), cut(5)],
                              axis=0).reshape(4, in_shard, D // 2)
    part_out = cut(7).reshape(4, out_shard, D // 2)
    grad_x2, dg_norm, slots_in, slots_out = _dh_rms_bwd(
        pieces, w_pack, x2, norm_g, dout, tl["tm_in"], tl["tk"], [part_in, part_out])
    chip = 2 * xi + yi
    half_in = _sum_slots(slots_in, "chip_sum_in", lax.dynamic_index_in_dim(part_in, chip, 0, keepdims=False))
    half_out = _sum_slots(slots_out, "chip_sum_out", lax.dynamic_index_in_dim(part_out, chip, 0, keepdims=False))
    lanes_to_d = lambda t: jnp.pad(t, ((0, 0), (0, D - LANES)))
    small = jnp.concatenate([
        dg_norm[0:1], lanes_to_d(db_f[0:1, :]), dg_qk[0:1], dg_qk[1:2],
        sg_conv[2:3], sg_conv[0:1], sg_conv[1:2], dconv_w, lanes_to_d(loss_acc[0:1, :])], axis=0)
    n_small = small.shape[0]
    (other_in, other_out), all_small = _share_results([half_in, half_out], small)

    def both_halves(mine, other, axis):
        return jnp.where(ci == 0, jnp.concatenate([mine, other], axis=axis), jnp.concatenate([other, mine], axis=axis))

    grad_w_t = both_halves(half_in, other_in, 1)
    grad_w_out = both_halves(half_out, other_out, 1)

    small_sum = _sum_slots(all_small, "small_sum", tr=n_small)
    loss = 0.5 * small_sum[n_small - 1, 0] / D
    grad_norm_g, grad_b_f = small_sum[0:1], small_sum[1:2, :H]
    grad_gq, grad_gk = small_sum[2:3].reshape(1, H, dh), small_sum[3:4].reshape(1, H, dh)
    grad_conv_b, grad_ln_g, grad_ln_b = small_sum[4:5], small_sum[5:6], small_sum[6:7]
    grad_conv_w = lax.dynamic_slice_in_dim(small_sum[7:7 + n_taps], chip * cw_shard, cw_shard, axis=1)

    d_t, nm_t, nv_t = _adamw(w_t, grad_w_t, jnp.transpose(m_w_in[0]), jnp.transpose(v_w_in[0]), "adamw_in")
    grad_w_in, d_in, nm_in, nv_in = (jnp.transpose(t)[None] for t in (grad_w_t, d_t, nm_t, nv_t))
    d_out, nm_out, nv_out = (t[None] for t in _adamw(w_out[0], grad_w_out, m_w_out[0], v_w_out[0], "adamw_out"))
    d_cw, nm_cw, nv_cw = (t[None] for t in _adamw(conv_w[0], grad_conv_w, m_conv_w[0], v_conv_w[0], "adamw_conv_w"))

    def rows(ws):
        return jnp.concatenate([jnp.pad(t.reshape(1, -1), ((0, 0), (0, D - t.size))) for t in ws], axis=0)

    small_w = [norm_g, b_forget, q_norm_g, k_norm_g, conv_b, conv_ln_g, conv_ln_b]
    small_m = [m_norm_g, m_b_forget, m_q_norm_g, m_k_norm_g, m_conv_b, m_conv_ln_g, m_conv_ln_b]
    small_v = [v_norm_g, v_b_forget, v_q_norm_g, v_k_norm_g, v_conv_b, v_conv_ln_g, v_conv_ln_b]
    d_s, nm_s, nv_s = _adamw(rows(small_w), small_sum[0:7], rows(small_m), rows(small_v), "adamw_small")

    def unpack(t):
        return [t[k:k + 1, :w.size].reshape(w.shape) for k, w in enumerate(small_w)]

    def order(s, in_, cw, out_):
        ng, bf, qg, kg, cb, lg, lb = s
        return [ng, in_, bf, qg, kg, cw, cb, lg, lb, out_]

    grads = [grad_norm_g, grad_w_in, grad_b_f, grad_gq, grad_gk, grad_conv_w[None],
             grad_conv_b, grad_ln_g, grad_ln_b, grad_w_out[None]]
    return (loss, grad_x2.reshape(B, S, D), *grads,
            *order(unpack(d_s), d_in, d_cw, d_out),
            *order(unpack(nm_s), nm_in, nm_cw, nm_out),
            *order(unpack(nv_s), nv_in, nv_cw, nv_out))
```

```python
import jax
import jax.numpy as jnp
from jax import lax
from jax.experimental import pallas as pl
from jax.experimental.pallas import tpu as pltpu

F32 = jnp.float32
BF16 = jnp.bfloat16
SDS = jax.ShapeDtypeStruct
MESH = pl.DeviceIdType.MESH

EPS = 1e-6
NEG_INF = -1e30
LANES = 128
SUBLANES = 8
HEAD_DIM = 64
HALO = 32
VMEM_LIMIT = 56 * 1024 * 1024

L_ROWSUM = 64
L_KDECAY = 67
L_LSE = 70
L_D = 65
L_QNORM = 73
L_KNORM = 74
NORM_SLACK = 1.02
SHIFT_MAX = 40.0

ADAM_LR = 0.001
ADAM_B1 = 0.9
ADAM_B2 = 0.999
ADAM_EPS = 1e-08
ADAM_WD = 0.01
ADAM_STEP = 10


def _params(sem, vmem=VMEM_LIMIT):
    return pltpu.CompilerParams(dimension_semantics=sem, vmem_limit_bytes=vmem)


def _sigmoid(x):
    return 1.0 / (1.0 + jnp.exp(-x))


def _split3(x):
    hi = x.astype(BF16).astype(F32)
    r = x - hi
    mid = r.astype(BF16).astype(F32)
    lo = (r - mid).astype(BF16).astype(F32)
    return hi, mid, lo


def _dot(a, b):
    return jnp.dot(a, b, preferred_element_type=F32)


def _dot_nt(a, b):
    return lax.dot_general(a, b, (((1,), (1,)), ((), ())), preferred_element_type=F32)


def _dot_tn(a, b):
    return lax.dot_general(a, b, (((0,), (0,)), ((), ())), preferred_element_type=F32)


def _lane(shape):
    return lax.broadcasted_iota(jnp.int32, shape, 1)


def _lane_col(x, lane, idx):
    return jnp.sum(jnp.where(lane == idx, x, 0.0), axis=-1, keepdims=True)


def _put3(base, lane, start, pieces):
    out = base
    for k, p in enumerate(pieces):
        out = jnp.where(lane == start + k, p, out)
    return out


def _half_stats(t):
    hi = t.astype(BF16)
    mid = (t - hi.astype(F32)).astype(BF16)
    row = lax.broadcasted_iota(jnp.int32, (2 * LANES, LANES), 0)
    col = lax.broadcasted_iota(jnp.int32, (2 * LANES, LANES), 1)
    same_half = (jnp.bitwise_and(row, LANES - 1) < HEAD_DIM) == (col < HEAD_DIM)
    return _dot(jnp.concatenate([hi, mid], axis=1), jnp.where(same_half, 1.0, 0.0).astype(BF16))


def _fwd_in(x2, g, w_t, tm, tn):
    T, D = x2.shape
    N = w_t.shape[0]

    def body(x_ref, g_ref, w_ref, z_ref, h_ref):
        @pl.when(pl.program_id(1) == 0)
        def _():
            x = x_ref[...]
            r = lax.rsqrt(jnp.mean(x * x, axis=-1, keepdims=True) + EPS)
            h_ref[...] = (x * r * g_ref[...]).astype(BF16)

        z_ref[...] = _dot_nt(h_ref[...], w_ref[...])

    return pl.pallas_call(
        body, name="fwd_in", grid=(T // tm, N // tn),
        in_specs=[pl.BlockSpec((tm, D), lambda i, j: (i, 0)),
                  pl.BlockSpec((1, D), lambda i, j: (0, 0)),
                  pl.BlockSpec((tn, D), lambda i, j: (j, 0))],
        out_specs=[pl.BlockSpec((tm, tn), lambda i, j: (i, j)),
                   pl.BlockSpec((tm, D), lambda i, j: (i, 0))],
        out_shape=[SDS((T, N), F32), SDS((T, D), BF16)],
        compiler_params=_params(("parallel", "arbitrary")),
    )(x2, g, w_t)


def _tri_cumsum(x, reverse):
    t = x.shape[0]
    row = lax.broadcasted_iota(jnp.int32, (t, t), 0)
    col = lax.broadcasted_iota(jnp.int32, (t, t), 1)
    tri = (row <= col) if reverse else (row >= col)
    tri = jnp.where(tri, 1.0, 0.0).astype(BF16)
    hi, mid, lo = _split3(x)
    return _dot(tri, hi.astype(BF16)) + _dot(tri, mid.astype(BF16)) + _dot(tri, lo.astype(BF16))


def _gate_fwd(z, b_pad, B, S, H, col_blk, tc):
    T = B * S
    nsb = S // tc

    def body(zf_ref, b_ref, c_ref, carry):
        @pl.when(pl.program_id(1) == 0)
        def _():
            carry[...] = jnp.zeros_like(carry)

        x = zf_ref[...] + b_ref[...]
        lf = jnp.minimum(x, 0.0) - jnp.log(1.0 + jnp.exp(-jnp.abs(x)))
        lf = jnp.where(_lane(lf.shape) < H, lf, 0.0)
        c_ref[...] = _tri_cumsum(lf, False) + carry[...]
        carry[...] = carry[...] + jnp.sum(lf, axis=0, keepdims=True)

    return pl.pallas_call(
        body, name="gate_fwd", grid=(B, nsb),
        in_specs=[pl.BlockSpec((tc, LANES), lambda b, s: (b * nsb + s, col_blk)),
                  pl.BlockSpec((1, LANES), lambda b, s: (0, 0))],
        out_specs=pl.BlockSpec((tc, LANES), lambda b, s: (b * nsb + s, 0)),
        out_shape=SDS((T, LANES), F32),
        scratch_shapes=[pltpu.VMEM((1, LANES), F32)],
        compiler_params=_params(("parallel", "arbitrary")),
    )(z, b_pad)


def _qk_normalize(x, g):
    r = lax.rsqrt(_half_stats(x * x) * (1.0 / HEAD_DIM) + EPS)
    return x * r * g


def _head_norms(x):
    own = jnp.sqrt(_half_stats(x * x)) * NORM_SLACK
    return [pltpu.roll(own, HEAD_DIM, 1), own]


def _attn_prep(z, c, gq, gk, B, S, H, tm):
    T = B * S
    FW = H * HEAD_DIM
    nsb = S // tm
    nfb = FW // LANES
    scale = HEAD_DIM ** -0.5

    def body(zq_ref, zk_ref, zv_ref, c_ref, gq_ref, gk_ref, qa_ref, ka_ref, va_ref):
        p = pl.program_id(1)
        lane = _lane((tm, LANES))
        lo = lane < HEAD_DIM
        qn = _qk_normalize(zq_ref[...], gq_ref[...]) * scale
        kn = _qk_normalize(zk_ref[...], gk_ref[...])
        v = zv_ref[...]
        cc = c_ref[...]
        ones_q = ((lane >= L_KDECAY) & (lane < L_KDECAY + 3)).astype(F32)
        ones_k = (((lane >= L_ROWSUM) & (lane < L_ROWSUM + 3)) | ((lane >= L_LSE) & (lane < L_LSE + 3))).astype(F32)
        ones_v = ((lane >= L_ROWSUM) & (lane < L_D + 3)).astype(F32)
        q_norms, k_norms = _head_norms(qn), _head_norms(kn)
        for e in range(2):
            if e == 0:
                qe, ke, ve = qn, kn, v
            else:
                qe, ke, ve = (pltpu.roll(t, HEAD_DIM, 1) for t in (qn, kn, v))
            ch = _lane_col(cc, lane, 2 * p + e)
            pieces = _split3(ch)
            qa = jnp.where(lo, qe, _put3(ones_q, lane, L_ROWSUM, pieces))
            qa = jnp.where(lane == L_QNORM, q_norms[e], qa)
            ka = jnp.where(lo, ke, _put3(ones_k, lane, L_KDECAY, [-t for t in pieces]))
            ka = jnp.where(lane == L_KNORM, k_norms[e], ka)
            va = jnp.where(lo, ve, ones_v)
            qa_ref[0, e] = qa.astype(BF16)
            ka_ref[0, e] = ka.astype(BF16)
            va_ref[0, e] = va.astype(BF16)

    zspec = lambda off: pl.BlockSpec((tm, LANES), lambda i, p: (i, off + p))
    gspec = pl.BlockSpec((1, LANES), lambda i, p: (0, p))
    ospec = pl.BlockSpec((1, 2, tm, LANES), lambda i, p: (i // nsb, p, i % nsb, 0))
    oshape = SDS((B, H, S, LANES), BF16)
    return pl.pallas_call(
        body, name="attn_prep", grid=(T // tm, H // 2),
        in_specs=[zspec(0), zspec(nfb), zspec(2 * nfb),
                  pl.BlockSpec((tm, LANES), lambda i, p: (i, 0)), gspec, gspec],
        out_specs=[ospec, ospec, ospec],
        out_shape=[oshape, oshape, oshape],
        compiler_params=_params(("parallel", "arbitrary")),
    )(z, z, z, c, gq, gk)


def _attn_fwd(qa, ka, va, t, hb):
    B, H, S, _ = qa.shape
    nq = S // t

    def body(q_ref, k_ref, v_ref, o_ref, m_ref, acc_ref, kmax_ref, qs_ref):
        i = pl.program_id(2)
        lane = _lane((t, LANES))

        @pl.when(i == 0)
        def _():
            for e in range(hb):
                norms = jnp.where(_lane((S, LANES)) == L_KNORM, k_ref[0, e].astype(F32), 0.0)
                kmax_ref[e] = jnp.full((1, LANES), jnp.max(norms), F32)

        shifts = [_lane_col(q_ref[0, e].astype(F32), lane, L_QNORM) * kmax_ref[e][:, 0:1] for e in range(hb)]
        worst = shifts[0]
        for e in range(1, hb):
            worst = jnp.maximum(worst, shifts[e])
        bounded = jnp.max(worst) <= SHIFT_MAX
        acc_ref[...] = jnp.zeros_like(acc_ref)

        def tiles(step):
            def loop_body(j, carry):
                step(j, False)
                return carry

            lax.fori_loop(0, i, loop_body, 0)
            step(i, True)

        def keep_mask(n=t):
            return lax.broadcasted_iota(jnp.int32, (n, n), 0) >= lax.broadcasted_iota(jnp.int32, (n, n), 1)

        def finish(e, shift):
            acc = acc_ref[e]
            l = _lane_col(acc, lane, L_ROWSUM)
            o_ref[0, e] = jnp.where(lane < HEAD_DIM, acc / l, shift + jnp.log(l))

        @pl.when(bounded)
        def _():
            for e in range(hb):
                qs_ref[e] = _put3(q_ref[0, e].astype(F32), lane, L_LSE, _split3(-shifts[e])).astype(BF16)

            def pair(e, q_rows, k_start, n, masked):
                k_rows = pl.ds(pl.multiple_of(k_start, n), n)
                p = jnp.exp(_dot_nt(qs_ref[e, q_rows, :], k_ref[0, e, k_rows, :]))
                if masked:
                    p = jnp.where(keep_mask(n), p, 0.0)
                acc_ref[e, q_rows, :] = acc_ref[e, q_rows, :] + _dot(p.astype(BF16), v_ref[0, e, k_rows, :])

            def step(j, masked):
                for e in range(hb):
                    if masked:
                        h = t // 2
                        pair(e, slice(0, h), j * t, h, True)
                        pair(e, slice(h, t), j * t, h, False)
                        pair(e, slice(h, t), j * t + h, h, True)
                    else:
                        pair(e, slice(0, t), j * t, t, False)

            tiles(step)
            for e in range(hb):
                finish(e, shifts[e])

        @pl.when(jnp.logical_not(bounded))
        def _():
            m_ref[...] = jnp.full_like(m_ref, NEG_INF)

            def step(j, masked):
                rows = pl.ds(pl.multiple_of(j * t, t), t)
                for e in range(hb):
                    s = _dot_nt(q_ref[0, e], k_ref[0, e, rows, :])
                    if masked:
                        s = jnp.where(keep_mask(), s, NEG_INF)
                    m_prev = m_ref[e]
                    m_new = jnp.maximum(m_prev, jnp.max(s, axis=-1, keepdims=True))
                    alpha = jnp.exp(m_prev - m_new)
                    p = jnp.exp(s - m_new).astype(BF16)
                    acc_ref[e] = alpha * acc_ref[e] + _dot(p, v_ref[0, e, rows, :])
                    m_ref[e] = m_new

            tiles(step)
            for e in range(hb):
                finish(e, m_ref[e])

    return pl.pallas_call(
        body, name="attn_fwd", grid=(B, H // hb, nq),
        in_specs=[pl.BlockSpec((1, hb, t, LANES), lambda b, h, i: (b, h, i, 0)),
                  pl.BlockSpec((1, hb, S, LANES), lambda b, h, i: (b, h, 0, 0)),
                  pl.BlockSpec((1, hb, S, LANES), lambda b, h, i: (b, h, 0, 0))],
        out_specs=pl.BlockSpec((1, hb, t, LANES), lambda b, h, i: (b, h, i, 0)),
        out_shape=SDS((B, H, S, LANES), F32),
        scratch_shapes=[pltpu.VMEM((hb, t, 1), F32), pltpu.VMEM((hb, t, LANES), F32),
                        pltpu.VMEM((hb, 1, LANES), F32), pltpu.VMEM((hb, t, LANES), BF16)],
        compiler_params=_params(("parallel", "parallel", "arbitrary")),
    )(qa, ka, va)


def _fill_shifts(ext_ref, sh_ref):
    rows = sh_ref.shape[1]
    for b in range(1, SUBLANES):
        sh_ref[b - 1] = ext_ref[pl.ds(b, rows), :]


def _tap_window(ext_ref, sh_ref, off, tm, cols):
    b = off % SUBLANES
    if b == 0:
        return ext_ref[pl.ds(off, tm), cols]
    return sh_ref[b - 1, pl.ds(off - b, tm), cols]


def _conv_taps(w_ref, ext_ref, sh_ref, out_ref, n_taps, tm, offset_of, bias_ref=None):
    for cc in range(out_ref.shape[1] // LANES):
        cols = slice(cc * LANES, (cc + 1) * LANES)
        acc = None
        for j in sorted(range(n_taps), key=offset_of):
            term = w_ref[j:j + 1, cols] * _tap_window(ext_ref, sh_ref, offset_of(j), tm, cols)
            acc = term if acc is None else acc + term
        out_ref[:, cols] = acc if bias_ref is None else acc + bias_ref[:, cols]


def _layernorm_stats(u2):
    mu = jnp.mean(u2, axis=-1, keepdims=True)
    xc = u2 - mu
    rstd = lax.rsqrt(jnp.mean(xc * xc, axis=-1, keepdims=True) + EPS)
    return xc * rstd, rstd


def _fwd_out(oa, z, x2, tgt, conv_w, conv_b, ln_g, ln_b, w_out, B, S, H, n_taps, tm):
    T, D = x2.shape
    FW = H * HEAD_DIM
    CW = conv_w.shape[1]
    nsb = S // tm
    hb = tm // HALO
    mb = 4 if nsb % 4 == 0 else 1
    mt = mb * tm

    def body(oa_ref, gf_ref, ga_ref, gb_ref, gc_ref, ha_ref, hb_ref, x_ref, t_ref, w_ref, cb_ref, lg_ref,
             lb_ref, wo_ref, y_ref, u2_ref, a_ref, do_ref, dob_ref, dy_ref, loss_ref, ext_ref, sh_ref):
        first_step = (pl.program_id(0) == 0) & (pl.program_id(1) == 0)
        sub = lax.rem(pl.program_id(1), mb)
        rows = pl.ds(pl.multiple_of(sub * tm, tm), tm)

        @pl.when(first_step)
        def _():
            loss_ref[...] = jnp.zeros_like(loss_ref)

        u1 = ga_ref[...] * _sigmoid(gb_ref[...])
        halo = ha_ref[...] * _sigmoid(hb_ref[...])
        ext_ref[0:HALO, :] = jnp.where(pl.program_id(1) > 0, halo, 0.0)
        ext_ref[HALO:, :] = u1
        _fill_shifts(ext_ref, sh_ref)
        _conv_taps(w_ref, ext_ref, sh_ref, u2_ref, n_taps, tm, lambda j: HALO - (n_taps - 1) + j, cb_ref)
        uh, _ = _layernorm_stats(u2_ref[...])
        u3 = uh * lg_ref[...] + lb_ref[...]
        gc = gc_ref[...]
        yu = u3 * _sigmoid(u3) * (gc * _sigmoid(gc))
        y_ref[rows, FW:] = yu.astype(BF16)

        lane = _lane((tm, LANES))
        lo = lane < HEAD_DIM
        for p in range(H // 2):
            a_ref[:, p * LANES:(p + 1) * LANES] = jnp.where(
                lo, oa_ref[0, 2 * p], pltpu.roll(oa_ref[0, 2 * p + 1], HEAD_DIM, 1))
        gf = gf_ref[...]
        y_ref[rows, :FW] = (a_ref[...] * (gf * _sigmoid(gf))).astype(BF16)

        @pl.when(sub == mb - 1)
        def _():
            out = x_ref[...] + _dot(y_ref[...], wo_ref[...])
            diff = out - t_ref[...]
            loss_ref[...] = loss_ref[...] + jnp.sum(diff * diff)
            dout = diff * (1.0 / D)
            do_ref[...] = dout
            dob = dout.astype(BF16)
            dob_ref[...] = dob
            dy_ref[...] = _dot_nt(dob, wo_ref[...])

    row = lambda b, s: b * nsb + s
    zspec = lambda cb: pl.BlockSpec((tm, FW), lambda b, s: (row(b, s), cb))
    hspec = lambda cb: pl.BlockSpec((HALO, CW), lambda b, s: (jnp.maximum(row(b, s) * hb - 1, 0), cb))
    vspec = pl.BlockSpec((1, CW), lambda b, s: (0, 0))
    tspec = lambda w: pl.BlockSpec((tm, w), lambda b, s: (row(b, s), 0))
    mspec = lambda w: pl.BlockSpec((mt, w), lambda b, s: (row(b, s) // mb, 0))
    return pl.pallas_call(
        body, name="fwd_out", grid=(B, nsb),
        in_specs=[pl.BlockSpec((1, H, tm, LANES), lambda b, s: (b, 0, s, 0)),
                  zspec(3), zspec(4), zspec(5), zspec(6), hspec(4), hspec(5),
                  mspec(D), mspec(D),
                  pl.BlockSpec((HALO, CW), lambda b, s: (0, 0)), vspec, vspec, vspec,
                  pl.BlockSpec((FW + CW, D), lambda b, s: (0, 0))],
        out_specs=[mspec(FW + CW), tspec(CW), tspec(FW), mspec(D), mspec(D), mspec(FW + CW),
                   pl.BlockSpec((8, LANES), lambda b, s: (0, 0))],
        out_shape=[SDS((T, FW + CW), BF16), SDS((T, CW), F32), SDS((T, FW), F32), SDS((T, D), F32),
                   SDS((T, D), BF16), SDS((T, FW + CW), F32), SDS((8, LANES), F32)],
        scratch_shapes=[pltpu.VMEM((tm + HALO, CW), F32),
                        pltpu.VMEM((SUBLANES - 1, tm + HALO - SUBLANES, CW), F32)],
        compiler_params=_params(("arbitrary", "arbitrary")),
    )(oa, z, z, z, z, z, z, x2, tgt, conv_w, conv_b, ln_g, ln_b, w_out)


def _bwd_prep(dy, z, a_nat, oa, qa, u2, ln_g, ln_b, B, S, H, tm):
    T = B * S
    FW = H * HEAD_DIM
    CW = u2.shape[1]
    nsb = S // tm

    def body(dya_ref, dyu_ref, gf_ref, gc_ref, a_ref, oa_ref, qa_ref, u2_ref, lg_ref, lb_ref,
             dzgf_ref, dzgc_ref, du2_ref, doa_ref, qb_ref, sg_ref):
        first_step = (pl.program_id(0) == 0) & (pl.program_id(1) == 0)

        @pl.when(first_step)
        def _():
            sg_ref[...] = jnp.zeros_like(sg_ref)

        gf = gf_ref[...]
        sg = _sigmoid(gf)
        a = a_ref[...]
        dya = dya_ref[...]
        da = dya * (gf * sg)
        dzgf_ref[...] = (dya * a * (sg * (1.0 + gf * (1.0 - sg)))).astype(BF16)
        dd = da * a
        lane = _lane((tm, LANES))
        lo = lane < HEAD_DIM
        for p in range(H // 2):
            cols = slice(p * LANES, (p + 1) * LANES)
            da_p = da[:, cols]
            dd_p = dd[:, cols]
            d_heads = (jnp.sum(jnp.where(lo, dd_p, 0.0), axis=-1, keepdims=True),
                       jnp.sum(jnp.where(lo, 0.0, dd_p), axis=-1, keepdims=True))
            for e in range(2):
                da_e = da_p if e == 0 else pltpu.roll(da_p, HEAD_DIM, 1)
                d_e = d_heads[e]
                aug = _put3(jnp.zeros((tm, LANES), F32), lane, L_D, _split3(-d_e))
                doa_ref[0, 2 * p + e] = jnp.where(lo, da_e, aug).astype(BF16)
                lse = _lane_col(oa_ref[0, 2 * p + e], lane, L_ROWSUM)
                qb = _put3(qa_ref[0, 2 * p + e].astype(F32), lane, L_LSE, _split3(-lse))
                qb_ref[0, 2 * p + e] = qb.astype(BF16)

        gc = gc_ref[...]
        sc = _sigmoid(gc)
        dyu = dyu_ref[...]
        uh, rstd = _layernorm_stats(u2_ref[...])
        u3 = uh * lg_ref[...] + lb_ref[...]
        s3 = _sigmoid(u3)
        dzgc_ref[...] = (dyu * (u3 * s3) * (sc * (1.0 + gc * (1.0 - sc)))).astype(BF16)
        du3 = dyu * (gc * sc) * (s3 * (1.0 + u3 * (1.0 - s3)))
        sg_ref[0:1, :] = sg_ref[0:1, :] + jnp.sum(du3 * uh, axis=0, keepdims=True)
        sg_ref[1:2, :] = sg_ref[1:2, :] + jnp.sum(du3, axis=0, keepdims=True)
        duh = du3 * lg_ref[...]
        du2 = rstd * (duh - jnp.mean(duh, axis=-1, keepdims=True)
                      - uh * jnp.mean(duh * uh, axis=-1, keepdims=True))
        sg_ref[2:3, :] = sg_ref[2:3, :] + jnp.sum(du2, axis=0, keepdims=True)
        du2_ref[...] = du2

    row = lambda b, s: b * nsb + s
    tspec = lambda w, cb=0: pl.BlockSpec((tm, w), lambda b, s: (row(b, s), cb))
    hspec = pl.BlockSpec((1, H, tm, LANES), lambda b, s: (b, 0, s, 0))
    vspec = pl.BlockSpec((1, CW), lambda b, s: (0, 0))
    return pl.pallas_call(
        body, name="bwd_prep", grid=(B, nsb),
        in_specs=[tspec(FW, 0), tspec(CW, 1), tspec(FW, 3), tspec(CW, 6), tspec(FW), hspec, hspec,
                  tspec(CW), vspec, vspec],
        out_specs=[tspec(FW), tspec(CW), tspec(CW), hspec, hspec,
                   pl.BlockSpec((8, CW), lambda b, s: (0, 0))],
        out_shape=[SDS((T, FW), BF16), SDS((T, CW), BF16), SDS((T, CW), F32),
                   SDS((B, H, S, LANES), BF16), SDS((B, H, S, LANES), BF16), SDS((8, CW), F32)],
        compiler_params=_params(("arbitrary", "arbitrary")),
    )(dy, dy, z, z, a_nat, oa, qa, u2, ln_g, ln_b)


def _conv_bwd(du2, z, conv_w, B, S, n_taps, tm):
    T, CW = du2.shape
    nsb = S // tm
    hb = tm // HALO

    def body(d_ref, dh_ref, ga_ref, gb_ref, ha_ref, hb_ref, w_ref, dz_ref, dw_ref,
             extu_ref, extd_ref, shu_ref, shd_ref, du1_ref, dwacc_ref):
        s = pl.program_id(1)
        first_step = (pl.program_id(0) == 0) & (s == 0)
        last_step = (pl.program_id(0) == B - 1) & (s == nsb - 1)

        @pl.when(first_step)
        def _():
            dwacc_ref[...] = jnp.zeros_like(dwacc_ref)

        ga = ga_ref[...]
        sb = _sigmoid(gb_ref[...])
        halo = ha_ref[...] * _sigmoid(hb_ref[...])
        extu_ref[0:HALO, :] = jnp.where(s > 0, halo, 0.0)
        extu_ref[HALO:, :] = ga * sb
        extd_ref[0:tm, :] = d_ref[...]
        extd_ref[tm:, :] = jnp.where(s < nsb - 1, dh_ref[...], 0.0)
        _fill_shifts(extu_ref, shu_ref)
        _fill_shifts(extd_ref, shd_ref)
        _conv_taps(w_ref, extd_ref, shd_ref, du1_ref, n_taps, tm, lambda j: n_taps - 1 - j)
        for cc in range(CW // LANES):
            cols = slice(cc * LANES, (cc + 1) * LANES)
            parts = [None] * n_taps
            for r in range(tm // SUBLANES):
                dv = d_ref[r * SUBLANES:(r + 1) * SUBLANES, cols]
                for j in range(n_taps):
                    off = HALO - (n_taps - 1) + j + r * SUBLANES
                    term = dv * _tap_window(extu_ref, shu_ref, off, SUBLANES, cols)
                    parts[j] = term if parts[j] is None else parts[j] + term
            for j in range(n_taps):
                rows = slice(j * SUBLANES, (j + 1) * SUBLANES)
                dwacc_ref[rows, cols] = dwacc_ref[rows, cols] + parts[j]
        du1 = du1_ref[...]
        dz_ref[:, :CW] = (du1 * sb).astype(BF16)
        dz_ref[:, CW:] = (du1 * ga * (sb * (1.0 - sb))).astype(BF16)

        @pl.when(last_step)
        def _():
            dw_ref[...] = jnp.zeros_like(dw_ref)
            for j in range(n_taps):
                dw_ref[j:j + 1, :] = jnp.sum(dwacc_ref[j * SUBLANES:(j + 1) * SUBLANES, :], axis=0, keepdims=True)

    row = lambda b, s: b * nsb + s
    last_halo = T // HALO - 1
    return pl.pallas_call(
        body, name="conv_bwd", grid=(B, nsb),
        in_specs=[pl.BlockSpec((tm, CW), lambda b, s: (row(b, s), 0)),
                  pl.BlockSpec((HALO, CW), lambda b, s: (jnp.minimum((row(b, s) + 1) * hb, last_halo), 0)),
                  pl.BlockSpec((tm, CW), lambda b, s: (row(b, s), 4)),
                  pl.BlockSpec((tm, CW), lambda b, s: (row(b, s), 5)),
                  pl.BlockSpec((HALO, CW), lambda b, s: (jnp.maximum(row(b, s) * hb - 1, 0), 4)),
                  pl.BlockSpec((HALO, CW), lambda b, s: (jnp.maximum(row(b, s) * hb - 1, 0), 5)),
                  pl.BlockSpec((HALO, CW), lambda b, s: (0, 0))],
        out_specs=[pl.BlockSpec((tm, 2 * CW), lambda b, s: (row(b, s), 0)),
                   pl.BlockSpec((HALO, CW), lambda b, s: (0, 0))],
        out_shape=[SDS((T, 2 * CW), BF16), SDS((HALO, CW), F32)],
        scratch_shapes=[pltpu.VMEM((tm + HALO, CW), F32), pltpu.VMEM((tm + HALO, CW), F32),
                        pltpu.VMEM((SUBLANES - 1, tm + HALO - SUBLANES, CW), F32),
                        pltpu.VMEM((SUBLANES - 1, tm + HALO - SUBLANES, CW), F32),
                        pltpu.VMEM((tm, CW), F32), pltpu.VMEM((HALO * SUBLANES, CW), F32)],
        compiler_params=_params(("arbitrary", "arbitrary")),
    )(du2, du2, z, z, z, z, conv_w)


def _attn_bwd(qb, ka, va, doa, t, hb):
    B, H, S, _ = qb.shape
    nk = S // t

    def body(q_ref, k_ref, v_ref, do_ref, dq_ref, dk_ref, dv_ref, dv_acc):
        j = pl.program_id(2)

        @pl.when(j == 0)
        def _():
            dq_ref[...] = jnp.zeros_like(dq_ref)

        dk_ref[...] = jnp.zeros_like(dk_ref)
        dv_acc[...] = jnp.zeros_like(dv_acc)

        def step(i, masked):
            q_rows = pl.ds(pl.multiple_of(i * t, t), t)
            for e in range(hb):
                k = k_ref[0, e]
                q = q_ref[0, e, q_rows, :]
                do = do_ref[0, e, q_rows, :]
                p = jnp.exp(_dot_nt(q, k))
                if masked:
                    keep = lax.broadcasted_iota(jnp.int32, (t, t), 0) >= lax.broadcasted_iota(jnp.int32, (t, t), 1)
                    p = jnp.where(keep, p, 0.0)
                ds = (p * _dot_nt(do, v_ref[0, e])).astype(BF16)
                dv_acc[e] = dv_acc[e] + _dot_tn(p.astype(BF16), do)
                dk_ref[0, e] = dk_ref[0, e] + _dot_tn(ds, q)
                dq_ref[0, e, q_rows, :] = dq_ref[0, e, q_rows, :] + _dot(ds, k)

        step(j, True)

        def loop_body(i, carry):
            step(i, False)
            return carry

        lax.fori_loop(j + 1, nk, loop_body, 0)
        dv_ref[0] = dv_acc[...].astype(BF16)

    full = pl.BlockSpec((1, hb, S, LANES), lambda b, h, j: (b, h, 0, 0))
    blk = pl.BlockSpec((1, hb, t, LANES), lambda b, h, j: (b, h, j, 0))
    oshape = SDS((B, H, S, LANES), F32)
    return pl.pallas_call(
        body, name="attn_bwd", grid=(B, H // hb, nk),
        in_specs=[full, blk, blk, full],
        out_specs=[full, blk, blk],
        out_shape=[oshape, oshape, SDS((B, H, S, LANES), BF16)],
        scratch_shapes=[pltpu.VMEM((hb, t, LANES), F32)],
        compiler_params=_params(("parallel", "parallel", "arbitrary")),
    )(qb, ka, va, doa)


def _qk_bwd(dqa, dka, dva, z, gq, gk, B, S, H, tm):
    T = B * S
    FW = H * HEAD_DIM
    nsb = S // tm
    nfb = FW // LANES
    scale = HEAD_DIM ** -0.5

    def body(dq_ref, dk_ref, dv_ref, zq_ref, zk_ref, gq_ref, gk_ref, dzq_ref, dzk_ref, dzv_ref, dc_ref, dg_ref):
        p = pl.program_id(0)

        @pl.when(pl.program_id(1) == 0)
        def _():
            dg_ref[...] = jnp.zeros_like(dg_ref)

        lane = _lane((tm, LANES))
        lo = lane < HEAD_DIM

        def natural(ref):
            return jnp.where(lo, ref[0, 0].astype(F32), pltpu.roll(ref[0, 1].astype(F32), HEAD_DIM, 1))

        def norm_bwd(dn, x, g, row, out_ref):
            r = lax.rsqrt(_half_stats(x * x) * (1.0 / HEAD_DIM) + EPS)
            xh = x * r
            dg_ref[row:row + 1, :] = dg_ref[row:row + 1, :] + jnp.sum(dn * xh, axis=0, keepdims=True)
            dxh = dn * g
            mm = _half_stats(dxh * xh) * (1.0 / HEAD_DIM)
            out_ref[...] = (r * (dxh - xh * mm)).astype(BF16)

        norm_bwd(natural(dq_ref) * scale, zq_ref[...], gq_ref[...], 0, dzq_ref)
        norm_bwd(natural(dk_ref), zk_ref[...], gk_ref[...], 1, dzk_ref)
        dzv_ref[...] = natural(dv_ref).astype(BF16)

        dc = jnp.zeros((tm, LANES), F32)
        for e in range(2):
            val = _lane_col(dq_ref[0, e], lane, L_ROWSUM) - _lane_col(dk_ref[0, e], lane, L_KDECAY)
            dc = jnp.where(lane == 2 * p + e, val, dc)
        dc_ref[0] = dc

    hspec = pl.BlockSpec((1, 2, tm, LANES), lambda p, i: (i // nsb, p, i % nsb, 0))
    zspec = lambda off: pl.BlockSpec((tm, LANES), lambda p, i: (i, off + p))
    gspec = pl.BlockSpec((1, LANES), lambda p, i: (0, p))
    ospec = pl.BlockSpec((tm, LANES), lambda p, i: (i, p))
    return pl.pallas_call(
        body, name="qk_bwd", grid=(H // 2, T // tm),
        in_specs=[hspec, hspec, hspec, zspec(0), zspec(nfb), gspec, gspec],
        out_specs=[ospec, ospec, ospec,
                   pl.BlockSpec((1, tm, LANES), lambda p, i: (p, i, 0)),
                   pl.BlockSpec((8, LANES), lambda p, i: (0, p))],
        out_shape=[SDS((T, FW), BF16), SDS((T, FW), BF16), SDS((T, FW), BF16),
                   SDS((H // 2, T, LANES), F32), SDS((8, FW), F32)],
        compiler_params=_params(("parallel", "arbitrary")),
    )(dqa, dka, dva, z, z, gq, gk)


def _gate_bwd(dc8, z, b_pad, B, S, H, col_blk, fp, tc):
    T = B * S
    nsb = S // tc
    npair = dc8.shape[0]

    def body(dc_ref, zf_ref, b_ref, dz_ref, db_ref, carry):
        first_step = (pl.program_id(0) == 0) & (pl.program_id(1) == 0)

        @pl.when(first_step)
        def _():
            db_ref[...] = jnp.zeros_like(db_ref)

        @pl.when(pl.program_id(1) == 0)
        def _():
            carry[...] = jnp.zeros_like(carry)

        dc = dc_ref[0]
        for k in range(1, npair):
            dc = dc + dc_ref[k]
        dlf = _tri_cumsum(dc, True) + carry[...]
        carry[...] = carry[...] + jnp.sum(dc, axis=0, keepdims=True)
        x = zf_ref[...] + b_ref[...]
        dlogit = dlf * _sigmoid(-x)
        db_ref[0:1, :] = db_ref[0:1, :] + jnp.sum(dlogit, axis=0, keepdims=True)
        dz_ref[...] = jnp.zeros_like(dz_ref)
        dz_ref[:, :LANES] = dlogit.astype(BF16)

    rrow = lambda b, s: b * nsb + (nsb - 1 - s)
    return pl.pallas_call(
        body, name="gate_bwd", grid=(B, nsb),
        in_specs=[pl.BlockSpec((npair, tc, LANES), lambda b, s: (0, rrow(b, s), 0)),
                  pl.BlockSpec((tc, LANES), lambda b, s: (rrow(b, s), col_blk)),
                  pl.BlockSpec((1, LANES), lambda b, s: (0, 0))],
        out_specs=[pl.BlockSpec((tc, fp), lambda b, s: (rrow(b, s), 0)),
                   pl.BlockSpec((8, LANES), lambda b, s: (0, 0))],
        out_shape=[SDS((T, fp), BF16), SDS((8, LANES), F32)],
        scratch_shapes=[pltpu.VMEM((1, LANES), F32)],
        compiler_params=_params(("arbitrary", "arbitrary")),
    )(dc8, z, b_pad)


def _matmul_tn(a, b, name, tmm, tn, tk):
    T, M = a.shape
    N = b.shape[1]
    tmm, tn, tk = min(tmm, M), min(tn, N), min(tk, T)

    def body(a_ref, b_ref, o_ref):
        @pl.when(pl.program_id(2) == 0)
        def _():
            o_ref[...] = jnp.zeros_like(o_ref)

        o_ref[...] = o_ref[...] + _dot_tn(a_ref[...], b_ref[...])

    return pl.pallas_call(
        body, name=name, grid=(M // tmm, N // tn, T // tk),
        in_specs=[pl.BlockSpec((tk, tmm), lambda i, j, k: (k, i)),
                  pl.BlockSpec((tk, tn), lambda i, j, k: (k, j))],
        out_specs=pl.BlockSpec((tmm, tn), lambda i, j, k: (i, j)),
        out_shape=SDS((M, N), F32),
        compiler_params=_params(("parallel", "parallel", "arbitrary")),
    )(a, b)


def _dh_rms_bwd(pieces, w_t, x2, g, dout, tm, tk, parts):
    T, D = x2.shape
    nks = [p.shape[1] // tk for p in pieces]
    starts = [sum(nks[:k]) for k in range(len(pieces))]
    nk = sum(nks)
    ni = T // tm
    n = len(parts)

    def body(*refs):
        dz_refs = refs[:len(pieces)]
        w_ref, x_ref, g_ref, do_ref = refs[len(pieces):len(pieces) + 4]
        part_refs = refs[len(pieces) + 4:len(pieces) + 4 + n]
        gx_ref, dg_ref = refs[len(pieces) + 4 + n:len(pieces) + 6 + n]
        slot_refs = refs[len(pieces) + 6 + n:len(pieces) + 6 + 2 * n]
        acc_ref, send_sems, recv_sems = refs[len(pieces) + 6 + 2 * n:]
        k = pl.program_id(1)
        first_step = (pl.program_id(0) == 0) & (k == 0)
        last_step = (pl.program_id(0) == ni - 1) & (k == nk - 1)
        x, y, c = _place()
        chips = [(1 - x, y), (x, 1 - y), (1 - x, 1 - y)]

        def copy(a, f, to):
            cx, cy = chips[f]
            return pltpu.make_async_remote_copy(
                src_ref=part_refs[a].at[2 * cx + cy], dst_ref=slot_refs[a].at[f],
                send_sem=send_sems.at[a * 3 + f], recv_sem=recv_sems.at[a * 3 + f],
                device_id=to, device_id_type=MESH)

        @pl.when(first_step)
        def _():
            dg_ref[...] = jnp.zeros_like(dg_ref)
            for a in range(n):
                for f in range(3):
                    copy(a, f, (*chips[f], c)).start()

        @pl.when(last_step)
        def _():
            for a in range(n):
                for f in range(3):
                    copy(a, f, (x, y, c)).wait_recv()
            for a in range(n):
                for f in range(3):
                    copy(a, f, (*chips[f], c)).wait_send()

        @pl.when(k == 0)
        def _():
            acc_ref[...] = jnp.zeros_like(acc_ref)

        for dz_ref, st, cnt in zip(dz_refs, starts, nks):
            @pl.when((k >= st) & (k < st + cnt))
            def _(dz_ref=dz_ref):
                acc_ref[...] = acc_ref[...] + _dot(dz_ref[...], w_ref[...])

        @pl.when(k == nk - 1)
        def _():
            x = x_ref[...]
            r = lax.rsqrt(jnp.mean(x * x, axis=-1, keepdims=True) + EPS)
            xh = x * r
            dh = acc_ref[...]
            dg_ref[0:1, :] = dg_ref[0:1, :] + jnp.sum(dh * xh, axis=0, keepdims=True)
            dxn = dh * g_ref[...]
            gx_ref[...] = do_ref[...] + r * (dxn - xh * jnp.mean(dxn * xh, axis=-1, keepdims=True))

    def piece_spec(st, cnt):
        return pl.BlockSpec((tm, tk), lambda i, k: (i, jnp.clip(k - st, 0, cnt - 1)))

    tspec = pl.BlockSpec((tm, D), lambda i, k: (i, 0))
    return pl.pallas_call(
        body, name="dh_rms_bwd", grid=(T // tm, nk),
        in_specs=[piece_spec(st, cnt) for st, cnt in zip(starts, nks)]
        + [pl.BlockSpec((tk, D), lambda i, k: (k, 0)), tspec, pl.BlockSpec((1, D), lambda i, k: (0, 0)), tspec]
        + [ANY] * n,
        out_specs=[tspec, pl.BlockSpec((8, D), lambda i, k: (0, 0))] + [ANY] * n,
        out_shape=[SDS((T, D), F32), SDS((8, D), F32)] + [SDS((3,) + p.shape[1:], p.dtype) for p in parts],
        scratch_shapes=[pltpu.VMEM((tm, D), F32),
                        pltpu.SemaphoreType.DMA((3 * n,)), pltpu.SemaphoreType.DMA((3 * n,))],
        compiler_params=_params(("arbitrary", "arbitrary")),
    )(*pieces, w_t, x2, g, dout, *parts)


def _block_plan(R, C, tr, tc):
    br = min(tr, R)
    if R % br == 0:
        return (br, C), R // br, lambda i: (i, 0)
    bc = min(tc, C)
    assert C % bc == 0
    return (R, bc), C // bc, lambda i: (0, i)


def _ew_call(body, name, ins, n_out, out_dtypes, tr, tc):
    R, C = ins[0].shape
    blk, steps, imap = _block_plan(R, C, tr, tc)
    spec = pl.BlockSpec(blk, imap)
    return pl.pallas_call(
        body, name=name, grid=(steps,),
        in_specs=[spec] * len(ins), out_specs=[spec] * n_out,
        out_shape=[SDS((R, C), dt) for dt in out_dtypes],
        compiler_params=_params(("parallel",)),
    )(*ins)


def _sum_slots(slots, name, first=None, tr=256):
    n, R, C = slots.shape
    blk, steps, imap = _block_plan(R, C, tr, 2 * LANES)
    lead = [] if first is None else [first]

    def body(*refs):
        s_ref, o_ref = refs[-2:]
        acc = refs[0][...].astype(F32) if lead else s_ref[0].astype(F32)
        for k in range(0 if lead else 1, n):
            acc = acc + s_ref[k].astype(F32)
        o_ref[...] = acc

    return pl.pallas_call(
        body, name=name, grid=(steps,),
        in_specs=[pl.BlockSpec(blk, imap)] * len(lead) + [pl.BlockSpec((n,) + blk, lambda i: (0,) + imap(i))],
        out_specs=pl.BlockSpec(blk, imap),
        out_shape=SDS((R, C), F32),
        compiler_params=_params(("parallel",)),
    )(*lead, slots)


def _adamw(w, g, m, v, name):
    def body(w_ref, g_ref, m_ref, v_ref, d_ref, nm_ref, nv_ref):
        gg = g_ref[...]
        nm = ADAM_B1 * m_ref[...] + (1.0 - ADAM_B1) * gg
        nv = ADAM_B2 * v_ref[...] + (1.0 - ADAM_B2) * (gg * gg)
        m_hat = nm / (1.0 - ADAM_B1 ** ADAM_STEP)
        v_hat = nv / (1.0 - ADAM_B2 ** ADAM_STEP)
        d_ref[...] = -ADAM_LR * (m_hat / (jnp.sqrt(v_hat) + ADAM_EPS) + ADAM_WD * w_ref[...])
        nm_ref[...] = nm
        nv_ref[...] = nv

    return _ew_call(body, name, [w, g, m, v], 3, [F32, F32, F32], 128, 2 * LANES)


ANY = pl.BlockSpec(memory_space=pl.ANY)


def _place():
    return lax.axis_index("x"), lax.axis_index("y"), lax.axis_index("c")


def _gather_chips(shards, splits):
    n = len(shards)
    per = 7

    def body(*refs):
        ins, outs = refs[:n], refs[n:2 * n]
        send_sems, recv_sems = refs[2 * n:]
        x, y, c = _place()
        mine = 2 * x + y
        chips = [(1 - x, y), (x, 1 - y), (1 - x, 1 - y)]

        def rows(a, half):
            return pl.ds(0, splits[a]) if half == 0 else pl.ds(splits[a], ins[a].shape[0] - splits[a])

        def copy(a, k, chip_idx, half, to, src=None):
            dst = outs[a].at[chip_idx, rows(a, half)]
            return pltpu.make_async_remote_copy(
                src_ref=dst if src is None else src, dst_ref=dst,
                send_sem=send_sems.at[a * per + k], recv_sem=recv_sems.at[a * per + k],
                device_id=to, device_id_type=MESH)

        def own(a, to):
            return pltpu.make_async_remote_copy(
                src_ref=ins[a], dst_ref=outs[a].at[mine],
                send_sem=send_sems.at[a * per + 6], recv_sem=recv_sems.at[a * per + 6],
                device_id=to, device_id_type=MESH)

        for cc in (0, 1):
            @pl.when(c == cc)
            def _(cc=cc):
                me, sibling = (x, y, cc), (x, y, 1 - cc)
                first = [copy(a, k, mine, cc, (*chip, cc), src=ins[a].at[rows(a, cc)])
                         for a in range(n) for k, chip in enumerate(chips)]
                first += [own(a, sibling) for a in range(n)]
                for cp in first:
                    cp.start()
                passed = []
                for k, (cx, cy) in enumerate(chips):
                    for a in range(n):
                        copy(a, k, 2 * cx + cy, cc, me).wait_recv()
                        fwd = copy(a, 3 + k, 2 * cx + cy, cc, sibling)
                        fwd.start()
                        passed.append(fwd)
                for k, (cx, cy) in enumerate(chips):
                    for a in range(n):
                        copy(a, 3 + k, 2 * cx + cy, 1 - cc, me).wait_recv()
                for a in range(n):
                    own(a, me).wait_recv()
                for cp in first + passed:
                    cp.wait_send()

    return pl.pallas_call(
        body, name="gather_chips",
        in_specs=[ANY] * n, out_specs=[ANY] * n,
        out_shape=[SDS((4,) + s.shape, s.dtype) for s in shards],
        scratch_shapes=[pltpu.SemaphoreType.DMA((per * n,)), pltpu.SemaphoreType.DMA((per * n,))],
    )(*shards)


def _pair_swap(arrs):
    n = len(arrs)

    def body(*refs):
        ins, outs = refs[:n], refs[n:2 * n]
        send_sems, recv_sems = refs[2 * n:]
        x, y, c = _place()
        for cc in (0, 1):
            @pl.when(c == cc)
            def _(cc=cc):
                copies = []
                for a in range(n):
                    half = ins[a].shape[1] // 2
                    copies.append(pltpu.make_async_remote_copy(
                        src_ref=ins[a].at[:, pl.ds((1 - cc) * half, half)], dst_ref=outs[a],
                        send_sem=send_sems.at[a], recv_sem=recv_sems.at[a],
                        device_id=(x, y, 1 - cc), device_id_type=MESH))
                for cp in copies:
                    cp.start()
                for cp in copies:
                    cp.wait()

    return pl.pallas_call(
        body, name="pair_swap",
        in_specs=[ANY] * n, out_specs=[ANY] * n,
        out_shape=[SDS((h.shape[0], h.shape[1] // 2), h.dtype) for h in arrs],
        scratch_shapes=[pltpu.SemaphoreType.DMA((n,)), pltpu.SemaphoreType.DMA((n,))],
    )(*arrs)


def _pair_sum(arrs, got, core, tr):
    half = arrs[0].shape[1] // 2
    cnts = [p.shape[0] // tr for p in arrs]
    starts = [sum(cnts[:k]) for k in range(len(arrs))]

    def body(core_ref, *refs):
        del core_ref
        own_refs, got_refs, o_ref = refs[:len(arrs)], refs[len(arrs):2 * len(arrs)], refs[-1]
        s = pl.program_id(0)
        for own_ref, got_ref, st, cnt in zip(own_refs, got_refs, starts, cnts):
            @pl.when((s >= st) & (s < st + cnt))
            def _(own_ref=own_ref, got_ref=got_ref):
                o_ref[...] = (own_ref[...] + got_ref[...]).astype(BF16)

    def own_spec(st, cnt):
        return pl.BlockSpec((tr, half), lambda s, core_ref: (jnp.clip(s - st, 0, cnt - 1), core_ref[0]))

    def got_spec(st, cnt):
        return pl.BlockSpec((tr, half), lambda s, core_ref: (jnp.clip(s - st, 0, cnt - 1), 0))

    return pl.pallas_call(
        body, name="pair_sum",
        grid_spec=pltpu.PrefetchScalarGridSpec(
            num_scalar_prefetch=1, grid=(sum(cnts),),
            in_specs=[own_spec(st, cnt) for st, cnt in zip(starts, cnts)]
            + [got_spec(st, cnt) for st, cnt in zip(starts, cnts)],
            out_specs=pl.BlockSpec((tr, half), lambda s, core_ref: (s, 0))),
        out_shape=SDS((sum(cnts) * tr, half), BF16),
        compiler_params=_params(("arbitrary",)),
    )(core, *arrs, *got)


def _share_results(arrs, rows):
    n = len(arrs)
    flips = [(fx, fy, fc) for fx in (0, 1) for fy in (0, 1) for fc in (0, 1)][1:]

    def body(*refs):
        ins, rows_ref, outs, all_ref = refs[:n], refs[n], refs[n + 1:2 * n + 1], refs[2 * n + 1]
        send_sems, recv_sems, local_sem = refs[2 * n + 2:]
        x, y, c = _place()
        me = 4 * x + 2 * y + c
        local = pltpu.make_async_copy(rows_ref, all_ref.at[me], local_sem)
        local.start()
        copies = [pltpu.make_async_remote_copy(
            src_ref=ins[a], dst_ref=outs[a], send_sem=send_sems.at[a], recv_sem=recv_sems.at[a],
            device_id=(x, y, 1 - c), device_id_type=MESH) for a in range(n)]
        for k, (fx, fy, fc) in enumerate(flips):
            copies.append(pltpu.make_async_remote_copy(
                src_ref=rows_ref, dst_ref=all_ref.at[me], send_sem=send_sems.at[n + k], recv_sem=recv_sems.at[n + k],
                device_id=(x ^ fx, y ^ fy, c ^ fc), device_id_type=MESH))
        for cp in copies:
            cp.start()
        for cp in copies[:n]:
            cp.wait_recv()
        for k, (fx, fy, fc) in enumerate(flips):
            src = 4 * (x ^ fx) + 2 * (y ^ fy) + (c ^ fc)
            pltpu.make_async_remote_copy(
                src_ref=rows_ref, dst_ref=all_ref.at[src], send_sem=send_sems.at[n + k], recv_sem=recv_sems.at[n + k],
                device_id=(x, y, c), device_id_type=MESH).wait_recv()
        for cp in copies:
            cp.wait_send()
        local.wait()

    outs = pl.pallas_call(
        body, name="share_results",
        in_specs=[ANY] * (n + 1), out_specs=[ANY] * (n + 1),
        out_shape=[SDS(h.shape, h.dtype) for h in arrs] + [SDS((8,) + rows.shape, rows.dtype)],
        scratch_shapes=[pltpu.SemaphoreType.DMA((n + 7,)), pltpu.SemaphoreType.DMA((n + 7,)),
                        pltpu.SemaphoreType.DMA],
    )(*arrs, rows)
    return outs[:n], outs[n]


def _tiles(S, FW):
    big = FW % 512 == 0
    return dict(
        fp=512 if big else LANES,
        tn=1536 if big else LANES,
        tm_in=min(1024, S),
        t_attn=min(512, S),
        hb_fwd=4,
        hb_bwd=2,
        tm_prep=min(512, S),
        tm_mix=min(128, S),
        tc=min(256, S),
        tk=512 if big else LANES,
    )


def kernel(x, norm_g, w_in, b_forget, q_norm_g, k_norm_g, conv_w, conv_b, conv_ln_g, conv_ln_b, w_out, loss_target, m_norm_g, m_w_in, m_b_forget, m_q_norm_g, m_k_norm_g, m_conv_w, m_conv_b, m_conv_ln_g, m_conv_ln_b, m_w_out, v_norm_g, v_w_in, v_b_forget, v_q_norm_g, v_k_norm_g, v_conv_w, v_conv_b, v_conv_ln_g, v_conv_ln_b, v_w_out):
    B, S, D = x.shape
    H, dh = q_norm_g.shape[1:]
    FW = H * dh
    CW = conv_b.shape[-1]
    n_taps, cw_shard = conv_w.shape[1:]
    in_shard = w_in.shape[2]
    out_shard = w_out.shape[1]
    assert dh == HEAD_DIM and H % 2 == 0 and H <= LANES and FW == CW == D
    assert n_taps - 1 <= HALO and 4 * cw_shard == CW and 4 * out_shard == FW + CW
    assert 4 * in_shard == 4 * FW + 3 * CW + H
    T = B * S
    tl = _tiles(S, FW)
    fp = tl["fp"]
    xi, yi, ci = _place()

    w_t = jnp.transpose(w_in[0])
    conv_pad = jnp.pad(conv_w[0], ((0, HALO - n_taps), (0, 0)))
    bf16_rows = 2 * SUBLANES
    g_in, g_out, g_cw = _gather_chips(
        [w_t.astype(BF16), w_out[0].astype(BF16), conv_pad],
        [in_shard // 2 // bf16_rows * bf16_rows, out_shard // 2, HALO // 2])
    w_t_full = g_in.reshape(4 * in_shard, D)
    w_out_full = g_out.reshape(FW + CW, D)
    conv_full = g_cw.transpose(1, 0, 2).reshape(HALO, CW)
    o_f = 3 * FW
    w_pack = jnp.concatenate([w_t_full[:o_f], w_t_full[o_f + H:],
                              jnp.pad(w_t_full[o_f:o_f + H], ((0, fp - H), (0, 0)))], axis=0)
    f_col = 4 * FW + 3 * CW

    x2 = x.reshape(T, D)
    tgt = loss_target.reshape(T, D)
    b_pad = jnp.pad(b_forget, ((0, 0), (0, LANES - H)))
    gq = q_norm_g.reshape(1, FW)
    gk = k_norm_g.reshape(1, FW)

    z, h = _fwd_in(x2, norm_g, w_pack, tl["tm_in"], tl["tn"])
    c = _gate_fwd(z, b_pad, B, S, H, f_col // LANES, tl["tc"])
    qa, ka, va = _attn_prep(z, c, gq, gk, B, S, H, tl["tm_prep"])
    oa = _attn_fwd(qa, ka, va, tl["t_attn"], tl["hb_fwd"])
    y, u2, a_nat, dout, dout_b, dy, loss_acc = _fwd_out(
        oa, z, x2, tgt, conv_full, conv_b, conv_ln_g, conv_ln_b, w_out_full, B, S, H, n_taps, tl["tm_mix"])

    dzgf, dzgc, du2, doa, qb, sg_conv = _bwd_prep(dy, z, a_nat, oa, qa, u2, conv_ln_g, conv_ln_b, B, S, H, tl["tm_mix"])
    dzglu, dconv_w = _conv_bwd(du2, z, conv_full, B, S, n_taps, tl["tm_mix"])
    dqa, dka, dva = _attn_bwd(qb, ka, va, doa, tl["t_attn"], tl["hb_bwd"])
    dzq, dzk, dzv, dc8, dg_qk = _qk_bwd(dqa, dka, dva, z, gq, gk, B, S, H, tl["tm_prep"])
    dzf, db_f = _gate_bwd(dc8, z, b_pad, B, S, H, f_col // LANES, fp, tl["tc"])
    pieces = [dzq, dzk, dzv, dzgf, dzglu, dzgc, dzf]
    dw_all = [_matmul_tn(p, h, f"dw_in_{k}", 1024, 1024, 1024) for k, p in enumerate(pieces)]
    dw_all.append(_matmul_tn(y, dout_b, "dw_out", 1024, 1024, 1024))

    summed = _pair_sum(dw_all, _pair_swap(dw_all), ci.astype(jnp.int32).reshape(1), fp)
    ends = [0]
    for t in dw_all:
        ends.append(ends[-1] + t.shape[0])
    cut = lambda k, rows=None: summed[ends[k]:(ends[k + 1] if rows is None else ends[k] + rows)]
    part_in = jnp.concatenate([cut(0), cut(1), cut(2), cut(6, H), cut(3), cut(4), cut(5)],
                              axis=0).reshape(4, in_shard, D // 2)
    part_out = cut(7).reshape(4, out_shard, D // 2)
    grad_x2, dg_norm, slots_in, slots_out = _dh_rms_bwd(
        pieces, w_pack, x2, norm_g, dout, tl["tm_in"], tl["tk"], [part_in, part_out])
    chip = 2 * xi + yi
    half_in = _sum_slots(slots_in, "chip_sum_in", lax.dynamic_index_in_dim(part_in, chip, 0, keepdims=False))
    half_out = _sum_slots(slots_out, "chip_sum_out", lax.dynamic_index_in_dim(part_out, chip, 0, keepdims=False))
    lanes_to_d = lambda t: jnp.pad(t, ((0, 0), (0, D - LANES)))
    small = jnp.concatenate([
        dg_norm[0:1], lanes_to_d(db_f[0:1, :]), dg_qk[0:1], dg_qk[1:2],
        sg_conv[2:3], sg_conv[0:1], sg_conv[1:2], dconv_w, lanes_to_d(loss_acc[0:1, :])], axis=0)
    n_small = small.shape[0]
    (other_in, other_out), all_small = _share_results([half_in, half_out], small)

    def both_halves(mine, other, axis):
        return jnp.where(ci == 0, jnp.concatenate([mine, other], axis=axis), jnp.concatenate([other, mine], axis=axis))

    grad_w_t = both_halves(half_in, other_in, 1)
    grad_w_out = both_halves(half_out, other_out, 1)

    small_sum = _sum_slots(all_small, "small_sum", tr=n_small)
    loss = 0.5 * small_sum[n_small - 1, 0] / D
    grad_norm_g, grad_b_f = small_sum[0:1], small_sum[1:2, :H]
    grad_gq, grad_gk = small_sum[2:3].reshape(1, H, dh), small_sum[3:4].reshape(1, H, dh)
    grad_conv_b, grad_ln_g, grad_ln_b = small_sum[4:5], small_sum[5:6], small_sum[6:7]
    grad_conv_w = lax.dynamic_slice_in_dim(small_sum[7:7 + n_taps], chip * cw_shard, cw_shard, axis=1)

    d_t, nm_t, nv_t = _adamw(w_t, grad_w_t, jnp.transpose(m_w_in[0]), jnp.transpose(v_w_in[0]), "adamw_in")
    grad_w_in, d_in, nm_in, nv_in = (jnp.transpose(t)[None] for t in (grad_w_t, d_t, nm_t, nv_t))
    d_out, nm_out, nv_out = (t[None] for t in _adamw(w_out[0], grad_w_out, m_w_out[0], v_w_out[0], "adamw_out"))
    d_cw, nm_cw, nv_cw = (t[None] for t in _adamw(conv_w[0], grad_conv_w, m_conv_w[0], v_conv_w[0], "adamw_conv_w"))

    def rows(ws):
        return jnp.concatenate([jnp.pad(t.reshape(1, -1), ((0, 0), (0, D - t.size))) for t in ws], axis=0)

    small_w = [norm_g, b_forget, q_norm_g, k_norm_g, conv_b, conv_ln_g, conv_ln_b]
    small_m = [m_norm_g, m_b_forget, m_q_norm_g, m_k_norm_g, m_conv_b, m_conv_ln_g, m_conv_ln_b]
    small_v = [v_norm_g, v_b_forget, v_q_norm_g, v_k_norm_g, v_conv_b, v_conv_ln_g, v_conv_ln_b]
    d_s, nm_s, nv_s = _adamw(rows(small_w), small_sum[0:7], rows(small_m), rows(small_v), "adamw_small")

    def unpack(t):
        return [t[k:k + 1, :w.size].reshape(w.shape) for k, w in enumerate(small_w)]

    def order(s, in_, cw, out_):
        ng, bf, qg, kg, cb, lg, lb = s
        return [ng, in_, bf, qg, kg, cw, cb, lg, lb, out_]

    grads = [grad_norm_g, grad_w_in, grad_b_f, grad_gq, grad_gk, grad_conv_w[None],
             grad_conv_b, grad_ln_g, grad_ln_b, grad_w_out[None]]
    return (loss, grad_x2.reshape(B, S, D), *grads,
            *order(unpack(d_s), d_in, d_cw, d_out),
            *order(unpack(nm_s), nm_in, nm_cw, nm_out),
            *order(unpack(nv_s), nv_in, nv_cw, nv_out))
```

```python
import jax
import jax.numpy as jnp
from jax import lax
from jax.experimental import pallas as pl
from jax.experimental.pallas import tpu as pltpu

F32 = jnp.float32
BF16 = jnp.bfloat16
SDS = jax.ShapeDtypeStruct
MESH = pl.DeviceIdType.MESH

EPS = 1e-6
NEG_INF = -1e30
LANES = 128
SUBLANES = 8
HEAD_DIM = 64
HALO = 32
VMEM_LIMIT = 56 * 1024 * 1024

L_ROWSUM = 64
L_KDECAY = 67
L_LSE = 70
L_D = 65
L_QNORM = 73
L_KNORM = 74
NORM_SLACK = 1.02
SHIFT_MAX = 40.0

ADAM_LR = 0.001
ADAM_B1 = 0.9
ADAM_B2 = 0.999
ADAM_EPS = 1e-08
ADAM_WD = 0.01
ADAM_STEP = 10


def _params(sem, vmem=VMEM_LIMIT):
    return pltpu.CompilerParams(dimension_semantics=sem, vmem_limit_bytes=vmem)


def _sigmoid(x):
    return 1.0 / (1.0 + jnp.exp(-x))


def _split3(x):
    hi = x.astype(BF16).astype(F32)
    r = x - hi
    mid = r.astype(BF16).astype(F32)
    lo = (r - mid).astype(BF16).astype(F32)
    return hi, mid, lo


def _dot(a, b):
    return jnp.dot(a, b, preferred_element_type=F32)


def _dot_nt(a, b):
    return lax.dot_general(a, b, (((1,), (1,)), ((), ())), preferred_element_type=F32)


def _dot_tn(a, b):
    return lax.dot_general(a, b, (((0,), (0,)), ((), ())), preferred_element_type=F32)


def _lane(shape):
    return lax.broadcasted_iota(jnp.int32, shape, 1)


def _lane_col(x, lane, idx):
    return jnp.sum(jnp.where(lane == idx, x, 0.0), axis=-1, keepdims=True)


def _put3(base, lane, start, pieces):
    out = base
    for k, p in enumerate(pieces):
        out = jnp.where(lane == start + k, p, out)
    return out


def _half_stats(t):
    hi = t.astype(BF16)
    mid = (t - hi.astype(F32)).astype(BF16)
    row = lax.broadcasted_iota(jnp.int32, (2 * LANES, LANES), 0)
    col = lax.broadcasted_iota(jnp.int32, (2 * LANES, LANES), 1)
    same_half = (jnp.bitwise_and(row, LANES - 1) < HEAD_DIM) == (col < HEAD_DIM)
    return _dot(jnp.concatenate([hi, mid], axis=1), jnp.where(same_half, 1.0, 0.0).astype(BF16))


def _fwd_in(x2, g, w_t, tm, tn):
    T, D = x2.shape
    N = w_t.shape[0]

    def body(x_ref, g_ref, w_ref, z_ref, h_ref):
        @pl.when(pl.program_id(1) == 0)
        def _():
            x = x_ref[...]
            r = lax.rsqrt(jnp.mean(x * x, axis=-1, keepdims=True) + EPS)
            h_ref[...] = (x * r * g_ref[...]).astype(BF16)

        z_ref[...] = _dot_nt(h_ref[...], w_ref[...])

    return pl.pallas_call(
        body, name="fwd_in", grid=(T // tm, N // tn),
        in_specs=[pl.BlockSpec((tm, D), lambda i, j: (i, 0)),
                  pl.BlockSpec((1, D), lambda i, j: (0, 0)),
                  pl.BlockSpec((tn, D), lambda i, j: (j, 0))],
        out_specs=[pl.BlockSpec((tm, tn), lambda i, j: (i, j)),
                   pl.BlockSpec((tm, D), lambda i, j: (i, 0))],
        out_shape=[SDS((T, N), F32), SDS((T, D), BF16)],
        compiler_params=_params(("parallel", "arbitrary")),
    )(x2, g, w_t)


def _tri_cumsum(x, reverse):
    t = x.shape[0]
    row = lax.broadcasted_iota(jnp.int32, (t, t), 0)
    col = lax.broadcasted_iota(jnp.int32, (t, t), 1)
    tri = (row <= col) if reverse else (row >= col)
    tri = jnp.where(tri, 1.0, 0.0).astype(BF16)
    hi, mid, lo = _split3(x)
    return _dot(tri, hi.astype(BF16)) + _dot(tri, mid.astype(BF16)) + _dot(tri, lo.astype(BF16))


def _gate_fwd(z, b_pad, B, S, H, col_blk, tc):
    T = B * S
    nsb = S // tc

    def body(zf_ref, b_ref, c_ref, carry):
        @pl.when(pl.program_id(1) == 0)
        def _():
            carry[...] = jnp.zeros_like(carry)

        x = zf_ref[...] + b_ref[...]
        lf = jnp.minimum(x, 0.0) - jnp.log(1.0 + jnp.exp(-jnp.abs(x)))
        lf = jnp.where(_lane(lf.shape) < H, lf, 0.0)
        c_ref[...] = _tri_cumsum(lf, False) + carry[...]
        carry[...] = carry[...] + jnp.sum(lf, axis=0, keepdims=True)

    return pl.pallas_call(
        body, name="gate_fwd", grid=(B, nsb),
        in_specs=[pl.BlockSpec((tc, LANES), lambda b, s: (b * nsb + s, col_blk)),
                  pl.BlockSpec((1, LANES), lambda b, s: (0, 0))],
        out_specs=pl.BlockSpec((tc, LANES), lambda b, s: (b * nsb + s, 0)),
        out_shape=SDS((T, LANES), F32),
        scratch_shapes=[pltpu.VMEM((1, LANES), F32)],
        compiler_params=_params(("parallel", "arbitrary")),
    )(z, b_pad)


def _qk_normalize(x, g):
    r = lax.rsqrt(_half_stats(x * x) * (1.0 / HEAD_DIM) + EPS)
    return x * r * g


def _head_norms(x):
    own = jnp.sqrt(_half_stats(x * x)) * NORM_SLACK
    return [pltpu.roll(own, HEAD_DIM, 1), own]


def _attn_prep(z, c, gq, gk, B, S, H, tm):
    T = B * S
    FW = H * HEAD_DIM
    nsb = S // tm
    nfb = FW // LANES
    scale = HEAD_DIM ** -0.5

    def body(zq_ref, zk_ref, zv_ref, c_ref, gq_ref, gk_ref, qa_ref, ka_ref, va_ref):
        p = pl.program_id(1)
        lane = _lane((tm, LANES))
        lo = lane < HEAD_DIM
        qn = _qk_normalize(zq_ref[...], gq_ref[...]) * scale
        kn = _qk_normalize(zk_ref[...], gk_ref[...])
        v = zv_ref[...]
        cc = c_ref[...]
        ones_q = ((lane >= L_KDECAY) & (lane < L_KDECAY + 3)).astype(F32)
        ones_k = (((lane >= L_ROWSUM) & (lane < L_ROWSUM + 3)) | ((lane >= L_LSE) & (lane < L_LSE + 3))).astype(F32)
        ones_v = ((lane >= L_ROWSUM) & (lane < L_D + 3)).astype(F32)
        q_norms, k_norms = _head_norms(qn), _head_norms(kn)
        for e in range(2):
            if e == 0:
                qe, ke, ve = qn, kn, v
            else:
                qe, ke, ve = (pltpu.roll(t, HEAD_DIM, 1) for t in (qn, kn, v))
            ch = _lane_col(cc, lane, 2 * p + e)
            pieces = _split3(ch)
            qa = jnp.where(lo, qe, _put3(ones_q, lane, L_ROWSUM, pieces))
            qa = jnp.where(lane == L_QNORM, q_norms[e], qa)
            ka = jnp.where(lo, ke, _put3(ones_k, lane, L_KDECAY, [-t for t in pieces]))
            ka = jnp.where(lane == L_KNORM, k_norms[e], ka)
            va = jnp.where(lo, ve, ones_v)
            qa_ref[0, e] = qa.astype(BF16)
            ka_ref[0, e] = ka.astype(BF16)
            va_ref[0, e] = va.astype(BF16)

    zspec = lambda off: pl.BlockSpec((tm, LANES), lambda i, p: (i, off + p))
    gspec = pl.BlockSpec((1, LANES), lambda i, p: (0, p))
    ospec = pl.BlockSpec((1, 2, tm, LANES), lambda i, p: (i // nsb, p, i % nsb, 0))
    oshape = SDS((B, H, S, LANES), BF16)
    return pl.pallas_call(
        body, name="attn_prep", grid=(T // tm, H // 2),
        in_specs=[zspec(0), zspec(nfb), zspec(2 * nfb),
                  pl.BlockSpec((tm, LANES), lambda i, p: (i, 0)), gspec, gspec],
        out_specs=[ospec, ospec, ospec],
        out_shape=[oshape, oshape, oshape],
        compiler_params=_params(("parallel", "arbitrary")),
    )(z, z, z, c, gq, gk)


def _attn_fwd(qa, ka, va, t, hb):
    B, H, S, _ = qa.shape
    nq = S // t

    def body(q_ref, k_ref, v_ref, o_ref, m_ref, acc_ref, kmax_ref, qs_ref):
        i = pl.program_id(2)
        lane = _lane((t, LANES))

        @pl.when(i == 0)
        def _():
            for e in range(hb):
                norms = jnp.where(_lane((S, LANES)) == L_KNORM, k_ref[0, e].astype(F32), 0.0)
                kmax_ref[e] = jnp.full((1, LANES), jnp.max(norms), F32)

        shifts = [_lane_col(q_ref[0, e].astype(F32), lane, L_QNORM) * kmax_ref[e][:, 0:1] for e in range(hb)]
        worst = shifts[0]
        for e in range(1, hb):
            worst = jnp.maximum(worst, shifts[e])
        bounded = jnp.max(worst) <= SHIFT_MAX
        acc_ref[...] = jnp.zeros_like(acc_ref)

        def tiles(step):
            def loop_body(j, carry):
                step(j, False)
                return carry

            lax.fori_loop(0, i, loop_body, 0)
            step(i, True)

        def keep_mask(n=t):
            return lax.broadcasted_iota(jnp.int32, (n, n), 0) >= lax.broadcasted_iota(jnp.int32, (n, n), 1)

        def finish(e, shift):
            acc = acc_ref[e]
            l = _lane_col(acc, lane, L_ROWSUM)
            o_ref[0, e] = jnp.where(lane < HEAD_DIM, acc / l, shift + jnp.log(l))

        @pl.when(bounded)
        def _():
            for e in range(hb):
                qs_ref[e] = _put3(q_ref[0, e].astype(F32), lane, L_LSE, _split3(-shifts[e])).astype(BF16)

            def pair(e, q_rows, k_start, n, masked):
                k_rows = pl.ds(pl.multiple_of(k_start, n), n)
                p = jnp.exp(_dot_nt(qs_ref[e, q_rows, :], k_ref[0, e, k_rows, :]))
                if masked:
                    p = jnp.where(keep_mask(n), p, 0.0)
                acc_ref[e, q_rows, :] = acc_ref[e, q_rows, :] + _dot(p.astype(BF16), v_ref[0, e, k_rows, :])

            def step(j, masked):
                for e in range(hb):
                    if masked:
                        h = t // 2
                        pair(e, slice(0, h), j * t, h, True)
                        pair(e, slice(h, t), j * t, h, False)
                        pair(e, slice(h, t), j * t + h, h, True)
                    else:
                        pair(e, slice(0, t), j * t, t, False)

            tiles(step)
            for e in range(hb):
                finish(e, shifts[e])

        @pl.when(jnp.logical_not(bounded))
        def _():
            m_ref[...] = jnp.full_like(m_ref, NEG_INF)

            def step(j, masked):
                rows = pl.ds(pl.multiple_of(j * t, t), t)
                for e in range(hb):
                    s = _dot_nt(q_ref[0, e], k_ref[0, e, rows, :])
                    if masked:
                        s = jnp.where(keep_mask(), s, NEG_INF)
                    m_prev = m_ref[e]
                    m_new = jnp.maximum(m_prev, jnp.max(s, axis=-1, keepdims=True))
                    alpha = jnp.exp(m_prev - m_new)
                    p = jnp.exp(s - m_new).astype(BF16)
                    acc_ref[e] = alpha * acc_ref[e] + _dot(p, v_ref[0, e, rows, :])
                    m_ref[e] = m_new

            tiles(step)
            for e in range(hb):
                finish(e, m_ref[e])

    return pl.pallas_call(
        body, name="attn_fwd", grid=(B, H // hb, nq),
        in_specs=[pl.BlockSpec((1, hb, t, LANES), lambda b, h, i: (b, h, i, 0)),
                  pl.BlockSpec((1, hb, S, LANES), lambda b, h, i: (b, h, 0, 0)),
                  pl.BlockSpec((1, hb, S, LANES), lambda b, h, i: (b, h, 0, 0))],
        out_specs=pl.BlockSpec((1, hb, t, LANES), lambda b, h, i: (b, h, i, 0)),
        out_shape=SDS((B, H, S, LANES), F32),
        scratch_shapes=[pltpu.VMEM((hb, t, 1), F32), pltpu.VMEM((hb, t, LANES), F32),
                        pltpu.VMEM((hb, 1, LANES), F32), pltpu.VMEM((hb, t, LANES), BF16)],
        compiler_params=_params(("parallel", "parallel", "arbitrary")),
    )(qa, ka, va)


def _fill_shifts(ext_ref, sh_ref):
    rows = sh_ref.shape[1]
    for b in range(1, SUBLANES):
        sh_ref[b - 1] = ext_ref[pl.ds(b, rows), :]


def _tap_window(ext_ref, sh_ref, off, tm, cols):
    b = off % SUBLANES
    if b == 0:
        return ext_ref[pl.ds(off, tm), cols]
    return sh_ref[b - 1, pl.ds(off - b, tm), cols]


def _conv_taps(w_ref, ext_ref, sh_ref, out_ref, n_taps, tm, offset_of, bias_ref=None):
    for cc in range(out_ref.shape[1] // LANES):
        cols = slice(cc * LANES, (cc + 1) * LANES)
        acc = None
        for j in sorted(range(n_taps), key=offset_of):
            term = w_ref[j:j + 1, cols] * _tap_window(ext_ref, sh_ref, offset_of(j), tm, cols)
            acc = term if acc is None else acc + term
        out_ref[:, cols] = acc if bias_ref is None else acc + bias_ref[:, cols]


def _layernorm_stats(u2):
    mu = jnp.mean(u2, axis=-1, keepdims=True)
    xc = u2 - mu
    rstd = lax.rsqrt(jnp.mean(xc * xc, axis=-1, keepdims=True) + EPS)
    return xc * rstd, rstd


def _fwd_out(oa, z, x2, tgt, conv_w, conv_b, ln_g, ln_b, w_out, B, S, H, n_taps, tm):
    T, D = x2.shape
    FW = H * HEAD_DIM
    CW = conv_w.shape[1]
    nsb = S // tm
    hb = tm // HALO
    mb = 4 if nsb % 4 == 0 else 1
    mt = mb * tm

    def body(oa_ref, gf_ref, ga_ref, gb_ref, gc_ref, ha_ref, hb_ref, x_ref, t_ref, w_ref, cb_ref, lg_ref,
             lb_ref, wo_ref, y_ref, u2_ref, a_ref, do_ref, dob_ref, dy_ref, loss_ref, ext_ref, sh_ref):
        first_step = (pl.program_id(0) == 0) & (pl.program_id(1) == 0)
        sub = lax.rem(pl.program_id(1), mb)
        rows = pl.ds(pl.multiple_of(sub * tm, tm), tm)

        @pl.when(first_step)
        def _():
            loss_ref[...] = jnp.zeros_like(loss_ref)

        u1 = ga_ref[...] * _sigmoid(gb_ref[...])
        halo = ha_ref[...] * _sigmoid(hb_ref[...])
        ext_ref[0:HALO, :] = jnp.where(pl.program_id(1) > 0, halo, 0.0)
        ext_ref[HALO:, :] = u1
        _fill_shifts(ext_ref, sh_ref)
        _conv_taps(w_ref, ext_ref, sh_ref, u2_ref, n_taps, tm, lambda j: HALO - (n_taps - 1) + j, cb_ref)
        uh, _ = _layernorm_stats(u2_ref[...])
        u3 = uh * lg_ref[...] + lb_ref[...]
        gc = gc_ref[...]
        yu = u3 * _sigmoid(u3) * (gc * _sigmoid(gc))
        y_ref[rows, FW:] = yu.astype(BF16)

        lane = _lane((tm, LANES))
        lo = lane < HEAD_DIM
        for p in range(H // 2):
            a_ref[:, p * LANES:(p + 1) * LANES] = jnp.where(
                lo, oa_ref[0, 2 * p], pltpu.roll(oa_ref[0, 2 * p + 1], HEAD_DIM, 1))
        gf = gf_ref[...]
        y_ref[rows, :FW] = (a_ref[...] * (gf * _sigmoid(gf))).astype(BF16)

        @pl.when(sub == mb - 1)
        def _():
            out = x_ref[...] + _dot(y_ref[...], wo_ref[...])
            diff = out - t_ref[...]
            loss_ref[...] = loss_ref[...] + jnp.sum(diff * diff)
            dout = diff * (1.0 / D)
            do_ref[...] = dout
            dob = dout.astype(BF16)
            dob_ref[...] = dob
            dy_ref[...] = _dot_nt(dob, wo_ref[...])

    row = lambda b, s: b * nsb + s
    zspec = lambda cb: pl.BlockSpec((tm, FW), lambda b, s: (row(b, s), cb))
    hspec = lambda cb: pl.BlockSpec((HALO, CW), lambda b, s: (jnp.maximum(row(b, s) * hb - 1, 0), cb))
    vspec = pl.BlockSpec((1, CW), lambda b, s: (0, 0))
    tspec = lambda w: pl.BlockSpec((tm, w), lambda b, s: (row(b, s), 0))
    mspec = lambda w: pl.BlockSpec((mt, w), lambda b, s: (row(b, s) // mb, 0))
    return pl.pallas_call(
        body, name="fwd_out", grid=(B, nsb),
        in_specs=[pl.BlockSpec((1, H, tm, LANES), lambda b, s: (b, 0, s, 0)),
                  zspec(3), zspec(4), zspec(5), zspec(6), hspec(4), hspec(5),
                  mspec(D), mspec(D),
                  pl.BlockSpec((HALO, CW), lambda b, s: (0, 0)), vspec, vspec, vspec,
                  pl.BlockSpec((FW + CW, D), lambda b, s: (0, 0))],
        out_specs=[mspec(FW + CW), tspec(CW), tspec(FW), mspec(D), mspec(D), mspec(FW + CW),
                   pl.BlockSpec((8, LANES), lambda b, s: (0, 0))],
        out_shape=[SDS((T, FW + CW), BF16), SDS((T, CW), F32), SDS((T, FW), F32), SDS((T, D), F32),
                   SDS((T, D), BF16), SDS((T, FW + CW), F32), SDS((8, LANES), F32)],
        scratch_shapes=[pltpu.VMEM((tm + HALO, CW), F32),
                        pltpu.VMEM((SUBLANES - 1, tm + HALO - SUBLANES, CW), F32)],
        compiler_params=_params(("arbitrary", "arbitrary")),
    )(oa, z, z, z, z, z, z, x2, tgt, conv_w, conv_b, ln_g, ln_b, w_out)


def _bwd_prep(dy, z, a_nat, oa, qa, u2, ln_g, ln_b, B, S, H, tm):
    T = B * S
    FW = H * HEAD_DIM
    CW = u2.shape[1]
    nsb = S // tm

    def body(dya_ref, dyu_ref, gf_ref, gc_ref, a_ref, oa_ref, qa_ref, u2_ref, lg_ref, lb_ref,
             dzgf_ref, dzgc_ref, du2_ref, doa_ref, qb_ref, sg_ref):
        first_step = (pl.program_id(0) == 0) & (pl.program_id(1) == 0)

        @pl.when(first_step)
        def _():
            sg_ref[...] = jnp.zeros_like(sg_ref)

        gf = gf_ref[...]
        sg = _sigmoid(gf)
        a = a_ref[...]
        dya = dya_ref[...]
        da = dya * (gf * sg)
        dzgf_ref[...] = (dya * a * (sg * (1.0 + gf * (1.0 - sg)))).astype(BF16)
        dd = da * a
        lane = _lane((tm, LANES))
        lo = lane < HEAD_DIM
        for p in range(H // 2):
            cols = slice(p * LANES, (p + 1) * LANES)
            da_p = da[:, cols]
            dd_p = dd[:, cols]
            d_heads = (jnp.sum(jnp.where(lo, dd_p, 0.0), axis=-1, keepdims=True),
                       jnp.sum(jnp.where(lo, 0.0, dd_p), axis=-1, keepdims=True))
            for e in range(2):
                da_e = da_p if e == 0 else pltpu.roll(da_p, HEAD_DIM, 1)
                d_e = d_heads[e]
                aug = _put3(jnp.zeros((tm, LANES), F32), lane, L_D, _split3(-d_e))
                doa_ref[0, 2 * p + e] = jnp.where(lo, da_e, aug).astype(BF16)
                lse = _lane_col(oa_ref[0, 2 * p + e], lane, L_ROWSUM)
                qb = _put3(qa_ref[0, 2 * p + e].astype(F32), lane, L_LSE, _split3(-lse))
                qb_ref[0, 2 * p + e] = qb.astype(BF16)

        gc = gc_ref[...]
        sc = _sigmoid(gc)
        dyu = dyu_ref[...]
        uh, rstd = _layernorm_stats(u2_ref[...])
        u3 = uh * lg_ref[...] + lb_ref[...]
        s3 = _sigmoid(u3)
        dzgc_ref[...] = (dyu * (u3 * s3) * (sc * (1.0 + gc * (1.0 - sc)))).astype(BF16)
        du3 = dyu * (gc * sc) * (s3 * (1.0 + u3 * (1.0 - s3)))
        sg_ref[0:1, :] = sg_ref[0:1, :] + jnp.sum(du3 * uh, axis=0, keepdims=True)
        sg_ref[1:2, :] = sg_ref[1:2, :] + jnp.sum(du3, axis=0, keepdims=True)
        duh = du3 * lg_ref[...]
        du2 = rstd * (duh - jnp.mean(duh, axis=-1, keepdims=True)
                      - uh * jnp.mean(duh * uh, axis=-1, keepdims=True))
        sg_ref[2:3, :] = sg_ref[2:3, :] + jnp.sum(du2, axis=0, keepdims=True)
        du2_ref[...] = du2

    row = lambda b, s: b * nsb + s
    tspec = lambda w, cb=0: pl.BlockSpec((tm, w), lambda b, s: (row(b, s), cb))
    hspec = pl.BlockSpec((1, H, tm, LANES), lambda b, s: (b, 0, s, 0))
    vspec = pl.BlockSpec((1, CW), lambda b, s: (0, 0))
    return pl.pallas_call(
        body, name="bwd_prep", grid=(B, nsb),
        in_specs=[tspec(FW, 0), tspec(CW, 1), tspec(FW, 3), tspec(CW, 6), tspec(FW), hspec, hspec,
                  tspec(CW), vspec, vspec],
        out_specs=[tspec(FW), tspec(CW), tspec(CW), hspec, hspec,
                   pl.BlockSpec((8, CW), lambda b, s: (0, 0))],
        out_shape=[SDS((T, FW), BF16), SDS((T, CW), BF16), SDS((T, CW), F32),
                   SDS((B, H, S, LANES), BF16), SDS((B, H, S, LANES), BF16), SDS((8, CW), F32)],
        compiler_params=_params(("arbitrary", "arbitrary")),
    )(dy, dy, z, z, a_nat, oa, qa, u2, ln_g, ln_b)


def _conv_bwd(du2, z, conv_w, B, S, n_taps, tm):
    T, CW = du2.shape
    nsb = S // tm
    hb = tm // HALO

    def body(d_ref, dh_ref, ga_ref, gb_ref, ha_ref, hb_ref, w_ref, dz_ref, dw_ref,
             extu_ref, extd_ref, shu_ref, shd_ref, du1_ref, dwacc_ref):
        s = pl.program_id(1)
        first_step = (pl.program_id(0) == 0) & (s == 0)
        last_step = (pl.program_id(0) == B - 1) & (s == nsb - 1)

        @pl.when(first_step)
        def _():
            dwacc_ref[...] = jnp.zeros_like(dwacc_ref)

        ga = ga_ref[...]
        sb = _sigmoid(gb_ref[...])
        halo = ha_ref[...] * _sigmoid(hb_ref[...])
        extu_ref[0:HALO, :] = jnp.where(s > 0, halo, 0.0)
        extu_ref[HALO:, :] = ga * sb
        extd_ref[0:tm, :] = d_ref[...]
        extd_ref[tm:, :] = jnp.where(s < nsb - 1, dh_ref[...], 0.0)
        _fill_shifts(extu_ref, shu_ref)
        _fill_shifts(extd_ref, shd_ref)
        _conv_taps(w_ref, extd_ref, shd_ref, du1_ref, n_taps, tm, lambda j: n_taps - 1 - j)
        for cc in range(CW // LANES):
            cols = slice(cc * LANES, (cc + 1) * LANES)
            parts = [None] * n_taps
            for r in range(tm // SUBLANES):
                dv = d_ref[r * SUBLANES:(r + 1) * SUBLANES, cols]
                for j in range(n_taps):
                    off = HALO - (n_taps - 1) + j + r * SUBLANES
                    term = dv * _tap_window(extu_ref, shu_ref, off, SUBLANES, cols)
                    parts[j] = term if parts[j] is None else parts[j] + term
            for j in range(n_taps):
                rows = slice(j * SUBLANES, (j + 1) * SUBLANES)
                dwacc_ref[rows, cols] = dwacc_ref[rows, cols] + parts[j]
        du1 = du1_ref[...]
        dz_ref[:, :CW] = (du1 * sb).astype(BF16)
        dz_ref[:, CW:] = (du1 * ga * (sb * (1.0 - sb))).astype(BF16)

        @pl.when(last_step)
        def _():
            dw_ref[...] = jnp.zeros_like(dw_ref)
            for j in range(n_taps):
                dw_ref[j:j + 1, :] = jnp.sum(dwacc_ref[j * SUBLANES:(j + 1) * SUBLANES, :], axis=0, keepdims=True)

    row = lambda b, s: b * nsb + s
    last_halo = T // HALO - 1
    return pl.pallas_call(
        body, name="conv_bwd", grid=(B, nsb),
        in_specs=[pl.BlockSpec((tm, CW), lambda b, s: (row(b, s), 0)),
                  pl.BlockSpec((HALO, CW), lambda b, s: (jnp.minimum((row(b, s) + 1) * hb, last_halo), 0)),
                  pl.BlockSpec((tm, CW), lambda b, s: (row(b, s), 4)),
                  pl.BlockSpec((tm, CW), lambda b, s: (row(b, s), 5)),
                  pl.BlockSpec((HALO, CW), lambda b, s: (jnp.maximum(row(b, s) * hb - 1, 0), 4)),
                  pl.BlockSpec((HALO, CW), lambda b, s: (jnp.maximum(row(b, s) * hb - 1, 0), 5)),
                  pl.BlockSpec((HALO, CW), lambda b, s: (0, 0))],
        out_specs=[pl.BlockSpec((tm, 2 * CW), lambda b, s: (row(b, s), 0)),
                   pl.BlockSpec((HALO, CW), lambda b, s: (0, 0))],
        out_shape=[SDS((T, 2 * CW), BF16), SDS((HALO, CW), F32)],
        scratch_shapes=[pltpu.VMEM((tm + HALO, CW), F32), pltpu.VMEM((tm + HALO, CW), F32),
                        pltpu.VMEM((SUBLANES - 1, tm + HALO - SUBLANES, CW), F32),
                        pltpu.VMEM((SUBLANES - 1, tm + HALO - SUBLANES, CW), F32),
                        pltpu.VMEM((tm, CW), F32), pltpu.VMEM((HALO * SUBLANES, CW), F32)],
        compiler_params=_params(("arbitrary", "arbitrary")),
    )(du2, du2, z, z, z, z, conv_w)


def _attn_bwd(qb, ka, va, doa, t, hb):
    B, H, S, _ = qb.shape
    nk = S // t

    def body(q_ref, k_ref, v_ref, do_ref, dq_ref, dk_ref, dv_ref, dv_acc):
        j = pl.program_id(2)

        @pl.when(j == 0)
        def _():
            dq_ref[...] = jnp.zeros_like(dq_ref)

        dk_ref[...] = jnp.zeros_like(dk_ref)
        dv_acc[...] = jnp.zeros_like(dv_acc)

        def step(i, masked):
            q_rows = pl.ds(pl.multiple_of(i * t, t), t)
            for e in range(hb):
                k = k_ref[0, e]
                q = q_ref[0, e, q_rows, :]
                do = do_ref[0, e, q_rows, :]
                p = jnp.exp(_dot_nt(q, k))
                if masked:
                    keep = lax.broadcasted_iota(jnp.int32, (t, t), 0) >= lax.broadcasted_iota(jnp.int32, (t, t), 1)
                    p = jnp.where(keep, p, 0.0)
                ds = (p * _dot_nt(do, v_ref[0, e])).astype(BF16)
                dv_acc[e] = dv_acc[e] + _dot_tn(p.astype(BF16), do)
                dk_ref[0, e] = dk_ref[0, e] + _dot_tn(ds, q)
                dq_ref[0, e, q_rows, :] = dq_ref[0, e, q_rows, :] + _dot(ds, k)

        step(j, True)

        def loop_body(i, carry):
            step(i, False)
            return carry

        lax.fori_loop(j + 1, nk, loop_body, 0)
        dv_ref[0] = dv_acc[...].astype(BF16)

    full = pl.BlockSpec((1, hb, S, LANES), lambda b, h, j: (b, h, 0, 0))
    blk = pl.BlockSpec((1, hb, t, LANES), lambda b, h, j: (b, h, j, 0))
    oshape = SDS((B, H, S, LANES), F32)
    return pl.pallas_call(
        body, name="attn_bwd", grid=(B, H // hb, nk),
        in_specs=[full, blk, blk, full],
        out_specs=[full, blk, blk],
        out_shape=[oshape, oshape, SDS((B, H, S, LANES), BF16)],
        scratch_shapes=[pltpu.VMEM((hb, t, LANES), F32)],
        compiler_params=_params(("parallel", "parallel", "arbitrary")),
    )(qb, ka, va, doa)


def _qk_bwd(dqa, dka, dva, z, gq, gk, B, S, H, tm):
    T = B * S
    FW = H * HEAD_DIM
    nsb = S // tm
    nfb = FW // LANES
    scale = HEAD_DIM ** -0.5

    def body(dq_ref, dk_ref, dv_ref, zq_ref, zk_ref, gq_ref, gk_ref, dzq_ref, dzk_ref, dzv_ref, dc_ref, dg_ref):
        p = pl.program_id(0)

        @pl.when(pl.program_id(1) == 0)
        def _():
            dg_ref[...] = jnp.zeros_like(dg_ref)

        lane = _lane((tm, LANES))
        lo = lane < HEAD_DIM

        def natural(ref):
            return jnp.where(lo, ref[0, 0].astype(F32), pltpu.roll(ref[0, 1].astype(F32), HEAD_DIM, 1))

        def norm_bwd(dn, x, g, row, out_ref):
            r = lax.rsqrt(_half_stats(x * x) * (1.0 / HEAD_DIM) + EPS)
            xh = x * r
            dg_ref[row:row + 1, :] = dg_ref[row:row + 1, :] + jnp.sum(dn * xh, axis=0, keepdims=True)
            dxh = dn * g
            mm = _half_stats(dxh * xh) * (1.0 / HEAD_DIM)
            out_ref[...] = (r * (dxh - xh * mm)).astype(BF16)

        norm_bwd(natural(dq_ref) * scale, zq_ref[...], gq_ref[...], 0, dzq_ref)
        norm_bwd(natural(dk_ref), zk_ref[...], gk_ref[...], 1, dzk_ref)
        dzv_ref[...] = natural(dv_ref).astype(BF16)

        dc = jnp.zeros((tm, LANES), F32)
        for e in range(2):
            val = _lane_col(dq_ref[0, e], lane, L_ROWSUM) - _lane_col(dk_ref[0, e], lane, L_KDECAY)
            dc = jnp.where(lane == 2 * p + e, val, dc)
        dc_ref[0] = dc

    hspec = pl.BlockSpec((1, 2, tm, LANES), lambda p, i: (i // nsb, p, i % nsb, 0))
    zspec = lambda off: pl.BlockSpec((tm, LANES), lambda p, i: (i, off + p))
    gspec = pl.BlockSpec((1, LANES), lambda p, i: (0, p))
    ospec = pl.BlockSpec((tm, LANES), lambda p, i: (i, p))
    return pl.pallas_call(
        body, name="qk_bwd", grid=(H // 2, T // tm),
        in_specs=[hspec, hspec, hspec, zspec(0), zspec(nfb), gspec, gspec],
        out_specs=[ospec, ospec, ospec,
                   pl.BlockSpec((1, tm, LANES), lambda p, i: (p, i, 0)),
                   pl.BlockSpec((8, LANES), lambda p, i: (0, p))],
        out_shape=[SDS((T, FW), BF16), SDS((T, FW), BF16), SDS((T, FW), BF16),
                   SDS((H // 2, T, LANES), F32), SDS((8, FW), F32)],
        compiler_params=_params(("parallel", "arbitrary")),
    )(dqa, dka, dva, z, z, gq, gk)


def _gate_bwd(dc8, z, b_pad, B, S, H, col_blk, fp, tc):
    T = B * S
    nsb = S // tc
    npair = dc8.shape[0]

    def body(dc_ref, zf_ref, b_ref, dz_ref, db_ref, carry):
        first_step = (pl.program_id(0) == 0) & (pl.program_id(1) == 0)

        @pl.when(first_step)
        def _():
            db_ref[...] = jnp.zeros_like(db_ref)

        @pl.when(pl.program_id(1) == 0)
        def _():
            carry[...] = jnp.zeros_like(carry)

        dc = dc_ref[0]
        for k in range(1, npair):
            dc = dc + dc_ref[k]
        dlf = _tri_cumsum(dc, True) + carry[...]
        carry[...] = carry[...] + jnp.sum(dc, axis=0, keepdims=True)
        x = zf_ref[...] + b_ref[...]
        dlogit = dlf * _sigmoid(-x)
        db_ref[0:1, :] = db_ref[0:1, :] + jnp.sum(dlogit, axis=0, keepdims=True)
        dz_ref[...] = jnp.zeros_like(dz_ref)
        dz_ref[:, :LANES] = dlogit.astype(BF16)

    rrow = lambda b, s: b * nsb + (nsb - 1 - s)
    return pl.pallas_call(
        body, name="gate_bwd", grid=(B, nsb),
        in_specs=[pl.BlockSpec((npair, tc, LANES), lambda b, s: (0, rrow(b, s), 0)),
                  pl.BlockSpec((tc, LANES), lambda b, s: (rrow(b, s), col_blk)),
                  pl.BlockSpec((1, LANES), lambda b, s: (0, 0))],
        out_specs=[pl.BlockSpec((tc, fp), lambda b, s: (rrow(b, s), 0)),
                   pl.BlockSpec((8, LANES), lambda b, s: (0, 0))],
        out_shape=[SDS((T, fp), BF16), SDS((8, LANES), F32)],
        scratch_shapes=[pltpu.VMEM((1, LANES), F32)],
        compiler_params=_params(("arbitrary", "arbitrary")),
    )(dc8, z, b_pad)


def _matmul_tn(a, b, name, tmm, tn, tk):
    T, M = a.shape
    N = b.shape[1]
    tmm, tn, tk = min(tmm, M), min(tn, N), min(tk, T)

    def body(a_ref, b_ref, o_ref):
        @pl.when(pl.program_id(2) == 0)
        def _():
            o_ref[...] = jnp.zeros_like(o_ref)

        o_ref[...] = o_ref[...] + _dot_tn(a_ref[...], b_ref[...])

    return pl.pallas_call(
        body, name=name, grid=(M // tmm, N // tn, T // tk),
        in_specs=[pl.BlockSpec((tk, tmm), lambda i, j, k: (k, i)),
                  pl.BlockSpec((tk, tn), lambda i, j, k: (k, j))],
        out_specs=pl.BlockSpec((tmm, tn), lambda i, j, k: (i, j)),
        out_shape=SDS((M, N), F32),
        compiler_params=_params(("parallel", "parallel", "arbitrary")),
    )(a, b)


def _dh_rms_bwd(pieces, w_t, x2, g, dout, tm, tk, parts):
    T, D = x2.shape
    nks = [p.shape[1] // tk for p in pieces]
    starts = [sum(nks[:k]) for k in range(len(pieces))]
    nk = sum(nks)
    ni = T // tm
    n = len(parts)

    def body(*refs):
        dz_refs = refs[:len(pieces)]
        w_ref, x_ref, g_ref, do_ref = refs[len(pieces):len(pieces) + 4]
        part_refs = refs[len(pieces) + 4:len(pieces) + 4 + n]
        gx_ref, dg_ref = refs[len(pieces) + 4 + n:len(pieces) + 6 + n]
        slot_refs = refs[len(pieces) + 6 + n:len(pieces) + 6 + 2 * n]
        acc_ref, send_sems, recv_sems = refs[len(pieces) + 6 + 2 * n:]
        k = pl.program_id(1)
        first_step = (pl.program_id(0) == 0) & (k == 0)
        last_step = (pl.program_id(0) == ni - 1) & (k == nk - 1)
        x, y, c = _place()
        chips = [(1 - x, y), (x, 1 - y), (1 - x, 1 - y)]

        def copy(a, f, to):
            cx, cy = chips[f]
            return pltpu.make_async_remote_copy(
                src_ref=part_refs[a].at[2 * cx + cy], dst_ref=slot_refs[a].at[f],
                send_sem=send_sems.at[a * 3 + f], recv_sem=recv_sems.at[a * 3 + f],
                device_id=to, device_id_type=MESH)

        @pl.when(first_step)
        def _():
            dg_ref[...] = jnp.zeros_like(dg_ref)
            for a in range(n):
                for f in range(3):
                    copy(a, f, (*chips[f], c)).start()

        @pl.when(last_step)
        def _():
            for a in range(n):
                for f in range(3):
                    copy(a, f, (x, y, c)).wait_recv()
            for a in range(n):
                for f in range(3):
                    copy(a, f, (*chips[f], c)).wait_send()

        @pl.when(k == 0)
        def _():
            acc_ref[...] = jnp.zeros_like(acc_ref)

        for dz_ref, st, cnt in zip(dz_refs, starts, nks):
            @pl.when((k >= st) & (k < st + cnt))
            def _(dz_ref=dz_ref):
                acc_ref[...] = acc_ref[...] + _dot(dz_ref[...], w_ref[...])

        @pl.when(k == nk - 1)
        def _():
            x = x_ref[...]
            r = lax.rsqrt(jnp.mean(x * x, axis=-1, keepdims=True) + EPS)
            xh = x * r
            dh = acc_ref[...]
            dg_ref[0:1, :] = dg_ref[0:1, :] + jnp.sum(dh * xh, axis=0, keepdims=True)
            dxn = dh * g_ref[...]
            gx_ref[...] = do_ref[...] + r * (dxn - xh * jnp.mean(dxn * xh, axis=-1, keepdims=True))

    def piece_spec(st, cnt):
        return pl.BlockSpec((tm, tk), lambda i, k: (i, jnp.clip(k - st, 0, cnt - 1)))

    tspec = pl.BlockSpec((tm, D), lambda i, k: (i, 0))
    return pl.pallas_call(
        body, name="dh_rms_bwd", grid=(T // tm, nk),
        in_specs=[piece_spec(st, cnt) for st, cnt in zip(starts, nks)]
        + [pl.BlockSpec((tk, D), lambda i, k: (k, 0)), tspec, pl.BlockSpec((1, D), lambda i, k: (0, 0)), tspec]
        + [ANY] * n,
        out_specs=[tspec, pl.BlockSpec((8, D), lambda i, k: (0, 0))] + [ANY] * n,
        out_shape=[SDS((T, D), F32), SDS((8, D), F32)] + [SDS((3,) + p.shape[1:], p.dtype) for p in parts],
        scratch_shapes=[pltpu.VMEM((tm, D), F32),
                        pltpu.SemaphoreType.DMA((3 * n,)), pltpu.SemaphoreType.DMA((3 * n,))],
        compiler_params=_params(("arbitrary", "arbitrary")),
    )(*pieces, w_t, x2, g, dout, *parts)


def _block_plan(R, C, tr, tc):
    br = min(tr, R)
    if R % br == 0:
        return (br, C), R // br, lambda i: (i, 0)
    bc = min(tc, C)
    assert C % bc == 0
    return (R, bc), C // bc, lambda i: (0, i)


def _ew_call(body, name, ins, n_out, out_dtypes, tr, tc):
    R, C = ins[0].shape
    blk, steps, imap = _block_plan(R, C, tr, tc)
    spec = pl.BlockSpec(blk, imap)
    return pl.pallas_call(
        body, name=name, grid=(steps,),
        in_specs=[spec] * len(ins), out_specs=[spec] * n_out,
        out_shape=[SDS((R, C), dt) for dt in out_dtypes],
        compiler_params=_params(("parallel",)),
    )(*ins)


def _sum_slots(slots, name, first=None, tr=256):
    n, R, C = slots.shape
    blk, steps, imap = _block_plan(R, C, tr, 2 * LANES)
    lead = [] if first is None else [first]

    def body(*refs):
        s_ref, o_ref = refs[-2:]
        acc = refs[0][...].astype(F32) if lead else s_ref[0].astype(F32)
        for k in range(0 if lead else 1, n):
            acc = acc + s_ref[k].astype(F32)
        o_ref[...] = acc

    return pl.pallas_call(
        body, name=name, grid=(steps,),
        in_specs=[pl.BlockSpec(blk, imap)] * len(lead) + [pl.BlockSpec((n,) + blk, lambda i: (0,) + imap(i))],
        out_specs=pl.BlockSpec(blk, imap),
        out_shape=SDS((R, C), F32),
        compiler_params=_params(("parallel",)),
    )(*lead, slots)


def _adamw_update(w, g, m, v):
    nm = ADAM_B1 * m + (1.0 - ADAM_B1) * g
    nv = ADAM_B2 * v + (1.0 - ADAM_B2) * (g * g)
    m_hat = nm / (1.0 - ADAM_B1 ** ADAM_STEP)
    v_hat = nv / (1.0 - ADAM_B2 ** ADAM_STEP)
    return -ADAM_LR * (m_hat / (jnp.sqrt(v_hat) + ADAM_EPS) + ADAM_WD * w), nm, nv


def _adamw(w, g, m, v, name):
    def body(w_ref, g_ref, m_ref, v_ref, d_ref, nm_ref, nv_ref):
        d_ref[...], nm_ref[...], nv_ref[...] = _adamw_update(w_ref[...], g_ref[...], m_ref[...], v_ref[...])

    return _ew_call(body, name, [w, g, m, v], 3, [F32, F32, F32], 128, 2 * LANES)


def _adamw_halves(w, mine, other, m, v, name):
    R, C = w.shape
    half = C // 2
    bc = min(2 * LANES, half)
    per = half // bc

    def body(w_ref, a_ref, b_ref, m_ref, v_ref, g_ref, d_ref, nm_ref, nv_ref):
        g = jnp.where(pl.program_id(0) // per == lax.axis_index("c"), a_ref[...], b_ref[...])
        g_ref[...] = g
        d_ref[...], nm_ref[...], nv_ref[...] = _adamw_update(w_ref[...], g, m_ref[...], v_ref[...])

    full = pl.BlockSpec((R, bc), lambda i: (0, i))
    part = pl.BlockSpec((R, bc), lambda i: (0, i % per))
    return pl.pallas_call(
        body, name=name, grid=(C // bc,),
        in_specs=[full, part, part, full, full], out_specs=[full] * 4,
        out_shape=[SDS((R, C), F32)] * 4,
        compiler_params=_params(("parallel",)),
    )(w, mine, other, m, v)


ANY = pl.BlockSpec(memory_space=pl.ANY)


def _place():
    return lax.axis_index("x"), lax.axis_index("y"), lax.axis_index("c")


def _gather_chips(shards, splits):
    n = len(shards)
    per = 7

    def body(*refs):
        ins, outs = refs[:n], refs[n:2 * n]
        send_sems, recv_sems = refs[2 * n:]
        x, y, c = _place()
        mine = 2 * x + y
        chips = [(1 - x, y), (x, 1 - y), (1 - x, 1 - y)]

        def rows(a, half):
            return pl.ds(0, splits[a]) if half == 0 else pl.ds(splits[a], ins[a].shape[0] - splits[a])

        def copy(a, k, chip_idx, half, to, src=None):
            dst = outs[a].at[chip_idx, rows(a, half)]
            return pltpu.make_async_remote_copy(
                src_ref=dst if src is None else src, dst_ref=dst,
                send_sem=send_sems.at[a * per + k], recv_sem=recv_sems.at[a * per + k],
                device_id=to, device_id_type=MESH)

        def own(a, to):
            return pltpu.make_async_remote_copy(
                src_ref=ins[a], dst_ref=outs[a].at[mine],
                send_sem=send_sems.at[a * per + 6], recv_sem=recv_sems.at[a * per + 6],
                device_id=to, device_id_type=MESH)

        for cc in (0, 1):
            @pl.when(c == cc)
            def _(cc=cc):
                me, sibling = (x, y, cc), (x, y, 1 - cc)
                first = [copy(a, k, mine, cc, (*chip, cc), src=ins[a].at[rows(a, cc)])
                         for a in range(n) for k, chip in enumerate(chips)]
                first += [own(a, sibling) for a in range(n)]
                for cp in first:
                    cp.start()
                passed = []
                for k, (cx, cy) in enumerate(chips):
                    for a in range(n):
                        copy(a, k, 2 * cx + cy, cc, me).wait_recv()
                        fwd = copy(a, 3 + k, 2 * cx + cy, cc, sibling)
                        fwd.start()
                        passed.append(fwd)
                for k, (cx, cy) in enumerate(chips):
                    for a in range(n):
                        copy(a, 3 + k, 2 * cx + cy, 1 - cc, me).wait_recv()
                for a in range(n):
                    own(a, me).wait_recv()
                for cp in first + passed:
                    cp.wait_send()

    return pl.pallas_call(
        body, name="gather_chips",
        in_specs=[ANY] * n, out_specs=[ANY] * n,
        out_shape=[SDS((4,) + s.shape, s.dtype) for s in shards],
        scratch_shapes=[pltpu.SemaphoreType.DMA((per * n,)), pltpu.SemaphoreType.DMA((per * n,))],
    )(*shards)


def _pair_swap(arrs):
    n = len(arrs)

    def body(*refs):
        ins, outs = refs[:n], refs[n:2 * n]
        send_sems, recv_sems = refs[2 * n:]
        x, y, c = _place()
        for cc in (0, 1):
            @pl.when(c == cc)
            def _(cc=cc):
                copies = []
                for a in range(n):
                    half = ins[a].shape[1] // 2
                    copies.append(pltpu.make_async_remote_copy(
                        src_ref=ins[a].at[:, pl.ds((1 - cc) * half, half)], dst_ref=outs[a],
                        send_sem=send_sems.at[a], recv_sem=recv_sems.at[a],
                        device_id=(x, y, 1 - cc), device_id_type=MESH))
                for cp in copies:
                    cp.start()
                for cp in copies:
                    cp.wait()

    return pl.pallas_call(
        body, name="pair_swap",
        in_specs=[ANY] * n, out_specs=[ANY] * n,
        out_shape=[SDS((h.shape[0], h.shape[1] // 2), h.dtype) for h in arrs],
        scratch_shapes=[pltpu.SemaphoreType.DMA((n,)), pltpu.SemaphoreType.DMA((n,))],
    )(*arrs)


def _pair_sum(arrs, got, core, tr):
    half = arrs[0].shape[1] // 2
    cnts = [p.shape[0] // tr for p in arrs]
    starts = [sum(cnts[:k]) for k in range(len(arrs))]

    def body(core_ref, *refs):
        del core_ref
        own_refs, got_refs, o_ref = refs[:len(arrs)], refs[len(arrs):2 * len(arrs)], refs[-1]
        s = pl.program_id(0)
        for own_ref, got_ref, st, cnt in zip(own_refs, got_refs, starts, cnts):
            @pl.when((s >= st) & (s < st + cnt))
            def _(own_ref=own_ref, got_ref=got_ref):
                o_ref[...] = (own_ref[...] + got_ref[...]).astype(BF16)

    def own_spec(st, cnt):
        return pl.BlockSpec((tr, half), lambda s, core_ref: (jnp.clip(s - st, 0, cnt - 1), core_ref[0]))

    def got_spec(st, cnt):
        return pl.BlockSpec((tr, half), lambda s, core_ref: (jnp.clip(s - st, 0, cnt - 1), 0))

    return pl.pallas_call(
        body, name="pair_sum",
        grid_spec=pltpu.PrefetchScalarGridSpec(
            num_scalar_prefetch=1, grid=(sum(cnts),),
            in_specs=[own_spec(st, cnt) for st, cnt in zip(starts, cnts)]
            + [got_spec(st, cnt) for st, cnt in zip(starts, cnts)],
            out_specs=pl.BlockSpec((tr, half), lambda s, core_ref: (s, 0))),
        out_shape=SDS((sum(cnts) * tr, half), BF16),
        compiler_params=_params(("arbitrary",)),
    )(core, *arrs, *got)


def _share_results(arrs, rows):
    n = len(arrs)
    flips = [(fx, fy, fc) for fx in (0, 1) for fy in (0, 1) for fc in (0, 1)][1:]

    def body(*refs):
        ins, rows_ref, outs, all_ref = refs[:n], refs[n], refs[n + 1:2 * n + 1], refs[2 * n + 1]
        send_sems, recv_sems, local_sem = refs[2 * n + 2:]
        x, y, c = _place()
        me = 4 * x + 2 * y + c
        local = pltpu.make_async_copy(rows_ref, all_ref.at[me], local_sem)
        local.start()
        copies = [pltpu.make_async_remote_copy(
            src_ref=ins[a], dst_ref=outs[a], send_sem=send_sems.at[a], recv_sem=recv_sems.at[a],
            device_id=(x, y, 1 - c), device_id_type=MESH) for a in range(n)]
        for k, (fx, fy, fc) in enumerate(flips):
            copies.append(pltpu.make_async_remote_copy(
                src_ref=rows_ref, dst_ref=all_ref.at[me], send_sem=send_sems.at[n + k], recv_sem=recv_sems.at[n + k],
                device_id=(x ^ fx, y ^ fy, c ^ fc), device_id_type=MESH))
        for cp in copies:
            cp.start()
        for cp in copies[:n]:
            cp.wait_recv()
        for k, (fx, fy, fc) in enumerate(flips):
            src = 4 * (x ^ fx) + 2 * (y ^ fy) + (c ^ fc)
            pltpu.make_async_remote_copy(
                src_ref=rows_ref, dst_ref=all_ref.at[src], send_sem=send_sems.at[n + k], recv_sem=recv_sems.at[n + k],
                device_id=(x, y, c), device_id_type=MESH).wait_recv()
        for cp in copies:
            cp.wait_send()
        local.wait()

    outs = pl.pallas_call(
        body, name="share_results",
        in_specs=[ANY] * (n + 1), out_specs=[ANY] * (n + 1),
        out_shape=[SDS(h.shape, h.dtype) for h in arrs] + [SDS((8,) + rows.shape, rows.dtype)],
        scratch_shapes=[pltpu.SemaphoreType.DMA((n + 7,)), pltpu.SemaphoreType.DMA((n + 7,)),
                        pltpu.SemaphoreType.DMA],
    )(*arrs, rows)
    return outs[:n], outs[n]


def _tiles(S, FW):
    big = FW % 512 == 0
    return dict(
        fp=512 if big else LANES,
        tn=1536 if big else LANES,
        tm_in=min(1024, S),
        t_attn=min(512, S),
        hb_fwd=4,
        hb_bwd=2,
        tm_prep=min(512, S),
        tm_mix=min(128, S),
        tc=min(256, S),
        tk=512 if big else LANES,
    )


def kernel(x, norm_g, w_in, b_forget, q_norm_g, k_norm_g, conv_w, conv_b, conv_ln_g, conv_ln_b, w_out, loss_target, m_norm_g, m_w_in, m_b_forget, m_q_norm_g, m_k_norm_g, m_conv_w, m_conv_b, m_conv_ln_g, m_conv_ln_b, m_w_out, v_norm_g, v_w_in, v_b_forget, v_q_norm_g, v_k_norm_g, v_conv_w, v_conv_b, v_conv_ln_g, v_conv_ln_b, v_w_out):
    B, S, D = x.shape
    H, dh = q_norm_g.shape[1:]
    FW = H * dh
    CW = conv_b.shape[-1]
    n_taps, cw_shard = conv_w.shape[1:]
    in_shard = w_in.shape[2]
    out_shard = w_out.shape[1]
    assert dh == HEAD_DIM and H % 2 == 0 and H <= LANES and FW == CW == D
    assert n_taps - 1 <= HALO and 4 * cw_shard == CW and 4 * out_shard == FW + CW
    assert 4 * in_shard == 4 * FW + 3 * CW + H
    T = B * S
    tl = _tiles(S, FW)
    fp = tl["fp"]
    xi, yi, ci = _place()

    w_t = jnp.transpose(w_in[0])
    conv_pad = jnp.pad(conv_w[0], ((0, HALO - n_taps), (0, 0)))
    bf16_rows = 2 * SUBLANES
    g_in, g_out, g_cw = _gather_chips(
        [w_t.astype(BF16), w_out[0].astype(BF16), conv_pad],
        [in_shard // 2 // bf16_rows * bf16_rows, out_shard // 2, HALO // 2])
    w_out_full = g_out.reshape(FW + CW, D)
    conv_full = g_cw.transpose(1, 0, 2).reshape(HALO, CW)
    o_f = 3 * FW

    def stacked_rows(stack, lo, hi):
        n = stack.shape[1]
        return [stack[j, max(lo, j * n) - j * n:min(hi, (j + 1) * n) - j * n]
                for j in range(4) if max(lo, j * n) < min(hi, (j + 1) * n)]

    w_pack = jnp.concatenate(
        stacked_rows(g_in, 0, o_f) + stacked_rows(g_in, o_f + H, 4 * in_shard) + stacked_rows(g_in, o_f, o_f + H)
        + [jnp.zeros((fp - H, D), BF16)], axis=0)
    f_col = 4 * FW + 3 * CW

    x2 = x.reshape(T, D)
    tgt = loss_target.reshape(T, D)
    b_pad = jnp.pad(b_forget, ((0, 0), (0, LANES - H)))
    gq = q_norm_g.reshape(1, FW)
    gk = k_norm_g.reshape(1, FW)

    z, h = _fwd_in(x2, norm_g, w_pack, tl["tm_in"], tl["tn"])
    c = _gate_fwd(z, b_pad, B, S, H, f_col // LANES, tl["tc"])
    qa, ka, va = _attn_prep(z, c, gq, gk, B, S, H, tl["tm_prep"])
    oa = _attn_fwd(qa, ka, va, tl["t_attn"], tl["hb_fwd"])
    y, u2, a_nat, dout, dout_b, dy, loss_acc = _fwd_out(
        oa, z, x2, tgt, conv_full, conv_b, conv_ln_g, conv_ln_b, w_out_full, B, S, H, n_taps, tl["tm_mix"])

    dzgf, dzgc, du2, doa, qb, sg_conv = _bwd_prep(dy, z, a_nat, oa, qa, u2, conv_ln_g, conv_ln_b, B, S, H, tl["tm_mix"])
    dzglu, dconv_w = _conv_bwd(du2, z, conv_full, B, S, n_taps, tl["tm_mix"])
    dqa, dka, dva = _attn_bwd(qb, ka, va, doa, tl["t_attn"], tl["hb_bwd"])
    dzq, dzk, dzv, dc8, dg_qk = _qk_bwd(dqa, dka, dva, z, gq, gk, B, S, H, tl["tm_prep"])
    dzf, db_f = _gate_bwd(dc8, z, b_pad, B, S, H, f_col // LANES, fp, tl["tc"])
    pieces = [dzq, dzk, dzv, dzgf, dzglu, dzgc, dzf]
    dw_all = [_matmul_tn(p, h, f"dw_in_{k}", 1024, 1024, 1024) for k, p in enumerate(pieces)]
    dw_all.append(_matmul_tn(y, dout_b, "dw_out", 1024, 1024, 1024))

    summed = _pair_sum(dw_all, _pair_swap(dw_all), ci.astype(jnp.int32).reshape(1), fp)
    ends = [0]
    for t in dw_all:
        ends.append(ends[-1] + t.shape[0])
    spans, at = [], 0
    for k, rows in [(0, FW), (1, FW), (2, FW), (6, H), (3, FW), (4, 2 * CW), (5, CW)]:
        spans.append((at, rows, ends[k]))
        at += rows

    def chip_rows(j):
        lo, hi = j * in_shard, (j + 1) * in_shard
        return jnp.concatenate([summed[src + max(lo, a) - a:src + min(hi, a + n) - a]
                                for a, n, src in spans if max(lo, a) < min(hi, a + n)], axis=0)

    part_in = jnp.stack([chip_rows(j) for j in range(4)])
    part_out = summed[ends[7]:ends[8]].reshape(4, out_shard, D // 2)
    grad_x2, dg_norm, slots_in, slots_out = _dh_rms_bwd(
        pieces, w_pack, x2, norm_g, dout, tl["tm_in"], tl["tk"], [part_in, part_out])
    chip = 2 * xi + yi
    half_in = _sum_slots(slots_in, "chip_sum_in", lax.dynamic_index_in_dim(part_in, chip, 0, keepdims=False))
    half_out = _sum_slots(slots_out, "chip_sum_out", lax.dynamic_index_in_dim(part_out, chip, 0, keepdims=False))
    lanes_to_d = lambda t: jnp.pad(t, ((0, 0), (0, D - LANES)))
    small = jnp.concatenate([
        dg_norm[0:1], lanes_to_d(db_f[0:1, :]), dg_qk[0:1], dg_qk[1:2],
        sg_conv[2:3], sg_conv[0:1], sg_conv[1:2], dconv_w, lanes_to_d(loss_acc[0:1, :])], axis=0)
    n_small = small.shape[0]
    (other_in, other_out), all_small = _share_results([half_in, half_out], small)

    small_sum = _sum_slots(all_small, "small_sum", tr=n_small)
    loss = 0.5 * small_sum[n_small - 1, 0] / D
    grad_norm_g, grad_b_f = small_sum[0:1], small_sum[1:2, :H]
    grad_gq, grad_gk = small_sum[2:3].reshape(1, H, dh), small_sum[3:4].reshape(1, H, dh)
    grad_conv_b, grad_ln_g, grad_ln_b = small_sum[4:5], small_sum[5:6], small_sum[6:7]
    grad_conv_w = lax.dynamic_slice_in_dim(small_sum[7:7 + n_taps], chip * cw_shard, cw_shard, axis=1)

    in_t = _adamw_halves(w_t, half_in, other_in, jnp.transpose(m_w_in[0]), jnp.transpose(v_w_in[0]), "adamw_in")
    grad_w_in, d_in, nm_in, nv_in = (jnp.transpose(t)[None] for t in in_t)
    grad_w_out, d_out, nm_out, nv_out = (
        t[None] for t in _adamw_halves(w_out[0], half_out, other_out, m_w_out[0], v_w_out[0], "adamw_out"))
    d_cw, nm_cw, nv_cw = (t[None] for t in _adamw(conv_w[0], grad_conv_w, m_conv_w[0], v_conv_w[0], "adamw_conv_w"))

    def rows(ws):
        return jnp.concatenate([jnp.pad(t.reshape(1, -1), ((0, 0), (0, D - t.size))) for t in ws], axis=0)

    small_w = [norm_g, b_forget, q_norm_g, k_norm_g, conv_b, conv_ln_g, conv_ln_b]
    small_m = [m_norm_g, m_b_forget, m_q_norm_g, m_k_norm_g, m_conv_b, m_conv_ln_g, m_conv_ln_b]
    small_v = [v_norm_g, v_b_forget, v_q_norm_g, v_k_norm_g, v_conv_b, v_conv_ln_g, v_conv_ln_b]
    d_s, nm_s, nv_s = _adamw(rows(small_w), small_sum[0:7], rows(small_m), rows(small_v), "adamw_small")

    def unpack(t):
        return [t[k:k + 1, :w.size].reshape(w.shape) for k, w in enumerate(small_w)]

    def order(s, in_, cw, out_):
        ng, bf, qg, kg, cb, lg, lb = s
        return [ng, in_, bf, qg, kg, cw, cb, lg, lb, out_]

    grads = [grad_norm_g, grad_w_in, grad_b_f, grad_gq, grad_gk, grad_conv_w[None],
             grad_conv_b, grad_ln_g, grad_ln_b, grad_w_out]
    return (loss, grad_x2.reshape(B, S, D), *grads,
            *order(unpack(d_s), d_in, d_cw, d_out),
            *order(unpack(nm_s), nm_in, nm_cw, nm_out),
            *order(unpack(nv_s), nv_in, nv_cw, nv_out))
```

```python
import jax
import jax.numpy as jnp
from jax import lax
from jax.experimental import pallas as pl
from jax.experimental.pallas import tpu as pltpu

F32 = jnp.float32
BF16 = jnp.bfloat16
SDS = jax.ShapeDtypeStruct
MESH = pl.DeviceIdType.MESH

EPS = 1e-6
NEG_INF = -1e30
LANES = 128
SUBLANES = 8
HEAD_DIM = 64
HALO = 32
VMEM_LIMIT = 56 * 1024 * 1024

L_ROWSUM = 64
L_KDECAY = 67
L_LSE = 70
L_D = 65
L_QNORM = 73
L_KNORM = 74
NORM_SLACK = 1.02
SHIFT_MAX = 40.0

ADAM_LR = 0.001
ADAM_B1 = 0.9
ADAM_B2 = 0.999
ADAM_EPS = 1e-08
ADAM_WD = 0.01
ADAM_STEP = 10


def _params(sem, vmem=VMEM_LIMIT):
    return pltpu.CompilerParams(dimension_semantics=sem, vmem_limit_bytes=vmem)


def _sigmoid(x):
    return 1.0 / (1.0 + jnp.exp(-x))


def _split3(x):
    hi = x.astype(BF16).astype(F32)
    r = x - hi
    mid = r.astype(BF16).astype(F32)
    lo = (r - mid).astype(BF16).astype(F32)
    return hi, mid, lo


def _dot(a, b):
    return jnp.dot(a, b, preferred_element_type=F32)


def _dot_nt(a, b):
    return lax.dot_general(a, b, (((1,), (1,)), ((), ())), preferred_element_type=F32)


def _dot_tn(a, b):
    return lax.dot_general(a, b, (((0,), (0,)), ((), ())), preferred_element_type=F32)


def _lane(shape):
    return lax.broadcasted_iota(jnp.int32, shape, 1)


def _lane_col(x, lane, idx):
    return jnp.sum(jnp.where(lane == idx, x, 0.0), axis=-1, keepdims=True)


def _put3(base, lane, start, pieces):
    out = base
    for k, p in enumerate(pieces):
        out = jnp.where(lane == start + k, p, out)
    return out


def _half_stats(t):
    hi = t.astype(BF16)
    mid = (t - hi.astype(F32)).astype(BF16)
    row = lax.broadcasted_iota(jnp.int32, (2 * LANES, LANES), 0)
    col = lax.broadcasted_iota(jnp.int32, (2 * LANES, LANES), 1)
    same_half = (jnp.bitwise_and(row, LANES - 1) < HEAD_DIM) == (col < HEAD_DIM)
    return _dot(jnp.concatenate([hi, mid], axis=1), jnp.where(same_half, 1.0, 0.0).astype(BF16))


def _fwd_in(x2, g, w_t, tm, tn):
    T, D = x2.shape
    N = w_t.shape[0]

    def body(x_ref, g_ref, w_ref, z_ref, h_ref):
        @pl.when(pl.program_id(1) == 0)
        def _():
            x = x_ref[...]
            r = lax.rsqrt(jnp.mean(x * x, axis=-1, keepdims=True) + EPS)
            h_ref[...] = (x * r * g_ref[...]).astype(BF16)

        z_ref[...] = _dot_nt(h_ref[...], w_ref[...])

    return pl.pallas_call(
        body, name="fwd_in", grid=(T // tm, N // tn),
        in_specs=[pl.BlockSpec((tm, D), lambda i, j: (i, 0)),
                  pl.BlockSpec((1, D), lambda i, j: (0, 0)),
                  pl.BlockSpec((tn, D), lambda i, j: (j, 0))],
        out_specs=[pl.BlockSpec((tm, tn), lambda i, j: (i, j)),
                   pl.BlockSpec((tm, D), lambda i, j: (i, 0))],
        out_shape=[SDS((T, N), F32), SDS((T, D), BF16)],
        compiler_params=_params(("parallel", "arbitrary")),
    )(x2, g, w_t)


def _tri_cumsum(x, reverse):
    t = x.shape[0]
    row = lax.broadcasted_iota(jnp.int32, (t, t), 0)
    col = lax.broadcasted_iota(jnp.int32, (t, t), 1)
    tri = (row <= col) if reverse else (row >= col)
    tri = jnp.where(tri, 1.0, 0.0).astype(BF16)
    hi, mid, lo = _split3(x)
    return _dot(tri, hi.astype(BF16)) + _dot(tri, mid.astype(BF16)) + _dot(tri, lo.astype(BF16))


def _gate_fwd(z, b_pad, B, S, H, col_blk, tc):
    T = B * S
    nsb = S // tc

    def body(zf_ref, b_ref, c_ref, carry):
        @pl.when(pl.program_id(1) == 0)
        def _():
            carry[...] = jnp.zeros_like(carry)

        x = zf_ref[...] + b_ref[...]
        lf = jnp.minimum(x, 0.0) - jnp.log(1.0 + jnp.exp(-jnp.abs(x)))
        lf = jnp.where(_lane(lf.shape) < H, lf, 0.0)
        c_ref[...] = _tri_cumsum(lf, False) + carry[...]
        carry[...] = carry[...] + jnp.sum(lf, axis=0, keepdims=True)

    return pl.pallas_call(
        body, name="gate_fwd", grid=(B, nsb),
        in_specs=[pl.BlockSpec((tc, LANES), lambda b, s: (b * nsb + s, col_blk)),
                  pl.BlockSpec((1, LANES), lambda b, s: (0, 0))],
        out_specs=pl.BlockSpec((tc, LANES), lambda b, s: (b * nsb + s, 0)),
        out_shape=SDS((T, LANES), F32),
        scratch_shapes=[pltpu.VMEM((1, LANES), F32)],
        compiler_params=_params(("parallel", "arbitrary")),
    )(z, b_pad)


def _qk_normalize(x, g):
    r = lax.rsqrt(_half_stats(x * x) * (1.0 / HEAD_DIM) + EPS)
    return x * r * g


def _head_norms(x):
    own = jnp.sqrt(_half_stats(x * x)) * NORM_SLACK
    return [pltpu.roll(own, HEAD_DIM, 1), own]


def _attn_prep(z, c, gq, gk, B, S, H, tm):
    T = B * S
    FW = H * HEAD_DIM
    nsb = S // tm
    nfb = FW // LANES
    scale = HEAD_DIM ** -0.5

    def body(zq_ref, zk_ref, zv_ref, c_ref, gq_ref, gk_ref, qa_ref, ka_ref, va_ref):
        p = pl.program_id(1)
        lane = _lane((tm, LANES))
        lo = lane < HEAD_DIM
        qn = _qk_normalize(zq_ref[...], gq_ref[...]) * scale
        kn = _qk_normalize(zk_ref[...], gk_ref[...])
        v = zv_ref[...]
        cc = c_ref[...]
        ones_q = ((lane >= L_KDECAY) & (lane < L_KDECAY + 3)).astype(F32)
        ones_k = (((lane >= L_ROWSUM) & (lane < L_ROWSUM + 3)) | ((lane >= L_LSE) & (lane < L_LSE + 3))).astype(F32)
        ones_v = ((lane >= L_ROWSUM) & (lane < L_D + 3)).astype(F32)
        q_norms, k_norms = _head_norms(qn), _head_norms(kn)
        for e in range(2):
            if e == 0:
                qe, ke, ve = qn, kn, v
            else:
                qe, ke, ve = (pltpu.roll(t, HEAD_DIM, 1) for t in (qn, kn, v))
            ch = _lane_col(cc, lane, 2 * p + e)
            pieces = _split3(ch)
            qa = jnp.where(lo, qe, _put3(ones_q, lane, L_ROWSUM, pieces))
            qa = jnp.where(lane == L_QNORM, q_norms[e], qa)
            ka = jnp.where(lo, ke, _put3(ones_k, lane, L_KDECAY, [-t for t in pieces]))
            ka = jnp.where(lane == L_KNORM, k_norms[e], ka)
            va = jnp.where(lo, ve, ones_v)
            qa_ref[0, e] = qa.astype(BF16)
            ka_ref[0, e] = ka.astype(BF16)
            va_ref[0, e] = va.astype(BF16)

    zspec = lambda off: pl.BlockSpec((tm, LANES), lambda i, p: (i, off + p))
    gspec = pl.BlockSpec((1, LANES), lambda i, p: (0, p))
    ospec = pl.BlockSpec((1, 2, tm, LANES), lambda i, p: (i // nsb, p, i % nsb, 0))
    oshape = SDS((B, H, S, LANES), BF16)
    return pl.pallas_call(
        body, name="attn_prep", grid=(T // tm, H // 2),
        in_specs=[zspec(0), zspec(nfb), zspec(2 * nfb),
                  pl.BlockSpec((tm, LANES), lambda i, p: (i, 0)), gspec, gspec],
        out_specs=[ospec, ospec, ospec],
        out_shape=[oshape, oshape, oshape],
        compiler_params=_params(("parallel", "arbitrary")),
    )(z, z, z, c, gq, gk)


def _attn_fwd(qa, ka, va, t, hb):
    B, H, S, _ = qa.shape
    nq = S // t

    def body(q_ref, k_ref, v_ref, o_ref, m_ref, acc_ref, kmax_ref, qs_ref):
        i = pl.program_id(2)
        lane = _lane((t, LANES))

        @pl.when(i == 0)
        def _():
            for e in range(hb):
                norms = jnp.where(_lane((S, LANES)) == L_KNORM, k_ref[0, e].astype(F32), 0.0)
                kmax_ref[e] = jnp.full((1, LANES), jnp.max(norms), F32)

        shifts = [_lane_col(q_ref[0, e].astype(F32), lane, L_QNORM) * kmax_ref[e][:, 0:1] for e in range(hb)]
        worst = shifts[0]
        for e in range(1, hb):
            worst = jnp.maximum(worst, shifts[e])
        bounded = jnp.max(worst) <= SHIFT_MAX
        acc_ref[...] = jnp.zeros_like(acc_ref)

        def tiles(step):
            def loop_body(j, carry):
                step(j, False)
                return carry

            lax.fori_loop(0, i, loop_body, 0)
            step(i, True)

        def keep_mask(n=t):
            return lax.broadcasted_iota(jnp.int32, (n, n), 0) >= lax.broadcasted_iota(jnp.int32, (n, n), 1)

        def finish(e, shift):
            acc = acc_ref[e]
            l = _lane_col(acc, lane, L_ROWSUM)
            o_ref[0, e] = jnp.where(lane < HEAD_DIM, acc / l, shift + jnp.log(l))

        @pl.when(bounded)
        def _():
            for e in range(hb):
                qs_ref[e] = _put3(q_ref[0, e].astype(F32), lane, L_LSE, _split3(-shifts[e])).astype(BF16)

            def pair(e, q_rows, k_start, n, masked):
                k_rows = pl.ds(pl.multiple_of(k_start, n), n)
                p = jnp.exp(_dot_nt(qs_ref[e, q_rows, :], k_ref[0, e, k_rows, :]))
                if masked:
                    p = jnp.where(keep_mask(n), p, 0.0)
                acc_ref[e, q_rows, :] = acc_ref[e, q_rows, :] + _dot(p.astype(BF16), v_ref[0, e, k_rows, :])

            def step(j, masked):
                for e in range(hb):
                    if masked:
                        h = t // 2
                        pair(e, slice(0, h), j * t, h, True)
                        pair(e, slice(h, t), j * t, h, False)
                        pair(e, slice(h, t), j * t + h, h, True)
                    else:
                        pair(e, slice(0, t), j * t, t, False)

            tiles(step)
            for e in range(hb):
                finish(e, shifts[e])

        @pl.when(jnp.logical_not(bounded))
        def _():
            m_ref[...] = jnp.full_like(m_ref, NEG_INF)

            def step(j, masked):
                rows = pl.ds(pl.multiple_of(j * t, t), t)
                for e in range(hb):
                    s = _dot_nt(q_ref[0, e], k_ref[0, e, rows, :])
                    if masked:
                        s = jnp.where(keep_mask(), s, NEG_INF)
                    m_prev = m_ref[e]
                    m_new = jnp.maximum(m_prev, jnp.max(s, axis=-1, keepdims=True))
                    alpha = jnp.exp(m_prev - m_new)
                    p = jnp.exp(s - m_new).astype(BF16)
                    acc_ref[e] = alpha * acc_ref[e] + _dot(p, v_ref[0, e, rows, :])
                    m_ref[e] = m_new

            tiles(step)
            for e in range(hb):
                finish(e, m_ref[e])

    return pl.pallas_call(
        body, name="attn_fwd", grid=(B, H // hb, nq),
        in_specs=[pl.BlockSpec((1, hb, t, LANES), lambda b, h, i: (b, h, i, 0)),
                  pl.BlockSpec((1, hb, S, LANES), lambda b, h, i: (b, h, 0, 0)),
                  pl.BlockSpec((1, hb, S, LANES), lambda b, h, i: (b, h, 0, 0))],
        out_specs=pl.BlockSpec((1, hb, t, LANES), lambda b, h, i: (b, h, i, 0)),
        out_shape=SDS((B, H, S, LANES), F32),
        scratch_shapes=[pltpu.VMEM((hb, t, 1), F32), pltpu.VMEM((hb, t, LANES), F32),
                        pltpu.VMEM((hb, 1, LANES), F32), pltpu.VMEM((hb, t, LANES), BF16)],
        compiler_params=_params(("parallel", "parallel", "arbitrary")),
    )(qa, ka, va)


def _fill_shifts(ext_ref, sh_ref):
    rows = sh_ref.shape[1]
    for b in range(1, SUBLANES):
        sh_ref[b - 1] = ext_ref[pl.ds(b, rows), :]


def _tap_window(ext_ref, sh_ref, off, tm, cols):
    b = off % SUBLANES
    if b == 0:
        return ext_ref[pl.ds(off, tm), cols]
    return sh_ref[b - 1, pl.ds(off - b, tm), cols]


def _conv_taps(w_ref, ext_ref, sh_ref, out_ref, n_taps, tm, offset_of, bias_ref=None):
    for cc in range(out_ref.shape[1] // LANES):
        cols = slice(cc * LANES, (cc + 1) * LANES)
        acc = None
        for j in sorted(range(n_taps), key=offset_of):
            term = w_ref[j:j + 1, cols] * _tap_window(ext_ref, sh_ref, offset_of(j), tm, cols)
            acc = term if acc is None else acc + term
        out_ref[:, cols] = acc if bias_ref is None else acc + bias_ref[:, cols]


def _layernorm_stats(u2):
    mu = jnp.mean(u2, axis=-1, keepdims=True)
    xc = u2 - mu
    rstd = lax.rsqrt(jnp.mean(xc * xc, axis=-1, keepdims=True) + EPS)
    return xc * rstd, rstd


def _fwd_out(oa, z, x2, tgt, conv_w, conv_b, ln_g, ln_b, w_out, B, S, H, n_taps, tm):
    T, D = x2.shape
    FW = H * HEAD_DIM
    CW = conv_w.shape[1]
    nsb = S // tm
    hb = tm // HALO
    mb = 4 if nsb % 4 == 0 else 1
    mt = mb * tm

    def body(oa_ref, gf_ref, ga_ref, gb_ref, gc_ref, ha_ref, hb_ref, x_ref, t_ref, w_ref, cb_ref, lg_ref,
             lb_ref, wo_ref, y_ref, u2_ref, a_ref, do_ref, dob_ref, dy_ref, loss_ref, ext_ref, sh_ref):
        first_step = (pl.program_id(0) == 0) & (pl.program_id(1) == 0)
        sub = lax.rem(pl.program_id(1), mb)
        rows = pl.ds(pl.multiple_of(sub * tm, tm), tm)

        @pl.when(first_step)
        def _():
            loss_ref[...] = jnp.zeros_like(loss_ref)

        u1 = ga_ref[...] * _sigmoid(gb_ref[...])
        halo = ha_ref[...] * _sigmoid(hb_ref[...])
        ext_ref[0:HALO, :] = jnp.where(pl.program_id(1) > 0, halo, 0.0)
        ext_ref[HALO:, :] = u1
        _fill_shifts(ext_ref, sh_ref)
        _conv_taps(w_ref, ext_ref, sh_ref, u2_ref, n_taps, tm, lambda j: HALO - (n_taps - 1) + j, cb_ref)
        uh, _ = _layernorm_stats(u2_ref[...])
        u3 = uh * lg_ref[...] + lb_ref[...]
        gc = gc_ref[...]
        yu = u3 * _sigmoid(u3) * (gc * _sigmoid(gc))
        y_ref[rows, FW:] = yu.astype(BF16)

        lane = _lane((tm, LANES))
        lo = lane < HEAD_DIM
        for p in range(H // 2):
            a_ref[:, p * LANES:(p + 1) * LANES] = jnp.where(
                lo, oa_ref[0, 2 * p], pltpu.roll(oa_ref[0, 2 * p + 1], HEAD_DIM, 1))
        gf = gf_ref[...]
        y_ref[rows, :FW] = (a_ref[...] * (gf * _sigmoid(gf))).astype(BF16)

        @pl.when(sub == mb - 1)
        def _():
            out = x_ref[...] + _dot(y_ref[...], wo_ref[...])
            diff = out - t_ref[...]
            loss_ref[...] = loss_ref[...] + jnp.sum(diff * diff)
            dout = diff * (1.0 / D)
            do_ref[...] = dout
            dob = dout.astype(BF16)
            dob_ref[...] = dob
            dy_ref[...] = _dot_nt(dob, wo_ref[...])

    row = lambda b, s: b * nsb + s
    zspec = lambda cb: pl.BlockSpec((tm, FW), lambda b, s: (row(b, s), cb))
    hspec = lambda cb: pl.BlockSpec((HALO, CW), lambda b, s: (jnp.maximum(row(b, s) * hb - 1, 0), cb))
    vspec = pl.BlockSpec((1, CW), lambda b, s: (0, 0))
    tspec = lambda w: pl.BlockSpec((tm, w), lambda b, s: (row(b, s), 0))
    mspec = lambda w: pl.BlockSpec((mt, w), lambda b, s: (row(b, s) // mb, 0))
    return pl.pallas_call(
        body, name="fwd_out", grid=(B, nsb),
        in_specs=[pl.BlockSpec((1, H, tm, LANES), lambda b, s: (b, 0, s, 0)),
                  zspec(3), zspec(4), zspec(5), zspec(6), hspec(4), hspec(5),
                  mspec(D), mspec(D),
                  pl.BlockSpec((HALO, CW), lambda b, s: (0, 0)), vspec, vspec, vspec,
                  pl.BlockSpec((FW + CW, D), lambda b, s: (0, 0))],
        out_specs=[mspec(FW + CW), tspec(CW), tspec(FW), mspec(D), mspec(D), mspec(FW + CW),
                   pl.BlockSpec((8, LANES), lambda b, s: (0, 0))],
        out_shape=[SDS((T, FW + CW), BF16), SDS((T, CW), F32), SDS((T, FW), F32), SDS((T, D), F32),
                   SDS((T, D), BF16), SDS((T, FW + CW), F32), SDS((8, LANES), F32)],
        scratch_shapes=[pltpu.VMEM((tm + HALO, CW), F32),
                        pltpu.VMEM((SUBLANES - 1, tm + HALO - SUBLANES, CW), F32)],
        compiler_params=_params(("arbitrary", "arbitrary")),
    )(oa, z, z, z, z, z, z, x2, tgt, conv_w, conv_b, ln_g, ln_b, w_out)


def _bwd_prep(dy, z, a_nat, oa, qa, u2, ln_g, ln_b, B, S, H, tm):
    T = B * S
    FW = H * HEAD_DIM
    CW = u2.shape[1]
    nsb = S // tm

    def body(dya_ref, dyu_ref, gf_ref, gc_ref, a_ref, oa_ref, qa_ref, u2_ref, lg_ref, lb_ref,
             dzgf_ref, dzgc_ref, du2_ref, doa_ref, qb_ref, sg_ref):
        first_step = (pl.program_id(0) == 0) & (pl.program_id(1) == 0)

        @pl.when(first_step)
        def _():
            sg_ref[...] = jnp.zeros_like(sg_ref)

        gf = gf_ref[...]
        sg = _sigmoid(gf)
        a = a_ref[...]
        dya = dya_ref[...]
        da = dya * (gf * sg)
        dzgf_ref[...] = (dya * a * (sg * (1.0 + gf * (1.0 - sg)))).astype(BF16)
        dd = da * a
        lane = _lane((tm, LANES))
        lo = lane < HEAD_DIM
        for p in range(H // 2):
            cols = slice(p * LANES, (p + 1) * LANES)
            da_p = da[:, cols]
            dd_p = dd[:, cols]
            d_heads = (jnp.sum(jnp.where(lo, dd_p, 0.0), axis=-1, keepdims=True),
                       jnp.sum(jnp.where(lo, 0.0, dd_p), axis=-1, keepdims=True))
            for e in range(2):
                da_e = da_p if e == 0 else pltpu.roll(da_p, HEAD_DIM, 1)
                d_e = d_heads[e]
                aug = _put3(jnp.zeros((tm, LANES), F32), lane, L_D, _split3(-d_e))
                doa_ref[0, 2 * p + e] = jnp.where(lo, da_e, aug).astype(BF16)
                lse = _lane_col(oa_ref[0, 2 * p + e], lane, L_ROWSUM)
                qb = _put3(qa_ref[0, 2 * p + e].astype(F32), lane, L_LSE, _split3(-lse))
                qb_ref[0, 2 * p + e] = qb.astype(BF16)

        gc = gc_ref[...]
        sc = _sigmoid(gc)
        dyu = dyu_ref[...]
        uh, rstd = _layernorm_stats(u2_ref[...])
        u3 = uh * lg_ref[...] + lb_ref[...]
        s3 = _sigmoid(u3)
        dzgc_ref[...] = (dyu * (u3 * s3) * (sc * (1.0 + gc * (1.0 - sc)))).astype(BF16)
        du3 = dyu * (gc * sc) * (s3 * (1.0 + u3 * (1.0 - s3)))
        sg_ref[0:1, :] = sg_ref[0:1, :] + jnp.sum(du3 * uh, axis=0, keepdims=True)
        sg_ref[1:2, :] = sg_ref[1:2, :] + jnp.sum(du3, axis=0, keepdims=True)
        duh = du3 * lg_ref[...]
        du2 = rstd * (duh - jnp.mean(duh, axis=-1, keepdims=True)
                      - uh * jnp.mean(duh * uh, axis=-1, keepdims=True))
        sg_ref[2:3, :] = sg_ref[2:3, :] + jnp.sum(du2, axis=0, keepdims=True)
        du2_ref[...] = du2

    row = lambda b, s: b * nsb + s
    tspec = lambda w, cb=0: pl.BlockSpec((tm, w), lambda b, s: (row(b, s), cb))
    hspec = pl.BlockSpec((1, H, tm, LANES), lambda b, s: (b, 0, s, 0))
    vspec = pl.BlockSpec((1, CW), lambda b, s: (0, 0))
    return pl.pallas_call(
        body, name="bwd_prep", grid=(B, nsb),
        in_specs=[tspec(FW, 0), tspec(CW, 1), tspec(FW, 3), tspec(CW, 6), tspec(FW), hspec, hspec,
                  tspec(CW), vspec, vspec],
        out_specs=[tspec(FW), tspec(CW), tspec(CW), hspec, hspec,
                   pl.BlockSpec((8, CW), lambda b, s: (0, 0))],
        out_shape=[SDS((T, FW), BF16), SDS((T, CW), BF16), SDS((T, CW), F32),
                   SDS((B, H, S, LANES), BF16), SDS((B, H, S, LANES), BF16), SDS((8, CW), F32)],
        compiler_params=_params(("arbitrary", "arbitrary")),
    )(dy, dy, z, z, a_nat, oa, qa, u2, ln_g, ln_b)


def _conv_bwd(du2, z, conv_w, B, S, n_taps, tm):
    T, CW = du2.shape
    nsb = S // tm
    hb = tm // HALO

    def body(d_ref, dh_ref, ga_ref, gb_ref, ha_ref, hb_ref, w_ref, dz_ref, dw_ref,
             extu_ref, extd_ref, shu_ref, shd_ref, du1_ref, dwacc_ref):
        s = pl.program_id(1)
        first_step = (pl.program_id(0) == 0) & (s == 0)
        last_step = (pl.program_id(0) == B - 1) & (s == nsb - 1)

        @pl.when(first_step)
        def _():
            dwacc_ref[...] = jnp.zeros_like(dwacc_ref)

        ga = ga_ref[...]
        sb = _sigmoid(gb_ref[...])
        halo = ha_ref[...] * _sigmoid(hb_ref[...])
        extu_ref[0:HALO, :] = jnp.where(s > 0, halo, 0.0)
        extu_ref[HALO:, :] = ga * sb
        extd_ref[0:tm, :] = d_ref[...]
        extd_ref[tm:, :] = jnp.where(s < nsb - 1, dh_ref[...], 0.0)
        _fill_shifts(extu_ref, shu_ref)
        _fill_shifts(extd_ref, shd_ref)
        _conv_taps(w_ref, extd_ref, shd_ref, du1_ref, n_taps, tm, lambda j: n_taps - 1 - j)
        for cc in range(CW // LANES):
            cols = slice(cc * LANES, (cc + 1) * LANES)
            parts = [None] * n_taps
            for r in range(tm // SUBLANES):
                dv = d_ref[r * SUBLANES:(r + 1) * SUBLANES, cols]
                for j in range(n_taps):
                    off = HALO - (n_taps - 1) + j + r * SUBLANES
                    term = dv * _tap_window(extu_ref, shu_ref, off, SUBLANES, cols)
                    parts[j] = term if parts[j] is None else parts[j] + term
            for j in range(n_taps):
                rows = slice(j * SUBLANES, (j + 1) * SUBLANES)
                dwacc_ref[rows, cols] = dwacc_ref[rows, cols] + parts[j]
        du1 = du1_ref[...]
        dz_ref[:, :CW] = (du1 * sb).astype(BF16)
        dz_ref[:, CW:] = (du1 * ga * (sb * (1.0 - sb))).astype(BF16)

        @pl.when(last_step)
        def _():
            dw_ref[...] = jnp.zeros_like(dw_ref)
            for j in range(n_taps):
                dw_ref[j:j + 1, :] = jnp.sum(dwacc_ref[j * SUBLANES:(j + 1) * SUBLANES, :], axis=0, keepdims=True)

    row = lambda b, s: b * nsb + s
    last_halo = T // HALO - 1
    return pl.pallas_call(
        body, name="conv_bwd", grid=(B, nsb),
        in_specs=[pl.BlockSpec((tm, CW), lambda b, s: (row(b, s), 0)),
                  pl.BlockSpec((HALO, CW), lambda b, s: (jnp.minimum((row(b, s) + 1) * hb, last_halo), 0)),
                  pl.BlockSpec((tm, CW), lambda b, s: (row(b, s), 4)),
                  pl.BlockSpec((tm, CW), lambda b, s: (row(b, s), 5)),
                  pl.BlockSpec((HALO, CW), lambda b, s: (jnp.maximum(row(b, s) * hb - 1, 0), 4)),
                  pl.BlockSpec((HALO, CW), lambda b, s: (jnp.maximum(row(b, s) * hb - 1, 0), 5)),
                  pl.BlockSpec((HALO, CW), lambda b, s: (0, 0))],
        out_specs=[pl.BlockSpec((tm, 2 * CW), lambda b, s: (row(b, s), 0)),
                   pl.BlockSpec((HALO, CW), lambda b, s: (0, 0))],
        out_shape=[SDS((T, 2 * CW), BF16), SDS((HALO, CW), F32)],
        scratch_shapes=[pltpu.VMEM((tm + HALO, CW), F32), pltpu.VMEM((tm + HALO, CW), F32),
                        pltpu.VMEM((SUBLANES - 1, tm + HALO - SUBLANES, CW), F32),
                        pltpu.VMEM((SUBLANES - 1, tm + HALO - SUBLANES, CW), F32),
                        pltpu.VMEM((tm, CW), F32), pltpu.VMEM((HALO * SUBLANES, CW), F32)],
        compiler_params=_params(("arbitrary", "arbitrary")),
    )(du2, du2, z, z, z, z, conv_w)


def _attn_bwd(qb, ka, va, doa, t, hb):
    B, H, S, _ = qb.shape
    nk = S // t

    def body(q_ref, k_ref, v_ref, do_ref, dq_ref, dk_ref, dv_ref, dv_acc):
        j = pl.program_id(2)

        @pl.when(j == 0)
        def _():
            dq_ref[...] = jnp.zeros_like(dq_ref)

        dk_ref[...] = jnp.zeros_like(dk_ref)
        dv_acc[...] = jnp.zeros_like(dv_acc)

        def step(i, masked):
            q_rows = pl.ds(pl.multiple_of(i * t, t), t)
            for e in range(hb):
                k = k_ref[0, e]
                q = q_ref[0, e, q_rows, :]
                do = do_ref[0, e, q_rows, :]
                p = jnp.exp(_dot_nt(q, k))
                if masked:
                    keep = lax.broadcasted_iota(jnp.int32, (t, t), 0) >= lax.broadcasted_iota(jnp.int32, (t, t), 1)
                    p = jnp.where(keep, p, 0.0)
                ds = (p * _dot_nt(do, v_ref[0, e])).astype(BF16)
                dv_acc[e] = dv_acc[e] + _dot_tn(p.astype(BF16), do)
                dk_ref[0, e] = dk_ref[0, e] + _dot_tn(ds, q)
                dq_ref[0, e, q_rows, :] = dq_ref[0, e, q_rows, :] + _dot(ds, k)

        step(j, True)

        def loop_body(i, carry):
            step(i, False)
            return carry

        lax.fori_loop(j + 1, nk, loop_body, 0)
        dv_ref[0] = dv_acc[...].astype(BF16)

    full = pl.BlockSpec((1, hb, S, LANES), lambda b, h, j: (b, h, 0, 0))
    blk = pl.BlockSpec((1, hb, t, LANES), lambda b, h, j: (b, h, j, 0))
    oshape = SDS((B, H, S, LANES), F32)
    return pl.pallas_call(
        body, name="attn_bwd", grid=(B, H // hb, nk),
        in_specs=[full, blk, blk, full],
        out_specs=[full, blk, blk],
        out_shape=[oshape, oshape, SDS((B, H, S, LANES), BF16)],
        scratch_shapes=[pltpu.VMEM((hb, t, LANES), F32)],
        compiler_params=_params(("parallel", "parallel", "arbitrary")),
    )(qb, ka, va, doa)


def _qk_bwd(dqa, dka, dva, z, gq, gk, B, S, H, tm):
    T = B * S
    FW = H * HEAD_DIM
    nsb = S // tm
    nfb = FW // LANES
    scale = HEAD_DIM ** -0.5

    def body(dq_ref, dk_ref, dv_ref, zq_ref, zk_ref, gq_ref, gk_ref, dzq_ref, dzk_ref, dzv_ref, dc_ref, dg_ref):
        p = pl.program_id(0)

        @pl.when(pl.program_id(1) == 0)
        def _():
            dg_ref[...] = jnp.zeros_like(dg_ref)

        lane = _lane((tm, LANES))
        lo = lane < HEAD_DIM

        def natural(ref):
            return jnp.where(lo, ref[0, 0].astype(F32), pltpu.roll(ref[0, 1].astype(F32), HEAD_DIM, 1))

        def norm_bwd(dn, x, g, row, out_ref):
            r = lax.rsqrt(_half_stats(x * x) * (1.0 / HEAD_DIM) + EPS)
            xh = x * r
            dg_ref[row:row + 1, :] = dg_ref[row:row + 1, :] + jnp.sum(dn * xh, axis=0, keepdims=True)
            dxh = dn * g
            mm = _half_stats(dxh * xh) * (1.0 / HEAD_DIM)
            out_ref[...] = (r * (dxh - xh * mm)).astype(BF16)

        norm_bwd(natural(dq_ref) * scale, zq_ref[...], gq_ref[...], 0, dzq_ref)
        norm_bwd(natural(dk_ref), zk_ref[...], gk_ref[...], 1, dzk_ref)
        dzv_ref[...] = natural(dv_ref).astype(BF16)

        dc = jnp.zeros((tm, LANES), F32)
        for e in range(2):
            val = _lane_col(dq_ref[0, e], lane, L_ROWSUM) - _lane_col(dk_ref[0, e], lane, L_KDECAY)
            dc = jnp.where(lane == 2 * p + e, val, dc)
        dc_ref[0] = dc

    hspec = pl.BlockSpec((1, 2, tm, LANES), lambda p, i: (i // nsb, p, i % nsb, 0))
    zspec = lambda off: pl.BlockSpec((tm, LANES), lambda p, i: (i, off + p))
    gspec = pl.BlockSpec((1, LANES), lambda p, i: (0, p))
    ospec = pl.BlockSpec((tm, LANES), lambda p, i: (i, p))
    return pl.pallas_call(
        body, name="qk_bwd", grid=(H // 2, T // tm),
        in_specs=[hspec, hspec, hspec, zspec(0), zspec(nfb), gspec, gspec],
        out_specs=[ospec, ospec, ospec,
                   pl.BlockSpec((1, tm, LANES), lambda p, i: (p, i, 0)),
                   pl.BlockSpec((8, LANES), lambda p, i: (0, p))],
        out_shape=[SDS((T, FW), BF16), SDS((T, FW), BF16), SDS((T, FW), BF16),
                   SDS((H // 2, T, LANES), F32), SDS((8, FW), F32)],
        compiler_params=_params(("parallel", "arbitrary")),
    )(dqa, dka, dva, z, z, gq, gk)


def _gate_bwd(dc8, z, b_pad, B, S, H, col_blk, fp, tc):
    T = B * S
    nsb = S // tc
    npair = dc8.shape[0]

    def body(dc_ref, zf_ref, b_ref, dz_ref, db_ref, carry):
        first_step = (pl.program_id(0) == 0) & (pl.program_id(1) == 0)

        @pl.when(first_step)
        def _():
            db_ref[...] = jnp.zeros_like(db_ref)

        @pl.when(pl.program_id(1) == 0)
        def _():
            carry[...] = jnp.zeros_like(carry)

        dc = dc_ref[0]
        for k in range(1, npair):
            dc = dc + dc_ref[k]
        dlf = _tri_cumsum(dc, True) + carry[...]
        carry[...] = carry[...] + jnp.sum(dc, axis=0, keepdims=True)
        x = zf_ref[...] + b_ref[...]
        dlogit = dlf * _sigmoid(-x)
        db_ref[0:1, :] = db_ref[0:1, :] + jnp.sum(dlogit, axis=0, keepdims=True)
        dz_ref[...] = jnp.zeros_like(dz_ref)
        dz_ref[:, :LANES] = dlogit.astype(BF16)

    rrow = lambda b, s: b * nsb + (nsb - 1 - s)
    return pl.pallas_call(
        body, name="gate_bwd", grid=(B, nsb),
        in_specs=[pl.BlockSpec((npair, tc, LANES), lambda b, s: (0, rrow(b, s), 0)),
                  pl.BlockSpec((tc, LANES), lambda b, s: (rrow(b, s), col_blk)),
                  pl.BlockSpec((1, LANES), lambda b, s: (0, 0))],
        out_specs=[pl.BlockSpec((tc, fp), lambda b, s: (rrow(b, s), 0)),
                   pl.BlockSpec((8, LANES), lambda b, s: (0, 0))],
        out_shape=[SDS((T, fp), BF16), SDS((8, LANES), F32)],
        scratch_shapes=[pltpu.VMEM((1, LANES), F32)],
        compiler_params=_params(("arbitrary", "arbitrary")),
    )(dc8, z, b_pad)


def _matmul_tn(a, b, name, tmm, tn, tk):
    T, M = a.shape
    N = b.shape[1]
    tmm, tn, tk = min(tmm, M), min(tn, N), min(tk, T)

    def body(a_ref, b_ref, o_ref):
        @pl.when(pl.program_id(2) == 0)
        def _():
            o_ref[...] = jnp.zeros_like(o_ref)

        o_ref[...] = o_ref[...] + _dot_tn(a_ref[...], b_ref[...])

    return pl.pallas_call(
        body, name=name, grid=(M // tmm, N // tn, T // tk),
        in_specs=[pl.BlockSpec((tk, tmm), lambda i, j, k: (k, i)),
                  pl.BlockSpec((tk, tn), lambda i, j, k: (k, j))],
        out_specs=pl.BlockSpec((tmm, tn), lambda i, j, k: (i, j)),
        out_shape=SDS((M, N), F32),
        compiler_params=_params(("parallel", "parallel", "arbitrary")),
    )(a, b)


def _dh_rms_bwd(pieces, w_t, x2, g, dout, tm, tk, parts):
    T, D = x2.shape
    nks = [p.shape[1] // tk for p in pieces]
    starts = [sum(nks[:k]) for k in range(len(pieces))]
    nk = sum(nks)
    ni = T // tm
    n = len(parts)

    def body(*refs):
        dz_refs = refs[:len(pieces)]
        w_ref, x_ref, g_ref, do_ref = refs[len(pieces):len(pieces) + 4]
        part_refs = refs[len(pieces) + 4:len(pieces) + 4 + n]
        gx_ref, dg_ref = refs[len(pieces) + 4 + n:len(pieces) + 6 + n]
        slot_refs = refs[len(pieces) + 6 + n:len(pieces) + 6 + 2 * n]
        acc_ref, send_sems, recv_sems = refs[len(pieces) + 6 + 2 * n:]
        k = pl.program_id(1)
        first_step = (pl.program_id(0) == 0) & (k == 0)
        last_step = (pl.program_id(0) == ni - 1) & (k == nk - 1)
        x, y, c = _place()
        chips = [(1 - x, y), (x, 1 - y), (1 - x, 1 - y)]

        def copy(a, f, to):
            cx, cy = chips[f]
            return pltpu.make_async_remote_copy(
                src_ref=part_refs[a].at[2 * cx + cy], dst_ref=slot_refs[a].at[f],
                send_sem=send_sems.at[a * 3 + f], recv_sem=recv_sems.at[a * 3 + f],
                device_id=to, device_id_type=MESH)

        @pl.when(first_step)
        def _():
            dg_ref[...] = jnp.zeros_like(dg_ref)
            for a in range(n):
                for f in range(3):
                    copy(a, f, (*chips[f], c)).start()

        @pl.when(last_step)
        def _():
            for a in range(n):
                for f in range(3):
                    copy(a, f, (x, y, c)).wait_recv()
            for a in range(n):
                for f in range(3):
                    copy(a, f, (*chips[f], c)).wait_send()

        @pl.when(k == 0)
        def _():
            acc_ref[...] = jnp.zeros_like(acc_ref)

        for dz_ref, st, cnt in zip(dz_refs, starts, nks):
            @pl.when((k >= st) & (k < st + cnt))
            def _(dz_ref=dz_ref):
                acc_ref[...] = acc_ref[...] + _dot(dz_ref[...], w_ref[...])

        @pl.when(k == nk - 1)
        def _():
            x = x_ref[...]
            r = lax.rsqrt(jnp.mean(x * x, axis=-1, keepdims=True) + EPS)
            xh = x * r
            dh = acc_ref[...]
            dg_ref[0:1, :] = dg_ref[0:1, :] + jnp.sum(dh * xh, axis=0, keepdims=True)
            dxn = dh * g_ref[...]
            gx_ref[...] = do_ref[...] + r * (dxn - xh * jnp.mean(dxn * xh, axis=-1, keepdims=True))

    def piece_spec(st, cnt):
        return pl.BlockSpec((tm, tk), lambda i, k: (i, jnp.clip(k - st, 0, cnt - 1)))

    tspec = pl.BlockSpec((tm, D), lambda i, k: (i, 0))
    return pl.pallas_call(
        body, name="dh_rms_bwd", grid=(T // tm, nk),
        in_specs=[piece_spec(st, cnt) for st, cnt in zip(starts, nks)]
        + [pl.BlockSpec((tk, D), lambda i, k: (k, 0)), tspec, pl.BlockSpec((1, D), lambda i, k: (0, 0)), tspec]
        + [ANY] * n,
        out_specs=[tspec, pl.BlockSpec((8, D), lambda i, k: (0, 0))] + [ANY] * n,
        out_shape=[SDS((T, D), F32), SDS((8, D), F32)] + [SDS((3,) + p.shape[1:], p.dtype) for p in parts],
        scratch_shapes=[pltpu.VMEM((tm, D), F32),
                        pltpu.SemaphoreType.DMA((3 * n,)), pltpu.SemaphoreType.DMA((3 * n,))],
        compiler_params=_params(("arbitrary", "arbitrary")),
    )(*pieces, w_t, x2, g, dout, *parts)


def _block_plan(R, C, tr, tc):
    br = min(tr, R)
    if R % br == 0:
        return (br, C), R // br, lambda i: (i, 0)
    bc = min(tc, C)
    assert C % bc == 0
    return (R, bc), C // bc, lambda i: (0, i)


def _ew_call(body, name, ins, n_out, out_dtypes, tr, tc):
    R, C = ins[0].shape
    blk, steps, imap = _block_plan(R, C, tr, tc)
    spec = pl.BlockSpec(blk, imap)
    return pl.pallas_call(
        body, name=name, grid=(steps,),
        in_specs=[spec] * len(ins), out_specs=[spec] * n_out,
        out_shape=[SDS((R, C), dt) for dt in out_dtypes],
        compiler_params=_params(("parallel",)),
    )(*ins)


def _sum_slots(slots, name, first=None, tr=256):
    n, R, C = slots.shape
    blk, steps, imap = _block_plan(R, C, tr, 2 * LANES)
    lead = [] if first is None else [first]

    def body(*refs):
        s_ref, o_ref = refs[-2:]
        acc = refs[0][...].astype(F32) if lead else s_ref[0].astype(F32)
        for k in range(0 if lead else 1, n):
            acc = acc + s_ref[k].astype(F32)
        o_ref[...] = acc

    return pl.pallas_call(
        body, name=name, grid=(steps,),
        in_specs=[pl.BlockSpec(blk, imap)] * len(lead) + [pl.BlockSpec((n,) + blk, lambda i: (0,) + imap(i))],
        out_specs=pl.BlockSpec(blk, imap),
        out_shape=SDS((R, C), F32),
        compiler_params=_params(("parallel",)),
    )(*lead, slots)


def _adamw_update(w, g, m, v):
    nm = ADAM_B1 * m + (1.0 - ADAM_B1) * g
    nv = ADAM_B2 * v + (1.0 - ADAM_B2) * (g * g)
    m_hat = nm / (1.0 - ADAM_B1 ** ADAM_STEP)
    v_hat = nv / (1.0 - ADAM_B2 ** ADAM_STEP)
    return -ADAM_LR * (m_hat / (jnp.sqrt(v_hat) + ADAM_EPS) + ADAM_WD * w), nm, nv


def _adamw(w, g, m, v, name):
    def body(w_ref, g_ref, m_ref, v_ref, d_ref, nm_ref, nv_ref):
        d_ref[...], nm_ref[...], nv_ref[...] = _adamw_update(w_ref[...], g_ref[...], m_ref[...], v_ref[...])

    return _ew_call(body, name, [w, g, m, v], 3, [F32, F32, F32], 128, 2 * LANES)


def _adamw_halves(w, mine, other, m, v, name):
    R, C = w.shape
    half = C // 2
    bc = min(2 * LANES, half)
    per = half // bc

    def body(w_ref, a_ref, b_ref, m_ref, v_ref, g_ref, d_ref, nm_ref, nv_ref):
        g = jnp.where(pl.program_id(0) // per == lax.axis_index("c"), a_ref[...], b_ref[...])
        g_ref[...] = g
        d_ref[...], nm_ref[...], nv_ref[...] = _adamw_update(w_ref[...], g, m_ref[...], v_ref[...])

    full = pl.BlockSpec((R, bc), lambda i: (0, i))
    part = pl.BlockSpec((R, bc), lambda i: (0, i % per))
    return pl.pallas_call(
        body, name=name, grid=(C // bc,),
        in_specs=[full, part, part, full, full], out_specs=[full] * 4,
        out_shape=[SDS((R, C), F32)] * 4,
        compiler_params=_params(("parallel",)),
    )(w, mine, other, m, v)


ANY = pl.BlockSpec(memory_space=pl.ANY)


def _place():
    return lax.axis_index("x"), lax.axis_index("y"), lax.axis_index("c")


def _gather_chips(shards, splits):
    n = len(shards)
    per = 7

    def body(*refs):
        ins, outs = refs[:n], refs[n:2 * n]
        send_sems, recv_sems = refs[2 * n:]
        x, y, c = _place()
        mine = 2 * x + y
        chips = [(1 - x, y), (x, 1 - y), (1 - x, 1 - y)]

        def rows(a, half):
            return pl.ds(0, splits[a]) if half == 0 else pl.ds(splits[a], ins[a].shape[0] - splits[a])

        def copy(a, k, chip_idx, half, to, src=None):
            dst = outs[a].at[chip_idx, rows(a, half)]
            return pltpu.make_async_remote_copy(
                src_ref=dst if src is None else src, dst_ref=dst,
                send_sem=send_sems.at[a * per + k], recv_sem=recv_sems.at[a * per + k],
                device_id=to, device_id_type=MESH)

        def own(a, to):
            return pltpu.make_async_remote_copy(
                src_ref=ins[a], dst_ref=outs[a].at[mine],
                send_sem=send_sems.at[a * per + 6], recv_sem=recv_sems.at[a * per + 6],
                device_id=to, device_id_type=MESH)

        for cc in (0, 1):
            @pl.when(c == cc)
            def _(cc=cc):
                me, sibling = (x, y, cc), (x, y, 1 - cc)
                first = [copy(a, k, mine, cc, (*chip, cc), src=ins[a].at[rows(a, cc)])
                         for a in range(n) for k, chip in enumerate(chips)]
                first += [own(a, sibling) for a in range(n)]
                for cp in first:
                    cp.start()
                passed = []
                for k, (cx, cy) in enumerate(chips):
                    for a in range(n):
                        copy(a, k, 2 * cx + cy, cc, me).wait_recv()
                        fwd = copy(a, 3 + k, 2 * cx + cy, cc, sibling)
                        fwd.start()
                        passed.append(fwd)
                for k, (cx, cy) in enumerate(chips):
                    for a in range(n):
                        copy(a, 3 + k, 2 * cx + cy, 1 - cc, me).wait_recv()
                for a in range(n):
                    own(a, me).wait_recv()
                for cp in first + passed:
                    cp.wait_send()

    return pl.pallas_call(
        body, name="gather_chips",
        in_specs=[ANY] * n, out_specs=[ANY] * n,
        out_shape=[SDS((4,) + s.shape, s.dtype) for s in shards],
        scratch_shapes=[pltpu.SemaphoreType.DMA((per * n,)), pltpu.SemaphoreType.DMA((per * n,))],
    )(*shards)


def _pair_swap(arrs):
    n = len(arrs)

    def body(*refs):
        ins, outs = refs[:n], refs[n:2 * n]
        send_sems, recv_sems = refs[2 * n:]
        x, y, c = _place()
        for cc in (0, 1):
            @pl.when(c == cc)
            def _(cc=cc):
                copies = []
                for a in range(n):
                    half = ins[a].shape[1] // 2
                    copies.append(pltpu.make_async_remote_copy(
                        src_ref=ins[a].at[:, pl.ds((1 - cc) * half, half)], dst_ref=outs[a],
                        send_sem=send_sems.at[a], recv_sem=recv_sems.at[a],
                        device_id=(x, y, 1 - cc), device_id_type=MESH))
                for cp in copies:
                    cp.start()
                for cp in copies:
                    cp.wait()

    return pl.pallas_call(
        body, name="pair_swap",
        in_specs=[ANY] * n, out_specs=[ANY] * n,
        out_shape=[SDS((h.shape[0], h.shape[1] // 2), h.dtype) for h in arrs],
        scratch_shapes=[pltpu.SemaphoreType.DMA((n,)), pltpu.SemaphoreType.DMA((n,))],
    )(*arrs)


def _pair_sum(arrs, got, core, tr):
    half = arrs[0].shape[1] // 2
    cnts = [p.shape[0] // tr for p in arrs]
    starts = [sum(cnts[:k]) for k in range(len(arrs))]

    def body(core_ref, *refs):
        del core_ref
        own_refs, got_refs, o_ref = refs[:len(arrs)], refs[len(arrs):2 * len(arrs)], refs[-1]
        s = pl.program_id(0)
        for own_ref, got_ref, st, cnt in zip(own_refs, got_refs, starts, cnts):
            @pl.when((s >= st) & (s < st + cnt))
            def _(own_ref=own_ref, got_ref=got_ref):
                o_ref[...] = (own_ref[...] + got_ref[...]).astype(BF16)

    def own_spec(st, cnt):
        return pl.BlockSpec((tr, half), lambda s, core_ref: (jnp.clip(s - st, 0, cnt - 1), core_ref[0]))

    def got_spec(st, cnt):
        return pl.BlockSpec((tr, half), lambda s, core_ref: (jnp.clip(s - st, 0, cnt - 1), 0))

    return pl.pallas_call(
        body, name="pair_sum",
        grid_spec=pltpu.PrefetchScalarGridSpec(
            num_scalar_prefetch=1, grid=(sum(cnts),),
            in_specs=[own_spec(st, cnt) for st, cnt in zip(starts, cnts)]
            + [got_spec(st, cnt) for st, cnt in zip(starts, cnts)],
            out_specs=pl.BlockSpec((tr, half), lambda s, core_ref: (s, 0))),
        out_shape=SDS((sum(cnts) * tr, half), BF16),
        compiler_params=_params(("arbitrary",)),
    )(core, *arrs, *got)


def _share_results(arrs, rows):
    n = len(arrs)
    flips = [(fx, fy, fc) for fx in (0, 1) for fy in (0, 1) for fc in (0, 1)][1:]

    def body(*refs):
        ins, rows_ref, outs, all_ref = refs[:n], refs[n], refs[n + 1:2 * n + 1], refs[2 * n + 1]
        send_sems, recv_sems, local_sem = refs[2 * n + 2:]
        x, y, c = _place()
        me = 4 * x + 2 * y + c
        local = pltpu.make_async_copy(rows_ref, all_ref.at[me], local_sem)
        local.start()
        copies = [pltpu.make_async_remote_copy(
            src_ref=ins[a], dst_ref=outs[a], send_sem=send_sems.at[a], recv_sem=recv_sems.at[a],
            device_id=(x, y, 1 - c), device_id_type=MESH) for a in range(n)]
        for k, (fx, fy, fc) in enumerate(flips):
            copies.append(pltpu.make_async_remote_copy(
                src_ref=rows_ref, dst_ref=all_ref.at[me], send_sem=send_sems.at[n + k], recv_sem=recv_sems.at[n + k],
                device_id=(x ^ fx, y ^ fy, c ^ fc), device_id_type=MESH))
        for cp in copies:
            cp.start()
        for cp in copies[:n]:
            cp.wait_recv()
        for k, (fx, fy, fc) in enumerate(flips):
            src = 4 * (x ^ fx) + 2 * (y ^ fy) + (c ^ fc)
            pltpu.make_async_remote_copy(
                src_ref=rows_ref, dst_ref=all_ref.at[src], send_sem=send_sems.at[n + k], recv_sem=recv_sems.at[n + k],
                device_id=(x, y, c), device_id_type=MESH).wait_recv()
        for cp in copies:
            cp.wait_send()
        local.wait()

    outs = pl.pallas_call(
        body, name="share_results",
        in_specs=[ANY] * (n + 1), out_specs=[ANY] * (n + 1),
        out_shape=[SDS(h.shape, h.dtype) for h in arrs] + [SDS((8,) + rows.shape, rows.dtype)],
        scratch_shapes=[pltpu.SemaphoreType.DMA((n + 7,)), pltpu.SemaphoreType.DMA((n + 7,)),
                        pltpu.SemaphoreType.DMA],
    )(*arrs, rows)
    return outs[:n], outs[n]


def _tiles(S, FW):
    big = FW % 512 == 0
    return dict(
        fp=512 if big else LANES,
        tn=1536 if big else LANES,
        tm_in=min(1024, S),
        t_attn=min(512, S),
        hb_fwd=4,
        hb_bwd=2,
        tm_prep=min(512, S),
        tm_mix=min(128, S),
        tc=min(256, S),
        tk=512 if big else LANES,
    )


def kernel(x, norm_g, w_in, b_forget, q_norm_g, k_norm_g, conv_w, conv_b, conv_ln_g, conv_ln_b, w_out, loss_target, m_norm_g, m_w_in, m_b_forget, m_q_norm_g, m_k_norm_g, m_conv_w, m_conv_b, m_conv_ln_g, m_conv_ln_b, m_w_out, v_norm_g, v_w_in, v_b_forget, v_q_norm_g, v_k_norm_g, v_conv_w, v_conv_b, v_conv_ln_g, v_conv_ln_b, v_w_out):
    B, S, D = x.shape
    H, dh = q_norm_g.shape[1:]
    FW = H * dh
    CW = conv_b.shape[-1]
    n_taps, cw_shard = conv_w.shape[1:]
    in_shard = w_in.shape[2]
    out_shard = w_out.shape[1]
    assert dh == HEAD_DIM and H % 2 == 0 and H <= LANES and FW == CW == D
    assert n_taps - 1 <= HALO and 4 * cw_shard == CW and 4 * out_shard == FW + CW
    assert 4 * in_shard == 4 * FW + 3 * CW + H
    T = B * S
    tl = _tiles(S, FW)
    fp = tl["fp"]
    xi, yi, ci = _place()

    w_t = jnp.transpose(w_in[0])
    conv_pad = jnp.pad(conv_w[0], ((0, HALO - n_taps), (0, 0)))
    bf16_rows = 2 * SUBLANES
    g_in, g_out, g_cw = _gather_chips(
        [w_t.astype(BF16), w_out[0].astype(BF16), conv_pad],
        [in_shard // 2 // bf16_rows * bf16_rows, out_shard // 2, HALO // 2])
    w_t_full = g_in.reshape(4 * in_shard, D)
    w_out_full = g_out.reshape(FW + CW, D)
    conv_full = g_cw.transpose(1, 0, 2).reshape(HALO, CW)
    o_f = 3 * FW
    w_pack = jnp.concatenate([w_t_full[:o_f], w_t_full[o_f + H:],
                              jnp.pad(w_t_full[o_f:o_f + H], ((0, fp - H), (0, 0)))], axis=0)
    f_col = 4 * FW + 3 * CW

    x2 = x.reshape(T, D)
    tgt = loss_target.reshape(T, D)
    b_pad = jnp.pad(b_forget, ((0, 0), (0, LANES - H)))
    gq = q_norm_g.reshape(1, FW)
    gk = k_norm_g.reshape(1, FW)

    z, h = _fwd_in(x2, norm_g, w_pack, tl["tm_in"], tl["tn"])
    c = _gate_fwd(z, b_pad, B, S, H, f_col // LANES, tl["tc"])
    qa, ka, va = _attn_prep(z, c, gq, gk, B, S, H, tl["tm_prep"])
    oa = _attn_fwd(qa, ka, va, tl["t_attn"], tl["hb_fwd"])
    y, u2, a_nat, dout, dout_b, dy, loss_acc = _fwd_out(
        oa, z, x2, tgt, conv_full, conv_b, conv_ln_g, conv_ln_b, w_out_full, B, S, H, n_taps, tl["tm_mix"])

    dzgf, dzgc, du2, doa, qb, sg_conv = _bwd_prep(dy, z, a_nat, oa, qa, u2, conv_ln_g, conv_ln_b, B, S, H, tl["tm_mix"])
    dzglu, dconv_w = _conv_bwd(du2, z, conv_full, B, S, n_taps, tl["tm_mix"])
    dqa, dka, dva = _attn_bwd(qb, ka, va, doa, tl["t_attn"], tl["hb_bwd"])
    dzq, dzk, dzv, dc8, dg_qk = _qk_bwd(dqa, dka, dva, z, gq, gk, B, S, H, tl["tm_prep"])
    dzf, db_f = _gate_bwd(dc8, z, b_pad, B, S, H, f_col // LANES, fp, tl["tc"])
    pieces = [dzq, dzk, dzv, dzgf, dzglu, dzgc, dzf]
    dw_all = [_matmul_tn(p, h, f"dw_in_{k}", 1024, 1024, 1024) for k, p in enumerate(pieces)]
    dw_all.append(_matmul_tn(y, dout_b, "dw_out", 1024, 1024, 1024))

    summed = _pair_sum(dw_all, _pair_swap(dw_all), ci.astype(jnp.int32).reshape(1), fp)
    ends = [0]
    for t in dw_all:
        ends.append(ends[-1] + t.shape[0])
    spans, at = [], 0
    for k, rows in [(0, FW), (1, FW), (2, FW), (6, H), (3, FW), (4, 2 * CW), (5, CW)]:
        spans.append((at, rows, ends[k]))
        at += rows

    def chip_rows(j):
        lo, hi = j * in_shard, (j + 1) * in_shard
        return jnp.concatenate([summed[src + max(lo, a) - a:src + min(hi, a + n) - a]
                                for a, n, src in spans if max(lo, a) < min(hi, a + n)], axis=0)

    part_in = jnp.stack([chip_rows(j) for j in range(4)])
    part_out = summed[ends[7]:ends[8]].reshape(4, out_shard, D // 2)
    grad_x2, dg_norm, slots_in, slots_out = _dh_rms_bwd(
        pieces, w_pack, x2, norm_g, dout, tl["tm_in"], tl["tk"], [part_in, part_out])
    chip = 2 * xi + yi
    half_in = _sum_slots(slots_in, "chip_sum_in", lax.dynamic_index_in_dim(part_in, chip, 0, keepdims=False))
    half_out = _sum_slots(slots_out, "chip_sum_out", lax.dynamic_index_in_dim(part_out, chip, 0, keepdims=False))
    lanes_to_d = lambda t: jnp.pad(t, ((0, 0), (0, D - LANES)))
    small = jnp.concatenate([
        dg_norm[0:1], lanes_to_d(db_f[0:1, :]), dg_qk[0:1], dg_qk[1:2],
        sg_conv[2:3], sg_conv[0:1], sg_conv[1:2], dconv_w, lanes_to_d(loss_acc[0:1, :])], axis=0)
    n_small = small.shape[0]
    (other_in, other_out), all_small = _share_results([half_in, half_out], small)

    small_sum = _sum_slots(all_small, "small_sum", tr=n_small)
    loss = 0.5 * small_sum[n_small - 1, 0] / D
    grad_norm_g, grad_b_f = small_sum[0:1], small_sum[1:2, :H]
    grad_gq, grad_gk = small_sum[2:3].reshape(1, H, dh), small_sum[3:4].reshape(1, H, dh)
    grad_conv_b, grad_ln_g, grad_ln_b = small_sum[4:5], small_sum[5:6], small_sum[6:7]
    grad_conv_w = lax.dynamic_slice_in_dim(small_sum[7:7 + n_taps], chip * cw_shard, cw_shard, axis=1)

    in_t = _adamw_halves(w_t, half_in, other_in, jnp.transpose(m_w_in[0]), jnp.transpose(v_w_in[0]), "adamw_in")
    grad_w_in, d_in, nm_in, nv_in = (jnp.transpose(t)[None] for t in in_t)
    grad_w_out, d_out, nm_out, nv_out = (
        t[None] for t in _adamw_halves(w_out[0], half_out, other_out, m_w_out[0], v_w_out[0], "adamw_out"))
    d_cw, nm_cw, nv_cw = (t[None] for t in _adamw(conv_w[0], grad_conv_w, m_conv_w[0], v_conv_w[0], "adamw_conv_w"))

    def rows(ws):
        return jnp.concatenate([jnp.pad(t.reshape(1, -1), ((0, 0), (0, D - t.size))) for t in ws], axis=0)

    small_w = [norm_g, b_forget, q_norm_g, k_norm_g, conv_b, conv_ln_g, conv_ln_b]
    small_m = [m_norm_g, m_b_forget, m_q_norm_g, m_k_norm_g, m_conv_b, m_conv_ln_g, m_conv_ln_b]
    small_v = [v_norm_g, v_b_forget, v_q_norm_g, v_k_norm_g, v_conv_b, v_conv_ln_g, v_conv_ln_b]
    d_s, nm_s, nv_s = _adamw(rows(small_w), small_sum[0:7], rows(small_m), rows(small_v), "adamw_small")

    def unpack(t):
        return [t[k:k + 1, :w.size].reshape(w.shape) for k, w in enumerate(small_w)]

    def order(s, in_, cw, out_):
        ng, bf, qg, kg, cb, lg, lb = s
        return [ng, in_, bf, qg, kg, cw, cb, lg, lb, out_]

    grads = [grad_norm_g, grad_w_in, grad_b_f, grad_gq, grad_gk, grad_conv_w[None],
             grad_conv_b, grad_ln_g, grad_ln_b, grad_w_out]
    return (loss, grad_x2.reshape(B, S, D), *grads,
            *order(unpack(d_s), d_in, d_cw, d_out),
            *order(unpack(nm_s), nm_in, nm_cw, nm_out),
            *order(unpack(nv_s), nv_in, nv_cw, nv_out))
```

```python
import jax
import jax.numpy as jnp
from jax import lax
from jax.experimental import pallas as pl
from jax.experimental.pallas import tpu as pltpu

F32 = jnp.float32
BF16 = jnp.bfloat16
SDS = jax.ShapeDtypeStruct
MESH = pl.DeviceIdType.MESH

EPS = 1e-6
NEG_INF = -1e30
LANES = 128
SUBLANES = 8
HEAD_DIM = 64
HALO = 32
VMEM_LIMIT = 56 * 1024 * 1024

L_ROWSUM = 64
L_KDECAY = 67
L_LSE = 70
L_D = 65
NORM_SLACK = 1.02
SHIFT_MAX = 40.0

ADAM_LR = 0.001
ADAM_B1 = 0.9
ADAM_B2 = 0.999
ADAM_EPS = 1e-08
ADAM_WD = 0.01
ADAM_STEP = 10


def _params(sem, vmem=VMEM_LIMIT):
    return pltpu.CompilerParams(dimension_semantics=sem, vmem_limit_bytes=vmem)


def _sigmoid(x):
    return 1.0 / (1.0 + jnp.exp(-x))


def _split3(x):
    hi = x.astype(BF16).astype(F32)
    r = x - hi
    mid = r.astype(BF16).astype(F32)
    lo = (r - mid).astype(BF16).astype(F32)
    return hi, mid, lo


def _dot(a, b):
    return jnp.dot(a, b, preferred_element_type=F32)


def _dot_nt(a, b):
    return lax.dot_general(a, b, (((1,), (1,)), ((), ())), preferred_element_type=F32)


def _dot_tn(a, b):
    return lax.dot_general(a, b, (((0,), (0,)), ((), ())), preferred_element_type=F32)


def _lane(shape):
    return lax.broadcasted_iota(jnp.int32, shape, 1)


def _lane_col(x, lane, idx):
    return jnp.sum(jnp.where(lane == idx, x, 0.0), axis=-1, keepdims=True)


def _put3(base, lane, start, pieces):
    out = base
    for k, p in enumerate(pieces):
        out = jnp.where(lane == start + k, p, out)
    return out


def _half_stats(t):
    hi = t.astype(BF16)
    mid = (t - hi.astype(F32)).astype(BF16)
    row = lax.broadcasted_iota(jnp.int32, (2 * LANES, LANES), 0)
    col = lax.broadcasted_iota(jnp.int32, (2 * LANES, LANES), 1)
    same_half = (jnp.bitwise_and(row, LANES - 1) < HEAD_DIM) == (col < HEAD_DIM)
    return _dot(jnp.concatenate([hi, mid], axis=1), jnp.where(same_half, 1.0, 0.0).astype(BF16))


def _fwd_in(x2, g, w_t, tm, tn):
    T, D = x2.shape
    N = w_t.shape[0]

    def body(x_ref, g_ref, w_ref, z_ref, h_ref):
        @pl.when(pl.program_id(1) == 0)
        def _():
            x = x_ref[...]
            r = lax.rsqrt(jnp.mean(x * x, axis=-1, keepdims=True) + EPS)
            h_ref[...] = (x * r * g_ref[...]).astype(BF16)

        z_ref[...] = _dot_nt(h_ref[...], w_ref[...])

    return pl.pallas_call(
        body, name="fwd_in", grid=(T // tm, N // tn),
        in_specs=[pl.BlockSpec((tm, D), lambda i, j: (i, 0)),
                  pl.BlockSpec((1, D), lambda i, j: (0, 0)),
                  pl.BlockSpec((tn, D), lambda i, j: (j, 0))],
        out_specs=[pl.BlockSpec((tm, tn), lambda i, j: (i, j)),
                   pl.BlockSpec((tm, D), lambda i, j: (i, 0))],
        out_shape=[SDS((T, N), F32), SDS((T, D), BF16)],
        compiler_params=_params(("parallel", "arbitrary")),
    )(x2, g, w_t)


def _tri_cumsum(x, reverse):
    t = x.shape[0]
    row = lax.broadcasted_iota(jnp.int32, (t, t), 0)
    col = lax.broadcasted_iota(jnp.int32, (t, t), 1)
    tri = (row <= col) if reverse else (row >= col)
    tri = jnp.where(tri, 1.0, 0.0).astype(BF16)
    hi, mid, lo = _split3(x)
    return _dot(tri, hi.astype(BF16)) + _dot(tri, mid.astype(BF16)) + _dot(tri, lo.astype(BF16))


def _gate_fwd(z, b_pad, B, S, H, col_blk, tc):
    T = B * S
    nsb = S // tc

    def body(zf_ref, b_ref, c_ref, carry):
        @pl.when(pl.program_id(1) == 0)
        def _():
            carry[...] = jnp.zeros_like(carry)

        x = zf_ref[...] + b_ref[...]
        lf = jnp.minimum(x, 0.0) - jnp.log(1.0 + jnp.exp(-jnp.abs(x)))
        lf = jnp.where(_lane(lf.shape) < H, lf, 0.0)
        c_ref[...] = _tri_cumsum(lf, False) + carry[...]
        carry[...] = carry[...] + jnp.sum(lf, axis=0, keepdims=True)

    return pl.pallas_call(
        body, name="gate_fwd", grid=(B, nsb),
        in_specs=[pl.BlockSpec((tc, LANES), lambda b, s: (b * nsb + s, col_blk)),
                  pl.BlockSpec((1, LANES), lambda b, s: (0, 0))],
        out_specs=pl.BlockSpec((tc, LANES), lambda b, s: (b * nsb + s, 0)),
        out_shape=SDS((T, LANES), F32),
        scratch_shapes=[pltpu.VMEM((1, LANES), F32)],
        compiler_params=_params(("parallel", "arbitrary")),
    )(z, b_pad)


def _qk_normalize(x, g):
    r = lax.rsqrt(_half_stats(x * x) * (1.0 / HEAD_DIM) + EPS)
    return x * r * g


def _attn_prep(z, c, gq, gk, B, S, H, tm):
    T = B * S
    FW = H * HEAD_DIM
    nsb = S // tm
    nfb = FW // LANES
    scale = HEAD_DIM ** -0.5

    def body(zq_ref, zk_ref, zv_ref, c_ref, gq_ref, gk_ref, qa_ref, ka_ref, va_ref):
        p = pl.program_id(1)
        lane = _lane((tm, LANES))
        lo = lane < HEAD_DIM
        qn = _qk_normalize(zq_ref[...], gq_ref[...]) * scale
        kn = _qk_normalize(zk_ref[...], gk_ref[...])
        v = zv_ref[...]
        cc = c_ref[...]
        ones_q = ((lane >= L_KDECAY) & (lane < L_KDECAY + 3)).astype(F32)
        ones_k = (((lane >= L_ROWSUM) & (lane < L_ROWSUM + 3)) | ((lane >= L_LSE) & (lane < L_LSE + 3))).astype(F32)
        ones_v = ((lane >= L_ROWSUM) & (lane < L_D + 3)).astype(F32)
        for e in range(2):
            if e == 0:
                qe, ke, ve = qn, kn, v
            else:
                qe, ke, ve = (pltpu.roll(t, HEAD_DIM, 1) for t in (qn, kn, v))
            ch = _lane_col(cc, lane, 2 * p + e)
            pieces = _split3(ch)
            qa = jnp.where(lo, qe, _put3(ones_q, lane, L_ROWSUM, pieces))
            ka = jnp.where(lo, ke, _put3(ones_k, lane, L_KDECAY, [-t for t in pieces]))
            va = jnp.where(lo, ve, ones_v)
            qa_ref[0, e] = qa.astype(BF16)
            ka_ref[0, e] = ka.astype(BF16)
            va_ref[0, e] = va.astype(BF16)

    zspec = lambda off: pl.BlockSpec((tm, LANES), lambda i, p: (i, off + p))
    gspec = pl.BlockSpec((1, LANES), lambda i, p: (0, p))
    ospec = pl.BlockSpec((1, 2, tm, LANES), lambda i, p: (i // nsb, p, i % nsb, 0))
    oshape = SDS((B, H, S, LANES), BF16)
    return pl.pallas_call(
        body, name="attn_prep", grid=(T // tm, H // 2),
        in_specs=[zspec(0), zspec(nfb), zspec(2 * nfb),
                  pl.BlockSpec((tm, LANES), lambda i, p: (i, 0)), gspec, gspec],
        out_specs=[ospec, ospec, ospec],
        out_shape=[oshape, oshape, oshape],
        compiler_params=_params(("parallel", "arbitrary")),
    )(z, z, z, c, gq, gk)


def _attn_fwd(qa, ka, va, bound, t, hb):
    B, H, S, _ = qa.shape
    nq = S // t

    def body(q_ref, k_ref, v_ref, b_ref, o_ref, m_ref, acc_ref, qs_ref):
        i = pl.program_id(2)
        lane = _lane((t, LANES))

        shifts = [b_ref[e] for e in range(hb)]
        worst = shifts[0]
        for e in range(1, hb):
            worst = jnp.maximum(worst, shifts[e])
        bounded = jnp.max(worst) <= SHIFT_MAX
        acc_ref[...] = jnp.zeros_like(acc_ref)

        def tiles(step):
            def loop_body(j, carry):
                step(j, False)
                return carry

            lax.fori_loop(0, i, loop_body, 0)
            step(i, True)

        def keep_mask(n=t):
            return lax.broadcasted_iota(jnp.int32, (n, n), 0) >= lax.broadcasted_iota(jnp.int32, (n, n), 1)

        def finish(e, shift):
            acc = acc_ref[e]
            l = _lane_col(acc, lane, L_ROWSUM)
            o_ref[0, e] = jnp.where(lane < HEAD_DIM, acc / l, shift + jnp.log(l))

        @pl.when(bounded)
        def _():
            for e in range(hb):
                qs_ref[e] = _put3(q_ref[0, e].astype(F32), lane, L_LSE, _split3(-shifts[e])).astype(BF16)

            def pair(e, q_rows, k_start, n, masked):
                k_rows = pl.ds(pl.multiple_of(k_start, n), n)
                p = jnp.exp(_dot_nt(qs_ref[e, q_rows, :], k_ref[0, e, k_rows, :]))
                if masked:
                    p = jnp.where(keep_mask(n), p, 0.0)
                acc_ref[e, q_rows, :] = acc_ref[e, q_rows, :] + _dot(p.astype(BF16), v_ref[0, e, k_rows, :])

            def step(j, masked):
                for e in range(hb):
                    if masked:
                        h = t // 2
                        pair(e, slice(0, h), j * t, h, True)
                        pair(e, slice(h, t), j * t, h, False)
                        pair(e, slice(h, t), j * t + h, h, True)
                    else:
                        pair(e, slice(0, t), j * t, t, False)

            tiles(step)
            for e in range(hb):
                finish(e, shifts[e])

        @pl.when(jnp.logical_not(bounded))
        def _():
            m_ref[...] = jnp.full_like(m_ref, NEG_INF)

            def step(j, masked):
                rows = pl.ds(pl.multiple_of(j * t, t), t)
                for e in range(hb):
                    s = _dot_nt(q_ref[0, e], k_ref[0, e, rows, :])
                    if masked:
                        s = jnp.where(keep_mask(), s, NEG_INF)
                    m_prev = m_ref[e]
                    m_new = jnp.maximum(m_prev, jnp.max(s, axis=-1, keepdims=True))
                    alpha = jnp.exp(m_prev - m_new)
                    p = jnp.exp(s - m_new).astype(BF16)
                    acc_ref[e] = alpha * acc_ref[e] + _dot(p, v_ref[0, e, rows, :])
                    m_ref[e] = m_new

            tiles(step)
            for e in range(hb):
                finish(e, m_ref[e])

    return pl.pallas_call(
        body, name="attn_fwd", grid=(B, H // hb, nq),
        in_specs=[pl.BlockSpec((1, hb, t, LANES), lambda b, h, i: (b, h, i, 0)),
                  pl.BlockSpec((1, hb, S, LANES), lambda b, h, i: (b, h, 0, 0)),
                  pl.BlockSpec((1, hb, S, LANES), lambda b, h, i: (b, h, 0, 0)),
                  pl.BlockSpec((hb, 1, LANES), lambda b, h, i: (h, 0, 0))],
        out_specs=pl.BlockSpec((1, hb, t, LANES), lambda b, h, i: (b, h, i, 0)),
        out_shape=SDS((B, H, S, LANES), F32),
        scratch_shapes=[pltpu.VMEM((hb, t, 1), F32), pltpu.VMEM((hb, t, LANES), F32),
                        pltpu.VMEM((hb, t, LANES), BF16)],
        compiler_params=_params(("parallel", "parallel", "arbitrary")),
    )(qa, ka, va, bound)


def _fill_shifts(ext_ref, sh_ref):
    rows = sh_ref.shape[1]
    for b in range(1, SUBLANES):
        sh_ref[b - 1] = ext_ref[pl.ds(b, rows), :]


def _tap_window(ext_ref, sh_ref, off, tm, cols):
    b = off % SUBLANES
    if b == 0:
        return ext_ref[pl.ds(off, tm), cols]
    return sh_ref[b - 1, pl.ds(off - b, tm), cols]


def _conv_taps(w_ref, ext_ref, sh_ref, out_ref, n_taps, tm, offset_of, bias_ref=None):
    for cc in range(out_ref.shape[1] // LANES):
        cols = slice(cc * LANES, (cc + 1) * LANES)
        acc = None
        for j in sorted(range(n_taps), key=offset_of):
            term = w_ref[j:j + 1, cols] * _tap_window(ext_ref, sh_ref, offset_of(j), tm, cols)
            acc = term if acc is None else acc + term
        out_ref[:, cols] = acc if bias_ref is None else acc + bias_ref[:, cols]


def _layernorm_stats(u2):
    mu = jnp.mean(u2, axis=-1, keepdims=True)
    xc = u2 - mu
    rstd = lax.rsqrt(jnp.mean(xc * xc, axis=-1, keepdims=True) + EPS)
    return xc * rstd, rstd


def _fwd_out(oa, z, x2, tgt, conv_w, conv_b, ln_g, ln_b, w_out, B, S, H, n_taps, tm):
    T, D = x2.shape
    FW = H * HEAD_DIM
    CW = conv_w.shape[1]
    nsb = S // tm
    hb = tm // HALO
    mb = 4 if nsb % 4 == 0 else 1
    mt = mb * tm

    def body(oa_ref, gf_ref, ga_ref, gb_ref, gc_ref, ha_ref, hb_ref, x_ref, t_ref, w_ref, cb_ref, lg_ref,
             lb_ref, wo_ref, y_ref, u2_ref, a_ref, do_ref, dob_ref, dy_ref, loss_ref, ext_ref, sh_ref):
        first_step = (pl.program_id(0) == 0) & (pl.program_id(1) == 0)
        sub = lax.rem(pl.program_id(1), mb)
        rows = pl.ds(pl.multiple_of(sub * tm, tm), tm)

        @pl.when(first_step)
        def _():
            loss_ref[...] = jnp.zeros_like(loss_ref)

        u1 = ga_ref[...] * _sigmoid(gb_ref[...])
        halo = ha_ref[...] * _sigmoid(hb_ref[...])
        ext_ref[0:HALO, :] = jnp.where(pl.program_id(1) > 0, halo, 0.0)
        ext_ref[HALO:, :] = u1
        _fill_shifts(ext_ref, sh_ref)
        _conv_taps(w_ref, ext_ref, sh_ref, u2_ref, n_taps, tm, lambda j: HALO - (n_taps - 1) + j, cb_ref)
        uh, _ = _layernorm_stats(u2_ref[...])
        u3 = uh * lg_ref[...] + lb_ref[...]
        gc = gc_ref[...]
        yu = u3 * _sigmoid(u3) * (gc * _sigmoid(gc))
        y_ref[rows, FW:] = yu.astype(BF16)

        lane = _lane((tm, LANES))
        lo = lane < HEAD_DIM
        for p in range(H // 2):
            a_ref[:, p * LANES:(p + 1) * LANES] = jnp.where(
                lo, oa_ref[0, 2 * p], pltpu.roll(oa_ref[0, 2 * p + 1], HEAD_DIM, 1))
        gf = gf_ref[...]
        y_ref[rows, :FW] = (a_ref[...] * (gf * _sigmoid(gf))).astype(BF16)

        @pl.when(sub == mb - 1)
        def _():
            out = x_ref[...] + _dot(y_ref[...], wo_ref[...])
            diff = out - t_ref[...]
            loss_ref[...] = loss_ref[...] + jnp.sum(diff * diff)
            dout = diff * (1.0 / D)
            do_ref[...] = dout
            dob = dout.astype(BF16)
            dob_ref[...] = dob
            dy_ref[...] = _dot_nt(dob, wo_ref[...])

    row = lambda b, s: b * nsb + s
    zspec = lambda cb: pl.BlockSpec((tm, FW), lambda b, s: (row(b, s), cb))
    hspec = lambda cb: pl.BlockSpec((HALO, CW), lambda b, s: (jnp.maximum(row(b, s) * hb - 1, 0), cb))
    vspec = pl.BlockSpec((1, CW), lambda b, s: (0, 0))
    tspec = lambda w: pl.BlockSpec((tm, w), lambda b, s: (row(b, s), 0))
    mspec = lambda w: pl.BlockSpec((mt, w), lambda b, s: (row(b, s) // mb, 0))
    return pl.pallas_call(
        body, name="fwd_out", grid=(B, nsb),
        in_specs=[pl.BlockSpec((1, H, tm, LANES), lambda b, s: (b, 0, s, 0)),
                  zspec(3), zspec(4), zspec(5), zspec(6), hspec(4), hspec(5),
                  mspec(D), mspec(D),
                  pl.BlockSpec((HALO, CW), lambda b, s: (0, 0)), vspec, vspec, vspec,
                  pl.BlockSpec((FW + CW, D), lambda b, s: (0, 0))],
        out_specs=[mspec(FW + CW), tspec(CW), tspec(FW), mspec(D), mspec(D), mspec(FW + CW),
                   pl.BlockSpec((8, LANES), lambda b, s: (0, 0))],
        out_shape=[SDS((T, FW + CW), BF16), SDS((T, CW), F32), SDS((T, FW), F32), SDS((T, D), F32),
                   SDS((T, D), BF16), SDS((T, FW + CW), F32), SDS((8, LANES), F32)],
        scratch_shapes=[pltpu.VMEM((tm + HALO, CW), F32),
                        pltpu.VMEM((SUBLANES - 1, tm + HALO - SUBLANES, CW), F32)],
        compiler_params=_params(("arbitrary", "arbitrary")),
    )(oa, z, z, z, z, z, z, x2, tgt, conv_w, conv_b, ln_g, ln_b, w_out)


def _bwd_prep(dy, z, a_nat, oa, qa, u2, ln_g, ln_b, B, S, H, tm):
    T = B * S
    FW = H * HEAD_DIM
    CW = u2.shape[1]
    nsb = S // tm

    def body(dya_ref, dyu_ref, gf_ref, gc_ref, a_ref, oa_ref, qa_ref, u2_ref, lg_ref, lb_ref,
             dzgf_ref, dzgc_ref, du2_ref, doa_ref, qb_ref, sg_ref):
        first_step = (pl.program_id(0) == 0) & (pl.program_id(1) == 0)

        @pl.when(first_step)
        def _():
            sg_ref[...] = jnp.zeros_like(sg_ref)

        gf = gf_ref[...]
        sg = _sigmoid(gf)
        a = a_ref[...]
        dya = dya_ref[...]
        da = dya * (gf * sg)
        dzgf_ref[...] = (dya * a * (sg * (1.0 + gf * (1.0 - sg)))).astype(BF16)
        dd = da * a
        lane = _lane((tm, LANES))
        lo = lane < HEAD_DIM
        for p in range(H // 2):
            cols = slice(p * LANES, (p + 1) * LANES)
            da_p = da[:, cols]
            dd_p = dd[:, cols]
            d_heads = (jnp.sum(jnp.where(lo, dd_p, 0.0), axis=-1, keepdims=True),
                       jnp.sum(jnp.where(lo, 0.0, dd_p), axis=-1, keepdims=True))
            for e in range(2):
                da_e = da_p if e == 0 else pltpu.roll(da_p, HEAD_DIM, 1)
                d_e = d_heads[e]
                aug = _put3(jnp.zeros((tm, LANES), F32), lane, L_D, _split3(-d_e))
                doa_ref[0, 2 * p + e] = jnp.where(lo, da_e, aug).astype(BF16)
                lse = _lane_col(oa_ref[0, 2 * p + e], lane, L_ROWSUM)
                qb = _put3(qa_ref[0, 2 * p + e].astype(F32), lane, L_LSE, _split3(-lse))
                qb_ref[0, 2 * p + e] = qb.astype(BF16)

        gc = gc_ref[...]
        sc = _sigmoid(gc)
        dyu = dyu_ref[...]
        uh, rstd = _layernorm_stats(u2_ref[...])
        u3 = uh * lg_ref[...] + lb_ref[...]
        s3 = _sigmoid(u3)
        dzgc_ref[...] = (dyu * (u3 * s3) * (sc * (1.0 + gc * (1.0 - sc)))).astype(BF16)
        du3 = dyu * (gc * sc) * (s3 * (1.0 + u3 * (1.0 - s3)))
        sg_ref[0:1, :] = sg_ref[0:1, :] + jnp.sum(du3 * uh, axis=0, keepdims=True)
        sg_ref[1:2, :] = sg_ref[1:2, :] + jnp.sum(du3, axis=0, keepdims=True)
        duh = du3 * lg_ref[...]
        du2 = rstd * (duh - jnp.mean(duh, axis=-1, keepdims=True)
                      - uh * jnp.mean(duh * uh, axis=-1, keepdims=True))
        sg_ref[2:3, :] = sg_ref[2:3, :] + jnp.sum(du2, axis=0, keepdims=True)
        du2_ref[...] = du2

    row = lambda b, s: b * nsb + s
    tspec = lambda w, cb=0: pl.BlockSpec((tm, w), lambda b, s: (row(b, s), cb))
    hspec = pl.BlockSpec((1, H, tm, LANES), lambda b, s: (b, 0, s, 0))
    vspec = pl.BlockSpec((1, CW), lambda b, s: (0, 0))
    return pl.pallas_call(
        body, name="bwd_prep", grid=(B, nsb),
        in_specs=[tspec(FW, 0), tspec(CW, 1), tspec(FW, 3), tspec(CW, 6), tspec(FW), hspec, hspec,
                  tspec(CW), vspec, vspec],
        out_specs=[tspec(FW), tspec(CW), tspec(CW), hspec, hspec,
                   pl.BlockSpec((8, CW), lambda b, s: (0, 0))],
        out_shape=[SDS((T, FW), BF16), SDS((T, CW), BF16), SDS((T, CW), F32),
                   SDS((B, H, S, LANES), BF16), SDS((B, H, S, LANES), BF16), SDS((8, CW), F32)],
        compiler_params=_params(("arbitrary", "arbitrary")),
    )(dy, dy, z, z, a_nat, oa, qa, u2, ln_g, ln_b)


def _conv_bwd(du2, z, conv_w, B, S, n_taps, tm):
    T, CW = du2.shape
    nsb = S // tm
    hb = tm // HALO

    def body(d_ref, dh_ref, ga_ref, gb_ref, ha_ref, hb_ref, w_ref, dz_ref, dw_ref,
             extu_ref, extd_ref, shu_ref, shd_ref, du1_ref, dwacc_ref):
        s = pl.program_id(1)
        first_step = (pl.program_id(0) == 0) & (s == 0)
        last_step = (pl.program_id(0) == B - 1) & (s == nsb - 1)

        @pl.when(first_step)
        def _():
            dwacc_ref[...] = jnp.zeros_like(dwacc_ref)

        ga = ga_ref[...]
        sb = _sigmoid(gb_ref[...])
        halo = ha_ref[...] * _sigmoid(hb_ref[...])
        extu_ref[0:HALO, :] = jnp.where(s > 0, halo, 0.0)
        extu_ref[HALO:, :] = ga * sb
        extd_ref[0:tm, :] = d_ref[...]
        extd_ref[tm:, :] = jnp.where(s < nsb - 1, dh_ref[...], 0.0)
        _fill_shifts(extu_ref, shu_ref)
        _fill_shifts(extd_ref, shd_ref)
        _conv_taps(w_ref, extd_ref, shd_ref, du1_ref, n_taps, tm, lambda j: n_taps - 1 - j)
        for cc in range(CW // LANES):
            cols = slice(cc * LANES, (cc + 1) * LANES)
            parts = [None] * n_taps
            for r in range(tm // SUBLANES):
                dv = d_ref[r * SUBLANES:(r + 1) * SUBLANES, cols]
                for j in range(n_taps):
                    off = HALO - (n_taps - 1) + j + r * SUBLANES
                    term = dv * _tap_window(extu_ref, shu_ref, off, SUBLANES, cols)
                    parts[j] = term if parts[j] is None else parts[j] + term
            for j in range(n_taps):
                rows = slice(j * SUBLANES, (j + 1) * SUBLANES)
                dwacc_ref[rows, cols] = dwacc_ref[rows, cols] + parts[j]
        du1 = du1_ref[...]
        dz_ref[:, :CW] = (du1 * sb).astype(BF16)
        dz_ref[:, CW:] = (du1 * ga * (sb * (1.0 - sb))).astype(BF16)

        @pl.when(last_step)
        def _():
            dw_ref[...] = jnp.zeros_like(dw_ref)
            for j in range(n_taps):
                dw_ref[j:j + 1, :] = jnp.sum(dwacc_ref[j * SUBLANES:(j + 1) * SUBLANES, :], axis=0, keepdims=True)

    row = lambda b, s: b * nsb + s
    last_halo = T // HALO - 1
    return pl.pallas_call(
        body, name="conv_bwd", grid=(B, nsb),
        in_specs=[pl.BlockSpec((tm, CW), lambda b, s: (row(b, s), 0)),
                  pl.BlockSpec((HALO, CW), lambda b, s: (jnp.minimum((row(b, s) + 1) * hb, last_halo), 0)),
                  pl.BlockSpec((tm, CW), lambda b, s: (row(b, s), 4)),
                  pl.BlockSpec((tm, CW), lambda b, s: (row(b, s), 5)),
                  pl.BlockSpec((HALO, CW), lambda b, s: (jnp.maximum(row(b, s) * hb - 1, 0), 4)),
                  pl.BlockSpec((HALO, CW), lambda b, s: (jnp.maximum(row(b, s) * hb - 1, 0), 5)),
                  pl.BlockSpec((HALO, CW), lambda b, s: (0, 0))],
        out_specs=[pl.BlockSpec((tm, 2 * CW), lambda b, s: (row(b, s), 0)),
                   pl.BlockSpec((HALO, CW), lambda b, s: (0, 0))],
        out_shape=[SDS((T, 2 * CW), BF16), SDS((HALO, CW), F32)],
        scratch_shapes=[pltpu.VMEM((tm + HALO, CW), F32), pltpu.VMEM((tm + HALO, CW), F32),
                        pltpu.VMEM((SUBLANES - 1, tm + HALO - SUBLANES, CW), F32),
                        pltpu.VMEM((SUBLANES - 1, tm + HALO - SUBLANES, CW), F32),
                        pltpu.VMEM((tm, CW), F32), pltpu.VMEM((HALO * SUBLANES, CW), F32)],
        compiler_params=_params(("arbitrary", "arbitrary")),
    )(du2, du2, z, z, z, z, conv_w)


def _attn_bwd(qb, ka, va, doa, t, hb):
    B, H, S, _ = qb.shape
    nk = S // t

    def body(q_ref, k_ref, v_ref, do_ref, dq_ref, dk_ref, dv_ref, dv_acc):
        j = pl.program_id(2)

        @pl.when(j == 0)
        def _():
            dq_ref[...] = jnp.zeros_like(dq_ref)

        dk_ref[...] = jnp.zeros_like(dk_ref)
        dv_acc[...] = jnp.zeros_like(dv_acc)

        def step(i, masked):
            q_rows = pl.ds(pl.multiple_of(i * t, t), t)
            for e in range(hb):
                k = k_ref[0, e]
                q = q_ref[0, e, q_rows, :]
                do = do_ref[0, e, q_rows, :]
                p = jnp.exp(_dot_nt(q, k))
                if masked:
                    keep = lax.broadcasted_iota(jnp.int32, (t, t), 0) >= lax.broadcasted_iota(jnp.int32, (t, t), 1)
                    p = jnp.where(keep, p, 0.0)
                ds = (p * _dot_nt(do, v_ref[0, e])).astype(BF16)
                dv_acc[e] = dv_acc[e] + _dot_tn(p.astype(BF16), do)
                dk_ref[0, e] = dk_ref[0, e] + _dot_tn(ds, q)
                dq_ref[0, e, q_rows, :] = dq_ref[0, e, q_rows, :] + _dot(ds, k)

        step(j, True)

        def loop_body(i, carry):
            step(i, False)
            return carry

        lax.fori_loop(j + 1, nk, loop_body, 0)
        dv_ref[0] = dv_acc[...].astype(BF16)

    full = pl.BlockSpec((1, hb, S, LANES), lambda b, h, j: (b, h, 0, 0))
    blk = pl.BlockSpec((1, hb, t, LANES), lambda b, h, j: (b, h, j, 0))
    oshape = SDS((B, H, S, LANES), F32)
    return pl.pallas_call(
        body, name="attn_bwd", grid=(B, H // hb, nk),
        in_specs=[full, blk, blk, full],
        out_specs=[full, blk, blk],
        out_shape=[oshape, oshape, SDS((B, H, S, LANES), BF16)],
        scratch_shapes=[pltpu.VMEM((hb, t, LANES), F32)],
        compiler_params=_params(("parallel", "parallel", "arbitrary")),
    )(qb, ka, va, doa)


def _qk_bwd(dqa, dka, dva, z, gq, gk, B, S, H, tm):
    T = B * S
    FW = H * HEAD_DIM
    nsb = S // tm
    nfb = FW // LANES
    scale = HEAD_DIM ** -0.5

    def body(dq_ref, dk_ref, dv_ref, zq_ref, zk_ref, gq_ref, gk_ref, dzq_ref, dzk_ref, dzv_ref, dc_ref, dg_ref):
        p = pl.program_id(0)

        @pl.when(pl.program_id(1) == 0)
        def _():
            dg_ref[...] = jnp.zeros_like(dg_ref)

        lane = _lane((tm, LANES))
        lo = lane < HEAD_DIM

        def natural(ref):
            return jnp.where(lo, ref[0, 0].astype(F32), pltpu.roll(ref[0, 1].astype(F32), HEAD_DIM, 1))

        def norm_bwd(dn, x, g, row, out_ref):
            r = lax.rsqrt(_half_stats(x * x) * (1.0 / HEAD_DIM) + EPS)
            xh = x * r
            dg_ref[row:row + 1, :] = dg_ref[row:row + 1, :] + jnp.sum(dn * xh, axis=0, keepdims=True)
            dxh = dn * g
            mm = _half_stats(dxh * xh) * (1.0 / HEAD_DIM)
            out_ref[...] = (r * (dxh - xh * mm)).astype(BF16)

        norm_bwd(natural(dq_ref) * scale, zq_ref[...], gq_ref[...], 0, dzq_ref)
        norm_bwd(natural(dk_ref), zk_ref[...], gk_ref[...], 1, dzk_ref)
        dzv_ref[...] = natural(dv_ref).astype(BF16)

        dc = jnp.zeros((tm, LANES), F32)
        for e in range(2):
            val = _lane_col(dq_ref[0, e], lane, L_ROWSUM) - _lane_col(dk_ref[0, e], lane, L_KDECAY)
            dc = jnp.where(lane == 2 * p + e, val, dc)
        dc_ref[0] = dc

    hspec = pl.BlockSpec((1, 2, tm, LANES), lambda p, i: (i // nsb, p, i % nsb, 0))
    zspec = lambda off: pl.BlockSpec((tm, LANES), lambda p, i: (i, off + p))
    gspec = pl.BlockSpec((1, LANES), lambda p, i: (0, p))
    ospec = pl.BlockSpec((tm, LANES), lambda p, i: (i, p))
    return pl.pallas_call(
        body, name="qk_bwd", grid=(H // 2, T // tm),
        in_specs=[hspec, hspec, hspec, zspec(0), zspec(nfb), gspec, gspec],
        out_specs=[ospec, ospec, ospec,
                   pl.BlockSpec((1, tm, LANES), lambda p, i: (p, i, 0)),
                   pl.BlockSpec((8, LANES), lambda p, i: (0, p))],
        out_shape=[SDS((T, FW), BF16), SDS((T, FW), BF16), SDS((T, FW), BF16),
                   SDS((H // 2, T, LANES), F32), SDS((8, FW), F32)],
        compiler_params=_params(("parallel", "arbitrary")),
    )(dqa, dka, dva, z, z, gq, gk)


def _gate_bwd(dc8, z, b_pad, B, S, H, col_blk, fp, tc):
    T = B * S
    nsb = S // tc
    npair = dc8.shape[0]

    def body(dc_ref, zf_ref, b_ref, dz_ref, db_ref, carry):
        first_step = (pl.program_id(0) == 0) & (pl.program_id(1) == 0)

        @pl.when(first_step)
        def _():
            db_ref[...] = jnp.zeros_like(db_ref)

        @pl.when(pl.program_id(1) == 0)
        def _():
            carry[...] = jnp.zeros_like(carry)

        dc = dc_ref[0]
        for k in range(1, npair):
            dc = dc + dc_ref[k]
        dlf = _tri_cumsum(dc, True) + carry[...]
        carry[...] = carry[...] + jnp.sum(dc, axis=0, keepdims=True)
        x = zf_ref[...] + b_ref[...]
        dlogit = dlf * _sigmoid(-x)
        db_ref[0:1, :] = db_ref[0:1, :] + jnp.sum(dlogit, axis=0, keepdims=True)
        dz_ref[...] = jnp.zeros_like(dz_ref)
        dz_ref[:, :LANES] = dlogit.astype(BF16)

    rrow = lambda b, s: b * nsb + (nsb - 1 - s)
    return pl.pallas_call(
        body, name="gate_bwd", grid=(B, nsb),
        in_specs=[pl.BlockSpec((npair, tc, LANES), lambda b, s: (0, rrow(b, s), 0)),
                  pl.BlockSpec((tc, LANES), lambda b, s: (rrow(b, s), col_blk)),
                  pl.BlockSpec((1, LANES), lambda b, s: (0, 0))],
        out_specs=[pl.BlockSpec((tc, fp), lambda b, s: (rrow(b, s), 0)),
                   pl.BlockSpec((8, LANES), lambda b, s: (0, 0))],
        out_shape=[SDS((T, fp), BF16), SDS((8, LANES), F32)],
        scratch_shapes=[pltpu.VMEM((1, LANES), F32)],
        compiler_params=_params(("arbitrary", "arbitrary")),
    )(dc8, z, b_pad)


def _matmul_tn(a, b, name, tmm, tn, tk):
    T, M = a.shape
    N = b.shape[1]
    tmm, tn, tk = min(tmm, M), min(tn, N), min(tk, T)

    def body(a_ref, b_ref, o_ref):
        @pl.when(pl.program_id(2) == 0)
        def _():
            o_ref[...] = jnp.zeros_like(o_ref)

        o_ref[...] = o_ref[...] + _dot_tn(a_ref[...], b_ref[...])

    return pl.pallas_call(
        body, name=name, grid=(M // tmm, N // tn, T // tk),
        in_specs=[pl.BlockSpec((tk, tmm), lambda i, j, k: (k, i)),
                  pl.BlockSpec((tk, tn), lambda i, j, k: (k, j))],
        out_specs=pl.BlockSpec((tmm, tn), lambda i, j, k: (i, j)),
        out_shape=SDS((M, N), F32),
        compiler_params=_params(("parallel", "parallel", "arbitrary")),
    )(a, b)


def _dh_rms_bwd(pieces, w_t, x2, g, dout, tm, tk, parts):
    T, D = x2.shape
    nks = [p.shape[1] // tk for p in pieces]
    starts = [sum(nks[:k]) for k in range(len(pieces))]
    nk = sum(nks)
    ni = T // tm
    n = len(parts)

    def body(*refs):
        dz_refs = refs[:len(pieces)]
        w_ref, x_ref, g_ref, do_ref = refs[len(pieces):len(pieces) + 4]
        part_refs = refs[len(pieces) + 4:len(pieces) + 4 + n]
        gx_ref, dg_ref = refs[len(pieces) + 4 + n:len(pieces) + 6 + n]
        slot_refs = refs[len(pieces) + 6 + n:len(pieces) + 6 + 2 * n]
        acc_ref, send_sems, recv_sems = refs[len(pieces) + 6 + 2 * n:]
        k = pl.program_id(1)
        first_step = (pl.program_id(0) == 0) & (k == 0)
        last_step = (pl.program_id(0) == ni - 1) & (k == nk - 1)
        x, y, c = _place()
        chips = [(1 - x, y), (x, 1 - y), (1 - x, 1 - y)]

        def copy(a, f, to):
            cx, cy = chips[f]
            return pltpu.make_async_remote_copy(
                src_ref=part_refs[a].at[2 * cx + cy], dst_ref=slot_refs[a].at[f],
                send_sem=send_sems.at[a * 3 + f], recv_sem=recv_sems.at[a * 3 + f],
                device_id=to, device_id_type=MESH)

        @pl.when(first_step)
        def _():
            dg_ref[...] = jnp.zeros_like(dg_ref)
            for a in range(n):
                for f in range(3):
                    copy(a, f, (*chips[f], c)).start()

        @pl.when(last_step)
        def _():
            for a in range(n):
                for f in range(3):
                    copy(a, f, (x, y, c)).wait_recv()
            for a in range(n):
                for f in range(3):
                    copy(a, f, (*chips[f], c)).wait_send()

        @pl.when(k == 0)
        def _():
            acc_ref[...] = jnp.zeros_like(acc_ref)

        for dz_ref, st, cnt in zip(dz_refs, starts, nks):
            @pl.when((k >= st) & (k < st + cnt))
            def _(dz_ref=dz_ref):
                acc_ref[...] = acc_ref[...] + _dot(dz_ref[...], w_ref[...])

        @pl.when(k == nk - 1)
        def _():
            x = x_ref[...]
            r = lax.rsqrt(jnp.mean(x * x, axis=-1, keepdims=True) + EPS)
            xh = x * r
            dh = acc_ref[...]
            dg_ref[0:1, :] = dg_ref[0:1, :] + jnp.sum(dh * xh, axis=0, keepdims=True)
            dxn = dh * g_ref[...]
            gx_ref[...] = do_ref[...] + r * (dxn - xh * jnp.mean(dxn * xh, axis=-1, keepdims=True))

    def piece_spec(st, cnt):
        return pl.BlockSpec((tm, tk), lambda i, k: (i, jnp.clip(k - st, 0, cnt - 1)))

    tspec = pl.BlockSpec((tm, D), lambda i, k: (i, 0))
    return pl.pallas_call(
        body, name="dh_rms_bwd", grid=(T // tm, nk),
        in_specs=[piece_spec(st, cnt) for st, cnt in zip(starts, nks)]
        + [pl.BlockSpec((tk, D), lambda i, k: (k, 0)), tspec, pl.BlockSpec((1, D), lambda i, k: (0, 0)), tspec]
        + [ANY] * n,
        out_specs=[tspec, pl.BlockSpec((8, D), lambda i, k: (0, 0))] + [ANY] * n,
        out_shape=[SDS((T, D), F32), SDS((8, D), F32)] + [SDS((3,) + p.shape[1:], p.dtype) for p in parts],
        scratch_shapes=[pltpu.VMEM((tm, D), F32),
                        pltpu.SemaphoreType.DMA((3 * n,)), pltpu.SemaphoreType.DMA((3 * n,))],
        compiler_params=_params(("arbitrary", "arbitrary")),
    )(*pieces, w_t, x2, g, dout, *parts)


def _block_plan(R, C, tr, tc):
    br = min(tr, R)
    if R % br == 0:
        return (br, C), R // br, lambda i: (i, 0)
    bc = min(tc, C)
    assert C % bc == 0
    return (R, bc), C // bc, lambda i: (0, i)


def _ew_call(body, name, ins, n_out, out_dtypes, tr, tc):
    R, C = ins[0].shape
    blk, steps, imap = _block_plan(R, C, tr, tc)
    spec = pl.BlockSpec(blk, imap)
    return pl.pallas_call(
        body, name=name, grid=(steps,),
        in_specs=[spec] * len(ins), out_specs=[spec] * n_out,
        out_shape=[SDS((R, C), dt) for dt in out_dtypes],
        compiler_params=_params(("parallel",)),
    )(*ins)


def _sum_slots(slots, name, first=None, tr=256):
    n, R, C = slots.shape
    blk, steps, imap = _block_plan(R, C, tr, 2 * LANES)
    lead = [] if first is None else [first]

    def body(*refs):
        s_ref, o_ref = refs[-2:]
        acc = refs[0][...].astype(F32) if lead else s_ref[0].astype(F32)
        for k in range(0 if lead else 1, n):
            acc = acc + s_ref[k].astype(F32)
        o_ref[...] = acc

    return pl.pallas_call(
        body, name=name, grid=(steps,),
        in_specs=[pl.BlockSpec(blk, imap)] * len(lead) + [pl.BlockSpec((n,) + blk, lambda i: (0,) + imap(i))],
        out_specs=pl.BlockSpec(blk, imap),
        out_shape=SDS((R, C), F32),
        compiler_params=_params(("parallel",)),
    )(*lead, slots)


def _adamw_update(w, g, m, v):
    nm = ADAM_B1 * m + (1.0 - ADAM_B1) * g
    nv = ADAM_B2 * v + (1.0 - ADAM_B2) * (g * g)
    m_hat = nm / (1.0 - ADAM_B1 ** ADAM_STEP)
    v_hat = nv / (1.0 - ADAM_B2 ** ADAM_STEP)
    return -ADAM_LR * (m_hat / (jnp.sqrt(v_hat) + ADAM_EPS) + ADAM_WD * w), nm, nv


def _adamw(w, g, m, v, name):
    def body(w_ref, g_ref, m_ref, v_ref, d_ref, nm_ref, nv_ref):
        d_ref[...], nm_ref[...], nv_ref[...] = _adamw_update(w_ref[...], g_ref[...], m_ref[...], v_ref[...])

    return _ew_call(body, name, [w, g, m, v], 3, [F32, F32, F32], 128, 2 * LANES)


def _adamw_halves(w, mine, other, m, v, name):
    R, C = w.shape
    half = C // 2
    bc = min(2 * LANES, half)
    per = half // bc

    def body(w_ref, a_ref, b_ref, m_ref, v_ref, g_ref, d_ref, nm_ref, nv_ref):
        g = jnp.where(pl.program_id(0) // per == lax.axis_index("c"), a_ref[...], b_ref[...])
        g_ref[...] = g
        d_ref[...], nm_ref[...], nv_ref[...] = _adamw_update(w_ref[...], g, m_ref[...], v_ref[...])

    full = pl.BlockSpec((R, bc), lambda i: (0, i))
    part = pl.BlockSpec((R, bc), lambda i: (0, i % per))
    return pl.pallas_call(
        body, name=name, grid=(C // bc,),
        in_specs=[full, part, part, full, full], out_specs=[full] * 4,
        out_shape=[SDS((R, C), F32)] * 4,
        compiler_params=_params(("parallel",)),
    )(w, mine, other, m, v)


ANY = pl.BlockSpec(memory_space=pl.ANY)


def _place():
    return lax.axis_index("x"), lax.axis_index("y"), lax.axis_index("c")


def _gather_chips(shards, splits):
    n = len(shards)
    per = 7

    def body(*refs):
        ins, outs = refs[:n], refs[n:2 * n]
        send_sems, recv_sems = refs[2 * n:]
        x, y, c = _place()
        mine = 2 * x + y
        chips = [(1 - x, y), (x, 1 - y), (1 - x, 1 - y)]

        def rows(a, half):
            return pl.ds(0, splits[a]) if half == 0 else pl.ds(splits[a], ins[a].shape[0] - splits[a])

        def copy(a, k, chip_idx, half, to, src=None):
            dst = outs[a].at[chip_idx, rows(a, half)]
            return pltpu.make_async_remote_copy(
                src_ref=dst if src is None else src, dst_ref=dst,
                send_sem=send_sems.at[a * per + k], recv_sem=recv_sems.at[a * per + k],
                device_id=to, device_id_type=MESH)

        def own(a, to):
            return pltpu.make_async_remote_copy(
                src_ref=ins[a], dst_ref=outs[a].at[mine],
                send_sem=send_sems.at[a * per + 6], recv_sem=recv_sems.at[a * per + 6],
                device_id=to, device_id_type=MESH)

        for cc in (0, 1):
            @pl.when(c == cc)
            def _(cc=cc):
                me, sibling = (x, y, cc), (x, y, 1 - cc)
                first = [copy(a, k, mine, cc, (*chip, cc), src=ins[a].at[rows(a, cc)])
                         for a in range(n) for k, chip in enumerate(chips)]
                first += [own(a, sibling) for a in range(n)]
                for cp in first:
                    cp.start()
                passed = []
                for k, (cx, cy) in enumerate(chips):
                    for a in range(n):
                        copy(a, k, 2 * cx + cy, cc, me).wait_recv()
                        fwd = copy(a, 3 + k, 2 * cx + cy, cc, sibling)
                        fwd.start()
                        passed.append(fwd)
                for k, (cx, cy) in enumerate(chips):
                    for a in range(n):
                        copy(a, 3 + k, 2 * cx + cy, 1 - cc, me).wait_recv()
                for a in range(n):
                    own(a, me).wait_recv()
                for cp in first + passed:
                    cp.wait_send()

    return pl.pallas_call(
        body, name="gather_chips",
        in_specs=[ANY] * n, out_specs=[ANY] * n,
        out_shape=[SDS((4,) + s.shape, s.dtype) for s in shards],
        scratch_shapes=[pltpu.SemaphoreType.DMA((per * n,)), pltpu.SemaphoreType.DMA((per * n,))],
    )(*shards)


def _pair_swap(arrs):
    n = len(arrs)

    def body(*refs):
        ins, outs = refs[:n], refs[n:2 * n]
        send_sems, recv_sems = refs[2 * n:]
        x, y, c = _place()
        for cc in (0, 1):
            @pl.when(c == cc)
            def _(cc=cc):
                copies = []
                for a in range(n):
                    half = ins[a].shape[1] // 2
                    copies.append(pltpu.make_async_remote_copy(
                        src_ref=ins[a].at[:, pl.ds((1 - cc) * half, half)], dst_ref=outs[a],
                        send_sem=send_sems.at[a], recv_sem=recv_sems.at[a],
                        device_id=(x, y, 1 - cc), device_id_type=MESH))
                for cp in copies:
                    cp.start()
                for cp in copies:
                    cp.wait()

    return pl.pallas_call(
        body, name="pair_swap",
        in_specs=[ANY] * n, out_specs=[ANY] * n,
        out_shape=[SDS((h.shape[0], h.shape[1] // 2), h.dtype) for h in arrs],
        scratch_shapes=[pltpu.SemaphoreType.DMA((n,)), pltpu.SemaphoreType.DMA((n,))],
    )(*arrs)


def _pair_sum(arrs, got, core, tr):
    half = arrs[0].shape[1] // 2
    cnts = [p.shape[0] // tr for p in arrs]
    starts = [sum(cnts[:k]) for k in range(len(arrs))]

    def body(core_ref, *refs):
        del core_ref
        own_refs, got_refs, o_ref = refs[:len(arrs)], refs[len(arrs):2 * len(arrs)], refs[-1]
        s = pl.program_id(0)
        for own_ref, got_ref, st, cnt in zip(own_refs, got_refs, starts, cnts):
            @pl.when((s >= st) & (s < st + cnt))
            def _(own_ref=own_ref, got_ref=got_ref):
                o_ref[...] = (own_ref[...] + got_ref[...]).astype(BF16)

    def own_spec(st, cnt):
        return pl.BlockSpec((tr, half), lambda s, core_ref: (jnp.clip(s - st, 0, cnt - 1), core_ref[0]))

    def got_spec(st, cnt):
        return pl.BlockSpec((tr, half), lambda s, core_ref: (jnp.clip(s - st, 0, cnt - 1), 0))

    return pl.pallas_call(
        body, name="pair_sum",
        grid_spec=pltpu.PrefetchScalarGridSpec(
            num_scalar_prefetch=1, grid=(sum(cnts),),
            in_specs=[own_spec(st, cnt) for st, cnt in zip(starts, cnts)]
            + [got_spec(st, cnt) for st, cnt in zip(starts, cnts)],
            out_specs=pl.BlockSpec((tr, half), lambda s, core_ref: (s, 0))),
        out_shape=SDS((sum(cnts) * tr, half), BF16),
        compiler_params=_params(("arbitrary",)),
    )(core, *arrs, *got)


def _share_results(arrs, rows):
    n = len(arrs)
    flips = [(fx, fy, fc) for fx in (0, 1) for fy in (0, 1) for fc in (0, 1)][1:]

    def body(*refs):
        ins, rows_ref, outs, all_ref = refs[:n], refs[n], refs[n + 1:2 * n + 1], refs[2 * n + 1]
        send_sems, recv_sems, local_sem = refs[2 * n + 2:]
        x, y, c = _place()
        me = 4 * x + 2 * y + c
        local = pltpu.make_async_copy(rows_ref, all_ref.at[me], local_sem)
        local.start()
        copies = [pltpu.make_async_remote_copy(
            src_ref=ins[a], dst_ref=outs[a], send_sem=send_sems.at[a], recv_sem=recv_sems.at[a],
            device_id=(x, y, 1 - c), device_id_type=MESH) for a in range(n)]
        for k, (fx, fy, fc) in enumerate(flips):
            copies.append(pltpu.make_async_remote_copy(
                src_ref=rows_ref, dst_ref=all_ref.at[me], send_sem=send_sems.at[n + k], recv_sem=recv_sems.at[n + k],
                device_id=(x ^ fx, y ^ fy, c ^ fc), device_id_type=MESH))
        for cp in copies:
            cp.start()
        for cp in copies[:n]:
            cp.wait_recv()
        for k, (fx, fy, fc) in enumerate(flips):
            src = 4 * (x ^ fx) + 2 * (y ^ fy) + (c ^ fc)
            pltpu.make_async_remote_copy(
                src_ref=rows_ref, dst_ref=all_ref.at[src], send_sem=send_sems.at[n + k], recv_sem=recv_sems.at[n + k],
                device_id=(x, y, c), device_id_type=MESH).wait_recv()
        for cp in copies:
            cp.wait_send()
        local.wait()

    outs = pl.pallas_call(
        body, name="share_results",
        in_specs=[ANY] * (n + 1), out_specs=[ANY] * (n + 1),
        out_shape=[SDS(h.shape, h.dtype) for h in arrs] + [SDS((8,) + rows.shape, rows.dtype)],
        scratch_shapes=[pltpu.SemaphoreType.DMA((n + 7,)), pltpu.SemaphoreType.DMA((n + 7,)),
                        pltpu.SemaphoreType.DMA],
    )(*arrs, rows)
    return outs[:n], outs[n]


def _tiles(S, FW):
    big = FW % 512 == 0
    return dict(
        fp=512 if big else LANES,
        tn=1536 if big else LANES,
        tm_in=min(1024, S),
        t_attn=min(512, S),
        hb_fwd=4,
        hb_bwd=4,
        tm_prep=min(512, S),
        tm_mix=min(128, S),
        tc=min(256, S),
        tk=512 if big else LANES,
    )


def kernel(x, norm_g, w_in, b_forget, q_norm_g, k_norm_g, conv_w, conv_b, conv_ln_g, conv_ln_b, w_out, loss_target, m_norm_g, m_w_in, m_b_forget, m_q_norm_g, m_k_norm_g, m_conv_w, m_conv_b, m_conv_ln_g, m_conv_ln_b, m_w_out, v_norm_g, v_w_in, v_b_forget, v_q_norm_g, v_k_norm_g, v_conv_w, v_conv_b, v_conv_ln_g, v_conv_ln_b, v_w_out):
    B, S, D = x.shape
    H, dh = q_norm_g.shape[1:]
    FW = H * dh
    CW = conv_b.shape[-1]
    n_taps, cw_shard = conv_w.shape[1:]
    in_shard = w_in.shape[2]
    out_shard = w_out.shape[1]
    assert dh == HEAD_DIM and H % 2 == 0 and H <= LANES and FW == CW == D
    assert n_taps - 1 <= HALO and 4 * cw_shard == CW and 4 * out_shard == FW + CW
    assert 4 * in_shard == 4 * FW + 3 * CW + H
    T = B * S
    tl = _tiles(S, FW)
    fp = tl["fp"]
    xi, yi, ci = _place()

    w_t = jnp.transpose(w_in[0])
    conv_pad = jnp.pad(conv_w[0], ((0, HALO - n_taps), (0, 0)))
    bf16_rows = 2 * SUBLANES
    g_in, g_out, g_cw = _gather_chips(
        [w_t.astype(BF16), w_out[0].astype(BF16), conv_pad],
        [in_shard // 2 // bf16_rows * bf16_rows, out_shard // 2, HALO // 2])
    w_t_full = g_in.reshape(4 * in_shard, D)
    w_out_full = g_out.reshape(FW + CW, D)
    conv_full = g_cw.transpose(1, 0, 2).reshape(HALO, CW)
    o_f = 3 * FW
    w_pack = jnp.concatenate([w_t_full[:o_f], w_t_full[o_f + H:],
                              jnp.pad(w_t_full[o_f:o_f + H], ((0, fp - H), (0, 0)))], axis=0)
    f_col = 4 * FW + 3 * CW

    x2 = x.reshape(T, D)
    tgt = loss_target.reshape(T, D)
    b_pad = jnp.pad(b_forget, ((0, 0), (0, LANES - H)))
    gq = q_norm_g.reshape(1, FW)
    gk = k_norm_g.reshape(1, FW)

    z, h = _fwd_in(x2, norm_g, w_pack, tl["tm_in"], tl["tn"])
    c = _gate_fwd(z, b_pad, B, S, H, f_col // LANES, tl["tc"])
    qa, ka, va = _attn_prep(z, c, gq, gk, B, S, H, tl["tm_prep"])
    gain = lambda g: jnp.max(jnp.abs(g[0]), axis=-1)
    bound = (NORM_SLACK ** 2 * dh ** 0.5) * gain(q_norm_g) * gain(k_norm_g)
    oa = _attn_fwd(qa, ka, va, jnp.broadcast_to(bound[:, None, None], (H, 1, LANES)), tl["t_attn"], tl["hb_fwd"])
    y, u2, a_nat, dout, dout_b, dy, loss_acc = _fwd_out(
        oa, z, x2, tgt, conv_full, conv_b, conv_ln_g, conv_ln_b, w_out_full, B, S, H, n_taps, tl["tm_mix"])

    dzgf, dzgc, du2, doa, qb, sg_conv = _bwd_prep(dy, z, a_nat, oa, qa, u2, conv_ln_g, conv_ln_b, B, S, H, tl["tm_mix"])
    dzglu, dconv_w = _conv_bwd(du2, z, conv_full, B, S, n_taps, tl["tm_mix"])
    dqa, dka, dva = _attn_bwd(qb, ka, va, doa, tl["t_attn"], tl["hb_bwd"])
    dzq, dzk, dzv, dc8, dg_qk = _qk_bwd(dqa, dka, dva, z, gq, gk, B, S, H, tl["tm_prep"])
    dzf, db_f = _gate_bwd(dc8, z, b_pad, B, S, H, f_col // LANES, fp, tl["tc"])
    pieces = [dzq, dzk, dzv, dzgf, dzglu, dzgc, dzf]
    dw_all = [_matmul_tn(p, h, f"dw_in_{k}", 1024, 1024, 1024) for k, p in enumerate(pieces)]
    dw_all.append(_matmul_tn(y, dout_b, "dw_out", 1024, 1024, 1024))

    summed = _pair_sum(dw_all, _pair_swap(dw_all), ci.astype(jnp.int32).reshape(1), fp)
    ends = [0]
    for t in dw_all:
        ends.append(ends[-1] + t.shape[0])
    spans, at = [], 0
    for k, rows in [(0, FW), (1, FW), (2, FW), (6, H), (3, FW), (4, 2 * CW), (5, CW)]:
        spans.append((at, rows, ends[k]))
        at += rows

    def chip_rows(j):
        lo, hi = j * in_shard, (j + 1) * in_shard
        return jnp.concatenate([summed[src + max(lo, a) - a:src + min(hi, a + n) - a]
                                for a, n, src in spans if max(lo, a) < min(hi, a + n)], axis=0)

    part_in = jnp.stack([chip_rows(j) for j in range(4)])
    part_out = summed[ends[7]:ends[8]].reshape(4, out_shard, D // 2)
    grad_x2, dg_norm, slots_in, slots_out = _dh_rms_bwd(
        pieces, w_pack, x2, norm_g, dout, tl["tm_in"], tl["tk"], [part_in, part_out])
    chip = 2 * xi + yi
    half_in = _sum_slots(slots_in, "chip_sum_in", lax.dynamic_index_in_dim(part_in, chip, 0, keepdims=False))
    half_out = _sum_slots(slots_out, "chip_sum_out", lax.dynamic_index_in_dim(part_out, chip, 0, keepdims=False))
    lanes_to_d = lambda t: jnp.pad(t, ((0, 0), (0, D - LANES)))
    small = jnp.concatenate([
        dg_norm[0:1], lanes_to_d(db_f[0:1, :]), dg_qk[0:1], dg_qk[1:2],
        sg_conv[2:3], sg_conv[0:1], sg_conv[1:2], dconv_w, lanes_to_d(loss_acc[0:1, :])], axis=0)
    n_small = small.shape[0]
    (other_in, other_out), all_small = _share_results([half_in, half_out], small)

    small_sum = _sum_slots(all_small, "small_sum", tr=n_small)
    loss = 0.5 * small_sum[n_small - 1, 0] / D
    grad_norm_g, grad_b_f = small_sum[0:1], small_sum[1:2, :H]
    grad_gq, grad_gk = small_sum[2:3].reshape(1, H, dh), small_sum[3:4].reshape(1, H, dh)
    grad_conv_b, grad_ln_g, grad_ln_b = small_sum[4:5], small_sum[5:6], small_sum[6:7]
    grad_conv_w = lax.dynamic_slice_in_dim(small_sum[7:7 + n_taps], chip * cw_shard, cw_shard, axis=1)

    in_t = _adamw_halves(w_t, half_in, other_in, jnp.transpose(m_w_in[0]), jnp.transpose(v_w_in[0]), "adamw_in")
    grad_w_in, d_in, nm_in, nv_in = (jnp.transpose(t)[None] for t in in_t)
    grad_w_out, d_out, nm_out, nv_out = (
        t[None] for t in _adamw_halves(w_out[0], half_out, other_out, m_w_out[0], v_w_out[0], "adamw_out"))
    d_cw, nm_cw, nv_cw = (t[None] for t in _adamw(conv_w[0], grad_conv_w, m_conv_w[0], v_conv_w[0], "adamw_conv_w"))

    def rows(ws):
        return jnp.concatenate([jnp.pad(t.reshape(1, -1), ((0, 0), (0, D - t.size))) for t in ws], axis=0)

    small_w = [norm_g, b_forget, q_norm_g, k_norm_g, conv_b, conv_ln_g, conv_ln_b]
    small_m = [m_norm_g, m_b_forget, m_q_norm_g, m_k_norm_g, m_conv_b, m_conv_ln_g, m_conv_ln_b]
    small_v = [v_norm_g, v_b_forget, v_q_norm_g, v_k_norm_g, v_conv_b, v_conv_ln_g, v_conv_ln_b]
    d_s, nm_s, nv_s = _adamw(rows(small_w), small_sum[0:7], rows(small_m), rows(small_v), "adamw_small")

    def unpack(t):
        return [t[k:k + 1, :w.size].reshape(w.shape) for k, w in enumerate(small_w)]

    def order(s, in_, cw, out_):
        ng, bf, qg, kg, cb, lg, lb = s
        return [ng, in_, bf, qg, kg, cw, cb, lg, lb, out_]

    grads = [grad_norm_g, grad_w_in, grad_b_f, grad_gq, grad_gk, grad_conv_w[None],
             grad_conv_b, grad_ln_g, grad_ln_b, grad_w_out]
    return (loss, grad_x2.reshape(B, S, D), *grads,
            *order(unpack(d_s), d_in, d_cw, d_out),
            *order(unpack(nm_s), nm_in, nm_cw, nm_out),
            *order(unpack(nv_s), nv_in, nv_cw, nv_out))
```

```python
import jax
import jax.numpy as jnp
from jax import lax
from jax.experimental import pallas as pl
from jax.experimental.pallas import tpu as pltpu

F32 = jnp.float32
BF16 = jnp.bfloat16
SDS = jax.ShapeDtypeStruct
MESH = pl.DeviceIdType.MESH

EPS = 1e-6
NEG_INF = -1e30
LANES = 128
SUBLANES = 8
HEAD_DIM = 64
HALO = 32
VMEM_LIMIT = 56 * 1024 * 1024

L_ROWSUM = 64
L_KDECAY = 67
L_LSE = 70
L_D = 65
NORM_SLACK = 1.02
SHIFT_MAX = 40.0

ADAM_LR = 0.001
ADAM_B1 = 0.9
ADAM_B2 = 0.999
ADAM_EPS = 1e-08
ADAM_WD = 0.01
ADAM_STEP = 10


def _params(sem, vmem=VMEM_LIMIT):
    return pltpu.CompilerParams(dimension_semantics=sem, vmem_limit_bytes=vmem)


def _sigmoid(x):
    return 1.0 / (1.0 + jnp.exp(-x))


def _split3(x):
    hi = x.astype(BF16).astype(F32)
    r = x - hi
    mid = r.astype(BF16).astype(F32)
    lo = (r - mid).astype(BF16).astype(F32)
    return hi, mid, lo


def _dot(a, b):
    return jnp.dot(a, b, preferred_element_type=F32)


def _dot_nt(a, b):
    return lax.dot_general(a, b, (((1,), (1,)), ((), ())), preferred_element_type=F32)


def _dot_tn(a, b):
    return lax.dot_general(a, b, (((0,), (0,)), ((), ())), preferred_element_type=F32)


def _lane(shape):
    return lax.broadcasted_iota(jnp.int32, shape, 1)


def _lane_col(x, lane, idx):
    return jnp.sum(jnp.where(lane == idx, x, 0.0), axis=-1, keepdims=True)


def _put3(base, lane, start, pieces):
    out = base
    for k, p in enumerate(pieces):
        out = jnp.where(lane == start + k, p, out)
    return out


def _half_stats(t):
    hi = t.astype(BF16)
    mid = (t - hi.astype(F32)).astype(BF16)
    row = lax.broadcasted_iota(jnp.int32, (2 * LANES, LANES), 0)
    col = lax.broadcasted_iota(jnp.int32, (2 * LANES, LANES), 1)
    same_half = (jnp.bitwise_and(row, LANES - 1) < HEAD_DIM) == (col < HEAD_DIM)
    return _dot(jnp.concatenate([hi, mid], axis=1), jnp.where(same_half, 1.0, 0.0).astype(BF16))


def _fwd_in(x2, g, w_t, tm, tn):
    T, D = x2.shape
    N = w_t.shape[0]

    def body(x_ref, g_ref, w_ref, z_ref, h_ref):
        @pl.when(pl.program_id(1) == 0)
        def _():
            x = x_ref[...]
            r = lax.rsqrt(jnp.mean(x * x, axis=-1, keepdims=True) + EPS)
            h_ref[...] = (x * r * g_ref[...]).astype(BF16)

        z_ref[...] = _dot_nt(h_ref[...], w_ref[...])

    return pl.pallas_call(
        body, name="fwd_in", grid=(T // tm, N // tn),
        in_specs=[pl.BlockSpec((tm, D), lambda i, j: (i, 0)),
                  pl.BlockSpec((1, D), lambda i, j: (0, 0)),
                  pl.BlockSpec((tn, D), lambda i, j: (j, 0))],
        out_specs=[pl.BlockSpec((tm, tn), lambda i, j: (i, j)),
                   pl.BlockSpec((tm, D), lambda i, j: (i, 0))],
        out_shape=[SDS((T, N), F32), SDS((T, D), BF16)],
        compiler_params=_params(("parallel", "arbitrary")),
    )(x2, g, w_t)


def _tri_cumsum(x, reverse):
    t = x.shape[0]
    row = lax.broadcasted_iota(jnp.int32, (t, t), 0)
    col = lax.broadcasted_iota(jnp.int32, (t, t), 1)
    tri = (row <= col) if reverse else (row >= col)
    tri = jnp.where(tri, 1.0, 0.0).astype(BF16)
    hi, mid, lo = _split3(x)
    return _dot(tri, hi.astype(BF16)) + _dot(tri, mid.astype(BF16)) + _dot(tri, lo.astype(BF16))


def _gate_fwd(z, b_pad, B, S, H, col_blk, tc):
    T = B * S
    nsb = S // tc

    def body(zf_ref, b_ref, c_ref, carry):
        @pl.when(pl.program_id(1) == 0)
        def _():
            carry[...] = jnp.zeros_like(carry)

        x = zf_ref[...] + b_ref[...]
        lf = jnp.minimum(x, 0.0) - jnp.log(1.0 + jnp.exp(-jnp.abs(x)))
        lf = jnp.where(_lane(lf.shape) < H, lf, 0.0)
        c_ref[...] = _tri_cumsum(lf, False) + carry[...]
        carry[...] = carry[...] + jnp.sum(lf, axis=0, keepdims=True)

    return pl.pallas_call(
        body, name="gate_fwd", grid=(B, nsb),
        in_specs=[pl.BlockSpec((tc, LANES), lambda b, s: (b * nsb + s, col_blk)),
                  pl.BlockSpec((1, LANES), lambda b, s: (0, 0))],
        out_specs=pl.BlockSpec((tc, LANES), lambda b, s: (b * nsb + s, 0)),
        out_shape=SDS((T, LANES), F32),
        scratch_shapes=[pltpu.VMEM((1, LANES), F32)],
        compiler_params=_params(("parallel", "arbitrary")),
    )(z, b_pad)


def _qk_normalize(x, g):
    r = lax.rsqrt(_half_stats(x * x) * (1.0 / HEAD_DIM) + EPS)
    return x * r * g


def _attn_prep(z, c, gq, gk, B, S, H, tm):
    T = B * S
    FW = H * HEAD_DIM
    nsb = S // tm
    nfb = FW // LANES
    scale = HEAD_DIM ** -0.5

    def body(zq_ref, zk_ref, zv_ref, c_ref, gq_ref, gk_ref, qa_ref, ka_ref, va_ref):
        p = pl.program_id(1)
        lane = _lane((tm, LANES))
        lo = lane < HEAD_DIM
        qn = _qk_normalize(zq_ref[...], gq_ref[...]) * scale
        kn = _qk_normalize(zk_ref[...], gk_ref[...])
        v = zv_ref[...]
        cc = c_ref[...]
        ones_q = ((lane >= L_KDECAY) & (lane < L_KDECAY + 3)).astype(F32)
        ones_k = (((lane >= L_ROWSUM) & (lane < L_ROWSUM + 3)) | ((lane >= L_LSE) & (lane < L_LSE + 3))).astype(F32)
        ones_v = ((lane >= L_ROWSUM) & (lane < L_D + 3)).astype(F32)
        for e in range(2):
            if e == 0:
                qe, ke, ve = qn, kn, v
            else:
                qe, ke, ve = (pltpu.roll(t, HEAD_DIM, 1) for t in (qn, kn, v))
            ch = _lane_col(cc, lane, 2 * p + e)
            pieces = _split3(ch)
            qa = jnp.where(lo, qe, _put3(ones_q, lane, L_ROWSUM, pieces))
            ka = jnp.where(lo, ke, _put3(ones_k, lane, L_KDECAY, [-t for t in pieces]))
            va = jnp.where(lo, ve, ones_v)
            qa_ref[0, e] = qa.astype(BF16)
            ka_ref[0, e] = ka.astype(BF16)
            va_ref[0, e] = va.astype(BF16)

    zspec = lambda off: pl.BlockSpec((tm, LANES), lambda i, p: (i, off + p))
    gspec = pl.BlockSpec((1, LANES), lambda i, p: (0, p))
    ospec = pl.BlockSpec((1, 2, tm, LANES), lambda i, p: (i // nsb, p, i % nsb, 0))
    oshape = SDS((B, H, S, LANES), BF16)
    return pl.pallas_call(
        body, name="attn_prep", grid=(T // tm, H // 2),
        in_specs=[zspec(0), zspec(nfb), zspec(2 * nfb),
                  pl.BlockSpec((tm, LANES), lambda i, p: (i, 0)), gspec, gspec],
        out_specs=[ospec, ospec, ospec],
        out_shape=[oshape, oshape, oshape],
        compiler_params=_params(("parallel", "arbitrary")),
    )(z, z, z, c, gq, gk)


def _attn_fwd(qa, ka, va, bound, t, hb):
    B, H, S, _ = qa.shape
    nq = S // t

    def body(q_ref, k_ref, v_ref, b_ref, o_ref, m_ref, acc_ref, qs_ref):
        i = pl.program_id(2)
        lane = _lane((t, LANES))

        shifts = [b_ref[e] for e in range(hb)]
        worst = shifts[0]
        for e in range(1, hb):
            worst = jnp.maximum(worst, shifts[e])
        bounded = jnp.max(worst) <= SHIFT_MAX
        acc_ref[...] = jnp.zeros_like(acc_ref)

        def tiles(step):
            def loop_body(j, carry):
                step(j, False)
                return carry

            lax.fori_loop(0, i, loop_body, 0)
            step(i, True)

        def keep_mask(n=t):
            return lax.broadcasted_iota(jnp.int32, (n, n), 0) >= lax.broadcasted_iota(jnp.int32, (n, n), 1)

        def finish(e, shift):
            acc = acc_ref[e]
            l = _lane_col(acc, lane, L_ROWSUM)
            o_ref[0, e] = jnp.where(lane < HEAD_DIM, acc / l, shift + jnp.log(l))

        @pl.when(bounded)
        def _():
            for e in range(hb):
                qs_ref[e] = _put3(q_ref[0, e].astype(F32), lane, L_LSE, _split3(-shifts[e])).astype(BF16)

            def pair(e, q_rows, k_start, n, masked):
                k_rows = pl.ds(pl.multiple_of(k_start, n), n)
                p = jnp.exp(_dot_nt(qs_ref[e, q_rows, :], k_ref[0, e, k_rows, :]))
                if masked:
                    p = jnp.where(keep_mask(n), p, 0.0)
                acc_ref[e, q_rows, :] = acc_ref[e, q_rows, :] + _dot(p.astype(BF16), v_ref[0, e, k_rows, :])

            def step(j, masked):
                for e in range(hb):
                    if masked:
                        h = t // 2
                        pair(e, slice(0, h), j * t, h, True)
                        pair(e, slice(h, t), j * t, h, False)
                        pair(e, slice(h, t), j * t + h, h, True)
                    else:
                        pair(e, slice(0, t), j * t, t, False)

            tiles(step)
            for e in range(hb):
                finish(e, shifts[e])

        @pl.when(jnp.logical_not(bounded))
        def _():
            m_ref[...] = jnp.full_like(m_ref, NEG_INF)

            def step(j, masked):
                rows = pl.ds(pl.multiple_of(j * t, t), t)
                for e in range(hb):
                    s = _dot_nt(q_ref[0, e], k_ref[0, e, rows, :])
                    if masked:
                        s = jnp.where(keep_mask(), s, NEG_INF)
                    m_prev = m_ref[e]
                    m_new = jnp.maximum(m_prev, jnp.max(s, axis=-1, keepdims=True))
                    alpha = jnp.exp(m_prev - m_new)
                    p = jnp.exp(s - m_new).astype(BF16)
                    acc_ref[e] = alpha * acc_ref[e] + _dot(p, v_ref[0, e, rows, :])
                    m_ref[e] = m_new

            tiles(step)
            for e in range(hb):
                finish(e, m_ref[e])

    return pl.pallas_call(
        body, name="attn_fwd", grid=(B, H // hb, nq),
        in_specs=[pl.BlockSpec((1, hb, t, LANES), lambda b, h, i: (b, h, i, 0)),
                  pl.BlockSpec((1, hb, S, LANES), lambda b, h, i: (b, h, 0, 0)),
                  pl.BlockSpec((1, hb, S, LANES), lambda b, h, i: (b, h, 0, 0)),
                  pl.BlockSpec((hb, 1, LANES), lambda b, h, i: (h, 0, 0))],
        out_specs=pl.BlockSpec((1, hb, t, LANES), lambda b, h, i: (b, h, i, 0)),
        out_shape=SDS((B, H, S, LANES), F32),
        scratch_shapes=[pltpu.VMEM((hb, t, 1), F32), pltpu.VMEM((hb, t, LANES), F32),
                        pltpu.VMEM((hb, t, LANES), BF16)],
        compiler_params=_params(("parallel", "parallel", "arbitrary")),
    )(qa, ka, va, bound)


def _fill_shifts(ext_ref, sh_ref):
    rows = sh_ref.shape[1]
    for b in range(1, SUBLANES):
        sh_ref[b - 1] = ext_ref[pl.ds(b, rows), :]


def _tap_window(ext_ref, sh_ref, off, tm, cols):
    b = off % SUBLANES
    if b == 0:
        return ext_ref[pl.ds(off, tm), cols]
    return sh_ref[b - 1, pl.ds(off - b, tm), cols]


def _conv_taps(w_ref, ext_ref, sh_ref, out_ref, n_taps, tm, offset_of, bias_ref=None):
    for cc in range(out_ref.shape[1] // LANES):
        cols = slice(cc * LANES, (cc + 1) * LANES)
        acc = None
        for j in sorted(range(n_taps), key=offset_of):
            term = w_ref[j:j + 1, cols] * _tap_window(ext_ref, sh_ref, offset_of(j), tm, cols)
            acc = term if acc is None else acc + term
        out_ref[:, cols] = acc if bias_ref is None else acc + bias_ref[:, cols]


def _layernorm_stats(u2):
    mu = jnp.mean(u2, axis=-1, keepdims=True)
    xc = u2 - mu
    rstd = lax.rsqrt(jnp.mean(xc * xc, axis=-1, keepdims=True) + EPS)
    return xc * rstd, rstd


def _fwd_out(oa, z, x2, tgt, conv_w, conv_b, ln_g, ln_b, w_out, B, S, H, n_taps, tm):
    T, D = x2.shape
    FW = H * HEAD_DIM
    CW = conv_w.shape[1]
    nsb = S // tm
    hb = tm // HALO
    mb = 4 if nsb % 4 == 0 else 1
    mt = mb * tm

    def body(oa_ref, gf_ref, ga_ref, gb_ref, gc_ref, ha_ref, hb_ref, x_ref, t_ref, w_ref, cb_ref, lg_ref,
             lb_ref, wo_ref, y_ref, u2_ref, a_ref, do_ref, dob_ref, dy_ref, loss_ref, ext_ref, sh_ref):
        first_step = (pl.program_id(0) == 0) & (pl.program_id(1) == 0)
        sub = lax.rem(pl.program_id(1), mb)
        rows = pl.ds(pl.multiple_of(sub * tm, tm), tm)

        @pl.when(first_step)
        def _():
            loss_ref[...] = jnp.zeros_like(loss_ref)

        u1 = ga_ref[...] * _sigmoid(gb_ref[...])
        halo = ha_ref[...] * _sigmoid(hb_ref[...])
        ext_ref[0:HALO, :] = jnp.where(pl.program_id(1) > 0, halo, 0.0)
        ext_ref[HALO:, :] = u1
        _fill_shifts(ext_ref, sh_ref)
        _conv_taps(w_ref, ext_ref, sh_ref, u2_ref, n_taps, tm, lambda j: HALO - (n_taps - 1) + j, cb_ref)
        uh, _ = _layernorm_stats(u2_ref[...])
        u3 = uh * lg_ref[...] + lb_ref[...]
        gc = gc_ref[...]
        yu = u3 * _sigmoid(u3) * (gc * _sigmoid(gc))
        y_ref[rows, FW:] = yu.astype(BF16)

        lane = _lane((tm, LANES))
        lo = lane < HEAD_DIM
        for p in range(H // 2):
            a_ref[:, p * LANES:(p + 1) * LANES] = jnp.where(
                lo, oa_ref[0, 2 * p], pltpu.roll(oa_ref[0, 2 * p + 1], HEAD_DIM, 1))
        gf = gf_ref[...]
        y_ref[rows, :FW] = (a_ref[...] * (gf * _sigmoid(gf))).astype(BF16)

        @pl.when(sub == mb - 1)
        def _():
            out = x_ref[...] + _dot(y_ref[...], wo_ref[...])
            diff = out - t_ref[...]
            loss_ref[...] = loss_ref[...] + jnp.sum(diff * diff)
            dout = diff * (1.0 / D)
            do_ref[...] = dout
            dob = dout.astype(BF16)
            dob_ref[...] = dob
            dy_ref[...] = _dot_nt(dob, wo_ref[...])

    row = lambda b, s: b * nsb + s
    zspec = lambda cb: pl.BlockSpec((tm, FW), lambda b, s: (row(b, s), cb))
    hspec = lambda cb: pl.BlockSpec((HALO, CW), lambda b, s: (jnp.maximum(row(b, s) * hb - 1, 0), cb))
    vspec = pl.BlockSpec((1, CW), lambda b, s: (0, 0))
    tspec = lambda w: pl.BlockSpec((tm, w), lambda b, s: (row(b, s), 0))
    mspec = lambda w: pl.BlockSpec((mt, w), lambda b, s: (row(b, s) // mb, 0))
    return pl.pallas_call(
        body, name="fwd_out", grid=(B, nsb),
        in_specs=[pl.BlockSpec((1, H, tm, LANES), lambda b, s: (b, 0, s, 0)),
                  zspec(3), zspec(4), zspec(5), zspec(6), hspec(4), hspec(5),
                  mspec(D), mspec(D),
                  pl.BlockSpec((HALO, CW), lambda b, s: (0, 0)), vspec, vspec, vspec,
                  pl.BlockSpec((FW + CW, D), lambda b, s: (0, 0))],
        out_specs=[mspec(FW + CW), tspec(CW), tspec(FW), mspec(D), mspec(D), mspec(FW + CW),
                   pl.BlockSpec((8, LANES), lambda b, s: (0, 0))],
        out_shape=[SDS((T, FW + CW), BF16), SDS((T, CW), F32), SDS((T, FW), F32), SDS((T, D), F32),
                   SDS((T, D), BF16), SDS((T, FW + CW), F32), SDS((8, LANES), F32)],
        scratch_shapes=[pltpu.VMEM((tm + HALO, CW), F32),
                        pltpu.VMEM((SUBLANES - 1, tm + HALO - SUBLANES, CW), F32)],
        compiler_params=_params(("arbitrary", "arbitrary")),
    )(oa, z, z, z, z, z, z, x2, tgt, conv_w, conv_b, ln_g, ln_b, w_out)


def _bwd_prep(dy, z, a_nat, oa, qa, u2, ln_g, ln_b, B, S, H, tm):
    T = B * S
    FW = H * HEAD_DIM
    CW = u2.shape[1]
    nsb = S // tm

    def body(dya_ref, dyu_ref, gf_ref, gc_ref, a_ref, oa_ref, qa_ref, u2_ref, lg_ref, lb_ref,
             dzgf_ref, dzgc_ref, du2_ref, doa_ref, qb_ref, sg_ref):
        first_step = (pl.program_id(0) == 0) & (pl.program_id(1) == 0)

        @pl.when(first_step)
        def _():
            sg_ref[...] = jnp.zeros_like(sg_ref)

        gf = gf_ref[...]
        sg = _sigmoid(gf)
        a = a_ref[...]
        dya = dya_ref[...]
        da = dya * (gf * sg)
        dzgf_ref[...] = (dya * a * (sg * (1.0 + gf * (1.0 - sg)))).astype(BF16)
        dd = da * a
        lane = _lane((tm, LANES))
        lo = lane < HEAD_DIM
        for p in range(H // 2):
            cols = slice(p * LANES, (p + 1) * LANES)
            da_p = da[:, cols]
            dd_p = dd[:, cols]
            d_heads = (jnp.sum(jnp.where(lo, dd_p, 0.0), axis=-1, keepdims=True),
                       jnp.sum(jnp.where(lo, 0.0, dd_p), axis=-1, keepdims=True))
            for e in range(2):
                da_e = da_p if e == 0 else pltpu.roll(da_p, HEAD_DIM, 1)
                d_e = d_heads[e]
                aug = _put3(jnp.zeros((tm, LANES), F32), lane, L_D, _split3(-d_e))
                doa_ref[0, 2 * p + e] = jnp.where(lo, da_e, aug).astype(BF16)
                lse = _lane_col(oa_ref[0, 2 * p + e], lane, L_ROWSUM)
                qb = _put3(qa_ref[0, 2 * p + e].astype(F32), lane, L_LSE, _split3(-lse))
                qb_ref[0, 2 * p + e] = qb.astype(BF16)

        gc = gc_ref[...]
        sc = _sigmoid(gc)
        dyu = dyu_ref[...]
        uh, rstd = _layernorm_stats(u2_ref[...])
        u3 = uh * lg_ref[...] + lb_ref[...]
        s3 = _sigmoid(u3)
        dzgc_ref[...] = (dyu * (u3 * s3) * (sc * (1.0 + gc * (1.0 - sc)))).astype(BF16)
        du3 = dyu * (gc * sc) * (s3 * (1.0 + u3 * (1.0 - s3)))
        sg_ref[0:1, :] = sg_ref[0:1, :] + jnp.sum(du3 * uh, axis=0, keepdims=True)
        sg_ref[1:2, :] = sg_ref[1:2, :] + jnp.sum(du3, axis=0, keepdims=True)
        duh = du3 * lg_ref[...]
        du2 = rstd * (duh - jnp.mean(duh, axis=-1, keepdims=True)
                      - uh * jnp.mean(duh * uh, axis=-1, keepdims=True))
        sg_ref[2:3, :] = sg_ref[2:3, :] + jnp.sum(du2, axis=0, keepdims=True)
        du2_ref[...] = du2

    row = lambda b, s: b * nsb + s
    tspec = lambda w, cb=0: pl.BlockSpec((tm, w), lambda b, s: (row(b, s), cb))
    hspec = pl.BlockSpec((1, H, tm, LANES), lambda b, s: (b, 0, s, 0))
    vspec = pl.BlockSpec((1, CW), lambda b, s: (0, 0))
    return pl.pallas_call(
        body, name="bwd_prep", grid=(B, nsb),
        in_specs=[tspec(FW, 0), tspec(CW, 1), tspec(FW, 3), tspec(CW, 6), tspec(FW), hspec, hspec,
                  tspec(CW), vspec, vspec],
        out_specs=[tspec(FW), tspec(CW), tspec(CW), hspec, hspec,
                   pl.BlockSpec((8, CW), lambda b, s: (0, 0))],
        out_shape=[SDS((T, FW), BF16), SDS((T, CW), BF16), SDS((T, CW), F32),
                   SDS((B, H, S, LANES), BF16), SDS((B, H, S, LANES), BF16), SDS((8, CW), F32)],
        compiler_params=_params(("arbitrary", "arbitrary")),
    )(dy, dy, z, z, a_nat, oa, qa, u2, ln_g, ln_b)


def _conv_bwd(du2, z, conv_w, B, S, n_taps, tm):
    T, CW = du2.shape
    nsb = S // tm
    hb = tm // HALO

    def body(d_ref, dh_ref, ga_ref, gb_ref, ha_ref, hb_ref, w_ref, dz_ref, dw_ref,
             extu_ref, extd_ref, shu_ref, shd_ref, du1_ref, dwacc_ref):
        s = pl.program_id(1)
        first_step = (pl.program_id(0) == 0) & (s == 0)
        last_step = (pl.program_id(0) == B - 1) & (s == nsb - 1)

        @pl.when(first_step)
        def _():
            dwacc_ref[...] = jnp.zeros_like(dwacc_ref)

        ga = ga_ref[...]
        sb = _sigmoid(gb_ref[...])
        halo = ha_ref[...] * _sigmoid(hb_ref[...])
        extu_ref[0:HALO, :] = jnp.where(s > 0, halo, 0.0)
        extu_ref[HALO:, :] = ga * sb
        extd_ref[0:tm, :] = d_ref[...]
        extd_ref[tm:, :] = jnp.where(s < nsb - 1, dh_ref[...], 0.0)
        _fill_shifts(extu_ref, shu_ref)
        _fill_shifts(extd_ref, shd_ref)
        _conv_taps(w_ref, extd_ref, shd_ref, du1_ref, n_taps, tm, lambda j: n_taps - 1 - j)
        for cc in range(CW // LANES):
            cols = slice(cc * LANES, (cc + 1) * LANES)
            parts = [None] * n_taps
            for r in range(tm // SUBLANES):
                dv = d_ref[r * SUBLANES:(r + 1) * SUBLANES, cols]
                for j in range(n_taps):
                    off = HALO - (n_taps - 1) + j + r * SUBLANES
                    term = dv * _tap_window(extu_ref, shu_ref, off, SUBLANES, cols)
                    parts[j] = term if parts[j] is None else parts[j] + term
            for j in range(n_taps):
                rows = slice(j * SUBLANES, (j + 1) * SUBLANES)
                dwacc_ref[rows, cols] = dwacc_ref[rows, cols] + parts[j]
        du1 = du1_ref[...]
        dz_ref[:, :CW] = (du1 * sb).astype(BF16)
        dz_ref[:, CW:] = (du1 * ga * (sb * (1.0 - sb))).astype(BF16)

        @pl.when(last_step)
        def _():
            dw_ref[...] = jnp.zeros_like(dw_ref)
            for j in range(n_taps):
                dw_ref[j:j + 1, :] = jnp.sum(dwacc_ref[j * SUBLANES:(j + 1) * SUBLANES, :], axis=0, keepdims=True)

    row = lambda b, s: b * nsb + s
    last_halo = T // HALO - 1
    return pl.pallas_call(
        body, name="conv_bwd", grid=(B, nsb),
        in_specs=[pl.BlockSpec((tm, CW), lambda b, s: (row(b, s), 0)),
                  pl.BlockSpec((HALO, CW), lambda b, s: (jnp.minimum((row(b, s) + 1) * hb, last_halo), 0)),
                  pl.BlockSpec((tm, CW), lambda b, s: (row(b, s), 4)),
                  pl.BlockSpec((tm, CW), lambda b, s: (row(b, s), 5)),
                  pl.BlockSpec((HALO, CW), lambda b, s: (jnp.maximum(row(b, s) * hb - 1, 0), 4)),
                  pl.BlockSpec((HALO, CW), lambda b, s: (jnp.maximum(row(b, s) * hb - 1, 0), 5)),
                  pl.BlockSpec((HALO, CW), lambda b, s: (0, 0))],
        out_specs=[pl.BlockSpec((tm, 2 * CW), lambda b, s: (row(b, s), 0)),
                   pl.BlockSpec((HALO, CW), lambda b, s: (0, 0))],
        out_shape=[SDS((T, 2 * CW), BF16), SDS((HALO, CW), F32)],
        scratch_shapes=[pltpu.VMEM((tm + HALO, CW), F32), pltpu.VMEM((tm + HALO, CW), F32),
                        pltpu.VMEM((SUBLANES - 1, tm + HALO - SUBLANES, CW), F32),
                        pltpu.VMEM((SUBLANES - 1, tm + HALO - SUBLANES, CW), F32),
                        pltpu.VMEM((tm, CW), F32), pltpu.VMEM((HALO * SUBLANES, CW), F32)],
        compiler_params=_params(("arbitrary", "arbitrary")),
    )(du2, du2, z, z, z, z, conv_w)


def _attn_bwd(qb, ka, va, doa, t, hb):
    B, H, S, _ = qb.shape
    nk = S // t

    def body(q_ref, k_ref, v_ref, do_ref, dq_ref, dk_ref, dv_ref, dv_acc):
        j = pl.program_id(2)

        @pl.when(j == 0)
        def _():
            dq_ref[...] = jnp.zeros_like(dq_ref)

        dk_ref[...] = jnp.zeros_like(dk_ref)
        dv_acc[...] = jnp.zeros_like(dv_acc)

        def step(i, masked):
            q_rows = pl.ds(pl.multiple_of(i * t, t), t)
            for e in range(hb):
                k = k_ref[0, e]
                q = q_ref[0, e, q_rows, :]
                do = do_ref[0, e, q_rows, :]
                p = jnp.exp(_dot_nt(q, k))
                if masked:
                    keep = lax.broadcasted_iota(jnp.int32, (t, t), 0) >= lax.broadcasted_iota(jnp.int32, (t, t), 1)
                    p = jnp.where(keep, p, 0.0)
                ds = (p * _dot_nt(do, v_ref[0, e])).astype(BF16)
                dv_acc[e] = dv_acc[e] + _dot_tn(p.astype(BF16), do)
                dk_ref[0, e] = dk_ref[0, e] + _dot_tn(ds, q)
                dq_ref[0, e, q_rows, :] = dq_ref[0, e, q_rows, :] + _dot(ds, k)

        step(j, True)

        def loop_body(i, carry):
            step(i, False)
            return carry

        lax.fori_loop(j + 1, nk, loop_body, 0)
        dv_ref[0] = dv_acc[...].astype(BF16)

    full = pl.BlockSpec((1, hb, S, LANES), lambda b, h, j: (b, h, 0, 0))
    blk = pl.BlockSpec((1, hb, t, LANES), lambda b, h, j: (b, h, j, 0))
    oshape = SDS((B, H, S, LANES), F32)
    return pl.pallas_call(
        body, name="attn_bwd", grid=(B, H // hb, nk),
        in_specs=[full, blk, blk, full],
        out_specs=[full, blk, blk],
        out_shape=[oshape, oshape, SDS((B, H, S, LANES), BF16)],
        scratch_shapes=[pltpu.VMEM((hb, t, LANES), F32)],
        compiler_params=_params(("parallel", "parallel", "arbitrary")),
    )(qb, ka, va, doa)


def _qk_bwd(dqa, dka, dva, z, gq, gk, B, S, H, tm):
    T = B * S
    FW = H * HEAD_DIM
    nsb = S // tm
    nfb = FW // LANES
    scale = HEAD_DIM ** -0.5

    def body(dq_ref, dk_ref, dv_ref, zq_ref, zk_ref, gq_ref, gk_ref, dzq_ref, dzk_ref, dzv_ref, dc_ref, dg_ref):
        p = pl.program_id(0)

        @pl.when(pl.program_id(1) == 0)
        def _():
            dg_ref[...] = jnp.zeros_like(dg_ref)

        lane = _lane((tm, LANES))
        lo = lane < HEAD_DIM

        def natural(ref):
            return jnp.where(lo, ref[0, 0].astype(F32), pltpu.roll(ref[0, 1].astype(F32), HEAD_DIM, 1))

        def norm_bwd(dn, x, g, row, out_ref):
            r = lax.rsqrt(_half_stats(x * x) * (1.0 / HEAD_DIM) + EPS)
            xh = x * r
            dg_ref[row:row + 1, :] = dg_ref[row:row + 1, :] + jnp.sum(dn * xh, axis=0, keepdims=True)
            dxh = dn * g
            mm = _half_stats(dxh * xh) * (1.0 / HEAD_DIM)
            out_ref[...] = (r * (dxh - xh * mm)).astype(BF16)

        norm_bwd(natural(dq_ref) * scale, zq_ref[...], gq_ref[...], 0, dzq_ref)
        norm_bwd(natural(dk_ref), zk_ref[...], gk_ref[...], 1, dzk_ref)
        dzv_ref[...] = natural(dv_ref).astype(BF16)

        dc = jnp.zeros((tm, LANES), F32)
        for e in range(2):
            val = _lane_col(dq_ref[0, e], lane, L_ROWSUM) - _lane_col(dk_ref[0, e], lane, L_KDECAY)
            dc = jnp.where(lane == 2 * p + e, val, dc)
        dc_ref[0] = dc

    hspec = pl.BlockSpec((1, 2, tm, LANES), lambda p, i: (i // nsb, p, i % nsb, 0))
    zspec = lambda off: pl.BlockSpec((tm, LANES), lambda p, i: (i, off + p))
    gspec = pl.BlockSpec((1, LANES), lambda p, i: (0, p))
    ospec = pl.BlockSpec((tm, LANES), lambda p, i: (i, p))
    return pl.pallas_call(
        body, name="qk_bwd", grid=(H // 2, T // tm),
        in_specs=[hspec, hspec, hspec, zspec(0), zspec(nfb), gspec, gspec],
        out_specs=[ospec, ospec, ospec,
                   pl.BlockSpec((1, tm, LANES), lambda p, i: (p, i, 0)),
                   pl.BlockSpec((8, LANES), lambda p, i: (0, p))],
        out_shape=[SDS((T, FW), BF16), SDS((T, FW), BF16), SDS((T, FW), BF16),
                   SDS((H // 2, T, LANES), F32), SDS((8, FW), F32)],
        compiler_params=_params(("parallel", "arbitrary")),
    )(dqa, dka, dva, z, z, gq, gk)


def _gate_bwd(dc8, z, b_pad, B, S, H, col_blk, fp, tc):
    T = B * S
    nsb = S // tc
    npair = dc8.shape[0]

    def body(dc_ref, zf_ref, b_ref, dz_ref, db_ref, carry):
        first_step = (pl.program_id(0) == 0) & (pl.program_id(1) == 0)

        @pl.when(first_step)
        def _():
            db_ref[...] = jnp.zeros_like(db_ref)

        @pl.when(pl.program_id(1) == 0)
        def _():
            carry[...] = jnp.zeros_like(carry)

        dc = dc_ref[0]
        for k in range(1, npair):
            dc = dc + dc_ref[k]
        dlf = _tri_cumsum(dc, True) + carry[...]
        carry[...] = carry[...] + jnp.sum(dc, axis=0, keepdims=True)
        x = zf_ref[...] + b_ref[...]
        dlogit = dlf * _sigmoid(-x)
        db_ref[0:1, :] = db_ref[0:1, :] + jnp.sum(dlogit, axis=0, keepdims=True)
        dz_ref[...] = jnp.zeros_like(dz_ref)
        dz_ref[:, :LANES] = dlogit.astype(BF16)

    rrow = lambda b, s: b * nsb + (nsb - 1 - s)
    return pl.pallas_call(
        body, name="gate_bwd", grid=(B, nsb),
        in_specs=[pl.BlockSpec((npair, tc, LANES), lambda b, s: (0, rrow(b, s), 0)),
                  pl.BlockSpec((tc, LANES), lambda b, s: (rrow(b, s), col_blk)),
                  pl.BlockSpec((1, LANES), lambda b, s: (0, 0))],
        out_specs=[pl.BlockSpec((tc, fp), lambda b, s: (rrow(b, s), 0)),
                   pl.BlockSpec((8, LANES), lambda b, s: (0, 0))],
        out_shape=[SDS((T, fp), BF16), SDS((8, LANES), F32)],
        scratch_shapes=[pltpu.VMEM((1, LANES), F32)],
        compiler_params=_params(("arbitrary", "arbitrary")),
    )(dc8, z, b_pad)


def _matmul_tn(a, b, name, tmm, tn, tk):
    T, M = a.shape
    N = b.shape[1]
    tmm, tn, tk = min(tmm, M), min(tn, N), min(tk, T)

    def body(a_ref, b_ref, o_ref):
        @pl.when(pl.program_id(2) == 0)
        def _():
            o_ref[...] = jnp.zeros_like(o_ref)

        o_ref[...] = o_ref[...] + _dot_tn(a_ref[...], b_ref[...])

    return pl.pallas_call(
        body, name=name, grid=(M // tmm, N // tn, T // tk),
        in_specs=[pl.BlockSpec((tk, tmm), lambda i, j, k: (k, i)),
                  pl.BlockSpec((tk, tn), lambda i, j, k: (k, j))],
        out_specs=pl.BlockSpec((tmm, tn), lambda i, j, k: (i, j)),
        out_shape=SDS((M, N), F32),
        compiler_params=_params(("parallel", "parallel", "arbitrary")),
    )(a, b)


def _dh_rms_bwd(pieces, w_t, x2, g, dout, tm, tk, parts):
    T, D = x2.shape
    nks = [p.shape[1] // tk for p in pieces]
    starts = [sum(nks[:k]) for k in range(len(pieces))]
    nk = sum(nks)
    ni = T // tm
    n = len(parts)

    def body(*refs):
        dz_refs = refs[:len(pieces)]
        w_ref, x_ref, g_ref, do_ref = refs[len(pieces):len(pieces) + 4]
        part_refs = refs[len(pieces) + 4:len(pieces) + 4 + n]
        gx_ref, dg_ref = refs[len(pieces) + 4 + n:len(pieces) + 6 + n]
        slot_refs = refs[len(pieces) + 6 + n:len(pieces) + 6 + 2 * n]
        acc_ref, send_sems, recv_sems = refs[len(pieces) + 6 + 2 * n:]
        k = pl.program_id(1)
        first_step = (pl.program_id(0) == 0) & (k == 0)
        last_step = (pl.program_id(0) == ni - 1) & (k == nk - 1)
        x, y, c = _place()
        chips = [(1 - x, y), (x, 1 - y), (1 - x, 1 - y)]

        def copy(a, f, to):
            cx, cy = chips[f]
            return pltpu.make_async_remote_copy(
                src_ref=part_refs[a].at[2 * cx + cy], dst_ref=slot_refs[a].at[f],
                send_sem=send_sems.at[a * 3 + f], recv_sem=recv_sems.at[a * 3 + f],
                device_id=to, device_id_type=MESH)

        @pl.when(first_step)
        def _():
            dg_ref[...] = jnp.zeros_like(dg_ref)
            for a in range(n):
                for f in range(3):
                    copy(a, f, (*chips[f], c)).start()

        @pl.when(last_step)
        def _():
            for a in range(n):
                for f in range(3):
                    copy(a, f, (x, y, c)).wait_recv()
            for a in range(n):
                for f in range(3):
                    copy(a, f, (*chips[f], c)).wait_send()

        @pl.when(k == 0)
        def _():
            acc_ref[...] = jnp.zeros_like(acc_ref)

        for dz_ref, st, cnt in zip(dz_refs, starts, nks):
            @pl.when((k >= st) & (k < st + cnt))
            def _(dz_ref=dz_ref):
                acc_ref[...] = acc_ref[...] + _dot(dz_ref[...], w_ref[...])

        @pl.when(k == nk - 1)
        def _():
            x = x_ref[...]
            r = lax.rsqrt(jnp.mean(x * x, axis=-1, keepdims=True) + EPS)
            xh = x * r
            dh = acc_ref[...]
            dg_ref[0:1, :] = dg_ref[0:1, :] + jnp.sum(dh * xh, axis=0, keepdims=True)
            dxn = dh * g_ref[...]
            gx_ref[...] = do_ref[...] + r * (dxn - xh * jnp.mean(dxn * xh, axis=-1, keepdims=True))

    def piece_spec(st, cnt):
        return pl.BlockSpec((tm, tk), lambda i, k: (i, jnp.clip(k - st, 0, cnt - 1)))

    tspec = pl.BlockSpec((tm, D), lambda i, k: (i, 0))
    return pl.pallas_call(
        body, name="dh_rms_bwd", grid=(T // tm, nk),
        in_specs=[piece_spec(st, cnt) for st, cnt in zip(starts, nks)]
        + [pl.BlockSpec((tk, D), lambda i, k: (k, 0)), tspec, pl.BlockSpec((1, D), lambda i, k: (0, 0)), tspec]
        + [ANY] * n,
        out_specs=[tspec, pl.BlockSpec((8, D), lambda i, k: (0, 0))] + [ANY] * n,
        out_shape=[SDS((T, D), F32), SDS((8, D), F32)] + [SDS((3,) + p.shape[1:], p.dtype) for p in parts],
        scratch_shapes=[pltpu.VMEM((tm, D), F32),
                        pltpu.SemaphoreType.DMA((3 * n,)), pltpu.SemaphoreType.DMA((3 * n,))],
        compiler_params=_params(("arbitrary", "arbitrary")),
    )(*pieces, w_t, x2, g, dout, *parts)


def _block_plan(R, C, tr, tc):
    br = min(tr, R)
    if R % br == 0:
        return (br, C), R // br, lambda i: (i, 0)
    bc = min(tc, C)
    assert C % bc == 0
    return (R, bc), C // bc, lambda i: (0, i)


def _ew_call(body, name, ins, n_out, out_dtypes, tr, tc):
    R, C = ins[0].shape
    blk, steps, imap = _block_plan(R, C, tr, tc)
    spec = pl.BlockSpec(blk, imap)
    return pl.pallas_call(
        body, name=name, grid=(steps,),
        in_specs=[spec] * len(ins), out_specs=[spec] * n_out,
        out_shape=[SDS((R, C), dt) for dt in out_dtypes],
        compiler_params=_params(("parallel",)),
    )(*ins)


def _sum_slots(slots, name, first=None, tr=256):
    n, R, C = slots.shape
    blk, steps, imap = _block_plan(R, C, tr, 2 * LANES)
    lead = [] if first is None else [first]

    def body(*refs):
        s_ref, o_ref = refs[-2:]
        acc = refs[0][...].astype(F32) if lead else s_ref[0].astype(F32)
        for k in range(0 if lead else 1, n):
            acc = acc + s_ref[k].astype(F32)
        o_ref[...] = acc

    return pl.pallas_call(
        body, name=name, grid=(steps,),
        in_specs=[pl.BlockSpec(blk, imap)] * len(lead) + [pl.BlockSpec((n,) + blk, lambda i: (0,) + imap(i))],
        out_specs=pl.BlockSpec(blk, imap),
        out_shape=SDS((R, C), F32),
        compiler_params=_params(("parallel",)),
    )(*lead, slots)


def _adamw_update(w, g, m, v):
    nm = ADAM_B1 * m + (1.0 - ADAM_B1) * g
    nv = ADAM_B2 * v + (1.0 - ADAM_B2) * (g * g)
    m_hat = nm / (1.0 - ADAM_B1 ** ADAM_STEP)
    v_hat = nv / (1.0 - ADAM_B2 ** ADAM_STEP)
    return -ADAM_LR * (m_hat / (jnp.sqrt(v_hat) + ADAM_EPS) + ADAM_WD * w), nm, nv


def _adamw(w, g, m, v, name):
    def body(w_ref, g_ref, m_ref, v_ref, d_ref, nm_ref, nv_ref):
        d_ref[...], nm_ref[...], nv_ref[...] = _adamw_update(w_ref[...], g_ref[...], m_ref[...], v_ref[...])

    return _ew_call(body, name, [w, g, m, v], 3, [F32, F32, F32], 128, 2 * LANES)


def _adamw_halves(w, mine, other, m, v, name):
    R, C = w.shape
    half = C // 2
    bc = min(2 * LANES, half)
    per = half // bc

    def body(w_ref, a_ref, b_ref, m_ref, v_ref, g_ref, d_ref, nm_ref, nv_ref):
        g = jnp.where(pl.program_id(0) // per == lax.axis_index("c"), a_ref[...], b_ref[...])
        g_ref[...] = g
        d_ref[...], nm_ref[...], nv_ref[...] = _adamw_update(w_ref[...], g, m_ref[...], v_ref[...])

    full = pl.BlockSpec((R, bc), lambda i: (0, i))
    part = pl.BlockSpec((R, bc), lambda i: (0, i % per))
    return pl.pallas_call(
        body, name=name, grid=(C // bc,),
        in_specs=[full, part, part, full, full], out_specs=[full] * 4,
        out_shape=[SDS((R, C), F32)] * 4,
        compiler_params=_params(("parallel",)),
    )(w, mine, other, m, v)


ANY = pl.BlockSpec(memory_space=pl.ANY)


def _place():
    return lax.axis_index("x"), lax.axis_index("y"), lax.axis_index("c")


def _gather_chips(shards, splits):
    n = len(shards)
    per = 7

    def body(*refs):
        ins, outs = refs[:n], refs[n:2 * n]
        send_sems, recv_sems = refs[2 * n:]
        x, y, c = _place()
        mine = 2 * x + y
        chips = [(1 - x, y), (x, 1 - y), (1 - x, 1 - y)]

        def rows(a, half):
            return pl.ds(0, splits[a]) if half == 0 else pl.ds(splits[a], ins[a].shape[0] - splits[a])

        def copy(a, k, chip_idx, half, to, src=None):
            dst = outs[a].at[chip_idx, rows(a, half)]
            return pltpu.make_async_remote_copy(
                src_ref=dst if src is None else src, dst_ref=dst,
                send_sem=send_sems.at[a * per + k], recv_sem=recv_sems.at[a * per + k],
                device_id=to, device_id_type=MESH)

        def own(a, to):
            return pltpu.make_async_remote_copy(
                src_ref=ins[a], dst_ref=outs[a].at[mine],
                send_sem=send_sems.at[a * per + 6], recv_sem=recv_sems.at[a * per + 6],
                device_id=to, device_id_type=MESH)

        for cc in (0, 1):
            @pl.when(c == cc)
            def _(cc=cc):
                me, sibling = (x, y, cc), (x, y, 1 - cc)
                first = [copy(a, k, mine, cc, (*chip, cc), src=ins[a].at[rows(a, cc)])
                         for a in range(n) for k, chip in enumerate(chips)]
                first += [own(a, sibling) for a in range(n)]
                for cp in first:
                    cp.start()
                passed = []
                for k, (cx, cy) in enumerate(chips):
                    for a in range(n):
                        copy(a, k, 2 * cx + cy, cc, me).wait_recv()
                        fwd = copy(a, 3 + k, 2 * cx + cy, cc, sibling)
                        fwd.start()
                        passed.append(fwd)
                for k, (cx, cy) in enumerate(chips):
                    for a in range(n):
                        copy(a, 3 + k, 2 * cx + cy, 1 - cc, me).wait_recv()
                for a in range(n):
                    own(a, me).wait_recv()
                for cp in first + passed:
                    cp.wait_send()

    return pl.pallas_call(
        body, name="gather_chips",
        in_specs=[ANY] * n, out_specs=[ANY] * n,
        out_shape=[SDS((4,) + s.shape, s.dtype) for s in shards],
        scratch_shapes=[pltpu.SemaphoreType.DMA((per * n,)), pltpu.SemaphoreType.DMA((per * n,))],
    )(*shards)


def _pair_swap(arrs):
    n = len(arrs)

    def body(*refs):
        ins, outs = refs[:n], refs[n:2 * n]
        send_sems, recv_sems = refs[2 * n:]
        x, y, c = _place()
        for cc in (0, 1):
            @pl.when(c == cc)
            def _(cc=cc):
                copies = []
                for a in range(n):
                    half = ins[a].shape[1] // 2
                    copies.append(pltpu.make_async_remote_copy(
                        src_ref=ins[a].at[:, pl.ds((1 - cc) * half, half)], dst_ref=outs[a],
                        send_sem=send_sems.at[a], recv_sem=recv_sems.at[a],
                        device_id=(x, y, 1 - cc), device_id_type=MESH))
                for cp in copies:
                    cp.start()
                for cp in copies:
                    cp.wait()

    return pl.pallas_call(
        body, name="pair_swap",
        in_specs=[ANY] * n, out_specs=[ANY] * n,
        out_shape=[SDS((h.shape[0], h.shape[1] // 2), h.dtype) for h in arrs],
        scratch_shapes=[pltpu.SemaphoreType.DMA((n,)), pltpu.SemaphoreType.DMA((n,))],
    )(*arrs)


def _pair_sum(arrs, got, core, tr):
    half = arrs[0].shape[1] // 2
    cnts = [p.shape[0] // tr for p in arrs]
    starts = [sum(cnts[:k]) for k in range(len(arrs))]

    def body(core_ref, *refs):
        del core_ref
        own_refs, got_refs, o_ref = refs[:len(arrs)], refs[len(arrs):2 * len(arrs)], refs[-1]
        s = pl.program_id(0)
        for own_ref, got_ref, st, cnt in zip(own_refs, got_refs, starts, cnts):
            @pl.when((s >= st) & (s < st + cnt))
            def _(own_ref=own_ref, got_ref=got_ref):
                o_ref[...] = (own_ref[...] + got_ref[...]).astype(BF16)

    def own_spec(st, cnt):
        return pl.BlockSpec((tr, half), lambda s, core_ref: (jnp.clip(s - st, 0, cnt - 1), core_ref[0]))

    def got_spec(st, cnt):
        return pl.BlockSpec((tr, half), lambda s, core_ref: (jnp.clip(s - st, 0, cnt - 1), 0))

    return pl.pallas_call(
        body, name="pair_sum",
        grid_spec=pltpu.PrefetchScalarGridSpec(
            num_scalar_prefetch=1, grid=(sum(cnts),),
            in_specs=[own_spec(st, cnt) for st, cnt in zip(starts, cnts)]
            + [got_spec(st, cnt) for st, cnt in zip(starts, cnts)],
            out_specs=pl.BlockSpec((tr, half), lambda s, core_ref: (s, 0))),
        out_shape=SDS((sum(cnts) * tr, half), BF16),
        compiler_params=_params(("arbitrary",)),
    )(core, *arrs, *got)


def _share_results(arrs, rows):
    n = len(arrs)
    flips = [(fx, fy, fc) for fx in (0, 1) for fy in (0, 1) for fc in (0, 1)][1:]

    def body(*refs):
        ins, rows_ref, outs, all_ref = refs[:n], refs[n], refs[n + 1:2 * n + 1], refs[2 * n + 1]
        send_sems, recv_sems, local_sem = refs[2 * n + 2:]
        x, y, c = _place()
        me = 4 * x + 2 * y + c
        local = pltpu.make_async_copy(rows_ref, all_ref.at[me], local_sem)
        local.start()
        copies = [pltpu.make_async_remote_copy(
            src_ref=ins[a], dst_ref=outs[a], send_sem=send_sems.at[a], recv_sem=recv_sems.at[a],
            device_id=(x, y, 1 - c), device_id_type=MESH) for a in range(n)]
        for k, (fx, fy, fc) in enumerate(flips):
            copies.append(pltpu.make_async_remote_copy(
                src_ref=rows_ref, dst_ref=all_ref.at[me], send_sem=send_sems.at[n + k], recv_sem=recv_sems.at[n + k],
                device_id=(x ^ fx, y ^ fy, c ^ fc), device_id_type=MESH))
        for cp in copies:
            cp.start()
        for cp in copies[:n]:
            cp.wait_recv()
        for k, (fx, fy, fc) in enumerate(flips):
            src = 4 * (x ^ fx) + 2 * (y ^ fy) + (c ^ fc)
            pltpu.make_async_remote_copy(
                src_ref=rows_ref, dst_ref=all_ref.at[src], send_sem=send_sems.at[n + k], recv_sem=recv_sems.at[n + k],
                device_id=(x, y, c), device_id_type=MESH).wait_recv()
        for cp in copies:
            cp.wait_send()
        local.wait()

    outs = pl.pallas_call(
        body, name="share_results",
        in_specs=[ANY] * (n + 1), out_specs=[ANY] * (n + 1),
        out_shape=[SDS(h.shape, h.dtype) for h in arrs] + [SDS((8,) + rows.shape, rows.dtype)],
        scratch_shapes=[pltpu.SemaphoreType.DMA((n + 7,)), pltpu.SemaphoreType.DMA((n + 7,)),
                        pltpu.SemaphoreType.DMA],
    )(*arrs, rows)
    return outs[:n], outs[n]


def _tiles(S, FW):
    big = FW % 512 == 0
    return dict(
        fp=512 if big else LANES,
        tn=1536 if big else LANES,
        tm_in=min(1024, S),
        t_attn=min(512, S),
        hb_fwd=8,
        hb_bwd=8,
        tm_prep=min(512, S),
        tm_mix=min(128, S),
        tc=min(256, S),
        tk=512 if big else LANES,
    )


def kernel(x, norm_g, w_in, b_forget, q_norm_g, k_norm_g, conv_w, conv_b, conv_ln_g, conv_ln_b, w_out, loss_target, m_norm_g, m_w_in, m_b_forget, m_q_norm_g, m_k_norm_g, m_conv_w, m_conv_b, m_conv_ln_g, m_conv_ln_b, m_w_out, v_norm_g, v_w_in, v_b_forget, v_q_norm_g, v_k_norm_g, v_conv_w, v_conv_b, v_conv_ln_g, v_conv_ln_b, v_w_out):
    B, S, D = x.shape
    H, dh = q_norm_g.shape[1:]
    FW = H * dh
    CW = conv_b.shape[-1]
    n_taps, cw_shard = conv_w.shape[1:]
    in_shard = w_in.shape[2]
    out_shard = w_out.shape[1]
    assert dh == HEAD_DIM and H % 2 == 0 and H <= LANES and FW == CW == D
    assert n_taps - 1 <= HALO and 4 * cw_shard == CW and 4 * out_shard == FW + CW
    assert 4 * in_shard == 4 * FW + 3 * CW + H
    T = B * S
    tl = _tiles(S, FW)
    fp = tl["fp"]
    xi, yi, ci = _place()

    w_t = jnp.transpose(w_in[0])
    conv_pad = jnp.pad(conv_w[0], ((0, HALO - n_taps), (0, 0)))
    bf16_rows = 2 * SUBLANES
    g_in, g_out, g_cw = _gather_chips(
        [w_t.astype(BF16), w_out[0].astype(BF16), conv_pad],
        [in_shard // 2 // bf16_rows * bf16_rows, out_shard // 2, HALO // 2])
    w_t_full = g_in.reshape(4 * in_shard, D)
    w_out_full = g_out.reshape(FW + CW, D)
    conv_full = g_cw.transpose(1, 0, 2).reshape(HALO, CW)
    o_f = 3 * FW
    w_pack = jnp.concatenate([w_t_full[:o_f], w_t_full[o_f + H:],
                              jnp.pad(w_t_full[o_f:o_f + H], ((0, fp - H), (0, 0)))], axis=0)
    f_col = 4 * FW + 3 * CW

    x2 = x.reshape(T, D)
    tgt = loss_target.reshape(T, D)
    b_pad = jnp.pad(b_forget, ((0, 0), (0, LANES - H)))
    gq = q_norm_g.reshape(1, FW)
    gk = k_norm_g.reshape(1, FW)

    z, h = _fwd_in(x2, norm_g, w_pack, tl["tm_in"], tl["tn"])
    c = _gate_fwd(z, b_pad, B, S, H, f_col // LANES, tl["tc"])
    qa, ka, va = _attn_prep(z, c, gq, gk, B, S, H, tl["tm_prep"])
    gain = lambda g: jnp.max(jnp.abs(g[0]), axis=-1)
    bound = (NORM_SLACK ** 2 * dh ** 0.5) * gain(q_norm_g) * gain(k_norm_g)
    oa = _attn_fwd(qa, ka, va, jnp.broadcast_to(bound[:, None, None], (H, 1, LANES)), tl["t_attn"], tl["hb_fwd"])
    y, u2, a_nat, dout, dout_b, dy, loss_acc = _fwd_out(
        oa, z, x2, tgt, conv_full, conv_b, conv_ln_g, conv_ln_b, w_out_full, B, S, H, n_taps, tl["tm_mix"])

    dzgf, dzgc, du2, doa, qb, sg_conv = _bwd_prep(dy, z, a_nat, oa, qa, u2, conv_ln_g, conv_ln_b, B, S, H, tl["tm_mix"])
    dzglu, dconv_w = _conv_bwd(du2, z, conv_full, B, S, n_taps, tl["tm_mix"])
    dqa, dka, dva = _attn_bwd(qb, ka, va, doa, tl["t_attn"], tl["hb_bwd"])
    dzq, dzk, dzv, dc8, dg_qk = _qk_bwd(dqa, dka, dva, z, gq, gk, B, S, H, tl["tm_prep"])
    dzf, db_f = _gate_bwd(dc8, z, b_pad, B, S, H, f_col // LANES, fp, tl["tc"])
    pieces = [dzq, dzk, dzv, dzgf, dzglu, dzgc, dzf]
    dw_all = [_matmul_tn(p, h, f"dw_in_{k}", 1024, 1024, 1024) for k, p in enumerate(pieces)]
    dw_all.append(_matmul_tn(y, dout_b, "dw_out", 1024, 1024, 1024))

    summed = _pair_sum(dw_all, _pair_swap(dw_all), ci.astype(jnp.int32).reshape(1), fp)
    ends = [0]
    for t in dw_all:
        ends.append(ends[-1] + t.shape[0])
    spans, at = [], 0
    for k, rows in [(0, FW), (1, FW), (2, FW), (6, H), (3, FW), (4, 2 * CW), (5, CW)]:
        spans.append((at, rows, ends[k]))
        at += rows

    def chip_rows(j):
        lo, hi = j * in_shard, (j + 1) * in_shard
        return jnp.concatenate([summed[src + max(lo, a) - a:src + min(hi, a + n) - a]
                                for a, n, src in spans if max(lo, a) < min(hi, a + n)], axis=0)

    part_in = jnp.stack([chip_rows(j) for j in range(4)])
    part_out = summed[ends[7]:ends[8]].reshape(4, out_shard, D // 2)
    grad_x2, dg_norm, slots_in, slots_out = _dh_rms_bwd(
        pieces, w_pack, x2, norm_g, dout, tl["tm_in"], tl["tk"], [part_in, part_out])
    chip = 2 * xi + yi
    half_in = _sum_slots(slots_in, "chip_sum_in", lax.dynamic_index_in_dim(part_in, chip, 0, keepdims=False))
    half_out = _sum_slots(slots_out, "chip_sum_out", lax.dynamic_index_in_dim(part_out, chip, 0, keepdims=False))
    lanes_to_d = lambda t: jnp.pad(t, ((0, 0), (0, D - LANES)))
    small = jnp.concatenate([
        dg_norm[0:1], lanes_to_d(db_f[0:1, :]), dg_qk[0:1], dg_qk[1:2],
        sg_conv[2:3], sg_conv[0:1], sg_conv[1:2], dconv_w, lanes_to_d(loss_acc[0:1, :])], axis=0)
    n_small = small.shape[0]
    (other_in, other_out), all_small = _share_results([half_in, half_out], small)

    small_sum = _sum_slots(all_small, "small_sum", tr=n_small)
    loss = 0.5 * small_sum[n_small - 1, 0] / D
    grad_norm_g, grad_b_f = small_sum[0:1], small_sum[1:2, :H]
    grad_gq, grad_gk = small_sum[2:3].reshape(1, H, dh), small_sum[3:4].reshape(1, H, dh)
    grad_conv_b, grad_ln_g, grad_ln_b = small_sum[4:5], small_sum[5:6], small_sum[6:7]
    grad_conv_w = lax.dynamic_slice_in_dim(small_sum[7:7 + n_taps], chip * cw_shard, cw_shard, axis=1)

    in_t = _adamw_halves(w_t, half_in, other_in, jnp.transpose(m_w_in[0]), jnp.transpose(v_w_in[0]), "adamw_in")
    grad_w_in, d_in, nm_in, nv_in = (jnp.transpose(t)[None] for t in in_t)
    grad_w_out, d_out, nm_out, nv_out = (
        t[None] for t in _adamw_halves(w_out[0], half_out, other_out, m_w_out[0], v_w_out[0], "adamw_out"))
    d_cw, nm_cw, nv_cw = (t[None] for t in _adamw(conv_w[0], grad_conv_w, m_conv_w[0], v_conv_w[0], "adamw_conv_w"))

    def rows(ws):
        return jnp.concatenate([jnp.pad(t.reshape(1, -1), ((0, 0), (0, D - t.size))) for t in ws], axis=0)

    small_w = [norm_g, b_forget, q_norm_g, k_norm_g, conv_b, conv_ln_g, conv_ln_b]
    small_m = [m_norm_g, m_b_forget, m_q_norm_g, m_k_norm_g, m_conv_b, m_conv_ln_g, m_conv_ln_b]
    small_v = [v_norm_g, v_b_forget, v_q_norm_g, v_k_norm_g, v_conv_b, v_conv_ln_g, v_conv_ln_b]
    d_s, nm_s, nv_s = _adamw(rows(small_w), small_sum[0:7], rows(small_m), rows(small_v), "adamw_small")

    def unpack(t):
        return [t[k:k + 1, :w.size].reshape(w.shape) for k, w in enumerate(small_w)]

    def order(s, in_, cw, out_):
        ng, bf, qg, kg, cb, lg, lb = s
        return [ng, in_, bf, qg, kg, cw, cb, lg, lb, out_]

    grads = [grad_norm_g, grad_w_in, grad_b_f, grad_gq, grad_gk, grad_conv_w[None],
             grad_conv_b, grad_ln_g, grad_ln_b, grad_w_out]
    return (loss, grad_x2.reshape(B, S, D), *grads,
            *order(unpack(d_s), d_in, d_cw, d_out),
            *order(unpack(nm_s), nm_in, nm_cw, nm_out),
            *order(unpack(nv_s), nv_in, nv_cw, nv_out))
```

```python
import jax
import jax.numpy as jnp
from jax import lax
from jax.experimental import pallas as pl
from jax.experimental.pallas import tpu as pltpu

F32 = jnp.float32
BF16 = jnp.bfloat16
SDS = jax.ShapeDtypeStruct
MESH = pl.DeviceIdType.MESH

EPS = 1e-6
NEG_INF = -1e30
LANES = 128
SUBLANES = 8
HEAD_DIM = 64
HALO = 32
VMEM_LIMIT = 56 * 1024 * 1024

L_ROWSUM = 64
L_KDECAY = 67
L_LSE = 70
L_D = 65
NORM_SLACK = 1.02
SHIFT_MAX = 40.0

ADAM_LR = 0.001
ADAM_B1 = 0.9
ADAM_B2 = 0.999
ADAM_EPS = 1e-08
ADAM_WD = 0.01
ADAM_STEP = 10


def _params(sem, vmem=VMEM_LIMIT):
    return pltpu.CompilerParams(dimension_semantics=sem, vmem_limit_bytes=vmem)


def _sigmoid(x):
    return 1.0 / (1.0 + jnp.exp(-x))


def _split3(x):
    hi = x.astype(BF16).astype(F32)
    r = x - hi
    mid = r.astype(BF16).astype(F32)
    lo = (r - mid).astype(BF16).astype(F32)
    return hi, mid, lo


def _dot(a, b):
    return jnp.dot(a, b, preferred_element_type=F32)


def _dot_nt(a, b):
    return lax.dot_general(a, b, (((1,), (1,)), ((), ())), preferred_element_type=F32)


def _dot_tn(a, b):
    return lax.dot_general(a, b, (((0,), (0,)), ((), ())), preferred_element_type=F32)


def _lane(shape):
    return lax.broadcasted_iota(jnp.int32, shape, 1)


def _lane_col(x, lane, idx):
    return jnp.sum(jnp.where(lane == idx, x, 0.0), axis=-1, keepdims=True)


def _put3(base, lane, start, pieces):
    out = base
    for k, p in enumerate(pieces):
        out = jnp.where(lane == start + k, p, out)
    return out


def _half_stats(t):
    hi = t.astype(BF16)
    mid = (t - hi.astype(F32)).astype(BF16)
    row = lax.broadcasted_iota(jnp.int32, (2 * LANES, LANES), 0)
    col = lax.broadcasted_iota(jnp.int32, (2 * LANES, LANES), 1)
    same_half = (jnp.bitwise_and(row, LANES - 1) < HEAD_DIM) == (col < HEAD_DIM)
    return _dot(jnp.concatenate([hi, mid], axis=1), jnp.where(same_half, 1.0, 0.0).astype(BF16))


def _fwd_in(x2, g, w_t, tm, tn):
    T, D = x2.shape
    N = w_t.shape[0]

    def body(x_ref, g_ref, w_ref, z_ref, h_ref):
        @pl.when(pl.program_id(1) == 0)
        def _():
            x = x_ref[...]
            r = lax.rsqrt(jnp.mean(x * x, axis=-1, keepdims=True) + EPS)
            h_ref[...] = (x * r * g_ref[...]).astype(BF16)

        z_ref[...] = _dot_nt(h_ref[...], w_ref[...])

    return pl.pallas_call(
        body, name="fwd_in", grid=(T // tm, N // tn),
        in_specs=[pl.BlockSpec((tm, D), lambda i, j: (i, 0)),
                  pl.BlockSpec((1, D), lambda i, j: (0, 0)),
                  pl.BlockSpec((tn, D), lambda i, j: (j, 0))],
        out_specs=[pl.BlockSpec((tm, tn), lambda i, j: (i, j)),
                   pl.BlockSpec((tm, D), lambda i, j: (i, 0))],
        out_shape=[SDS((T, N), F32), SDS((T, D), BF16)],
        compiler_params=_params(("parallel", "arbitrary")),
    )(x2, g, w_t)


def _tri_cumsum(x, reverse):
    t = x.shape[0]
    row = lax.broadcasted_iota(jnp.int32, (t, t), 0)
    col = lax.broadcasted_iota(jnp.int32, (t, t), 1)
    tri = (row <= col) if reverse else (row >= col)
    tri = jnp.where(tri, 1.0, 0.0).astype(BF16)
    hi, mid, lo = _split3(x)
    return _dot(tri, hi.astype(BF16)) + _dot(tri, mid.astype(BF16)) + _dot(tri, lo.astype(BF16))


def _gate_fwd(z, b_pad, B, S, H, col_blk, tc):
    T = B * S
    nsb = S // tc

    def body(zf_ref, b_ref, c_ref, carry):
        @pl.when(pl.program_id(1) == 0)
        def _():
            carry[...] = jnp.zeros_like(carry)

        x = zf_ref[...] + b_ref[...]
        lf = jnp.minimum(x, 0.0) - jnp.log(1.0 + jnp.exp(-jnp.abs(x)))
        lf = jnp.where(_lane(lf.shape) < H, lf, 0.0)
        c_ref[...] = _tri_cumsum(lf, False) + carry[...]
        carry[...] = carry[...] + jnp.sum(lf, axis=0, keepdims=True)

    return pl.pallas_call(
        body, name="gate_fwd", grid=(B, nsb),
        in_specs=[pl.BlockSpec((tc, LANES), lambda b, s: (b * nsb + s, col_blk)),
                  pl.BlockSpec((1, LANES), lambda b, s: (0, 0))],
        out_specs=pl.BlockSpec((tc, LANES), lambda b, s: (b * nsb + s, 0)),
        out_shape=SDS((T, LANES), F32),
        scratch_shapes=[pltpu.VMEM((1, LANES), F32)],
        compiler_params=_params(("parallel", "arbitrary")),
    )(z, b_pad)


def _qk_normalize(x, g):
    r = lax.rsqrt(_half_stats(x * x) * (1.0 / HEAD_DIM) + EPS)
    return x * r * g


def _attn_prep(z, c, gq, gk, B, S, H, tm):
    T = B * S
    FW = H * HEAD_DIM
    nsb = S // tm
    nfb = FW // LANES
    scale = HEAD_DIM ** -0.5

    def body(zq_ref, zk_ref, zv_ref, c_ref, gq_ref, gk_ref, qa_ref, ka_ref, va_ref):
        p = pl.program_id(1)
        lane = _lane((tm, LANES))
        lo = lane < HEAD_DIM
        qn = _qk_normalize(zq_ref[...], gq_ref[...]) * scale
        kn = _qk_normalize(zk_ref[...], gk_ref[...])
        v = zv_ref[...]
        cc = c_ref[...]
        ones_q = ((lane >= L_KDECAY) & (lane < L_KDECAY + 3)).astype(F32)
        ones_k = (((lane >= L_ROWSUM) & (lane < L_ROWSUM + 3)) | ((lane >= L_LSE) & (lane < L_LSE + 3))).astype(F32)
        ones_v = ((lane >= L_ROWSUM) & (lane < L_D + 3)).astype(F32)
        for e in range(2):
            if e == 0:
                qe, ke, ve = qn, kn, v
            else:
                qe, ke, ve = (pltpu.roll(t, HEAD_DIM, 1) for t in (qn, kn, v))
            ch = _lane_col(cc, lane, 2 * p + e)
            pieces = _split3(ch)
            qa = jnp.where(lo, qe, _put3(ones_q, lane, L_ROWSUM, pieces))
            ka = jnp.where(lo, ke, _put3(ones_k, lane, L_KDECAY, [-t for t in pieces]))
            va = jnp.where(lo, ve, ones_v)
            qa_ref[0, e] = qa.astype(BF16)
            ka_ref[0, e] = ka.astype(BF16)
            va_ref[0, e] = va.astype(BF16)

    zspec = lambda off: pl.BlockSpec((tm, LANES), lambda i, p: (i, off + p))
    gspec = pl.BlockSpec((1, LANES), lambda i, p: (0, p))
    ospec = pl.BlockSpec((1, 2, tm, LANES), lambda i, p: (i // nsb, p, i % nsb, 0))
    oshape = SDS((B, H, S, LANES), BF16)
    return pl.pallas_call(
        body, name="attn_prep", grid=(T // tm, H // 2),
        in_specs=[zspec(0), zspec(nfb), zspec(2 * nfb),
                  pl.BlockSpec((tm, LANES), lambda i, p: (i, 0)), gspec, gspec],
        out_specs=[ospec, ospec, ospec],
        out_shape=[oshape, oshape, oshape],
        compiler_params=_params(("parallel", "arbitrary")),
    )(z, z, z, c, gq, gk)


def _attn_fwd(qa, ka, va, bound, t, hb):
    B, H, S, _ = qa.shape
    nq = S // t

    def body(q_ref, k_ref, v_ref, b_ref, o_ref, m_ref, acc_ref, qs_ref):
        i = pl.program_id(2)
        lane = _lane((t, LANES))

        shifts = [b_ref[e] for e in range(hb)]
        worst = shifts[0]
        for e in range(1, hb):
            worst = jnp.maximum(worst, shifts[e])
        bounded = jnp.max(worst) <= SHIFT_MAX
        acc_ref[...] = jnp.zeros_like(acc_ref)

        def tiles(step):
            def loop_body(j, carry):
                step(j, False)
                return carry

            lax.fori_loop(0, i, loop_body, 0)
            step(i, True)

        def keep_mask(n=t):
            return lax.broadcasted_iota(jnp.int32, (n, n), 0) >= lax.broadcasted_iota(jnp.int32, (n, n), 1)

        def finish(e, shift):
            acc = acc_ref[e]
            l = _lane_col(acc, lane, L_ROWSUM)
            o_ref[0, e] = jnp.where(lane < HEAD_DIM, acc / l, shift + jnp.log(l))

        @pl.when(bounded)
        def _():
            for e in range(hb):
                qs_ref[e] = _put3(q_ref[0, e].astype(F32), lane, L_LSE, _split3(-shifts[e])).astype(BF16)

            def pair(e, q_rows, k_start, n, masked):
                k_rows = pl.ds(pl.multiple_of(k_start, n), n)
                p = jnp.exp(_dot_nt(qs_ref[e, q_rows, :], k_ref[0, e, k_rows, :]))
                if masked:
                    p = jnp.where(keep_mask(n), p, 0.0)
                acc_ref[e, q_rows, :] = acc_ref[e, q_rows, :] + _dot(p.astype(BF16), v_ref[0, e, k_rows, :])

            def step(j, masked):
                for e in range(hb):
                    if masked:
                        h = t // 2
                        pair(e, slice(0, h), j * t, h, True)
                        pair(e, slice(h, t), j * t, h, False)
                        pair(e, slice(h, t), j * t + h, h, True)
                    else:
                        pair(e, slice(0, t), j * t, t, False)

            tiles(step)
            for e in range(hb):
                finish(e, shifts[e])

        @pl.when(jnp.logical_not(bounded))
        def _():
            m_ref[...] = jnp.full_like(m_ref, NEG_INF)

            def step(j, masked):
                rows = pl.ds(pl.multiple_of(j * t, t), t)
                for e in range(hb):
                    s = _dot_nt(q_ref[0, e], k_ref[0, e, rows, :])
                    if masked:
                        s = jnp.where(keep_mask(), s, NEG_INF)
                    m_prev = m_ref[e]
                    m_new = jnp.maximum(m_prev, jnp.max(s, axis=-1, keepdims=True))
                    alpha = jnp.exp(m_prev - m_new)
                    p = jnp.exp(s - m_new).astype(BF16)
                    acc_ref[e] = alpha * acc_ref[e] + _dot(p, v_ref[0, e, rows, :])
                    m_ref[e] = m_new

            tiles(step)
            for e in range(hb):
                finish(e, m_ref[e])

    return pl.pallas_call(
        body, name="attn_fwd", grid=(B, H // hb, nq),
        in_specs=[pl.BlockSpec((1, hb, t, LANES), lambda b, h, i: (b, h, i, 0)),
                  pl.BlockSpec((1, hb, S, LANES), lambda b, h, i: (b, h, 0, 0)),
                  pl.BlockSpec((1, hb, S, LANES), lambda b, h, i: (b, h, 0, 0)),
                  pl.BlockSpec((hb, 1, LANES), lambda b, h, i: (h, 0, 0))],
        out_specs=pl.BlockSpec((1, hb, t, LANES), lambda b, h, i: (b, h, i, 0)),
        out_shape=SDS((B, H, S, LANES), F32),
        scratch_shapes=[pltpu.VMEM((hb, t, 1), F32), pltpu.VMEM((hb, t, LANES), F32),
                        pltpu.VMEM((hb, t, LANES), BF16)],
        compiler_params=_params(("parallel", "parallel", "arbitrary")),
    )(qa, ka, va, bound)


def _fill_shifts(ext_ref, sh_ref):
    rows = sh_ref.shape[1]
    for b in range(1, SUBLANES):
        sh_ref[b - 1] = ext_ref[pl.ds(b, rows), :]


def _tap_window(ext_ref, sh_ref, off, tm, cols):
    b = off % SUBLANES
    if b == 0:
        return ext_ref[pl.ds(off, tm), cols]
    return sh_ref[b - 1, pl.ds(off - b, tm), cols]


def _conv_taps(w_ref, ext_ref, sh_ref, out_ref, n_taps, tm, offset_of, bias_ref=None):
    for cc in range(out_ref.shape[1] // LANES):
        cols = slice(cc * LANES, (cc + 1) * LANES)
        acc = None
        for j in sorted(range(n_taps), key=offset_of):
            term = w_ref[j:j + 1, cols] * _tap_window(ext_ref, sh_ref, offset_of(j), tm, cols)
            acc = term if acc is None else acc + term
        out_ref[:, cols] = acc if bias_ref is None else acc + bias_ref[:, cols]


def _layernorm_stats(u2):
    mu = jnp.mean(u2, axis=-1, keepdims=True)
    xc = u2 - mu
    rstd = lax.rsqrt(jnp.mean(xc * xc, axis=-1, keepdims=True) + EPS)
    return xc * rstd, rstd


def _fwd_out(oa, z, x2, tgt, conv_w, conv_b, ln_g, ln_b, w_out, B, S, H, n_taps, tm):
    T, D = x2.shape
    FW = H * HEAD_DIM
    CW = conv_w.shape[1]
    nsb = S // tm
    hb = tm // HALO
    mb = 4 if nsb % 4 == 0 else 1
    mt = mb * tm

    def body(oa_ref, gf_ref, ga_ref, gb_ref, gc_ref, ha_ref, hb_ref, x_ref, t_ref, w_ref, cb_ref, lg_ref,
             lb_ref, wo_ref, y_ref, u2_ref, a_ref, do_ref, dob_ref, dy_ref, loss_ref, ext_ref, sh_ref):
        first_step = (pl.program_id(0) == 0) & (pl.program_id(1) == 0)
        sub = lax.rem(pl.program_id(1), mb)
        rows = pl.ds(pl.multiple_of(sub * tm, tm), tm)

        @pl.when(first_step)
        def _():
            loss_ref[...] = jnp.zeros_like(loss_ref)

        u1 = ga_ref[...] * _sigmoid(gb_ref[...])
        halo = ha_ref[...] * _sigmoid(hb_ref[...])
        ext_ref[0:HALO, :] = jnp.where(pl.program_id(1) > 0, halo, 0.0)
        ext_ref[HALO:, :] = u1
        _fill_shifts(ext_ref, sh_ref)
        _conv_taps(w_ref, ext_ref, sh_ref, u2_ref, n_taps, tm, lambda j: HALO - (n_taps - 1) + j, cb_ref)
        uh, _ = _layernorm_stats(u2_ref[...])
        u3 = uh * lg_ref[...] + lb_ref[...]
        gc = gc_ref[...]
        yu = u3 * _sigmoid(u3) * (gc * _sigmoid(gc))
        y_ref[rows, FW:] = yu.astype(BF16)

        lane = _lane((tm, LANES))
        lo = lane < HEAD_DIM
        for p in range(H // 2):
            a_ref[:, p * LANES:(p + 1) * LANES] = jnp.where(
                lo, oa_ref[0, 2 * p], pltpu.roll(oa_ref[0, 2 * p + 1], HEAD_DIM, 1))
        gf = gf_ref[...]
        y_ref[rows, :FW] = (a_ref[...] * (gf * _sigmoid(gf))).astype(BF16)

        @pl.when(sub == mb - 1)
        def _():
            out = x_ref[...] + _dot(y_ref[...], wo_ref[...])
            diff = out - t_ref[...]
            loss_ref[...] = loss_ref[...] + jnp.sum(diff * diff)
            dout = diff * (1.0 / D)
            do_ref[...] = dout
            dob = dout.astype(BF16)
            dob_ref[...] = dob
            dy_ref[...] = _dot_nt(dob, wo_ref[...])

    row = lambda b, s: b * nsb + s
    zspec = lambda cb: pl.BlockSpec((tm, FW), lambda b, s: (row(b, s), cb))
    hspec = lambda cb: pl.BlockSpec((HALO, CW), lambda b, s: (jnp.maximum(row(b, s) * hb - 1, 0), cb))
    vspec = pl.BlockSpec((1, CW), lambda b, s: (0, 0))
    tspec = lambda w: pl.BlockSpec((tm, w), lambda b, s: (row(b, s), 0))
    mspec = lambda w: pl.BlockSpec((mt, w), lambda b, s: (row(b, s) // mb, 0))
    return pl.pallas_call(
        body, name="fwd_out", grid=(B, nsb),
        in_specs=[pl.BlockSpec((1, H, tm, LANES), lambda b, s: (b, 0, s, 0)),
                  zspec(3), zspec(4), zspec(5), zspec(6), hspec(4), hspec(5),
                  mspec(D), mspec(D),
                  pl.BlockSpec((HALO, CW), lambda b, s: (0, 0)), vspec, vspec, vspec,
                  pl.BlockSpec((FW + CW, D), lambda b, s: (0, 0))],
        out_specs=[mspec(FW + CW), tspec(CW), tspec(FW), mspec(D), mspec(D), mspec(FW + CW),
                   pl.BlockSpec((8, LANES), lambda b, s: (0, 0))],
        out_shape=[SDS((T, FW + CW), BF16), SDS((T, CW), F32), SDS((T, FW), F32), SDS((T, D), F32),
                   SDS((T, D), BF16), SDS((T, FW + CW), F32), SDS((8, LANES), F32)],
        scratch_shapes=[pltpu.VMEM((tm + HALO, CW), F32),
                        pltpu.VMEM((SUBLANES - 1, tm + HALO - SUBLANES, CW), F32)],
        compiler_params=_params(("arbitrary", "arbitrary")),
    )(oa, z, z, z, z, z, z, x2, tgt, conv_w, conv_b, ln_g, ln_b, w_out)


def _bwd_prep(dy, z, a_nat, oa, qa, u2, ln_g, ln_b, B, S, H, tm):
    T = B * S
    FW = H * HEAD_DIM
    CW = u2.shape[1]
    nsb = S // tm

    def body(dya_ref, dyu_ref, gf_ref, gc_ref, a_ref, oa_ref, qa_ref, u2_ref, lg_ref, lb_ref,
             dzgf_ref, dzgc_ref, du2_ref, doa_ref, qb_ref, sg_ref):
        first_step = (pl.program_id(0) == 0) & (pl.program_id(1) == 0)

        @pl.when(first_step)
        def _():
            sg_ref[...] = jnp.zeros_like(sg_ref)

        gf = gf_ref[...]
        sg = _sigmoid(gf)
        a = a_ref[...]
        dya = dya_ref[...]
        da = dya * (gf * sg)
        dzgf_ref[...] = (dya * a * (sg * (1.0 + gf * (1.0 - sg)))).astype(BF16)
        dd = da * a
        lane = _lane((tm, LANES))
        lo = lane < HEAD_DIM
        for p in range(H // 2):
            cols = slice(p * LANES, (p + 1) * LANES)
            da_p = da[:, cols]
            dd_p = dd[:, cols]
            d_heads = (jnp.sum(jnp.where(lo, dd_p, 0.0), axis=-1, keepdims=True),
                       jnp.sum(jnp.where(lo, 0.0, dd_p), axis=-1, keepdims=True))
            for e in range(2):
                da_e = da_p if e == 0 else pltpu.roll(da_p, HEAD_DIM, 1)
                d_e = d_heads[e]
                aug = _put3(jnp.zeros((tm, LANES), F32), lane, L_D, _split3(-d_e))
                doa_ref[0, 2 * p + e] = jnp.where(lo, da_e, aug).astype(BF16)
                lse = _lane_col(oa_ref[0, 2 * p + e], lane, L_ROWSUM)
                qb = _put3(qa_ref[0, 2 * p + e].astype(F32), lane, L_LSE, _split3(-lse))
                qb_ref[0, 2 * p + e] = qb.astype(BF16)

        gc = gc_ref[...]
        sc = _sigmoid(gc)
        dyu = dyu_ref[...]
        uh, rstd = _layernorm_stats(u2_ref[...])
        u3 = uh * lg_ref[...] + lb_ref[...]
        s3 = _sigmoid(u3)
        dzgc_ref[...] = (dyu * (u3 * s3) * (sc * (1.0 + gc * (1.0 - sc)))).astype(BF16)
        du3 = dyu * (gc * sc) * (s3 * (1.0 + u3 * (1.0 - s3)))
        sg_ref[0:1, :] = sg_ref[0:1, :] + jnp.sum(du3 * uh, axis=0, keepdims=True)
        sg_ref[1:2, :] = sg_ref[1:2, :] + jnp.sum(du3, axis=0, keepdims=True)
        duh = du3 * lg_ref[...]
        du2 = rstd * (duh - jnp.mean(duh, axis=-1, keepdims=True)
                      - uh * jnp.mean(duh * uh, axis=-1, keepdims=True))
        sg_ref[2:3, :] = sg_ref[2:3, :] + jnp.sum(du2, axis=0, keepdims=True)
        du2_ref[...] = du2

    row = lambda b, s: b * nsb + s
    tspec = lambda w, cb=0: pl.BlockSpec((tm, w), lambda b, s: (row(b, s), cb))
    hspec = pl.BlockSpec((1, H, tm, LANES), lambda b, s: (b, 0, s, 0))
    vspec = pl.BlockSpec((1, CW), lambda b, s: (0, 0))
    return pl.pallas_call(
        body, name="bwd_prep", grid=(B, nsb),
        in_specs=[tspec(FW, 0), tspec(CW, 1), tspec(FW, 3), tspec(CW, 6), tspec(FW), hspec, hspec,
                  tspec(CW), vspec, vspec],
        out_specs=[tspec(FW), tspec(CW), tspec(CW), hspec, hspec,
                   pl.BlockSpec((8, CW), lambda b, s: (0, 0))],
        out_shape=[SDS((T, FW), BF16), SDS((T, CW), BF16), SDS((T, CW), F32),
                   SDS((B, H, S, LANES), BF16), SDS((B, H, S, LANES), BF16), SDS((8, CW), F32)],
        compiler_params=_params(("arbitrary", "arbitrary")),
    )(dy, dy, z, z, a_nat, oa, qa, u2, ln_g, ln_b)


def _conv_bwd(du2, z, conv_w, B, S, n_taps, tm):
    T, CW = du2.shape
    nsb = S // tm
    hb = tm // HALO

    def body(d_ref, dh_ref, ga_ref, gb_ref, ha_ref, hb_ref, w_ref, dz_ref, dw_ref,
             extu_ref, extd_ref, shu_ref, shd_ref, du1_ref, dwacc_ref):
        s = pl.program_id(1)
        first_step = (pl.program_id(0) == 0) & (s == 0)
        last_step = (pl.program_id(0) == B - 1) & (s == nsb - 1)

        @pl.when(first_step)
        def _():
            dwacc_ref[...] = jnp.zeros_like(dwacc_ref)

        ga = ga_ref[...]
        sb = _sigmoid(gb_ref[...])
        halo = ha_ref[...] * _sigmoid(hb_ref[...])
        extu_ref[0:HALO, :] = jnp.where(s > 0, halo, 0.0)
        extu_ref[HALO:, :] = ga * sb
        extd_ref[0:tm, :] = d_ref[...]
        extd_ref[tm:, :] = jnp.where(s < nsb - 1, dh_ref[...], 0.0)
        _fill_shifts(extu_ref, shu_ref)
        _fill_shifts(extd_ref, shd_ref)
        _conv_taps(w_ref, extd_ref, shd_ref, du1_ref, n_taps, tm, lambda j: n_taps - 1 - j)
        for cc in range(CW // LANES):
            cols = slice(cc * LANES, (cc + 1) * LANES)
            parts = [None] * n_taps
            for r in range(tm // SUBLANES):
                dv = d_ref[r * SUBLANES:(r + 1) * SUBLANES, cols]
                for j in range(n_taps):
                    off = HALO - (n_taps - 1) + j + r * SUBLANES
                    term = dv * _tap_window(extu_ref, shu_ref, off, SUBLANES, cols)
                    parts[j] = term if parts[j] is None else parts[j] + term
            for j in range(n_taps):
                rows = slice(j * SUBLANES, (j + 1) * SUBLANES)
                dwacc_ref[rows, cols] = dwacc_ref[rows, cols] + parts[j]
        du1 = du1_ref[...]
        dz_ref[:, :CW] = (du1 * sb).astype(BF16)
        dz_ref[:, CW:] = (du1 * ga * (sb * (1.0 - sb))).astype(BF16)

        @pl.when(last_step)
        def _():
            dw_ref[...] = jnp.zeros_like(dw_ref)
            for j in range(n_taps):
                dw_ref[j:j + 1, :] = jnp.sum(dwacc_ref[j * SUBLANES:(j + 1) * SUBLANES, :], axis=0, keepdims=True)

    row = lambda b, s: b * nsb + s
    last_halo = T // HALO - 1
    return pl.pallas_call(
        body, name="conv_bwd", grid=(B, nsb),
        in_specs=[pl.BlockSpec((tm, CW), lambda b, s: (row(b, s), 0)),
                  pl.BlockSpec((HALO, CW), lambda b, s: (jnp.minimum((row(b, s) + 1) * hb, last_halo), 0)),
                  pl.BlockSpec((tm, CW), lambda b, s: (row(b, s), 4)),
                  pl.BlockSpec((tm, CW), lambda b, s: (row(b, s), 5)),
                  pl.BlockSpec((HALO, CW), lambda b, s: (jnp.maximum(row(b, s) * hb - 1, 0), 4)),
                  pl.BlockSpec((HALO, CW), lambda b, s: (jnp.maximum(row(b, s) * hb - 1, 0), 5)),
                  pl.BlockSpec((HALO, CW), lambda b, s: (0, 0))],
        out_specs=[pl.BlockSpec((tm, 2 * CW), lambda b, s: (row(b, s), 0)),
                   pl.BlockSpec((HALO, CW), lambda b, s: (0, 0))],
        out_shape=[SDS((T, 2 * CW), BF16), SDS((HALO, CW), F32)],
        scratch_shapes=[pltpu.VMEM((tm + HALO, CW), F32), pltpu.VMEM((tm + HALO, CW), F32),
                        pltpu.VMEM((SUBLANES - 1, tm + HALO - SUBLANES, CW), F32),
                        pltpu.VMEM((SUBLANES - 1, tm + HALO - SUBLANES, CW), F32),
                        pltpu.VMEM((tm, CW), F32), pltpu.VMEM((HALO * SUBLANES, CW), F32)],
        compiler_params=_params(("arbitrary", "arbitrary")),
    )(du2, du2, z, z, z, z, conv_w)


def _attn_bwd(qb, ka, va, doa, t, hb):
    B, H, S, _ = qb.shape
    nk = S // t

    def body(q_ref, k_ref, v_ref, do_ref, dq_ref, dk_ref, dv_ref, dv_acc):
        j = pl.program_id(2)

        @pl.when(j == 0)
        def _():
            dq_ref[...] = jnp.zeros_like(dq_ref)

        dk_ref[...] = jnp.zeros_like(dk_ref)
        dv_acc[...] = jnp.zeros_like(dv_acc)

        def step(i, masked):
            q_rows = pl.ds(pl.multiple_of(i * t, t), t)
            for e in range(hb):
                k = k_ref[0, e]
                q = q_ref[0, e, q_rows, :]
                do = do_ref[0, e, q_rows, :]
                p = jnp.exp(_dot_nt(q, k))
                if masked:
                    keep = lax.broadcasted_iota(jnp.int32, (t, t), 0) >= lax.broadcasted_iota(jnp.int32, (t, t), 1)
                    p = jnp.where(keep, p, 0.0)
                ds = (p * _dot_nt(do, v_ref[0, e])).astype(BF16)
                dv_acc[e] = dv_acc[e] + _dot_tn(p.astype(BF16), do)
                dk_ref[0, e] = dk_ref[0, e] + _dot_tn(ds, q)
                dq_ref[0, e, q_rows, :] = dq_ref[0, e, q_rows, :] + _dot(ds, k)

        step(j, True)

        def loop_body(i, carry):
            step(i, False)
            return carry

        lax.fori_loop(j + 1, nk, loop_body, 0)
        dv_ref[0] = dv_acc[...].astype(BF16)

    full = pl.BlockSpec((1, hb, S, LANES), lambda b, h, j: (b, h, 0, 0))
    blk = pl.BlockSpec((1, hb, t, LANES), lambda b, h, j: (b, h, j, 0))
    oshape = SDS((B, H, S, LANES), F32)
    return pl.pallas_call(
        body, name="attn_bwd", grid=(B, H // hb, nk),
        in_specs=[full, blk, blk, full],
        out_specs=[full, blk, blk],
        out_shape=[oshape, oshape, SDS((B, H, S, LANES), BF16)],
        scratch_shapes=[pltpu.VMEM((hb, t, LANES), F32)],
        compiler_params=_params(("parallel", "parallel", "arbitrary")),
    )(qb, ka, va, doa)


def _qk_bwd(dqa, dka, dva, z, gq, gk, B, S, H, tm):
    T = B * S
    FW = H * HEAD_DIM
    nsb = S // tm
    nfb = FW // LANES
    scale = HEAD_DIM ** -0.5

    def body(dq_ref, dk_ref, dv_ref, zq_ref, zk_ref, gq_ref, gk_ref, dzq_ref, dzk_ref, dzv_ref, dc_ref, dg_ref):
        p = pl.program_id(0)

        @pl.when(pl.program_id(1) == 0)
        def _():
            dg_ref[...] = jnp.zeros_like(dg_ref)

        lane = _lane((tm, LANES))
        lo = lane < HEAD_DIM

        def natural(ref):
            return jnp.where(lo, ref[0, 0].astype(F32), pltpu.roll(ref[0, 1].astype(F32), HEAD_DIM, 1))

        def norm_bwd(dn, x, g, row, out_ref):
            r = lax.rsqrt(_half_stats(x * x) * (1.0 / HEAD_DIM) + EPS)
            xh = x * r
            dg_ref[row:row + 1, :] = dg_ref[row:row + 1, :] + jnp.sum(dn * xh, axis=0, keepdims=True)
            dxh = dn * g
            mm = _half_stats(dxh * xh) * (1.0 / HEAD_DIM)
            out_ref[...] = (r * (dxh - xh * mm)).astype(BF16)

        norm_bwd(natural(dq_ref) * scale, zq_ref[...], gq_ref[...], 0, dzq_ref)
        norm_bwd(natural(dk_ref), zk_ref[...], gk_ref[...], 1, dzk_ref)
        dzv_ref[...] = natural(dv_ref).astype(BF16)

        dc = jnp.zeros((tm, LANES), F32)
        for e in range(2):
            val = _lane_col(dq_ref[0, e], lane, L_ROWSUM) - _lane_col(dk_ref[0, e], lane, L_KDECAY)
            dc = jnp.where(lane == 2 * p + e, val, dc)
        dc_ref[0] = dc

    hspec = pl.BlockSpec((1, 2, tm, LANES), lambda p, i: (i // nsb, p, i % nsb, 0))
    zspec = lambda off: pl.BlockSpec((tm, LANES), lambda p, i: (i, off + p))
    gspec = pl.BlockSpec((1, LANES), lambda p, i: (0, p))
    ospec = pl.BlockSpec((tm, LANES), lambda p, i: (i, p))
    return pl.pallas_call(
        body, name="qk_bwd", grid=(H // 2, T // tm),
        in_specs=[hspec, hspec, hspec, zspec(0), zspec(nfb), gspec, gspec],
        out_specs=[ospec, ospec, ospec,
                   pl.BlockSpec((1, tm, LANES), lambda p, i: (p, i, 0)),
                   pl.BlockSpec((8, LANES), lambda p, i: (0, p))],
        out_shape=[SDS((T, FW), BF16), SDS((T, FW), BF16), SDS((T, FW), BF16),
                   SDS((H // 2, T, LANES), F32), SDS((8, FW), F32)],
        compiler_params=_params(("parallel", "arbitrary")),
    )(dqa, dka, dva, z, z, gq, gk)


def _gate_bwd(dc8, z, b_pad, B, S, H, col_blk, fp, tc):
    T = B * S
    nsb = S // tc
    npair = dc8.shape[0]

    def body(dc_ref, zf_ref, b_ref, dz_ref, db_ref, carry):
        first_step = (pl.program_id(0) == 0) & (pl.program_id(1) == 0)

        @pl.when(first_step)
        def _():
            db_ref[...] = jnp.zeros_like(db_ref)

        @pl.when(pl.program_id(1) == 0)
        def _():
            carry[...] = jnp.zeros_like(carry)

        dc = dc_ref[0]
        for k in range(1, npair):
            dc = dc + dc_ref[k]
        dlf = _tri_cumsum(dc, True) + carry[...]
        carry[...] = carry[...] + jnp.sum(dc, axis=0, keepdims=True)
        x = zf_ref[...] + b_ref[...]
        dlogit = dlf * _sigmoid(-x)
        db_ref[0:1, :] = db_ref[0:1, :] + jnp.sum(dlogit, axis=0, keepdims=True)
        dz_ref[...] = jnp.zeros_like(dz_ref)
        dz_ref[:, :LANES] = dlogit.astype(BF16)

    rrow = lambda b, s: b * nsb + (nsb - 1 - s)
    return pl.pallas_call(
        body, name="gate_bwd", grid=(B, nsb),
        in_specs=[pl.BlockSpec((npair, tc, LANES), lambda b, s: (0, rrow(b, s), 0)),
                  pl.BlockSpec((tc, LANES), lambda b, s: (rrow(b, s), col_blk)),
                  pl.BlockSpec((1, LANES), lambda b, s: (0, 0))],
        out_specs=[pl.BlockSpec((tc, fp), lambda b, s: (rrow(b, s), 0)),
                   pl.BlockSpec((8, LANES), lambda b, s: (0, 0))],
        out_shape=[SDS((T, fp), BF16), SDS((8, LANES), F32)],
        scratch_shapes=[pltpu.VMEM((1, LANES), F32)],
        compiler_params=_params(("arbitrary", "arbitrary")),
    )(dc8, z, b_pad)


def _matmul_tn(a, b, name, tmm, tn, tk):
    T, M = a.shape
    N = b.shape[1]
    tmm, tn, tk = min(tmm, M), min(tn, N), min(tk, T)

    def body(a_ref, b_ref, o_ref):
        @pl.when(pl.program_id(2) == 0)
        def _():
            o_ref[...] = jnp.zeros_like(o_ref)

        o_ref[...] = o_ref[...] + _dot_tn(a_ref[...], b_ref[...])

    return pl.pallas_call(
        body, name=name, grid=(M // tmm, N // tn, T // tk),
        in_specs=[pl.BlockSpec((tk, tmm), lambda i, j, k: (k, i)),
                  pl.BlockSpec((tk, tn), lambda i, j, k: (k, j))],
        out_specs=pl.BlockSpec((tmm, tn), lambda i, j, k: (i, j)),
        out_shape=SDS((M, N), F32),
        compiler_params=_params(("parallel", "parallel", "arbitrary")),
    )(a, b)


def _dh_rms_bwd(pieces, w_t, x2, g, dout, tm, tk, parts):
    T, D = x2.shape
    nks = [p.shape[1] // tk for p in pieces]
    starts = [sum(nks[:k]) for k in range(len(pieces))]
    nk = sum(nks)
    ni = T // tm
    n = len(parts)

    def body(*refs):
        dz_refs = refs[:len(pieces)]
        w_ref, x_ref, g_ref, do_ref = refs[len(pieces):len(pieces) + 4]
        part_refs = refs[len(pieces) + 4:len(pieces) + 4 + n]
        gx_ref, dg_ref = refs[len(pieces) + 4 + n:len(pieces) + 6 + n]
        slot_refs = refs[len(pieces) + 6 + n:len(pieces) + 6 + 2 * n]
        acc_ref, send_sems, recv_sems = refs[len(pieces) + 6 + 2 * n:]
        k = pl.program_id(1)
        first_step = (pl.program_id(0) == 0) & (k == 0)
        last_step = (pl.program_id(0) == ni - 1) & (k == nk - 1)
        x, y, c = _place()
        chips = [(1 - x, y), (x, 1 - y), (1 - x, 1 - y)]

        def copy(a, f, to):
            cx, cy = chips[f]
            return pltpu.make_async_remote_copy(
                src_ref=part_refs[a].at[2 * cx + cy], dst_ref=slot_refs[a].at[f],
                send_sem=send_sems.at[a * 3 + f], recv_sem=recv_sems.at[a * 3 + f],
                device_id=to, device_id_type=MESH)

        @pl.when(first_step)
        def _():
            dg_ref[...] = jnp.zeros_like(dg_ref)
            for a in range(n):
                for f in range(3):
                    copy(a, f, (*chips[f], c)).start()

        @pl.when(last_step)
        def _():
            for a in range(n):
                for f in range(3):
                    copy(a, f, (x, y, c)).wait_recv()
            for a in range(n):
                for f in range(3):
                    copy(a, f, (*chips[f], c)).wait_send()

        @pl.when(k == 0)
        def _():
            acc_ref[...] = jnp.zeros_like(acc_ref)

        for dz_ref, st, cnt in zip(dz_refs, starts, nks):
            @pl.when((k >= st) & (k < st + cnt))
            def _(dz_ref=dz_ref):
                acc_ref[...] = acc_ref[...] + _dot(dz_ref[...], w_ref[...])

        @pl.when(k == nk - 1)
        def _():
            x = x_ref[...]
            r = lax.rsqrt(jnp.mean(x * x, axis=-1, keepdims=True) + EPS)
            xh = x * r
            dh = acc_ref[...]
            dg_ref[0:1, :] = dg_ref[0:1, :] + jnp.sum(dh * xh, axis=0, keepdims=True)
            dxn = dh * g_ref[...]
            gx_ref[...] = do_ref[...] + r * (dxn - xh * jnp.mean(dxn * xh, axis=-1, keepdims=True))

    def piece_spec(st, cnt):
        return pl.BlockSpec((tm, tk), lambda i, k: (i, jnp.clip(k - st, 0, cnt - 1)))

    tspec = pl.BlockSpec((tm, D), lambda i, k: (i, 0))
    return pl.pallas_call(
        body, name="dh_rms_bwd", grid=(T // tm, nk),
        in_specs=[piece_spec(st, cnt) for st, cnt in zip(starts, nks)]
        + [pl.BlockSpec((tk, D), lambda i, k: (k, 0)), tspec, pl.BlockSpec((1, D), lambda i, k: (0, 0)), tspec]
        + [ANY] * n,
        out_specs=[tspec, pl.BlockSpec((8, D), lambda i, k: (0, 0))] + [ANY] * n,
        out_shape=[SDS((T, D), F32), SDS((8, D), F32)] + [SDS((3,) + p.shape[1:], p.dtype) for p in parts],
        scratch_shapes=[pltpu.VMEM((tm, D), F32),
                        pltpu.SemaphoreType.DMA((3 * n,)), pltpu.SemaphoreType.DMA((3 * n,))],
        compiler_params=_params(("arbitrary", "arbitrary")),
    )(*pieces, w_t, x2, g, dout, *parts)


def _block_plan(R, C, tr, tc):
    br = min(tr, R)
    if R % br == 0:
        return (br, C), R // br, lambda i: (i, 0)
    bc = min(tc, C)
    assert C % bc == 0
    return (R, bc), C // bc, lambda i: (0, i)


def _ew_call(body, name, ins, n_out, out_dtypes, tr, tc):
    R, C = ins[0].shape
    blk, steps, imap = _block_plan(R, C, tr, tc)
    spec = pl.BlockSpec(blk, imap)
    return pl.pallas_call(
        body, name=name, grid=(steps,),
        in_specs=[spec] * len(ins), out_specs=[spec] * n_out,
        out_shape=[SDS((R, C), dt) for dt in out_dtypes],
        compiler_params=_params(("parallel",)),
    )(*ins)


def _sum_slots(slots, name, first=None, tr=256):
    n, R, C = slots.shape
    blk, steps, imap = _block_plan(R, C, tr, 2 * LANES)
    lead = [] if first is None else [first]

    def body(*refs):
        s_ref, o_ref = refs[-2:]
        acc = refs[0][...].astype(F32) if lead else s_ref[0].astype(F32)
        for k in range(0 if lead else 1, n):
            acc = acc + s_ref[k].astype(F32)
        o_ref[...] = acc

    return pl.pallas_call(
        body, name=name, grid=(steps,),
        in_specs=[pl.BlockSpec(blk, imap)] * len(lead) + [pl.BlockSpec((n,) + blk, lambda i: (0,) + imap(i))],
        out_specs=pl.BlockSpec(blk, imap),
        out_shape=SDS((R, C), F32),
        compiler_params=_params(("parallel",)),
    )(*lead, slots)


def _adamw_update(w, g, m, v):
    nm = ADAM_B1 * m + (1.0 - ADAM_B1) * g
    nv = ADAM_B2 * v + (1.0 - ADAM_B2) * (g * g)
    m_hat = nm / (1.0 - ADAM_B1 ** ADAM_STEP)
    v_hat = nv / (1.0 - ADAM_B2 ** ADAM_STEP)
    return -ADAM_LR * (m_hat / (jnp.sqrt(v_hat) + ADAM_EPS) + ADAM_WD * w), nm, nv


def _adamw(w, g, m, v, name):
    def body(w_ref, g_ref, m_ref, v_ref, d_ref, nm_ref, nv_ref):
        d_ref[...], nm_ref[...], nv_ref[...] = _adamw_update(w_ref[...], g_ref[...], m_ref[...], v_ref[...])

    return _ew_call(body, name, [w, g, m, v], 3, [F32, F32, F32], 128, 2 * LANES)


def _adamw_halves(w, mine, other, m, v, name):
    R, C = w.shape
    half = C // 2
    bc = min(2 * LANES, half)
    per = half // bc

    def body(w_ref, a_ref, b_ref, m_ref, v_ref, g_ref, d_ref, nm_ref, nv_ref):
        g = jnp.where(pl.program_id(0) // per == lax.axis_index("c"), a_ref[...], b_ref[...])
        g_ref[...] = g
        d_ref[...], nm_ref[...], nv_ref[...] = _adamw_update(w_ref[...], g, m_ref[...], v_ref[...])

    full = pl.BlockSpec((R, bc), lambda i: (0, i))
    part = pl.BlockSpec((R, bc), lambda i: (0, i % per))
    return pl.pallas_call(
        body, name=name, grid=(C // bc,),
        in_specs=[full, part, part, full, full], out_specs=[full] * 4,
        out_shape=[SDS((R, C), F32)] * 4,
        compiler_params=_params(("parallel",)),
    )(w, mine, other, m, v)


ANY = pl.BlockSpec(memory_space=pl.ANY)


def _place():
    return lax.axis_index("x"), lax.axis_index("y"), lax.axis_index("c")


def _gather_chips(shards, splits):
    n = len(shards)
    per = 7

    def body(*refs):
        ins, outs = refs[:n], refs[n:2 * n]
        send_sems, recv_sems = refs[2 * n:]
        x, y, c = _place()
        mine = 2 * x + y
        chips = [(1 - x, y), (x, 1 - y), (1 - x, 1 - y)]

        def rows(a, half):
            return pl.ds(0, splits[a]) if half == 0 else pl.ds(splits[a], ins[a].shape[0] - splits[a])

        def copy(a, k, chip_idx, half, to, src=None):
            dst = outs[a].at[chip_idx, rows(a, half)]
            return pltpu.make_async_remote_copy(
                src_ref=dst if src is None else src, dst_ref=dst,
                send_sem=send_sems.at[a * per + k], recv_sem=recv_sems.at[a * per + k],
                device_id=to, device_id_type=MESH)

        def own(a, to):
            return pltpu.make_async_remote_copy(
                src_ref=ins[a], dst_ref=outs[a].at[mine],
                send_sem=send_sems.at[a * per + 6], recv_sem=recv_sems.at[a * per + 6],
                device_id=to, device_id_type=MESH)

        for cc in (0, 1):
            @pl.when(c == cc)
            def _(cc=cc):
                me, sibling = (x, y, cc), (x, y, 1 - cc)
                first = [copy(a, k, mine, cc, (*chip, cc), src=ins[a].at[rows(a, cc)])
                         for a in range(n) for k, chip in enumerate(chips)]
                first += [own(a, sibling) for a in range(n)]
                for cp in first:
                    cp.start()
                passed = []
                for k, (cx, cy) in enumerate(chips):
                    for a in range(n):
                        copy(a, k, 2 * cx + cy, cc, me).wait_recv()
                        fwd = copy(a, 3 + k, 2 * cx + cy, cc, sibling)
                        fwd.start()
                        passed.append(fwd)
                for k, (cx, cy) in enumerate(chips):
                    for a in range(n):
                        copy(a, 3 + k, 2 * cx + cy, 1 - cc, me).wait_recv()
                for a in range(n):
                    own(a, me).wait_recv()
                for cp in first + passed:
                    cp.wait_send()

    return pl.pallas_call(
        body, name="gather_chips",
        in_specs=[ANY] * n, out_specs=[ANY] * n,
        out_shape=[SDS((4,) + s.shape, s.dtype) for s in shards],
        scratch_shapes=[pltpu.SemaphoreType.DMA((per * n,)), pltpu.SemaphoreType.DMA((per * n,))],
    )(*shards)


def _pair_swap(arrs):
    n = len(arrs)

    def body(*refs):
        ins, outs = refs[:n], refs[n:2 * n]
        send_sems, recv_sems = refs[2 * n:]
        x, y, c = _place()
        for cc in (0, 1):
            @pl.when(c == cc)
            def _(cc=cc):
                copies = []
                for a in range(n):
                    half = ins[a].shape[1] // 2
                    copies.append(pltpu.make_async_remote_copy(
                        src_ref=ins[a].at[:, pl.ds((1 - cc) * half, half)], dst_ref=outs[a],
                        send_sem=send_sems.at[a], recv_sem=recv_sems.at[a],
                        device_id=(x, y, 1 - cc), device_id_type=MESH))
                for cp in copies:
                    cp.start()
                for cp in copies:
                    cp.wait()

    return pl.pallas_call(
        body, name="pair_swap",
        in_specs=[ANY] * n, out_specs=[ANY] * n,
        out_shape=[SDS((h.shape[0], h.shape[1] // 2), h.dtype) for h in arrs],
        scratch_shapes=[pltpu.SemaphoreType.DMA((n,)), pltpu.SemaphoreType.DMA((n,))],
    )(*arrs)


def _pair_sum(arrs, got, core, tr):
    half = arrs[0].shape[1] // 2
    cnts = [p.shape[0] // tr for p in arrs]
    starts = [sum(cnts[:k]) for k in range(len(arrs))]

    def body(core_ref, *refs):
        del core_ref
        own_refs, got_refs, o_ref = refs[:len(arrs)], refs[len(arrs):2 * len(arrs)], refs[-1]
        s = pl.program_id(0)
        for own_ref, got_ref, st, cnt in zip(own_refs, got_refs, starts, cnts):
            @pl.when((s >= st) & (s < st + cnt))
            def _(own_ref=own_ref, got_ref=got_ref):
                o_ref[...] = (own_ref[...] + got_ref[...]).astype(BF16)

    def own_spec(st, cnt):
        return pl.BlockSpec((tr, half), lambda s, core_ref: (jnp.clip(s - st, 0, cnt - 1), core_ref[0]))

    def got_spec(st, cnt):
        return pl.BlockSpec((tr, half), lambda s, core_ref: (jnp.clip(s - st, 0, cnt - 1), 0))

    return pl.pallas_call(
        body, name="pair_sum",
        grid_spec=pltpu.PrefetchScalarGridSpec(
            num_scalar_prefetch=1, grid=(sum(cnts),),
            in_specs=[own_spec(st, cnt) for st, cnt in zip(starts, cnts)]
            + [got_spec(st, cnt) for st, cnt in zip(starts, cnts)],
            out_specs=pl.BlockSpec((tr, half), lambda s, core_ref: (s, 0))),
        out_shape=SDS((sum(cnts) * tr, half), BF16),
        compiler_params=_params(("arbitrary",)),
    )(core, *arrs, *got)


def _share_results(arrs, rows):
    n = len(arrs)
    flips = [(fx, fy, fc) for fx in (0, 1) for fy in (0, 1) for fc in (0, 1)][1:]

    def body(*refs):
        ins, rows_ref, outs, all_ref = refs[:n], refs[n], refs[n + 1:2 * n + 1], refs[2 * n + 1]
        send_sems, recv_sems, local_sem = refs[2 * n + 2:]
        x, y, c = _place()
        me = 4 * x + 2 * y + c
        local = pltpu.make_async_copy(rows_ref, all_ref.at[me], local_sem)
        local.start()
        copies = [pltpu.make_async_remote_copy(
            src_ref=ins[a], dst_ref=outs[a], send_sem=send_sems.at[a], recv_sem=recv_sems.at[a],
            device_id=(x, y, 1 - c), device_id_type=MESH) for a in range(n)]
        for k, (fx, fy, fc) in enumerate(flips):
            copies.append(pltpu.make_async_remote_copy(
                src_ref=rows_ref, dst_ref=all_ref.at[me], send_sem=send_sems.at[n + k], recv_sem=recv_sems.at[n + k],
                device_id=(x ^ fx, y ^ fy, c ^ fc), device_id_type=MESH))
        for cp in copies:
            cp.start()
        for cp in copies[:n]:
            cp.wait_recv()
        for k, (fx, fy, fc) in enumerate(flips):
            src = 4 * (x ^ fx) + 2 * (y ^ fy) + (c ^ fc)
            pltpu.make_async_remote_copy(
                src_ref=rows_ref, dst_ref=all_ref.at[src], send_sem=send_sems.at[n + k], recv_sem=recv_sems.at[n + k],
                device_id=(x, y, c), device_id_type=MESH).wait_recv()
        for cp in copies:
            cp.wait_send()
        local.wait()

    outs = pl.pallas_call(
        body, name="share_results",
        in_specs=[ANY] * (n + 1), out_specs=[ANY] * (n + 1),
        out_shape=[SDS(h.shape, h.dtype) for h in arrs] + [SDS((8,) + rows.shape, rows.dtype)],
        scratch_shapes=[pltpu.SemaphoreType.DMA((n + 7,)), pltpu.SemaphoreType.DMA((n + 7,)),
                        pltpu.SemaphoreType.DMA],
    )(*arrs, rows)
    return outs[:n], outs[n]


def _tiles(S, FW):
    big = FW % 512 == 0
    return dict(
        fp=512 if big else LANES,
        tn=1536 if big else LANES,
        tm_in=min(1024, S),
        t_attn=min(512, S),
        hb_fwd=8,
        hb_bwd=8,
        tm_prep=min(1024, S),
        tm_mix=min(128, S),
        tc=min(256, S),
        tk=512 if big else LANES,
    )


def kernel(x, norm_g, w_in, b_forget, q_norm_g, k_norm_g, conv_w, conv_b, conv_ln_g, conv_ln_b, w_out, loss_target, m_norm_g, m_w_in, m_b_forget, m_q_norm_g, m_k_norm_g, m_conv_w, m_conv_b, m_conv_ln_g, m_conv_ln_b, m_w_out, v_norm_g, v_w_in, v_b_forget, v_q_norm_g, v_k_norm_g, v_conv_w, v_conv_b, v_conv_ln_g, v_conv_ln_b, v_w_out):
    B, S, D = x.shape
    H, dh = q_norm_g.shape[1:]
    FW = H * dh
    CW = conv_b.shape[-1]
    n_taps, cw_shard = conv_w.shape[1:]
    in_shard = w_in.shape[2]
    out_shard = w_out.shape[1]
    assert dh == HEAD_DIM and H % 2 == 0 and H <= LANES and FW == CW == D
    assert n_taps - 1 <= HALO and 4 * cw_shard == CW and 4 * out_shard == FW + CW
    assert 4 * in_shard == 4 * FW + 3 * CW + H
    T = B * S
    tl = _tiles(S, FW)
    fp = tl["fp"]
    xi, yi, ci = _place()

    w_t = jnp.transpose(w_in[0])
    conv_pad = jnp.pad(conv_w[0], ((0, HALO - n_taps), (0, 0)))
    bf16_rows = 2 * SUBLANES
    g_in, g_out, g_cw = _gather_chips(
        [w_t.astype(BF16), w_out[0].astype(BF16), conv_pad],
        [in_shard // 2 // bf16_rows * bf16_rows, out_shard // 2, HALO // 2])
    w_t_full = g_in.reshape(4 * in_shard, D)
    w_out_full = g_out.reshape(FW + CW, D)
    conv_full = g_cw.transpose(1, 0, 2).reshape(HALO, CW)
    o_f = 3 * FW
    w_pack = jnp.concatenate([w_t_full[:o_f], w_t_full[o_f + H:],
                              jnp.pad(w_t_full[o_f:o_f + H], ((0, fp - H), (0, 0)))], axis=0)
    f_col = 4 * FW + 3 * CW

    x2 = x.reshape(T, D)
    tgt = loss_target.reshape(T, D)
    b_pad = jnp.pad(b_forget, ((0, 0), (0, LANES - H)))
    gq = q_norm_g.reshape(1, FW)
    gk = k_norm_g.reshape(1, FW)

    z, h = _fwd_in(x2, norm_g, w_pack, tl["tm_in"], tl["tn"])
    c = _gate_fwd(z, b_pad, B, S, H, f_col // LANES, tl["tc"])
    qa, ka, va = _attn_prep(z, c, gq, gk, B, S, H, tl["tm_prep"])
    gain = lambda g: jnp.max(jnp.abs(g[0]), axis=-1)
    bound = (NORM_SLACK ** 2 * dh ** 0.5) * gain(q_norm_g) * gain(k_norm_g)
    oa = _attn_fwd(qa, ka, va, jnp.broadcast_to(bound[:, None, None], (H, 1, LANES)), tl["t_attn"], tl["hb_fwd"])
    y, u2, a_nat, dout, dout_b, dy, loss_acc = _fwd_out(
        oa, z, x2, tgt, conv_full, conv_b, conv_ln_g, conv_ln_b, w_out_full, B, S, H, n_taps, tl["tm_mix"])

    dzgf, dzgc, du2, doa, qb, sg_conv = _bwd_prep(dy, z, a_nat, oa, qa, u2, conv_ln_g, conv_ln_b, B, S, H, tl["tm_mix"])
    dzglu, dconv_w = _conv_bwd(du2, z, conv_full, B, S, n_taps, tl["tm_mix"])
    dqa, dka, dva = _attn_bwd(qb, ka, va, doa, tl["t_attn"], tl["hb_bwd"])
    dzq, dzk, dzv, dc8, dg_qk = _qk_bwd(dqa, dka, dva, z, gq, gk, B, S, H, tl["tm_prep"])
    dzf, db_f = _gate_bwd(dc8, z, b_pad, B, S, H, f_col // LANES, fp, tl["tc"])
    pieces = [dzq, dzk, dzv, dzgf, dzglu, dzgc, dzf]
    dw_all = [_matmul_tn(p, h, f"dw_in_{k}", 1024, 1024, 1024) for k, p in enumerate(pieces)]
    dw_all.append(_matmul_tn(y, dout_b, "dw_out", 1024, 1024, 1024))

    summed = _pair_sum(dw_all, _pair_swap(dw_all), ci.astype(jnp.int32).reshape(1), fp)
    ends = [0]
    for t in dw_all:
        ends.append(ends[-1] + t.shape[0])
    spans, at = [], 0
    for k, rows in [(0, FW), (1, FW), (2, FW), (6, H), (3, FW), (4, 2 * CW), (5, CW)]:
        spans.append((at, rows, ends[k]))
        at += rows

    def chip_rows(j):
        lo, hi = j * in_shard, (j + 1) * in_shard
        return jnp.concatenate([summed[src + max(lo, a) - a:src + min(hi, a + n) - a]
                                for a, n, src in spans if max(lo, a) < min(hi, a + n)], axis=0)

    part_in = jnp.stack([chip_rows(j) for j in range(4)])
    part_out = summed[ends[7]:ends[8]].reshape(4, out_shard, D // 2)
    grad_x2, dg_norm, slots_in, slots_out = _dh_rms_bwd(
        pieces, w_pack, x2, norm_g, dout, tl["tm_in"], tl["tk"], [part_in, part_out])
    chip = 2 * xi + yi
    half_in = _sum_slots(slots_in, "chip_sum_in", lax.dynamic_index_in_dim(part_in, chip, 0, keepdims=False))
    half_out = _sum_slots(slots_out, "chip_sum_out", lax.dynamic_index_in_dim(part_out, chip, 0, keepdims=False))
    lanes_to_d = lambda t: jnp.pad(t, ((0, 0), (0, D - LANES)))
    small = jnp.concatenate([
        dg_norm[0:1], lanes_to_d(db_f[0:1, :]), dg_qk[0:1], dg_qk[1:2],
        sg_conv[2:3], sg_conv[0:1], sg_conv[1:2], dconv_w, lanes_to_d(loss_acc[0:1, :])], axis=0)
    n_small = small.shape[0]
    (other_in, other_out), all_small = _share_results([half_in, half_out], small)

    small_sum = _sum_slots(all_small, "small_sum", tr=n_small)
    loss = 0.5 * small_sum[n_small - 1, 0] / D
    grad_norm_g, grad_b_f = small_sum[0:1], small_sum[1:2, :H]
    grad_gq, grad_gk = small_sum[2:3].reshape(1, H, dh), small_sum[3:4].reshape(1, H, dh)
    grad_conv_b, grad_ln_g, grad_ln_b = small_sum[4:5], small_sum[5:6], small_sum[6:7]
    grad_conv_w = lax.dynamic_slice_in_dim(small_sum[7:7 + n_taps], chip * cw_shard, cw_shard, axis=1)

    in_t = _adamw_halves(w_t, half_in, other_in, jnp.transpose(m_w_in[0]), jnp.transpose(v_w_in[0]), "adamw_in")
    grad_w_in, d_in, nm_in, nv_in = (jnp.transpose(t)[None] for t in in_t)
    grad_w_out, d_out, nm_out, nv_out = (
        t[None] for t in _adamw_halves(w_out[0], half_out, other_out, m_w_out[0], v_w_out[0], "adamw_out"))
    d_cw, nm_cw, nv_cw = (t[None] for t in _adamw(conv_w[0], grad_conv_w, m_conv_w[0], v_conv_w[0], "adamw_conv_w"))

    def rows(ws):
        return jnp.concatenate([jnp.pad(t.reshape(1, -1), ((0, 0), (0, D - t.size))) for t in ws], axis=0)

    small_w = [norm_g, b_forget, q_norm_g, k_norm_g, conv_b, conv_ln_g, conv_ln_b]
    small_m = [m_norm_g, m_b_forget, m_q_norm_g, m_k_norm_g, m_conv_b, m_conv_ln_g, m_conv_ln_b]
    small_v = [v_norm_g, v_b_forget, v_q_norm_g, v_k_norm_g, v_conv_b, v_conv_ln_g, v_conv_ln_b]
    d_s, nm_s, nv_s = _adamw(rows(small_w), small_sum[0:7], rows(small_m), rows(small_v), "adamw_small")

    def unpack(t):
        return [t[k:k + 1, :w.size].reshape(w.shape) for k, w in enumerate(small_w)]

    def order(s, in_, cw, out_):
        ng, bf, qg, kg, cb, lg, lb = s
        return [ng, in_, bf, qg, kg, cw, cb, lg, lb, out_]

    grads = [grad_norm_g, grad_w_in, grad_b_f, grad_gq, grad_gk, grad_conv_w[None],
             grad_conv_b, grad_ln_g, grad_ln_b, grad_w_out]
    return (loss, grad_x2.reshape(B, S, D), *grads,
            *order(unpack(d_s), d_in, d_cw, d_out),
            *order(unpack(nm_s), nm_in, nm_cw, nm_out),
            *order(unpack(nv_s), nv_in, nv_cw, nv_out))
```

```python
import jax
import jax.numpy as jnp
from jax import lax
from jax.experimental import pallas as pl
from jax.experimental.pallas import tpu as pltpu

F32 = jnp.float32
BF16 = jnp.bfloat16
SDS = jax.ShapeDtypeStruct
MESH = pl.DeviceIdType.MESH

EPS = 1e-6
NEG_INF = -1e30
LANES = 128
SUBLANES = 8
HEAD_DIM = 64
HALO = 32
VMEM_LIMIT = 56 * 1024 * 1024

L_ROWSUM = 64
L_KDECAY = 67
L_LSE = 70
L_D = 65
NORM_SLACK = 1.02
SHIFT_MAX = 40.0

ADAM_LR = 0.001
ADAM_B1 = 0.9
ADAM_B2 = 0.999
ADAM_EPS = 1e-08
ADAM_WD = 0.01
ADAM_STEP = 10


def _params(sem, vmem=VMEM_LIMIT):
    return pltpu.CompilerParams(dimension_semantics=sem, vmem_limit_bytes=vmem)


def _sigmoid(x):
    return 1.0 / (1.0 + jnp.exp(-x))


def _split3(x):
    hi = x.astype(BF16).astype(F32)
    r = x - hi
    mid = r.astype(BF16).astype(F32)
    lo = (r - mid).astype(BF16).astype(F32)
    return hi, mid, lo


def _dot(a, b):
    return jnp.dot(a, b, preferred_element_type=F32)


def _dot_nt(a, b):
    return lax.dot_general(a, b, (((1,), (1,)), ((), ())), preferred_element_type=F32)


def _dot_tn(a, b):
    return lax.dot_general(a, b, (((0,), (0,)), ((), ())), preferred_element_type=F32)


def _lane(shape):
    return lax.broadcasted_iota(jnp.int32, shape, 1)


def _lane_col(x, lane, idx):
    return jnp.sum(jnp.where(lane == idx, x, 0.0), axis=-1, keepdims=True)


def _put3(base, lane, start, pieces):
    out = base
    for k, p in enumerate(pieces):
        out = jnp.where(lane == start + k, p, out)
    return out


def _half_stats(t):
    hi = t.astype(BF16)
    mid = (t - hi.astype(F32)).astype(BF16)
    row = lax.broadcasted_iota(jnp.int32, (2 * LANES, LANES), 0)
    col = lax.broadcasted_iota(jnp.int32, (2 * LANES, LANES), 1)
    same_half = (jnp.bitwise_and(row, LANES - 1) < HEAD_DIM) == (col < HEAD_DIM)
    return _dot(jnp.concatenate([hi, mid], axis=1), jnp.where(same_half, 1.0, 0.0).astype(BF16))


def _fwd_in(x2, g, w_t, tm, tn):
    T, D = x2.shape
    N = w_t.shape[0]

    def body(x_ref, g_ref, w_ref, z_ref, h_ref):
        @pl.when(pl.program_id(1) == 0)
        def _():
            x = x_ref[...]
            r = lax.rsqrt(jnp.mean(x * x, axis=-1, keepdims=True) + EPS)
            h_ref[...] = (x * r * g_ref[...]).astype(BF16)

        z_ref[...] = _dot_nt(h_ref[...], w_ref[...])

    return pl.pallas_call(
        body, name="fwd_in", grid=(T // tm, N // tn),
        in_specs=[pl.BlockSpec((tm, D), lambda i, j: (i, 0)),
                  pl.BlockSpec((1, D), lambda i, j: (0, 0)),
                  pl.BlockSpec((tn, D), lambda i, j: (j, 0))],
        out_specs=[pl.BlockSpec((tm, tn), lambda i, j: (i, j)),
                   pl.BlockSpec((tm, D), lambda i, j: (i, 0))],
        out_shape=[SDS((T, N), F32), SDS((T, D), BF16)],
        compiler_params=_params(("parallel", "arbitrary")),
    )(x2, g, w_t)


def _tri_cumsum(x, reverse):
    t = x.shape[0]
    row = lax.broadcasted_iota(jnp.int32, (t, t), 0)
    col = lax.broadcasted_iota(jnp.int32, (t, t), 1)
    tri = (row <= col) if reverse else (row >= col)
    tri = jnp.where(tri, 1.0, 0.0).astype(BF16)
    hi, mid, lo = _split3(x)
    return _dot(tri, hi.astype(BF16)) + _dot(tri, mid.astype(BF16)) + _dot(tri, lo.astype(BF16))


def _gate_fwd(z, b_pad, B, S, H, col_blk, tc):
    T = B * S
    nsb = S // tc

    def body(zf_ref, b_ref, c_ref, carry):
        @pl.when(pl.program_id(1) == 0)
        def _():
            carry[...] = jnp.zeros_like(carry)

        x = zf_ref[...] + b_ref[...]
        lf = jnp.minimum(x, 0.0) - jnp.log(1.0 + jnp.exp(-jnp.abs(x)))
        lf = jnp.where(_lane(lf.shape) < H, lf, 0.0)
        c_ref[...] = _tri_cumsum(lf, False) + carry[...]
        carry[...] = carry[...] + jnp.sum(lf, axis=0, keepdims=True)

    return pl.pallas_call(
        body, name="gate_fwd", grid=(B, nsb),
        in_specs=[pl.BlockSpec((tc, LANES), lambda b, s: (b * nsb + s, col_blk)),
                  pl.BlockSpec((1, LANES), lambda b, s: (0, 0))],
        out_specs=pl.BlockSpec((tc, LANES), lambda b, s: (b * nsb + s, 0)),
        out_shape=SDS((T, LANES), F32),
        scratch_shapes=[pltpu.VMEM((1, LANES), F32)],
        compiler_params=_params(("parallel", "arbitrary")),
    )(z, b_pad)


def _qk_normalize(x, g):
    r = lax.rsqrt(_half_stats(x * x) * (1.0 / HEAD_DIM) + EPS)
    return x * r * g


def _attn_prep(z, c, gq, gk, B, S, H, tm):
    T = B * S
    FW = H * HEAD_DIM
    nsb = S // tm
    nfb = FW // LANES
    scale = HEAD_DIM ** -0.5

    def body(zq_ref, zk_ref, zv_ref, c_ref, gq_ref, gk_ref, qa_ref, ka_ref, va_ref):
        p = pl.program_id(1)
        lane = _lane((tm, LANES))
        lo = lane < HEAD_DIM
        qn = _qk_normalize(zq_ref[...], gq_ref[...]) * scale
        kn = _qk_normalize(zk_ref[...], gk_ref[...])
        v = zv_ref[...]
        cc = c_ref[...]
        ones_q = ((lane >= L_KDECAY) & (lane < L_KDECAY + 3)).astype(F32)
        ones_k = (((lane >= L_ROWSUM) & (lane < L_ROWSUM + 3)) | ((lane >= L_LSE) & (lane < L_LSE + 3))).astype(F32)
        ones_v = ((lane >= L_ROWSUM) & (lane < L_D + 3)).astype(F32)
        for e in range(2):
            if e == 0:
                qe, ke, ve = qn, kn, v
            else:
                qe, ke, ve = (pltpu.roll(t, HEAD_DIM, 1) for t in (qn, kn, v))
            ch = _lane_col(cc, lane, 2 * p + e)
            pieces = _split3(ch)
            qa = jnp.where(lo, qe, _put3(ones_q, lane, L_ROWSUM, pieces))
            ka = jnp.where(lo, ke, _put3(ones_k, lane, L_KDECAY, [-t for t in pieces]))
            va = jnp.where(lo, ve, ones_v)
            qa_ref[0, e] = qa.astype(BF16)
            ka_ref[0, e] = ka.astype(BF16)
            va_ref[0, e] = va.astype(BF16)

    zspec = lambda off: pl.BlockSpec((tm, LANES), lambda i, p: (i, off + p))
    gspec = pl.BlockSpec((1, LANES), lambda i, p: (0, p))
    ospec = pl.BlockSpec((1, 2, tm, LANES), lambda i, p: (i // nsb, p, i % nsb, 0))
    oshape = SDS((B, H, S, LANES), BF16)
    return pl.pallas_call(
        body, name="attn_prep", grid=(T // tm, H // 2),
        in_specs=[zspec(0), zspec(nfb), zspec(2 * nfb),
                  pl.BlockSpec((tm, LANES), lambda i, p: (i, 0)), gspec, gspec],
        out_specs=[ospec, ospec, ospec],
        out_shape=[oshape, oshape, oshape],
        compiler_params=_params(("parallel", "arbitrary")),
    )(z, z, z, c, gq, gk)


def _attn_fwd(qa, ka, va, bound, t, hb):
    B, H, S, _ = qa.shape
    nq = S // t

    def body(q_ref, k_ref, v_ref, b_ref, o_ref, m_ref, acc_ref, qs_ref):
        i = pl.program_id(2)
        lane = _lane((t, LANES))

        shifts = [b_ref[e] for e in range(hb)]
        worst = shifts[0]
        for e in range(1, hb):
            worst = jnp.maximum(worst, shifts[e])
        bounded = jnp.max(worst) <= SHIFT_MAX
        acc_ref[...] = jnp.zeros_like(acc_ref)

        def tiles(step):
            def loop_body(j, carry):
                step(j, False)
                return carry

            lax.fori_loop(0, i, loop_body, 0)
            step(i, True)

        def keep_mask(n=t):
            return lax.broadcasted_iota(jnp.int32, (n, n), 0) >= lax.broadcasted_iota(jnp.int32, (n, n), 1)

        def finish(e, shift):
            acc = acc_ref[e]
            l = _lane_col(acc, lane, L_ROWSUM)
            o_ref[0, e] = jnp.where(lane < HEAD_DIM, acc / l, shift + jnp.log(l))

        @pl.when(bounded)
        def _():
            for e in range(hb):
                qs_ref[e] = _put3(q_ref[0, e].astype(F32), lane, L_LSE, _split3(-shifts[e])).astype(BF16)

            def pair(e, q_rows, k_start, n, masked):
                k_rows = pl.ds(pl.multiple_of(k_start, n), n)
                p = jnp.exp(_dot_nt(qs_ref[e, q_rows, :], k_ref[0, e, k_rows, :]))
                if masked:
                    p = jnp.where(keep_mask(n), p, 0.0)
                acc_ref[e, q_rows, :] = acc_ref[e, q_rows, :] + _dot(p.astype(BF16), v_ref[0, e, k_rows, :])

            def step(j, masked):
                for e in range(hb):
                    if masked:
                        h = t // 2
                        pair(e, slice(0, h), j * t, h, True)
                        pair(e, slice(h, t), j * t, h, False)
                        pair(e, slice(h, t), j * t + h, h, True)
                    else:
                        pair(e, slice(0, t), j * t, t, False)

            tiles(step)
            for e in range(hb):
                finish(e, shifts[e])

        @pl.when(jnp.logical_not(bounded))
        def _():
            m_ref[...] = jnp.full_like(m_ref, NEG_INF)

            def step(j, masked):
                rows = pl.ds(pl.multiple_of(j * t, t), t)
                for e in range(hb):
                    s = _dot_nt(q_ref[0, e], k_ref[0, e, rows, :])
                    if masked:
                        s = jnp.where(keep_mask(), s, NEG_INF)
                    m_prev = m_ref[e]
                    m_new = jnp.maximum(m_prev, jnp.max(s, axis=-1, keepdims=True))
                    alpha = jnp.exp(m_prev - m_new)
                    p = jnp.exp(s - m_new).astype(BF16)
                    acc_ref[e] = alpha * acc_ref[e] + _dot(p, v_ref[0, e, rows, :])
                    m_ref[e] = m_new

            tiles(step)
            for e in range(hb):
                finish(e, m_ref[e])

    return pl.pallas_call(
        body, name="attn_fwd", grid=(B, H // hb, nq),
        in_specs=[pl.BlockSpec((1, hb, t, LANES), lambda b, h, i: (b, h, i, 0)),
                  pl.BlockSpec((1, hb, S, LANES), lambda b, h, i: (b, h, 0, 0)),
                  pl.BlockSpec((1, hb, S, LANES), lambda b, h, i: (b, h, 0, 0)),
                  pl.BlockSpec((hb, 1, LANES), lambda b, h, i: (h, 0, 0))],
        out_specs=pl.BlockSpec((1, hb, t, LANES), lambda b, h, i: (b, h, i, 0)),
        out_shape=SDS((B, H, S, LANES), F32),
        scratch_shapes=[pltpu.VMEM((hb, t, 1), F32), pltpu.VMEM((hb, t, LANES), F32),
                        pltpu.VMEM((hb, t, LANES), BF16)],
        compiler_params=_params(("parallel", "parallel", "arbitrary")),
    )(qa, ka, va, bound)


def _fill_shifts(ext_ref, sh_ref):
    rows = sh_ref.shape[1]
    for b in range(1, SUBLANES):
        sh_ref[b - 1] = ext_ref[pl.ds(b, rows), :]


def _tap_window(ext_ref, sh_ref, off, tm, cols):
    b = off % SUBLANES
    if b == 0:
        return ext_ref[pl.ds(off, tm), cols]
    return sh_ref[b - 1, pl.ds(off - b, tm), cols]


def _conv_taps(w_ref, ext_ref, sh_ref, out_ref, n_taps, tm, offset_of, bias_ref=None):
    for cc in range(out_ref.shape[1] // LANES):
        cols = slice(cc * LANES, (cc + 1) * LANES)
        acc = None
        for j in sorted(range(n_taps), key=offset_of):
            term = w_ref[j:j + 1, cols] * _tap_window(ext_ref, sh_ref, offset_of(j), tm, cols)
            acc = term if acc is None else acc + term
        out_ref[:, cols] = acc if bias_ref is None else acc + bias_ref[:, cols]


def _layernorm_stats(u2):
    mu = jnp.mean(u2, axis=-1, keepdims=True)
    xc = u2 - mu
    rstd = lax.rsqrt(jnp.mean(xc * xc, axis=-1, keepdims=True) + EPS)
    return xc * rstd, rstd


def _fwd_out(oa, z, x2, tgt, conv_w, conv_b, ln_g, ln_b, w_out, B, S, H, n_taps, tm):
    T, D = x2.shape
    FW = H * HEAD_DIM
    CW = conv_w.shape[1]
    nsb = S // tm
    hb = tm // HALO
    mb = 4 if nsb % 4 == 0 else 1
    mt = mb * tm

    def body(oa_ref, gf_ref, ga_ref, gb_ref, gc_ref, ha_ref, hb_ref, x_ref, t_ref, w_ref, cb_ref, lg_ref,
             lb_ref, wo_ref, y_ref, u2_ref, a_ref, do_ref, dob_ref, dy_ref, loss_ref, ext_ref, sh_ref):
        first_step = (pl.program_id(0) == 0) & (pl.program_id(1) == 0)
        sub = lax.rem(pl.program_id(1), mb)
        rows = pl.ds(pl.multiple_of(sub * tm, tm), tm)

        @pl.when(first_step)
        def _():
            loss_ref[...] = jnp.zeros_like(loss_ref)

        u1 = ga_ref[...] * _sigmoid(gb_ref[...])
        halo = ha_ref[...] * _sigmoid(hb_ref[...])
        ext_ref[0:HALO, :] = jnp.where(pl.program_id(1) > 0, halo, 0.0)
        ext_ref[HALO:, :] = u1
        _fill_shifts(ext_ref, sh_ref)
        _conv_taps(w_ref, ext_ref, sh_ref, u2_ref, n_taps, tm, lambda j: HALO - (n_taps - 1) + j, cb_ref)
        uh, _ = _layernorm_stats(u2_ref[...])
        u3 = uh * lg_ref[...] + lb_ref[...]
        gc = gc_ref[...]
        yu = u3 * _sigmoid(u3) * (gc * _sigmoid(gc))
        y_ref[rows, FW:] = yu.astype(BF16)

        lane = _lane((tm, LANES))
        lo = lane < HEAD_DIM
        for p in range(H // 2):
            a_ref[:, p * LANES:(p + 1) * LANES] = jnp.where(
                lo, oa_ref[0, 2 * p], pltpu.roll(oa_ref[0, 2 * p + 1], HEAD_DIM, 1))
        gf = gf_ref[...]
        y_ref[rows, :FW] = (a_ref[...] * (gf * _sigmoid(gf))).astype(BF16)

        @pl.when(sub == mb - 1)
        def _():
            out = x_ref[...] + _dot(y_ref[...], wo_ref[...])
            diff = out - t_ref[...]
            loss_ref[...] = loss_ref[...] + jnp.sum(diff * diff)
            dout = diff * (1.0 / D)
            do_ref[...] = dout
            dob = dout.astype(BF16)
            dob_ref[...] = dob
            dy_ref[...] = _dot_nt(dob, wo_ref[...])

    row = lambda b, s: b * nsb + s
    zspec = lambda cb: pl.BlockSpec((tm, FW), lambda b, s: (row(b, s), cb))
    hspec = lambda cb: pl.BlockSpec((HALO, CW), lambda b, s: (jnp.maximum(row(b, s) * hb - 1, 0), cb))
    vspec = pl.BlockSpec((1, CW), lambda b, s: (0, 0))
    tspec = lambda w: pl.BlockSpec((tm, w), lambda b, s: (row(b, s), 0))
    mspec = lambda w: pl.BlockSpec((mt, w), lambda b, s: (row(b, s) // mb, 0))
    return pl.pallas_call(
        body, name="fwd_out", grid=(B, nsb),
        in_specs=[pl.BlockSpec((1, H, tm, LANES), lambda b, s: (b, 0, s, 0)),
                  zspec(3), zspec(4), zspec(5), zspec(6), hspec(4), hspec(5),
                  mspec(D), mspec(D),
                  pl.BlockSpec((HALO, CW), lambda b, s: (0, 0)), vspec, vspec, vspec,
                  pl.BlockSpec((FW + CW, D), lambda b, s: (0, 0))],
        out_specs=[mspec(FW + CW), tspec(CW), tspec(FW), mspec(D), mspec(D), mspec(FW + CW),
                   pl.BlockSpec((8, LANES), lambda b, s: (0, 0))],
        out_shape=[SDS((T, FW + CW), BF16), SDS((T, CW), F32), SDS((T, FW), F32), SDS((T, D), F32),
                   SDS((T, D), BF16), SDS((T, FW + CW), F32), SDS((8, LANES), F32)],
        scratch_shapes=[pltpu.VMEM((tm + HALO, CW), F32),
                        pltpu.VMEM((SUBLANES - 1, tm + HALO - SUBLANES, CW), F32)],
        compiler_params=_params(("arbitrary", "arbitrary")),
    )(oa, z, z, z, z, z, z, x2, tgt, conv_w, conv_b, ln_g, ln_b, w_out)


def _bwd_prep(dy, z, a_nat, oa, qa, u2, ln_g, ln_b, B, S, H, tm):
    T = B * S
    FW = H * HEAD_DIM
    CW = u2.shape[1]
    nsb = S // tm

    def body(dya_ref, dyu_ref, gf_ref, gc_ref, a_ref, oa_ref, qa_ref, u2_ref, lg_ref, lb_ref,
             dzgf_ref, dzgc_ref, du2_ref, doa_ref, qb_ref, sg_ref):
        first_step = (pl.program_id(0) == 0) & (pl.program_id(1) == 0)

        @pl.when(first_step)
        def _():
            sg_ref[...] = jnp.zeros_like(sg_ref)

        gf = gf_ref[...]
        sg = _sigmoid(gf)
        a = a_ref[...]
        dya = dya_ref[...]
        da = dya * (gf * sg)
        dzgf_ref[...] = (dya * a * (sg * (1.0 + gf * (1.0 - sg)))).astype(BF16)
        dd = da * a
        lane = _lane((tm, LANES))
        lo = lane < HEAD_DIM
        for p in range(H // 2):
            cols = slice(p * LANES, (p + 1) * LANES)
            da_p = da[:, cols]
            dd_p = dd[:, cols]
            d_heads = (jnp.sum(jnp.where(lo, dd_p, 0.0), axis=-1, keepdims=True),
                       jnp.sum(jnp.where(lo, 0.0, dd_p), axis=-1, keepdims=True))
            for e in range(2):
                da_e = da_p if e == 0 else pltpu.roll(da_p, HEAD_DIM, 1)
                d_e = d_heads[e]
                aug = _put3(jnp.zeros((tm, LANES), F32), lane, L_D, _split3(-d_e))
                doa_ref[0, 2 * p + e] = jnp.where(lo, da_e, aug).astype(BF16)
                lse = _lane_col(oa_ref[0, 2 * p + e], lane, L_ROWSUM)
                qb = _put3(qa_ref[0, 2 * p + e].astype(F32), lane, L_LSE, _split3(-lse))
                qb_ref[0, 2 * p + e] = qb.astype(BF16)

        gc = gc_ref[...]
        sc = _sigmoid(gc)
        dyu = dyu_ref[...]
        uh, rstd = _layernorm_stats(u2_ref[...])
        u3 = uh * lg_ref[...] + lb_ref[...]
        s3 = _sigmoid(u3)
        dzgc_ref[...] = (dyu * (u3 * s3) * (sc * (1.0 + gc * (1.0 - sc)))).astype(BF16)
        du3 = dyu * (gc * sc) * (s3 * (1.0 + u3 * (1.0 - s3)))
        sg_ref[0:1, :] = sg_ref[0:1, :] + jnp.sum(du3 * uh, axis=0, keepdims=True)
        sg_ref[1:2, :] = sg_ref[1:2, :] + jnp.sum(du3, axis=0, keepdims=True)
        duh = du3 * lg_ref[...]
        du2 = rstd * (duh - jnp.mean(duh, axis=-1, keepdims=True)
                      - uh * jnp.mean(duh * uh, axis=-1, keepdims=True))
        sg_ref[2:3, :] = sg_ref[2:3, :] + jnp.sum(du2, axis=0, keepdims=True)
        du2_ref[...] = du2

    row = lambda b, s: b * nsb + s
    tspec = lambda w, cb=0: pl.BlockSpec((tm, w), lambda b, s: (row(b, s), cb))
    hspec = pl.BlockSpec((1, H, tm, LANES), lambda b, s: (b, 0, s, 0))
    vspec = pl.BlockSpec((1, CW), lambda b, s: (0, 0))
    return pl.pallas_call(
        body, name="bwd_prep", grid=(B, nsb),
        in_specs=[tspec(FW, 0), tspec(CW, 1), tspec(FW, 3), tspec(CW, 6), tspec(FW), hspec, hspec,
                  tspec(CW), vspec, vspec],
        out_specs=[tspec(FW), tspec(CW), tspec(CW), hspec, hspec,
                   pl.BlockSpec((8, CW), lambda b, s: (0, 0))],
        out_shape=[SDS((T, FW), BF16), SDS((T, CW), BF16), SDS((T, CW), F32),
                   SDS((B, H, S, LANES), BF16), SDS((B, H, S, LANES), BF16), SDS((8, CW), F32)],
        compiler_params=_params(("arbitrary", "arbitrary")),
    )(dy, dy, z, z, a_nat, oa, qa, u2, ln_g, ln_b)


def _conv_bwd(du2, z, conv_w, B, S, n_taps, tm):
    T, CW = du2.shape
    nsb = S // tm
    hb = tm // HALO

    def body(d_ref, dh_ref, ga_ref, gb_ref, ha_ref, hb_ref, w_ref, dz_ref, dw_ref,
             extu_ref, extd_ref, shu_ref, shd_ref, du1_ref, dwacc_ref):
        s = pl.program_id(1)
        first_step = (pl.program_id(0) == 0) & (s == 0)
        last_step = (pl.program_id(0) == B - 1) & (s == nsb - 1)

        @pl.when(first_step)
        def _():
            dwacc_ref[...] = jnp.zeros_like(dwacc_ref)

        ga = ga_ref[...]
        sb = _sigmoid(gb_ref[...])
        halo = ha_ref[...] * _sigmoid(hb_ref[...])
        extu_ref[0:HALO, :] = jnp.where(s > 0, halo, 0.0)
        extu_ref[HALO:, :] = ga * sb
        extd_ref[0:tm, :] = d_ref[...]
        extd_ref[tm:, :] = jnp.where(s < nsb - 1, dh_ref[...], 0.0)
        _fill_shifts(extu_ref, shu_ref)
        _fill_shifts(extd_ref, shd_ref)
        _conv_taps(w_ref, extd_ref, shd_ref, du1_ref, n_taps, tm, lambda j: n_taps - 1 - j)
        for cc in range(CW // LANES):
            cols = slice(cc * LANES, (cc + 1) * LANES)
            parts = [None] * n_taps
            for r in range(tm // SUBLANES):
                dv = d_ref[r * SUBLANES:(r + 1) * SUBLANES, cols]
                for j in range(n_taps):
                    off = HALO - (n_taps - 1) + j + r * SUBLANES
                    term = dv * _tap_window(extu_ref, shu_ref, off, SUBLANES, cols)
                    parts[j] = term if parts[j] is None else parts[j] + term
            for j in range(n_taps):
                rows = slice(j * SUBLANES, (j + 1) * SUBLANES)
                dwacc_ref[rows, cols] = dwacc_ref[rows, cols] + parts[j]
        du1 = du1_ref[...]
        dz_ref[:, :CW] = (du1 * sb).astype(BF16)
        dz_ref[:, CW:] = (du1 * ga * (sb * (1.0 - sb))).astype(BF16)

        @pl.when(last_step)
        def _():
            dw_ref[...] = jnp.zeros_like(dw_ref)
            for j in range(n_taps):
                dw_ref[j:j + 1, :] = jnp.sum(dwacc_ref[j * SUBLANES:(j + 1) * SUBLANES, :], axis=0, keepdims=True)

    row = lambda b, s: b * nsb + s
    last_halo = T // HALO - 1
    return pl.pallas_call(
        body, name="conv_bwd", grid=(B, nsb),
        in_specs=[pl.BlockSpec((tm, CW), lambda b, s: (row(b, s), 0)),
                  pl.BlockSpec((HALO, CW), lambda b, s: (jnp.minimum((row(b, s) + 1) * hb, last_halo), 0)),
                  pl.BlockSpec((tm, CW), lambda b, s: (row(b, s), 4)),
                  pl.BlockSpec((tm, CW), lambda b, s: (row(b, s), 5)),
                  pl.BlockSpec((HALO, CW), lambda b, s: (jnp.maximum(row(b, s) * hb - 1, 0), 4)),
                  pl.BlockSpec((HALO, CW), lambda b, s: (jnp.maximum(row(b, s) * hb - 1, 0), 5)),
                  pl.BlockSpec((HALO, CW), lambda b, s: (0, 0))],
        out_specs=[pl.BlockSpec((tm, 2 * CW), lambda b, s: (row(b, s), 0)),
                   pl.BlockSpec((HALO, CW), lambda b, s: (0, 0))],
        out_shape=[SDS((T, 2 * CW), BF16), SDS((HALO, CW), F32)],
        scratch_shapes=[pltpu.VMEM((tm + HALO, CW), F32), pltpu.VMEM((tm + HALO, CW), F32),
                        pltpu.VMEM((SUBLANES - 1, tm + HALO - SUBLANES, CW), F32),
                        pltpu.VMEM((SUBLANES - 1, tm + HALO - SUBLANES, CW), F32),
                        pltpu.VMEM((tm, CW), F32), pltpu.VMEM((HALO * SUBLANES, CW), F32)],
        compiler_params=_params(("arbitrary", "arbitrary")),
    )(du2, du2, z, z, z, z, conv_w)


def _attn_bwd(qb, ka, va, doa, t, hb):
    B, H, S, _ = qb.shape
    nk = S // t

    def body(q_ref, k_ref, v_ref, do_ref, dq_ref, dk_ref, dv_ref, dv_acc):
        j = pl.program_id(2)

        @pl.when(j == 0)
        def _():
            dq_ref[...] = jnp.zeros_like(dq_ref)

        dk_ref[...] = jnp.zeros_like(dk_ref)
        dv_acc[...] = jnp.zeros_like(dv_acc)

        def step(i, masked):
            q_rows = pl.ds(pl.multiple_of(i * t, t), t)
            for e in range(hb):
                k = k_ref[0, e]
                q = q_ref[0, e, q_rows, :]
                do = do_ref[0, e, q_rows, :]
                p = jnp.exp(_dot_nt(q, k))
                if masked:
                    keep = lax.broadcasted_iota(jnp.int32, (t, t), 0) >= lax.broadcasted_iota(jnp.int32, (t, t), 1)
                    p = jnp.where(keep, p, 0.0)
                ds = (p * _dot_nt(do, v_ref[0, e])).astype(BF16)
                dv_acc[e] = dv_acc[e] + _dot_tn(p.astype(BF16), do)
                dk_ref[0, e] = dk_ref[0, e] + _dot_tn(ds, q)
                dq_ref[0, e, q_rows, :] = dq_ref[0, e, q_rows, :] + _dot(ds, k)

        step(j, True)

        def loop_body(i, carry):
            step(i, False)
            return carry

        lax.fori_loop(j + 1, nk, loop_body, 0)
        dv_ref[0] = dv_acc[...].astype(BF16)

    full = pl.BlockSpec((1, hb, S, LANES), lambda b, h, j: (b, h, 0, 0))
    blk = pl.BlockSpec((1, hb, t, LANES), lambda b, h, j: (b, h, j, 0))
    oshape = SDS((B, H, S, LANES), F32)
    return pl.pallas_call(
        body, name="attn_bwd", grid=(B, H // hb, nk),
        in_specs=[full, blk, blk, full],
        out_specs=[full, blk, blk],
        out_shape=[oshape, oshape, SDS((B, H, S, LANES), BF16)],
        scratch_shapes=[pltpu.VMEM((hb, t, LANES), F32)],
        compiler_params=_params(("parallel", "parallel", "arbitrary")),
    )(qb, ka, va, doa)


def _qk_bwd(dqa, dka, dva, z, gq, gk, B, S, H, tm):
    T = B * S
    FW = H * HEAD_DIM
    nsb = S // tm
    nfb = FW // LANES
    scale = HEAD_DIM ** -0.5

    def body(dq_ref, dk_ref, dv_ref, zq_ref, zk_ref, gq_ref, gk_ref, dzq_ref, dzk_ref, dzv_ref, dc_ref, dg_ref):
        p = pl.program_id(0)

        @pl.when(pl.program_id(1) == 0)
        def _():
            dg_ref[...] = jnp.zeros_like(dg_ref)

        lane = _lane((tm, LANES))
        lo = lane < HEAD_DIM

        def natural(ref):
            return jnp.where(lo, ref[0, 0].astype(F32), pltpu.roll(ref[0, 1].astype(F32), HEAD_DIM, 1))

        def norm_bwd(dn, x, g, row, out_ref):
            r = lax.rsqrt(_half_stats(x * x) * (1.0 / HEAD_DIM) + EPS)
            xh = x * r
            dg_ref[row:row + 1, :] = dg_ref[row:row + 1, :] + jnp.sum(dn * xh, axis=0, keepdims=True)
            dxh = dn * g
            mm = _half_stats(dxh * xh) * (1.0 / HEAD_DIM)
            out_ref[...] = (r * (dxh - xh * mm)).astype(BF16)

        norm_bwd(natural(dq_ref) * scale, zq_ref[...], gq_ref[...], 0, dzq_ref)
        norm_bwd(natural(dk_ref), zk_ref[...], gk_ref[...], 1, dzk_ref)
        dzv_ref[...] = natural(dv_ref).astype(BF16)

        dc = jnp.zeros((tm, LANES), F32)
        for e in range(2):
            val = _lane_col(dq_ref[0, e], lane, L_ROWSUM) - _lane_col(dk_ref[0, e], lane, L_KDECAY)
            dc = jnp.where(lane == 2 * p + e, val, dc)
        dc_ref[0] = dc

    hspec = pl.BlockSpec((1, 2, tm, LANES), lambda p, i: (i // nsb, p, i % nsb, 0))
    zspec = lambda off: pl.BlockSpec((tm, LANES), lambda p, i: (i, off + p))
    gspec = pl.BlockSpec((1, LANES), lambda p, i: (0, p))
    ospec = pl.BlockSpec((tm, LANES), lambda p, i: (i, p))
    return pl.pallas_call(
        body, name="qk_bwd", grid=(H // 2, T // tm),
        in_specs=[hspec, hspec, hspec, zspec(0), zspec(nfb), gspec, gspec],
        out_specs=[ospec, ospec, ospec,
                   pl.BlockSpec((1, tm, LANES), lambda p, i: (p, i, 0)),
                   pl.BlockSpec((8, LANES), lambda p, i: (0, p))],
        out_shape=[SDS((T, FW), BF16), SDS((T, FW), BF16), SDS((T, FW), BF16),
                   SDS((H // 2, T, LANES), F32), SDS((8, FW), F32)],
        compiler_params=_params(("parallel", "arbitrary")),
    )(dqa, dka, dva, z, z, gq, gk)


def _gate_bwd(dc8, z, b_pad, B, S, H, col_blk, fp, tc):
    T = B * S
    nsb = S // tc
    npair = dc8.shape[0]

    def body(dc_ref, zf_ref, b_ref, dz_ref, db_ref, carry):
        first_step = (pl.program_id(0) == 0) & (pl.program_id(1) == 0)

        @pl.when(first_step)
        def _():
            db_ref[...] = jnp.zeros_like(db_ref)

        @pl.when(pl.program_id(1) == 0)
        def _():
            carry[...] = jnp.zeros_like(carry)

        dc = dc_ref[0]
        for k in range(1, npair):
            dc = dc + dc_ref[k]
        dlf = _tri_cumsum(dc, True) + carry[...]
        carry[...] = carry[...] + jnp.sum(dc, axis=0, keepdims=True)
        x = zf_ref[...] + b_ref[...]
        dlogit = dlf * _sigmoid(-x)
        db_ref[0:1, :] = db_ref[0:1, :] + jnp.sum(dlogit, axis=0, keepdims=True)
        dz_ref[...] = jnp.zeros_like(dz_ref)
        dz_ref[:, :LANES] = dlogit.astype(BF16)

    rrow = lambda b, s: b * nsb + (nsb - 1 - s)
    return pl.pallas_call(
        body, name="gate_bwd", grid=(B, nsb),
        in_specs=[pl.BlockSpec((npair, tc, LANES), lambda b, s: (0, rrow(b, s), 0)),
                  pl.BlockSpec((tc, LANES), lambda b, s: (rrow(b, s), col_blk)),
                  pl.BlockSpec((1, LANES), lambda b, s: (0, 0))],
        out_specs=[pl.BlockSpec((tc, fp), lambda b, s: (rrow(b, s), 0)),
                   pl.BlockSpec((8, LANES), lambda b, s: (0, 0))],
        out_shape=[SDS((T, fp), BF16), SDS((8, LANES), F32)],
        scratch_shapes=[pltpu.VMEM((1, LANES), F32)],
        compiler_params=_params(("arbitrary", "arbitrary")),
    )(dc8, z, b_pad)


def _matmul_tn(a, b, name, tmm, tn, tk):
    T, M = a.shape
    N = b.shape[1]
    tmm, tn, tk = min(tmm, M), min(tn, N), min(tk, T)

    def body(a_ref, b_ref, o_ref):
        @pl.when(pl.program_id(2) == 0)
        def _():
            o_ref[...] = jnp.zeros_like(o_ref)

        o_ref[...] = o_ref[...] + _dot_tn(a_ref[...], b_ref[...])

    return pl.pallas_call(
        body, name=name, grid=(M // tmm, N // tn, T // tk),
        in_specs=[pl.BlockSpec((tk, tmm), lambda i, j, k: (k, i)),
                  pl.BlockSpec((tk, tn), lambda i, j, k: (k, j))],
        out_specs=pl.BlockSpec((tmm, tn), lambda i, j, k: (i, j)),
        out_shape=SDS((M, N), F32),
        compiler_params=_params(("parallel", "parallel", "arbitrary")),
    )(a, b)


def _dh_rms_bwd(pieces, w_t, x2, g, dout, tm, tk, parts):
    T, D = x2.shape
    nks = [p.shape[1] // tk for p in pieces]
    starts = [sum(nks[:k]) for k in range(len(pieces))]
    nk = sum(nks)
    ni = T // tm
    n = len(parts)

    def body(*refs):
        dz_refs = refs[:len(pieces)]
        w_ref, x_ref, g_ref, do_ref = refs[len(pieces):len(pieces) + 4]
        part_refs = refs[len(pieces) + 4:len(pieces) + 4 + n]
        gx_ref, dg_ref = refs[len(pieces) + 4 + n:len(pieces) + 6 + n]
        slot_refs = refs[len(pieces) + 6 + n:len(pieces) + 6 + 2 * n]
        acc_ref, send_sems, recv_sems = refs[len(pieces) + 6 + 2 * n:]
        k = pl.program_id(1)
        first_step = (pl.program_id(0) == 0) & (k == 0)
        last_step = (pl.program_id(0) == ni - 1) & (k == nk - 1)
        x, y, c = _place()
        chips = [(1 - x, y), (x, 1 - y), (1 - x, 1 - y)]

        def copy(a, f, to):
            cx, cy = chips[f]
            return pltpu.make_async_remote_copy(
                src_ref=part_refs[a].at[2 * cx + cy], dst_ref=slot_refs[a].at[f],
                send_sem=send_sems.at[a * 3 + f], recv_sem=recv_sems.at[a * 3 + f],
                device_id=to, device_id_type=MESH)

        @pl.when(first_step)
        def _():
            dg_ref[...] = jnp.zeros_like(dg_ref)
            for a in range(n):
                for f in range(3):
                    copy(a, f, (*chips[f], c)).start()

        @pl.when(last_step)
        def _():
            for a in range(n):
                for f in range(3):
                    copy(a, f, (x, y, c)).wait_recv()
            for a in range(n):
                for f in range(3):
                    copy(a, f, (*chips[f], c)).wait_send()

        @pl.when(k == 0)
        def _():
            acc_ref[...] = jnp.zeros_like(acc_ref)

        for dz_ref, st, cnt in zip(dz_refs, starts, nks):
            @pl.when((k >= st) & (k < st + cnt))
            def _(dz_ref=dz_ref):
                acc_ref[...] = acc_ref[...] + _dot(dz_ref[...], w_ref[...])

        @pl.when(k == nk - 1)
        def _():
            x = x_ref[...]
            r = lax.rsqrt(jnp.mean(x * x, axis=-1, keepdims=True) + EPS)
            xh = x * r
            dh = acc_ref[...]
            dg_ref[0:1, :] = dg_ref[0:1, :] + jnp.sum(dh * xh, axis=0, keepdims=True)
            dxn = dh * g_ref[...]
            gx_ref[...] = do_ref[...] + r * (dxn - xh * jnp.mean(dxn * xh, axis=-1, keepdims=True))

    def piece_spec(st, cnt):
        return pl.BlockSpec((tm, tk), lambda i, k: (i, jnp.clip(k - st, 0, cnt - 1)))

    tspec = pl.BlockSpec((tm, D), lambda i, k: (i, 0))
    return pl.pallas_call(
        body, name="dh_rms_bwd", grid=(T // tm, nk),
        in_specs=[piece_spec(st, cnt) for st, cnt in zip(starts, nks)]
        + [pl.BlockSpec((tk, D), lambda i, k: (k, 0)), tspec, pl.BlockSpec((1, D), lambda i, k: (0, 0)), tspec]
        + [ANY] * n,
        out_specs=[tspec, pl.BlockSpec((8, D), lambda i, k: (0, 0))] + [ANY] * n,
        out_shape=[SDS((T, D), F32), SDS((8, D), F32)] + [SDS((3,) + p.shape[1:], p.dtype) for p in parts],
        scratch_shapes=[pltpu.VMEM((tm, D), F32),
                        pltpu.SemaphoreType.DMA((3 * n,)), pltpu.SemaphoreType.DMA((3 * n,))],
        compiler_params=_params(("arbitrary", "arbitrary")),
    )(*pieces, w_t, x2, g, dout, *parts)


def _block_plan(R, C, tr, tc):
    br = min(tr, R)
    if R % br == 0:
        return (br, C), R // br, lambda i: (i, 0)
    bc = min(tc, C)
    assert C % bc == 0
    return (R, bc), C // bc, lambda i: (0, i)


def _ew_call(body, name, ins, n_out, out_dtypes, tr, tc):
    R, C = ins[0].shape
    blk, steps, imap = _block_plan(R, C, tr, tc)
    spec = pl.BlockSpec(blk, imap)
    return pl.pallas_call(
        body, name=name, grid=(steps,),
        in_specs=[spec] * len(ins), out_specs=[spec] * n_out,
        out_shape=[SDS((R, C), dt) for dt in out_dtypes],
        compiler_params=_params(("parallel",)),
    )(*ins)


def _sum_slots(slots, name, first=None, tr=256):
    n, R, C = slots.shape
    blk, steps, imap = _block_plan(R, C, tr, 2 * LANES)
    lead = [] if first is None else [first]

    def body(*refs):
        s_ref, o_ref = refs[-2:]
        acc = refs[0][...].astype(F32) if lead else s_ref[0].astype(F32)
        for k in range(0 if lead else 1, n):
            acc = acc + s_ref[k].astype(F32)
        o_ref[...] = acc

    return pl.pallas_call(
        body, name=name, grid=(steps,),
        in_specs=[pl.BlockSpec(blk, imap)] * len(lead) + [pl.BlockSpec((n,) + blk, lambda i: (0,) + imap(i))],
        out_specs=pl.BlockSpec(blk, imap),
        out_shape=SDS((R, C), F32),
        compiler_params=_params(("parallel",)),
    )(*lead, slots)


def _adamw_update(w, g, m, v):
    nm = ADAM_B1 * m + (1.0 - ADAM_B1) * g
    nv = ADAM_B2 * v + (1.0 - ADAM_B2) * (g * g)
    m_hat = nm / (1.0 - ADAM_B1 ** ADAM_STEP)
    v_hat = nv / (1.0 - ADAM_B2 ** ADAM_STEP)
    return -ADAM_LR * (m_hat / (jnp.sqrt(v_hat) + ADAM_EPS) + ADAM_WD * w), nm, nv


def _adamw(w, g, m, v, name):
    def body(w_ref, g_ref, m_ref, v_ref, d_ref, nm_ref, nv_ref):
        d_ref[...], nm_ref[...], nv_ref[...] = _adamw_update(w_ref[...], g_ref[...], m_ref[...], v_ref[...])

    return _ew_call(body, name, [w, g, m, v], 3, [F32, F32, F32], 128, 2 * LANES)


def _adamw_halves(w, mine, other, m, v, name):
    R, C = w.shape
    half = C // 2
    bc = min(2 * LANES, half)
    per = half // bc

    def body(w_ref, a_ref, b_ref, m_ref, v_ref, g_ref, d_ref, nm_ref, nv_ref):
        g = jnp.where(pl.program_id(0) // per == lax.axis_index("c"), a_ref[...], b_ref[...])
        g_ref[...] = g
        d_ref[...], nm_ref[...], nv_ref[...] = _adamw_update(w_ref[...], g, m_ref[...], v_ref[...])

    full = pl.BlockSpec((R, bc), lambda i: (0, i))
    part = pl.BlockSpec((R, bc), lambda i: (0, i % per))
    return pl.pallas_call(
        body, name=name, grid=(C // bc,),
        in_specs=[full, part, part, full, full], out_specs=[full] * 4,
        out_shape=[SDS((R, C), F32)] * 4,
        compiler_params=_params(("parallel",)),
    )(w, mine, other, m, v)


ANY = pl.BlockSpec(memory_space=pl.ANY)


def _place():
    return lax.axis_index("x"), lax.axis_index("y"), lax.axis_index("c")


def _gather_chips(shards, splits):
    n = len(shards)
    per = 7

    def body(*refs):
        ins, outs = refs[:n], refs[n:2 * n]
        send_sems, recv_sems = refs[2 * n:]
        x, y, c = _place()
        mine = 2 * x + y
        chips = [(1 - x, y), (x, 1 - y), (1 - x, 1 - y)]

        def rows(a, half):
            return pl.ds(0, splits[a]) if half == 0 else pl.ds(splits[a], ins[a].shape[0] - splits[a])

        def copy(a, k, chip_idx, half, to, src=None):
            dst = outs[a].at[chip_idx, rows(a, half)]
            return pltpu.make_async_remote_copy(
                src_ref=dst if src is None else src, dst_ref=dst,
                send_sem=send_sems.at[a * per + k], recv_sem=recv_sems.at[a * per + k],
                device_id=to, device_id_type=MESH)

        def own(a, to):
            return pltpu.make_async_remote_copy(
                src_ref=ins[a], dst_ref=outs[a].at[mine],
                send_sem=send_sems.at[a * per + 6], recv_sem=recv_sems.at[a * per + 6],
                device_id=to, device_id_type=MESH)

        for cc in (0, 1):
            @pl.when(c == cc)
            def _(cc=cc):
                me, sibling = (x, y, cc), (x, y, 1 - cc)
                first = [copy(a, k, mine, cc, (*chip, cc), src=ins[a].at[rows(a, cc)])
                         for a in range(n) for k, chip in enumerate(chips)]
                first += [own(a, sibling) for a in range(n)]
                for cp in first:
                    cp.start()
                passed = []
                for k, (cx, cy) in enumerate(chips):
                    for a in range(n):
                        copy(a, k, 2 * cx + cy, cc, me).wait_recv()
                        fwd = copy(a, 3 + k, 2 * cx + cy, cc, sibling)
                        fwd.start()
                        passed.append(fwd)
                for k, (cx, cy) in enumerate(chips):
                    for a in range(n):
                        copy(a, 3 + k, 2 * cx + cy, 1 - cc, me).wait_recv()
                for a in range(n):
                    own(a, me).wait_recv()
                for cp in first + passed:
                    cp.wait_send()

    return pl.pallas_call(
        body, name="gather_chips",
        in_specs=[ANY] * n, out_specs=[ANY] * n,
        out_shape=[SDS((4,) + s.shape, s.dtype) for s in shards],
        scratch_shapes=[pltpu.SemaphoreType.DMA((per * n,)), pltpu.SemaphoreType.DMA((per * n,))],
    )(*shards)


def _pair_swap(arrs):
    n = len(arrs)

    def body(*refs):
        ins, outs = refs[:n], refs[n:2 * n]
        send_sems, recv_sems = refs[2 * n:]
        x, y, c = _place()
        for cc in (0, 1):
            @pl.when(c == cc)
            def _(cc=cc):
                copies = []
                for a in range(n):
                    half = ins[a].shape[1] // 2
                    copies.append(pltpu.make_async_remote_copy(
                        src_ref=ins[a].at[:, pl.ds((1 - cc) * half, half)], dst_ref=outs[a],
                        send_sem=send_sems.at[a], recv_sem=recv_sems.at[a],
                        device_id=(x, y, 1 - cc), device_id_type=MESH))
                for cp in copies:
                    cp.start()
                for cp in copies:
                    cp.wait()

    return pl.pallas_call(
        body, name="pair_swap",
        in_specs=[ANY] * n, out_specs=[ANY] * n,
        out_shape=[SDS((h.shape[0], h.shape[1] // 2), h.dtype) for h in arrs],
        scratch_shapes=[pltpu.SemaphoreType.DMA((n,)), pltpu.SemaphoreType.DMA((n,))],
    )(*arrs)


def _pair_sum(arrs, got, core, tr):
    half = arrs[0].shape[1] // 2
    cnts = [p.shape[0] // tr for p in arrs]
    starts = [sum(cnts[:k]) for k in range(len(arrs))]

    def body(core_ref, *refs):
        del core_ref
        own_refs, got_refs, o_ref = refs[:len(arrs)], refs[len(arrs):2 * len(arrs)], refs[-1]
        s = pl.program_id(0)
        for own_ref, got_ref, st, cnt in zip(own_refs, got_refs, starts, cnts):
            @pl.when((s >= st) & (s < st + cnt))
            def _(own_ref=own_ref, got_ref=got_ref):
                o_ref[...] = (own_ref[...] + got_ref[...]).astype(BF16)

    def own_spec(st, cnt):
        return pl.BlockSpec((tr, half), lambda s, core_ref: (jnp.clip(s - st, 0, cnt - 1), core_ref[0]))

    def got_spec(st, cnt):
        return pl.BlockSpec((tr, half), lambda s, core_ref: (jnp.clip(s - st, 0, cnt - 1), 0))

    return pl.pallas_call(
        body, name="pair_sum",
        grid_spec=pltpu.PrefetchScalarGridSpec(
            num_scalar_prefetch=1, grid=(sum(cnts),),
            in_specs=[own_spec(st, cnt) for st, cnt in zip(starts, cnts)]
            + [got_spec(st, cnt) for st, cnt in zip(starts, cnts)],
            out_specs=pl.BlockSpec((tr, half), lambda s, core_ref: (s, 0))),
        out_shape=SDS((sum(cnts) * tr, half), BF16),
        compiler_params=_params(("arbitrary",)),
    )(core, *arrs, *got)


def _share_results(arrs, rows):
    n = len(arrs)
    flips = [(fx, fy, fc) for fx in (0, 1) for fy in (0, 1) for fc in (0, 1)][1:]

    def body(*refs):
        ins, rows_ref, outs, all_ref = refs[:n], refs[n], refs[n + 1:2 * n + 1], refs[2 * n + 1]
        send_sems, recv_sems, local_sem = refs[2 * n + 2:]
        x, y, c = _place()
        me = 4 * x + 2 * y + c
        local = pltpu.make_async_copy(rows_ref, all_ref.at[me], local_sem)
        local.start()
        copies = [pltpu.make_async_remote_copy(
            src_ref=ins[a], dst_ref=outs[a], send_sem=send_sems.at[a], recv_sem=recv_sems.at[a],
            device_id=(x, y, 1 - c), device_id_type=MESH) for a in range(n)]
        for k, (fx, fy, fc) in enumerate(flips):
            copies.append(pltpu.make_async_remote_copy(
                src_ref=rows_ref, dst_ref=all_ref.at[me], send_sem=send_sems.at[n + k], recv_sem=recv_sems.at[n + k],
                device_id=(x ^ fx, y ^ fy, c ^ fc), device_id_type=MESH))
        for cp in copies:
            cp.start()
        for cp in copies[:n]:
            cp.wait_recv()
        for k, (fx, fy, fc) in enumerate(flips):
            src = 4 * (x ^ fx) + 2 * (y ^ fy) + (c ^ fc)
            pltpu.make_async_remote_copy(
                src_ref=rows_ref, dst_ref=all_ref.at[src], send_sem=send_sems.at[n + k], recv_sem=recv_sems.at[n + k],
                device_id=(x, y, c), device_id_type=MESH).wait_recv()
        for cp in copies:
            cp.wait_send()
        local.wait()

    outs = pl.pallas_call(
        body, name="share_results",
        in_specs=[ANY] * (n + 1), out_specs=[ANY] * (n + 1),
        out_shape=[SDS(h.shape, h.dtype) for h in arrs] + [SDS((8,) + rows.shape, rows.dtype)],
        scratch_shapes=[pltpu.SemaphoreType.DMA((n + 7,)), pltpu.SemaphoreType.DMA((n + 7,)),
                        pltpu.SemaphoreType.DMA],
    )(*arrs, rows)
    return outs[:n], outs[n]


def _tiles(S, FW):
    big = FW % 512 == 0
    return dict(
        fp=512 if big else LANES,
        tn=1536 if big else LANES,
        tm_in=min(1024, S),
        t_attn=min(512, S),
        hb_fwd=8,
        hb_bwd=8,
        tm_prep=min(2048, S),
        tm_mix=min(128, S),
        tm_bwd=min(256, S),
        tc=min(512, S),
        tk=512 if big else LANES,
    )


def kernel(x, norm_g, w_in, b_forget, q_norm_g, k_norm_g, conv_w, conv_b, conv_ln_g, conv_ln_b, w_out, loss_target, m_norm_g, m_w_in, m_b_forget, m_q_norm_g, m_k_norm_g, m_conv_w, m_conv_b, m_conv_ln_g, m_conv_ln_b, m_w_out, v_norm_g, v_w_in, v_b_forget, v_q_norm_g, v_k_norm_g, v_conv_w, v_conv_b, v_conv_ln_g, v_conv_ln_b, v_w_out):
    B, S, D = x.shape
    H, dh = q_norm_g.shape[1:]
    FW = H * dh
    CW = conv_b.shape[-1]
    n_taps, cw_shard = conv_w.shape[1:]
    in_shard = w_in.shape[2]
    out_shard = w_out.shape[1]
    assert dh == HEAD_DIM and H % 2 == 0 and H <= LANES and FW == CW == D
    assert n_taps - 1 <= HALO and 4 * cw_shard == CW and 4 * out_shard == FW + CW
    assert 4 * in_shard == 4 * FW + 3 * CW + H
    T = B * S
    tl = _tiles(S, FW)
    fp = tl["fp"]
    xi, yi, ci = _place()

    w_t = jnp.transpose(w_in[0])
    conv_pad = jnp.pad(conv_w[0], ((0, HALO - n_taps), (0, 0)))
    bf16_rows = 2 * SUBLANES
    g_in, g_out, g_cw = _gather_chips(
        [w_t.astype(BF16), w_out[0].astype(BF16), conv_pad],
        [in_shard // 2 // bf16_rows * bf16_rows, out_shard // 2, HALO // 2])
    w_t_full = g_in.reshape(4 * in_shard, D)
    w_out_full = g_out.reshape(FW + CW, D)
    conv_full = g_cw.transpose(1, 0, 2).reshape(HALO, CW)
    o_f = 3 * FW
    w_pack = jnp.concatenate([w_t_full[:o_f], w_t_full[o_f + H:],
                              jnp.pad(w_t_full[o_f:o_f + H], ((0, fp - H), (0, 0)))], axis=0)
    f_col = 4 * FW + 3 * CW

    x2 = x.reshape(T, D)
    tgt = loss_target.reshape(T, D)
    b_pad = jnp.pad(b_forget, ((0, 0), (0, LANES - H)))
    gq = q_norm_g.reshape(1, FW)
    gk = k_norm_g.reshape(1, FW)

    z, h = _fwd_in(x2, norm_g, w_pack, tl["tm_in"], tl["tn"])
    c = _gate_fwd(z, b_pad, B, S, H, f_col // LANES, tl["tc"])
    qa, ka, va = _attn_prep(z, c, gq, gk, B, S, H, tl["tm_prep"])
    gain = lambda g: jnp.max(jnp.abs(g[0]), axis=-1)
    bound = (NORM_SLACK ** 2 * dh ** 0.5) * gain(q_norm_g) * gain(k_norm_g)
    oa = _attn_fwd(qa, ka, va, jnp.broadcast_to(bound[:, None, None], (H, 1, LANES)), tl["t_attn"], tl["hb_fwd"])
    y, u2, a_nat, dout, dout_b, dy, loss_acc = _fwd_out(
        oa, z, x2, tgt, conv_full, conv_b, conv_ln_g, conv_ln_b, w_out_full, B, S, H, n_taps, tl["tm_mix"])

    dzgf, dzgc, du2, doa, qb, sg_conv = _bwd_prep(dy, z, a_nat, oa, qa, u2, conv_ln_g, conv_ln_b, B, S, H, tl["tm_bwd"])
    dzglu, dconv_w = _conv_bwd(du2, z, conv_full, B, S, n_taps, tl["tm_bwd"])
    dqa, dka, dva = _attn_bwd(qb, ka, va, doa, tl["t_attn"], tl["hb_bwd"])
    dzq, dzk, dzv, dc8, dg_qk = _qk_bwd(dqa, dka, dva, z, gq, gk, B, S, H, tl["tm_prep"])
    dzf, db_f = _gate_bwd(dc8, z, b_pad, B, S, H, f_col // LANES, fp, tl["tc"])
    pieces = [dzq, dzk, dzv, dzgf, dzglu, dzgc, dzf]
    dw_all = [_matmul_tn(p, h, f"dw_in_{k}", 1024, 1024, 1024) for k, p in enumerate(pieces)]
    dw_all.append(_matmul_tn(y, dout_b, "dw_out", 1024, 1024, 1024))

    summed = _pair_sum(dw_all, _pair_swap(dw_all), ci.astype(jnp.int32).reshape(1), fp)
    ends = [0]
    for t in dw_all:
        ends.append(ends[-1] + t.shape[0])
    spans, at = [], 0
    for k, rows in [(0, FW), (1, FW), (2, FW), (6, H), (3, FW), (4, 2 * CW), (5, CW)]:
        spans.append((at, rows, ends[k]))
        at += rows

    def chip_rows(j):
        lo, hi = j * in_shard, (j + 1) * in_shard
        return jnp.concatenate([summed[src + max(lo, a) - a:src + min(hi, a + n) - a]
                                for a, n, src in spans if max(lo, a) < min(hi, a + n)], axis=0)

    part_in = jnp.stack([chip_rows(j) for j in range(4)])
    part_out = summed[ends[7]:ends[8]].reshape(4, out_shard, D // 2)
    grad_x2, dg_norm, slots_in, slots_out = _dh_rms_bwd(
        pieces, w_pack, x2, norm_g, dout, tl["tm_in"], tl["tk"], [part_in, part_out])
    chip = 2 * xi + yi
    half_in = _sum_slots(slots_in, "chip_sum_in", lax.dynamic_index_in_dim(part_in, chip, 0, keepdims=False))
    half_out = _sum_slots(slots_out, "chip_sum_out", lax.dynamic_index_in_dim(part_out, chip, 0, keepdims=False))
    lanes_to_d = lambda t: jnp.pad(t, ((0, 0), (0, D - LANES)))
    small = jnp.concatenate([
        dg_norm[0:1], lanes_to_d(db_f[0:1, :]), dg_qk[0:1], dg_qk[1:2],
        sg_conv[2:3], sg_conv[0:1], sg_conv[1:2], dconv_w, lanes_to_d(loss_acc[0:1, :])], axis=0)
    n_small = small.shape[0]
    (other_in, other_out), all_small = _share_results([half_in, half_out], small)

    small_sum = _sum_slots(all_small, "small_sum", tr=n_small)
    loss = 0.5 * small_sum[n_small - 1, 0] / D
    grad_norm_g, grad_b_f = small_sum[0:1], small_sum[1:2, :H]
    grad_gq, grad_gk = small_sum[2:3].reshape(1, H, dh), small_sum[3:4].reshape(1, H, dh)
    grad_conv_b, grad_ln_g, grad_ln_b = small_sum[4:5], small_sum[5:6], small_sum[6:7]
    grad_conv_w = lax.dynamic_slice_in_dim(small_sum[7:7 + n_taps], chip * cw_shard, cw_shard, axis=1)

    in_t = _adamw_halves(w_t, half_in, other_in, jnp.transpose(m_w_in[0]), jnp.transpose(v_w_in[0]), "adamw_in")
    grad_w_in, d_in, nm_in, nv_in = (jnp.transpose(t)[None] for t in in_t)
    grad_w_out, d_out, nm_out, nv_out = (
        t[None] for t in _adamw_halves(w_out[0], half_out, other_out, m_w_out[0], v_w_out[0], "adamw_out"))
    d_cw, nm_cw, nv_cw = (t[None] for t in _adamw(conv_w[0], grad_conv_w, m_conv_w[0], v_conv_w[0], "adamw_conv_w"))

    def rows(ws):
        return jnp.concatenate([jnp.pad(t.reshape(1, -1), ((0, 0), (0, D - t.size))) for t in ws], axis=0)

    small_w = [norm_g, b_forget, q_norm_g, k_norm_g, conv_b, conv_ln_g, conv_ln_b]
    small_m = [m_norm_g, m_b_forget, m_q_norm_g, m_k_norm_g, m_conv_b, m_conv_ln_g, m_conv_ln_b]
    small_v = [v_norm_g, v_b_forget, v_q_norm_g, v_k_norm_g, v_conv_b, v_conv_ln_g, v_conv_ln_b]
    d_s, nm_s, nv_s = _adamw(rows(small_w), small_sum[0:7], rows(small_m), rows(small_v), "adamw_small")

    def unpack(t):
        return [t[k:k + 1, :w.size].reshape(w.shape) for k, w in enumerate(small_w)]

    def order(s, in_, cw, out_):
        ng, bf, qg, kg, cb, lg, lb = s
        return [ng, in_, bf, qg, kg, cw, cb, lg, lb, out_]

    grads = [grad_norm_g, grad_w_in, grad_b_f, grad_gq, grad_gk, grad_conv_w[None],
             grad_conv_b, grad_ln_g, grad_ln_b, grad_w_out]
    return (loss, grad_x2.reshape(B, S, D), *grads,
            *order(unpack(d_s), d_in, d_cw, d_out),
            *order(unpack(nm_s), nm_in, nm_cw, nm_out),
            *order(unpack(nv_s), nv_in, nv_cw, nv_out))
```

```python
import jax
import jax.numpy as jnp
from jax import lax
from jax.experimental import pallas as pl
from jax.experimental.pallas import tpu as pltpu

F32 = jnp.float32
BF16 = jnp.bfloat16
SDS = jax.ShapeDtypeStruct
MESH = pl.DeviceIdType.MESH

EPS = 1e-6
NEG_INF = -1e30
LANES = 128
SUBLANES = 8
HEAD_DIM = 64
HALO = 32
VMEM_LIMIT = 56 * 1024 * 1024

L_ROWSUM = 64
L_KDECAY = 67
L_LSE = 70
L_D = 65
NORM_SLACK = 1.02
SHIFT_MAX = 40.0

ADAM_LR = 0.001
ADAM_B1 = 0.9
ADAM_B2 = 0.999
ADAM_EPS = 1e-08
ADAM_WD = 0.01
ADAM_STEP = 10


def _params(sem, vmem=VMEM_LIMIT):
    return pltpu.CompilerParams(dimension_semantics=sem, vmem_limit_bytes=vmem)


def _sigmoid(x):
    return 1.0 / (1.0 + jnp.exp(-x))


def _split3(x):
    hi = x.astype(BF16).astype(F32)
    r = x - hi
    mid = r.astype(BF16).astype(F32)
    lo = (r - mid).astype(BF16).astype(F32)
    return hi, mid, lo


def _dot(a, b):
    return jnp.dot(a, b, preferred_element_type=F32)


def _dot_nt(a, b):
    return lax.dot_general(a, b, (((1,), (1,)), ((), ())), preferred_element_type=F32)


def _dot_tn(a, b):
    return lax.dot_general(a, b, (((0,), (0,)), ((), ())), preferred_element_type=F32)


def _lane(shape):
    return lax.broadcasted_iota(jnp.int32, shape, 1)


def _lane_col(x, lane, idx):
    return jnp.sum(jnp.where(lane == idx, x, 0.0), axis=-1, keepdims=True)


def _put3(base, lane, start, pieces):
    out = base
    for k, p in enumerate(pieces):
        out = jnp.where(lane == start + k, p, out)
    return out


def _half_stats(t):
    hi = t.astype(BF16)
    mid = (t - hi.astype(F32)).astype(BF16)
    row = lax.broadcasted_iota(jnp.int32, (2 * LANES, LANES), 0)
    col = lax.broadcasted_iota(jnp.int32, (2 * LANES, LANES), 1)
    same_half = (jnp.bitwise_and(row, LANES - 1) < HEAD_DIM) == (col < HEAD_DIM)
    return _dot(jnp.concatenate([hi, mid], axis=1), jnp.where(same_half, 1.0, 0.0).astype(BF16))


def _fwd_in(x2, g, w_t, tm, tn):
    T, D = x2.shape
    N = w_t.shape[0]

    def body(x_ref, g_ref, w_ref, z_ref, h_ref):
        @pl.when(pl.program_id(1) == 0)
        def _():
            x = x_ref[...]
            r = lax.rsqrt(jnp.mean(x * x, axis=-1, keepdims=True) + EPS)
            h_ref[...] = (x * r * g_ref[...]).astype(BF16)

        z_ref[...] = _dot_nt(h_ref[...], w_ref[...])

    return pl.pallas_call(
        body, name="fwd_in", grid=(T // tm, N // tn),
        in_specs=[pl.BlockSpec((tm, D), lambda i, j: (i, 0)),
                  pl.BlockSpec((1, D), lambda i, j: (0, 0)),
                  pl.BlockSpec((tn, D), lambda i, j: (j, 0))],
        out_specs=[pl.BlockSpec((tm, tn), lambda i, j: (i, j)),
                   pl.BlockSpec((tm, D), lambda i, j: (i, 0))],
        out_shape=[SDS((T, N), F32), SDS((T, D), BF16)],
        compiler_params=_params(("parallel", "arbitrary")),
    )(x2, g, w_t)


def _tri_cumsum(x, reverse):
    t = x.shape[0]
    row = lax.broadcasted_iota(jnp.int32, (t, t), 0)
    col = lax.broadcasted_iota(jnp.int32, (t, t), 1)
    tri = (row <= col) if reverse else (row >= col)
    tri = jnp.where(tri, 1.0, 0.0).astype(BF16)
    hi, mid, lo = _split3(x)
    return _dot(tri, hi.astype(BF16)) + _dot(tri, mid.astype(BF16)) + _dot(tri, lo.astype(BF16))


def _gate_fwd(z, b_pad, B, S, H, col_blk, tc):
    T = B * S
    nsb = S // tc

    def body(zf_ref, b_ref, c_ref, carry):
        @pl.when(pl.program_id(1) == 0)
        def _():
            carry[...] = jnp.zeros_like(carry)

        x = zf_ref[...] + b_ref[...]
        lf = jnp.minimum(x, 0.0) - jnp.log(1.0 + jnp.exp(-jnp.abs(x)))
        lf = jnp.where(_lane(lf.shape) < H, lf, 0.0)
        c_ref[...] = _tri_cumsum(lf, False) + carry[...]
        carry[...] = carry[...] + jnp.sum(lf, axis=0, keepdims=True)

    return pl.pallas_call(
        body, name="gate_fwd", grid=(B, nsb),
        in_specs=[pl.BlockSpec((tc, LANES), lambda b, s: (b * nsb + s, col_blk)),
                  pl.BlockSpec((1, LANES), lambda b, s: (0, 0))],
        out_specs=pl.BlockSpec((tc, LANES), lambda b, s: (b * nsb + s, 0)),
        out_shape=SDS((T, LANES), F32),
        scratch_shapes=[pltpu.VMEM((1, LANES), F32)],
        compiler_params=_params(("parallel", "arbitrary")),
    )(z, b_pad)


def _qk_normalize(x, g):
    r = lax.rsqrt(_half_stats(x * x) * (1.0 / HEAD_DIM) + EPS)
    return x * r * g


def _attn_prep(z, c, gq, gk, B, S, H, tm):
    T = B * S
    FW = H * HEAD_DIM
    nsb = S // tm
    nfb = FW // LANES
    scale = HEAD_DIM ** -0.5

    def body(zq_ref, zk_ref, zv_ref, c_ref, gq_ref, gk_ref, qa_ref, ka_ref, va_ref):
        p = pl.program_id(1)
        lane = _lane((tm, LANES))
        lo = lane < HEAD_DIM
        qn = _qk_normalize(zq_ref[...], gq_ref[...]) * scale
        kn = _qk_normalize(zk_ref[...], gk_ref[...])
        v = zv_ref[...]
        cc = c_ref[...]
        ones_q = ((lane >= L_KDECAY) & (lane < L_KDECAY + 3)).astype(F32)
        ones_k = (((lane >= L_ROWSUM) & (lane < L_ROWSUM + 3)) | ((lane >= L_LSE) & (lane < L_LSE + 3))).astype(F32)
        ones_v = ((lane >= L_ROWSUM) & (lane < L_D + 3)).astype(F32)
        for e in range(2):
            if e == 0:
                qe, ke, ve = qn, kn, v
            else:
                qe, ke, ve = (pltpu.roll(t, HEAD_DIM, 1) for t in (qn, kn, v))
            ch = _lane_col(cc, lane, 2 * p + e)
            pieces = _split3(ch)
            qa = jnp.where(lo, qe, _put3(ones_q, lane, L_ROWSUM, pieces))
            ka = jnp.where(lo, ke, _put3(ones_k, lane, L_KDECAY, [-t for t in pieces]))
            va = jnp.where(lo, ve, ones_v)
            qa_ref[0, e] = qa.astype(BF16)
            ka_ref[0, e] = ka.astype(BF16)
            va_ref[0, e] = va.astype(BF16)

    zspec = lambda off: pl.BlockSpec((tm, LANES), lambda i, p: (i, off + p))
    gspec = pl.BlockSpec((1, LANES), lambda i, p: (0, p))
    ospec = pl.BlockSpec((1, 2, tm, LANES), lambda i, p: (i // nsb, p, i % nsb, 0))
    oshape = SDS((B, H, S, LANES), BF16)
    return pl.pallas_call(
        body, name="attn_prep", grid=(T // tm, H // 2),
        in_specs=[zspec(0), zspec(nfb), zspec(2 * nfb),
                  pl.BlockSpec((tm, LANES), lambda i, p: (i, 0)), gspec, gspec],
        out_specs=[ospec, ospec, ospec],
        out_shape=[oshape, oshape, oshape],
        compiler_params=_params(("parallel", "arbitrary")),
    )(z, z, z, c, gq, gk)


def _attn_fwd(qa, ka, va, bound, t, hb):
    B, H, S, _ = qa.shape
    nq = S // t

    def body(q_ref, k_ref, v_ref, b_ref, o_ref, m_ref, acc_ref, qs_ref):
        i = pl.program_id(2)
        lane = _lane((t, LANES))

        shifts = [b_ref[e] for e in range(hb)]
        worst = shifts[0]
        for e in range(1, hb):
            worst = jnp.maximum(worst, shifts[e])
        bounded = jnp.max(worst) <= SHIFT_MAX
        acc_ref[...] = jnp.zeros_like(acc_ref)

        def tiles(step):
            def loop_body(j, carry):
                step(j, False)
                return carry

            lax.fori_loop(0, i, loop_body, 0)
            step(i, True)

        def keep_mask(n=t):
            return lax.broadcasted_iota(jnp.int32, (n, n), 0) >= lax.broadcasted_iota(jnp.int32, (n, n), 1)

        def finish(e, shift):
            acc = acc_ref[e]
            l = _lane_col(acc, lane, L_ROWSUM)
            o_ref[0, e] = jnp.where(lane < HEAD_DIM, acc / l, shift + jnp.log(l))

        @pl.when(bounded)
        def _():
            for e in range(hb):
                qs_ref[e] = _put3(q_ref[0, e].astype(F32), lane, L_LSE, _split3(-shifts[e])).astype(BF16)

            def pair(e, q_rows, k_start, n, masked):
                k_rows = pl.ds(pl.multiple_of(k_start, n), n)
                p = jnp.exp(_dot_nt(qs_ref[e, q_rows, :], k_ref[0, e, k_rows, :]))
                if masked:
                    p = jnp.where(keep_mask(n), p, 0.0)
                acc_ref[e, q_rows, :] = acc_ref[e, q_rows, :] + _dot(p.astype(BF16), v_ref[0, e, k_rows, :])

            def step(j, masked):
                for e in range(hb):
                    if masked:
                        h = t // 2
                        pair(e, slice(0, h), j * t, h, True)
                        pair(e, slice(h, t), j * t, h, False)
                        pair(e, slice(h, t), j * t + h, h, True)
                    else:
                        pair(e, slice(0, t), j * t, t, False)

            tiles(step)
            for e in range(hb):
                finish(e, shifts[e])

        @pl.when(jnp.logical_not(bounded))
        def _():
            m_ref[...] = jnp.full_like(m_ref, NEG_INF)

            def step(j, masked):
                rows = pl.ds(pl.multiple_of(j * t, t), t)
                for e in range(hb):
                    s = _dot_nt(q_ref[0, e], k_ref[0, e, rows, :])
                    if masked:
                        s = jnp.where(keep_mask(), s, NEG_INF)
                    m_prev = m_ref[e]
                    m_new = jnp.maximum(m_prev, jnp.max(s, axis=-1, keepdims=True))
                    alpha = jnp.exp(m_prev - m_new)
                    p = jnp.exp(s - m_new).astype(BF16)
                    acc_ref[e] = alpha * acc_ref[e] + _dot(p, v_ref[0, e, rows, :])
                    m_ref[e] = m_new

            tiles(step)
            for e in range(hb):
                finish(e, m_ref[e])

    return pl.pallas_call(
        body, name="attn_fwd", grid=(B, H // hb, nq),
        in_specs=[pl.BlockSpec((1, hb, t, LANES), lambda b, h, i: (b, h, i, 0)),
                  pl.BlockSpec((1, hb, S, LANES), lambda b, h, i: (b, h, 0, 0)),
                  pl.BlockSpec((1, hb, S, LANES), lambda b, h, i: (b, h, 0, 0)),
                  pl.BlockSpec((hb, 1, LANES), lambda b, h, i: (h, 0, 0))],
        out_specs=pl.BlockSpec((1, hb, t, LANES), lambda b, h, i: (b, h, i, 0)),
        out_shape=SDS((B, H, S, LANES), F32),
        scratch_shapes=[pltpu.VMEM((hb, t, 1), F32), pltpu.VMEM((hb, t, LANES), F32),
                        pltpu.VMEM((hb, t, LANES), BF16)],
        compiler_params=_params(("parallel", "parallel", "arbitrary")),
    )(qa, ka, va, bound)


def _fill_shifts(ext_ref, sh_ref):
    rows = sh_ref.shape[1]
    for b in range(1, SUBLANES):
        sh_ref[b - 1] = ext_ref[pl.ds(b, rows), :]


def _tap_window(ext_ref, sh_ref, off, tm, cols):
    b = off % SUBLANES
    if b == 0:
        return ext_ref[pl.ds(off, tm), cols]
    return sh_ref[b - 1, pl.ds(off - b, tm), cols]


def _conv_taps(w_ref, ext_ref, sh_ref, out_ref, n_taps, tm, offset_of, bias_ref=None):
    for cc in range(out_ref.shape[1] // LANES):
        cols = slice(cc * LANES, (cc + 1) * LANES)
        acc = None
        for j in sorted(range(n_taps), key=offset_of):
            term = w_ref[j:j + 1, cols] * _tap_window(ext_ref, sh_ref, offset_of(j), tm, cols)
            acc = term if acc is None else acc + term
        out_ref[:, cols] = acc if bias_ref is None else acc + bias_ref[:, cols]


def _layernorm_stats(u2):
    mu = jnp.mean(u2, axis=-1, keepdims=True)
    xc = u2 - mu
    rstd = lax.rsqrt(jnp.mean(xc * xc, axis=-1, keepdims=True) + EPS)
    return xc * rstd, rstd


def _fwd_out(oa, z, x2, tgt, conv_w, conv_b, ln_g, ln_b, w_out, B, S, H, n_taps, tm):
    T, D = x2.shape
    FW = H * HEAD_DIM
    CW = conv_w.shape[1]
    nsb = S // tm
    hb = tm // HALO
    mb = 4 if nsb % 4 == 0 else 1
    mt = mb * tm

    def body(oa_ref, gf_ref, ga_ref, gb_ref, gc_ref, ha_ref, hb_ref, x_ref, t_ref, w_ref, cb_ref, lg_ref,
             lb_ref, wo_ref, y_ref, u2_ref, a_ref, do_ref, dob_ref, dy_ref, loss_ref, ext_ref, sh_ref):
        first_step = (pl.program_id(0) == 0) & (pl.program_id(1) == 0)
        sub = lax.rem(pl.program_id(1), mb)
        rows = pl.ds(pl.multiple_of(sub * tm, tm), tm)

        @pl.when(first_step)
        def _():
            loss_ref[...] = jnp.zeros_like(loss_ref)

        u1 = ga_ref[...] * _sigmoid(gb_ref[...])
        halo = ha_ref[...] * _sigmoid(hb_ref[...])
        ext_ref[0:HALO, :] = jnp.where(pl.program_id(1) > 0, halo, 0.0)
        ext_ref[HALO:, :] = u1
        _fill_shifts(ext_ref, sh_ref)
        _conv_taps(w_ref, ext_ref, sh_ref, u2_ref, n_taps, tm, lambda j: HALO - (n_taps - 1) + j, cb_ref)
        uh, _ = _layernorm_stats(u2_ref[...])
        u3 = uh * lg_ref[...] + lb_ref[...]
        gc = gc_ref[...]
        yu = u3 * _sigmoid(u3) * (gc * _sigmoid(gc))
        y_ref[rows, FW:] = yu.astype(BF16)

        lane = _lane((tm, LANES))
        lo = lane < HEAD_DIM
        for p in range(H // 2):
            a_ref[:, p * LANES:(p + 1) * LANES] = jnp.where(
                lo, oa_ref[0, 2 * p], pltpu.roll(oa_ref[0, 2 * p + 1], HEAD_DIM, 1))
        gf = gf_ref[...]
        y_ref[rows, :FW] = (a_ref[...] * (gf * _sigmoid(gf))).astype(BF16)

        @pl.when(sub == mb - 1)
        def _():
            out = x_ref[...] + _dot(y_ref[...], wo_ref[...])
            diff = out - t_ref[...]
            loss_ref[...] = loss_ref[...] + jnp.sum(diff * diff)
            dout = diff * (1.0 / D)
            do_ref[...] = dout
            dob = dout.astype(BF16)
            dob_ref[...] = dob
            dy_ref[...] = _dot_nt(dob, wo_ref[...])

    row = lambda b, s: b * nsb + s
    zspec = lambda cb: pl.BlockSpec((tm, FW), lambda b, s: (row(b, s), cb))
    hspec = lambda cb: pl.BlockSpec((HALO, CW), lambda b, s: (jnp.maximum(row(b, s) * hb - 1, 0), cb))
    vspec = pl.BlockSpec((1, CW), lambda b, s: (0, 0))
    tspec = lambda w: pl.BlockSpec((tm, w), lambda b, s: (row(b, s), 0))
    mspec = lambda w: pl.BlockSpec((mt, w), lambda b, s: (row(b, s) // mb, 0))
    return pl.pallas_call(
        body, name="fwd_out", grid=(B, nsb),
        in_specs=[pl.BlockSpec((1, H, tm, LANES), lambda b, s: (b, 0, s, 0)),
                  zspec(3), zspec(4), zspec(5), zspec(6), hspec(4), hspec(5),
                  mspec(D), mspec(D),
                  pl.BlockSpec((HALO, CW), lambda b, s: (0, 0)), vspec, vspec, vspec,
                  pl.BlockSpec((FW + CW, D), lambda b, s: (0, 0))],
        out_specs=[mspec(FW + CW), tspec(CW), tspec(FW), mspec(D), mspec(D), mspec(FW + CW),
                   pl.BlockSpec((8, LANES), lambda b, s: (0, 0))],
        out_shape=[SDS((T, FW + CW), BF16), SDS((T, CW), F32), SDS((T, FW), F32), SDS((T, D), F32),
                   SDS((T, D), BF16), SDS((T, FW + CW), F32), SDS((8, LANES), F32)],
        scratch_shapes=[pltpu.VMEM((tm + HALO, CW), F32),
                        pltpu.VMEM((SUBLANES - 1, tm + HALO - SUBLANES, CW), F32)],
        compiler_params=_params(("arbitrary", "arbitrary")),
    )(oa, z, z, z, z, z, z, x2, tgt, conv_w, conv_b, ln_g, ln_b, w_out)


def _bwd_prep(dy, z, a_nat, oa, qa, u2, ln_g, ln_b, B, S, H, tm):
    T = B * S
    FW = H * HEAD_DIM
    CW = u2.shape[1]
    nsb = S // tm

    def body(dya_ref, dyu_ref, gf_ref, gc_ref, a_ref, oa_ref, qa_ref, u2_ref, lg_ref, lb_ref,
             dzgf_ref, dzgc_ref, du2_ref, doa_ref, qb_ref, sg_ref):
        first_step = (pl.program_id(0) == 0) & (pl.program_id(1) == 0)

        @pl.when(first_step)
        def _():
            sg_ref[...] = jnp.zeros_like(sg_ref)

        gf = gf_ref[...]
        sg = _sigmoid(gf)
        a = a_ref[...]
        dya = dya_ref[...]
        da = dya * (gf * sg)
        dzgf_ref[...] = (dya * a * (sg * (1.0 + gf * (1.0 - sg)))).astype(BF16)
        dd = da * a
        lane = _lane((tm, LANES))
        lo = lane < HEAD_DIM
        for p in range(H // 2):
            cols = slice(p * LANES, (p + 1) * LANES)
            da_p = da[:, cols]
            dd_p = dd[:, cols]
            d_heads = (jnp.sum(jnp.where(lo, dd_p, 0.0), axis=-1, keepdims=True),
                       jnp.sum(jnp.where(lo, 0.0, dd_p), axis=-1, keepdims=True))
            for e in range(2):
                da_e = da_p if e == 0 else pltpu.roll(da_p, HEAD_DIM, 1)
                d_e = d_heads[e]
                aug = _put3(jnp.zeros((tm, LANES), F32), lane, L_D, _split3(-d_e))
                doa_ref[0, 2 * p + e] = jnp.where(lo, da_e, aug).astype(BF16)
                lse = _lane_col(oa_ref[0, 2 * p + e], lane, L_ROWSUM)
                qb = _put3(qa_ref[0, 2 * p + e].astype(F32), lane, L_LSE, _split3(-lse))
                qb_ref[0, 2 * p + e] = qb.astype(BF16)

        gc = gc_ref[...]
        sc = _sigmoid(gc)
        dyu = dyu_ref[...]
        uh, rstd = _layernorm_stats(u2_ref[...])
        u3 = uh * lg_ref[...] + lb_ref[...]
        s3 = _sigmoid(u3)
        dzgc_ref[...] = (dyu * (u3 * s3) * (sc * (1.0 + gc * (1.0 - sc)))).astype(BF16)
        du3 = dyu * (gc * sc) * (s3 * (1.0 + u3 * (1.0 - s3)))
        sg_ref[0:1, :] = sg_ref[0:1, :] + jnp.sum(du3 * uh, axis=0, keepdims=True)
        sg_ref[1:2, :] = sg_ref[1:2, :] + jnp.sum(du3, axis=0, keepdims=True)
        duh = du3 * lg_ref[...]
        du2 = rstd * (duh - jnp.mean(duh, axis=-1, keepdims=True)
                      - uh * jnp.mean(duh * uh, axis=-1, keepdims=True))
        sg_ref[2:3, :] = sg_ref[2:3, :] + jnp.sum(du2, axis=0, keepdims=True)
        du2_ref[...] = du2

    row = lambda b, s: b * nsb + s
    tspec = lambda w, cb=0: pl.BlockSpec((tm, w), lambda b, s: (row(b, s), cb))
    hspec = pl.BlockSpec((1, H, tm, LANES), lambda b, s: (b, 0, s, 0))
    vspec = pl.BlockSpec((1, CW), lambda b, s: (0, 0))
    return pl.pallas_call(
        body, name="bwd_prep", grid=(B, nsb),
        in_specs=[tspec(FW, 0), tspec(CW, 1), tspec(FW, 3), tspec(CW, 6), tspec(FW), hspec, hspec,
                  tspec(CW), vspec, vspec],
        out_specs=[tspec(FW), tspec(CW), tspec(CW), hspec, hspec,
                   pl.BlockSpec((8, CW), lambda b, s: (0, 0))],
        out_shape=[SDS((T, FW), BF16), SDS((T, CW), BF16), SDS((T, CW), F32),
                   SDS((B, H, S, LANES), BF16), SDS((B, H, S, LANES), BF16), SDS((8, CW), F32)],
        compiler_params=_params(("arbitrary", "arbitrary")),
    )(dy, dy, z, z, a_nat, oa, qa, u2, ln_g, ln_b)


def _conv_bwd(du2, z, conv_w, B, S, n_taps, tm):
    T, CW = du2.shape
    nsb = S // tm
    hb = tm // HALO

    def body(d_ref, dh_ref, ga_ref, gb_ref, ha_ref, hb_ref, w_ref, dz_ref, dw_ref,
             extu_ref, extd_ref, shu_ref, shd_ref, du1_ref, dwacc_ref):
        s = pl.program_id(1)
        first_step = (pl.program_id(0) == 0) & (s == 0)
        last_step = (pl.program_id(0) == B - 1) & (s == nsb - 1)

        @pl.when(first_step)
        def _():
            dwacc_ref[...] = jnp.zeros_like(dwacc_ref)

        ga = ga_ref[...]
        sb = _sigmoid(gb_ref[...])
        halo = ha_ref[...] * _sigmoid(hb_ref[...])
        extu_ref[0:HALO, :] = jnp.where(s > 0, halo, 0.0)
        extu_ref[HALO:, :] = ga * sb
        extd_ref[0:tm, :] = d_ref[...]
        extd_ref[tm:, :] = jnp.where(s < nsb - 1, dh_ref[...], 0.0)
        _fill_shifts(extu_ref, shu_ref)
        _fill_shifts(extd_ref, shd_ref)
        _conv_taps(w_ref, extd_ref, shd_ref, du1_ref, n_taps, tm, lambda j: n_taps - 1 - j)
        for cc in range(CW // LANES):
            cols = slice(cc * LANES, (cc + 1) * LANES)
            parts = [None] * n_taps
            for r in range(tm // SUBLANES):
                dv = d_ref[r * SUBLANES:(r + 1) * SUBLANES, cols]
                for j in range(n_taps):
                    off = HALO - (n_taps - 1) + j + r * SUBLANES
                    term = dv * _tap_window(extu_ref, shu_ref, off, SUBLANES, cols)
                    parts[j] = term if parts[j] is None else parts[j] + term
            for j in range(n_taps):
                rows = slice(j * SUBLANES, (j + 1) * SUBLANES)
                dwacc_ref[rows, cols] = dwacc_ref[rows, cols] + parts[j]
        du1 = du1_ref[...]
        dz_ref[:, :CW] = (du1 * sb).astype(BF16)
        dz_ref[:, CW:] = (du1 * ga * (sb * (1.0 - sb))).astype(BF16)

        @pl.when(last_step)
        def _():
            dw_ref[...] = jnp.zeros_like(dw_ref)
            for j in range(n_taps):
                dw_ref[j:j + 1, :] = jnp.sum(dwacc_ref[j * SUBLANES:(j + 1) * SUBLANES, :], axis=0, keepdims=True)

    row = lambda b, s: b * nsb + s
    last_halo = T // HALO - 1
    return pl.pallas_call(
        body, name="conv_bwd", grid=(B, nsb),
        in_specs=[pl.BlockSpec((tm, CW), lambda b, s: (row(b, s), 0)),
                  pl.BlockSpec((HALO, CW), lambda b, s: (jnp.minimum((row(b, s) + 1) * hb, last_halo), 0)),
                  pl.BlockSpec((tm, CW), lambda b, s: (row(b, s), 4)),
                  pl.BlockSpec((tm, CW), lambda b, s: (row(b, s), 5)),
                  pl.BlockSpec((HALO, CW), lambda b, s: (jnp.maximum(row(b, s) * hb - 1, 0), 4)),
                  pl.BlockSpec((HALO, CW), lambda b, s: (jnp.maximum(row(b, s) * hb - 1, 0), 5)),
                  pl.BlockSpec((HALO, CW), lambda b, s: (0, 0))],
        out_specs=[pl.BlockSpec((tm, 2 * CW), lambda b, s: (row(b, s), 0)),
                   pl.BlockSpec((HALO, CW), lambda b, s: (0, 0))],
        out_shape=[SDS((T, 2 * CW), BF16), SDS((HALO, CW), F32)],
        scratch_shapes=[pltpu.VMEM((tm + HALO, CW), F32), pltpu.VMEM((tm + HALO, CW), F32),
                        pltpu.VMEM((SUBLANES - 1, tm + HALO - SUBLANES, CW), F32),
                        pltpu.VMEM((SUBLANES - 1, tm + HALO - SUBLANES, CW), F32),
                        pltpu.VMEM((tm, CW), F32), pltpu.VMEM((HALO * SUBLANES, CW), F32)],
        compiler_params=_params(("arbitrary", "arbitrary")),
    )(du2, du2, z, z, z, z, conv_w)


def _attn_bwd(qb, ka, va, doa, t, hb):
    B, H, S, _ = qb.shape
    nk = S // t

    def body(q_ref, k_ref, v_ref, do_ref, dq_ref, dk_ref, dv_ref, dv_acc):
        j = pl.program_id(2)

        @pl.when(j == 0)
        def _():
            dq_ref[...] = jnp.zeros_like(dq_ref)

        dk_ref[...] = jnp.zeros_like(dk_ref)
        dv_acc[...] = jnp.zeros_like(dv_acc)

        def step(i, masked):
            q_rows = pl.ds(pl.multiple_of(i * t, t), t)
            for e in range(hb):
                k = k_ref[0, e]
                q = q_ref[0, e, q_rows, :]
                do = do_ref[0, e, q_rows, :]
                p = jnp.exp(_dot_nt(q, k))
                if masked:
                    keep = lax.broadcasted_iota(jnp.int32, (t, t), 0) >= lax.broadcasted_iota(jnp.int32, (t, t), 1)
                    p = jnp.where(keep, p, 0.0)
                ds = (p * _dot_nt(do, v_ref[0, e])).astype(BF16)
                dv_acc[e] = dv_acc[e] + _dot_tn(p.astype(BF16), do)
                dk_ref[0, e] = dk_ref[0, e] + _dot_tn(ds, q)
                dq_ref[0, e, q_rows, :] = dq_ref[0, e, q_rows, :] + _dot(ds, k)

        step(j, True)

        def loop_body(i, carry):
            step(i, False)
            return carry

        lax.fori_loop(j + 1, nk, loop_body, 0)
        dv_ref[0] = dv_acc[...].astype(BF16)

    full = pl.BlockSpec((1, hb, S, LANES), lambda b, h, j: (b, h, 0, 0))
    blk = pl.BlockSpec((1, hb, t, LANES), lambda b, h, j: (b, h, j, 0))
    oshape = SDS((B, H, S, LANES), F32)
    return pl.pallas_call(
        body, name="attn_bwd", grid=(B, H // hb, nk),
        in_specs=[full, blk, blk, full],
        out_specs=[full, blk, blk],
        out_shape=[oshape, oshape, SDS((B, H, S, LANES), BF16)],
        scratch_shapes=[pltpu.VMEM((hb, t, LANES), F32)],
        compiler_params=_params(("parallel", "parallel", "arbitrary")),
    )(qb, ka, va, doa)


def _qk_bwd(dqa, dka, dva, z, gq, gk, B, S, H, tm):
    T = B * S
    FW = H * HEAD_DIM
    nsb = S // tm
    nfb = FW // LANES
    scale = HEAD_DIM ** -0.5

    def body(dq_ref, dk_ref, dv_ref, zq_ref, zk_ref, gq_ref, gk_ref, dzq_ref, dzk_ref, dzv_ref, dc_ref, dg_ref):
        p = pl.program_id(0)

        @pl.when(pl.program_id(1) == 0)
        def _():
            dg_ref[...] = jnp.zeros_like(dg_ref)

        lane = _lane((tm, LANES))
        lo = lane < HEAD_DIM

        def natural(ref):
            return jnp.where(lo, ref[0, 0].astype(F32), pltpu.roll(ref[0, 1].astype(F32), HEAD_DIM, 1))

        def norm_bwd(dn, x, g, row, out_ref):
            r = lax.rsqrt(_half_stats(x * x) * (1.0 / HEAD_DIM) + EPS)
            xh = x * r
            dg_ref[row:row + 1, :] = dg_ref[row:row + 1, :] + jnp.sum(dn * xh, axis=0, keepdims=True)
            dxh = dn * g
            mm = _half_stats(dxh * xh) * (1.0 / HEAD_DIM)
            out_ref[...] = (r * (dxh - xh * mm)).astype(BF16)

        norm_bwd(natural(dq_ref) * scale, zq_ref[...], gq_ref[...], 0, dzq_ref)
        norm_bwd(natural(dk_ref), zk_ref[...], gk_ref[...], 1, dzk_ref)
        dzv_ref[...] = natural(dv_ref).astype(BF16)

        dc = jnp.zeros((tm, LANES), F32)
        for e in range(2):
            val = _lane_col(dq_ref[0, e], lane, L_ROWSUM) - _lane_col(dk_ref[0, e], lane, L_KDECAY)
            dc = jnp.where(lane == 2 * p + e, val, dc)
        dc_ref[0] = dc

    hspec = pl.BlockSpec((1, 2, tm, LANES), lambda p, i: (i // nsb, p, i % nsb, 0))
    zspec = lambda off: pl.BlockSpec((tm, LANES), lambda p, i: (i, off + p))
    gspec = pl.BlockSpec((1, LANES), lambda p, i: (0, p))
    ospec = pl.BlockSpec((tm, LANES), lambda p, i: (i, p))
    return pl.pallas_call(
        body, name="qk_bwd", grid=(H // 2, T // tm),
        in_specs=[hspec, hspec, hspec, zspec(0), zspec(nfb), gspec, gspec],
        out_specs=[ospec, ospec, ospec,
                   pl.BlockSpec((1, tm, LANES), lambda p, i: (p, i, 0)),
                   pl.BlockSpec((8, LANES), lambda p, i: (0, p))],
        out_shape=[SDS((T, FW), BF16), SDS((T, FW), BF16), SDS((T, FW), BF16),
                   SDS((H // 2, T, LANES), F32), SDS((8, FW), F32)],
        compiler_params=_params(("parallel", "arbitrary")),
    )(dqa, dka, dva, z, z, gq, gk)


def _gate_bwd(dc8, z, b_pad, B, S, H, col_blk, fp, tc):
    T = B * S
    nsb = S // tc
    npair = dc8.shape[0]

    def body(dc_ref, zf_ref, b_ref, dz_ref, db_ref, carry):
        first_step = (pl.program_id(0) == 0) & (pl.program_id(1) == 0)

        @pl.when(first_step)
        def _():
            db_ref[...] = jnp.zeros_like(db_ref)

        @pl.when(pl.program_id(1) == 0)
        def _():
            carry[...] = jnp.zeros_like(carry)

        dc = dc_ref[0]
        for k in range(1, npair):
            dc = dc + dc_ref[k]
        dlf = _tri_cumsum(dc, True) + carry[...]
        carry[...] = carry[...] + jnp.sum(dc, axis=0, keepdims=True)
        x = zf_ref[...] + b_ref[...]
        dlogit = dlf * _sigmoid(-x)
        db_ref[0:1, :] = db_ref[0:1, :] + jnp.sum(dlogit, axis=0, keepdims=True)
        dz_ref[...] = jnp.zeros_like(dz_ref)
        dz_ref[:, :LANES] = dlogit.astype(BF16)

    rrow = lambda b, s: b * nsb + (nsb - 1 - s)
    return pl.pallas_call(
        body, name="gate_bwd", grid=(B, nsb),
        in_specs=[pl.BlockSpec((npair, tc, LANES), lambda b, s: (0, rrow(b, s), 0)),
                  pl.BlockSpec((tc, LANES), lambda b, s: (rrow(b, s), col_blk)),
                  pl.BlockSpec((1, LANES), lambda b, s: (0, 0))],
        out_specs=[pl.BlockSpec((tc, fp), lambda b, s: (rrow(b, s), 0)),
                   pl.BlockSpec((8, LANES), lambda b, s: (0, 0))],
        out_shape=[SDS((T, fp), BF16), SDS((8, LANES), F32)],
        scratch_shapes=[pltpu.VMEM((1, LANES), F32)],
        compiler_params=_params(("arbitrary", "arbitrary")),
    )(dc8, z, b_pad)


def _matmul_tn(a, b, name, tmm, tn, tk):
    T, M = a.shape
    N = b.shape[1]
    tmm, tn, tk = min(tmm, M), min(tn, N), min(tk, T)

    def body(a_ref, b_ref, o_ref):
        @pl.when(pl.program_id(2) == 0)
        def _():
            o_ref[...] = jnp.zeros_like(o_ref)

        o_ref[...] = o_ref[...] + _dot_tn(a_ref[...], b_ref[...])

    return pl.pallas_call(
        body, name=name, grid=(M // tmm, N // tn, T // tk),
        in_specs=[pl.BlockSpec((tk, tmm), lambda i, j, k: (k, i)),
                  pl.BlockSpec((tk, tn), lambda i, j, k: (k, j))],
        out_specs=pl.BlockSpec((tmm, tn), lambda i, j, k: (i, j)),
        out_shape=SDS((M, N), F32),
        compiler_params=_params(("parallel", "parallel", "arbitrary")),
    )(a, b)


def _dh_rms_bwd(pieces, w_t, x2, g, dout, tm, tk, parts):
    T, D = x2.shape
    nks = [p.shape[1] // tk for p in pieces]
    starts = [sum(nks[:k]) for k in range(len(pieces))]
    nk = sum(nks)
    ni = T // tm
    n = len(parts)

    def body(*refs):
        dz_refs = refs[:len(pieces)]
        w_ref, x_ref, g_ref, do_ref = refs[len(pieces):len(pieces) + 4]
        part_refs = refs[len(pieces) + 4:len(pieces) + 4 + n]
        gx_ref, dg_ref = refs[len(pieces) + 4 + n:len(pieces) + 6 + n]
        slot_refs = refs[len(pieces) + 6 + n:len(pieces) + 6 + 2 * n]
        acc_ref, send_sems, recv_sems = refs[len(pieces) + 6 + 2 * n:]
        k = pl.program_id(1)
        first_step = (pl.program_id(0) == 0) & (k == 0)
        last_step = (pl.program_id(0) == ni - 1) & (k == nk - 1)
        x, y, c = _place()
        chips = [(1 - x, y), (x, 1 - y), (1 - x, 1 - y)]

        def copy(a, f, to):
            cx, cy = chips[f]
            return pltpu.make_async_remote_copy(
                src_ref=part_refs[a].at[2 * cx + cy], dst_ref=slot_refs[a].at[f],
                send_sem=send_sems.at[a * 3 + f], recv_sem=recv_sems.at[a * 3 + f],
                device_id=to, device_id_type=MESH)

        @pl.when(first_step)
        def _():
            dg_ref[...] = jnp.zeros_like(dg_ref)
            for a in range(n):
                for f in range(3):
                    copy(a, f, (*chips[f], c)).start()

        @pl.when(last_step)
        def _():
            for a in range(n):
                for f in range(3):
                    copy(a, f, (x, y, c)).wait_recv()
            for a in range(n):
                for f in range(3):
                    copy(a, f, (*chips[f], c)).wait_send()

        @pl.when(k == 0)
        def _():
            acc_ref[...] = jnp.zeros_like(acc_ref)

        for dz_ref, st, cnt in zip(dz_refs, starts, nks):
            @pl.when((k >= st) & (k < st + cnt))
            def _(dz_ref=dz_ref):
                acc_ref[...] = acc_ref[...] + _dot(dz_ref[...], w_ref[...])

        @pl.when(k == nk - 1)
        def _():
            x = x_ref[...]
            r = lax.rsqrt(jnp.mean(x * x, axis=-1, keepdims=True) + EPS)
            xh = x * r
            dh = acc_ref[...]
            dg_ref[0:1, :] = dg_ref[0:1, :] + jnp.sum(dh * xh, axis=0, keepdims=True)
            dxn = dh * g_ref[...]
            gx_ref[...] = do_ref[...] + r * (dxn - xh * jnp.mean(dxn * xh, axis=-1, keepdims=True))

    def piece_spec(st, cnt):
        return pl.BlockSpec((tm, tk), lambda i, k: (i, jnp.clip(k - st, 0, cnt - 1)))

    tspec = pl.BlockSpec((tm, D), lambda i, k: (i, 0))
    return pl.pallas_call(
        body, name="dh_rms_bwd", grid=(T // tm, nk),
        in_specs=[piece_spec(st, cnt) for st, cnt in zip(starts, nks)]
        + [pl.BlockSpec((tk, D), lambda i, k: (k, 0)), tspec, pl.BlockSpec((1, D), lambda i, k: (0, 0)), tspec]
        + [ANY] * n,
        out_specs=[tspec, pl.BlockSpec((8, D), lambda i, k: (0, 0))] + [ANY] * n,
        out_shape=[SDS((T, D), F32), SDS((8, D), F32)] + [SDS((3,) + p.shape[1:], p.dtype) for p in parts],
        scratch_shapes=[pltpu.VMEM((tm, D), F32),
                        pltpu.SemaphoreType.DMA((3 * n,)), pltpu.SemaphoreType.DMA((3 * n,))],
        compiler_params=_params(("arbitrary", "arbitrary")),
    )(*pieces, w_t, x2, g, dout, *parts)


def _block_plan(R, C, tr, tc):
    br = min(tr, R)
    if R % br == 0:
        return (br, C), R // br, lambda i: (i, 0)
    bc = min(tc, C)
    assert C % bc == 0
    return (R, bc), C // bc, lambda i: (0, i)


def _ew_call(body, name, ins, n_out, out_dtypes, tr, tc):
    R, C = ins[0].shape
    blk, steps, imap = _block_plan(R, C, tr, tc)
    spec = pl.BlockSpec(blk, imap)
    return pl.pallas_call(
        body, name=name, grid=(steps,),
        in_specs=[spec] * len(ins), out_specs=[spec] * n_out,
        out_shape=[SDS((R, C), dt) for dt in out_dtypes],
        compiler_params=_params(("parallel",)),
    )(*ins)


def _sum_slots(slots, name, first=None, tr=256):
    n, R, C = slots.shape
    blk, steps, imap = _block_plan(R, C, tr, 2 * LANES)
    lead = [] if first is None else [first]

    def body(*refs):
        s_ref, o_ref = refs[-2:]
        acc = refs[0][...].astype(F32) if lead else s_ref[0].astype(F32)
        for k in range(0 if lead else 1, n):
            acc = acc + s_ref[k].astype(F32)
        o_ref[...] = acc

    return pl.pallas_call(
        body, name=name, grid=(steps,),
        in_specs=[pl.BlockSpec(blk, imap)] * len(lead) + [pl.BlockSpec((n,) + blk, lambda i: (0,) + imap(i))],
        out_specs=pl.BlockSpec(blk, imap),
        out_shape=SDS((R, C), F32),
        compiler_params=_params(("parallel",)),
    )(*lead, slots)


def _adamw_update(w, g, m, v):
    nm = ADAM_B1 * m + (1.0 - ADAM_B1) * g
    nv = ADAM_B2 * v + (1.0 - ADAM_B2) * (g * g)
    m_hat = nm / (1.0 - ADAM_B1 ** ADAM_STEP)
    v_hat = nv / (1.0 - ADAM_B2 ** ADAM_STEP)
    return -ADAM_LR * (m_hat / (jnp.sqrt(v_hat) + ADAM_EPS) + ADAM_WD * w), nm, nv


def _adamw(w, g, m, v, name):
    def body(w_ref, g_ref, m_ref, v_ref, d_ref, nm_ref, nv_ref):
        d_ref[...], nm_ref[...], nv_ref[...] = _adamw_update(w_ref[...], g_ref[...], m_ref[...], v_ref[...])

    return _ew_call(body, name, [w, g, m, v], 3, [F32, F32, F32], 128, 2 * LANES)


def _adamw_halves(w, mine, other, m, v, name):
    R, C = w.shape
    half = C // 2
    bc = min(2 * LANES, half)
    per = half // bc

    def body(w_ref, a_ref, b_ref, m_ref, v_ref, g_ref, d_ref, nm_ref, nv_ref):
        g = jnp.where(pl.program_id(0) // per == lax.axis_index("c"), a_ref[...], b_ref[...])
        g_ref[...] = g
        d_ref[...], nm_ref[...], nv_ref[...] = _adamw_update(w_ref[...], g, m_ref[...], v_ref[...])

    full = pl.BlockSpec((R, bc), lambda i: (0, i))
    part = pl.BlockSpec((R, bc), lambda i: (0, i % per))
    return pl.pallas_call(
        body, name=name, grid=(C // bc,),
        in_specs=[full, part, part, full, full], out_specs=[full] * 4,
        out_shape=[SDS((R, C), F32)] * 4,
        compiler_params=_params(("parallel",)),
    )(w, mine, other, m, v)


ANY = pl.BlockSpec(memory_space=pl.ANY)


def _place():
    return lax.axis_index("x"), lax.axis_index("y"), lax.axis_index("c")


def _gather_chips(shards, splits):
    n = len(shards)
    per = 7

    def body(*refs):
        ins, outs = refs[:n], refs[n:2 * n]
        send_sems, recv_sems = refs[2 * n:]
        x, y, c = _place()
        mine = 2 * x + y
        chips = [(1 - x, y), (x, 1 - y), (1 - x, 1 - y)]

        def rows(a, half):
            return pl.ds(0, splits[a]) if half == 0 else pl.ds(splits[a], ins[a].shape[0] - splits[a])

        def copy(a, k, chip_idx, half, to, src=None):
            dst = outs[a].at[chip_idx, rows(a, half)]
            return pltpu.make_async_remote_copy(
                src_ref=dst if src is None else src, dst_ref=dst,
                send_sem=send_sems.at[a * per + k], recv_sem=recv_sems.at[a * per + k],
                device_id=to, device_id_type=MESH)

        def own(a, to):
            return pltpu.make_async_remote_copy(
                src_ref=ins[a], dst_ref=outs[a].at[mine],
                send_sem=send_sems.at[a * per + 6], recv_sem=recv_sems.at[a * per + 6],
                device_id=to, device_id_type=MESH)

        for cc in (0, 1):
            @pl.when(c == cc)
            def _(cc=cc):
                me, sibling = (x, y, cc), (x, y, 1 - cc)
                first = [copy(a, k, mine, cc, (*chip, cc), src=ins[a].at[rows(a, cc)])
                         for a in range(n) for k, chip in enumerate(chips)]
                first += [own(a, sibling) for a in range(n)]
                for cp in first:
                    cp.start()
                passed = []
                for k, (cx, cy) in enumerate(chips):
                    for a in range(n):
                        copy(a, k, 2 * cx + cy, cc, me).wait_recv()
                        fwd = copy(a, 3 + k, 2 * cx + cy, cc, sibling)
                        fwd.start()
                        passed.append(fwd)
                for k, (cx, cy) in enumerate(chips):
                    for a in range(n):
                        copy(a, 3 + k, 2 * cx + cy, 1 - cc, me).wait_recv()
                for a in range(n):
                    own(a, me).wait_recv()
                for cp in first + passed:
                    cp.wait_send()

    return pl.pallas_call(
        body, name="gather_chips",
        in_specs=[ANY] * n, out_specs=[ANY] * n,
        out_shape=[SDS((4,) + s.shape, s.dtype) for s in shards],
        scratch_shapes=[pltpu.SemaphoreType.DMA((per * n,)), pltpu.SemaphoreType.DMA((per * n,))],
    )(*shards)


def _pair_swap(arrs):
    n = len(arrs)

    def body(*refs):
        ins, outs = refs[:n], refs[n:2 * n]
        send_sems, recv_sems = refs[2 * n:]
        x, y, c = _place()
        for cc in (0, 1):
            @pl.when(c == cc)
            def _(cc=cc):
                copies = []
                for a in range(n):
                    half = ins[a].shape[1] // 2
                    copies.append(pltpu.make_async_remote_copy(
                        src_ref=ins[a].at[:, pl.ds((1 - cc) * half, half)], dst_ref=outs[a],
                        send_sem=send_sems.at[a], recv_sem=recv_sems.at[a],
                        device_id=(x, y, 1 - cc), device_id_type=MESH))
                for cp in copies:
                    cp.start()
                for cp in copies:
                    cp.wait()

    return pl.pallas_call(
        body, name="pair_swap",
        in_specs=[ANY] * n, out_specs=[ANY] * n,
        out_shape=[SDS((h.shape[0], h.shape[1] // 2), h.dtype) for h in arrs],
        scratch_shapes=[pltpu.SemaphoreType.DMA((n,)), pltpu.SemaphoreType.DMA((n,))],
    )(*arrs)


def _pair_sum(arrs, got, core, tr):
    half = arrs[0].shape[1] // 2
    cnts = [p.shape[0] // tr for p in arrs]
    starts = [sum(cnts[:k]) for k in range(len(arrs))]

    def body(core_ref, *refs):
        del core_ref
        own_refs, got_refs, o_ref = refs[:len(arrs)], refs[len(arrs):2 * len(arrs)], refs[-1]
        s = pl.program_id(0)
        for own_ref, got_ref, st, cnt in zip(own_refs, got_refs, starts, cnts):
            @pl.when((s >= st) & (s < st + cnt))
            def _(own_ref=own_ref, got_ref=got_ref):
                o_ref[...] = (own_ref[...] + got_ref[...]).astype(BF16)

    def own_spec(st, cnt):
        return pl.BlockSpec((tr, half), lambda s, core_ref: (jnp.clip(s - st, 0, cnt - 1), core_ref[0]))

    def got_spec(st, cnt):
        return pl.BlockSpec((tr, half), lambda s, core_ref: (jnp.clip(s - st, 0, cnt - 1), 0))

    return pl.pallas_call(
        body, name="pair_sum",
        grid_spec=pltpu.PrefetchScalarGridSpec(
            num_scalar_prefetch=1, grid=(sum(cnts),),
            in_specs=[own_spec(st, cnt) for st, cnt in zip(starts, cnts)]
            + [got_spec(st, cnt) for st, cnt in zip(starts, cnts)],
            out_specs=pl.BlockSpec((tr, half), lambda s, core_ref: (s, 0))),
        out_shape=SDS((sum(cnts) * tr, half), BF16),
        compiler_params=_params(("arbitrary",)),
    )(core, *arrs, *got)


def _share_results(arrs, rows):
    n = len(arrs)
    flips = [(fx, fy, fc) for fx in (0, 1) for fy in (0, 1) for fc in (0, 1)][1:]

    def body(*refs):
        ins, rows_ref, outs, all_ref = refs[:n], refs[n], refs[n + 1:2 * n + 1], refs[2 * n + 1]
        send_sems, recv_sems, local_sem = refs[2 * n + 2:]
        x, y, c = _place()
        me = 4 * x + 2 * y + c
        local = pltpu.make_async_copy(rows_ref, all_ref.at[me], local_sem)
        local.start()
        copies = [pltpu.make_async_remote_copy(
            src_ref=ins[a], dst_ref=outs[a], send_sem=send_sems.at[a], recv_sem=recv_sems.at[a],
            device_id=(x, y, 1 - c), device_id_type=MESH) for a in range(n)]
        for k, (fx, fy, fc) in enumerate(flips):
            copies.append(pltpu.make_async_remote_copy(
                src_ref=rows_ref, dst_ref=all_ref.at[me], send_sem=send_sems.at[n + k], recv_sem=recv_sems.at[n + k],
                device_id=(x ^ fx, y ^ fy, c ^ fc), device_id_type=MESH))
        for cp in copies:
            cp.start()
        for cp in copies[:n]:
            cp.wait_recv()
        for k, (fx, fy, fc) in enumerate(flips):
            src = 4 * (x ^ fx) + 2 * (y ^ fy) + (c ^ fc)
            pltpu.make_async_remote_copy(
                src_ref=rows_ref, dst_ref=all_ref.at[src], send_sem=send_sems.at[n + k], recv_sem=recv_sems.at[n + k],
                device_id=(x, y, c), device_id_type=MESH).wait_recv()
        for cp in copies:
            cp.wait_send()
        local.wait()

    outs = pl.pallas_call(
        body, name="share_results",
        in_specs=[ANY] * (n + 1), out_specs=[ANY] * (n + 1),
        out_shape=[SDS(h.shape, h.dtype) for h in arrs] + [SDS((8,) + rows.shape, rows.dtype)],
        scratch_shapes=[pltpu.SemaphoreType.DMA((n + 7,)), pltpu.SemaphoreType.DMA((n + 7,)),
                        pltpu.SemaphoreType.DMA],
    )(*arrs, rows)
    return outs[:n], outs[n]


def _tiles(S, FW):
    big = FW % 512 == 0
    return dict(
        fp=512 if big else LANES,
        tn=1536 if big else LANES,
        tm_in=min(1024, S),
        t_attn=min(512, S),
        hb_fwd=8,
        hb_bwd=8,
        tm_prep=min(2048, S),
        tm_mix=min(128, S),
        tm_bwd=min(256, S),
        tc=min(512, S),
        tk=512 if big else LANES,
    )


def kernel(x, norm_g, w_in, b_forget, q_norm_g, k_norm_g, conv_w, conv_b, conv_ln_g, conv_ln_b, w_out, loss_target, m_norm_g, m_w_in, m_b_forget, m_q_norm_g, m_k_norm_g, m_conv_w, m_conv_b, m_conv_ln_g, m_conv_ln_b, m_w_out, v_norm_g, v_w_in, v_b_forget, v_q_norm_g, v_k_norm_g, v_conv_w, v_conv_b, v_conv_ln_g, v_conv_ln_b, v_w_out):
    B, S, D = x.shape
    H, dh = q_norm_g.shape[1:]
    FW = H * dh
    CW = conv_b.shape[-1]
    n_taps, cw_shard = conv_w.shape[1:]
    in_shard = w_in.shape[2]
    out_shard = w_out.shape[1]
    assert dh == HEAD_DIM and H % 2 == 0 and H <= LANES and FW == CW == D
    assert n_taps - 1 <= HALO and 4 * cw_shard == CW and 4 * out_shard == FW + CW
    assert 4 * in_shard == 4 * FW + 3 * CW + H
    T = B * S
    tl = _tiles(S, FW)
    fp = tl["fp"]
    xi, yi, ci = _place()

    w_t = jnp.transpose(w_in[0])
    conv_pad = jnp.pad(conv_w[0], ((0, HALO - n_taps), (0, 0)))
    bf16_rows = 2 * SUBLANES
    g_in, g_out, g_cw = _gather_chips(
        [w_t.astype(BF16), w_out[0].astype(BF16), conv_pad],
        [in_shard // 2 // bf16_rows * bf16_rows, out_shard // 2, HALO // 2])
    w_t_full = g_in.reshape(4 * in_shard, D)
    w_out_full = g_out.reshape(FW + CW, D)
    conv_full = g_cw.transpose(1, 0, 2).reshape(HALO, CW)
    o_f = 3 * FW
    w_pack = jnp.concatenate([w_t_full[:o_f], w_t_full[o_f + H:],
                              jnp.pad(w_t_full[o_f:o_f + H], ((0, fp - H), (0, 0)))], axis=0)
    f_col = 4 * FW + 3 * CW

    x2 = x.reshape(T, D)
    tgt = loss_target.reshape(T, D)
    b_pad = jnp.pad(b_forget, ((0, 0), (0, LANES - H)))
    gq = q_norm_g.reshape(1, FW)
    gk = k_norm_g.reshape(1, FW)

    z, h = _fwd_in(x2, norm_g, w_pack, tl["tm_in"], tl["tn"])
    c = _gate_fwd(z, b_pad, B, S, H, f_col // LANES, tl["tc"])
    qa, ka, va = _attn_prep(z, c, gq, gk, B, S, H, tl["tm_prep"])
    gain = lambda g: jnp.max(jnp.abs(g[0]), axis=-1)
    bound = (NORM_SLACK ** 2 * dh ** 0.5) * gain(q_norm_g) * gain(k_norm_g)
    oa = _attn_fwd(qa, ka, va, jnp.broadcast_to(bound[:, None, None], (H, 1, LANES)), tl["t_attn"], tl["hb_fwd"])
    y, u2, a_nat, dout, dout_b, dy, loss_acc = _fwd_out(
        oa, z, x2, tgt, conv_full, conv_b, conv_ln_g, conv_ln_b, w_out_full, B, S, H, n_taps, tl["tm_mix"])

    dzgf, dzgc, du2, doa, qb, sg_conv = _bwd_prep(dy, z, a_nat, oa, qa, u2, conv_ln_g, conv_ln_b, B, S, H, tl["tm_bwd"])
    dzglu, dconv_w = _conv_bwd(du2, z, conv_full, B, S, n_taps, tl["tm_mix"])
    dqa, dka, dva = _attn_bwd(qb, ka, va, doa, tl["t_attn"], tl["hb_bwd"])
    dzq, dzk, dzv, dc8, dg_qk = _qk_bwd(dqa, dka, dva, z, gq, gk, B, S, H, tl["tm_prep"])
    dzf, db_f = _gate_bwd(dc8, z, b_pad, B, S, H, f_col // LANES, fp, tl["tc"])
    pieces = [dzq, dzk, dzv, dzgf, dzglu, dzgc, dzf]
    dw_all = [_matmul_tn(p, h, f"dw_in_{k}", 1024, 1024, 1024) for k, p in enumerate(pieces)]
    dw_all.append(_matmul_tn(y, dout_b, "dw_out", 1024, 1024, 1024))

    summed = _pair_sum(dw_all, _pair_swap(dw_all), ci.astype(jnp.int32).reshape(1), fp)
    ends = [0]
    for t in dw_all:
        ends.append(ends[-1] + t.shape[0])
    spans, at = [], 0
    for k, rows in [(0, FW), (1, FW), (2, FW), (6, H), (3, FW), (4, 2 * CW), (5, CW)]:
        spans.append((at, rows, ends[k]))
        at += rows

    def chip_rows(j):
        lo, hi = j * in_shard, (j + 1) * in_shard
        return jnp.concatenate([summed[src + max(lo, a) - a:src + min(hi, a + n) - a]
                                for a, n, src in spans if max(lo, a) < min(hi, a + n)], axis=0)

    part_in = jnp.stack([chip_rows(j) for j in range(4)])
    part_out = summed[ends[7]:ends[8]].reshape(4, out_shard, D // 2)
    grad_x2, dg_norm, slots_in, slots_out = _dh_rms_bwd(
        pieces, w_pack, x2, norm_g, dout, tl["tm_in"], tl["tk"], [part_in, part_out])
    chip = 2 * xi + yi
    half_in = _sum_slots(slots_in, "chip_sum_in", lax.dynamic_index_in_dim(part_in, chip, 0, keepdims=False))
    half_out = _sum_slots(slots_out, "chip_sum_out", lax.dynamic_index_in_dim(part_out, chip, 0, keepdims=False))
    lanes_to_d = lambda t: jnp.pad(t, ((0, 0), (0, D - LANES)))
    small = jnp.concatenate([
        dg_norm[0:1], lanes_to_d(db_f[0:1, :]), dg_qk[0:1], dg_qk[1:2],
        sg_conv[2:3], sg_conv[0:1], sg_conv[1:2], dconv_w, lanes_to_d(loss_acc[0:1, :])], axis=0)
    n_small = small.shape[0]
    (other_in, other_out), all_small = _share_results([half_in, half_out], small)

    small_sum = _sum_slots(all_small, "small_sum", tr=n_small)
    loss = 0.5 * small_sum[n_small - 1, 0] / D
    grad_norm_g, grad_b_f = small_sum[0:1], small_sum[1:2, :H]
    grad_gq, grad_gk = small_sum[2:3].reshape(1, H, dh), small_sum[3:4].reshape(1, H, dh)
    grad_conv_b, grad_ln_g, grad_ln_b = small_sum[4:5], small_sum[5:6], small_sum[6:7]
    grad_conv_w = lax.dynamic_slice_in_dim(small_sum[7:7 + n_taps], chip * cw_shard, cw_shard, axis=1)

    in_t = _adamw_halves(w_t, half_in, other_in, jnp.transpose(m_w_in[0]), jnp.transpose(v_w_in[0]), "adamw_in")
    grad_w_in, d_in, nm_in, nv_in = (jnp.transpose(t)[None] for t in in_t)
    grad_w_out, d_out, nm_out, nv_out = (
        t[None] for t in _adamw_halves(w_out[0], half_out, other_out, m_w_out[0], v_w_out[0], "adamw_out"))
    d_cw, nm_cw, nv_cw = (t[None] for t in _adamw(conv_w[0], grad_conv_w, m_conv_w[0], v_conv_w[0], "adamw_conv_w"))

    def rows(ws):
        return jnp.concatenate([jnp.pad(t.reshape(1, -1), ((0, 0), (0, D - t.size))) for t in ws], axis=0)

    small_w = [norm_g, b_forget, q_norm_g, k_norm_g, conv_b, conv_ln_g, conv_ln_b]
    small_m = [m_norm_g, m_b_forget, m_q_norm_g, m_k_norm_g, m_conv_b, m_conv_ln_g, m_conv_ln_b]
    small_v = [v_norm_g, v_b_forget, v_q_norm_g, v_k_norm_g, v_conv_b, v_conv_ln_g, v_conv_ln_b]
    d_s, nm_s, nv_s = _adamw(rows(small_w), small_sum[0:7], rows(small_m), rows(small_v), "adamw_small")

    def unpack(t):
        return [t[k:k + 1, :w.size].reshape(w.shape) for k, w in enumerate(small_w)]

    def order(s, in_, cw, out_):
        ng, bf, qg, kg, cb, lg, lb = s
        return [ng, in_, bf, qg, kg, cw, cb, lg, lb, out_]

    grads = [grad_norm_g, grad_w_in, grad_b_f, grad_gq, grad_gk, grad_conv_w[None],
             grad_conv_b, grad_ln_g, grad_ln_b, grad_w_out]
    return (loss, grad_x2.reshape(B, S, D), *grads,
            *order(unpack(d_s), d_in, d_cw, d_out),
            *order(unpack(nm_s), nm_in, nm_cw, nm_out),
            *order(unpack(nv_s), nv_in, nv_cw, nv_out))
```

```python
import jax
import jax.numpy as jnp
from jax import lax
from jax.experimental import pallas as pl
from jax.experimental.pallas import tpu as pltpu

F32 = jnp.float32
BF16 = jnp.bfloat16
SDS = jax.ShapeDtypeStruct
MESH = pl.DeviceIdType.MESH

EPS = 1e-6
NEG_INF = -1e30
LANES = 128
SUBLANES = 8
HEAD_DIM = 64
HALO = 32
VMEM_LIMIT = 56 * 1024 * 1024

L_ROWSUM = 64
L_KDECAY = 67
L_LSE = 70
L_D = 65
NORM_SLACK = 1.02
SHIFT_MAX = 40.0

ADAM_LR = 0.001
ADAM_B1 = 0.9
ADAM_B2 = 0.999
ADAM_EPS = 1e-08
ADAM_WD = 0.01
ADAM_STEP = 10


def _params(sem, vmem=VMEM_LIMIT):
    return pltpu.CompilerParams(dimension_semantics=sem, vmem_limit_bytes=vmem)


def _sigmoid(x):
    return 1.0 / (1.0 + jnp.exp(-x))


def _split3(x):
    hi = x.astype(BF16).astype(F32)
    r = x - hi
    mid = r.astype(BF16).astype(F32)
    lo = (r - mid).astype(BF16).astype(F32)
    return hi, mid, lo


def _dot(a, b):
    return jnp.dot(a, b, preferred_element_type=F32)


def _dot_nt(a, b):
    return lax.dot_general(a, b, (((1,), (1,)), ((), ())), preferred_element_type=F32)


def _dot_tn(a, b):
    return lax.dot_general(a, b, (((0,), (0,)), ((), ())), preferred_element_type=F32)


def _lane(shape):
    return lax.broadcasted_iota(jnp.int32, shape, 1)


def _lane_col(x, lane, idx):
    return jnp.sum(jnp.where(lane == idx, x, 0.0), axis=-1, keepdims=True)


def _put3(base, lane, start, pieces):
    out = base
    for k, p in enumerate(pieces):
        out = jnp.where(lane == start + k, p, out)
    return out


def _half_stats(t):
    hi = t.astype(BF16)
    mid = (t - hi.astype(F32)).astype(BF16)
    row = lax.broadcasted_iota(jnp.int32, (2 * LANES, LANES), 0)
    col = lax.broadcasted_iota(jnp.int32, (2 * LANES, LANES), 1)
    same_half = (jnp.bitwise_and(row, LANES - 1) < HEAD_DIM) == (col < HEAD_DIM)
    return _dot(jnp.concatenate([hi, mid], axis=1), jnp.where(same_half, 1.0, 0.0).astype(BF16))


def _fwd_in(x2, g, w_t, tm, tn):
    T, D = x2.shape
    N = w_t.shape[0]

    def body(x_ref, g_ref, w_ref, z_ref, h_ref):
        @pl.when(pl.program_id(1) == 0)
        def _():
            x = x_ref[...]
            r = lax.rsqrt(jnp.mean(x * x, axis=-1, keepdims=True) + EPS)
            h_ref[...] = (x * r * g_ref[...]).astype(BF16)

        z_ref[...] = _dot_nt(h_ref[...], w_ref[...])

    return pl.pallas_call(
        body, name="fwd_in", grid=(T // tm, N // tn),
        in_specs=[pl.BlockSpec((tm, D), lambda i, j: (i, 0)),
                  pl.BlockSpec((1, D), lambda i, j: (0, 0)),
                  pl.BlockSpec((tn, D), lambda i, j: (j, 0))],
        out_specs=[pl.BlockSpec((tm, tn), lambda i, j: (i, j)),
                   pl.BlockSpec((tm, D), lambda i, j: (i, 0))],
        out_shape=[SDS((T, N), F32), SDS((T, D), BF16)],
        compiler_params=_params(("parallel", "arbitrary")),
    )(x2, g, w_t)


def _tri_cumsum(x, reverse):
    t = x.shape[0]
    row = lax.broadcasted_iota(jnp.int32, (t, t), 0)
    col = lax.broadcasted_iota(jnp.int32, (t, t), 1)
    tri = (row <= col) if reverse else (row >= col)
    tri = jnp.where(tri, 1.0, 0.0).astype(BF16)
    hi, mid, lo = _split3(x)
    return _dot(tri, hi.astype(BF16)) + _dot(tri, mid.astype(BF16)) + _dot(tri, lo.astype(BF16))


def _gate_fwd(z, b_pad, B, S, H, col_blk, tc):
    T = B * S
    nsb = S // tc

    def body(zf_ref, b_ref, c_ref, carry):
        @pl.when(pl.program_id(1) == 0)
        def _():
            carry[...] = jnp.zeros_like(carry)

        x = zf_ref[...] + b_ref[...]
        lf = jnp.minimum(x, 0.0) - jnp.log(1.0 + jnp.exp(-jnp.abs(x)))
        lf = jnp.where(_lane(lf.shape) < H, lf, 0.0)
        c_ref[...] = _tri_cumsum(lf, False) + carry[...]
        carry[...] = carry[...] + jnp.sum(lf, axis=0, keepdims=True)

    return pl.pallas_call(
        body, name="gate_fwd", grid=(B, nsb),
        in_specs=[pl.BlockSpec((tc, LANES), lambda b, s: (b * nsb + s, col_blk)),
                  pl.BlockSpec((1, LANES), lambda b, s: (0, 0))],
        out_specs=pl.BlockSpec((tc, LANES), lambda b, s: (b * nsb + s, 0)),
        out_shape=SDS((T, LANES), F32),
        scratch_shapes=[pltpu.VMEM((1, LANES), F32)],
        compiler_params=_params(("parallel", "arbitrary")),
    )(z, b_pad)


def _qk_normalize(x, g):
    r = lax.rsqrt(_half_stats(x * x) * (1.0 / HEAD_DIM) + EPS)
    return x * r * g


def _attn_prep(z, c, gq, gk, B, S, H, tm):
    T = B * S
    FW = H * HEAD_DIM
    nsb = S // tm
    nfb = FW // LANES
    scale = HEAD_DIM ** -0.5

    def body(zq_ref, zk_ref, zv_ref, c_ref, gq_ref, gk_ref, qa_ref, ka_ref, va_ref):
        p = pl.program_id(1)
        lane = _lane((tm, LANES))
        lo = lane < HEAD_DIM
        qn = _qk_normalize(zq_ref[...], gq_ref[...]) * scale
        kn = _qk_normalize(zk_ref[...], gk_ref[...])
        v = zv_ref[...]
        cc = c_ref[...]
        ones_q = ((lane >= L_KDECAY) & (lane < L_KDECAY + 3)).astype(F32)
        ones_k = (((lane >= L_ROWSUM) & (lane < L_ROWSUM + 3)) | ((lane >= L_LSE) & (lane < L_LSE + 3))).astype(F32)
        ones_v = ((lane >= L_ROWSUM) & (lane < L_D + 3)).astype(F32)
        for e in range(2):
            if e == 0:
                qe, ke, ve = qn, kn, v
            else:
                qe, ke, ve = (pltpu.roll(t, HEAD_DIM, 1) for t in (qn, kn, v))
            ch = _lane_col(cc, lane, 2 * p + e)
            pieces = _split3(ch)
            qa = jnp.where(lo, qe, _put3(ones_q, lane, L_ROWSUM, pieces))
            ka = jnp.where(lo, ke, _put3(ones_k, lane, L_KDECAY, [-t for t in pieces]))
            va = jnp.where(lo, ve, ones_v)
            qa_ref[0, e] = qa.astype(BF16)
            ka_ref[0, e] = ka.astype(BF16)
            va_ref[0, e] = va.astype(BF16)

    zspec = lambda off: pl.BlockSpec((tm, LANES), lambda i, p: (i, off + p))
    gspec = pl.BlockSpec((1, LANES), lambda i, p: (0, p))
    ospec = pl.BlockSpec((1, 2, tm, LANES), lambda i, p: (i // nsb, p, i % nsb, 0))
    oshape = SDS((B, H, S, LANES), BF16)
    return pl.pallas_call(
        body, name="attn_prep", grid=(T // tm, H // 2),
        in_specs=[zspec(0), zspec(nfb), zspec(2 * nfb),
                  pl.BlockSpec((tm, LANES), lambda i, p: (i, 0)), gspec, gspec],
        out_specs=[ospec, ospec, ospec],
        out_shape=[oshape, oshape, oshape],
        compiler_params=_params(("parallel", "arbitrary")),
    )(z, z, z, c, gq, gk)


def _attn_fwd(qa, ka, va, bound, t, hb):
    B, H, S, _ = qa.shape
    nq = S // t

    def body(q_ref, k_ref, v_ref, b_ref, o_ref, m_ref, acc_ref, qs_ref):
        i = pl.program_id(2)
        lane = _lane((t, LANES))

        shifts = [b_ref[e] for e in range(hb)]
        worst = shifts[0]
        for e in range(1, hb):
            worst = jnp.maximum(worst, shifts[e])
        bounded = jnp.max(worst) <= SHIFT_MAX
        acc_ref[...] = jnp.zeros_like(acc_ref)

        def tiles(step):
            def loop_body(j, carry):
                step(j, False)
                return carry

            lax.fori_loop(0, i, loop_body, 0)
            step(i, True)

        def keep_mask(n=t):
            return lax.broadcasted_iota(jnp.int32, (n, n), 0) >= lax.broadcasted_iota(jnp.int32, (n, n), 1)

        def finish(e, shift):
            acc = acc_ref[e]
            l = _lane_col(acc, lane, L_ROWSUM)
            o_ref[0, e] = jnp.where(lane < HEAD_DIM, acc / l, shift + jnp.log(l))

        @pl.when(bounded)
        def _():
            for e in range(hb):
                qs_ref[e] = _put3(q_ref[0, e].astype(F32), lane, L_LSE, _split3(-shifts[e])).astype(BF16)

            def pair(e, q_rows, k_start, n, masked):
                k_rows = pl.ds(pl.multiple_of(k_start, n), n)
                p = jnp.exp(_dot_nt(qs_ref[e, q_rows, :], k_ref[0, e, k_rows, :]))
                if masked:
                    p = jnp.where(keep_mask(n), p, 0.0)
                acc_ref[e, q_rows, :] = acc_ref[e, q_rows, :] + _dot(p.astype(BF16), v_ref[0, e, k_rows, :])

            def step(j, masked):
                for e in range(hb):
                    if masked:
                        h = t // 2
                        pair(e, slice(0, h), j * t, h, True)
                        pair(e, slice(h, t), j * t, h, False)
                        pair(e, slice(h, t), j * t + h, h, True)
                    else:
                        pair(e, slice(0, t), j * t, t, False)

            tiles(step)
            for e in range(hb):
                finish(e, shifts[e])

        @pl.when(jnp.logical_not(bounded))
        def _():
            m_ref[...] = jnp.full_like(m_ref, NEG_INF)

            def step(j, masked):
                rows = pl.ds(pl.multiple_of(j * t, t), t)
                for e in range(hb):
                    s = _dot_nt(q_ref[0, e], k_ref[0, e, rows, :])
                    if masked:
                        s = jnp.where(keep_mask(), s, NEG_INF)
                    m_prev = m_ref[e]
                    m_new = jnp.maximum(m_prev, jnp.max(s, axis=-1, keepdims=True))
                    alpha = jnp.exp(m_prev - m_new)
                    p = jnp.exp(s - m_new).astype(BF16)
                    acc_ref[e] = alpha * acc_ref[e] + _dot(p, v_ref[0, e, rows, :])
                    m_ref[e] = m_new

            tiles(step)
            for e in range(hb):
                finish(e, m_ref[e])

    return pl.pallas_call(
        body, name="attn_fwd", grid=(B, H // hb, nq),
        in_specs=[pl.BlockSpec((1, hb, t, LANES), lambda b, h, i: (b, h, i, 0)),
                  pl.BlockSpec((1, hb, S, LANES), lambda b, h, i: (b, h, 0, 0)),
                  pl.BlockSpec((1, hb, S, LANES), lambda b, h, i: (b, h, 0, 0)),
                  pl.BlockSpec((hb, 1, LANES), lambda b, h, i: (h, 0, 0))],
        out_specs=pl.BlockSpec((1, hb, t, LANES), lambda b, h, i: (b, h, i, 0)),
        out_shape=SDS((B, H, S, LANES), F32),
        scratch_shapes=[pltpu.VMEM((hb, t, 1), F32), pltpu.VMEM((hb, t, LANES), F32),
                        pltpu.VMEM((hb, t, LANES), BF16)],
        compiler_params=_params(("parallel", "parallel", "arbitrary")),
    )(qa, ka, va, bound)


def _fill_shifts(ext_ref, sh_ref):
    rows = sh_ref.shape[1]
    for b in range(1, SUBLANES):
        sh_ref[b - 1] = ext_ref[pl.ds(b, rows), :]


def _tap_window(ext_ref, sh_ref, off, tm, cols):
    b = off % SUBLANES
    if b == 0:
        return ext_ref[pl.ds(off, tm), cols]
    return sh_ref[b - 1, pl.ds(off - b, tm), cols]


def _conv_taps(w_ref, ext_ref, sh_ref, out_ref, n_taps, tm, offset_of, bias_ref=None):
    for cc in range(out_ref.shape[1] // LANES):
        cols = slice(cc * LANES, (cc + 1) * LANES)
        acc = None
        for j in sorted(range(n_taps), key=offset_of):
            term = w_ref[j:j + 1, cols] * _tap_window(ext_ref, sh_ref, offset_of(j), tm, cols)
            acc = term if acc is None else acc + term
        out_ref[:, cols] = acc if bias_ref is None else acc + bias_ref[:, cols]


def _layernorm_stats(u2):
    mu = jnp.mean(u2, axis=-1, keepdims=True)
    xc = u2 - mu
    rstd = lax.rsqrt(jnp.mean(xc * xc, axis=-1, keepdims=True) + EPS)
    return xc * rstd, rstd


def _fwd_out(oa, z, x2, tgt, conv_w, conv_b, ln_g, ln_b, w_out, B, S, H, n_taps, tm):
    T, D = x2.shape
    FW = H * HEAD_DIM
    CW = conv_w.shape[1]
    nsb = S // tm
    hb = tm // HALO
    mb = 4 if nsb % 4 == 0 else 1
    mt = mb * tm

    def body(oa_ref, gf_ref, ga_ref, gb_ref, gc_ref, ha_ref, hb_ref, x_ref, t_ref, w_ref, cb_ref, lg_ref,
             lb_ref, wo_ref, y_ref, u2_ref, a_ref, do_ref, dob_ref, dy_ref, loss_ref, ext_ref, sh_ref):
        first_step = (pl.program_id(0) == 0) & (pl.program_id(1) == 0)
        sub = lax.rem(pl.program_id(1), mb)
        rows = pl.ds(pl.multiple_of(sub * tm, tm), tm)

        @pl.when(first_step)
        def _():
            loss_ref[...] = jnp.zeros_like(loss_ref)

        u1 = ga_ref[...] * _sigmoid(gb_ref[...])
        halo = ha_ref[...] * _sigmoid(hb_ref[...])
        ext_ref[0:HALO, :] = jnp.where(pl.program_id(1) > 0, halo, 0.0)
        ext_ref[HALO:, :] = u1
        _fill_shifts(ext_ref, sh_ref)
        _conv_taps(w_ref, ext_ref, sh_ref, u2_ref, n_taps, tm, lambda j: HALO - (n_taps - 1) + j, cb_ref)
        uh, _ = _layernorm_stats(u2_ref[...])
        u3 = uh * lg_ref[...] + lb_ref[...]
        gc = gc_ref[...]
        yu = u3 * _sigmoid(u3) * (gc * _sigmoid(gc))
        y_ref[rows, FW:] = yu.astype(BF16)

        lane = _lane((tm, LANES))
        lo = lane < HEAD_DIM
        for p in range(H // 2):
            a_ref[:, p * LANES:(p + 1) * LANES] = jnp.where(
                lo, oa_ref[0, 2 * p], pltpu.roll(oa_ref[0, 2 * p + 1], HEAD_DIM, 1))
        gf = gf_ref[...]
        y_ref[rows, :FW] = (a_ref[...] * (gf * _sigmoid(gf))).astype(BF16)

        @pl.when(sub == mb - 1)
        def _():
            out = x_ref[...] + _dot(y_ref[...], wo_ref[...])
            diff = out - t_ref[...]
            loss_ref[...] = loss_ref[...] + jnp.sum(diff * diff)
            dout = diff * (1.0 / D)
            do_ref[...] = dout
            dob = dout.astype(BF16)
            dob_ref[...] = dob
            dy_ref[...] = _dot_nt(dob, wo_ref[...])

    row = lambda b, s: b * nsb + s
    zspec = lambda cb: pl.BlockSpec((tm, FW), lambda b, s: (row(b, s), cb))
    hspec = lambda cb: pl.BlockSpec((HALO, CW), lambda b, s: (jnp.maximum(row(b, s) * hb - 1, 0), cb))
    vspec = pl.BlockSpec((1, CW), lambda b, s: (0, 0))
    tspec = lambda w: pl.BlockSpec((tm, w), lambda b, s: (row(b, s), 0))
    mspec = lambda w: pl.BlockSpec((mt, w), lambda b, s: (row(b, s) // mb, 0))
    return pl.pallas_call(
        body, name="fwd_out", grid=(B, nsb),
        in_specs=[pl.BlockSpec((1, H, tm, LANES), lambda b, s: (b, 0, s, 0)),
                  zspec(3), zspec(4), zspec(5), zspec(6), hspec(4), hspec(5),
                  mspec(D), mspec(D),
                  pl.BlockSpec((HALO, CW), lambda b, s: (0, 0)), vspec, vspec, vspec,
                  pl.BlockSpec((FW + CW, D), lambda b, s: (0, 0))],
        out_specs=[mspec(FW + CW), tspec(CW), tspec(FW), mspec(D), mspec(D), mspec(FW + CW),
                   pl.BlockSpec((8, LANES), lambda b, s: (0, 0))],
        out_shape=[SDS((T, FW + CW), BF16), SDS((T, CW), F32), SDS((T, FW), F32), SDS((T, D), F32),
                   SDS((T, D), BF16), SDS((T, FW + CW), F32), SDS((8, LANES), F32)],
        scratch_shapes=[pltpu.VMEM((tm + HALO, CW), F32),
                        pltpu.VMEM((SUBLANES - 1, tm + HALO - SUBLANES, CW), F32)],
        compiler_params=_params(("arbitrary", "arbitrary")),
    )(oa, z, z, z, z, z, z, x2, tgt, conv_w, conv_b, ln_g, ln_b, w_out)


def _bwd_prep(dy, z, a_nat, oa, qa, u2, ln_g, ln_b, B, S, H, tm):
    T = B * S
    FW = H * HEAD_DIM
    CW = u2.shape[1]
    nsb = S // tm

    def body(dya_ref, dyu_ref, gf_ref, gc_ref, a_ref, oa_ref, qa_ref, u2_ref, lg_ref, lb_ref,
             dzgf_ref, dzgc_ref, du2_ref, doa_ref, qb_ref, sg_ref):
        first_step = (pl.program_id(0) == 0) & (pl.program_id(1) == 0)

        @pl.when(first_step)
        def _():
            sg_ref[...] = jnp.zeros_like(sg_ref)

        gf = gf_ref[...]
        sg = _sigmoid(gf)
        a = a_ref[...]
        dya = dya_ref[...]
        da = dya * (gf * sg)
        dzgf_ref[...] = (dya * a * (sg * (1.0 + gf * (1.0 - sg)))).astype(BF16)
        dd = da * a
        lane = _lane((tm, LANES))
        lo = lane < HEAD_DIM
        for p in range(H // 2):
            cols = slice(p * LANES, (p + 1) * LANES)
            da_p = da[:, cols]
            dd_p = dd[:, cols]
            d_heads = (jnp.sum(jnp.where(lo, dd_p, 0.0), axis=-1, keepdims=True),
                       jnp.sum(jnp.where(lo, 0.0, dd_p), axis=-1, keepdims=True))
            for e in range(2):
                da_e = da_p if e == 0 else pltpu.roll(da_p, HEAD_DIM, 1)
                d_e = d_heads[e]
                aug = _put3(jnp.zeros((tm, LANES), F32), lane, L_D, _split3(-d_e))
                doa_ref[0, 2 * p + e] = jnp.where(lo, da_e, aug).astype(BF16)
                lse = _lane_col(oa_ref[0, 2 * p + e], lane, L_ROWSUM)
                qb = _put3(qa_ref[0, 2 * p + e].astype(F32), lane, L_LSE, _split3(-lse))
                qb_ref[0, 2 * p + e] = qb.astype(BF16)

        gc = gc_ref[...]
        sc = _sigmoid(gc)
        dyu = dyu_ref[...]
        uh, rstd = _layernorm_stats(u2_ref[...])
        u3 = uh * lg_ref[...] + lb_ref[...]
        s3 = _sigmoid(u3)
        dzgc_ref[...] = (dyu * (u3 * s3) * (sc * (1.0 + gc * (1.0 - sc)))).astype(BF16)
        du3 = dyu * (gc * sc) * (s3 * (1.0 + u3 * (1.0 - s3)))
        sg_ref[0:1, :] = sg_ref[0:1, :] + jnp.sum(du3 * uh, axis=0, keepdims=True)
        sg_ref[1:2, :] = sg_ref[1:2, :] + jnp.sum(du3, axis=0, keepdims=True)
        duh = du3 * lg_ref[...]
        du2 = rstd * (duh - jnp.mean(duh, axis=-1, keepdims=True)
                      - uh * jnp.mean(duh * uh, axis=-1, keepdims=True))
        sg_ref[2:3, :] = sg_ref[2:3, :] + jnp.sum(du2, axis=0, keepdims=True)
        du2_ref[...] = du2

    row = lambda b, s: b * nsb + s
    tspec = lambda w, cb=0: pl.BlockSpec((tm, w), lambda b, s: (row(b, s), cb))
    hspec = pl.BlockSpec((1, H, tm, LANES), lambda b, s: (b, 0, s, 0))
    vspec = pl.BlockSpec((1, CW), lambda b, s: (0, 0))
    return pl.pallas_call(
        body, name="bwd_prep", grid=(B, nsb),
        in_specs=[tspec(FW, 0), tspec(CW, 1), tspec(FW, 3), tspec(CW, 6), tspec(FW), hspec, hspec,
                  tspec(CW), vspec, vspec],
        out_specs=[tspec(FW), tspec(CW), tspec(CW), hspec, hspec,
                   pl.BlockSpec((8, CW), lambda b, s: (0, 0))],
        out_shape=[SDS((T, FW), BF16), SDS((T, CW), BF16), SDS((T, CW), F32),
                   SDS((B, H, S, LANES), BF16), SDS((B, H, S, LANES), BF16), SDS((8, CW), F32)],
        compiler_params=_params(("arbitrary", "arbitrary")),
    )(dy, dy, z, z, a_nat, oa, qa, u2, ln_g, ln_b)


def _conv_bwd(du2, z, conv_w, B, S, n_taps, tm):
    T, CW = du2.shape
    nsb = S // tm
    hb = tm // HALO

    def body(d_ref, dh_ref, ga_ref, gb_ref, ha_ref, hb_ref, w_ref, dz_ref, dw_ref,
             extu_ref, extd_ref, shu_ref, shd_ref, du1_ref, dwacc_ref):
        s = pl.program_id(1)
        first_step = (pl.program_id(0) == 0) & (s == 0)
        last_step = (pl.program_id(0) == B - 1) & (s == nsb - 1)

        @pl.when(first_step)
        def _():
            dwacc_ref[...] = jnp.zeros_like(dwacc_ref)

        ga = ga_ref[...]
        sb = _sigmoid(gb_ref[...])
        halo = ha_ref[...] * _sigmoid(hb_ref[...])
        extu_ref[0:HALO, :] = jnp.where(s > 0, halo, 0.0)
        extu_ref[HALO:, :] = ga * sb
        extd_ref[0:tm, :] = d_ref[...]
        extd_ref[tm:, :] = jnp.where(s < nsb - 1, dh_ref[...], 0.0)
        _fill_shifts(extu_ref, shu_ref)
        _fill_shifts(extd_ref, shd_ref)
        _conv_taps(w_ref, extd_ref, shd_ref, du1_ref, n_taps, tm, lambda j: n_taps - 1 - j)
        for cc in range(CW // LANES):
            cols = slice(cc * LANES, (cc + 1) * LANES)
            parts = [None] * n_taps
            for r in range(tm // SUBLANES):
                dv = d_ref[r * SUBLANES:(r + 1) * SUBLANES, cols]
                for j in range(n_taps):
                    off = HALO - (n_taps - 1) + j + r * SUBLANES
                    term = dv * _tap_window(extu_ref, shu_ref, off, SUBLANES, cols)
                    parts[j] = term if parts[j] is None else parts[j] + term
            for j in range(n_taps):
                rows = slice(j * SUBLANES, (j + 1) * SUBLANES)
                dwacc_ref[rows, cols] = dwacc_ref[rows, cols] + parts[j]
        du1 = du1_ref[...]
        dz_ref[:, :CW] = (du1 * sb).astype(BF16)
        dz_ref[:, CW:] = (du1 * ga * (sb * (1.0 - sb))).astype(BF16)

        @pl.when(last_step)
        def _():
            dw_ref[...] = jnp.zeros_like(dw_ref)
            for j in range(n_taps):
                dw_ref[j:j + 1, :] = jnp.sum(dwacc_ref[j * SUBLANES:(j + 1) * SUBLANES, :], axis=0, keepdims=True)

    row = lambda b, s: b * nsb + s
    last_halo = T // HALO - 1
    return pl.pallas_call(
        body, name="conv_bwd", grid=(B, nsb),
        in_specs=[pl.BlockSpec((tm, CW), lambda b, s: (row(b, s), 0)),
                  pl.BlockSpec((HALO, CW), lambda b, s: (jnp.minimum((row(b, s) + 1) * hb, last_halo), 0)),
                  pl.BlockSpec((tm, CW), lambda b, s: (row(b, s), 4)),
                  pl.BlockSpec((tm, CW), lambda b, s: (row(b, s), 5)),
                  pl.BlockSpec((HALO, CW), lambda b, s: (jnp.maximum(row(b, s) * hb - 1, 0), 4)),
                  pl.BlockSpec((HALO, CW), lambda b, s: (jnp.maximum(row(b, s) * hb - 1, 0), 5)),
                  pl.BlockSpec((HALO, CW), lambda b, s: (0, 0))],
        out_specs=[pl.BlockSpec((tm, 2 * CW), lambda b, s: (row(b, s), 0)),
                   pl.BlockSpec((HALO, CW), lambda b, s: (0, 0))],
        out_shape=[SDS((T, 2 * CW), BF16), SDS((HALO, CW), F32)],
        scratch_shapes=[pltpu.VMEM((tm + HALO, CW), F32), pltpu.VMEM((tm + HALO, CW), F32),
                        pltpu.VMEM((SUBLANES - 1, tm + HALO - SUBLANES, CW), F32),
                        pltpu.VMEM((SUBLANES - 1, tm + HALO - SUBLANES, CW), F32),
                        pltpu.VMEM((tm, CW), F32), pltpu.VMEM((HALO * SUBLANES, CW), F32)],
        compiler_params=_params(("arbitrary", "arbitrary")),
    )(du2, du2, z, z, z, z, conv_w)


def _attn_bwd(qb, ka, va, doa, t, hb):
    B, H, S, _ = qb.shape
    nk = S // t

    def body(q_ref, k_ref, v_ref, do_ref, dq_ref, dk_ref, dv_ref, dv_acc):
        j = pl.program_id(2)

        @pl.when(j == 0)
        def _():
            dq_ref[...] = jnp.zeros_like(dq_ref)

        dk_ref[...] = jnp.zeros_like(dk_ref)
        dv_acc[...] = jnp.zeros_like(dv_acc)

        def step(i, masked):
            q_rows = pl.ds(pl.multiple_of(i * t, t), t)
            for e in range(hb):
                k = k_ref[0, e]
                q = q_ref[0, e, q_rows, :]
                do = do_ref[0, e, q_rows, :]
                p = jnp.exp(_dot_nt(q, k))
                if masked:
                    keep = lax.broadcasted_iota(jnp.int32, (t, t), 0) >= lax.broadcasted_iota(jnp.int32, (t, t), 1)
                    p = jnp.where(keep, p, 0.0)
                ds = (p * _dot_nt(do, v_ref[0, e])).astype(BF16)
                dv_acc[e] = dv_acc[e] + _dot_tn(p.astype(BF16), do)
                dk_ref[0, e] = dk_ref[0, e] + _dot_tn(ds, q)
                dq_ref[0, e, q_rows, :] = dq_ref[0, e, q_rows, :] + _dot(ds, k)

        step(j, True)

        def loop_body(i, carry):
            step(i, False)
            return carry

        lax.fori_loop(j + 1, nk, loop_body, 0)
        dv_ref[0] = dv_acc[...].astype(BF16)

    full = pl.BlockSpec((1, hb, S, LANES), lambda b, h, j: (b, h, 0, 0))
    blk = pl.BlockSpec((1, hb, t, LANES), lambda b, h, j: (b, h, j, 0))
    oshape = SDS((B, H, S, LANES), F32)
    return pl.pallas_call(
        body, name="attn_bwd", grid=(B, H // hb, nk),
        in_specs=[full, blk, blk, full],
        out_specs=[full, blk, blk],
        out_shape=[oshape, oshape, SDS((B, H, S, LANES), BF16)],
        scratch_shapes=[pltpu.VMEM((hb, t, LANES), F32)],
        compiler_params=_params(("parallel", "parallel", "arbitrary")),
    )(qb, ka, va, doa)


def _qk_bwd(dqa, dka, dva, z, gq, gk, B, S, H, tm):
    T = B * S
    FW = H * HEAD_DIM
    nsb = S // tm
    nfb = FW // LANES
    scale = HEAD_DIM ** -0.5

    def body(dq_ref, dk_ref, dv_ref, zq_ref, zk_ref, gq_ref, gk_ref, dzq_ref, dzk_ref, dzv_ref, dc_ref, dg_ref):
        p = pl.program_id(0)

        @pl.when(pl.program_id(1) == 0)
        def _():
            dg_ref[...] = jnp.zeros_like(dg_ref)

        lane = _lane((tm, LANES))
        lo = lane < HEAD_DIM

        def natural(ref):
            return jnp.where(lo, ref[0, 0].astype(F32), pltpu.roll(ref[0, 1].astype(F32), HEAD_DIM, 1))

        def norm_bwd(dn, x, g, row, out_ref):
            r = lax.rsqrt(_half_stats(x * x) * (1.0 / HEAD_DIM) + EPS)
            xh = x * r
            dg_ref[row:row + 1, :] = dg_ref[row:row + 1, :] + jnp.sum(dn * xh, axis=0, keepdims=True)
            dxh = dn * g
            mm = _half_stats(dxh * xh) * (1.0 / HEAD_DIM)
            out_ref[...] = (r * (dxh - xh * mm)).astype(BF16)

        norm_bwd(natural(dq_ref) * scale, zq_ref[...], gq_ref[...], 0, dzq_ref)
        norm_bwd(natural(dk_ref), zk_ref[...], gk_ref[...], 1, dzk_ref)
        dzv_ref[...] = natural(dv_ref).astype(BF16)

        dc = jnp.zeros((tm, LANES), F32)
        for e in range(2):
            val = _lane_col(dq_ref[0, e], lane, L_ROWSUM) - _lane_col(dk_ref[0, e], lane, L_KDECAY)
            dc = jnp.where(lane == 2 * p + e, val, dc)
        dc_ref[0] = dc

    hspec = pl.BlockSpec((1, 2, tm, LANES), lambda p, i: (i // nsb, p, i % nsb, 0))
    zspec = lambda off: pl.BlockSpec((tm, LANES), lambda p, i: (i, off + p))
    gspec = pl.BlockSpec((1, LANES), lambda p, i: (0, p))
    ospec = pl.BlockSpec((tm, LANES), lambda p, i: (i, p))
    return pl.pallas_call(
        body, name="qk_bwd", grid=(H // 2, T // tm),
        in_specs=[hspec, hspec, hspec, zspec(0), zspec(nfb), gspec, gspec],
        out_specs=[ospec, ospec, ospec,
                   pl.BlockSpec((1, tm, LANES), lambda p, i: (p, i, 0)),
                   pl.BlockSpec((8, LANES), lambda p, i: (0, p))],
        out_shape=[SDS((T, FW), BF16), SDS((T, FW), BF16), SDS((T, FW), BF16),
                   SDS((H // 2, T, LANES), F32), SDS((8, FW), F32)],
        compiler_params=_params(("parallel", "arbitrary")),
    )(dqa, dka, dva, z, z, gq, gk)


def _gate_bwd(dc8, z, b_pad, B, S, H, col_blk, fp, tc):
    T = B * S
    nsb = S // tc
    npair = dc8.shape[0]

    def body(dc_ref, zf_ref, b_ref, dz_ref, db_ref, carry):
        first_step = (pl.program_id(0) == 0) & (pl.program_id(1) == 0)

        @pl.when(first_step)
        def _():
            db_ref[...] = jnp.zeros_like(db_ref)

        @pl.when(pl.program_id(1) == 0)
        def _():
            carry[...] = jnp.zeros_like(carry)

        dc = dc_ref[0]
        for k in range(1, npair):
            dc = dc + dc_ref[k]
        dlf = _tri_cumsum(dc, True) + carry[...]
        carry[...] = carry[...] + jnp.sum(dc, axis=0, keepdims=True)
        x = zf_ref[...] + b_ref[...]
        dlogit = dlf * _sigmoid(-x)
        db_ref[0:1, :] = db_ref[0:1, :] + jnp.sum(dlogit, axis=0, keepdims=True)
        dz_ref[...] = jnp.zeros_like(dz_ref)
        dz_ref[:, :LANES] = dlogit.astype(BF16)

    rrow = lambda b, s: b * nsb + (nsb - 1 - s)
    return pl.pallas_call(
        body, name="gate_bwd", grid=(B, nsb),
        in_specs=[pl.BlockSpec((npair, tc, LANES), lambda b, s: (0, rrow(b, s), 0)),
                  pl.BlockSpec((tc, LANES), lambda b, s: (rrow(b, s), col_blk)),
                  pl.BlockSpec((1, LANES), lambda b, s: (0, 0))],
        out_specs=[pl.BlockSpec((tc, fp), lambda b, s: (rrow(b, s), 0)),
                   pl.BlockSpec((8, LANES), lambda b, s: (0, 0))],
        out_shape=[SDS((T, fp), BF16), SDS((8, LANES), F32)],
        scratch_shapes=[pltpu.VMEM((1, LANES), F32)],
        compiler_params=_params(("arbitrary", "arbitrary")),
    )(dc8, z, b_pad)


def _matmul_tn(a, b, name, tmm, tn, tk):
    T, M = a.shape
    N = b.shape[1]
    tmm, tn, tk = min(tmm, M), min(tn, N), min(tk, T)

    def body(a_ref, b_ref, o_ref):
        @pl.when(pl.program_id(2) == 0)
        def _():
            o_ref[...] = jnp.zeros_like(o_ref)

        o_ref[...] = o_ref[...] + _dot_tn(a_ref[...], b_ref[...])

    return pl.pallas_call(
        body, name=name, grid=(M // tmm, N // tn, T // tk),
        in_specs=[pl.BlockSpec((tk, tmm), lambda i, j, k: (k, i)),
                  pl.BlockSpec((tk, tn), lambda i, j, k: (k, j))],
        out_specs=pl.BlockSpec((tmm, tn), lambda i, j, k: (i, j)),
        out_shape=SDS((M, N), F32),
        compiler_params=_params(("parallel", "parallel", "arbitrary")),
    )(a, b)


def _dh_rms_bwd(pieces, w_t, x2, g, dout, tm, tk, parts):
    T, D = x2.shape
    nks = [p.shape[1] // tk for p in pieces]
    starts = [sum(nks[:k]) for k in range(len(pieces))]
    nk = sum(nks)
    ni = T // tm
    n = len(parts)

    def body(*refs):
        dz_refs = refs[:len(pieces)]
        w_ref, x_ref, g_ref, do_ref = refs[len(pieces):len(pieces) + 4]
        part_refs = refs[len(pieces) + 4:len(pieces) + 4 + n]
        gx_ref, dg_ref = refs[len(pieces) + 4 + n:len(pieces) + 6 + n]
        slot_refs = refs[len(pieces) + 6 + n:len(pieces) + 6 + 2 * n]
        acc_ref, send_sems, recv_sems = refs[len(pieces) + 6 + 2 * n:]
        k = pl.program_id(1)
        first_step = (pl.program_id(0) == 0) & (k == 0)
        last_step = (pl.program_id(0) == ni - 1) & (k == nk - 1)
        x, y, c = _place()
        chips = [(1 - x, y), (x, 1 - y), (1 - x, 1 - y)]

        def copy(a, f, to):
            cx, cy = chips[f]
            return pltpu.make_async_remote_copy(
                src_ref=part_refs[a].at[2 * cx + cy], dst_ref=slot_refs[a].at[f],
                send_sem=send_sems.at[a * 3 + f], recv_sem=recv_sems.at[a * 3 + f],
                device_id=to, device_id_type=MESH)

        @pl.when(first_step)
        def _():
            dg_ref[...] = jnp.zeros_like(dg_ref)
            for a in range(n):
                for f in range(3):
                    copy(a, f, (*chips[f], c)).start()

        @pl.when(last_step)
        def _():
            for a in range(n):
                for f in range(3):
                    copy(a, f, (x, y, c)).wait_recv()
            for a in range(n):
                for f in range(3):
                    copy(a, f, (*chips[f], c)).wait_send()

        @pl.when(k == 0)
        def _():
            acc_ref[...] = jnp.zeros_like(acc_ref)

        for dz_ref, st, cnt in zip(dz_refs, starts, nks):
            @pl.when((k >= st) & (k < st + cnt))
            def _(dz_ref=dz_ref):
                acc_ref[...] = acc_ref[...] + _dot(dz_ref[...], w_ref[...])

        @pl.when(k == nk - 1)
        def _():
            x = x_ref[...]
            r = lax.rsqrt(jnp.mean(x * x, axis=-1, keepdims=True) + EPS)
            xh = x * r
            dh = acc_ref[...]
            dg_ref[0:1, :] = dg_ref[0:1, :] + jnp.sum(dh * xh, axis=0, keepdims=True)
            dxn = dh * g_ref[...]
            gx_ref[...] = do_ref[...] + r * (dxn - xh * jnp.mean(dxn * xh, axis=-1, keepdims=True))

    def piece_spec(st, cnt):
        return pl.BlockSpec((tm, tk), lambda i, k: (i, jnp.clip(k - st, 0, cnt - 1)))

    tspec = pl.BlockSpec((tm, D), lambda i, k: (i, 0))
    return pl.pallas_call(
        body, name="dh_rms_bwd", grid=(T // tm, nk),
        in_specs=[piece_spec(st, cnt) for st, cnt in zip(starts, nks)]
        + [pl.BlockSpec((tk, D), lambda i, k: (k, 0)), tspec, pl.BlockSpec((1, D), lambda i, k: (0, 0)), tspec]
        + [ANY] * n,
        out_specs=[tspec, pl.BlockSpec((8, D), lambda i, k: (0, 0))] + [ANY] * n,
        out_shape=[SDS((T, D), F32), SDS((8, D), F32)] + [SDS((3,) + p.shape[1:], p.dtype) for p in parts],
        scratch_shapes=[pltpu.VMEM((tm, D), F32),
                        pltpu.SemaphoreType.DMA((3 * n,)), pltpu.SemaphoreType.DMA((3 * n,))],
        compiler_params=_params(("arbitrary", "arbitrary")),
    )(*pieces, w_t, x2, g, dout, *parts)


def _block_plan(R, C, tr, tc):
    br = min(tr, R)
    if R % br == 0:
        return (br, C), R // br, lambda i: (i, 0)
    bc = min(tc, C)
    assert C % bc == 0
    return (R, bc), C // bc, lambda i: (0, i)


def _ew_call(body, name, ins, n_out, out_dtypes, tr, tc):
    R, C = ins[0].shape
    blk, steps, imap = _block_plan(R, C, tr, tc)
    spec = pl.BlockSpec(blk, imap)
    return pl.pallas_call(
        body, name=name, grid=(steps,),
        in_specs=[spec] * len(ins), out_specs=[spec] * n_out,
        out_shape=[SDS((R, C), dt) for dt in out_dtypes],
        compiler_params=_params(("parallel",)),
    )(*ins)


def _sum_slots(slots, name, first=None, tr=256):
    n, R, C = slots.shape
    blk, steps, imap = _block_plan(R, C, tr, 2 * LANES)
    lead = [] if first is None else [first]

    def body(*refs):
        s_ref, o_ref = refs[-2:]
        acc = refs[0][...].astype(F32) if lead else s_ref[0].astype(F32)
        for k in range(0 if lead else 1, n):
            acc = acc + s_ref[k].astype(F32)
        o_ref[...] = acc

    return pl.pallas_call(
        body, name=name, grid=(steps,),
        in_specs=[pl.BlockSpec(blk, imap)] * len(lead) + [pl.BlockSpec((n,) + blk, lambda i: (0,) + imap(i))],
        out_specs=pl.BlockSpec(blk, imap),
        out_shape=SDS((R, C), F32),
        compiler_params=_params(("parallel",)),
    )(*lead, slots)


def _adamw_update(w, g, m, v):
    nm = ADAM_B1 * m + (1.0 - ADAM_B1) * g
    nv = ADAM_B2 * v + (1.0 - ADAM_B2) * (g * g)
    m_hat = nm / (1.0 - ADAM_B1 ** ADAM_STEP)
    v_hat = nv / (1.0 - ADAM_B2 ** ADAM_STEP)
    return -ADAM_LR * (m_hat / (jnp.sqrt(v_hat) + ADAM_EPS) + ADAM_WD * w), nm, nv


def _adamw(w, g, m, v, name):
    def body(w_ref, g_ref, m_ref, v_ref, d_ref, nm_ref, nv_ref):
        d_ref[...], nm_ref[...], nv_ref[...] = _adamw_update(w_ref[...], g_ref[...], m_ref[...], v_ref[...])

    return _ew_call(body, name, [w, g, m, v], 3, [F32, F32, F32], 128, 2 * LANES)


def _adamw_halves(w, mine, other, m, v, name):
    R, C = w.shape
    half = C // 2
    bc = min(2 * LANES, half)
    per = half // bc

    def body(w_ref, a_ref, b_ref, m_ref, v_ref, g_ref, d_ref, nm_ref, nv_ref):
        g = jnp.where(pl.program_id(0) // per == lax.axis_index("c"), a_ref[...], b_ref[...])
        g_ref[...] = g
        d_ref[...], nm_ref[...], nv_ref[...] = _adamw_update(w_ref[...], g, m_ref[...], v_ref[...])

    full = pl.BlockSpec((R, bc), lambda i: (0, i))
    part = pl.BlockSpec((R, bc), lambda i: (0, i % per))
    return pl.pallas_call(
        body, name=name, grid=(C // bc,),
        in_specs=[full, part, part, full, full], out_specs=[full] * 4,
        out_shape=[SDS((R, C), F32)] * 4,
        compiler_params=_params(("parallel",)),
    )(w, mine, other, m, v)


ANY = pl.BlockSpec(memory_space=pl.ANY)


def _place():
    return lax.axis_index("x"), lax.axis_index("y"), lax.axis_index("c")


def _gather_chips(shards, splits):
    n = len(shards)
    per = 7

    def body(*refs):
        ins, outs = refs[:n], refs[n:2 * n]
        send_sems, recv_sems = refs[2 * n:]
        x, y, c = _place()
        mine = 2 * x + y
        chips = [(1 - x, y), (x, 1 - y), (1 - x, 1 - y)]

        def rows(a, half):
            return pl.ds(0, splits[a]) if half == 0 else pl.ds(splits[a], ins[a].shape[0] - splits[a])

        def copy(a, k, chip_idx, half, to, src=None):
            dst = outs[a].at[chip_idx, rows(a, half)]
            return pltpu.make_async_remote_copy(
                src_ref=dst if src is None else src, dst_ref=dst,
                send_sem=send_sems.at[a * per + k], recv_sem=recv_sems.at[a * per + k],
                device_id=to, device_id_type=MESH)

        def own(a, to):
            return pltpu.make_async_remote_copy(
                src_ref=ins[a], dst_ref=outs[a].at[mine],
                send_sem=send_sems.at[a * per + 6], recv_sem=recv_sems.at[a * per + 6],
                device_id=to, device_id_type=MESH)

        for cc in (0, 1):
            @pl.when(c == cc)
            def _(cc=cc):
                me, sibling = (x, y, cc), (x, y, 1 - cc)
                first = [copy(a, k, mine, cc, (*chip, cc), src=ins[a].at[rows(a, cc)])
                         for a in range(n) for k, chip in enumerate(chips)]
                first += [own(a, sibling) for a in range(n)]
                for cp in first:
                    cp.start()
                passed = []
                for k, (cx, cy) in enumerate(chips):
                    for a in range(n):
                        copy(a, k, 2 * cx + cy, cc, me).wait_recv()
                        fwd = copy(a, 3 + k, 2 * cx + cy, cc, sibling)
                        fwd.start()
                        passed.append(fwd)
                for k, (cx, cy) in enumerate(chips):
                    for a in range(n):
                        copy(a, 3 + k, 2 * cx + cy, 1 - cc, me).wait_recv()
                for a in range(n):
                    own(a, me).wait_recv()
                for cp in first + passed:
                    cp.wait_send()

    return pl.pallas_call(
        body, name="gather_chips",
        in_specs=[ANY] * n, out_specs=[ANY] * n,
        out_shape=[SDS((4,) + s.shape, s.dtype) for s in shards],
        scratch_shapes=[pltpu.SemaphoreType.DMA((per * n,)), pltpu.SemaphoreType.DMA((per * n,))],
    )(*shards)


def _pair_swap(arrs):
    n = len(arrs)

    def body(*refs):
        ins, outs = refs[:n], refs[n:2 * n]
        send_sems, recv_sems = refs[2 * n:]
        x, y, c = _place()
        for cc in (0, 1):
            @pl.when(c == cc)
            def _(cc=cc):
                copies = []
                for a in range(n):
                    half = ins[a].shape[1] // 2
                    copies.append(pltpu.make_async_remote_copy(
                        src_ref=ins[a].at[:, pl.ds((1 - cc) * half, half)], dst_ref=outs[a],
                        send_sem=send_sems.at[a], recv_sem=recv_sems.at[a],
                        device_id=(x, y, 1 - cc), device_id_type=MESH))
                for cp in copies:
                    cp.start()
                for cp in copies:
                    cp.wait()

    return pl.pallas_call(
        body, name="pair_swap",
        in_specs=[ANY] * n, out_specs=[ANY] * n,
        out_shape=[SDS((h.shape[0], h.shape[1] // 2), h.dtype) for h in arrs],
        scratch_shapes=[pltpu.SemaphoreType.DMA((n,)), pltpu.SemaphoreType.DMA((n,))],
    )(*arrs)


def _pair_sum(arrs, got, core, tr):
    half = arrs[0].shape[1] // 2
    cnts = [p.shape[0] // tr for p in arrs]
    starts = [sum(cnts[:k]) for k in range(len(arrs))]

    def body(core_ref, *refs):
        del core_ref
        own_refs, got_refs, o_ref = refs[:len(arrs)], refs[len(arrs):2 * len(arrs)], refs[-1]
        s = pl.program_id(0)
        for own_ref, got_ref, st, cnt in zip(own_refs, got_refs, starts, cnts):
            @pl.when((s >= st) & (s < st + cnt))
            def _(own_ref=own_ref, got_ref=got_ref):
                o_ref[...] = (own_ref[...] + got_ref[...]).astype(BF16)

    def own_spec(st, cnt):
        return pl.BlockSpec((tr, half), lambda s, core_ref: (jnp.clip(s - st, 0, cnt - 1), core_ref[0]))

    def got_spec(st, cnt):
        return pl.BlockSpec((tr, half), lambda s, core_ref: (jnp.clip(s - st, 0, cnt - 1), 0))

    return pl.pallas_call(
        body, name="pair_sum",
        grid_spec=pltpu.PrefetchScalarGridSpec(
            num_scalar_prefetch=1, grid=(sum(cnts),),
            in_specs=[own_spec(st, cnt) for st, cnt in zip(starts, cnts)]
            + [got_spec(st, cnt) for st, cnt in zip(starts, cnts)],
            out_specs=pl.BlockSpec((tr, half), lambda s, core_ref: (s, 0))),
        out_shape=SDS((sum(cnts) * tr, half), BF16),
        compiler_params=_params(("arbitrary",)),
    )(core, *arrs, *got)


def _share_results(arrs, rows):
    n = len(arrs)
    flips = [(fx, fy, fc) for fx in (0, 1) for fy in (0, 1) for fc in (0, 1)][1:]

    def body(*refs):
        ins, rows_ref, outs, all_ref = refs[:n], refs[n], refs[n + 1:2 * n + 1], refs[2 * n + 1]
        send_sems, recv_sems, local_sem = refs[2 * n + 2:]
        x, y, c = _place()
        me = 4 * x + 2 * y + c
        local = pltpu.make_async_copy(rows_ref, all_ref.at[me], local_sem)
        local.start()
        copies = [pltpu.make_async_remote_copy(
            src_ref=ins[a], dst_ref=outs[a], send_sem=send_sems.at[a], recv_sem=recv_sems.at[a],
            device_id=(x, y, 1 - c), device_id_type=MESH) for a in range(n)]
        for k, (fx, fy, fc) in enumerate(flips):
            copies.append(pltpu.make_async_remote_copy(
                src_ref=rows_ref, dst_ref=all_ref.at[me], send_sem=send_sems.at[n + k], recv_sem=recv_sems.at[n + k],
                device_id=(x ^ fx, y ^ fy, c ^ fc), device_id_type=MESH))
        for cp in copies:
            cp.start()
        for cp in copies[:n]:
            cp.wait_recv()
        for k, (fx, fy, fc) in enumerate(flips):
            src = 4 * (x ^ fx) + 2 * (y ^ fy) + (c ^ fc)
            pltpu.make_async_remote_copy(
                src_ref=rows_ref, dst_ref=all_ref.at[src], send_sem=send_sems.at[n + k], recv_sem=recv_sems.at[n + k],
                device_id=(x, y, c), device_id_type=MESH).wait_recv()
        for cp in copies:
            cp.wait_send()
        local.wait()

    outs = pl.pallas_call(
        body, name="share_results",
        in_specs=[ANY] * (n + 1), out_specs=[ANY] * (n + 1),
        out_shape=[SDS(h.shape, h.dtype) for h in arrs] + [SDS((8,) + rows.shape, rows.dtype)],
        scratch_shapes=[pltpu.SemaphoreType.DMA((n + 7,)), pltpu.SemaphoreType.DMA((n + 7,)),
                        pltpu.SemaphoreType.DMA],
    )(*arrs, rows)
    return outs[:n], outs[n]


def _tiles(S, FW):
    big = FW % 512 == 0
    return dict(
        fp=512 if big else LANES,
        tn=2560 if big else LANES,
        tm_in=min(1024, S),
        t_attn=min(512, S),
        hb_fwd=8,
        hb_bwd=8,
        tm_prep=min(2048, S),
        tm_mix=min(128, S),
        tm_bwd=min(256, S),
        tc=min(512, S),
        tk=512 if big else LANES,
    )


def kernel(x, norm_g, w_in, b_forget, q_norm_g, k_norm_g, conv_w, conv_b, conv_ln_g, conv_ln_b, w_out, loss_target, m_norm_g, m_w_in, m_b_forget, m_q_norm_g, m_k_norm_g, m_conv_w, m_conv_b, m_conv_ln_g, m_conv_ln_b, m_w_out, v_norm_g, v_w_in, v_b_forget, v_q_norm_g, v_k_norm_g, v_conv_w, v_conv_b, v_conv_ln_g, v_conv_ln_b, v_w_out):
    B, S, D = x.shape
    H, dh = q_norm_g.shape[1:]
    FW = H * dh
    CW = conv_b.shape[-1]
    n_taps, cw_shard = conv_w.shape[1:]
    in_shard = w_in.shape[2]
    out_shard = w_out.shape[1]
    assert dh == HEAD_DIM and H % 2 == 0 and H <= LANES and FW == CW == D
    assert n_taps - 1 <= HALO and 4 * cw_shard == CW and 4 * out_shard == FW + CW
    assert 4 * in_shard == 4 * FW + 3 * CW + H
    T = B * S
    tl = _tiles(S, FW)
    fp = tl["fp"]
    xi, yi, ci = _place()

    w_t = jnp.transpose(w_in[0])
    conv_pad = jnp.pad(conv_w[0], ((0, HALO - n_taps), (0, 0)))
    bf16_rows = 2 * SUBLANES
    g_in, g_out, g_cw = _gather_chips(
        [w_t.astype(BF16), w_out[0].astype(BF16), conv_pad],
        [in_shard // 2 // bf16_rows * bf16_rows, out_shard // 2, HALO // 2])
    w_t_full = g_in.reshape(4 * in_shard, D)
    w_out_full = g_out.reshape(FW + CW, D)
    conv_full = g_cw.transpose(1, 0, 2).reshape(HALO, CW)
    o_f = 3 * FW
    w_pack = jnp.concatenate([w_t_full[:o_f], w_t_full[o_f + H:],
                              jnp.pad(w_t_full[o_f:o_f + H], ((0, fp - H), (0, 0)))], axis=0)
    f_col = 4 * FW + 3 * CW

    x2 = x.reshape(T, D)
    tgt = loss_target.reshape(T, D)
    b_pad = jnp.pad(b_forget, ((0, 0), (0, LANES - H)))
    gq = q_norm_g.reshape(1, FW)
    gk = k_norm_g.reshape(1, FW)

    z, h = _fwd_in(x2, norm_g, w_pack, tl["tm_in"], tl["tn"])
    c = _gate_fwd(z, b_pad, B, S, H, f_col // LANES, tl["tc"])
    qa, ka, va = _attn_prep(z, c, gq, gk, B, S, H, tl["tm_prep"])
    gain = lambda g: jnp.max(jnp.abs(g[0]), axis=-1)
    bound = (NORM_SLACK ** 2 * dh ** 0.5) * gain(q_norm_g) * gain(k_norm_g)
    oa = _attn_fwd(qa, ka, va, jnp.broadcast_to(bound[:, None, None], (H, 1, LANES)), tl["t_attn"], tl["hb_fwd"])
    y, u2, a_nat, dout, dout_b, dy, loss_acc = _fwd_out(
        oa, z, x2, tgt, conv_full, conv_b, conv_ln_g, conv_ln_b, w_out_full, B, S, H, n_taps, tl["tm_mix"])

    dzgf, dzgc, du2, doa, qb, sg_conv = _bwd_prep(dy, z, a_nat, oa, qa, u2, conv_ln_g, conv_ln_b, B, S, H, tl["tm_bwd"])
    dzglu, dconv_w = _conv_bwd(du2, z, conv_full, B, S, n_taps, tl["tm_mix"])
    dqa, dka, dva = _attn_bwd(qb, ka, va, doa, tl["t_attn"], tl["hb_bwd"])
    dzq, dzk, dzv, dc8, dg_qk = _qk_bwd(dqa, dka, dva, z, gq, gk, B, S, H, tl["tm_prep"])
    dzf, db_f = _gate_bwd(dc8, z, b_pad, B, S, H, f_col // LANES, fp, tl["tc"])
    pieces = [dzq, dzk, dzv, dzgf, dzglu, dzgc, dzf]
    dw_all = [_matmul_tn(p, h, f"dw_in_{k}", 1024, 1024, 1024) for k, p in enumerate(pieces)]
    dw_all.append(_matmul_tn(y, dout_b, "dw_out", 1024, 1024, 1024))

    summed = _pair_sum(dw_all, _pair_swap(dw_all), ci.astype(jnp.int32).reshape(1), fp)
    ends = [0]
    for t in dw_all:
        ends.append(ends[-1] + t.shape[0])
    spans, at = [], 0
    for k, rows in [(0, FW), (1, FW), (2, FW), (6, H), (3, FW), (4, 2 * CW), (5, CW)]:
        spans.append((at, rows, ends[k]))
        at += rows

    def chip_rows(j):
        lo, hi = j * in_shard, (j + 1) * in_shard
        return jnp.concatenate([summed[src + max(lo, a) - a:src + min(hi, a + n) - a]
                                for a, n, src in spans if max(lo, a) < min(hi, a + n)], axis=0)

    part_in = jnp.stack([chip_rows(j) for j in range(4)])
    part_out = summed[ends[7]:ends[8]].reshape(4, out_shard, D // 2)
    grad_x2, dg_norm, slots_in, slots_out = _dh_rms_bwd(
        pieces, w_pack, x2, norm_g, dout, tl["tm_in"], tl["tk"], [part_in, part_out])
    chip = 2 * xi + yi
    half_in = _sum_slots(slots_in, "chip_sum_in", lax.dynamic_index_in_dim(part_in, chip, 0, keepdims=False))
    half_out = _sum_slots(slots_out, "chip_sum_out", lax.dynamic_index_in_dim(part_out, chip, 0, keepdims=False))
    lanes_to_d = lambda t: jnp.pad(t, ((0, 0), (0, D - LANES)))
    small = jnp.concatenate([
        dg_norm[0:1], lanes_to_d(db_f[0:1, :]), dg_qk[0:1], dg_qk[1:2],
        sg_conv[2:3], sg_conv[0:1], sg_conv[1:2], dconv_w, lanes_to_d(loss_acc[0:1, :])], axis=0)
    n_small = small.shape[0]
    (other_in, other_out), all_small = _share_results([half_in, half_out], small)

    small_sum = _sum_slots(all_small, "small_sum", tr=n_small)
    loss = 0.5 * small_sum[n_small - 1, 0] / D
    grad_norm_g, grad_b_f = small_sum[0:1], small_sum[1:2, :H]
    grad_gq, grad_gk = small_sum[2:3].reshape(1, H, dh), small_sum[3:4].reshape(1, H, dh)
    grad_conv_b, grad_ln_g, grad_ln_b = small_sum[4:5], small_sum[5:6], small_sum[6:7]
    grad_conv_w = lax.dynamic_slice_in_dim(small_sum[7:7 + n_taps], chip * cw_shard, cw_shard, axis=1)

    in_t = _adamw_halves(w_t, half_in, other_in, jnp.transpose(m_w_in[0]), jnp.transpose(v_w_in[0]), "adamw_in")
    grad_w_in, d_in, nm_in, nv_in = (jnp.transpose(t)[None] for t in in_t)
    grad_w_out, d_out, nm_out, nv_out = (
        t[None] for t in _adamw_halves(w_out[0], half_out, other_out, m_w_out[0], v_w_out[0], "adamw_out"))
    d_cw, nm_cw, nv_cw = (t[None] for t in _adamw(conv_w[0], grad_conv_w, m_conv_w[0], v_conv_w[0], "adamw_conv_w"))

    def rows(ws):
        return jnp.concatenate([jnp.pad(t.reshape(1, -1), ((0, 0), (0, D - t.size))) for t in ws], axis=0)

    small_w = [norm_g, b_forget, q_norm_g, k_norm_g, conv_b, conv_ln_g, conv_ln_b]
    small_m = [m_norm_g, m_b_forget, m_q_norm_g, m_k_norm_g, m_conv_b, m_conv_ln_g, m_conv_ln_b]
    small_v = [v_norm_g, v_b_forget, v_q_norm_g, v_k_norm_g, v_conv_b, v_conv_ln_g, v_conv_ln_b]
    d_s, nm_s, nv_s = _adamw(rows(small_w), small_sum[0:7], rows(small_m), rows(small_v), "adamw_small")

    def unpack(t):
        return [t[k:k + 1, :w.size].reshape(w.shape) for k, w in enumerate(small_w)]

    def order(s, in_, cw, out_):
        ng, bf, qg, kg, cb, lg, lb = s
        return [ng, in_, bf, qg, kg, cw, cb, lg, lb, out_]

    grads = [grad_norm_g, grad_w_in, grad_b_f, grad_gq, grad_gk, grad_conv_w[None],
             grad_conv_b, grad_ln_g, grad_ln_b, grad_w_out]
    return (loss, grad_x2.reshape(B, S, D), *grads,
            *order(unpack(d_s), d_in, d_cw, d_out),
            *order(unpack(nm_s), nm_in, nm_cw, nm_out),
            *order(unpack(nv_s), nv_in, nv_cw, nv_out))
```

```python
import jax
import jax.numpy as jnp
from jax import lax
from jax.experimental import pallas as pl
from jax.experimental.pallas import tpu as pltpu

F32 = jnp.float32
BF16 = jnp.bfloat16
SDS = jax.ShapeDtypeStruct
MESH = pl.DeviceIdType.MESH

EPS = 1e-6
NEG_INF = -1e30
LANES = 128
SUBLANES = 8
HEAD_DIM = 64
HALO = 32
VMEM_LIMIT = 56 * 1024 * 1024

L_ROWSUM = 64
L_KDECAY = 67
L_LSE = 70
L_D = 65
NORM_SLACK = 1.02
SHIFT_MAX = 40.0

ADAM_LR = 0.001
ADAM_B1 = 0.9
ADAM_B2 = 0.999
ADAM_EPS = 1e-08
ADAM_WD = 0.01
ADAM_STEP = 10


def _params(sem, vmem=VMEM_LIMIT):
    return pltpu.CompilerParams(dimension_semantics=sem, vmem_limit_bytes=vmem)


def _sigmoid(x):
    return 1.0 / (1.0 + jnp.exp(-x))


def _split3(x):
    hi = x.astype(BF16).astype(F32)
    r = x - hi
    mid = r.astype(BF16).astype(F32)
    lo = (r - mid).astype(BF16).astype(F32)
    return hi, mid, lo


def _dot(a, b):
    return jnp.dot(a, b, preferred_element_type=F32)


def _dot_nt(a, b):
    return lax.dot_general(a, b, (((1,), (1,)), ((), ())), preferred_element_type=F32)


def _dot_tn(a, b):
    return lax.dot_general(a, b, (((0,), (0,)), ((), ())), preferred_element_type=F32)


def _lane(shape):
    return lax.broadcasted_iota(jnp.int32, shape, 1)


def _lane_col(x, lane, idx):
    return jnp.sum(jnp.where(lane == idx, x, 0.0), axis=-1, keepdims=True)


def _put3(base, lane, start, pieces):
    out = base
    for k, p in enumerate(pieces):
        out = jnp.where(lane == start + k, p, out)
    return out


def _half_stats(t):
    hi = t.astype(BF16)
    mid = (t - hi.astype(F32)).astype(BF16)
    row = lax.broadcasted_iota(jnp.int32, (2 * LANES, LANES), 0)
    col = lax.broadcasted_iota(jnp.int32, (2 * LANES, LANES), 1)
    same_half = (jnp.bitwise_and(row, LANES - 1) < HEAD_DIM) == (col < HEAD_DIM)
    return _dot(jnp.concatenate([hi, mid], axis=1), jnp.where(same_half, 1.0, 0.0).astype(BF16))


def _fwd_in(x2, g, w_t, tm, tn):
    T, D = x2.shape
    N = w_t.shape[0]

    def body(x_ref, g_ref, w_ref, z_ref, h_ref):
        @pl.when(pl.program_id(1) == 0)
        def _():
            x = x_ref[...]
            r = lax.rsqrt(jnp.mean(x * x, axis=-1, keepdims=True) + EPS)
            h_ref[...] = (x * r * g_ref[...]).astype(BF16)

        z_ref[...] = _dot_nt(h_ref[...], w_ref[...])

    return pl.pallas_call(
        body, name="fwd_in", grid=(T // tm, N // tn),
        in_specs=[pl.BlockSpec((tm, D), lambda i, j: (i, 0)),
                  pl.BlockSpec((1, D), lambda i, j: (0, 0)),
                  pl.BlockSpec((tn, D), lambda i, j: (j, 0))],
        out_specs=[pl.BlockSpec((tm, tn), lambda i, j: (i, j)),
                   pl.BlockSpec((tm, D), lambda i, j: (i, 0))],
        out_shape=[SDS((T, N), F32), SDS((T, D), BF16)],
        compiler_params=_params(("parallel", "arbitrary")),
    )(x2, g, w_t)


def _tri_cumsum(x, reverse):
    t = x.shape[0]
    row = lax.broadcasted_iota(jnp.int32, (t, t), 0)
    col = lax.broadcasted_iota(jnp.int32, (t, t), 1)
    tri = (row <= col) if reverse else (row >= col)
    tri = jnp.where(tri, 1.0, 0.0).astype(BF16)
    hi, mid, lo = _split3(x)
    return _dot(tri, hi.astype(BF16)) + _dot(tri, mid.astype(BF16)) + _dot(tri, lo.astype(BF16))


def _gate_fwd(z, b_pad, B, S, H, col_blk, tc):
    T = B * S
    nsb = S // tc

    def body(zf_ref, b_ref, c_ref, carry):
        @pl.when(pl.program_id(1) == 0)
        def _():
            carry[...] = jnp.zeros_like(carry)

        x = zf_ref[...] + b_ref[...]
        lf = jnp.minimum(x, 0.0) - jnp.log(1.0 + jnp.exp(-jnp.abs(x)))
        lf = jnp.where(_lane(lf.shape) < H, lf, 0.0)
        c_ref[...] = _tri_cumsum(lf, False) + carry[...]
        carry[...] = carry[...] + jnp.sum(lf, axis=0, keepdims=True)

    return pl.pallas_call(
        body, name="gate_fwd", grid=(B, nsb),
        in_specs=[pl.BlockSpec((tc, LANES), lambda b, s: (b * nsb + s, col_blk)),
                  pl.BlockSpec((1, LANES), lambda b, s: (0, 0))],
        out_specs=pl.BlockSpec((tc, LANES), lambda b, s: (b * nsb + s, 0)),
        out_shape=SDS((T, LANES), F32),
        scratch_shapes=[pltpu.VMEM((1, LANES), F32)],
        compiler_params=_params(("parallel", "arbitrary")),
    )(z, b_pad)


def _qk_normalize(x, g):
    r = lax.rsqrt(_half_stats(x * x) * (1.0 / HEAD_DIM) + EPS)
    return x * r * g


def _attn_prep(z, c, gq, gk, B, S, H, tm):
    T = B * S
    FW = H * HEAD_DIM
    nsb = S // tm
    nfb = FW // LANES
    scale = HEAD_DIM ** -0.5

    def body(zq_ref, zk_ref, zv_ref, c_ref, gq_ref, gk_ref, qa_ref, ka_ref, va_ref):
        p = pl.program_id(1)
        lane = _lane((tm, LANES))
        lo = lane < HEAD_DIM
        qn = _qk_normalize(zq_ref[...], gq_ref[...]) * scale
        kn = _qk_normalize(zk_ref[...], gk_ref[...])
        v = zv_ref[...]
        cc = c_ref[...]
        ones_q = ((lane >= L_KDECAY) & (lane < L_KDECAY + 3)).astype(F32)
        ones_k = (((lane >= L_ROWSUM) & (lane < L_ROWSUM + 3)) | ((lane >= L_LSE) & (lane < L_LSE + 3))).astype(F32)
        ones_v = ((lane >= L_ROWSUM) & (lane < L_D + 3)).astype(F32)
        for e in range(2):
            if e == 0:
                qe, ke, ve = qn, kn, v
            else:
                qe, ke, ve = (pltpu.roll(t, HEAD_DIM, 1) for t in (qn, kn, v))
            ch = _lane_col(cc, lane, 2 * p + e)
            pieces = _split3(ch)
            qa = jnp.where(lo, qe, _put3(ones_q, lane, L_ROWSUM, pieces))
            ka = jnp.where(lo, ke, _put3(ones_k, lane, L_KDECAY, [-t for t in pieces]))
            va = jnp.where(lo, ve, ones_v)
            qa_ref[0, e] = qa.astype(BF16)
            ka_ref[0, e] = ka.astype(BF16)
            va_ref[0, e] = va.astype(BF16)

    zspec = lambda off: pl.BlockSpec((tm, LANES), lambda i, p: (i, off + p))
    gspec = pl.BlockSpec((1, LANES), lambda i, p: (0, p))
    ospec = pl.BlockSpec((1, 2, tm, LANES), lambda i, p: (i // nsb, p, i % nsb, 0))
    oshape = SDS((B, H, S, LANES), BF16)
    return pl.pallas_call(
        body, name="attn_prep", grid=(T // tm, H // 2),
        in_specs=[zspec(0), zspec(nfb), zspec(2 * nfb),
                  pl.BlockSpec((tm, LANES), lambda i, p: (i, 0)), gspec, gspec],
        out_specs=[ospec, ospec, ospec],
        out_shape=[oshape, oshape, oshape],
        compiler_params=_params(("parallel", "arbitrary")),
    )(z, z, z, c, gq, gk)


def _attn_fwd(qa, ka, va, bound, t, hb):
    B, H, S, _ = qa.shape
    nq = S // t

    def body(q_ref, k_ref, v_ref, b_ref, o_ref, m_ref, acc_ref, qs_ref):
        i = pl.program_id(2)
        lane = _lane((t, LANES))

        shifts = [b_ref[e] for e in range(hb)]
        worst = shifts[0]
        for e in range(1, hb):
            worst = jnp.maximum(worst, shifts[e])
        bounded = jnp.max(worst) <= SHIFT_MAX
        acc_ref[...] = jnp.zeros_like(acc_ref)

        def tiles(step):
            def loop_body(j, carry):
                step(j, False)
                return carry

            lax.fori_loop(0, i, loop_body, 0)
            step(i, True)

        def keep_mask(n=t):
            return lax.broadcasted_iota(jnp.int32, (n, n), 0) >= lax.broadcasted_iota(jnp.int32, (n, n), 1)

        def finish(e, shift):
            acc = acc_ref[e]
            l = _lane_col(acc, lane, L_ROWSUM)
            o_ref[0, e] = jnp.where(lane < HEAD_DIM, acc / l, shift + jnp.log(l))

        @pl.when(bounded)
        def _():
            for e in range(hb):
                qs_ref[e] = _put3(q_ref[0, e].astype(F32), lane, L_LSE, _split3(-shifts[e])).astype(BF16)

            def pair(e, q_rows, k_start, n, masked):
                k_rows = pl.ds(pl.multiple_of(k_start, n), n)
                p = jnp.exp(_dot_nt(qs_ref[e, q_rows, :], k_ref[0, e, k_rows, :]))
                if masked:
                    p = jnp.where(keep_mask(n), p, 0.0)
                acc_ref[e, q_rows, :] = acc_ref[e, q_rows, :] + _dot(p.astype(BF16), v_ref[0, e, k_rows, :])

            def step(j, masked):
                for e in range(hb):
                    if masked:
                        h = t // 2
                        pair(e, slice(0, h), j * t, h, True)
                        pair(e, slice(h, t), j * t, h, False)
                        pair(e, slice(h, t), j * t + h, h, True)
                    else:
                        pair(e, slice(0, t), j * t, t, False)

            tiles(step)
            for e in range(hb):
                finish(e, shifts[e])

        @pl.when(jnp.logical_not(bounded))
        def _():
            m_ref[...] = jnp.full_like(m_ref, NEG_INF)

            def step(j, masked):
                rows = pl.ds(pl.multiple_of(j * t, t), t)
                for e in range(hb):
                    s = _dot_nt(q_ref[0, e], k_ref[0, e, rows, :])
                    if masked:
                        s = jnp.where(keep_mask(), s, NEG_INF)
                    m_prev = m_ref[e]
                    m_new = jnp.maximum(m_prev, jnp.max(s, axis=-1, keepdims=True))
                    alpha = jnp.exp(m_prev - m_new)
                    p = jnp.exp(s - m_new).astype(BF16)
                    acc_ref[e] = alpha * acc_ref[e] + _dot(p, v_ref[0, e, rows, :])
                    m_ref[e] = m_new

            tiles(step)
            for e in range(hb):
                finish(e, m_ref[e])

    return pl.pallas_call(
        body, name="attn_fwd", grid=(B, H // hb, nq),
        in_specs=[pl.BlockSpec((1, hb, t, LANES), lambda b, h, i: (b, h, i, 0)),
                  pl.BlockSpec((1, hb, S, LANES), lambda b, h, i: (b, h, 0, 0)),
                  pl.BlockSpec((1, hb, S, LANES), lambda b, h, i: (b, h, 0, 0)),
                  pl.BlockSpec((hb, 1, LANES), lambda b, h, i: (h, 0, 0))],
        out_specs=pl.BlockSpec((1, hb, t, LANES), lambda b, h, i: (b, h, i, 0)),
        out_shape=SDS((B, H, S, LANES), F32),
        scratch_shapes=[pltpu.VMEM((hb, t, 1), F32), pltpu.VMEM((hb, t, LANES), F32),
                        pltpu.VMEM((hb, t, LANES), BF16)],
        compiler_params=_params(("parallel", "parallel", "arbitrary")),
    )(qa, ka, va, bound)


def _fill_shifts(ext_ref, sh_ref):
    rows = sh_ref.shape[1]
    for b in range(1, SUBLANES):
        sh_ref[b - 1] = ext_ref[pl.ds(b, rows), :]


def _tap_window(ext_ref, sh_ref, off, tm, cols):
    b = off % SUBLANES
    if b == 0:
        return ext_ref[pl.ds(off, tm), cols]
    return sh_ref[b - 1, pl.ds(off - b, tm), cols]


def _conv_taps(w_ref, ext_ref, sh_ref, out_ref, n_taps, tm, offset_of, bias_ref=None):
    for cc in range(out_ref.shape[1] // LANES):
        cols = slice(cc * LANES, (cc + 1) * LANES)
        acc = None
        for j in sorted(range(n_taps), key=offset_of):
            term = w_ref[j:j + 1, cols] * _tap_window(ext_ref, sh_ref, offset_of(j), tm, cols)
            acc = term if acc is None else acc + term
        out_ref[:, cols] = acc if bias_ref is None else acc + bias_ref[:, cols]


def _layernorm_stats(u2):
    mu = jnp.mean(u2, axis=-1, keepdims=True)
    xc = u2 - mu
    rstd = lax.rsqrt(jnp.mean(xc * xc, axis=-1, keepdims=True) + EPS)
    return xc * rstd, rstd


def _fwd_out(oa, z, x2, tgt, conv_w, conv_b, ln_g, ln_b, w_out, B, S, H, n_taps, tm):
    T, D = x2.shape
    FW = H * HEAD_DIM
    CW = conv_w.shape[1]
    nsb = S // tm
    hb = tm // HALO
    mb = 4 if nsb % 4 == 0 else 1
    mt = mb * tm

    def body(oa_ref, gf_ref, ga_ref, gb_ref, gc_ref, ha_ref, hb_ref, x_ref, t_ref, w_ref, cb_ref, lg_ref,
             lb_ref, wo_ref, y_ref, u2_ref, a_ref, do_ref, dob_ref, dy_ref, loss_ref, ext_ref, sh_ref):
        first_step = (pl.program_id(0) == 0) & (pl.program_id(1) == 0)
        sub = lax.rem(pl.program_id(1), mb)
        rows = pl.ds(pl.multiple_of(sub * tm, tm), tm)

        @pl.when(first_step)
        def _():
            loss_ref[...] = jnp.zeros_like(loss_ref)

        u1 = ga_ref[...] * _sigmoid(gb_ref[...])
        halo = ha_ref[...] * _sigmoid(hb_ref[...])
        ext_ref[0:HALO, :] = jnp.where(pl.program_id(1) > 0, halo, 0.0)
        ext_ref[HALO:, :] = u1
        _fill_shifts(ext_ref, sh_ref)
        _conv_taps(w_ref, ext_ref, sh_ref, u2_ref, n_taps, tm, lambda j: HALO - (n_taps - 1) + j, cb_ref)
        uh, _ = _layernorm_stats(u2_ref[...])
        u3 = uh * lg_ref[...] + lb_ref[...]
        gc = gc_ref[...]
        yu = u3 * _sigmoid(u3) * (gc * _sigmoid(gc))
        y_ref[rows, FW:] = yu.astype(BF16)

        lane = _lane((tm, LANES))
        lo = lane < HEAD_DIM
        for p in range(H // 2):
            a_ref[:, p * LANES:(p + 1) * LANES] = jnp.where(
                lo, oa_ref[0, 2 * p], pltpu.roll(oa_ref[0, 2 * p + 1], HEAD_DIM, 1))
        gf = gf_ref[...]
        y_ref[rows, :FW] = (a_ref[...] * (gf * _sigmoid(gf))).astype(BF16)

        @pl.when(sub == mb - 1)
        def _():
            out = x_ref[...] + _dot(y_ref[...], wo_ref[...])
            diff = out - t_ref[...]
            loss_ref[...] = loss_ref[...] + jnp.sum(diff * diff)
            dout = diff * (1.0 / D)
            do_ref[...] = dout
            dob = dout.astype(BF16)
            dob_ref[...] = dob
            dy_ref[...] = _dot_nt(dob, wo_ref[...])

    row = lambda b, s: b * nsb + s
    zspec = lambda cb: pl.BlockSpec((tm, FW), lambda b, s: (row(b, s), cb))
    hspec = lambda cb: pl.BlockSpec((HALO, CW), lambda b, s: (jnp.maximum(row(b, s) * hb - 1, 0), cb))
    vspec = pl.BlockSpec((1, CW), lambda b, s: (0, 0))
    tspec = lambda w: pl.BlockSpec((tm, w), lambda b, s: (row(b, s), 0))
    mspec = lambda w: pl.BlockSpec((mt, w), lambda b, s: (row(b, s) // mb, 0))
    return pl.pallas_call(
        body, name="fwd_out", grid=(B, nsb),
        in_specs=[pl.BlockSpec((1, H, tm, LANES), lambda b, s: (b, 0, s, 0)),
                  zspec(3), zspec(4), zspec(5), zspec(6), hspec(4), hspec(5),
                  mspec(D), mspec(D),
                  pl.BlockSpec((HALO, CW), lambda b, s: (0, 0)), vspec, vspec, vspec,
                  pl.BlockSpec((FW + CW, D), lambda b, s: (0, 0))],
        out_specs=[mspec(FW + CW), tspec(CW), tspec(FW), mspec(D), mspec(D), mspec(FW + CW),
                   pl.BlockSpec((8, LANES), lambda b, s: (0, 0))],
        out_shape=[SDS((T, FW + CW), BF16), SDS((T, CW), F32), SDS((T, FW), F32), SDS((T, D), F32),
                   SDS((T, D), BF16), SDS((T, FW + CW), F32), SDS((8, LANES), F32)],
        scratch_shapes=[pltpu.VMEM((tm + HALO, CW), F32),
                        pltpu.VMEM((SUBLANES - 1, tm + HALO - SUBLANES, CW), F32)],
        compiler_params=_params(("arbitrary", "arbitrary")),
    )(oa, z, z, z, z, z, z, x2, tgt, conv_w, conv_b, ln_g, ln_b, w_out)


def _bwd_prep(dy, z, a_nat, oa, qa, u2, ln_g, ln_b, B, S, H, tm):
    T = B * S
    FW = H * HEAD_DIM
    CW = u2.shape[1]
    nsb = S // tm

    def body(dya_ref, dyu_ref, gf_ref, gc_ref, a_ref, oa_ref, qa_ref, u2_ref, lg_ref, lb_ref,
             dzgf_ref, dzgc_ref, du2_ref, doa_ref, qb_ref, sg_ref):
        first_step = (pl.program_id(0) == 0) & (pl.program_id(1) == 0)

        @pl.when(first_step)
        def _():
            sg_ref[...] = jnp.zeros_like(sg_ref)

        gf = gf_ref[...]
        sg = _sigmoid(gf)
        a = a_ref[...]
        dya = dya_ref[...]
        da = dya * (gf * sg)
        dzgf_ref[...] = (dya * a * (sg * (1.0 + gf * (1.0 - sg)))).astype(BF16)
        dd = da * a
        lane = _lane((tm, LANES))
        lo = lane < HEAD_DIM
        for p in range(H // 2):
            cols = slice(p * LANES, (p + 1) * LANES)
            da_p = da[:, cols]
            dd_p = dd[:, cols]
            d_heads = (jnp.sum(jnp.where(lo, dd_p, 0.0), axis=-1, keepdims=True),
                       jnp.sum(jnp.where(lo, 0.0, dd_p), axis=-1, keepdims=True))
            for e in range(2):
                da_e = da_p if e == 0 else pltpu.roll(da_p, HEAD_DIM, 1)
                d_e = d_heads[e]
                aug = _put3(jnp.zeros((tm, LANES), F32), lane, L_D, _split3(-d_e))
                doa_ref[0, 2 * p + e] = jnp.where(lo, da_e, aug).astype(BF16)
                lse = _lane_col(oa_ref[0, 2 * p + e], lane, L_ROWSUM)
                qb = _put3(qa_ref[0, 2 * p + e].astype(F32), lane, L_LSE, _split3(-lse))
                qb_ref[0, 2 * p + e] = qb.astype(BF16)

        gc = gc_ref[...]
        sc = _sigmoid(gc)
        dyu = dyu_ref[...]
        uh, rstd = _layernorm_stats(u2_ref[...])
        u3 = uh * lg_ref[...] + lb_ref[...]
        s3 = _sigmoid(u3)
        dzgc_ref[...] = (dyu * (u3 * s3) * (sc * (1.0 + gc * (1.0 - sc)))).astype(BF16)
        du3 = dyu * (gc * sc) * (s3 * (1.0 + u3 * (1.0 - s3)))
        sg_ref[0:1, :] = sg_ref[0:1, :] + jnp.sum(du3 * uh, axis=0, keepdims=True)
        sg_ref[1:2, :] = sg_ref[1:2, :] + jnp.sum(du3, axis=0, keepdims=True)
        duh = du3 * lg_ref[...]
        du2 = rstd * (duh - jnp.mean(duh, axis=-1, keepdims=True)
                      - uh * jnp.mean(duh * uh, axis=-1, keepdims=True))
        sg_ref[2:3, :] = sg_ref[2:3, :] + jnp.sum(du2, axis=0, keepdims=True)
        du2_ref[...] = du2

    row = lambda b, s: b * nsb + s
    tspec = lambda w, cb=0: pl.BlockSpec((tm, w), lambda b, s: (row(b, s), cb))
    hspec = pl.BlockSpec((1, H, tm, LANES), lambda b, s: (b, 0, s, 0))
    vspec = pl.BlockSpec((1, CW), lambda b, s: (0, 0))
    return pl.pallas_call(
        body, name="bwd_prep", grid=(B, nsb),
        in_specs=[tspec(FW, 0), tspec(CW, 1), tspec(FW, 3), tspec(CW, 6), tspec(FW), hspec, hspec,
                  tspec(CW), vspec, vspec],
        out_specs=[tspec(FW), tspec(CW), tspec(CW), hspec, hspec,
                   pl.BlockSpec((8, CW), lambda b, s: (0, 0))],
        out_shape=[SDS((T, FW), BF16), SDS((T, CW), BF16), SDS((T, CW), F32),
                   SDS((B, H, S, LANES), BF16), SDS((B, H, S, LANES), BF16), SDS((8, CW), F32)],
        compiler_params=_params(("arbitrary", "arbitrary")),
    )(dy, dy, z, z, a_nat, oa, qa, u2, ln_g, ln_b)


def _conv_bwd(du2, z, conv_w, B, S, n_taps, tm):
    T, CW = du2.shape
    nsb = S // tm
    hb = tm // HALO

    def body(d_ref, dh_ref, ga_ref, gb_ref, ha_ref, hb_ref, w_ref, dz_ref, dw_ref,
             extu_ref, extd_ref, shu_ref, shd_ref, du1_ref, dwacc_ref):
        s = pl.program_id(1)
        first_step = (pl.program_id(0) == 0) & (s == 0)
        last_step = (pl.program_id(0) == B - 1) & (s == nsb - 1)

        @pl.when(first_step)
        def _():
            dwacc_ref[...] = jnp.zeros_like(dwacc_ref)

        ga = ga_ref[...]
        sb = _sigmoid(gb_ref[...])
        halo = ha_ref[...] * _sigmoid(hb_ref[...])
        extu_ref[0:HALO, :] = jnp.where(s > 0, halo, 0.0)
        extu_ref[HALO:, :] = ga * sb
        extd_ref[0:tm, :] = d_ref[...]
        extd_ref[tm:, :] = jnp.where(s < nsb - 1, dh_ref[...], 0.0)
        _fill_shifts(extu_ref, shu_ref)
        _fill_shifts(extd_ref, shd_ref)
        _conv_taps(w_ref, extd_ref, shd_ref, du1_ref, n_taps, tm, lambda j: n_taps - 1 - j)
        for cc in range(CW // LANES):
            cols = slice(cc * LANES, (cc + 1) * LANES)
            parts = [None] * n_taps
            for r in range(tm // SUBLANES):
                dv = d_ref[r * SUBLANES:(r + 1) * SUBLANES, cols]
                for j in range(n_taps):
                    off = HALO - (n_taps - 1) + j + r * SUBLANES
                    term = dv * _tap_window(extu_ref, shu_ref, off, SUBLANES, cols)
                    parts[j] = term if parts[j] is None else parts[j] + term
            for j in range(n_taps):
                rows = slice(j * SUBLANES, (j + 1) * SUBLANES)
                dwacc_ref[rows, cols] = dwacc_ref[rows, cols] + parts[j]
        du1 = du1_ref[...]
        dz_ref[:, :CW] = (du1 * sb).astype(BF16)
        dz_ref[:, CW:] = (du1 * ga * (sb * (1.0 - sb))).astype(BF16)

        @pl.when(last_step)
        def _():
            dw_ref[...] = jnp.zeros_like(dw_ref)
            for j in range(n_taps):
                dw_ref[j:j + 1, :] = jnp.sum(dwacc_ref[j * SUBLANES:(j + 1) * SUBLANES, :], axis=0, keepdims=True)

    row = lambda b, s: b * nsb + s
    last_halo = T // HALO - 1
    return pl.pallas_call(
        body, name="conv_bwd", grid=(B, nsb),
        in_specs=[pl.BlockSpec((tm, CW), lambda b, s: (row(b, s), 0)),
                  pl.BlockSpec((HALO, CW), lambda b, s: (jnp.minimum((row(b, s) + 1) * hb, last_halo), 0)),
                  pl.BlockSpec((tm, CW), lambda b, s: (row(b, s), 4)),
                  pl.BlockSpec((tm, CW), lambda b, s: (row(b, s), 5)),
                  pl.BlockSpec((HALO, CW), lambda b, s: (jnp.maximum(row(b, s) * hb - 1, 0), 4)),
                  pl.BlockSpec((HALO, CW), lambda b, s: (jnp.maximum(row(b, s) * hb - 1, 0), 5)),
                  pl.BlockSpec((HALO, CW), lambda b, s: (0, 0))],
        out_specs=[pl.BlockSpec((tm, 2 * CW), lambda b, s: (row(b, s), 0)),
                   pl.BlockSpec((HALO, CW), lambda b, s: (0, 0))],
        out_shape=[SDS((T, 2 * CW), BF16), SDS((HALO, CW), F32)],
        scratch_shapes=[pltpu.VMEM((tm + HALO, CW), F32), pltpu.VMEM((tm + HALO, CW), F32),
                        pltpu.VMEM((SUBLANES - 1, tm + HALO - SUBLANES, CW), F32),
                        pltpu.VMEM((SUBLANES - 1, tm + HALO - SUBLANES, CW), F32),
                        pltpu.VMEM((tm, CW), F32), pltpu.VMEM((HALO * SUBLANES, CW), F32)],
        compiler_params=_params(("arbitrary", "arbitrary")),
    )(du2, du2, z, z, z, z, conv_w)


def _attn_bwd(qb, ka, va, doa, t, hb):
    B, H, S, _ = qb.shape
    nk = S // t

    def body(q_ref, k_ref, v_ref, do_ref, dq_ref, dk_ref, dv_ref, dv_acc):
        j = pl.program_id(2)

        @pl.when(j == 0)
        def _():
            dq_ref[...] = jnp.zeros_like(dq_ref)

        dk_ref[...] = jnp.zeros_like(dk_ref)
        dv_acc[...] = jnp.zeros_like(dv_acc)

        def step(i, masked):
            q_rows = pl.ds(pl.multiple_of(i * t, t), t)
            for e in range(hb):
                k = k_ref[0, e]
                q = q_ref[0, e, q_rows, :]
                do = do_ref[0, e, q_rows, :]
                p = jnp.exp(_dot_nt(q, k))
                if masked:
                    keep = lax.broadcasted_iota(jnp.int32, (t, t), 0) >= lax.broadcasted_iota(jnp.int32, (t, t), 1)
                    p = jnp.where(keep, p, 0.0)
                ds = (p * _dot_nt(do, v_ref[0, e])).astype(BF16)
                dv_acc[e] = dv_acc[e] + _dot_tn(p.astype(BF16), do)
                dk_ref[0, e] = dk_ref[0, e] + _dot_tn(ds, q)
                dq_ref[0, e, q_rows, :] = dq_ref[0, e, q_rows, :] + _dot(ds, k)

        step(j, True)

        def loop_body(i, carry):
            step(i, False)
            return carry

        lax.fori_loop(j + 1, nk, loop_body, 0)
        dv_ref[0] = dv_acc[...].astype(BF16)

    full = pl.BlockSpec((1, hb, S, LANES), lambda b, h, j: (b, h, 0, 0))
    blk = pl.BlockSpec((1, hb, t, LANES), lambda b, h, j: (b, h, j, 0))
    oshape = SDS((B, H, S, LANES), F32)
    return pl.pallas_call(
        body, name="attn_bwd", grid=(B, H // hb, nk),
        in_specs=[full, blk, blk, full],
        out_specs=[full, blk, blk],
        out_shape=[oshape, oshape, SDS((B, H, S, LANES), BF16)],
        scratch_shapes=[pltpu.VMEM((hb, t, LANES), F32)],
        compiler_params=_params(("parallel", "parallel", "arbitrary")),
    )(qb, ka, va, doa)


def _qk_bwd(dqa, dka, dva, z, gq, gk, B, S, H, tm):
    T = B * S
    FW = H * HEAD_DIM
    nsb = S // tm
    nfb = FW // LANES
    scale = HEAD_DIM ** -0.5

    def body(dq_ref, dk_ref, dv_ref, zq_ref, zk_ref, gq_ref, gk_ref, dzq_ref, dzk_ref, dzv_ref, dc_ref, dg_ref):
        p = pl.program_id(0)

        @pl.when(pl.program_id(1) == 0)
        def _():
            dg_ref[...] = jnp.zeros_like(dg_ref)

        lane = _lane((tm, LANES))
        lo = lane < HEAD_DIM

        def natural(ref):
            return jnp.where(lo, ref[0, 0].astype(F32), pltpu.roll(ref[0, 1].astype(F32), HEAD_DIM, 1))

        def norm_bwd(dn, x, g, row, out_ref):
            r = lax.rsqrt(_half_stats(x * x) * (1.0 / HEAD_DIM) + EPS)
            xh = x * r
            dg_ref[row:row + 1, :] = dg_ref[row:row + 1, :] + jnp.sum(dn * xh, axis=0, keepdims=True)
            dxh = dn * g
            mm = _half_stats(dxh * xh) * (1.0 / HEAD_DIM)
            out_ref[...] = (r * (dxh - xh * mm)).astype(BF16)

        norm_bwd(natural(dq_ref) * scale, zq_ref[...], gq_ref[...], 0, dzq_ref)
        norm_bwd(natural(dk_ref), zk_ref[...], gk_ref[...], 1, dzk_ref)
        dzv_ref[...] = natural(dv_ref).astype(BF16)

        dc = jnp.zeros((tm, LANES), F32)
        for e in range(2):
            val = _lane_col(dq_ref[0, e], lane, L_ROWSUM) - _lane_col(dk_ref[0, e], lane, L_KDECAY)
            dc = jnp.where(lane == 2 * p + e, val, dc)
        dc_ref[0] = dc

    hspec = pl.BlockSpec((1, 2, tm, LANES), lambda p, i: (i // nsb, p, i % nsb, 0))
    zspec = lambda off: pl.BlockSpec((tm, LANES), lambda p, i: (i, off + p))
    gspec = pl.BlockSpec((1, LANES), lambda p, i: (0, p))
    ospec = pl.BlockSpec((tm, LANES), lambda p, i: (i, p))
    return pl.pallas_call(
        body, name="qk_bwd", grid=(H // 2, T // tm),
        in_specs=[hspec, hspec, hspec, zspec(0), zspec(nfb), gspec, gspec],
        out_specs=[ospec, ospec, ospec,
                   pl.BlockSpec((1, tm, LANES), lambda p, i: (p, i, 0)),
                   pl.BlockSpec((8, LANES), lambda p, i: (0, p))],
        out_shape=[SDS((T, FW), BF16), SDS((T, FW), BF16), SDS((T, FW), BF16),
                   SDS((H // 2, T, LANES), F32), SDS((8, FW), F32)],
        compiler_params=_params(("parallel", "arbitrary")),
    )(dqa, dka, dva, z, z, gq, gk)


def _gate_bwd(dc8, z, b_pad, B, S, H, col_blk, fp, tc):
    T = B * S
    nsb = S // tc
    npair = dc8.shape[0]

    def body(dc_ref, zf_ref, b_ref, dz_ref, db_ref, carry):
        first_step = (pl.program_id(0) == 0) & (pl.program_id(1) == 0)

        @pl.when(first_step)
        def _():
            db_ref[...] = jnp.zeros_like(db_ref)

        @pl.when(pl.program_id(1) == 0)
        def _():
            carry[...] = jnp.zeros_like(carry)

        dc = dc_ref[0]
        for k in range(1, npair):
            dc = dc + dc_ref[k]
        dlf = _tri_cumsum(dc, True) + carry[...]
        carry[...] = carry[...] + jnp.sum(dc, axis=0, keepdims=True)
        x = zf_ref[...] + b_ref[...]
        dlogit = dlf * _sigmoid(-x)
        db_ref[0:1, :] = db_ref[0:1, :] + jnp.sum(dlogit, axis=0, keepdims=True)
        dz_ref[...] = jnp.zeros_like(dz_ref)
        dz_ref[:, :LANES] = dlogit.astype(BF16)

    rrow = lambda b, s: b * nsb + (nsb - 1 - s)
    return pl.pallas_call(
        body, name="gate_bwd", grid=(B, nsb),
        in_specs=[pl.BlockSpec((npair, tc, LANES), lambda b, s: (0, rrow(b, s), 0)),
                  pl.BlockSpec((tc, LANES), lambda b, s: (rrow(b, s), col_blk)),
                  pl.BlockSpec((1, LANES), lambda b, s: (0, 0))],
        out_specs=[pl.BlockSpec((tc, fp), lambda b, s: (rrow(b, s), 0)),
                   pl.BlockSpec((8, LANES), lambda b, s: (0, 0))],
        out_shape=[SDS((T, fp), BF16), SDS((8, LANES), F32)],
        scratch_shapes=[pltpu.VMEM((1, LANES), F32)],
        compiler_params=_params(("arbitrary", "arbitrary")),
    )(dc8, z, b_pad)


def _matmul_tn(a, b, name, tmm, tn, tk):
    T, M = a.shape
    N = b.shape[1]
    tmm, tn, tk = min(tmm, M), min(tn, N), min(tk, T)

    def body(a_ref, b_ref, o_ref):
        @pl.when(pl.program_id(2) == 0)
        def _():
            o_ref[...] = jnp.zeros_like(o_ref)

        o_ref[...] = o_ref[...] + _dot_tn(a_ref[...], b_ref[...])

    return pl.pallas_call(
        body, name=name, grid=(M // tmm, N // tn, T // tk),
        in_specs=[pl.BlockSpec((tk, tmm), lambda i, j, k: (k, i)),
                  pl.BlockSpec((tk, tn), lambda i, j, k: (k, j))],
        out_specs=pl.BlockSpec((tmm, tn), lambda i, j, k: (i, j)),
        out_shape=SDS((M, N), F32),
        compiler_params=_params(("parallel", "parallel", "arbitrary")),
    )(a, b)


def _dh_rms_bwd(pieces, w_t, x2, g, dout, tm, tk, parts):
    T, D = x2.shape
    nks = [p.shape[1] // tk for p in pieces]
    starts = [sum(nks[:k]) for k in range(len(pieces))]
    nk = sum(nks)
    ni = T // tm
    n = len(parts)

    def body(*refs):
        dz_refs = refs[:len(pieces)]
        w_ref, x_ref, g_ref, do_ref = refs[len(pieces):len(pieces) + 4]
        part_refs = refs[len(pieces) + 4:len(pieces) + 4 + n]
        gx_ref, dg_ref = refs[len(pieces) + 4 + n:len(pieces) + 6 + n]
        slot_refs = refs[len(pieces) + 6 + n:len(pieces) + 6 + 2 * n]
        acc_ref, send_sems, recv_sems = refs[len(pieces) + 6 + 2 * n:]
        k = pl.program_id(1)
        first_step = (pl.program_id(0) == 0) & (k == 0)
        last_step = (pl.program_id(0) == ni - 1) & (k == nk - 1)
        x, y, c = _place()
        chips = [(1 - x, y), (x, 1 - y), (1 - x, 1 - y)]

        def copy(a, f, to):
            cx, cy = chips[f]
            return pltpu.make_async_remote_copy(
                src_ref=part_refs[a].at[2 * cx + cy], dst_ref=slot_refs[a].at[f],
                send_sem=send_sems.at[a * 3 + f], recv_sem=recv_sems.at[a * 3 + f],
                device_id=to, device_id_type=MESH)

        @pl.when(first_step)
        def _():
            dg_ref[...] = jnp.zeros_like(dg_ref)
            for a in range(n):
                for f in range(3):
                    copy(a, f, (*chips[f], c)).start()

        @pl.when(last_step)
        def _():
            for a in range(n):
                for f in range(3):
                    copy(a, f, (x, y, c)).wait_recv()
            for a in range(n):
                for f in range(3):
                    copy(a, f, (*chips[f], c)).wait_send()

        @pl.when(k == 0)
        def _():
            acc_ref[...] = jnp.zeros_like(acc_ref)

        for dz_ref, st, cnt in zip(dz_refs, starts, nks):
            @pl.when((k >= st) & (k < st + cnt))
            def _(dz_ref=dz_ref):
                acc_ref[...] = acc_ref[...] + _dot(dz_ref[...], w_ref[...])

        @pl.when(k == nk - 1)
        def _():
            x = x_ref[...]
            r = lax.rsqrt(jnp.mean(x * x, axis=-1, keepdims=True) + EPS)
            xh = x * r
            dh = acc_ref[...]
            dg_ref[0:1, :] = dg_ref[0:1, :] + jnp.sum(dh * xh, axis=0, keepdims=True)
            dxn = dh * g_ref[...]
            gx_ref[...] = do_ref[...] + r * (dxn - xh * jnp.mean(dxn * xh, axis=-1, keepdims=True))

    def piece_spec(st, cnt):
        return pl.BlockSpec((tm, tk), lambda i, k: (i, jnp.clip(k - st, 0, cnt - 1)))

    tspec = pl.BlockSpec((tm, D), lambda i, k: (i, 0))
    return pl.pallas_call(
        body, name="dh_rms_bwd", grid=(T // tm, nk),
        in_specs=[piece_spec(st, cnt) for st, cnt in zip(starts, nks)]
        + [pl.BlockSpec((tk, D), lambda i, k: (k, 0)), tspec, pl.BlockSpec((1, D), lambda i, k: (0, 0)), tspec]
        + [ANY] * n,
        out_specs=[tspec, pl.BlockSpec((8, D), lambda i, k: (0, 0))] + [ANY] * n,
        out_shape=[SDS((T, D), F32), SDS((8, D), F32)] + [SDS((3,) + p.shape[1:], p.dtype) for p in parts],
        scratch_shapes=[pltpu.VMEM((tm, D), F32),
                        pltpu.SemaphoreType.DMA((3 * n,)), pltpu.SemaphoreType.DMA((3 * n,))],
        compiler_params=_params(("arbitrary", "arbitrary")),
    )(*pieces, w_t, x2, g, dout, *parts)


def _block_plan(R, C, tr, tc):
    br = min(tr, R)
    if R % br == 0:
        return (br, C), R // br, lambda i: (i, 0)
    bc = min(tc, C)
    assert C % bc == 0
    return (R, bc), C // bc, lambda i: (0, i)


def _ew_call(body, name, ins, n_out, out_dtypes, tr, tc):
    R, C = ins[0].shape
    blk, steps, imap = _block_plan(R, C, tr, tc)
    spec = pl.BlockSpec(blk, imap)
    return pl.pallas_call(
        body, name=name, grid=(steps,),
        in_specs=[spec] * len(ins), out_specs=[spec] * n_out,
        out_shape=[SDS((R, C), dt) for dt in out_dtypes],
        compiler_params=_params(("parallel",)),
    )(*ins)


def _sum_slots(slots, name, first=None, tr=256):
    n, R, C = slots.shape
    blk, steps, imap = _block_plan(R, C, tr, 2 * LANES)
    lead = [] if first is None else [first]

    def body(*refs):
        s_ref, o_ref = refs[-2:]
        acc = refs[0][...].astype(F32) if lead else s_ref[0].astype(F32)
        for k in range(0 if lead else 1, n):
            acc = acc + s_ref[k].astype(F32)
        o_ref[...] = acc

    return pl.pallas_call(
        body, name=name, grid=(steps,),
        in_specs=[pl.BlockSpec(blk, imap)] * len(lead) + [pl.BlockSpec((n,) + blk, lambda i: (0,) + imap(i))],
        out_specs=pl.BlockSpec(blk, imap),
        out_shape=SDS((R, C), F32),
        compiler_params=_params(("parallel",)),
    )(*lead, slots)


def _adamw_update(w, g, m, v):
    nm = ADAM_B1 * m + (1.0 - ADAM_B1) * g
    nv = ADAM_B2 * v + (1.0 - ADAM_B2) * (g * g)
    m_hat = nm / (1.0 - ADAM_B1 ** ADAM_STEP)
    v_hat = nv / (1.0 - ADAM_B2 ** ADAM_STEP)
    return -ADAM_LR * (m_hat / (jnp.sqrt(v_hat) + ADAM_EPS) + ADAM_WD * w), nm, nv


def _adamw(w, g, m, v, name):
    def body(w_ref, g_ref, m_ref, v_ref, d_ref, nm_ref, nv_ref):
        d_ref[...], nm_ref[...], nv_ref[...] = _adamw_update(w_ref[...], g_ref[...], m_ref[...], v_ref[...])

    return _ew_call(body, name, [w, g, m, v], 3, [F32, F32, F32], 128, 2 * LANES)


def _adamw_halves(w, mine, other, m, v, name):
    R, C = w.shape
    half = C // 2
    bc = min(2 * LANES, half)
    per = half // bc

    def body(w_ref, a_ref, b_ref, m_ref, v_ref, g_ref, d_ref, nm_ref, nv_ref):
        g = jnp.where(pl.program_id(0) // per == lax.axis_index("c"), a_ref[...], b_ref[...])
        g_ref[...] = g
        d_ref[...], nm_ref[...], nv_ref[...] = _adamw_update(w_ref[...], g, m_ref[...], v_ref[...])

    full = pl.BlockSpec((R, bc), lambda i: (0, i))
    part = pl.BlockSpec((R, bc), lambda i: (0, i % per))
    return pl.pallas_call(
        body, name=name, grid=(C // bc,),
        in_specs=[full, part, part, full, full], out_specs=[full] * 4,
        out_shape=[SDS((R, C), F32)] * 4,
        compiler_params=_params(("parallel",)),
    )(w, mine, other, m, v)


ANY = pl.BlockSpec(memory_space=pl.ANY)


def _place():
    return lax.axis_index("x"), lax.axis_index("y"), lax.axis_index("c")


def _gather_chips(shards, splits):
    n = len(shards)
    per = 7

    def body(*refs):
        ins, outs = refs[:n], refs[n:2 * n]
        send_sems, recv_sems = refs[2 * n:]
        x, y, c = _place()
        mine = 2 * x + y
        chips = [(1 - x, y), (x, 1 - y), (1 - x, 1 - y)]

        def rows(a, half):
            return pl.ds(0, splits[a]) if half == 0 else pl.ds(splits[a], ins[a].shape[0] - splits[a])

        def copy(a, k, chip_idx, half, to, src=None):
            dst = outs[a].at[chip_idx, rows(a, half)]
            return pltpu.make_async_remote_copy(
                src_ref=dst if src is None else src, dst_ref=dst,
                send_sem=send_sems.at[a * per + k], recv_sem=recv_sems.at[a * per + k],
                device_id=to, device_id_type=MESH)

        def own(a, to):
            return pltpu.make_async_remote_copy(
                src_ref=ins[a], dst_ref=outs[a].at[mine],
                send_sem=send_sems.at[a * per + 6], recv_sem=recv_sems.at[a * per + 6],
                device_id=to, device_id_type=MESH)

        for cc in (0, 1):
            @pl.when(c == cc)
            def _(cc=cc):
                me, sibling = (x, y, cc), (x, y, 1 - cc)
                first = [copy(a, k, mine, cc, (*chip, cc), src=ins[a].at[rows(a, cc)])
                         for a in range(n) for k, chip in enumerate(chips)]
                first += [own(a, sibling) for a in range(n)]
                for cp in first:
                    cp.start()
                passed = []
                for k, (cx, cy) in enumerate(chips):
                    for a in range(n):
                        copy(a, k, 2 * cx + cy, cc, me).wait_recv()
                        fwd = copy(a, 3 + k, 2 * cx + cy, cc, sibling)
                        fwd.start()
                        passed.append(fwd)
                for k, (cx, cy) in enumerate(chips):
                    for a in range(n):
                        copy(a, 3 + k, 2 * cx + cy, 1 - cc, me).wait_recv()
                for a in range(n):
                    own(a, me).wait_recv()
                for cp in first + passed:
                    cp.wait_send()

    return pl.pallas_call(
        body, name="gather_chips",
        in_specs=[ANY] * n, out_specs=[ANY] * n,
        out_shape=[SDS((4,) + s.shape, s.dtype) for s in shards],
        scratch_shapes=[pltpu.SemaphoreType.DMA((per * n,)), pltpu.SemaphoreType.DMA((per * n,))],
    )(*shards)


def _pair_swap(arrs):
    n = len(arrs)

    def body(*refs):
        ins, outs = refs[:n], refs[n:2 * n]
        send_sems, recv_sems = refs[2 * n:]
        x, y, c = _place()
        for cc in (0, 1):
            @pl.when(c == cc)
            def _(cc=cc):
                copies = []
                for a in range(n):
                    half = ins[a].shape[1] // 2
                    copies.append(pltpu.make_async_remote_copy(
                        src_ref=ins[a].at[:, pl.ds((1 - cc) * half, half)], dst_ref=outs[a],
                        send_sem=send_sems.at[a], recv_sem=recv_sems.at[a],
                        device_id=(x, y, 1 - cc), device_id_type=MESH))
                for cp in copies:
                    cp.start()
                for cp in copies:
                    cp.wait()

    return pl.pallas_call(
        body, name="pair_swap",
        in_specs=[ANY] * n, out_specs=[ANY] * n,
        out_shape=[SDS((h.shape[0], h.shape[1] // 2), h.dtype) for h in arrs],
        scratch_shapes=[pltpu.SemaphoreType.DMA((n,)), pltpu.SemaphoreType.DMA((n,))],
    )(*arrs)


def _pair_sum(arrs, got, core, tr):
    half = arrs[0].shape[1] // 2
    cnts = [p.shape[0] // tr for p in arrs]
    starts = [sum(cnts[:k]) for k in range(len(arrs))]

    def body(core_ref, *refs):
        del core_ref
        own_refs, got_refs, o_ref = refs[:len(arrs)], refs[len(arrs):2 * len(arrs)], refs[-1]
        s = pl.program_id(0)
        for own_ref, got_ref, st, cnt in zip(own_refs, got_refs, starts, cnts):
            @pl.when((s >= st) & (s < st + cnt))
            def _(own_ref=own_ref, got_ref=got_ref):
                o_ref[...] = (own_ref[...] + got_ref[...]).astype(BF16)

    def own_spec(st, cnt):
        return pl.BlockSpec((tr, half), lambda s, core_ref: (jnp.clip(s - st, 0, cnt - 1), core_ref[0]))

    def got_spec(st, cnt):
        return pl.BlockSpec((tr, half), lambda s, core_ref: (jnp.clip(s - st, 0, cnt - 1), 0))

    return pl.pallas_call(
        body, name="pair_sum",
        grid_spec=pltpu.PrefetchScalarGridSpec(
            num_scalar_prefetch=1, grid=(sum(cnts),),
            in_specs=[own_spec(st, cnt) for st, cnt in zip(starts, cnts)]
            + [got_spec(st, cnt) for st, cnt in zip(starts, cnts)],
            out_specs=pl.BlockSpec((tr, half), lambda s, core_ref: (s, 0))),
        out_shape=SDS((sum(cnts) * tr, half), BF16),
        compiler_params=_params(("arbitrary",)),
    )(core, *arrs, *got)


def _share_results(arrs, rows):
    n = len(arrs)
    flips = [(fx, fy, fc) for fx in (0, 1) for fy in (0, 1) for fc in (0, 1)][1:]

    def body(*refs):
        ins, rows_ref, outs, all_ref = refs[:n], refs[n], refs[n + 1:2 * n + 1], refs[2 * n + 1]
        send_sems, recv_sems, local_sem = refs[2 * n + 2:]
        x, y, c = _place()
        me = 4 * x + 2 * y + c
        local = pltpu.make_async_copy(rows_ref, all_ref.at[me], local_sem)
        local.start()
        copies = [pltpu.make_async_remote_copy(
            src_ref=ins[a], dst_ref=outs[a], send_sem=send_sems.at[a], recv_sem=recv_sems.at[a],
            device_id=(x, y, 1 - c), device_id_type=MESH) for a in range(n)]
        for k, (fx, fy, fc) in enumerate(flips):
            copies.append(pltpu.make_async_remote_copy(
                src_ref=rows_ref, dst_ref=all_ref.at[me], send_sem=send_sems.at[n + k], recv_sem=recv_sems.at[n + k],
                device_id=(x ^ fx, y ^ fy, c ^ fc), device_id_type=MESH))
        for cp in copies:
            cp.start()
        for cp in copies[:n]:
            cp.wait_recv()
        for k, (fx, fy, fc) in enumerate(flips):
            src = 4 * (x ^ fx) + 2 * (y ^ fy) + (c ^ fc)
            pltpu.make_async_remote_copy(
                src_ref=rows_ref, dst_ref=all_ref.at[src], send_sem=send_sems.at[n + k], recv_sem=recv_sems.at[n + k],
                device_id=(x, y, c), device_id_type=MESH).wait_recv()
        for cp in copies:
            cp.wait_send()
        local.wait()

    outs = pl.pallas_call(
        body, name="share_results",
        in_specs=[ANY] * (n + 1), out_specs=[ANY] * (n + 1),
        out_shape=[SDS(h.shape, h.dtype) for h in arrs] + [SDS((8,) + rows.shape, rows.dtype)],
        scratch_shapes=[pltpu.SemaphoreType.DMA((n + 7,)), pltpu.SemaphoreType.DMA((n + 7,)),
                        pltpu.SemaphoreType.DMA],
    )(*arrs, rows)
    return outs[:n], outs[n]


def _tiles(S, FW):
    big = FW % 512 == 0
    return dict(
        fp=512 if big else LANES,
        tn=2560 if big else LANES,
        tm_in=min(1024, S),
        t_attn=min(512, S),
        hb_fwd=8,
        hb_bwd=8,
        tm_prep=min(2048, S),
        tm_mix=min(128, S),
        tm_bwd=min(256, S),
        tc=min(512, S),
        tk=512 if big else LANES,
    )


def kernel(x, norm_g, w_in, b_forget, q_norm_g, k_norm_g, conv_w, conv_b, conv_ln_g, conv_ln_b, w_out, loss_target, m_norm_g, m_w_in, m_b_forget, m_q_norm_g, m_k_norm_g, m_conv_w, m_conv_b, m_conv_ln_g, m_conv_ln_b, m_w_out, v_norm_g, v_w_in, v_b_forget, v_q_norm_g, v_k_norm_g, v_conv_w, v_conv_b, v_conv_ln_g, v_conv_ln_b, v_w_out):
    B, S, D = x.shape
    H, dh = q_norm_g.shape[1:]
    FW = H * dh
    CW = conv_b.shape[-1]
    n_taps, cw_shard = conv_w.shape[1:]
    in_shard = w_in.shape[2]
    out_shard = w_out.shape[1]
    assert dh == HEAD_DIM and H % 2 == 0 and H <= LANES and FW == CW == D
    assert n_taps - 1 <= HALO and 4 * cw_shard == CW and 4 * out_shard == FW + CW
    assert 4 * in_shard == 4 * FW + 3 * CW + H
    T = B * S
    tl = _tiles(S, FW)
    fp = tl["fp"]
    xi, yi, ci = _place()

    w_t = jnp.transpose(w_in[0])
    conv_pad = jnp.pad(conv_w[0], ((0, HALO - n_taps), (0, 0)))
    bf16_rows = 2 * SUBLANES
    g_in, g_out, g_cw = _gather_chips(
        [w_t.astype(BF16), w_out[0].astype(BF16), conv_pad],
        [in_shard // 2 // bf16_rows * bf16_rows, out_shard // 2, HALO // 2])
    w_t_full = g_in.reshape(4 * in_shard, D)
    w_out_full = g_out.reshape(FW + CW, D)
    conv_full = g_cw.transpose(1, 0, 2).reshape(HALO, CW)
    o_f = 3 * FW
    w_pack = jnp.concatenate([w_t_full[:o_f], w_t_full[o_f + H:],
                              jnp.pad(w_t_full[o_f:o_f + H], ((0, fp - H), (0, 0)))], axis=0)
    f_col = 4 * FW + 3 * CW

    x2 = x.reshape(T, D)
    tgt = loss_target.reshape(T, D)
    b_pad = jnp.pad(b_forget, ((0, 0), (0, LANES - H)))
    gq = q_norm_g.reshape(1, FW)
    gk = k_norm_g.reshape(1, FW)

    z, h = _fwd_in(x2, norm_g, w_pack, tl["tm_in"], tl["tn"])
    c = _gate_fwd(z, b_pad, B, S, H, f_col // LANES, tl["tc"])
    qa, ka, va = _attn_prep(z, c, gq, gk, B, S, H, tl["tm_prep"])
    gain = lambda g: jnp.max(jnp.abs(g[0]), axis=-1)
    bound = (NORM_SLACK ** 2 * dh ** 0.5) * gain(q_norm_g) * gain(k_norm_g)
    oa = _attn_fwd(qa, ka, va, jnp.broadcast_to(bound[:, None, None], (H, 1, LANES)), tl["t_attn"], tl["hb_fwd"])
    y, u2, a_nat, dout, dout_b, dy, loss_acc = _fwd_out(
        oa, z, x2, tgt, conv_full, conv_b, conv_ln_g, conv_ln_b, w_out_full, B, S, H, n_taps, tl["tm_mix"])

    dzgf, dzgc, du2, doa, qb, sg_conv = _bwd_prep(dy, z, a_nat, oa, qa, u2, conv_ln_g, conv_ln_b, B, S, H, tl["tm_bwd"])
    dzglu, dconv_w = _conv_bwd(du2, z, conv_full, B, S, n_taps, tl["tm_mix"])
    dqa, dka, dva = _attn_bwd(qb, ka, va, doa, tl["t_attn"], tl["hb_bwd"])
    dzq, dzk, dzv, dc8, dg_qk = _qk_bwd(dqa, dka, dva, z, gq, gk, B, S, H, tl["tm_prep"])
    dzf, db_f = _gate_bwd(dc8, z, b_pad, B, S, H, f_col // LANES, fp, tl["tc"])
    pieces = [dzq, dzk, dzv, dzgf, dzglu, dzgc, dzf]
    dw_all = [_matmul_tn(p, h, f"dw_in_{k}", 1024, 1024, 2048) for k, p in enumerate(pieces)]
    dw_all.append(_matmul_tn(y, dout_b, "dw_out", 1024, 1024, 2048))

    summed = _pair_sum(dw_all, _pair_swap(dw_all), ci.astype(jnp.int32).reshape(1), fp)
    ends = [0]
    for t in dw_all:
        ends.append(ends[-1] + t.shape[0])
    spans, at = [], 0
    for k, rows in [(0, FW), (1, FW), (2, FW), (6, H), (3, FW), (4, 2 * CW), (5, CW)]:
        spans.append((at, rows, ends[k]))
        at += rows

    def chip_rows(j):
        lo, hi = j * in_shard, (j + 1) * in_shard
        return jnp.concatenate([summed[src + max(lo, a) - a:src + min(hi, a + n) - a]
                                for a, n, src in spans if max(lo, a) < min(hi, a + n)], axis=0)

    part_in = jnp.stack([chip_rows(j) for j in range(4)])
    part_out = summed[ends[7]:ends[8]].reshape(4, out_shard, D // 2)
    grad_x2, dg_norm, slots_in, slots_out = _dh_rms_bwd(
        pieces, w_pack, x2, norm_g, dout, tl["tm_in"], tl["tk"], [part_in, part_out])
    chip = 2 * xi + yi
    half_in = _sum_slots(slots_in, "chip_sum_in", lax.dynamic_index_in_dim(part_in, chip, 0, keepdims=False))
    half_out = _sum_slots(slots_out, "chip_sum_out", lax.dynamic_index_in_dim(part_out, chip, 0, keepdims=False))
    lanes_to_d = lambda t: jnp.pad(t, ((0, 0), (0, D - LANES)))
    small = jnp.concatenate([
        dg_norm[0:1], lanes_to_d(db_f[0:1, :]), dg_qk[0:1], dg_qk[1:2],
        sg_conv[2:3], sg_conv[0:1], sg_conv[1:2], dconv_w, lanes_to_d(loss_acc[0:1, :])], axis=0)
    n_small = small.shape[0]
    (other_in, other_out), all_small = _share_results([half_in, half_out], small)

    small_sum = _sum_slots(all_small, "small_sum", tr=n_small)
    loss = 0.5 * small_sum[n_small - 1, 0] / D
    grad_norm_g, grad_b_f = small_sum[0:1], small_sum[1:2, :H]
    grad_gq, grad_gk = small_sum[2:3].reshape(1, H, dh), small_sum[3:4].reshape(1, H, dh)
    grad_conv_b, grad_ln_g, grad_ln_b = small_sum[4:5], small_sum[5:6], small_sum[6:7]
    grad_conv_w = lax.dynamic_slice_in_dim(small_sum[7:7 + n_taps], chip * cw_shard, cw_shard, axis=1)

    in_t = _adamw_halves(w_t, half_in, other_in, jnp.transpose(m_w_in[0]), jnp.transpose(v_w_in[0]), "adamw_in")
    grad_w_in, d_in, nm_in, nv_in = (jnp.transpose(t)[None] for t in in_t)
    grad_w_out, d_out, nm_out, nv_out = (
        t[None] for t in _adamw_halves(w_out[0], half_out, other_out, m_w_out[0], v_w_out[0], "adamw_out"))
    d_cw, nm_cw, nv_cw = (t[None] for t in _adamw(conv_w[0], grad_conv_w, m_conv_w[0], v_conv_w[0], "adamw_conv_w"))

    def rows(ws):
        return jnp.concatenate([jnp.pad(t.reshape(1, -1), ((0, 0), (0, D - t.size))) for t in ws], axis=0)

    small_w = [norm_g, b_forget, q_norm_g, k_norm_g, conv_b, conv_ln_g, conv_ln_b]
    small_m = [m_norm_g, m_b_forget, m_q_norm_g, m_k_norm_g, m_conv_b, m_conv_ln_g, m_conv_ln_b]
    small_v = [v_norm_g, v_b_forget, v_q_norm_g, v_k_norm_g, v_conv_b, v_conv_ln_g, v_conv_ln_b]
    d_s, nm_s, nv_s = _adamw(rows(small_w), small_sum[0:7], rows(small_m), rows(small_v), "adamw_small")

    def unpack(t):
        return [t[k:k + 1, :w.size].reshape(w.shape) for k, w in enumerate(small_w)]

    def order(s, in_, cw, out_):
        ng, bf, qg, kg, cb, lg, lb = s
        return [ng, in_, bf, qg, kg, cw, cb, lg, lb, out_]

    grads = [grad_norm_g, grad_w_in, grad_b_f, grad_gq, grad_gk, grad_conv_w[None],
             grad_conv_b, grad_ln_g, grad_ln_b, grad_w_out]
    return (loss, grad_x2.reshape(B, S, D), *grads,
            *order(unpack(d_s), d_in, d_cw, d_out),
            *order(unpack(nm_s), nm_in, nm_cw, nm_out),
            *order(unpack(nv_s), nv_in, nv_cw, nv_out))
```

```python
import jax
import jax.numpy as jnp
from jax import lax
from jax.experimental import pallas as pl
from jax.experimental.pallas import tpu as pltpu

F32 = jnp.float32
BF16 = jnp.bfloat16
SDS = jax.ShapeDtypeStruct
MESH = pl.DeviceIdType.MESH

EPS = 1e-6
NEG_INF = -1e30
LANES = 128
SUBLANES = 8
HEAD_DIM = 64
HALO = 32
VMEM_LIMIT = 56 * 1024 * 1024

L_ROWSUM = 64
L_KDECAY = 67
L_LSE = 70
L_D = 65
NORM_SLACK = 1.02
SHIFT_MAX = 40.0

ADAM_LR = 0.001
ADAM_B1 = 0.9
ADAM_B2 = 0.999
ADAM_EPS = 1e-08
ADAM_WD = 0.01
ADAM_STEP = 10


def _params(sem, vmem=VMEM_LIMIT):
    return pltpu.CompilerParams(dimension_semantics=sem, vmem_limit_bytes=vmem)


def _sigmoid(x):
    return 1.0 / (1.0 + jnp.exp(-x))


def _split3(x):
    hi = x.astype(BF16).astype(F32)
    r = x - hi
    mid = r.astype(BF16).astype(F32)
    lo = (r - mid).astype(BF16).astype(F32)
    return hi, mid, lo


def _dot(a, b):
    return jnp.dot(a, b, preferred_element_type=F32)


def _dot_nt(a, b):
    return lax.dot_general(a, b, (((1,), (1,)), ((), ())), preferred_element_type=F32)


def _dot_tn(a, b):
    return lax.dot_general(a, b, (((0,), (0,)), ((), ())), preferred_element_type=F32)


def _lane(shape):
    return lax.broadcasted_iota(jnp.int32, shape, 1)


def _lane_col(x, lane, idx):
    return jnp.sum(jnp.where(lane == idx, x, 0.0), axis=-1, keepdims=True)


def _put3(base, lane, start, pieces):
    out = base
    for k, p in enumerate(pieces):
        out = jnp.where(lane == start + k, p, out)
    return out


def _half_stats(t):
    hi = t.astype(BF16)
    mid = (t - hi.astype(F32)).astype(BF16)
    row = lax.broadcasted_iota(jnp.int32, (2 * LANES, LANES), 0)
    col = lax.broadcasted_iota(jnp.int32, (2 * LANES, LANES), 1)
    same_half = (jnp.bitwise_and(row, LANES - 1) < HEAD_DIM) == (col < HEAD_DIM)
    return _dot(jnp.concatenate([hi, mid], axis=1), jnp.where(same_half, 1.0, 0.0).astype(BF16))


def _fwd_in(x2, g, w_t, tm, tn):
    T, D = x2.shape
    N = w_t.shape[0]

    def body(x_ref, g_ref, w_ref, z_ref, h_ref):
        @pl.when(pl.program_id(1) == 0)
        def _():
            x = x_ref[...]
            r = lax.rsqrt(jnp.mean(x * x, axis=-1, keepdims=True) + EPS)
            h_ref[...] = (x * r * g_ref[...]).astype(BF16)

        z_ref[...] = _dot_nt(h_ref[...], w_ref[...])

    return pl.pallas_call(
        body, name="fwd_in", grid=(T // tm, N // tn),
        in_specs=[pl.BlockSpec((tm, D), lambda i, j: (i, 0)),
                  pl.BlockSpec((1, D), lambda i, j: (0, 0)),
                  pl.BlockSpec((tn, D), lambda i, j: (j, 0))],
        out_specs=[pl.BlockSpec((tm, tn), lambda i, j: (i, j)),
                   pl.BlockSpec((tm, D), lambda i, j: (i, 0))],
        out_shape=[SDS((T, N), F32), SDS((T, D), BF16)],
        compiler_params=_params(("parallel", "arbitrary")),
    )(x2, g, w_t)


def _tri_cumsum(x, reverse):
    t = x.shape[0]
    row = lax.broadcasted_iota(jnp.int32, (t, t), 0)
    col = lax.broadcasted_iota(jnp.int32, (t, t), 1)
    tri = (row <= col) if reverse else (row >= col)
    tri = jnp.where(tri, 1.0, 0.0).astype(BF16)
    hi, mid, lo = _split3(x)
    return _dot(tri, hi.astype(BF16)) + _dot(tri, mid.astype(BF16)) + _dot(tri, lo.astype(BF16))


def _gate_fwd(z, b_pad, B, S, H, col_blk, tc):
    T = B * S
    nsb = S // tc

    def body(zf_ref, b_ref, c_ref, carry):
        @pl.when(pl.program_id(1) == 0)
        def _():
            carry[...] = jnp.zeros_like(carry)

        x = zf_ref[...] + b_ref[...]
        lf = jnp.minimum(x, 0.0) - jnp.log(1.0 + jnp.exp(-jnp.abs(x)))
        lf = jnp.where(_lane(lf.shape) < H, lf, 0.0)
        c_ref[...] = _tri_cumsum(lf, False) + carry[...]
        carry[...] = carry[...] + jnp.sum(lf, axis=0, keepdims=True)

    return pl.pallas_call(
        body, name="gate_fwd", grid=(B, nsb),
        in_specs=[pl.BlockSpec((tc, LANES), lambda b, s: (b * nsb + s, col_blk)),
                  pl.BlockSpec((1, LANES), lambda b, s: (0, 0))],
        out_specs=pl.BlockSpec((tc, LANES), lambda b, s: (b * nsb + s, 0)),
        out_shape=SDS((T, LANES), F32),
        scratch_shapes=[pltpu.VMEM((1, LANES), F32)],
        compiler_params=_params(("parallel", "arbitrary")),
    )(z, b_pad)


def _qk_normalize(x, g):
    r = lax.rsqrt(_half_stats(x * x) * (1.0 / HEAD_DIM) + EPS)
    return x * r * g


def _attn_prep(z, c, gq, gk, B, S, H, tm):
    T = B * S
    FW = H * HEAD_DIM
    nsb = S // tm
    nfb = FW // LANES
    scale = HEAD_DIM ** -0.5

    def body(zq_ref, zk_ref, zv_ref, c_ref, gq_ref, gk_ref, qa_ref, ka_ref, va_ref):
        p = pl.program_id(1)
        lane = _lane((tm, LANES))
        lo = lane < HEAD_DIM
        qn = _qk_normalize(zq_ref[...], gq_ref[...]) * scale
        kn = _qk_normalize(zk_ref[...], gk_ref[...])
        v = zv_ref[...]
        cc = c_ref[...]
        ones_q = ((lane >= L_KDECAY) & (lane < L_KDECAY + 3)).astype(F32)
        ones_k = (((lane >= L_ROWSUM) & (lane < L_ROWSUM + 3)) | ((lane >= L_LSE) & (lane < L_LSE + 3))).astype(F32)
        ones_v = ((lane >= L_ROWSUM) & (lane < L_D + 3)).astype(F32)
        for e in range(2):
            if e == 0:
                qe, ke, ve = qn, kn, v
            else:
                qe, ke, ve = (pltpu.roll(t, HEAD_DIM, 1) for t in (qn, kn, v))
            ch = _lane_col(cc, lane, 2 * p + e)
            pieces = _split3(ch)
            qa = jnp.where(lo, qe, _put3(ones_q, lane, L_ROWSUM, pieces))
            ka = jnp.where(lo, ke, _put3(ones_k, lane, L_KDECAY, [-t for t in pieces]))
            va = jnp.where(lo, ve, ones_v)
            qa_ref[0, e] = qa.astype(BF16)
            ka_ref[0, e] = ka.astype(BF16)
            va_ref[0, e] = va.astype(BF16)

    zspec = lambda off: pl.BlockSpec((tm, LANES), lambda i, p: (i, off + p))
    gspec = pl.BlockSpec((1, LANES), lambda i, p: (0, p))
    ospec = pl.BlockSpec((1, 2, tm, LANES), lambda i, p: (i // nsb, p, i % nsb, 0))
    oshape = SDS((B, H, S, LANES), BF16)
    return pl.pallas_call(
        body, name="attn_prep", grid=(T // tm, H // 2),
        in_specs=[zspec(0), zspec(nfb), zspec(2 * nfb),
                  pl.BlockSpec((tm, LANES), lambda i, p: (i, 0)), gspec, gspec],
        out_specs=[ospec, ospec, ospec],
        out_shape=[oshape, oshape, oshape],
        compiler_params=_params(("parallel", "arbitrary")),
    )(z, z, z, c, gq, gk)


def _attn_fwd(qa, ka, va, bound, t, hb):
    B, H, S, _ = qa.shape
    nq = S // t

    def body(q_ref, k_ref, v_ref, b_ref, o_ref, m_ref, acc_ref, qs_ref):
        i = pl.program_id(2)
        lane = _lane((t, LANES))

        shifts = [b_ref[e] for e in range(hb)]
        worst = shifts[0]
        for e in range(1, hb):
            worst = jnp.maximum(worst, shifts[e])
        bounded = jnp.max(worst) <= SHIFT_MAX
        acc_ref[...] = jnp.zeros_like(acc_ref)

        def tiles(step):
            def loop_body(j, carry):
                step(j, False)
                return carry

            lax.fori_loop(0, i, loop_body, 0)
            step(i, True)

        def keep_mask(n=t):
            return lax.broadcasted_iota(jnp.int32, (n, n), 0) >= lax.broadcasted_iota(jnp.int32, (n, n), 1)

        def finish(e, shift):
            acc = acc_ref[e]
            l = _lane_col(acc, lane, L_ROWSUM)
            o_ref[0, e] = jnp.where(lane < HEAD_DIM, acc / l, shift + jnp.log(l))

        @pl.when(bounded)
        def _():
            for e in range(hb):
                qs_ref[e] = _put3(q_ref[0, e].astype(F32), lane, L_LSE, _split3(-shifts[e])).astype(BF16)

            def pair(e, q_rows, k_start, n, masked):
                k_rows = pl.ds(pl.multiple_of(k_start, n), n)
                p = jnp.exp(_dot_nt(qs_ref[e, q_rows, :], k_ref[0, e, k_rows, :]))
                if masked:
                    p = jnp.where(keep_mask(n), p, 0.0)
                acc_ref[e, q_rows, :] = acc_ref[e, q_rows, :] + _dot(p.astype(BF16), v_ref[0, e, k_rows, :])

            def step(j, masked):
                for e in range(hb):
                    if masked:
                        h = t // 2
                        pair(e, slice(0, h), j * t, h, True)
                        pair(e, slice(h, t), j * t, h, False)
                        pair(e, slice(h, t), j * t + h, h, True)
                    else:
                        pair(e, slice(0, t), j * t, t, False)

            tiles(step)
            for e in range(hb):
                finish(e, shifts[e])

        @pl.when(jnp.logical_not(bounded))
        def _():
            m_ref[...] = jnp.full_like(m_ref, NEG_INF)

            def step(j, masked):
                rows = pl.ds(pl.multiple_of(j * t, t), t)
                for e in range(hb):
                    s = _dot_nt(q_ref[0, e], k_ref[0, e, rows, :])
                    if masked:
                        s = jnp.where(keep_mask(), s, NEG_INF)
                    m_prev = m_ref[e]
                    m_new = jnp.maximum(m_prev, jnp.max(s, axis=-1, keepdims=True))
                    alpha = jnp.exp(m_prev - m_new)
                    p = jnp.exp(s - m_new).astype(BF16)
                    acc_ref[e] = alpha * acc_ref[e] + _dot(p, v_ref[0, e, rows, :])
                    m_ref[e] = m_new

            tiles(step)
            for e in range(hb):
                finish(e, m_ref[e])

    return pl.pallas_call(
        body, name="attn_fwd", grid=(B, H // hb, nq),
        in_specs=[pl.BlockSpec((1, hb, t, LANES), lambda b, h, i: (b, h, i, 0)),
                  pl.BlockSpec((1, hb, S, LANES), lambda b, h, i: (b, h, 0, 0)),
                  pl.BlockSpec((1, hb, S, LANES), lambda b, h, i: (b, h, 0, 0)),
                  pl.BlockSpec((hb, 1, LANES), lambda b, h, i: (h, 0, 0))],
        out_specs=pl.BlockSpec((1, hb, t, LANES), lambda b, h, i: (b, h, i, 0)),
        out_shape=SDS((B, H, S, LANES), F32),
        scratch_shapes=[pltpu.VMEM((hb, t, 1), F32), pltpu.VMEM((hb, t, LANES), F32),
                        pltpu.VMEM((hb, t, LANES), BF16)],
        compiler_params=_params(("parallel", "parallel", "arbitrary")),
    )(qa, ka, va, bound)


def _fill_shifts(ext_ref, sh_ref):
    rows = sh_ref.shape[1]
    for b in range(1, SUBLANES):
        sh_ref[b - 1] = ext_ref[pl.ds(b, rows), :]


def _tap_window(ext_ref, sh_ref, off, tm, cols):
    b = off % SUBLANES
    if b == 0:
        return ext_ref[pl.ds(off, tm), cols]
    return sh_ref[b - 1, pl.ds(off - b, tm), cols]


def _conv_taps(w_ref, ext_ref, sh_ref, out_ref, n_taps, tm, offset_of, bias_ref=None):
    for cc in range(out_ref.shape[1] // LANES):
        cols = slice(cc * LANES, (cc + 1) * LANES)
        acc = None
        for j in sorted(range(n_taps), key=offset_of):
            term = w_ref[j:j + 1, cols] * _tap_window(ext_ref, sh_ref, offset_of(j), tm, cols)
            acc = term if acc is None else acc + term
        out_ref[:, cols] = acc if bias_ref is None else acc + bias_ref[:, cols]


def _layernorm_stats(u2):
    mu = jnp.mean(u2, axis=-1, keepdims=True)
    xc = u2 - mu
    rstd = lax.rsqrt(jnp.mean(xc * xc, axis=-1, keepdims=True) + EPS)
    return xc * rstd, rstd


def _fwd_out(oa, z, x2, tgt, conv_w, conv_b, ln_g, ln_b, w_out, B, S, H, n_taps, tm):
    T, D = x2.shape
    FW = H * HEAD_DIM
    CW = conv_w.shape[1]
    nsb = S // tm
    hb = tm // HALO
    mb = 4 if nsb % 4 == 0 else 1
    mt = mb * tm

    def body(oa_ref, gf_ref, ga_ref, gb_ref, gc_ref, ha_ref, hb_ref, x_ref, t_ref, w_ref, cb_ref, lg_ref,
             lb_ref, wo_ref, y_ref, u2_ref, a_ref, do_ref, dob_ref, dy_ref, loss_ref, ext_ref, sh_ref):
        first_step = (pl.program_id(0) == 0) & (pl.program_id(1) == 0)
        sub = lax.rem(pl.program_id(1), mb)
        rows = pl.ds(pl.multiple_of(sub * tm, tm), tm)

        @pl.when(first_step)
        def _():
            loss_ref[...] = jnp.zeros_like(loss_ref)

        u1 = ga_ref[...] * _sigmoid(gb_ref[...])
        halo = ha_ref[...] * _sigmoid(hb_ref[...])
        ext_ref[0:HALO, :] = jnp.where(pl.program_id(1) > 0, halo, 0.0)
        ext_ref[HALO:, :] = u1
        _fill_shifts(ext_ref, sh_ref)
        _conv_taps(w_ref, ext_ref, sh_ref, u2_ref, n_taps, tm, lambda j: HALO - (n_taps - 1) + j, cb_ref)
        uh, _ = _layernorm_stats(u2_ref[...])
        u3 = uh * lg_ref[...] + lb_ref[...]
        gc = gc_ref[...]
        yu = u3 * _sigmoid(u3) * (gc * _sigmoid(gc))
        y_ref[rows, FW:] = yu.astype(BF16)

        lane = _lane((tm, LANES))
        lo = lane < HEAD_DIM
        for p in range(H // 2):
            a_ref[:, p * LANES:(p + 1) * LANES] = jnp.where(
                lo, oa_ref[0, 2 * p], pltpu.roll(oa_ref[0, 2 * p + 1], HEAD_DIM, 1))
        gf = gf_ref[...]
        y_ref[rows, :FW] = (a_ref[...] * (gf * _sigmoid(gf))).astype(BF16)

        @pl.when(sub == mb - 1)
        def _():
            out = x_ref[...] + _dot(y_ref[...], wo_ref[...])
            diff = out - t_ref[...]
            loss_ref[...] = loss_ref[...] + jnp.sum(diff * diff)
            dout = diff * (1.0 / D)
            do_ref[...] = dout
            dob = dout.astype(BF16)
            dob_ref[...] = dob
            dy_ref[...] = _dot_nt(dob, wo_ref[...])

    row = lambda b, s: b * nsb + s
    zspec = lambda cb: pl.BlockSpec((tm, FW), lambda b, s: (row(b, s), cb))
    hspec = lambda cb: pl.BlockSpec((HALO, CW), lambda b, s: (jnp.maximum(row(b, s) * hb - 1, 0), cb))
    vspec = pl.BlockSpec((1, CW), lambda b, s: (0, 0))
    tspec = lambda w: pl.BlockSpec((tm, w), lambda b, s: (row(b, s), 0))
    mspec = lambda w: pl.BlockSpec((mt, w), lambda b, s: (row(b, s) // mb, 0))
    return pl.pallas_call(
        body, name="fwd_out", grid=(B, nsb),
        in_specs=[pl.BlockSpec((1, H, tm, LANES), lambda b, s: (b, 0, s, 0)),
                  zspec(3), zspec(4), zspec(5), zspec(6), hspec(4), hspec(5),
                  mspec(D), mspec(D),
                  pl.BlockSpec((HALO, CW), lambda b, s: (0, 0)), vspec, vspec, vspec,
                  pl.BlockSpec((FW + CW, D), lambda b, s: (0, 0))],
        out_specs=[mspec(FW + CW), tspec(CW), tspec(FW), mspec(D), mspec(D), mspec(FW + CW),
                   pl.BlockSpec((8, LANES), lambda b, s: (0, 0))],
        out_shape=[SDS((T, FW + CW), BF16), SDS((T, CW), F32), SDS((T, FW), F32), SDS((T, D), F32),
                   SDS((T, D), BF16), SDS((T, FW + CW), F32), SDS((8, LANES), F32)],
        scratch_shapes=[pltpu.VMEM((tm + HALO, CW), F32),
                        pltpu.VMEM((SUBLANES - 1, tm + HALO - SUBLANES, CW), F32)],
        compiler_params=_params(("arbitrary", "arbitrary")),
    )(oa, z, z, z, z, z, z, x2, tgt, conv_w, conv_b, ln_g, ln_b, w_out)


def _bwd_prep(dy, z, a_nat, oa, qa, u2, ln_g, ln_b, B, S, H, tm):
    T = B * S
    FW = H * HEAD_DIM
    CW = u2.shape[1]
    nsb = S // tm

    def body(dya_ref, dyu_ref, gf_ref, gc_ref, a_ref, oa_ref, qa_ref, u2_ref, lg_ref, lb_ref,
             dzgf_ref, dzgc_ref, du2_ref, doa_ref, qb_ref, sg_ref):
        first_step = (pl.program_id(0) == 0) & (pl.program_id(1) == 0)

        @pl.when(first_step)
        def _():
            sg_ref[...] = jnp.zeros_like(sg_ref)

        gf = gf_ref[...]
        sg = _sigmoid(gf)
        a = a_ref[...]
        dya = dya_ref[...]
        da = dya * (gf * sg)
        dzgf_ref[...] = (dya * a * (sg * (1.0 + gf * (1.0 - sg)))).astype(BF16)
        dd = da * a
        lane = _lane((tm, LANES))
        lo = lane < HEAD_DIM
        for p in range(H // 2):
            cols = slice(p * LANES, (p + 1) * LANES)
            da_p = da[:, cols]
            dd_p = dd[:, cols]
            d_heads = (jnp.sum(jnp.where(lo, dd_p, 0.0), axis=-1, keepdims=True),
                       jnp.sum(jnp.where(lo, 0.0, dd_p), axis=-1, keepdims=True))
            for e in range(2):
                da_e = da_p if e == 0 else pltpu.roll(da_p, HEAD_DIM, 1)
                d_e = d_heads[e]
                aug = _put3(jnp.zeros((tm, LANES), F32), lane, L_D, _split3(-d_e))
                doa_ref[0, 2 * p + e] = jnp.where(lo, da_e, aug).astype(BF16)
                lse = _lane_col(oa_ref[0, 2 * p + e], lane, L_ROWSUM)
                qb = _put3(qa_ref[0, 2 * p + e].astype(F32), lane, L_LSE, _split3(-lse))
                qb_ref[0, 2 * p + e] = qb.astype(BF16)

        gc = gc_ref[...]
        sc = _sigmoid(gc)
        dyu = dyu_ref[...]
        uh, rstd = _layernorm_stats(u2_ref[...])
        u3 = uh * lg_ref[...] + lb_ref[...]
        s3 = _sigmoid(u3)
        dzgc_ref[...] = (dyu * (u3 * s3) * (sc * (1.0 + gc * (1.0 - sc)))).astype(BF16)
        du3 = dyu * (gc * sc) * (s3 * (1.0 + u3 * (1.0 - s3)))
        sg_ref[0:1, :] = sg_ref[0:1, :] + jnp.sum(du3 * uh, axis=0, keepdims=True)
        sg_ref[1:2, :] = sg_ref[1:2, :] + jnp.sum(du3, axis=0, keepdims=True)
        duh = du3 * lg_ref[...]
        du2 = rstd * (duh - jnp.mean(duh, axis=-1, keepdims=True)
                      - uh * jnp.mean(duh * uh, axis=-1, keepdims=True))
        sg_ref[2:3, :] = sg_ref[2:3, :] + jnp.sum(du2, axis=0, keepdims=True)
        du2_ref[...] = du2

    row = lambda b, s: b * nsb + s
    tspec = lambda w, cb=0: pl.BlockSpec((tm, w), lambda b, s: (row(b, s), cb))
    hspec = pl.BlockSpec((1, H, tm, LANES), lambda b, s: (b, 0, s, 0))
    vspec = pl.BlockSpec((1, CW), lambda b, s: (0, 0))
    return pl.pallas_call(
        body, name="bwd_prep", grid=(B, nsb),
        in_specs=[tspec(FW, 0), tspec(CW, 1), tspec(FW, 3), tspec(CW, 6), tspec(FW), hspec, hspec,
                  tspec(CW), vspec, vspec],
        out_specs=[tspec(FW), tspec(CW), tspec(CW), hspec, hspec,
                   pl.BlockSpec((8, CW), lambda b, s: (0, 0))],
        out_shape=[SDS((T, FW), BF16), SDS((T, CW), BF16), SDS((T, CW), F32),
                   SDS((B, H, S, LANES), BF16), SDS((B, H, S, LANES), BF16), SDS((8, CW), F32)],
        compiler_params=_params(("arbitrary", "arbitrary")),
    )(dy, dy, z, z, a_nat, oa, qa, u2, ln_g, ln_b)


def _conv_bwd(du2, z, conv_w, B, S, n_taps, tm):
    T, CW = du2.shape
    nsb = S // tm
    hb = tm // HALO

    def body(d_ref, dh_ref, ga_ref, gb_ref, ha_ref, hb_ref, w_ref, dz_ref, dw_ref,
             extu_ref, extd_ref, shu_ref, shd_ref, du1_ref, dwacc_ref):
        s = pl.program_id(1)
        first_step = (pl.program_id(0) == 0) & (s == 0)
        last_step = (pl.program_id(0) == B - 1) & (s == nsb - 1)

        @pl.when(first_step)
        def _():
            dwacc_ref[...] = jnp.zeros_like(dwacc_ref)

        ga = ga_ref[...]
        sb = _sigmoid(gb_ref[...])
        halo = ha_ref[...] * _sigmoid(hb_ref[...])
        extu_ref[0:HALO, :] = jnp.where(s > 0, halo, 0.0)
        extu_ref[HALO:, :] = ga * sb
        extd_ref[0:tm, :] = d_ref[...]
        extd_ref[tm:, :] = jnp.where(s < nsb - 1, dh_ref[...], 0.0)
        _fill_shifts(extu_ref, shu_ref)
        _fill_shifts(extd_ref, shd_ref)
        _conv_taps(w_ref, extd_ref, shd_ref, du1_ref, n_taps, tm, lambda j: n_taps - 1 - j)
        for cc in range(CW // LANES):
            cols = slice(cc * LANES, (cc + 1) * LANES)
            parts = [None] * n_taps
            for r in range(tm // SUBLANES):
                dv = d_ref[r * SUBLANES:(r + 1) * SUBLANES, cols]
                for j in range(n_taps):
                    off = HALO - (n_taps - 1) + j + r * SUBLANES
                    term = dv * _tap_window(extu_ref, shu_ref, off, SUBLANES, cols)
                    parts[j] = term if parts[j] is None else parts[j] + term
            for j in range(n_taps):
                rows = slice(j * SUBLANES, (j + 1) * SUBLANES)
                dwacc_ref[rows, cols] = dwacc_ref[rows, cols] + parts[j]
        du1 = du1_ref[...]
        dz_ref[:, :CW] = (du1 * sb).astype(BF16)
        dz_ref[:, CW:] = (du1 * ga * (sb * (1.0 - sb))).astype(BF16)

        @pl.when(last_step)
        def _():
            dw_ref[...] = jnp.zeros_like(dw_ref)
            for j in range(n_taps):
                dw_ref[j:j + 1, :] = jnp.sum(dwacc_ref[j * SUBLANES:(j + 1) * SUBLANES, :], axis=0, keepdims=True)

    row = lambda b, s: b * nsb + s
    last_halo = T // HALO - 1
    return pl.pallas_call(
        body, name="conv_bwd", grid=(B, nsb),
        in_specs=[pl.BlockSpec((tm, CW), lambda b, s: (row(b, s), 0)),
                  pl.BlockSpec((HALO, CW), lambda b, s: (jnp.minimum((row(b, s) + 1) * hb, last_halo), 0)),
                  pl.BlockSpec((tm, CW), lambda b, s: (row(b, s), 4)),
                  pl.BlockSpec((tm, CW), lambda b, s: (row(b, s), 5)),
                  pl.BlockSpec((HALO, CW), lambda b, s: (jnp.maximum(row(b, s) * hb - 1, 0), 4)),
                  pl.BlockSpec((HALO, CW), lambda b, s: (jnp.maximum(row(b, s) * hb - 1, 0), 5)),
                  pl.BlockSpec((HALO, CW), lambda b, s: (0, 0))],
        out_specs=[pl.BlockSpec((tm, 2 * CW), lambda b, s: (row(b, s), 0)),
                   pl.BlockSpec((HALO, CW), lambda b, s: (0, 0))],
        out_shape=[SDS((T, 2 * CW), BF16), SDS((HALO, CW), F32)],
        scratch_shapes=[pltpu.VMEM((tm + HALO, CW), F32), pltpu.VMEM((tm + HALO, CW), F32),
                        pltpu.VMEM((SUBLANES - 1, tm + HALO - SUBLANES, CW), F32),
                        pltpu.VMEM((SUBLANES - 1, tm + HALO - SUBLANES, CW), F32),
                        pltpu.VMEM((tm, CW), F32), pltpu.VMEM((HALO * SUBLANES, CW), F32)],
        compiler_params=_params(("arbitrary", "arbitrary")),
    )(du2, du2, z, z, z, z, conv_w)


def _attn_bwd(qb, ka, va, doa, t, hb):
    B, H, S, _ = qb.shape
    nk = S // t

    def body(q_ref, k_ref, v_ref, do_ref, dq_ref, dk_ref, dv_ref, dv_acc):
        j = pl.program_id(2)

        @pl.when(j == 0)
        def _():
            dq_ref[...] = jnp.zeros_like(dq_ref)

        dk_ref[...] = jnp.zeros_like(dk_ref)
        dv_acc[...] = jnp.zeros_like(dv_acc)

        def step(i, masked):
            q_rows = pl.ds(pl.multiple_of(i * t, t), t)
            for e in range(hb):
                k = k_ref[0, e]
                q = q_ref[0, e, q_rows, :]
                do = do_ref[0, e, q_rows, :]
                p = jnp.exp(_dot_nt(q, k))
                if masked:
                    keep = lax.broadcasted_iota(jnp.int32, (t, t), 0) >= lax.broadcasted_iota(jnp.int32, (t, t), 1)
                    p = jnp.where(keep, p, 0.0)
                ds = (p * _dot_nt(do, v_ref[0, e])).astype(BF16)
                dv_acc[e] = dv_acc[e] + _dot_tn(p.astype(BF16), do)
                dk_ref[0, e] = dk_ref[0, e] + _dot_tn(ds, q)
                dq_ref[0, e, q_rows, :] = dq_ref[0, e, q_rows, :] + _dot(ds, k)

        step(j, True)

        def loop_body(i, carry):
            step(i, False)
            return carry

        lax.fori_loop(j + 1, nk, loop_body, 0)
        dv_ref[0] = dv_acc[...].astype(BF16)

    full = pl.BlockSpec((1, hb, S, LANES), lambda b, h, j: (b, h, 0, 0))
    blk = pl.BlockSpec((1, hb, t, LANES), lambda b, h, j: (b, h, j, 0))
    oshape = SDS((B, H, S, LANES), F32)
    return pl.pallas_call(
        body, name="attn_bwd", grid=(B, H // hb, nk),
        in_specs=[full, blk, blk, full],
        out_specs=[full, blk, blk],
        out_shape=[oshape, oshape, SDS((B, H, S, LANES), BF16)],
        scratch_shapes=[pltpu.VMEM((hb, t, LANES), F32)],
        compiler_params=_params(("parallel", "parallel", "arbitrary")),
    )(qb, ka, va, doa)


def _qk_bwd(dqa, dka, dva, z, gq, gk, B, S, H, tm):
    T = B * S
    FW = H * HEAD_DIM
    nsb = S // tm
    nfb = FW // LANES
    scale = HEAD_DIM ** -0.5

    def body(dq_ref, dk_ref, dv_ref, zq_ref, zk_ref, gq_ref, gk_ref, dzq_ref, dzk_ref, dzv_ref, dc_ref, dg_ref):
        p = pl.program_id(0)

        @pl.when(pl.program_id(1) == 0)
        def _():
            dg_ref[...] = jnp.zeros_like(dg_ref)

        lane = _lane((tm, LANES))
        lo = lane < HEAD_DIM

        def natural(ref):
            return jnp.where(lo, ref[0, 0].astype(F32), pltpu.roll(ref[0, 1].astype(F32), HEAD_DIM, 1))

        def norm_bwd(dn, x, g, row, out_ref):
            r = lax.rsqrt(_half_stats(x * x) * (1.0 / HEAD_DIM) + EPS)
            xh = x * r
            dg_ref[row:row + 1, :] = dg_ref[row:row + 1, :] + jnp.sum(dn * xh, axis=0, keepdims=True)
            dxh = dn * g
            mm = _half_stats(dxh * xh) * (1.0 / HEAD_DIM)
            out_ref[...] = (r * (dxh - xh * mm)).astype(BF16)

        norm_bwd(natural(dq_ref) * scale, zq_ref[...], gq_ref[...], 0, dzq_ref)
        norm_bwd(natural(dk_ref), zk_ref[...], gk_ref[...], 1, dzk_ref)
        dzv_ref[...] = natural(dv_ref).astype(BF16)

        dc = jnp.zeros((tm, LANES), F32)
        for e in range(2):
            val = _lane_col(dq_ref[0, e], lane, L_ROWSUM) - _lane_col(dk_ref[0, e], lane, L_KDECAY)
            dc = jnp.where(lane == 2 * p + e, val, dc)
        dc_ref[0] = dc

    hspec = pl.BlockSpec((1, 2, tm, LANES), lambda p, i: (i // nsb, p, i % nsb, 0))
    zspec = lambda off: pl.BlockSpec((tm, LANES), lambda p, i: (i, off + p))
    gspec = pl.BlockSpec((1, LANES), lambda p, i: (0, p))
    ospec = pl.BlockSpec((tm, LANES), lambda p, i: (i, p))
    return pl.pallas_call(
        body, name="qk_bwd", grid=(H // 2, T // tm),
        in_specs=[hspec, hspec, hspec, zspec(0), zspec(nfb), gspec, gspec],
        out_specs=[ospec, ospec, ospec,
                   pl.BlockSpec((1, tm, LANES), lambda p, i: (p, i, 0)),
                   pl.BlockSpec((8, LANES), lambda p, i: (0, p))],
        out_shape=[SDS((T, FW), BF16), SDS((T, FW), BF16), SDS((T, FW), BF16),
                   SDS((H // 2, T, LANES), F32), SDS((8, FW), F32)],
        compiler_params=_params(("parallel", "arbitrary")),
    )(dqa, dka, dva, z, z, gq, gk)


def _gate_bwd(dc8, z, b_pad, B, S, H, col_blk, fp, tc):
    T = B * S
    nsb = S // tc
    npair = dc8.shape[0]

    def body(dc_ref, zf_ref, b_ref, dz_ref, db_ref, carry):
        first_step = (pl.program_id(0) == 0) & (pl.program_id(1) == 0)

        @pl.when(first_step)
        def _():
            db_ref[...] = jnp.zeros_like(db_ref)

        @pl.when(pl.program_id(1) == 0)
        def _():
            carry[...] = jnp.zeros_like(carry)

        dc = dc_ref[0]
        for k in range(1, npair):
            dc = dc + dc_ref[k]
        dlf = _tri_cumsum(dc, True) + carry[...]
        carry[...] = carry[...] + jnp.sum(dc, axis=0, keepdims=True)
        x = zf_ref[...] + b_ref[...]
        dlogit = dlf * _sigmoid(-x)
        db_ref[0:1, :] = db_ref[0:1, :] + jnp.sum(dlogit, axis=0, keepdims=True)
        dz_ref[...] = jnp.zeros_like(dz_ref)
        dz_ref[:, :LANES] = dlogit.astype(BF16)

    rrow = lambda b, s: b * nsb + (nsb - 1 - s)
    return pl.pallas_call(
        body, name="gate_bwd", grid=(B, nsb),
        in_specs=[pl.BlockSpec((npair, tc, LANES), lambda b, s: (0, rrow(b, s), 0)),
                  pl.BlockSpec((tc, LANES), lambda b, s: (rrow(b, s), col_blk)),
                  pl.BlockSpec((1, LANES), lambda b, s: (0, 0))],
        out_specs=[pl.BlockSpec((tc, fp), lambda b, s: (rrow(b, s), 0)),
                   pl.BlockSpec((8, LANES), lambda b, s: (0, 0))],
        out_shape=[SDS((T, fp), BF16), SDS((8, LANES), F32)],
        scratch_shapes=[pltpu.VMEM((1, LANES), F32)],
        compiler_params=_params(("arbitrary", "arbitrary")),
    )(dc8, z, b_pad)


def _matmul_tn(a, b, name, tmm, tn, tk):
    T, M = a.shape
    N = b.shape[1]
    tmm, tn, tk = min(tmm, M), min(tn, N), min(tk, T)

    def body(a_ref, b_ref, o_ref):
        @pl.when(pl.program_id(2) == 0)
        def _():
            o_ref[...] = jnp.zeros_like(o_ref)

        o_ref[...] = o_ref[...] + _dot_tn(a_ref[...], b_ref[...])

    return pl.pallas_call(
        body, name=name, grid=(M // tmm, N // tn, T // tk),
        in_specs=[pl.BlockSpec((tk, tmm), lambda i, j, k: (k, i)),
                  pl.BlockSpec((tk, tn), lambda i, j, k: (k, j))],
        out_specs=pl.BlockSpec((tmm, tn), lambda i, j, k: (i, j)),
        out_shape=SDS((M, N), F32),
        compiler_params=_params(("parallel", "parallel", "arbitrary")),
    )(a, b)


def _dh_rms_bwd(pieces, w_t, x2, g, dout, tm, tk, parts):
    T, D = x2.shape
    nks = [p.shape[1] // tk for p in pieces]
    starts = [sum(nks[:k]) for k in range(len(pieces))]
    nk = sum(nks)
    ni = T // tm
    n = len(parts)
    rc = min(256, tm)
    chunks = [slice(r, r + rc) for r in range(0, tm, rc)]

    def body(*refs):
        dz_refs = refs[:len(pieces)]
        w_ref, x_ref, g_ref, do_ref = refs[len(pieces):len(pieces) + 4]
        part_refs = refs[len(pieces) + 4:len(pieces) + 4 + n]
        gx_ref, dg_ref = refs[len(pieces) + 4 + n:len(pieces) + 6 + n]
        slot_refs = refs[len(pieces) + 6 + n:len(pieces) + 6 + 2 * n]
        acc_ref, send_sems, recv_sems = refs[len(pieces) + 6 + 2 * n:]
        k = pl.program_id(1)
        first_step = (pl.program_id(0) == 0) & (k == 0)
        last_step = (pl.program_id(0) == ni - 1) & (k == nk - 1)
        x, y, c = _place()
        chips = [(1 - x, y), (x, 1 - y), (1 - x, 1 - y)]

        def copy(a, f, to):
            cx, cy = chips[f]
            return pltpu.make_async_remote_copy(
                src_ref=part_refs[a].at[2 * cx + cy], dst_ref=slot_refs[a].at[f],
                send_sem=send_sems.at[a * 3 + f], recv_sem=recv_sems.at[a * 3 + f],
                device_id=to, device_id_type=MESH)

        @pl.when(first_step)
        def _():
            dg_ref[...] = jnp.zeros_like(dg_ref)
            for a in range(n):
                for f in range(3):
                    copy(a, f, (*chips[f], c)).start()

        @pl.when(last_step)
        def _():
            for a in range(n):
                for f in range(3):
                    copy(a, f, (x, y, c)).wait_recv()
            for a in range(n):
                for f in range(3):
                    copy(a, f, (*chips[f], c)).wait_send()

        @pl.when(k == 0)
        def _():
            acc_ref[...] = jnp.zeros_like(acc_ref)

        for dz_ref, st, cnt in zip(dz_refs, starts, nks):
            @pl.when((k >= st) & (k < st + cnt))
            def _(dz_ref=dz_ref):
                for rows in chunks:
                    acc_ref[rows, :] = acc_ref[rows, :] + _dot(dz_ref[rows, :], w_ref[...])

        @pl.when(k == nk - 1)
        def _():
            for rows in chunks:
                x = x_ref[rows, :]
                r = lax.rsqrt(jnp.mean(x * x, axis=-1, keepdims=True) + EPS)
                xh = x * r
                dh = acc_ref[rows, :]
                dg_ref[0:1, :] = dg_ref[0:1, :] + jnp.sum(dh * xh, axis=0, keepdims=True)
                dxn = dh * g_ref[...]
                gx_ref[rows, :] = do_ref[rows, :] + r * (dxn - xh * jnp.mean(dxn * xh, axis=-1, keepdims=True))

    def piece_spec(st, cnt):
        return pl.BlockSpec((tm, tk), lambda i, k: (i, jnp.clip(k - st, 0, cnt - 1)))

    tspec = pl.BlockSpec((tm, D), lambda i, k: (i, 0))
    return pl.pallas_call(
        body, name="dh_rms_bwd", grid=(T // tm, nk),
        in_specs=[piece_spec(st, cnt) for st, cnt in zip(starts, nks)]
        + [pl.BlockSpec((tk, D), lambda i, k: (k, 0)), tspec, pl.BlockSpec((1, D), lambda i, k: (0, 0)), tspec]
        + [ANY] * n,
        out_specs=[tspec, pl.BlockSpec((8, D), lambda i, k: (0, 0))] + [ANY] * n,
        out_shape=[SDS((T, D), F32), SDS((8, D), F32)] + [SDS((3,) + p.shape[1:], p.dtype) for p in parts],
        scratch_shapes=[pltpu.VMEM((tm, D), F32),
                        pltpu.SemaphoreType.DMA((3 * n,)), pltpu.SemaphoreType.DMA((3 * n,))],
        compiler_params=_params(("arbitrary", "arbitrary")),
    )(*pieces, w_t, x2, g, dout, *parts)


def _block_plan(R, C, tr, tc):
    br = min(tr, R)
    if R % br == 0:
        return (br, C), R // br, lambda i: (i, 0)
    bc = min(tc, C)
    assert C % bc == 0
    return (R, bc), C // bc, lambda i: (0, i)


def _ew_call(body, name, ins, n_out, out_dtypes, tr, tc):
    R, C = ins[0].shape
    blk, steps, imap = _block_plan(R, C, tr, tc)
    spec = pl.BlockSpec(blk, imap)
    return pl.pallas_call(
        body, name=name, grid=(steps,),
        in_specs=[spec] * len(ins), out_specs=[spec] * n_out,
        out_shape=[SDS((R, C), dt) for dt in out_dtypes],
        compiler_params=_params(("parallel",)),
    )(*ins)


def _sum_slots(slots, name, first=None, tr=256):
    n, R, C = slots.shape
    blk, steps, imap = _block_plan(R, C, tr, 2 * LANES)
    lead = [] if first is None else [first]

    def body(*refs):
        s_ref, o_ref = refs[-2:]
        acc = refs[0][...].astype(F32) if lead else s_ref[0].astype(F32)
        for k in range(0 if lead else 1, n):
            acc = acc + s_ref[k].astype(F32)
        o_ref[...] = acc

    return pl.pallas_call(
        body, name=name, grid=(steps,),
        in_specs=[pl.BlockSpec(blk, imap)] * len(lead) + [pl.BlockSpec((n,) + blk, lambda i: (0,) + imap(i))],
        out_specs=pl.BlockSpec(blk, imap),
        out_shape=SDS((R, C), F32),
        compiler_params=_params(("parallel",)),
    )(*lead, slots)


def _adamw_update(w, g, m, v):
    nm = ADAM_B1 * m + (1.0 - ADAM_B1) * g
    nv = ADAM_B2 * v + (1.0 - ADAM_B2) * (g * g)
    m_hat = nm / (1.0 - ADAM_B1 ** ADAM_STEP)
    v_hat = nv / (1.0 - ADAM_B2 ** ADAM_STEP)
    return -ADAM_LR * (m_hat / (jnp.sqrt(v_hat) + ADAM_EPS) + ADAM_WD * w), nm, nv


def _adamw(w, g, m, v, name):
    def body(w_ref, g_ref, m_ref, v_ref, d_ref, nm_ref, nv_ref):
        d_ref[...], nm_ref[...], nv_ref[...] = _adamw_update(w_ref[...], g_ref[...], m_ref[...], v_ref[...])

    return _ew_call(body, name, [w, g, m, v], 3, [F32, F32, F32], 128, 2 * LANES)


def _adamw_halves(w, mine, other, m, v, name):
    R, C = w.shape
    half = C // 2
    bc = min(2 * LANES, half)
    per = half // bc

    def body(w_ref, a_ref, b_ref, m_ref, v_ref, g_ref, d_ref, nm_ref, nv_ref):
        g = jnp.where(pl.program_id(0) // per == lax.axis_index("c"), a_ref[...], b_ref[...])
        g_ref[...] = g
        d_ref[...], nm_ref[...], nv_ref[...] = _adamw_update(w_ref[...], g, m_ref[...], v_ref[...])

    full = pl.BlockSpec((R, bc), lambda i: (0, i))
    part = pl.BlockSpec((R, bc), lambda i: (0, i % per))
    return pl.pallas_call(
        body, name=name, grid=(C // bc,),
        in_specs=[full, part, part, full, full], out_specs=[full] * 4,
        out_shape=[SDS((R, C), F32)] * 4,
        compiler_params=_params(("parallel",)),
    )(w, mine, other, m, v)


ANY = pl.BlockSpec(memory_space=pl.ANY)


def _place():
    return lax.axis_index("x"), lax.axis_index("y"), lax.axis_index("c")


def _gather_chips(shards, splits):
    n = len(shards)
    per = 7

    def body(*refs):
        ins, outs = refs[:n], refs[n:2 * n]
        send_sems, recv_sems = refs[2 * n:]
        x, y, c = _place()
        mine = 2 * x + y
        chips = [(1 - x, y), (x, 1 - y), (1 - x, 1 - y)]

        def rows(a, half):
            return pl.ds(0, splits[a]) if half == 0 else pl.ds(splits[a], ins[a].shape[0] - splits[a])

        def copy(a, k, chip_idx, half, to, src=None):
            dst = outs[a].at[chip_idx, rows(a, half)]
            return pltpu.make_async_remote_copy(
                src_ref=dst if src is None else src, dst_ref=dst,
                send_sem=send_sems.at[a * per + k], recv_sem=recv_sems.at[a * per + k],
                device_id=to, device_id_type=MESH)

        def own(a, to):
            return pltpu.make_async_remote_copy(
                src_ref=ins[a], dst_ref=outs[a].at[mine],
                send_sem=send_sems.at[a * per + 6], recv_sem=recv_sems.at[a * per + 6],
                device_id=to, device_id_type=MESH)

        for cc in (0, 1):
            @pl.when(c == cc)
            def _(cc=cc):
                me, sibling = (x, y, cc), (x, y, 1 - cc)
                first = [copy(a, k, mine, cc, (*chip, cc), src=ins[a].at[rows(a, cc)])
                         for a in range(n) for k, chip in enumerate(chips)]
                first += [own(a, sibling) for a in range(n)]
                for cp in first:
                    cp.start()
                passed = []
                for k, (cx, cy) in enumerate(chips):
                    for a in range(n):
                        copy(a, k, 2 * cx + cy, cc, me).wait_recv()
                        fwd = copy(a, 3 + k, 2 * cx + cy, cc, sibling)
                        fwd.start()
                        passed.append(fwd)
                for k, (cx, cy) in enumerate(chips):
                    for a in range(n):
                        copy(a, 3 + k, 2 * cx + cy, 1 - cc, me).wait_recv()
                for a in range(n):
                    own(a, me).wait_recv()
                for cp in first + passed:
                    cp.wait_send()

    return pl.pallas_call(
        body, name="gather_chips",
        in_specs=[ANY] * n, out_specs=[ANY] * n,
        out_shape=[SDS((4,) + s.shape, s.dtype) for s in shards],
        scratch_shapes=[pltpu.SemaphoreType.DMA((per * n,)), pltpu.SemaphoreType.DMA((per * n,))],
    )(*shards)


def _pair_swap(arrs):
    n = len(arrs)

    def body(*refs):
        ins, outs = refs[:n], refs[n:2 * n]
        send_sems, recv_sems = refs[2 * n:]
        x, y, c = _place()
        for cc in (0, 1):
            @pl.when(c == cc)
            def _(cc=cc):
                copies = []
                for a in range(n):
                    half = ins[a].shape[1] // 2
                    copies.append(pltpu.make_async_remote_copy(
                        src_ref=ins[a].at[:, pl.ds((1 - cc) * half, half)], dst_ref=outs[a],
                        send_sem=send_sems.at[a], recv_sem=recv_sems.at[a],
                        device_id=(x, y, 1 - cc), device_id_type=MESH))
                for cp in copies:
                    cp.start()
                for cp in copies:
                    cp.wait()

    return pl.pallas_call(
        body, name="pair_swap",
        in_specs=[ANY] * n, out_specs=[ANY] * n,
        out_shape=[SDS((h.shape[0], h.shape[1] // 2), h.dtype) for h in arrs],
        scratch_shapes=[pltpu.SemaphoreType.DMA((n,)), pltpu.SemaphoreType.DMA((n,))],
    )(*arrs)


def _pair_sum(arrs, got, core, tr):
    half = arrs[0].shape[1] // 2
    cnts = [p.shape[0] // tr for p in arrs]
    starts = [sum(cnts[:k]) for k in range(len(arrs))]

    def body(core_ref, *refs):
        del core_ref
        own_refs, got_refs, o_ref = refs[:len(arrs)], refs[len(arrs):2 * len(arrs)], refs[-1]
        s = pl.program_id(0)
        for own_ref, got_ref, st, cnt in zip(own_refs, got_refs, starts, cnts):
            @pl.when((s >= st) & (s < st + cnt))
            def _(own_ref=own_ref, got_ref=got_ref):
                o_ref[...] = (own_ref[...] + got_ref[...]).astype(BF16)

    def own_spec(st, cnt):
        return pl.BlockSpec((tr, half), lambda s, core_ref: (jnp.clip(s - st, 0, cnt - 1), core_ref[0]))

    def got_spec(st, cnt):
        return pl.BlockSpec((tr, half), lambda s, core_ref: (jnp.clip(s - st, 0, cnt - 1), 0))

    return pl.pallas_call(
        body, name="pair_sum",
        grid_spec=pltpu.PrefetchScalarGridSpec(
            num_scalar_prefetch=1, grid=(sum(cnts),),
            in_specs=[own_spec(st, cnt) for st, cnt in zip(starts, cnts)]
            + [got_spec(st, cnt) for st, cnt in zip(starts, cnts)],
            out_specs=pl.BlockSpec((tr, half), lambda s, core_ref: (s, 0))),
        out_shape=SDS((sum(cnts) * tr, half), BF16),
        compiler_params=_params(("arbitrary",)),
    )(core, *arrs, *got)


def _share_results(arrs, rows):
    n = len(arrs)
    flips = [(fx, fy, fc) for fx in (0, 1) for fy in (0, 1) for fc in (0, 1)][1:]

    def body(*refs):
        ins, rows_ref, outs, all_ref = refs[:n], refs[n], refs[n + 1:2 * n + 1], refs[2 * n + 1]
        send_sems, recv_sems, local_sem = refs[2 * n + 2:]
        x, y, c = _place()
        me = 4 * x + 2 * y + c
        local = pltpu.make_async_copy(rows_ref, all_ref.at[me], local_sem)
        local.start()
        copies = [pltpu.make_async_remote_copy(
            src_ref=ins[a], dst_ref=outs[a], send_sem=send_sems.at[a], recv_sem=recv_sems.at[a],
            device_id=(x, y, 1 - c), device_id_type=MESH) for a in range(n)]
        for k, (fx, fy, fc) in enumerate(flips):
            copies.append(pltpu.make_async_remote_copy(
                src_ref=rows_ref, dst_ref=all_ref.at[me], send_sem=send_sems.at[n + k], recv_sem=recv_sems.at[n + k],
                device_id=(x ^ fx, y ^ fy, c ^ fc), device_id_type=MESH))
        for cp in copies:
            cp.start()
        for cp in copies[:n]:
            cp.wait_recv()
        for k, (fx, fy, fc) in enumerate(flips):
            src = 4 * (x ^ fx) + 2 * (y ^ fy) + (c ^ fc)
            pltpu.make_async_remote_copy(
                src_ref=rows_ref, dst_ref=all_ref.at[src], send_sem=send_sems.at[n + k], recv_sem=recv_sems.at[n + k],
                device_id=(x, y, c), device_id_type=MESH).wait_recv()
        for cp in copies:
            cp.wait_send()
        local.wait()

    outs = pl.pallas_call(
        body, name="share_results",
        in_specs=[ANY] * (n + 1), out_specs=[ANY] * (n + 1),
        out_shape=[SDS(h.shape, h.dtype) for h in arrs] + [SDS((8,) + rows.shape, rows.dtype)],
        scratch_shapes=[pltpu.SemaphoreType.DMA((n + 7,)), pltpu.SemaphoreType.DMA((n + 7,)),
                        pltpu.SemaphoreType.DMA],
    )(*arrs, rows)
    return outs[:n], outs[n]


def _tiles(S, FW):
    big = FW % 512 == 0
    return dict(
        fp=512 if big else LANES,
        tn=2560 if big else LANES,
        tm_in=min(1024, S),
        t_attn=min(512, S),
        hb_fwd=8,
        hb_bwd=8,
        tm_prep=min(2048, S),
        tm_mix=min(128, S),
        tm_bwd=min(256, S),
        tc=min(512, S),
        tk=512 if big else LANES,
    )


def kernel(x, norm_g, w_in, b_forget, q_norm_g, k_norm_g, conv_w, conv_b, conv_ln_g, conv_ln_b, w_out, loss_target, m_norm_g, m_w_in, m_b_forget, m_q_norm_g, m_k_norm_g, m_conv_w, m_conv_b, m_conv_ln_g, m_conv_ln_b, m_w_out, v_norm_g, v_w_in, v_b_forget, v_q_norm_g, v_k_norm_g, v_conv_w, v_conv_b, v_conv_ln_g, v_conv_ln_b, v_w_out):
    B, S, D = x.shape
    H, dh = q_norm_g.shape[1:]
    FW = H * dh
    CW = conv_b.shape[-1]
    n_taps, cw_shard = conv_w.shape[1:]
    in_shard = w_in.shape[2]
    out_shard = w_out.shape[1]
    assert dh == HEAD_DIM and H % 2 == 0 and H <= LANES and FW == CW == D
    assert n_taps - 1 <= HALO and 4 * cw_shard == CW and 4 * out_shard == FW + CW
    assert 4 * in_shard == 4 * FW + 3 * CW + H
    T = B * S
    tl = _tiles(S, FW)
    fp = tl["fp"]
    xi, yi, ci = _place()

    w_t = jnp.transpose(w_in[0])
    conv_pad = jnp.pad(conv_w[0], ((0, HALO - n_taps), (0, 0)))
    bf16_rows = 2 * SUBLANES
    g_in, g_out, g_cw = _gather_chips(
        [w_t.astype(BF16), w_out[0].astype(BF16), conv_pad],
        [in_shard // 2 // bf16_rows * bf16_rows, out_shard // 2, HALO // 2])
    w_t_full = g_in.reshape(4 * in_shard, D)
    w_out_full = g_out.reshape(FW + CW, D)
    conv_full = g_cw.transpose(1, 0, 2).reshape(HALO, CW)
    o_f = 3 * FW
    w_pack = jnp.concatenate([w_t_full[:o_f], w_t_full[o_f + H:],
                              jnp.pad(w_t_full[o_f:o_f + H], ((0, fp - H), (0, 0)))], axis=0)
    f_col = 4 * FW + 3 * CW

    x2 = x.reshape(T, D)
    tgt = loss_target.reshape(T, D)
    b_pad = jnp.pad(b_forget, ((0, 0), (0, LANES - H)))
    gq = q_norm_g.reshape(1, FW)
    gk = k_norm_g.reshape(1, FW)

    z, h = _fwd_in(x2, norm_g, w_pack, tl["tm_in"], tl["tn"])
    c = _gate_fwd(z, b_pad, B, S, H, f_col // LANES, tl["tc"])
    qa, ka, va = _attn_prep(z, c, gq, gk, B, S, H, tl["tm_prep"])
    gain = lambda g: jnp.max(jnp.abs(g[0]), axis=-1)
    bound = (NORM_SLACK ** 2 * dh ** 0.5) * gain(q_norm_g) * gain(k_norm_g)
    oa = _attn_fwd(qa, ka, va, jnp.broadcast_to(bound[:, None, None], (H, 1, LANES)), tl["t_attn"], tl["hb_fwd"])
    y, u2, a_nat, dout, dout_b, dy, loss_acc = _fwd_out(
        oa, z, x2, tgt, conv_full, conv_b, conv_ln_g, conv_ln_b, w_out_full, B, S, H, n_taps, tl["tm_mix"])

    dzgf, dzgc, du2, doa, qb, sg_conv = _bwd_prep(dy, z, a_nat, oa, qa, u2, conv_ln_g, conv_ln_b, B, S, H, tl["tm_bwd"])
    dzglu, dconv_w = _conv_bwd(du2, z, conv_full, B, S, n_taps, tl["tm_mix"])
    dqa, dka, dva = _attn_bwd(qb, ka, va, doa, tl["t_attn"], tl["hb_bwd"])
    dzq, dzk, dzv, dc8, dg_qk = _qk_bwd(dqa, dka, dva, z, gq, gk, B, S, H, tl["tm_prep"])
    dzf, db_f = _gate_bwd(dc8, z, b_pad, B, S, H, f_col // LANES, fp, tl["tc"])
    pieces = [dzq, dzk, dzv, dzgf, dzglu, dzgc, dzf]
    dw_all = [_matmul_tn(p, h, f"dw_in_{k}", 1024, 1024, 1024) for k, p in enumerate(pieces)]
    dw_all.append(_matmul_tn(y, dout_b, "dw_out", 1024, 1024, 1024))

    summed = _pair_sum(dw_all, _pair_swap(dw_all), ci.astype(jnp.int32).reshape(1), fp)
    ends = [0]
    for t in dw_all:
        ends.append(ends[-1] + t.shape[0])
    spans, at = [], 0
    for k, rows in [(0, FW), (1, FW), (2, FW), (6, H), (3, FW), (4, 2 * CW), (5, CW)]:
        spans.append((at, rows, ends[k]))
        at += rows

    def chip_rows(j):
        lo, hi = j * in_shard, (j + 1) * in_shard
        return jnp.concatenate([summed[src + max(lo, a) - a:src + min(hi, a + n) - a]
                                for a, n, src in spans if max(lo, a) < min(hi, a + n)], axis=0)

    part_in = jnp.stack([chip_rows(j) for j in range(4)])
    part_out = summed[ends[7]:ends[8]].reshape(4, out_shard, D // 2)
    grad_x2, dg_norm, slots_in, slots_out = _dh_rms_bwd(
        pieces, w_pack, x2, norm_g, dout, tl["tm_in"], tl["tk"], [part_in, part_out])
    chip = 2 * xi + yi
    half_in = _sum_slots(slots_in, "chip_sum_in", lax.dynamic_index_in_dim(part_in, chip, 0, keepdims=False))
    half_out = _sum_slots(slots_out, "chip_sum_out", lax.dynamic_index_in_dim(part_out, chip, 0, keepdims=False))
    lanes_to_d = lambda t: jnp.pad(t, ((0, 0), (0, D - LANES)))
    small = jnp.concatenate([
        dg_norm[0:1], lanes_to_d(db_f[0:1, :]), dg_qk[0:1], dg_qk[1:2],
        sg_conv[2:3], sg_conv[0:1], sg_conv[1:2], dconv_w, lanes_to_d(loss_acc[0:1, :])], axis=0)
    n_small = small.shape[0]
    (other_in, other_out), all_small = _share_results([half_in, half_out], small)

    small_sum = _sum_slots(all_small, "small_sum", tr=n_small)
    loss = 0.5 * small_sum[n_small - 1, 0] / D
    grad_norm_g, grad_b_f = small_sum[0:1], small_sum[1:2, :H]
    grad_gq, grad_gk = small_sum[2:3].reshape(1, H, dh), small_sum[3:4].reshape(1, H, dh)
    grad_conv_b, grad_ln_g, grad_ln_b = small_sum[4:5], small_sum[5:6], small_sum[6:7]
    grad_conv_w = lax.dynamic_slice_in_dim(small_sum[7:7 + n_taps], chip * cw_shard, cw_shard, axis=1)

    in_t = _adamw_halves(w_t, half_in, other_in, jnp.transpose(m_w_in[0]), jnp.transpose(v_w_in[0]), "adamw_in")
    grad_w_in, d_in, nm_in, nv_in = (jnp.transpose(t)[None] for t in in_t)
    grad_w_out, d_out, nm_out, nv_out = (
        t[None] for t in _adamw_halves(w_out[0], half_out, other_out, m_w_out[0], v_w_out[0], "adamw_out"))
    d_cw, nm_cw, nv_cw = (t[None] for t in _adamw(conv_w[0], grad_conv_w, m_conv_w[0], v_conv_w[0], "adamw_conv_w"))

    def rows(ws):
        return jnp.concatenate([jnp.pad(t.reshape(1, -1), ((0, 0), (0, D - t.size))) for t in ws], axis=0)

    small_w = [norm_g, b_forget, q_norm_g, k_norm_g, conv_b, conv_ln_g, conv_ln_b]
    small_m = [m_norm_g, m_b_forget, m_q_norm_g, m_k_norm_g, m_conv_b, m_conv_ln_g, m_conv_ln_b]
    small_v = [v_norm_g, v_b_forget, v_q_norm_g, v_k_norm_g, v_conv_b, v_conv_ln_g, v_conv_ln_b]
    d_s, nm_s, nv_s = _adamw(rows(small_w), small_sum[0:7], rows(small_m), rows(small_v), "adamw_small")

    def unpack(t):
        return [t[k:k + 1, :w.size].reshape(w.shape) for k, w in enumerate(small_w)]

    def order(s, in_, cw, out_):
        ng, bf, qg, kg, cb, lg, lb = s
        return [ng, in_, bf, qg, kg, cw, cb, lg, lb, out_]

    grads = [grad_norm_g, grad_w_in, grad_b_f, grad_gq, grad_gk, grad_conv_w[None],
             grad_conv_b, grad_ln_g, grad_ln_b, grad_w_out]
    return (loss, grad_x2.reshape(B, S, D), *grads,
            *order(unpack(d_s), d_in, d_cw, d_out),
            *order(unpack(nm_s), nm_in, nm_cw, nm_out),
            *order(unpack(nv_s), nv_in, nv_cw, nv_out))
```
